```python
import math
import jax, jax.numpy as jnp
from jax import lax
import numpy as np

D_MODEL = 1024
BATCH = 8
SEQ = 8192
DEPTH = 1

D_LRU = D_MODEL
LRU_BLOCKS = 16
LRU_BW = D_LRU // LRU_BLOCKS
CONV_W = 4
LRU_C = 8.0
N_HEADS = 8
HEAD_DIM = 128
D_ATTN = N_HEADS * HEAD_DIM
BLOCK_Q = 128
D_FF = 4 * D_MODEL
N_BRANCH = 2
RMS_EPS = 1e-6
IN_SPLITS = (D_LRU, D_LRU, D_ATTN, D_ATTN, D_ATTN, N_BRANCH * D_MODEL, N_HEADS)
D_IN = sum(IN_SPLITS)

kernel_name = "hybrid_rglru_fox_gated_block"


def rms_norm(x, g):
    xf = x.astype(jnp.float32)
    y = xf * lax.rsqrt(jnp.mean(xf * xf, axis=-1, keepdims=True) + RMS_EPS)
    return (y * g.astype(jnp.float32)).astype(x.dtype)


def causal_depthwise_conv(x, w, b):
    S = x.shape[1]
    xp = jnp.pad(x, ((0, 0), (CONV_W - 1, 0), (0, 0)))
    out = b
    for k in range(CONV_W):
        out = out + xp[:, k:k + S, :] * w[k]
    return out


def block_diag_linear(x, w, b):
    B, S, _ = x.shape
    xb = x.reshape(B, S, LRU_BLOCKS, LRU_BW)
    y = jnp.einsum('bsnc,ncd->bsnd', xb, w).reshape(B, S, D_LRU)
    return y + b


def rg_lru(x, wa, ba, wx, bx, lam):
    r = jax.nn.sigmoid(block_diag_linear(x, wa, ba).astype(jnp.float32))
    i = jax.nn.sigmoid(block_diag_linear(x, wx, bx).astype(jnp.float32))
    log_a = -LRU_C * r * jax.nn.softplus(-lam.astype(jnp.float32))
    a = jnp.exp(log_a)
    mult = jnp.sqrt(-jnp.expm1(2.0 * log_a))
    bterm = mult * (i * x.astype(jnp.float32))

    def combine(left, right):
        a1, b1 = left
        a2, b2 = right
        return a1 * a2, a2 * b1 + b2

    _, h = lax.associative_scan(combine, (a, bterm), axis=1)
    return h.astype(x.dtype)


def fox_attention(q, k, v, log_f):
    S = q.shape[1]
    scale = 1.0 / math.sqrt(HEAD_DIM)
    F = jnp.cumsum(log_f.astype(jnp.float32), axis=1)
    F = jnp.transpose(F, (0, 2, 1))
    outs = []
    for blk in range(S // BLOCK_Q):
        q0, q1 = blk * BLOCK_Q, (blk + 1) * BLOCK_Q
        qb = q[:, q0:q1]
        kb = k[:, :q1]
        vb = v[:, :q1]
        s = jnp.einsum('bqhd,bkhd->bhqk', qb, kb).astype(jnp.float32) * scale
        s = s + F[:, :, q0:q1, None] - F[:, :, None, :q1]
        q_pos = jnp.arange(q0, q1)
        k_pos = jnp.arange(q1)
        mask = q_pos[:, None] >= k_pos[None, :]
        s = jnp.where(mask[None, None], s, -jnp.inf)
        p = jax.nn.softmax(s, axis=-1).astype(v.dtype)
        outs.append(jnp.einsum('bhqk,bkhd->bqhd', p, vb))
    return jnp.concatenate(outs, axis=1)


def _fwd_setup_inputs(seed: int = 0) -> dict:
    key = jax.random.key(seed)
    ks = jax.random.split(key, 20)
    f32 = jnp.float32
    n = lambda k, shape, s: jax.random.normal(k, shape, f32) * s
    x = jax.random.normal(ks[0], (BATCH, SEQ, D_MODEL), f32)
    norm_mix_g = 1.0 + n(ks[1], (D_MODEL,), 0.02)
    w_in = n(ks[2], (D_MODEL, D_IN), D_MODEL ** -0.5)
    conv_w = n(ks[3], (CONV_W, D_LRU), CONV_W ** -0.5)
    conv_b = n(ks[4], (D_LRU,), 0.02)
    lru_wa = n(ks[5], (LRU_BLOCKS, LRU_BW, LRU_BW), LRU_BW ** -0.5)
    lru_ba = n(ks[6], (D_LRU,), 0.02)
    lru_wx = n(ks[7], (LRU_BLOCKS, LRU_BW, LRU_BW), LRU_BW ** -0.5)
    lru_bx = n(ks[8], (D_LRU,), 0.02)
    a0 = jax.random.uniform(ks[9], (D_LRU,), f32, 0.9, 0.999)
    s0 = a0 ** (1.0 / LRU_C)
    lru_lambda = jnp.log(s0) - jnp.log1p(-s0)
    forget_b = 2.0 + n(ks[10], (N_HEADS,), 0.5)
    w_branch_a = n(ks[11], (D_LRU, D_MODEL), D_LRU ** -0.5)
    w_branch_b = n(ks[12], (D_ATTN, D_MODEL), D_ATTN ** -0.5)
    w_out = n(ks[13], (D_MODEL, D_MODEL), D_MODEL ** -0.5)
    norm_mlp_g = 1.0 + n(ks[14], (D_MODEL,), 0.02)
    w_up = n(ks[15], (D_MODEL, D_FF), D_MODEL ** -0.5)
    w_down = n(ks[16], (D_FF, D_MODEL), D_FF ** -0.5)
    norm_final_g = 1.0 + n(ks[17], (D_MODEL,), 0.02)
    return {"x": x, "norm_mix_g": norm_mix_g, "w_in": w_in, "conv_w": conv_w,
            "conv_b": conv_b, "lru_wa": lru_wa, "lru_ba": lru_ba, "lru_wx": lru_wx,
            "lru_bx": lru_bx, "lru_lambda": lru_lambda, "forget_b": forget_b,
            "w_branch_a": w_branch_a, "w_branch_b": w_branch_b, "w_out": w_out,
            "norm_mlp_g": norm_mlp_g, "w_up": w_up, "w_down": w_down,
            "norm_final_g": norm_final_g}


def _fwd_reference(x, norm_mix_g, w_in, conv_w, conv_b, lru_wa, lru_ba, lru_wx, lru_bx,
              lru_lambda, forget_b, w_branch_a, w_branch_b, w_out, norm_mlp_g,
              w_up, w_down, norm_final_g):
    B, S, _ = x.shape
    for _layer in range(DEPTH):
        u = rms_norm(x, norm_mix_g)
        proj = u @ w_in
        cuts = list(np.cumsum(IN_SPLITS)[:-1])
        x_lru, g_lru, q, k, v, gates, f_logit = jnp.split(proj, cuts, axis=-1)

        xa = causal_depthwise_conv(x_lru, conv_w, conv_b)
        ha = rg_lru(xa, lru_wa, lru_ba, lru_wx, lru_bx, lru_lambda)
        ya = (jax.nn.gelu(g_lru) * ha) @ w_branch_a

        log_f = jax.nn.log_sigmoid((f_logit + forget_b).astype(jnp.float32))
        qh = q.reshape(B, S, N_HEADS, HEAD_DIM)
        kh = k.reshape(B, S, N_HEADS, HEAD_DIM)
        vh = v.reshape(B, S, N_HEADS, HEAD_DIM)
        ob = fox_attention(qh, kh, vh, log_f).reshape(B, S, D_ATTN)
        yb = ob @ w_branch_b

        g_a, g_b = jnp.split(jax.nn.sigmoid(gates), N_BRANCH, axis=-1)
        x = x + (g_a * ya + g_b * yb) @ w_out

        m = rms_norm(x, norm_mlp_g)
        h = jnp.square(jax.nn.relu(m @ w_up))
        x = x + h @ w_down
    return rms_norm(x, norm_final_g)


import jax as _jax
import jax.numpy as _jnp

TWIN_FORMAT = 'train_step'
FWD_PARAMS = ['x', 'norm_mix_g', 'w_in', 'conv_w', 'conv_b', 'lru_wa', 'lru_ba', 'lru_wx', 'lru_bx', 'lru_lambda', 'forget_b', 'w_branch_a', 'w_branch_b', 'w_out', 'norm_mlp_g', 'w_up', 'w_down', 'norm_final_g']
TWIN_WEIGHTS = ['norm_mix_g', 'w_in', 'conv_w', 'conv_b', 'lru_wa', 'lru_ba', 'lru_wx', 'lru_bx', 'lru_lambda', 'forget_b', 'w_branch_a', 'w_branch_b', 'w_out', 'norm_mlp_g', 'w_up', 'w_down', 'norm_final_g']
TWIN_DIFF_INPUT = 'x'
TWIN_INPUTS = ['x', 'norm_mix_g', 'w_in', 'conv_w', 'conv_b', 'lru_wa', 'lru_ba', 'lru_wx', 'lru_bx', 'lru_lambda', 'forget_b', 'w_branch_a', 'w_branch_b', 'w_out', 'norm_mlp_g', 'w_up', 'w_down', 'norm_final_g', 'loss_target', 'm_norm_mix_g', 'm_w_in', 'm_conv_w', 'm_conv_b', 'm_lru_wa', 'm_lru_ba', 'm_lru_wx', 'm_lru_bx', 'm_lru_lambda', 'm_forget_b', 'm_w_branch_a', 'm_w_branch_b', 'm_w_out', 'm_norm_mlp_g', 'm_w_up', 'm_w_down', 'm_norm_final_g', 'v_norm_mix_g', 'v_w_in', 'v_conv_w', 'v_conv_b', 'v_lru_wa', 'v_lru_ba', 'v_lru_wx', 'v_lru_bx', 'v_lru_lambda', 'v_forget_b', 'v_w_branch_a', 'v_w_branch_b', 'v_w_out', 'v_norm_mlp_g', 'v_w_up', 'v_w_down', 'v_norm_final_g']
TWIN_OUTPUTS = ['loss', 'grad_x', 'grad_norm_mix_g', 'grad_w_in', 'grad_conv_w', 'grad_conv_b', 'grad_lru_wa', 'grad_lru_ba', 'grad_lru_wx', 'grad_lru_bx', 'grad_lru_lambda', 'grad_forget_b', 'grad_w_branch_a', 'grad_w_branch_b', 'grad_w_out', 'grad_norm_mlp_g', 'grad_w_up', 'grad_w_down', 'grad_norm_final_g', 'delta_norm_mix_g', 'delta_w_in', 'delta_conv_w', 'delta_conv_b', 'delta_lru_wa', 'delta_lru_ba', 'delta_lru_wx', 'delta_lru_bx', 'delta_lru_lambda', 'delta_forget_b', 'delta_w_branch_a', 'delta_w_branch_b', 'delta_w_out', 'delta_norm_mlp_g', 'delta_w_up', 'delta_w_down', 'delta_norm_final_g', 'new_m_norm_mix_g', 'new_m_w_in', 'new_m_conv_w', 'new_m_conv_b', 'new_m_lru_wa', 'new_m_lru_ba', 'new_m_lru_wx', 'new_m_lru_bx', 'new_m_lru_lambda', 'new_m_forget_b', 'new_m_w_branch_a', 'new_m_w_branch_b', 'new_m_w_out', 'new_m_norm_mlp_g', 'new_m_w_up', 'new_m_w_down', 'new_m_norm_final_g', 'new_v_norm_mix_g', 'new_v_w_in', 'new_v_conv_w', 'new_v_conv_b', 'new_v_lru_wa', 'new_v_lru_ba', 'new_v_lru_wx', 'new_v_lru_bx', 'new_v_lru_lambda', 'new_v_forget_b', 'new_v_w_branch_a', 'new_v_w_branch_b', 'new_v_w_out', 'new_v_norm_mlp_g', 'new_v_w_up', 'new_v_w_down', 'new_v_norm_final_g']
TWIN_LEAF_KINDS = {'loss': 'loss', 'grad_x': 'grad_x', 'grad_norm_mix_g': 'grad_w', 'grad_w_in': 'grad_w', 'grad_conv_w': 'grad_w', 'grad_conv_b': 'grad_w', 'grad_lru_wa': 'grad_w', 'grad_lru_ba': 'grad_w', 'grad_lru_wx': 'grad_w', 'grad_lru_bx': 'grad_w', 'grad_lru_lambda': 'grad_w', 'grad_forget_b': 'grad_w', 'grad_w_branch_a': 'grad_w', 'grad_w_branch_b': 'grad_w', 'grad_w_out': 'grad_w', 'grad_norm_mlp_g': 'grad_w', 'grad_w_up': 'grad_w', 'grad_w_down': 'grad_w', 'grad_norm_final_g': 'grad_w', 'delta_norm_mix_g': 'delta_w', 'delta_w_in': 'delta_w', 'delta_conv_w': 'delta_w', 'delta_conv_b': 'delta_w', 'delta_lru_wa': 'delta_w', 'delta_lru_ba': 'delta_w', 'delta_lru_wx': 'delta_w', 'delta_lru_bx': 'delta_w', 'delta_lru_lambda': 'delta_w', 'delta_forget_b': 'delta_w', 'delta_w_branch_a': 'delta_w', 'delta_w_branch_b': 'delta_w', 'delta_w_out': 'delta_w', 'delta_norm_mlp_g': 'delta_w', 'delta_w_up': 'delta_w', 'delta_w_down': 'delta_w', 'delta_norm_final_g': 'delta_w', 'new_m_norm_mix_g': 'new_m', 'new_m_w_in': 'new_m', 'new_m_conv_w': 'new_m', 'new_m_conv_b': 'new_m', 'new_m_lru_wa': 'new_m', 'new_m_lru_ba': 'new_m', 'new_m_lru_wx': 'new_m', 'new_m_lru_bx': 'new_m', 'new_m_lru_lambda': 'new_m', 'new_m_forget_b': 'new_m', 'new_m_w_branch_a': 'new_m', 'new_m_w_branch_b': 'new_m', 'new_m_w_out': 'new_m', 'new_m_norm_mlp_g': 'new_m', 'new_m_w_up': 'new_m', 'new_m_w_down': 'new_m', 'new_m_norm_final_g': 'new_m', 'new_v_norm_mix_g': 'new_v', 'new_v_w_in': 'new_v', 'new_v_conv_w': 'new_v', 'new_v_conv_b': 'new_v', 'new_v_lru_wa': 'new_v', 'new_v_lru_ba': 'new_v', 'new_v_lru_wx': 'new_v', 'new_v_lru_bx': 'new_v', 'new_v_lru_lambda': 'new_v', 'new_v_forget_b': 'new_v', 'new_v_w_branch_a': 'new_v', 'new_v_w_branch_b': 'new_v', 'new_v_w_out': 'new_v', 'new_v_norm_mlp_g': 'new_v', 'new_v_w_up': 'new_v', 'new_v_w_down': 'new_v', 'new_v_norm_final_g': 'new_v'}


def _forward(args):
    return _fwd_reference(*[args[k] for k in FWD_PARAMS])


def _output_shape():
    def fwd():
        inp = _fwd_setup_inputs(0)
        return _fwd_reference(*[inp[k] for k in FWD_PARAMS])
    out = _jax.eval_shape(fwd)
    return out.shape, out.dtype

N_MICROBATCH = 1
ADAM_LR = 0.001
ADAM_B1 = 0.9
ADAM_B2 = 0.999
ADAM_EPS = 1e-08
ADAM_WD = 0.01
ADAM_STEP = 10
PER_EXAMPLE_BATCH_AXIS = {'x': 0, 'loss_target': 0}
SHARED_INPUTS = []
_WEIGHT_DTYPES = {'norm_mix_g': _jnp.float32, 'w_in': _jnp.float32, 'conv_w': _jnp.float32, 'conv_b': _jnp.float32, 'lru_wa': _jnp.float32, 'lru_ba': _jnp.float32, 'lru_wx': _jnp.float32, 'lru_bx': _jnp.float32, 'lru_lambda': _jnp.float32, 'forget_b': _jnp.float32, 'w_branch_a': _jnp.float32, 'w_branch_b': _jnp.float32, 'w_out': _jnp.float32, 'norm_mlp_g': _jnp.float32, 'w_up': _jnp.float32, 'w_down': _jnp.float32, 'norm_final_g': _jnp.float32}
MOMENT_SCALE = {'norm_mix_g': 1.196066e-01, 'w_in': 4.798710e-02, 'conv_w': 6.935191e-02, 'conv_b': 8.101299e-01, 'lru_wa': 2.668354e-02, 'lru_ba': 2.205230e-02, 'lru_wx': 4.846606e-02, 'lru_bx': 2.609215e-02, 'lru_lambda': 3.938881e-02, 'forget_b': 3.109954e-01, 'w_branch_a': 7.357557e-02, 'w_branch_b': 5.891342e-02, 'w_out': 8.903578e-02, 'norm_mlp_g': 2.203246e-01, 'w_up': 1.103598e-01, 'w_down': 2.358218e-01, 'norm_final_g': 6.451904e+01}


def _to_microbatches(a, axis):
    t = _jnp.moveaxis(a, axis, 0)
    t = t.reshape((N_MICROBATCH, t.shape[0] // N_MICROBATCH) + t.shape[1:])
    return _jnp.moveaxis(t, 1, axis + 1)


def setup_inputs(seed: int = 0) -> dict:
    inp = _fwd_setup_inputs(seed)
    key = _jax.random.fold_in(_jax.random.key(seed), 7919)
    shape, _ = _output_shape()
    out = dict(inp)
    out["loss_target"] = _jax.random.normal(_jax.random.fold_in(key, 0), shape, _jnp.float32)
    for i, name in enumerate(TWIN_WEIGHTS):
        w = inp[name].astype(_jnp.float32)
        if MOMENT_SCALE is None:
            s = _jnp.sqrt(_jnp.mean(_jnp.square(w)) + 1e-30)
        else:
            s = MOMENT_SCALE[name]
        km, kv = _jax.random.split(_jax.random.fold_in(key, i + 1))
        out[name] = w
        out["m_" + name] = s * _jax.random.normal(km, w.shape, _jnp.float32)
        out["v_" + name] = (s * s) * _jax.random.uniform(kv, w.shape, _jnp.float32, 0.5, 1.5)
    if N_MICROBATCH > 1:
        for name, axis in PER_EXAMPLE_BATCH_AXIS.items():
            out[name] = _to_microbatches(out[name], axis)
    return {'x': out['x'], 'norm_mix_g': out['norm_mix_g'], 'w_in': out['w_in'], 'conv_w': out['conv_w'], 'conv_b': out['conv_b'], 'lru_wa': out['lru_wa'], 'lru_ba': out['lru_ba'], 'lru_wx': out['lru_wx'], 'lru_bx': out['lru_bx'], 'lru_lambda': out['lru_lambda'], 'forget_b': out['forget_b'], 'w_branch_a': out['w_branch_a'], 'w_branch_b': out['w_branch_b'], 'w_out': out['w_out'], 'norm_mlp_g': out['norm_mlp_g'], 'w_up': out['w_up'], 'w_down': out['w_down'], 'norm_final_g': out['norm_final_g'], 'loss_target': out['loss_target'], 'm_norm_mix_g': out['m_norm_mix_g'], 'm_w_in': out['m_w_in'], 'm_conv_w': out['m_conv_w'], 'm_conv_b': out['m_conv_b'], 'm_lru_wa': out['m_lru_wa'], 'm_lru_ba': out['m_lru_ba'], 'm_lru_wx': out['m_lru_wx'], 'm_lru_bx': out['m_lru_bx'], 'm_lru_lambda': out['m_lru_lambda'], 'm_forget_b': out['m_forget_b'], 'm_w_branch_a': out['m_w_branch_a'], 'm_w_branch_b': out['m_w_branch_b'], 'm_w_out': out['m_w_out'], 'm_norm_mlp_g': out['m_norm_mlp_g'], 'm_w_up': out['m_w_up'], 'm_w_down': out['m_w_down'], 'm_norm_final_g': out['m_norm_final_g'], 'v_norm_mix_g': out['v_norm_mix_g'], 'v_w_in': out['v_w_in'], 'v_conv_w': out['v_conv_w'], 'v_conv_b': out['v_conv_b'], 'v_lru_wa': out['v_lru_wa'], 'v_lru_ba': out['v_lru_ba'], 'v_lru_wx': out['v_lru_wx'], 'v_lru_bx': out['v_lru_bx'], 'v_lru_lambda': out['v_lru_lambda'], 'v_forget_b': out['v_forget_b'], 'v_w_branch_a': out['v_w_branch_a'], 'v_w_branch_b': out['v_w_branch_b'], 'v_w_out': out['v_w_out'], 'v_norm_mlp_g': out['v_norm_mlp_g'], 'v_w_up': out['v_w_up'], 'v_w_down': out['v_w_down'], 'v_norm_final_g': out['v_norm_final_g']}


def _loss(weights, diff, rest, loss_target):
    with _jax.named_scope("forward"):
        args = {**rest, TWIN_DIFF_INPUT: diff, **{k: w.astype(_WEIGHT_DTYPES[k]) for k, w in weights.items()}}
        y = _forward(args)
    with _jax.named_scope("loss_head"):
        err = _jnp.square(y.astype(_jnp.float32) - loss_target)
        return 0.5 * _jnp.sum(_jnp.mean(err, axis=-1)) if err.ndim else 0.5 * err


def _adamw(w, g, m, v):
    m = ADAM_B1 * m + (1.0 - ADAM_B1) * g
    v = ADAM_B2 * v + (1.0 - ADAM_B2) * _jnp.square(g)
    m_hat = m / (1.0 - ADAM_B1 ** ADAM_STEP)
    v_hat = v / (1.0 - ADAM_B2 ** ADAM_STEP)
    delta = -ADAM_LR * (m_hat / (_jnp.sqrt(v_hat) + ADAM_EPS) + ADAM_WD * w)
    return delta, m, v


def reference(x, norm_mix_g, w_in, conv_w, conv_b, lru_wa, lru_ba, lru_wx, lru_bx, lru_lambda, forget_b, w_branch_a, w_branch_b, w_out, norm_mlp_g, w_up, w_down, norm_final_g, loss_target, m_norm_mix_g, m_w_in, m_conv_w, m_conv_b, m_lru_wa, m_lru_ba, m_lru_wx, m_lru_bx, m_lru_lambda, m_forget_b, m_w_branch_a, m_w_branch_b, m_w_out, m_norm_mlp_g, m_w_up, m_w_down, m_norm_final_g, v_norm_mix_g, v_w_in, v_conv_w, v_conv_b, v_lru_wa, v_lru_ba, v_lru_wx, v_lru_bx, v_lru_lambda, v_forget_b, v_w_branch_a, v_w_branch_b, v_w_out, v_norm_mlp_g, v_w_up, v_w_down, v_norm_final_g):
    given = dict(x=x, norm_mix_g=norm_mix_g, w_in=w_in, conv_w=conv_w, conv_b=conv_b, lru_wa=lru_wa, lru_ba=lru_ba, lru_wx=lru_wx, lru_bx=lru_bx, lru_lambda=lru_lambda, forget_b=forget_b, w_branch_a=w_branch_a, w_branch_b=w_branch_b, w_out=w_out, norm_mlp_g=norm_mlp_g, w_up=w_up, w_down=w_down, norm_final_g=norm_final_g, loss_target=loss_target, m_norm_mix_g=m_norm_mix_g, m_w_in=m_w_in, m_conv_w=m_conv_w, m_conv_b=m_conv_b, m_lru_wa=m_lru_wa, m_lru_ba=m_lru_ba, m_lru_wx=m_lru_wx, m_lru_bx=m_lru_bx, m_lru_lambda=m_lru_lambda, m_forget_b=m_forget_b, m_w_branch_a=m_w_branch_a, m_w_branch_b=m_w_branch_b, m_w_out=m_w_out, m_norm_mlp_g=m_norm_mlp_g, m_w_up=m_w_up, m_w_down=m_w_down, m_norm_final_g=m_norm_final_g, v_norm_mix_g=v_norm_mix_g, v_w_in=v_w_in, v_conv_w=v_conv_w, v_conv_b=v_conv_b, v_lru_wa=v_lru_wa, v_lru_ba=v_lru_ba, v_lru_wx=v_lru_wx, v_lru_bx=v_lru_bx, v_lru_lambda=v_lru_lambda, v_forget_b=v_forget_b, v_w_branch_a=v_w_branch_a, v_w_branch_b=v_w_branch_b, v_w_out=v_w_out, v_norm_mlp_g=v_norm_mlp_g, v_w_up=v_w_up, v_w_down=v_w_down, v_norm_final_g=v_norm_final_g)
    weights = {n: given[n] for n in TWIN_WEIGHTS}
    shared = {n: given[n] for n in SHARED_INPUTS}
    per_example = {n: given[n] for n in ['x']}
    grad_fn = _jax.value_and_grad(_loss, argnums=(0, 1))

    def one_microbatch(ex, loss_target):
        ex = dict(ex)
        diff = ex.pop(TWIN_DIFF_INPUT)
        return grad_fn(weights, diff, {**shared, **ex}, loss_target)

    if N_MICROBATCH == 1:
        loss, (grad_w, grad_x) = one_microbatch(per_example, given["loss_target"])
    else:
        def body(carry, xs):
            loss_sum, grad_sum = carry
            l_k, (gw_k, gx_k) = one_microbatch(xs[0], xs[1])
            with _jax.named_scope("update"):
                return (loss_sum + l_k, _jax.tree.map(_jnp.add, grad_sum, gw_k)), gx_k

        init = (_jnp.zeros((), _jnp.float32), _jax.tree.map(_jnp.zeros_like, weights))
        (loss, grad_w), grad_x = _jax.lax.scan(body, init, (per_example, given["loss_target"]))
    with _jax.named_scope("update"):
        delta_w, new_m, new_v = {}, {}, {}
        for n in TWIN_WEIGHTS:
            delta_w[n], new_m[n], new_v[n] = _adamw(weights[n], grad_w[n], given["m_" + n], given["v_" + n])
    return (loss, grad_x, *[grad_w[n] for n in TWIN_WEIGHTS], *[delta_w[n] for n in TWIN_WEIGHTS],
            *[new_m[n] for n in TWIN_WEIGHTS], *[new_v[n] for n in TWIN_WEIGHTS])
```

```python
import functools
import math

import jax
import jax.numpy as jnp
import numpy as np
from jax import lax
from jax.experimental import pallas as pl
from jax.experimental.pallas import tpu as pltpu

F32 = jnp.float32
BF = jnp.bfloat16

D = 1024
NH = 8
DH = 128
FF = 4096
CONV = 4
LRU_BLOCKS = 16
LRU_BW = 64
BD = 256
NBD = D // BD
LRU_C = 8.0
EPS = 1e-6
DIN = 7176
DINP = 7296
NCHIP = 4
SLAB = 8
VMEM_CAP = 60 * 1024 * 1024

ADAM_LR = 0.001
ADAM_B1 = 0.9
ADAM_B2 = 0.999
ADAM_EPS = 1e-08
ADAM_WD = 0.01
ADAM_STEP = 10

MESH = pl.DeviceIdType.MESH


def _vmem_limit(nbytes):
    return int(min(VMEM_CAP, max(32 * 1024 * 1024, 3 * nbytes)))


def _nbytes(shape, dtype):
    return int(np.prod(shape)) * jnp.dtype(dtype).itemsize


def _sig(x):
    return 1.0 / (1.0 + jnp.exp(-x))


def _log1p(u):
    w = 1.0 + u
    return jnp.where(w == 1.0, u, jnp.log(w) * (u / (w - 1.0)))


def _expm1(z):
    e = jnp.exp(z)
    le = jnp.log(e)
    return jnp.where(e == 1.0, z, jnp.where(le == 0.0, z, (e - 1.0) * (z / le)))


def _softplus(z):
    return jnp.maximum(z, 0.0) + _log1p(jnp.exp(-jnp.abs(z)))


_GELU_C = math.sqrt(2.0 / math.pi)


def _gelu(x):
    return 0.5 * x * (1.0 + jnp.tanh(_GELU_C * (x + 0.044715 * x * x * x)))


def _gelu_grad(x):
    t = jnp.tanh(_GELU_C * (x + 0.044715 * x * x * x))
    return 0.5 * (1.0 + t) + 0.5 * x * (1.0 - t * t) * _GELU_C * (1.0 + 3.0 * 0.044715 * x * x)


def _shift_down(x, d, prev8):
    n = x.shape[0]
    row8 = lax.broadcasted_iota(jnp.int32, (SLAB, x.shape[1]), 0)
    y = pltpu.roll(x, d, 0)
    top = jnp.where(row8 < d, pltpu.roll(prev8, d, 0), y[0:SLAB])
    if n == SLAB:
        return top
    return jnp.concatenate([top, y[SLAB:]], axis=0)


def _shift_up(x, d, next8):
    n = x.shape[0]
    row8 = lax.broadcasted_iota(jnp.int32, (SLAB, x.shape[1]), 0)
    y = pltpu.roll(x, n - d, 0)
    bottom = jnp.where(row8 >= SLAB - d, pltpu.roll(next8, SLAB - d, 0), y[n - SLAB:])
    if n == SLAB:
        return bottom
    return jnp.concatenate([y[:n - SLAB], bottom], axis=0)


def _slab_scan_fwd(a, b):
    row = lax.broadcasted_iota(jnp.int32, a.shape, 0)
    for k in (1, 2, 4):
        a_s = pltpu.roll(a, k, 0)
        b_s = pltpu.roll(b, k, 0)
        m = row >= k
        b = jnp.where(m, a * b_s + b, b)
        a = jnp.where(m, a * a_s, a)
    return a, b


def _slab_scan_bwd(a, b):
    row = lax.broadcasted_iota(jnp.int32, a.shape, 0)
    for k in (1, 2, 4):
        a_s = pltpu.roll(a, SLAB - k, 0)
        b_s = pltpu.roll(b, SLAB - k, 0)
        m = row < SLAB - k
        b = jnp.where(m, a * b_s + b, b)
        a = jnp.where(m, a * a_s, a)
    return a, b


_DIMS = {"nn": (((1,), (0,)), ((), ())), "nt": (((1,), (1,)), ((), ())), "tn": (((0,), (0,)), ((), ()))}


def _dot(a, b, mode="nn"):
    return lax.dot_general(a, b, _DIMS[mode], preferred_element_type=F32)


def _mm(a, b, mode, M, N, K, *, name, out_dtype=F32, tm=1024, tn=1024, tk=1024,
        a_off=(0, 0), b_off=(0, 0), add=None, epi=None, epi_ins=()):
    tm, tn, tk = min(tm, M), min(tn, N), min(tk, K)
    nk = K // tk
    grid = (M // tm, N // tn, nk)
    if mode == "nn":
        a_spec = pl.BlockSpec((tm, tk), lambda i, j, k: (i + a_off[0], k + a_off[1]))
        b_spec = pl.BlockSpec((tk, tn), lambda i, j, k: (k + b_off[0], j + b_off[1]))
    elif mode == "nt":
        a_spec = pl.BlockSpec((tm, tk), lambda i, j, k: (i + a_off[0], k + a_off[1]))
        b_spec = pl.BlockSpec((tn, tk), lambda i, j, k: (j + b_off[0], k + b_off[1]))
    else:
        a_spec = pl.BlockSpec((tk, tm), lambda i, j, k: (k + a_off[0], i + a_off[1]))
        b_spec = pl.BlockSpec((tk, tn), lambda i, j, k: (k + b_off[0], j + b_off[1]))
    o_spec = pl.BlockSpec((tm, tn), lambda i, j, k: (i, j))
    extra = ([add] if add is not None else []) + list(epi_ins)
    n_extra = len(extra)
    has_add = add is not None

    def body(*refs):
        a_ref, b_ref = refs[0], refs[1]
        ex = refs[2:2 + n_extra]
        o_ref = refs[2 + n_extra]

        def finish(acc):
            if has_add:
                acc = acc + ex[0][...].astype(F32)
            if epi is not None:
                acc = epi(acc, *[e[...] for e in ex[(1 if has_add else 0):]])
            o_ref[...] = acc.astype(o_ref.dtype)

        p = _dot(a_ref[...].astype(BF), b_ref[...].astype(BF), mode)
        if nk == 1:
            finish(p)
        else:
            acc_ref = refs[3 + n_extra]
            k = pl.program_id(2)

            @pl.when(k == 0)
            def _():
                acc_ref[...] = p

            @pl.when(k > 0)
            def _():
                acc_ref[...] += p

            @pl.when(k == nk - 1)
            def _():
                finish(acc_ref[...])

    blk = (_nbytes((tm, tk), a.dtype) + _nbytes((tk, tn), b.dtype) + _nbytes((tm, tn), out_dtype)
           + sum(_nbytes((tm, tn), e.dtype) for e in extra) + 2 * _nbytes((tm, tn), F32))
    return pl.pallas_call(
        body, name=name, grid=grid,
        in_specs=[a_spec, b_spec] + [o_spec] * n_extra,
        out_specs=o_spec,
        out_shape=jax.ShapeDtypeStruct((M, N), out_dtype),
        scratch_shapes=[pltpu.VMEM((tm, tn), F32)] if nk > 1 else [],
        compiler_params=pltpu.CompilerParams(
            dimension_semantics=("parallel", "parallel", "arbitrary"), vmem_limit_bytes=_vmem_limit(blk)),
    )(a, b, *extra)


def _ew(fn, T, tm, ins, consts, outs, accs, *, name, reverse=False):
    tm = min(tm, T)
    nt = T // tm
    n_in, n_c, n_o, n_a = len(ins), len(consts), len(outs), len(accs)

    def row(i):
        return nt - 1 - i if reverse else i

    in_specs = [pl.BlockSpec((tm, w), functools.partial(lambda i, cb: (row(i), cb), cb=cb)) for (_, w, cb) in ins]
    in_specs += [pl.BlockSpec(c.shape, functools.partial(lambda i, nd: (0,) * nd, nd=c.ndim)) for c in consts]
    out_specs = [pl.BlockSpec((tm, w), lambda i: (row(i), 0)) for (w, _) in outs]
    out_specs += [pl.BlockSpec((r, w), lambda i: (0, 0)) for (r, w) in accs]
    out_shape = [jax.ShapeDtypeStruct((T, w), dt) for (w, dt) in outs]
    out_shape += [jax.ShapeDtypeStruct((r, w), F32) for (r, w) in accs]

    def body(*refs):
        in_refs = refs[:n_in]
        c_refs = refs[n_in:n_in + n_c]
        o_refs = refs[n_in + n_c:n_in + n_c + n_o]
        a_refs = refs[n_in + n_c + n_o:]
        ov, av = fn([r[...] for r in in_refs], [r[...] for r in c_refs])
        for r, v in zip(o_refs, ov):
            r[...] = v.astype(r.dtype)
        if n_a:
            i = pl.program_id(0)

            @pl.when(i == 0)
            def _():
                for r, v in zip(a_refs, av):
                    r[...] = v

            @pl.when(i > 0)
            def _():
                for r, v in zip(a_refs, av):
                    r[...] += v

    blk = (sum(_nbytes((tm, w), a.dtype) for (a, w, _) in ins) + sum(_nbytes(c.shape, c.dtype) for c in consts)
           + sum(_nbytes((tm, w), dt) for (w, dt) in outs) + sum(_nbytes(s, F32) for s in accs))
    res = pl.pallas_call(
        body, name=name, grid=(nt,), in_specs=in_specs, out_specs=out_specs, out_shape=out_shape,
        compiler_params=pltpu.CompilerParams(
            dimension_semantics=("arbitrary",), vmem_limit_bytes=_vmem_limit(blk)),
    )(*[a for (a, _, _) in ins], *consts)
    return res


def _colsum(v):
    return jnp.sum(v, axis=0, keepdims=True)


def _fgate_fwd(fl, fb, T, tm=512):
    tm = min(tm, T)

    def body(fl_ref, fb_ref, f_ref, carry_ref):
        i = pl.program_id(0)

        @pl.when(i == 0)
        def _():
            carry_ref[...] = jnp.zeros_like(carry_ref)

        row = lax.broadcasted_iota(jnp.int32, (SLAB, DH), 0)

        def slab(s, carry):
            r0 = pl.multiple_of(s * SLAB, SLAB)
            z = fl_ref[pl.ds(r0, SLAB), :] + fb_ref[...]
            c = jnp.minimum(z, 0.0) - _log1p(jnp.exp(-jnp.abs(z)))
            for k in (1, 2, 4):
                c = c + jnp.where(row >= k, pltpu.roll(c, k, 0), 0.0)
            c = c + carry
            f_ref[pl.ds(r0, SLAB), :] = c
            return c[SLAB - 1:SLAB, :]

        carry_ref[0:1, :] = lax.fori_loop(0, tm // SLAB, slab, carry_ref[0:1, :])

    return pl.pallas_call(
        body, name="fgate_fwd", grid=(T // tm,),
        in_specs=[pl.BlockSpec((tm, DH), lambda i: (i, 0)), pl.BlockSpec((1, DH), lambda i: (0, 0))],
        out_specs=pl.BlockSpec((tm, DH), lambda i: (i, 0)),
        out_shape=jax.ShapeDtypeStruct((T, DH), F32),
        scratch_shapes=[pltpu.VMEM((SLAB, DH), F32)],
        compiler_params=pltpu.CompilerParams(dimension_semantics=("arbitrary",)),
    )(fl, fb)


def _fgate_bwd(dF, fl, fb, T, tm=512):
    tm = min(tm, T)
    nt = T // tm

    def body(df_ref, fl_ref, fb_ref, o_ref, acc_ref, carry_ref):
        i = pl.program_id(0)

        @pl.when(i == 0)
        def _():
            carry_ref[...] = jnp.zeros_like(carry_ref)
            acc_ref[...] = jnp.zeros_like(acc_ref)

        row = lax.broadcasted_iota(jnp.int32, (SLAB, DH), 0)

        def slab(n, carry):
            g_next, acc = carry
            r0 = pl.multiple_of((tm // SLAB - 1 - n) * SLAB, SLAB)
            c = df_ref[pl.ds(r0, SLAB), :]
            for k in (1, 2, 4):
                c = c + jnp.where(row < SLAB - k, pltpu.roll(c, SLAB - k, 0), 0.0)
            c = c + g_next
            z = fl_ref[pl.ds(r0, SLAB), :] + fb_ref[...]
            dfl = c * _sig(-z)
            o_ref[pl.ds(r0, SLAB), :] = dfl.astype(o_ref.dtype)
            return c[0:1, :], acc + _colsum(dfl)

        g, acc = lax.fori_loop(0, tm // SLAB, slab, (carry_ref[0:1, :], jnp.zeros((1, DH), F32)))
        carry_ref[0:1, :] = g
        acc_ref[...] += acc

    return pl.pallas_call(
        body, name="fgate_bwd", grid=(nt,),
        in_specs=[pl.BlockSpec((tm, DH), lambda i: (nt - 1 - i, 0)), pl.BlockSpec((tm, DH), lambda i: (nt - 1 - i, 0)),
                  pl.BlockSpec((1, DH), lambda i: (0, 0))],
        out_specs=[pl.BlockSpec((tm, DH), lambda i: (nt - 1 - i, 0)), pl.BlockSpec((1, DH), lambda i: (0, 0))],
        out_shape=[jax.ShapeDtypeStruct((T, DH), BF), jax.ShapeDtypeStruct((1, DH), F32)],
        scratch_shapes=[pltpu.VMEM((SLAB, DH), F32)],
        compiler_params=pltpu.CompilerParams(dimension_semantics=("arbitrary",)),
    )(dF, fl, fb)


def _conv(x, prev8, cw, cb):
    xs = [x] + [_shift_down(x, d, prev8) for d in (1, 2, 3)]
    xa = cb + cw[3:4, :] * xs[0] + cw[2:3, :] * xs[1] + cw[1:2, :] * xs[2] + cw[0:1, :] * xs[3]
    return xa, xs


def _lru_gates(xa_g, wa_g, wx_g, ba_g, bx_g, sp_g):
    xb = xa_g.astype(BF)
    r = _sig(_dot(xb, wa_g) + ba_g)
    ig = _sig(_dot(xb, wx_g) + bx_g)
    la = -LRU_C * r * sp_g
    a = jnp.exp(la)
    mult = jnp.sqrt(-_expm1(2.0 * la))
    return r, ig, a, mult


def _lru_fwd(xg, cw, vec, wabd, wxbd, T, tm=256):
    tm = min(tm, T)
    nsl = tm // SLAB

    def body(x_ref, xp_ref, cw_ref, vec_ref, wa_ref, wx_ref, h_ref, a_s, b_s, carry_ref):
        i = pl.program_id(0)

        @pl.when(i == 0)
        def _():
            carry_ref[...] = jnp.zeros_like(carry_ref)

        x = x_ref[...]
        prev8 = jnp.where(i > 0, xp_ref[...], 0.0)
        vec_v = vec_ref[...]
        xa, _ = _conv(x, prev8, cw_ref[...], vec_v[0:1, :])
        sp = _softplus(-vec_v[3:4, :])
        for g in range(NBD):
            sl = slice(g * BD, (g + 1) * BD)
            _, ig, a, mult = _lru_gates(xa[:, sl], wa_ref[g], wx_ref[g], vec_v[1:2, sl], vec_v[2:3, sl], sp[:, sl])
            a_s[:, sl] = a
            b_s[:, sl] = mult * ig * xa[:, sl]

        def slab(s, carry):
            r0 = pl.multiple_of(s * SLAB, SLAB)
            A, B = _slab_scan_fwd(a_s[pl.ds(r0, SLAB), :], b_s[pl.ds(r0, SLAB), :])
            h = A * carry + B
            h_ref[pl.ds(r0, SLAB), :] = h
            return h[SLAB - 1:SLAB, :]

        carry_ref[0:1, :] = lax.fori_loop(0, nsl, slab, carry_ref[0:1, :])

    blk = 5 * _nbytes((tm, D), F32) + 2 * _nbytes((NBD, BD, BD), BF)
    return pl.pallas_call(
        body, name="lru_fwd", grid=(T // tm,),
        in_specs=[pl.BlockSpec((tm, D), lambda i: (i, 0)),
                  pl.BlockSpec((SLAB, D), lambda i: (jnp.maximum(i * nsl - 1, 0), 0)),
                  pl.BlockSpec((CONV, D), lambda i: (0, 0)),
                  pl.BlockSpec((SLAB, D), lambda i: (0, 0)),
                  pl.BlockSpec((NBD, BD, BD), lambda i: (0, 0, 0)),
                  pl.BlockSpec((NBD, BD, BD), lambda i: (0, 0, 0))],
        out_specs=pl.BlockSpec((tm, D), lambda i: (i, 0)),
        out_shape=jax.ShapeDtypeStruct((T, D), F32),
        scratch_shapes=[pltpu.VMEM((tm, D), F32), pltpu.VMEM((tm, D), F32), pltpu.VMEM((SLAB, D), F32)],
        compiler_params=pltpu.CompilerParams(dimension_semantics=("arbitrary",), vmem_limit_bytes=_vmem_limit(blk)),
    )(xg, xg, cw, vec, wabd, wxbd)


def _lru_bwd(xg, h, dha, cw, vec, wabd, wxbd, T, tm=256):
    tm = min(tm, T)
    nsl = tm // SLAB
    nt = T // tm

    def body(x_ref, xp_ref, h_ref, hp_ref, dh_ref, cw_ref, vec_ref, wa_ref, wx_ref,
             dx_ref, dwa_ref, dwx_ref, acc_ref, a_s, b_s, g_s, dxa_s, carry_ref, dxan_ref):
        n = pl.program_id(0)
        it = nt - 1 - n

        @pl.when(n == 0)
        def _():
            carry_ref[...] = jnp.zeros_like(carry_ref)
            dxan_ref[...] = jnp.zeros_like(dxan_ref)
            dwa_ref[...] = jnp.zeros_like(dwa_ref)
            dwx_ref[...] = jnp.zeros_like(dwx_ref)
            acc_ref[...] = jnp.zeros_like(acc_ref)

        x = x_ref[...]
        prev8 = jnp.where(it > 0, xp_ref[...], 0.0)
        hprev8 = jnp.where(it > 0, hp_ref[...], 0.0)
        vec_v = vec_ref[...]
        cw_v = cw_ref[...]
        xa, xs = _conv(x, prev8, cw_v, vec_v[0:1, :])
        sp = _softplus(-vec_v[3:4, :])
        gates = []
        for g in range(NBD):
            sl = slice(g * BD, (g + 1) * BD)
            r, ig, a, mult = _lru_gates(xa[:, sl], wa_ref[g], wx_ref[g], vec_v[1:2, sl], vec_v[2:3, sl], sp[:, sl])
            gates.append((r, ig, a, mult))
            a_s[:, sl] = a
        a_next = _shift_up(a_s[...], 1, carry_ref[...])
        a_s[...] = a_next
        b_s[...] = dh_ref[...]

        def slab(m, carry):
            r0 = pl.multiple_of((nsl - 1 - m) * SLAB, SLAB)
            A, B = _slab_scan_bwd(a_s[pl.ds(r0, SLAB), :], b_s[pl.ds(r0, SLAB), :])
            gg = A * carry + B
            g_s[pl.ds(r0, SLAB), :] = gg
            return gg[0:1, :]

        g_first = lax.fori_loop(0, nsl, slab, carry_ref[1:2, :])
        gt = g_s[...]
        h_prev = _shift_down(h_ref[...], 1, hprev8)
        dba = []
        dbx = []
        dsp = []
        for g in range(NBD):
            sl = slice(g * BD, (g + 1) * BD)
            r, ig, a, mult = gates[g]
            xa_g = xa[:, sl]
            g_g = gt[:, sl]
            da = g_g * h_prev[:, sl]
            dmult = g_g * ig * xa_g
            di = g_g * mult * xa_g
            dxa_g = g_g * mult * ig
            dla = da * a - dmult * (a * a / mult)
            dr = dla * (-LRU_C) * sp[:, sl]
            dsp.append(_colsum(dla * (-LRU_C) * r))
            dra = (dr * r * (1.0 - r))
            dix = (di * ig * (1.0 - ig))
            dba.append(_colsum(dra))
            dbx.append(_colsum(dix))
            dra_b = dra.astype(BF)
            dix_b = dix.astype(BF)
            xb = xa_g.astype(BF)
            dxa_g = dxa_g + _dot(dra_b, wa_ref[g], "nt") + _dot(dix_b, wx_ref[g], "nt")
            dwa_ref[g] += _dot(xb, dra_b, "tn")
            dwx_ref[g] += _dot(xb, dix_b, "tn")
            dxa_s[:, sl] = dxa_g
        dxa = dxa_s[...]
        nxt = dxan_ref[...]
        dx = (cw_v[3:4, :] * dxa + cw_v[2:3, :] * _shift_up(dxa, 1, nxt)
              + cw_v[1:2, :] * _shift_up(dxa, 2, nxt) + cw_v[0:1, :] * _shift_up(dxa, 3, nxt))
        dx_ref[...] = dx.astype(dx_ref.dtype)
        acc_ref[0:1, :] += jnp.concatenate(dba, axis=1)
        acc_ref[1:2, :] += jnp.concatenate(dbx, axis=1)
        acc_ref[2:3, :] += jnp.concatenate(dsp, axis=1)
        acc_ref[3:4, :] += _colsum(dxa)
        for k in range(CONV):
            acc_ref[4 + k:5 + k, :] += _colsum(dxa * xs[CONV - 1 - k])
        dxan_ref[...] = dxa[0:SLAB, :]
        a_first = jnp.concatenate([gates[g][2][0:1, :] for g in range(NBD)], axis=1)
        carry_ref[0:1, :] = a_first
        carry_ref[1:2, :] = g_first

        @pl.when(n == nt - 1)
        def _():
            acc_ref[2:3, :] = acc_ref[2:3, :] * (-_sig(-vec_v[3:4, :]))

    rowblk = lambda i: (nt - 1 - i, 0)
    prevblk = lambda i: (jnp.maximum((nt - 1 - i) * nsl - 1, 0), 0)
    c2 = lambda i: (0, 0)
    c3 = lambda i: (0, 0, 0)
    blk = 12 * _nbytes((tm, D), F32) + 6 * _nbytes((NBD, BD, BD), F32)
    return pl.pallas_call(
        body, name="lru_bwd", grid=(nt,),
        in_specs=[pl.BlockSpec((tm, D), rowblk), pl.BlockSpec((SLAB, D), prevblk),
                  pl.BlockSpec((tm, D), rowblk), pl.BlockSpec((SLAB, D), prevblk),
                  pl.BlockSpec((tm, D), rowblk),
                  pl.BlockSpec((CONV, D), c2), pl.BlockSpec((SLAB, D), c2),
                  pl.BlockSpec((NBD, BD, BD), c3), pl.BlockSpec((NBD, BD, BD), c3)],
        out_specs=[pl.BlockSpec((tm, D), rowblk), pl.BlockSpec((NBD, BD, BD), c3), pl.BlockSpec((NBD, BD, BD), c3),
                   pl.BlockSpec((16, D), c2)],
        out_shape=[jax.ShapeDtypeStruct((T, D), BF), jax.ShapeDtypeStruct((NBD, BD, BD), F32),
                   jax.ShapeDtypeStruct((NBD, BD, BD), F32), jax.ShapeDtypeStruct((16, D), F32)],
        scratch_shapes=[pltpu.VMEM((tm, D), F32), pltpu.VMEM((tm, D), F32), pltpu.VMEM((tm, D), F32),
                        pltpu.VMEM((tm, D), F32), pltpu.VMEM((SLAB, D), F32), pltpu.VMEM((SLAB, D), F32)],
        compiler_params=pltpu.CompilerParams(dimension_semantics=("arbitrary",), vmem_limit_bytes=_vmem_limit(blk)),
    )(xg, xg, h, h, dha, cw, vec, wabd, wxbd)


_SCALE = 1.0 / math.sqrt(DH)


def _attn_fwd(qkv, fcol, frow, T, blk=512):
    blk = min(blk, T)
    nb = T // blk

    def body(q_ref, k_ref, v_ref, fc_ref, fr_ref, o_ref, lse_ref):
        i = pl.program_id(1)
        q = q_ref[...]
        fc = fc_ref[...]

        def scores(j):
            r0 = pl.multiple_of(j * blk, blk)
            s = _dot(q, k_ref[pl.ds(r0, blk), :], "nt") * _SCALE + (fc - fr_ref[j])
            return s, v_ref[pl.ds(r0, blk), :]

        def update(s, vj, carry):
            m, l, acc = carry
            m_new = jnp.maximum(m, jnp.max(s, axis=1, keepdims=True))
            alpha = jnp.exp(m - m_new)
            p = jnp.exp(s - m_new)
            l = alpha * l + jnp.sum(p, axis=1, keepdims=True)
            acc = alpha * acc + _dot(p.astype(BF), vj)
            return m_new, l, acc

        def step(j, carry):
            s, vj = scores(j)
            return update(s, vj, carry)

        init = (jnp.full((blk, 1), -jnp.inf, F32), jnp.zeros((blk, 1), F32), jnp.zeros((blk, DH), F32))
        carry = lax.fori_loop(0, i, step, init)
        s, vj = scores(i)
        rq = lax.broadcasted_iota(jnp.int32, (blk, blk), 0)
        ck = lax.broadcasted_iota(jnp.int32, (blk, blk), 1)
        m, l, acc = update(jnp.where(rq >= ck, s, -jnp.inf), vj, carry)
        o_ref[...] = (acc / l).astype(o_ref.dtype)
        lse_ref[...] = m + jnp.log(l)

    vm = 2 * _nbytes((T, DH), BF) + 8 * _nbytes((blk, blk), F32)
    return pl.pallas_call(
        body, name="attn_fwd", grid=(NH, nb),
        in_specs=[pl.BlockSpec((blk, DH), lambda h, i: (i, h)),
                  pl.BlockSpec((T, DH), lambda h, i: (0, NH + h)),
                  pl.BlockSpec((T, DH), lambda h, i: (0, 2 * NH + h)),
                  pl.BlockSpec((None, blk, 1), lambda h, i: (h, i, 0)),
                  pl.BlockSpec((None, nb, 1, blk), lambda h, i: (h, 0, 0, 0))],
        out_specs=[pl.BlockSpec((blk, DH), lambda h, i: (i, h)),
                   pl.BlockSpec((None, blk, 1), lambda h, i: (h, i, 0))],
        out_shape=[jax.ShapeDtypeStruct((T, D), BF), jax.ShapeDtypeStruct((NH, T, 1), F32)],
        compiler_params=pltpu.CompilerParams(dimension_semantics=("parallel", "arbitrary"),
                                             vmem_limit_bytes=_vmem_limit(vm)),
    )(qkv, qkv, qkv, fcol, frow)


def _attn_bwd(qkv, do, fcol, frow, lrow, drow, T, blk=512):
    blk = min(blk, T)
    nb = T // blk

    def body(k_ref, v_ref, q_ref, do_ref, fs_ref, ft_ref, l_ref, d_ref, dq_ref, dk_ref, dv_ref, dfs_ref, dft_ref):
        j = pl.program_id(1)

        @pl.when(j == 0)
        def _():
            dq_ref[...] = jnp.zeros_like(dq_ref)
            dft_ref[...] = jnp.zeros_like(dft_ref)

        kj = k_ref[...]
        vj = v_ref[...]
        fs = fs_ref[...]

        def step(i, carry, diag):
            dk, dv, dfs = carry
            r0 = pl.multiple_of(i * blk, blk)
            qi = q_ref[pl.ds(r0, blk), :]
            doi = do_ref[pl.ds(r0, blk), :]
            st = _dot(kj, qi, "nt") * _SCALE + (ft_ref[i] - fs) - l_ref[i]
            if diag:
                rk = lax.broadcasted_iota(jnp.int32, (blk, blk), 0)
                cq = lax.broadcasted_iota(jnp.int32, (blk, blk), 1)
                st = jnp.where(cq >= rk, st, -jnp.inf)
            pt = jnp.exp(st)
            dv = dv + _dot(pt.astype(BF), doi)
            dpt = _dot(vj, doi, "nt")
            dst = pt * (dpt - d_ref[i])
            dsb = dst.astype(BF)
            dk = dk + _dot(dsb, qi)
            dq_ref[pl.ds(r0, blk), :] += _dot(dsb, kj, "tn") * _SCALE
            dfs = dfs - jnp.sum(dst, axis=1, keepdims=True)
            dft_ref[i] += jnp.sum(dst, axis=0, keepdims=True)
            return dk, dv, dfs

        init = (jnp.zeros((blk, DH), F32), jnp.zeros((blk, DH), F32), jnp.zeros((blk, 1), F32))
        carry = step(j, init, True)
        dk, dv, dfs = lax.fori_loop(j + 1, nb, lambda i, c: step(i, c, False), carry)
        dk_ref[...] = (dk * _SCALE).astype(dk_ref.dtype)
        dv_ref[...] = dv.astype(dv_ref.dtype)
        dfs_ref[...] = dfs

    rowv = pl.BlockSpec((None, nb, 1, blk), lambda h, j: (h, 0, 0, 0))
    vm = 3 * _nbytes((T, DH), F32) + 8 * _nbytes((blk, blk), F32)
    return pl.pallas_call(
        body, name="attn_bwd", grid=(NH, nb),
        in_specs=[pl.BlockSpec((blk, DH), lambda h, j: (j, NH + h)),
                  pl.BlockSpec((blk, DH), lambda h, j: (j, 2 * NH + h)),
                  pl.BlockSpec((T, DH), lambda h, j: (0, h)),
                  pl.BlockSpec((T, DH), lambda h, j: (0, h)),
                  pl.BlockSpec((None, blk, 1), lambda h, j: (h, j, 0)),
                  rowv, rowv, rowv],
        out_specs=[pl.BlockSpec((T, DH), lambda h, j: (0, h)),
                   pl.BlockSpec((blk, DH), lambda h, j: (j, h)),
                   pl.BlockSpec((blk, DH), lambda h, j: (j, h)),
                   pl.BlockSpec((None, blk, 1), lambda h, j: (h, j, 0)), rowv],
        out_shape=[jax.ShapeDtypeStruct((T, D), F32), jax.ShapeDtypeStruct((T, D), BF),
                   jax.ShapeDtypeStruct((T, D), BF), jax.ShapeDtypeStruct((NH, T, 1), F32),
                   jax.ShapeDtypeStruct((NH, nb, 1, blk), F32)],
        compiler_params=pltpu.CompilerParams(dimension_semantics=("parallel", "arbitrary"),
                                             vmem_limit_bytes=_vmem_limit(vm)),
    )(qkv, qkv, qkv, do, fcol, frow, lrow, drow)


def _norm_fn(ins, cs):
    x, = ins
    g, = cs
    r = lax.rsqrt(jnp.mean(x * x, axis=-1, keepdims=True) + EPS)
    return [x * r * g], []


def _norm_bwd_fn(ins, cs):
    x, dy, dres = ins
    g, = cs
    r = lax.rsqrt(jnp.mean(x * x, axis=-1, keepdims=True) + EPS)
    xh = x * r
    dxh = dy * g
    dx = dres + r * (dxh - xh * jnp.mean(dxh * xh, axis=-1, keepdims=True))
    return [dx], [_colsum(dy * xh)]


def _final_fn(ins, cs):
    x2, tgt = ins
    g, = cs
    r = lax.rsqrt(jnp.mean(x2 * x2, axis=-1, keepdims=True) + EPS)
    xh = x2 * r
    e = xh * g - tgt
    dy = e * (1.0 / D)
    dxh = dy * g
    dx2 = r * (dxh - xh * jnp.mean(dxh * xh, axis=-1, keepdims=True))
    return [dx2], [_colsum(0.5 * e * e * (1.0 / D)), _colsum(dy * xh)]


def _z_fn(ins, cs):
    g, h = ins
    return [_gelu(g) * h], []


def _mix_fn(ins, cs):
    gates, ya, yb = ins
    return [_sig(gates[:, :D]) * ya + _sig(gates[:, D:]) * yb], []


def _relu2_fn(ins, cs):
    hp, = ins
    r = jnp.maximum(hp, 0.0)
    return [r * r], []


def _mix_bwd_fn(ins, cs):
    dmix, gates, ya, yb = ins
    ga = _sig(gates[:, :D])
    gb = _sig(gates[:, D:])
    dgates = jnp.concatenate([dmix * ya * ga * (1.0 - ga), dmix * yb * gb * (1.0 - gb)], axis=1)
    return [dmix * ga, dmix * gb, dgates], []


def _z_bwd_fn(ins, cs):
    dz, g, h = ins
    return [dz * _gelu(g), dz * h * _gelu_grad(g)], []


def _delta_fn(ins, cs):
    do, o = ins
    p = do.astype(F32) * o.astype(F32)
    lane = lax.broadcasted_iota(jnp.int32, (p.shape[0], DH), 1)
    out = jnp.zeros((p.shape[0], DH), F32)
    for hd in range(NH):
        s = jnp.sum(p[:, hd * DH:(hd + 1) * DH], axis=1, keepdims=True)
        out = jnp.where(lane == hd, s, out)
    return [out], []


def _local_step(x, tgt, w, T, blk=512):
    blk = min(blk, T)
    nb = T // blk
    win = w["win"]

    u, = _ew(_norm_fn, T, 512, [(x, D, 0)], [w["g_mix"]], [(D, BF)], [], name="norm_mix")
    xg = _mm(u, win, "nn", T, 2 * D, D, name="proj_lru")
    qkv = _mm(u, win, "nn", T, 3 * D, D, name="proj_qkv", out_dtype=BF, b_off=(0, 2))
    gates = _mm(u, win, "nn", T, 2 * D, D, name="proj_gates", b_off=(0, 5))
    fl = _mm(u, win, "nn", T, DH, D, name="proj_f", tn=DH, b_off=(0, 7 * D // DH))
    fcum = _fgate_fwd(fl, w["fb"], T)
    f8 = fcum[:, :NH].T
    fcol = f8[:, :, None]
    frow = f8.reshape(NH, nb, 1, blk)
    h = _lru_fwd(xg, w["cw"], w["vec"], w["wabd"], w["wxbd"], T)
    ob, lse = _attn_fwd(qkv, fcol, frow, T, blk)
    z, = _ew(_z_fn, T, 512, [(xg, D, 1), (h, D, 0)], [], [(D, BF)], [], name="lru_gelu")
    ya = _mm(z, w["wa"], "nn", T, D, D, name="branch_a")
    yb = _mm(ob, w["wb"], "nn", T, D, D, name="branch_b")
    mix, = _ew(_mix_fn, T, 256, [(gates, 2 * D, 0), (ya, D, 0), (yb, D, 0)], [], [(D, BF)], [], name="mix")
    x1 = _mm(mix, w["wout"], "nn", T, D, D, name="out_proj", add=x)
    m, = _ew(_norm_fn, T, 512, [(x1, D, 0)], [w["g_mlp"]], [(D, BF)], [], name="norm_mlp")
    hpre = _mm(m, w["wup"], "nn", T, FF, D, name="mlp_up")
    hh, = _ew(_relu2_fn, T, 256, [(hpre, FF, 0)], [], [(FF, BF)], [], name="relu2")
    x2 = _mm(hh, w["wdown"], "nn", T, D, FF, name="mlp_down", add=x1)
    dx2, loss_vec, dg_fin = _ew(_final_fn, T, 256, [(x2, D, 0), (tgt, D, 0)], [w["g_fin"]], [(D, F32)],
                                [(1, D), (1, D)], name="final_norm_loss")

    dhpre = _mm(dx2, w["wdown"], "nt", T, FF, D, name="mlp_down_bwd", out_dtype=BF,
                epi=lambda acc, hp: acc * (2.0 * jnp.maximum(hp, 0.0)), epi_ins=[hpre])
    dwdown = _mm(hh, dx2, "tn", FF, D, T, name="dw_down", tk=512)
    dwup = _mm(m, dhpre, "tn", D, FF, T, name="dw_up", tk=512)
    dm = _mm(dhpre, w["wup"], "nt", T, D, FF, name="mlp_up_bwd")
    dx1, dg_mlp = _ew(_norm_bwd_fn, T, 256, [(x1, D, 0), (dm, D, 0), (dx2, D, 0)], [w["g_mlp"]], [(D, F32)],
                      [(1, D)], name="norm_mlp_bwd")

    dmix = _mm(dx1, w["wout"], "nt", T, D, D, name="out_proj_bwd")
    dwout = _mm(mix, dx1, "tn", D, D, T, name="dw_out", tk=512)
    dya, dyb, dgates = _ew(_mix_bwd_fn, T, 256, [(dmix, D, 0), (gates, 2 * D, 0), (ya, D, 0), (yb, D, 0)], [],
                           [(D, BF), (D, BF), (2 * D, BF)], [], name="mix_bwd")
    dob = _mm(dyb, w["wb"], "nt", T, D, D, name="branch_b_bwd", out_dtype=BF)
    dwb = _mm(ob, dyb, "tn", D, D, T, name="dw_b", tk=512)
    dz = _mm(dya, w["wa"], "nt", T, D, D, name="branch_a_bwd")
    dwa = _mm(z, dya, "tn", D, D, T, name="dw_a", tk=512)
    dha, dglru = _ew(_z_bwd_fn, T, 256, [(dz, D, 0), (xg, D, 1), (h, D, 0)], [], [(D, F32), (D, BF)], [],
                     name="lru_gelu_bwd")

    delta, = _ew(_delta_fn, T, 512, [(dob, D, 0), (ob, D, 0)], [], [(DH, F32)], [], name="attn_delta")
    drow = delta[:, :NH].T.reshape(NH, nb, 1, blk)
    lrow = lse.reshape(NH, nb, 1, blk)
    dq, dk, dv, dfs, dft = _attn_bwd(qkv, dob, fcol, frow, lrow, drow, T, blk)
    dfcum = jnp.pad((dfs[:, :, 0] + dft.reshape(NH, T)).T, ((0, 0), (0, DH - NH)))
    dfl, dfb = _fgate_bwd(dfcum, fl, w["fb"], T)

    dxl, dwabd, dwxbd, lacc = _lru_bwd(xg, h, dha, w["cw"], w["vec"], w["wabd"], w["wxbd"], T)

    du = _mm(dxl, win, "nt", T, D, D, name="du_lru")
    du = _mm(dglru, win, "nt", T, D, D, name="du_glru", b_off=(0, 1), add=du)
    du = _mm(dq, win, "nt", T, D, D, name="du_q", b_off=(0, 2), add=du)
    du = _mm(dk, win, "nt", T, D, D, name="du_k", b_off=(0, 3), add=du)
    du = _mm(dv, win, "nt", T, D, D, name="du_v", b_off=(0, 4), add=du)
    du = _mm(dgates, win, "nt", T, D, 2 * D, name="du_gates", b_off=(0, 5), add=du)
    du = _mm(dfl, win, "nt", T, D, DH, name="du_f", tk=DH, b_off=(0, 7 * D // DH), add=du)
    pieces = [_mm(u, p, "tn", D, p.shape[1], T, name="dw_in_%d" % n, tk=512)
              for n, p in enumerate((dxl, dglru, dq, dk, dv, dgates))]
    pieces.append(_mm(u, dfl, "tn", D, DH, T, name="dw_in_f", tk=512)[:, :NH])
    dwin = jnp.concatenate(pieces, axis=1)
    dx, dg_mix = _ew(_norm_bwd_fn, T, 256, [(x, D, 0), (du, D, 0), (dx1, D, 0)], [w["g_mix"]], [(D, F32)],
                     [(1, D)], name="norm_mix_bwd")

    return dict(dx=dx, dwin=dwin, dwa=dwa, dwb=dwb, dwout=dwout, dwup=dwup, dwdown=dwdown,
                dwabd=dwabd, dwxbd=dwxbd, lacc=lacc, dfb=dfb, dg_mix=dg_mix, dg_mlp=dg_mlp, dg_fin=dg_fin,
                loss_vec=loss_vec)


def _block_diag(w):
    per = BD // LRU_BW
    w4 = w.reshape(NBD, per, LRU_BW, LRU_BW)
    out = jnp.zeros((NBD, per, LRU_BW, per, LRU_BW), w.dtype)
    for b in range(per):
        out = out.at[:, b, :, b, :].set(w4[:, b])
    return out.reshape(NBD, BD, BD)


def _block_diag_extract(wbd):
    per = BD // LRU_BW
    w5 = wbd.reshape(NBD, per, LRU_BW, per, LRU_BW)
    return jnp.stack([w5[:, b, :, b, :] for b in range(per)], axis=1).reshape(LRU_BLOCKS, LRU_BW, LRU_BW)


_ANY = pl.BlockSpec(memory_space=pl.ANY)


def _place():
    x, y, c = lax.axis_index("x"), lax.axis_index("y"), lax.axis_index("c")
    chips = [(1 - x, y), (x, 1 - y), (1 - x, 1 - y)]
    return x, y, c, chips


def _allgather_shards(shards):
    n = len(shards)

    def body(*refs):
        ins, outs = refs[:n], refs[n:2 * n]
        send_sems, recv_sems, local_sems = refs[2 * n:]
        x, y, c, chips = _place()
        me = 2 * x + y
        sibling = (x, y, 1 - c)

        def remote(p, k, src, dst, to):
            return pltpu.make_async_remote_copy(src_ref=src, dst_ref=dst, send_sem=send_sems.at[p, k],
                                                recv_sem=recv_sems.at[p, k], device_id=to, device_id_type=MESH)

        started = []
        for p in range(n):
            mine = pltpu.make_async_copy(ins[p], outs[p].at[me], local_sems.at[p])
            mine.start()
            started.append(mine)
        sent = []
        for p in range(n):
            for k, chip in enumerate(chips):
                cp = remote(p, k, ins[p].at[c], outs[p].at[me, c], (chip[0], chip[1], c))
                cp.start()
                sent.append(cp)
        for p in range(n):
            for k, chip in enumerate(chips):
                half = outs[p].at[2 * chip[0] + chip[1], c]
                remote(p, k, half, half, sibling).wait_recv()
                fwd = remote(p, 3 + k, half, half, sibling)
                fwd.start()
                sent.append(fwd)
        for p in range(n):
            for k, chip in enumerate(chips):
                half = outs[p].at[2 * chip[0] + chip[1], 1 - c]
                remote(p, 3 + k, half, half, sibling).wait_recv()
        for cp in sent:
            cp.wait_send()
        for cp in started:
            cp.wait()

    return pl.pallas_call(
        body, name="allgather_weights",
        in_specs=[_ANY] * n, out_specs=[_ANY] * n,
        out_shape=[jax.ShapeDtypeStruct((NCHIP,) + s.shape, s.dtype) for s in shards],
        scratch_shapes=[pltpu.SemaphoreType.DMA((n, 6)), pltpu.SemaphoreType.DMA((n, 6)),
                        pltpu.SemaphoreType.DMA((n,))],
    )(*shards)


def _exchange_sibling(grads):
    n = len(grads)

    def body(*refs):
        ins, outs = refs[:n], refs[n:2 * n]
        send_sems, recv_sems = refs[2 * n:]
        x, y, c, _ = _place()
        sibling = (x, y, 1 - c)
        copies = []
        for p in range(n):
            for j in range(NCHIP):
                cp = pltpu.make_async_remote_copy(src_ref=ins[p].at[j, 1 - c], dst_ref=outs[p].at[j],
                                                  send_sem=send_sems.at[p, j], recv_sem=recv_sems.at[p, j],
                                                  device_id=sibling, device_id_type=MESH)
                cp.start()
                copies.append(cp)
        for cp in copies:
            cp.wait()

    return pl.pallas_call(
        body, name="reduce_sibling_exchange",
        in_specs=[_ANY] * n, out_specs=[_ANY] * n,
        out_shape=[jax.ShapeDtypeStruct((NCHIP,) + g.shape[2:], g.dtype) for g in grads],
        scratch_shapes=[pltpu.SemaphoreType.DMA((n, NCHIP)), pltpu.SemaphoreType.DMA((n, NCHIP))],
    )(*grads)


def _exchange_chips(sums):
    n = len(sums)

    def body(*refs):
        ins, outs = refs[:n], refs[n:2 * n]
        send_sems, recv_sems = refs[2 * n:]
        x, y, c, chips = _place()
        copies = []
        for p in range(n):
            for k, chip in enumerate(chips):
                cp = pltpu.make_async_remote_copy(src_ref=ins[p].at[2 * chip[0] + chip[1]], dst_ref=outs[p].at[k],
                                                  send_sem=send_sems.at[p, k], recv_sem=recv_sems.at[p, k],
                                                  device_id=(chip[0], chip[1], c), device_id_type=MESH)
                cp.start()
                copies.append(cp)
        for cp in copies:
            cp.wait()

    return pl.pallas_call(
        body, name="reduce_chip_exchange",
        in_specs=[_ANY] * n, out_specs=[_ANY] * n,
        out_shape=[jax.ShapeDtypeStruct((3,) + s.shape[1:], s.dtype) for s in sums],
        scratch_shapes=[pltpu.SemaphoreType.DMA((n, 3)), pltpu.SemaphoreType.DMA((n, 3))],
    )(*sums)


def _share_halves(halves):
    n = len(halves)

    def body(*refs):
        ins, outs = refs[:n], refs[n:2 * n]
        send_sems, recv_sems, local_sems = refs[2 * n:]
        x, y, c, _ = _place()
        sibling = (x, y, 1 - c)
        copies = []
        for p in range(n):
            mine = pltpu.make_async_copy(ins[p], outs[p].at[c], local_sems.at[p])
            mine.start()
            cp = pltpu.make_async_remote_copy(src_ref=ins[p], dst_ref=outs[p].at[c], send_sem=send_sems.at[p],
                                              recv_sem=recv_sems.at[p], device_id=sibling, device_id_type=MESH)
            cp.start()
            copies += [mine, cp]
        for cp in copies:
            cp.wait()

    return pl.pallas_call(
        body, name="reduce_share_halves",
        in_specs=[_ANY] * n, out_specs=[_ANY] * n,
        out_shape=[jax.ShapeDtypeStruct((2,) + h.shape, h.dtype) for h in halves],
        scratch_shapes=[pltpu.SemaphoreType.DMA((n,)), pltpu.SemaphoreType.DMA((n,)), pltpu.SemaphoreType.DMA((n,))],
    )(*halves)


def _row_tile(half, cols):
    th = max(SLAB, min(half, (1 << 18) // cols // SLAB * SLAB))
    while half % th:
        th -= SLAB
    return th


def _add_sibling(g, r, c, name):
    _, _, half, cols = g.shape
    th = _row_tile(half, cols)

    def body(c_ref, g_ref, r_ref, o_ref):
        o_ref[...] = g_ref[...] + r_ref[...]

    return pl.pallas_call(
        body, name=name,
        grid_spec=pltpu.PrefetchScalarGridSpec(
            num_scalar_prefetch=1, grid=(NCHIP, half // th),
            in_specs=[pl.BlockSpec((None, None, th, cols), lambda j, i, c_ref: (j, c_ref[0], i, 0)),
                      pl.BlockSpec((None, th, cols), lambda j, i, c_ref: (j, i, 0))],
            out_specs=pl.BlockSpec((None, th, cols), lambda j, i, c_ref: (j, i, 0))),
        out_shape=jax.ShapeDtypeStruct((NCHIP, half, cols), F32),
    )(c, g, r)


def _add_chips(s, rb, me, name):
    _, half, cols = s.shape
    th = _row_tile(half, cols)

    def body(me_ref, s_ref, rb_ref, o_ref):
        o_ref[...] = ((s_ref[...] + rb_ref[0]) + rb_ref[1]) + rb_ref[2]

    return pl.pallas_call(
        body, name=name,
        grid_spec=pltpu.PrefetchScalarGridSpec(
            num_scalar_prefetch=1, grid=(half // th,),
            in_specs=[pl.BlockSpec((None, th, cols), lambda i, me_ref: (me_ref[0], i, 0)),
                      pl.BlockSpec((3, th, cols), lambda i, me_ref: (0, i, 0))],
            out_specs=pl.BlockSpec((th, cols), lambda i, me_ref: (i, 0))),
        out_shape=jax.ShapeDtypeStruct((half, cols), F32),
    )(me, s, rb)


def _reduce_scatter(grads, c1, me1):
    names = ["w_in", "w_a", "w_b", "w_out", "w_up", "w_down"]
    from_sibling = _exchange_sibling(grads)
    sums = [_add_sibling(g, r, c1, "add_sibling_" + nm) for g, r, nm in zip(grads, from_sibling, names)]
    from_chips = _exchange_chips(sums)
    halves = [_add_chips(s, rb, me1, "add_chips_" + nm) for s, rb, nm in zip(sums, from_chips, names)]
    full = _share_halves(halves)
    return [f.reshape(2 * f.shape[1], f.shape[2]) for f in full]


N_DEV = 8
SMALL_ROWS = 144


def _allreduce_small(pack):
    def body(x_ref, out_ref, gbuf, send_sems, recv_sems, local_sem):
        x, y, c, chips = _place()
        me, sibling = (x, y, c), (x, y, 1 - c)

        def rows(px, py, pc):
            return gbuf.at[4 * px + 2 * py + pc]

        def copy(k, block, to, src=None):
            return pltpu.make_async_remote_copy(
                src_ref=rows(*block) if src is None else src, dst_ref=rows(*block),
                send_sem=send_sems.at[k], recv_sem=recv_sems.at[k], device_id=to, device_id_type=MESH)

        mine = pltpu.make_async_copy(x_ref, rows(*me), local_sem)
        mine.start()
        first = [copy(0, me, sibling, src=x_ref)]
        first += [copy(1 + j, me, (chip[0], chip[1], c), src=x_ref) for j, chip in enumerate(chips)]
        for cp in first:
            cp.start()
        passed = [copy(4 + j, (chip[0], chip[1], c), sibling) for j, chip in enumerate(chips)]
        for j, chip in enumerate(chips):
            copy(1 + j, (chip[0], chip[1], c), me).wait_recv()
            passed[j].start()
        copy(0, sibling, me).wait_recv()
        for j, chip in enumerate(chips):
            copy(4 + j, (chip[0], chip[1], 1 - c), me).wait_recv()
        for cp in first + passed:
            cp.wait_send()
        mine.wait()
        acc = gbuf[0]
        for d in range(1, N_DEV):
            acc = acc + gbuf[d]
        out_ref[...] = acc

    return pl.pallas_call(
        body, name="allreduce_small",
        in_specs=[pl.BlockSpec(memory_space=pltpu.VMEM)],
        out_specs=pl.BlockSpec(memory_space=pltpu.VMEM),
        out_shape=jax.ShapeDtypeStruct((SMALL_ROWS, D), F32),
        scratch_shapes=[pltpu.VMEM((N_DEV, SMALL_ROWS, D), F32), pltpu.SemaphoreType.DMA((7,)),
                        pltpu.SemaphoreType.DMA((7,)), pltpu.SemaphoreType.DMA],
    )(pack)


def _adamw(w, g, m, v, name):
    rows, cols = w.shape
    th = _row_tile(rows, cols)

    def body(w_ref, g_ref, m_ref, v_ref, d_ref, mo_ref, vo_ref):
        gv = g_ref[...]
        mn = ADAM_B1 * m_ref[...] + (1.0 - ADAM_B1) * gv
        vn = ADAM_B2 * v_ref[...] + (1.0 - ADAM_B2) * (gv * gv)
        m_hat = mn / (1.0 - ADAM_B1 ** ADAM_STEP)
        v_hat = vn / (1.0 - ADAM_B2 ** ADAM_STEP)
        d_ref[...] = -ADAM_LR * (m_hat / (jnp.sqrt(v_hat) + ADAM_EPS) + ADAM_WD * w_ref[...])
        mo_ref[...] = mn
        vo_ref[...] = vn

    spec = pl.BlockSpec((th, cols), lambda i: (i, 0))
    return pl.pallas_call(
        body, name=name, grid=(rows // th,),
        in_specs=[spec] * 4, out_specs=[spec] * 3,
        out_shape=[jax.ShapeDtypeStruct((rows, cols), F32)] * 3,
        compiler_params=pltpu.CompilerParams(dimension_semantics=("parallel",)),
    )(w, g, m, v)


_SMALL = ["norm_mix_g", "norm_mlp_g", "norm_final_g", "conv_b", "lru_ba", "lru_bx", "lru_lambda"]
_ROW_FB, _ROW_CW, _ROW_WA, _ROW_WX, _ROW_LOSS = 7, 8, 12, 76, 140


def _pack_small(vals, col0):
    rows = [vals[n].reshape(1, D) for n in _SMALL]
    rows.append(jnp.pad(vals["forget_b"].reshape(1, NH), ((0, 0), (0, D - NH))))
    if vals["conv_w"].shape[1] == D:
        rows.append(vals["conv_w"])
    else:
        rows.append(lax.dynamic_update_slice(jnp.zeros((CONV, D), F32), vals["conv_w"], (0, col0)))
    rows.append(vals["lru_wa"].reshape(LRU_BLOCKS * LRU_BW * LRU_BW // D, D))
    rows.append(vals["lru_wx"].reshape(LRU_BLOCKS * LRU_BW * LRU_BW // D, D))
    rows.append(vals["loss"] if "loss" in vals else jnp.zeros((1, D), F32))
    rows.append(jnp.zeros((SMALL_ROWS - _ROW_LOSS - 1, D), F32))
    return jnp.concatenate(rows, axis=0)


def _unpack_small(pack, col0):
    out = {n: pack[i] for i, n in enumerate(_SMALL)}
    out["forget_b"] = pack[_ROW_FB, :NH]
    out["conv_w"] = lax.dynamic_slice(pack[_ROW_CW:_ROW_CW + CONV], (0, col0), (CONV, D // NCHIP))
    out["lru_wa"] = pack[_ROW_WA:_ROW_WX].reshape(LRU_BLOCKS, LRU_BW, LRU_BW)
    out["lru_wx"] = pack[_ROW_WX:_ROW_LOSS].reshape(LRU_BLOCKS, LRU_BW, LRU_BW)
    return out


_WEIGHTS = ["norm_mix_g", "w_in", "conv_w", "conv_b", "lru_wa", "lru_ba", "lru_wx", "lru_bx", "lru_lambda",
            "forget_b", "w_branch_a", "w_branch_b", "w_out", "norm_mlp_g", "w_up", "w_down", "norm_final_g"]
_BIG = ["w_in", "w_branch_a", "w_branch_b", "w_out", "w_up", "w_down"]


def _halves(a):
    return a.reshape(2, a.shape[0] // 2, a.shape[1])


def _columns_to_shards(a):
    rows, cols = a.shape[0], a.shape[1] // NCHIP
    return jnp.transpose(a.reshape(rows, NCHIP, cols), (1, 0, 2))


def _shards_to_columns(a):
    n, rows, cols = a.shape
    return jnp.transpose(a, (1, 0, 2)).reshape(rows, n * cols)


def kernel(x, norm_mix_g, w_in, conv_w, conv_b, lru_wa, lru_ba, lru_wx, lru_bx, lru_lambda, forget_b, w_branch_a, w_branch_b, w_out, norm_mlp_g, w_up, w_down, norm_final_g, loss_target, m_norm_mix_g, m_w_in, m_conv_w, m_conv_b, m_lru_wa, m_lru_ba, m_lru_wx, m_lru_bx, m_lru_lambda, m_forget_b, m_w_branch_a, m_w_branch_b, m_w_out, m_norm_mlp_g, m_w_up, m_w_down, m_norm_final_g, v_norm_mix_g, v_w_in, v_conv_w, v_conv_b, v_lru_wa, v_lru_ba, v_lru_wx, v_lru_bx, v_lru_lambda, v_forget_b, v_w_branch_a, v_w_branch_b, v_w_out, v_norm_mlp_g, v_w_up, v_w_down, v_norm_final_g):
    args = dict(locals())
    wts = {n: args[n] for n in _WEIGHTS}
    mom = {n: args["m_" + n] for n in _WEIGHTS}
    var = {n: args["v_" + n] for n in _WEIGHTS}
    T = x.shape[1]
    xi, yi, ci = lax.axis_index("x"), lax.axis_index("y"), lax.axis_index("c")
    me = 2 * xi + yi
    c1 = jnp.reshape(ci, (1,)).astype(jnp.int32)
    me1 = jnp.reshape(me, (1,)).astype(jnp.int32)
    col0 = me * (D // NCHIP)

    cw_pad = jnp.pad(conv_w, ((0, 4 * SLAB - CONV), (0, 0)))
    shards = [_halves(wts[n].astype(BF)) for n in _BIG] + [_halves(cw_pad)]
    g_in, g_a, g_b, g_out, g_up, g_down, g_cw = _allgather_shards(shards)
    cin = DIN // NCHIP
    win = _shards_to_columns(g_in.reshape(NCHIP, D, cin))
    w = dict(
        win=jnp.pad(win, ((0, 0), (0, DINP - DIN))),
        wa=g_a.reshape(D, D), wb=g_b.reshape(D, D), wout=g_out.reshape(D, D),
        wup=_shards_to_columns(g_up.reshape(NCHIP, D, D)), wdown=g_down.reshape(FF, D),
        cw=_shards_to_columns(g_cw.reshape(NCHIP, 4 * SLAB, D // NCHIP)[:, :CONV]),
        vec=jnp.concatenate([conv_b[None], lru_ba[None], lru_bx[None], lru_lambda[None],
                             jnp.zeros((SLAB - 4, D), F32)], axis=0),
        fb=jnp.pad(forget_b[None], ((0, 0), (0, DH - NH))),
        wabd=_block_diag(lru_wa).astype(BF), wxbd=_block_diag(lru_wx).astype(BF),
        g_mix=norm_mix_g[None], g_mlp=norm_mlp_g[None], g_fin=norm_final_g[None])

    r = _local_step(x[0], loss_target[0], w, T)

    big = [_columns_to_shards(r["dwin"]), r["dwa"].reshape(NCHIP, D // NCHIP, D), r["dwb"].reshape(NCHIP, D // NCHIP, D),
           r["dwout"].reshape(NCHIP, D // NCHIP, D), _columns_to_shards(r["dwup"]), r["dwdown"].reshape(NCHIP, D, D)]
    big = [b.reshape(NCHIP, 2, b.shape[1] // 2, b.shape[2]) for b in big]
    gsum = dict(zip(_BIG, _reduce_scatter(big, c1, me1)))
    lacc = r["lacc"]
    small = dict(norm_mix_g=r["dg_mix"], norm_mlp_g=r["dg_mlp"], norm_final_g=r["dg_fin"], conv_b=lacc[3],
                 lru_ba=lacc[0], lru_bx=lacc[1], lru_lambda=lacc[2], forget_b=r["dfb"][0, :NH],
                 conv_w=lacc[4:4 + CONV], lru_wa=_block_diag_extract(r["dwabd"]),
                 lru_wx=_block_diag_extract(r["dwxbd"]), loss=r["loss_vec"])
    gpack = _allreduce_small(_pack_small(small, col0))
    loss = jnp.sum(gpack[_ROW_LOSS])

    grads, delta, new_m, new_v = {}, {}, {}, {}
    for n in _BIG:
        grads[n] = gsum[n]
        delta[n], new_m[n], new_v[n] = _adamw(wts[n], gsum[n], mom[n], var[n], "adamw_" + n)
    dp, mp, vp = _adamw(_pack_small(wts, col0), gpack, _pack_small(mom, col0), _pack_small(var, col0), "adamw_small")
    for dst, pack in ((grads, gpack), (delta, dp), (new_m, mp), (new_v, vp)):
        dst.update(_unpack_small(pack, col0))
    return (loss, r["dx"][None], *[grads[n] for n in _WEIGHTS], *[delta[n] for n in _WEIGHTS],
            *[new_m[n] for n in _WEIGHTS], *[new_v[n] for n in _WEIGHTS])
```

```python
import functools
import math

import jax
import jax.numpy as jnp
import numpy as np
from jax import lax
from jax.experimental import pallas as pl
from jax.experimental.pallas import tpu as pltpu

F32 = jnp.float32
BF = jnp.bfloat16

D = 1024
NH = 8
DH = 128
FF = 4096
CONV = 4
LRU_BLOCKS = 16
LRU_BW = 64
BD = 256
NBD = D // BD
LRU_C = 8.0
EPS = 1e-6
DIN = 7176
DINP = 7296
NCHIP = 4
SLAB = 8
VMEM_CAP = 60 * 1024 * 1024

ADAM_LR = 0.001
ADAM_B1 = 0.9
ADAM_B2 = 0.999
ADAM_EPS = 1e-08
ADAM_WD = 0.01
ADAM_STEP = 10

MESH = pl.DeviceIdType.MESH


def _vmem_limit(nbytes):
    return int(min(VMEM_CAP, max(32 * 1024 * 1024, 3 * nbytes)))


def _nbytes(shape, dtype):
    return int(np.prod(shape)) * jnp.dtype(dtype).itemsize


def _sig(x):
    return 1.0 / (1.0 + jnp.exp(-x))


def _log1p(u):
    w = 1.0 + u
    return jnp.where(w == 1.0, u, jnp.log(w) * (u / (w - 1.0)))


def _expm1(z):
    e = jnp.exp(z)
    le = jnp.log(e)
    return jnp.where(e == 1.0, z, jnp.where(le == 0.0, z, (e - 1.0) * (z / le)))


def _softplus(z):
    return jnp.maximum(z, 0.0) + _log1p(jnp.exp(-jnp.abs(z)))


_GELU_C = math.sqrt(2.0 / math.pi)


def _gelu(x):
    return 0.5 * x * (1.0 + jnp.tanh(_GELU_C * (x + 0.044715 * x * x * x)))


def _gelu_grad(x):
    t = jnp.tanh(_GELU_C * (x + 0.044715 * x * x * x))
    return 0.5 * (1.0 + t) + 0.5 * x * (1.0 - t * t) * _GELU_C * (1.0 + 3.0 * 0.044715 * x * x)


def _shift_down(x, d, prev8):
    n = x.shape[0]
    row8 = lax.broadcasted_iota(jnp.int32, (SLAB, x.shape[1]), 0)
    y = pltpu.roll(x, d, 0)
    top = jnp.where(row8 < d, pltpu.roll(prev8, d, 0), y[0:SLAB])
    if n == SLAB:
        return top
    return jnp.concatenate([top, y[SLAB:]], axis=0)


def _shift_up(x, d, next8):
    n = x.shape[0]
    row8 = lax.broadcasted_iota(jnp.int32, (SLAB, x.shape[1]), 0)
    y = pltpu.roll(x, n - d, 0)
    bottom = jnp.where(row8 >= SLAB - d, pltpu.roll(next8, SLAB - d, 0), y[n - SLAB:])
    if n == SLAB:
        return bottom
    return jnp.concatenate([y[:n - SLAB], bottom], axis=0)


def _slab_scan_fwd(a, b):
    row = lax.broadcasted_iota(jnp.int32, a.shape, 0)
    for k in (1, 2, 4):
        a_s = pltpu.roll(a, k, 0)
        b_s = pltpu.roll(b, k, 0)
        m = row >= k
        b = jnp.where(m, a * b_s + b, b)
        a = jnp.where(m, a * a_s, a)
    return a, b


def _slab_scan_bwd(a, b):
    row = lax.broadcasted_iota(jnp.int32, a.shape, 0)
    for k in (1, 2, 4):
        a_s = pltpu.roll(a, SLAB - k, 0)
        b_s = pltpu.roll(b, SLAB - k, 0)
        m = row < SLAB - k
        b = jnp.where(m, a * b_s + b, b)
        a = jnp.where(m, a * a_s, a)
    return a, b


_DIMS = {"nn": (((1,), (0,)), ((), ())), "nt": (((1,), (1,)), ((), ())), "tn": (((0,), (0,)), ((), ()))}


def _dot(a, b, mode="nn"):
    return lax.dot_general(a, b, _DIMS[mode], preferred_element_type=F32)


def _mm(a, b, mode, M, N, K, *, name, out_dtype=F32, tm=1024, tn=1024, tk=1024,
        a_off=(0, 0), b_off=(0, 0), add=None, epi=None, epi_ins=()):
    tm, tn, tk = min(tm, M), min(tn, N), min(tk, K)
    nk = K // tk
    grid = (M // tm, N // tn, nk)
    if mode == "nn":
        a_spec = pl.BlockSpec((tm, tk), lambda i, j, k: (i + a_off[0], k + a_off[1]))
        b_spec = pl.BlockSpec((tk, tn), lambda i, j, k: (k + b_off[0], j + b_off[1]))
    elif mode == "nt":
        a_spec = pl.BlockSpec((tm, tk), lambda i, j, k: (i + a_off[0], k + a_off[1]))
        b_spec = pl.BlockSpec((tn, tk), lambda i, j, k: (j + b_off[0], k + b_off[1]))
    else:
        a_spec = pl.BlockSpec((tk, tm), lambda i, j, k: (k + a_off[0], i + a_off[1]))
        b_spec = pl.BlockSpec((tk, tn), lambda i, j, k: (k + b_off[0], j + b_off[1]))
    o_spec = pl.BlockSpec((tm, tn), lambda i, j, k: (i, j))
    extra = ([add] if add is not None else []) + list(epi_ins)
    n_extra = len(extra)
    has_add = add is not None

    def body(*refs):
        a_ref, b_ref = refs[0], refs[1]
        ex = refs[2:2 + n_extra]
        o_ref = refs[2 + n_extra]

        def finish(acc):
            if has_add:
                acc = acc + ex[0][...].astype(F32)
            if epi is not None:
                acc = epi(acc, *[e[...] for e in ex[(1 if has_add else 0):]])
            o_ref[...] = acc.astype(o_ref.dtype)

        p = _dot(a_ref[...].astype(BF), b_ref[...].astype(BF), mode)
        if nk == 1:
            finish(p)
        else:
            acc_ref = refs[3 + n_extra]
            k = pl.program_id(2)

            @pl.when(k == 0)
            def _():
                acc_ref[...] = p

            @pl.when(k > 0)
            def _():
                acc_ref[...] += p

            @pl.when(k == nk - 1)
            def _():
                finish(acc_ref[...])

    blk = (_nbytes((tm, tk), a.dtype) + _nbytes((tk, tn), b.dtype) + _nbytes((tm, tn), out_dtype)
           + sum(_nbytes((tm, tn), e.dtype) for e in extra) + 2 * _nbytes((tm, tn), F32))
    return pl.pallas_call(
        body, name=name, grid=grid,
        in_specs=[a_spec, b_spec] + [o_spec] * n_extra,
        out_specs=o_spec,
        out_shape=jax.ShapeDtypeStruct((M, N), out_dtype),
        scratch_shapes=[pltpu.VMEM((tm, tn), F32)] if nk > 1 else [],
        compiler_params=pltpu.CompilerParams(
            dimension_semantics=("parallel", "parallel", "arbitrary"), vmem_limit_bytes=_vmem_limit(blk)),
    )(a, b, *extra)


def _ew(fn, T, tm, ins, consts, outs, accs, *, name, reverse=False):
    tm = min(tm, T)
    nt = T // tm
    n_in, n_c, n_o, n_a = len(ins), len(consts), len(outs), len(accs)

    def row(i):
        return nt - 1 - i if reverse else i

    in_specs = [pl.BlockSpec((tm, w), functools.partial(lambda i, cb: (row(i), cb), cb=cb)) for (_, w, cb) in ins]
    in_specs += [pl.BlockSpec(c.shape, functools.partial(lambda i, nd: (0,) * nd, nd=c.ndim)) for c in consts]
    out_specs = [pl.BlockSpec((tm, w), lambda i: (row(i), 0)) for (w, _) in outs]
    out_specs += [pl.BlockSpec((r, w), lambda i: (0, 0)) for (r, w) in accs]
    out_shape = [jax.ShapeDtypeStruct((T, w), dt) for (w, dt) in outs]
    out_shape += [jax.ShapeDtypeStruct((r, w), F32) for (r, w) in accs]

    def body(*refs):
        in_refs = refs[:n_in]
        c_refs = refs[n_in:n_in + n_c]
        o_refs = refs[n_in + n_c:n_in + n_c + n_o]
        a_refs = refs[n_in + n_c + n_o:]
        ov, av = fn([r[...] for r in in_refs], [r[...] for r in c_refs])
        for r, v in zip(o_refs, ov):
            r[...] = v.astype(r.dtype)
        if n_a:
            i = pl.program_id(0)

            @pl.when(i == 0)
            def _():
                for r, v in zip(a_refs, av):
                    r[...] = v

            @pl.when(i > 0)
            def _():
                for r, v in zip(a_refs, av):
                    r[...] += v

    blk = (sum(_nbytes((tm, w), a.dtype) for (a, w, _) in ins) + sum(_nbytes(c.shape, c.dtype) for c in consts)
           + sum(_nbytes((tm, w), dt) for (w, dt) in outs) + sum(_nbytes(s, F32) for s in accs))
    res = pl.pallas_call(
        body, name=name, grid=(nt,), in_specs=in_specs, out_specs=out_specs, out_shape=out_shape,
        compiler_params=pltpu.CompilerParams(
            dimension_semantics=("arbitrary",), vmem_limit_bytes=_vmem_limit(blk)),
    )(*[a for (a, _, _) in ins], *consts)
    return res


def _colsum(v):
    return jnp.sum(v, axis=0, keepdims=True)


def _fgate_fwd(fl, fb, T, tm=512):
    tm = min(tm, T)

    def body(fl_ref, fb_ref, f_ref, carry_ref):
        i = pl.program_id(0)

        @pl.when(i == 0)
        def _():
            carry_ref[...] = jnp.zeros_like(carry_ref)

        row = lax.broadcasted_iota(jnp.int32, (SLAB, DH), 0)

        def slab(s, carry):
            r0 = pl.multiple_of(s * SLAB, SLAB)
            z = fl_ref[pl.ds(r0, SLAB), :] + fb_ref[...]
            c = jnp.minimum(z, 0.0) - _log1p(jnp.exp(-jnp.abs(z)))
            for k in (1, 2, 4):
                c = c + jnp.where(row >= k, pltpu.roll(c, k, 0), 0.0)
            c = c + carry
            f_ref[pl.ds(r0, SLAB), :] = c
            return c[SLAB - 1:SLAB, :]

        carry_ref[0:1, :] = lax.fori_loop(0, tm // SLAB, slab, carry_ref[0:1, :])

    return pl.pallas_call(
        body, name="fgate_fwd", grid=(T // tm,),
        in_specs=[pl.BlockSpec((tm, DH), lambda i: (i, 0)), pl.BlockSpec((1, DH), lambda i: (0, 0))],
        out_specs=pl.BlockSpec((tm, DH), lambda i: (i, 0)),
        out_shape=jax.ShapeDtypeStruct((T, DH), F32),
        scratch_shapes=[pltpu.VMEM((SLAB, DH), F32)],
        compiler_params=pltpu.CompilerParams(dimension_semantics=("arbitrary",)),
    )(fl, fb)


def _fgate_bwd(dF, fl, fb, T, tm=512):
    tm = min(tm, T)
    nt = T // tm

    def body(df_ref, fl_ref, fb_ref, o_ref, acc_ref, carry_ref):
        i = pl.program_id(0)

        @pl.when(i == 0)
        def _():
            carry_ref[...] = jnp.zeros_like(carry_ref)
            acc_ref[...] = jnp.zeros_like(acc_ref)

        row = lax.broadcasted_iota(jnp.int32, (SLAB, DH), 0)

        def slab(n, carry):
            g_next, acc = carry
            r0 = pl.multiple_of((tm // SLAB - 1 - n) * SLAB, SLAB)
            c = df_ref[pl.ds(r0, SLAB), :]
            for k in (1, 2, 4):
                c = c + jnp.where(row < SLAB - k, pltpu.roll(c, SLAB - k, 0), 0.0)
            c = c + g_next
            z = fl_ref[pl.ds(r0, SLAB), :] + fb_ref[...]
            dfl = c * _sig(-z)
            o_ref[pl.ds(r0, SLAB), :] = dfl.astype(o_ref.dtype)
            return c[0:1, :], acc + _colsum(dfl)

        g, acc = lax.fori_loop(0, tm // SLAB, slab, (carry_ref[0:1, :], jnp.zeros((1, DH), F32)))
        carry_ref[0:1, :] = g
        acc_ref[...] += acc

    return pl.pallas_call(
        body, name="fgate_bwd", grid=(nt,),
        in_specs=[pl.BlockSpec((tm, DH), lambda i: (nt - 1 - i, 0)), pl.BlockSpec((tm, DH), lambda i: (nt - 1 - i, 0)),
                  pl.BlockSpec((1, DH), lambda i: (0, 0))],
        out_specs=[pl.BlockSpec((tm, DH), lambda i: (nt - 1 - i, 0)), pl.BlockSpec((1, DH), lambda i: (0, 0))],
        out_shape=[jax.ShapeDtypeStruct((T, DH), BF), jax.ShapeDtypeStruct((1, DH), F32)],
        scratch_shapes=[pltpu.VMEM((SLAB, DH), F32)],
        compiler_params=pltpu.CompilerParams(dimension_semantics=("arbitrary",)),
    )(dF, fl, fb)


def _conv(x, prev8, cw, cb):
    xs = [x] + [_shift_down(x, d, prev8) for d in (1, 2, 3)]
    xa = cb + cw[3:4, :] * xs[0] + cw[2:3, :] * xs[1] + cw[1:2, :] * xs[2] + cw[0:1, :] * xs[3]
    return xa, xs


def _lru_gates(xa_g, wa_g, wx_g, ba_g, bx_g, sp_g):
    xb = xa_g.astype(BF)
    r = _sig(_dot(xb, wa_g) + ba_g)
    ig = _sig(_dot(xb, wx_g) + bx_g)
    la = -LRU_C * r * sp_g
    a = jnp.exp(la)
    mult = jnp.sqrt(-_expm1(2.0 * la))
    return r, ig, a, mult


def _lru_fwd(xg, cw, vec, wabd, wxbd, T, tm=256):
    tm = min(tm, T)
    nsl = tm // SLAB

    def body(x_ref, xp_ref, cw_ref, vec_ref, wa_ref, wx_ref, h_ref, a_s, b_s, carry_ref):
        i = pl.program_id(0)

        @pl.when(i == 0)
        def _():
            carry_ref[...] = jnp.zeros_like(carry_ref)

        x = x_ref[...]
        prev8 = jnp.where(i > 0, xp_ref[...], 0.0)
        vec_v = vec_ref[...]
        xa, _ = _conv(x, prev8, cw_ref[...], vec_v[0:1, :])
        sp = _softplus(-vec_v[3:4, :])
        for g in range(NBD):
            sl = slice(g * BD, (g + 1) * BD)
            _, ig, a, mult = _lru_gates(xa[:, sl], wa_ref[g], wx_ref[g], vec_v[1:2, sl], vec_v[2:3, sl], sp[:, sl])
            a_s[:, sl] = a
            b_s[:, sl] = mult * ig * xa[:, sl]

        def slab(s, carry):
            r0 = pl.multiple_of(s * SLAB, SLAB)
            A, B = _slab_scan_fwd(a_s[pl.ds(r0, SLAB), :], b_s[pl.ds(r0, SLAB), :])
            h = A * carry + B
            h_ref[pl.ds(r0, SLAB), :] = h
            return h[SLAB - 1:SLAB, :]

        carry_ref[0:1, :] = lax.fori_loop(0, nsl, slab, carry_ref[0:1, :])

    blk = 5 * _nbytes((tm, D), F32) + 2 * _nbytes((NBD, BD, BD), BF)
    return pl.pallas_call(
        body, name="lru_fwd", grid=(T // tm,),
        in_specs=[pl.BlockSpec((tm, D), lambda i: (i, 0)),
                  pl.BlockSpec((SLAB, D), lambda i: (jnp.maximum(i * nsl - 1, 0), 0)),
                  pl.BlockSpec((CONV, D), lambda i: (0, 0)),
                  pl.BlockSpec((SLAB, D), lambda i: (0, 0)),
                  pl.BlockSpec((NBD, BD, BD), lambda i: (0, 0, 0)),
                  pl.BlockSpec((NBD, BD, BD), lambda i: (0, 0, 0))],
        out_specs=pl.BlockSpec((tm, D), lambda i: (i, 0)),
        out_shape=jax.ShapeDtypeStruct((T, D), F32),
        scratch_shapes=[pltpu.VMEM((tm, D), F32), pltpu.VMEM((tm, D), F32), pltpu.VMEM((SLAB, D), F32)],
        compiler_params=pltpu.CompilerParams(dimension_semantics=("arbitrary",), vmem_limit_bytes=_vmem_limit(blk)),
    )(xg, xg, cw, vec, wabd, wxbd)


def _lru_bwd(xg, h, dha, cw, vec, wabd, wxbd, T, tm=256):
    tm = min(tm, T)
    nsl = tm // SLAB
    nt = T // tm

    def body(x_ref, xp_ref, h_ref, hp_ref, dh_ref, cw_ref, vec_ref, wa_ref, wx_ref,
             dx_ref, dwa_ref, dwx_ref, acc_ref, a_s, b_s, g_s, dxa_s, carry_ref, dxan_ref):
        n = pl.program_id(0)
        it = nt - 1 - n

        @pl.when(n == 0)
        def _():
            carry_ref[...] = jnp.zeros_like(carry_ref)
            dxan_ref[...] = jnp.zeros_like(dxan_ref)
            dwa_ref[...] = jnp.zeros_like(dwa_ref)
            dwx_ref[...] = jnp.zeros_like(dwx_ref)
            acc_ref[...] = jnp.zeros_like(acc_ref)

        x = x_ref[...]
        prev8 = jnp.where(it > 0, xp_ref[...], 0.0)
        hprev8 = jnp.where(it > 0, hp_ref[...], 0.0)
        vec_v = vec_ref[...]
        cw_v = cw_ref[...]
        xa, xs = _conv(x, prev8, cw_v, vec_v[0:1, :])
        sp = _softplus(-vec_v[3:4, :])
        gates = []
        for g in range(NBD):
            sl = slice(g * BD, (g + 1) * BD)
            r, ig, a, mult = _lru_gates(xa[:, sl], wa_ref[g], wx_ref[g], vec_v[1:2, sl], vec_v[2:3, sl], sp[:, sl])
            gates.append((r, ig, a, mult))
            a_s[:, sl] = a
        a_next = _shift_up(a_s[...], 1, carry_ref[...])
        a_s[...] = a_next
        b_s[...] = dh_ref[...]

        def slab(m, carry):
            r0 = pl.multiple_of((nsl - 1 - m) * SLAB, SLAB)
            A, B = _slab_scan_bwd(a_s[pl.ds(r0, SLAB), :], b_s[pl.ds(r0, SLAB), :])
            gg = A * carry + B
            g_s[pl.ds(r0, SLAB), :] = gg
            return gg[0:1, :]

        g_first = lax.fori_loop(0, nsl, slab, carry_ref[1:2, :])
        gt = g_s[...]
        h_prev = _shift_down(h_ref[...], 1, hprev8)
        dba = []
        dbx = []
        dsp = []
        for g in range(NBD):
            sl = slice(g * BD, (g + 1) * BD)
            r, ig, a, mult = gates[g]
            xa_g = xa[:, sl]
            g_g = gt[:, sl]
            da = g_g * h_prev[:, sl]
            dmult = g_g * ig * xa_g
            di = g_g * mult * xa_g
            dxa_g = g_g * mult * ig
            dla = da * a - dmult * (a * a / mult)
            dr = dla * (-LRU_C) * sp[:, sl]
            dsp.append(_colsum(dla * (-LRU_C) * r))
            dra = (dr * r * (1.0 - r))
            dix = (di * ig * (1.0 - ig))
            dba.append(_colsum(dra))
            dbx.append(_colsum(dix))
            dra_b = dra.astype(BF)
            dix_b = dix.astype(BF)
            xb = xa_g.astype(BF)
            dxa_g = dxa_g + _dot(dra_b, wa_ref[g], "nt") + _dot(dix_b, wx_ref[g], "nt")
            dwa_ref[g] += _dot(xb, dra_b, "tn")
            dwx_ref[g] += _dot(xb, dix_b, "tn")
            dxa_s[:, sl] = dxa_g
        dxa = dxa_s[...]
        nxt = dxan_ref[...]
        dx = (cw_v[3:4, :] * dxa + cw_v[2:3, :] * _shift_up(dxa, 1, nxt)
              + cw_v[1:2, :] * _shift_up(dxa, 2, nxt) + cw_v[0:1, :] * _shift_up(dxa, 3, nxt))
        dx_ref[...] = dx.astype(dx_ref.dtype)
        acc_ref[0:1, :] += jnp.concatenate(dba, axis=1)
        acc_ref[1:2, :] += jnp.concatenate(dbx, axis=1)
        acc_ref[2:3, :] += jnp.concatenate(dsp, axis=1)
        acc_ref[3:4, :] += _colsum(dxa)
        for k in range(CONV):
            acc_ref[4 + k:5 + k, :] += _colsum(dxa * xs[CONV - 1 - k])
        dxan_ref[...] = dxa[0:SLAB, :]
        a_first = jnp.concatenate([gates[g][2][0:1, :] for g in range(NBD)], axis=1)
        carry_ref[0:1, :] = a_first
        carry_ref[1:2, :] = g_first

        @pl.when(n == nt - 1)
        def _():
            acc_ref[2:3, :] = acc_ref[2:3, :] * (-_sig(-vec_v[3:4, :]))

    rowblk = lambda i: (nt - 1 - i, 0)
    prevblk = lambda i: (jnp.maximum((nt - 1 - i) * nsl - 1, 0), 0)
    c2 = lambda i: (0, 0)
    c3 = lambda i: (0, 0, 0)
    blk = 12 * _nbytes((tm, D), F32) + 6 * _nbytes((NBD, BD, BD), F32)
    return pl.pallas_call(
        body, name="lru_bwd", grid=(nt,),
        in_specs=[pl.BlockSpec((tm, D), rowblk), pl.BlockSpec((SLAB, D), prevblk),
                  pl.BlockSpec((tm, D), rowblk), pl.BlockSpec((SLAB, D), prevblk),
                  pl.BlockSpec((tm, D), rowblk),
                  pl.BlockSpec((CONV, D), c2), pl.BlockSpec((SLAB, D), c2),
                  pl.BlockSpec((NBD, BD, BD), c3), pl.BlockSpec((NBD, BD, BD), c3)],
        out_specs=[pl.BlockSpec((tm, D), rowblk), pl.BlockSpec((NBD, BD, BD), c3), pl.BlockSpec((NBD, BD, BD), c3),
                   pl.BlockSpec((16, D), c2)],
        out_shape=[jax.ShapeDtypeStruct((T, D), BF), jax.ShapeDtypeStruct((NBD, BD, BD), F32),
                   jax.ShapeDtypeStruct((NBD, BD, BD), F32), jax.ShapeDtypeStruct((16, D), F32)],
        scratch_shapes=[pltpu.VMEM((tm, D), F32), pltpu.VMEM((tm, D), F32), pltpu.VMEM((tm, D), F32),
                        pltpu.VMEM((tm, D), F32), pltpu.VMEM((SLAB, D), F32), pltpu.VMEM((SLAB, D), F32)],
        compiler_params=pltpu.CompilerParams(dimension_semantics=("arbitrary",), vmem_limit_bytes=_vmem_limit(blk)),
    )(xg, xg, h, h, dha, cw, vec, wabd, wxbd)


_SCALE = 1.0 / math.sqrt(DH)


DA = 2 * DH
_LOG2E = math.log2(math.e)
_C2 = _SCALE * _LOG2E


def _aug_fn(ins, cs):
    q, k, fcum = ins
    g_all = fcum * (1.0 / _SCALE)
    lane = lax.broadcasted_iota(jnp.int32, (q.shape[0], DH), 1)
    qa, ka = [], []
    for hd in range(NH):
        g = g_all[:, hd:hd + 1]
        hi = g.astype(BF).astype(F32)
        mid = (g - hi).astype(BF).astype(F32)
        lo = ((g - hi) - mid).astype(BF).astype(F32)
        qx = jnp.where(lane == 0, hi, jnp.where(lane == 1, mid, jnp.where(lane == 2, lo,
                                                                          jnp.where(lane < 6, 1.0, 0.0))))
        kx = jnp.where(lane < 3, 1.0, jnp.where(lane == 3, -hi, jnp.where(lane == 4, -mid,
                                                                          jnp.where(lane == 5, -lo, 0.0))))
        qa += [q[:, hd * DH:(hd + 1) * DH], qx.astype(BF)]
        ka += [k[:, hd * DH:(hd + 1) * DH], kx.astype(BF)]
    return [jnp.concatenate(qa, axis=1), jnp.concatenate(ka, axis=1)], []


_KA_ONES = DH + 3
KT_ONES = 16
ATTN_CHAINS = 1


def _attn_fwd(qa, ka, vt, T, blk=512):
    blk = min(blk, T)
    nb = T // blk
    bqs = blk // ATTN_CHAINS

    def body(q_ref, k_ref, v_ref, o_ref, lse_ref):
        i = pl.program_id(1)
        qs = [q_ref[pl.ds(c * bqs, bqs), :] for c in range(ATTN_CHAINS)]

        def scores(j, c):
            r0 = pl.multiple_of(j * blk, blk)
            return _dot(k_ref[pl.ds(r0, blk), :], qs[c], "nt") * _C2

        def update(s, vj, carry):
            m, l, acc = carry
            m_new = jnp.maximum(m, jnp.max(s, axis=0, keepdims=True))
            alpha = jnp.exp2(m - m_new)
            p = jnp.exp2(s - m_new)
            l = alpha * l + jnp.sum(p, axis=0, keepdims=True)
            acc = alpha * acc + _dot(vj, p.astype(BF))
            return m_new, l, acc

        def step(j, carry):
            ss, st = carry
            nxt = tuple(scores(j + 1, c) for c in range(ATTN_CHAINS))
            vj = v_ref[j]
            return nxt, tuple(update(ss[c], vj, st[c]) for c in range(ATTN_CHAINS))

        init = tuple((jnp.full((1, bqs), -jnp.inf, F32), jnp.zeros((1, bqs), F32), jnp.zeros((DH, bqs), F32))
                     for _ in range(ATTN_CHAINS))
        first = tuple(scores(0, c) for c in range(ATTN_CHAINS))
        last, carry = lax.fori_loop(0, i, step, (first, init))
        rk = lax.broadcasted_iota(jnp.int32, (blk, bqs), 0)
        cq = lax.broadcasted_iota(jnp.int32, (blk, bqs), 1)
        vi = v_ref[i]
        for c in range(ATTN_CHAINS):
            s = jnp.where(cq + c * bqs >= rk, last[c], -jnp.inf)
            m, l, acc = update(s, vi, carry[c])
            o_ref[:, c * bqs:(c + 1) * bqs] = (acc / l).astype(o_ref.dtype)
            lse_ref[:, c * bqs:(c + 1) * bqs] = m + jnp.log(l) * _LOG2E

    vm = _nbytes((T, DA), BF) + _nbytes((T, DH), BF) + 6 * _nbytes((blk, blk), F32)
    return pl.pallas_call(
        body, name="attn_fwd", grid=(NH, nb),
        in_specs=[pl.BlockSpec((blk, DA), lambda h, i: (i, h)),
                  pl.BlockSpec((T, DA), lambda h, i: (0, h)),
                  pl.BlockSpec((None, nb, DH, blk), lambda h, i: (h, 0, 0, 0))],
        out_specs=[pl.BlockSpec((None, None, DH, blk), lambda h, i: (h, i, 0, 0)),
                   pl.BlockSpec((None, None, 1, blk), lambda h, i: (h, i, 0, 0))],
        out_shape=[jax.ShapeDtypeStruct((NH, nb, DH, blk), BF), jax.ShapeDtypeStruct((NH, nb, 1, blk), F32)],
        compiler_params=pltpu.CompilerParams(dimension_semantics=("parallel", "arbitrary"),
                                             vmem_limit_bytes=_vmem_limit(vm)),
    )(qa, ka, vt)


def _attn_bwd(qa, ka, kt, qkv, do, lrow, drow, T, blk=512):
    blk = min(blk, T)
    nb = T // blk

    def body(ka_ref, kt_ref, v_ref, qa_ref, do_ref, l_ref, d_ref, dq_ref, dk_ref, dv_ref, dfs_ref):
        j = pl.program_id(1)

        @pl.when(j == 0)
        def _():
            dq_ref[...] = jnp.zeros_like(dq_ref)

        row = lax.broadcasted_iota(jnp.int32, (DH + KT_ONES, blk), 0)
        dq_scale = jnp.where(row < DH, _SCALE, 1.0)

        kaj = ka_ref[...]
        ktj = kt_ref[...]
        vj = v_ref[...]

        def step(i, carry, diag):
            dka, dv = carry
            r0 = pl.multiple_of(i * blk, blk)
            qi = qa_ref[pl.ds(r0, blk), :]
            doi = do_ref[pl.ds(r0, blk), :]
            st = _dot(kaj, qi, "nt") * _C2 - l_ref[i]
            if diag:
                rk = lax.broadcasted_iota(jnp.int32, (blk, blk), 0)
                cq = lax.broadcasted_iota(jnp.int32, (blk, blk), 1)
                st = jnp.where(cq >= rk, st, -jnp.inf)
            pt = jnp.exp2(st)
            dv = dv + _dot(pt.astype(BF), doi)
            dpt = _dot(vj, doi, "nt")
            dst = pt * (dpt - d_ref[i])
            dsb = dst.astype(BF)
            dka = dka + _dot(dsb, qi)
            dq_ref[i] += _dot(ktj, dsb) * dq_scale
            return dka, dv

        init = (jnp.zeros((blk, DA), F32), jnp.zeros((blk, DH), F32))
        carry = step(j, init, True)
        dka, dv = lax.fori_loop(j + 1, nb, lambda i, c: step(i, c, False), carry)
        dk_ref[...] = (dka[:, :DH] * _SCALE).astype(dk_ref.dtype)
        dv_ref[...] = dv.astype(dv_ref.dtype)
        dfs_ref[...] = dka[:, _KA_ONES:_KA_ONES + 1]

    rowv = pl.BlockSpec((None, nb, 1, blk), lambda h, j: (h, 0, 0, 0))
    vm = _nbytes((T, DA), BF) + _nbytes((T, DH), BF) + _nbytes((T, DH), F32) + 8 * _nbytes((blk, blk), F32)
    return pl.pallas_call(
        body, name="attn_bwd", grid=(NH, nb),
        in_specs=[pl.BlockSpec((blk, DA), lambda h, j: (j, h)),
                  pl.BlockSpec((None, None, DH + KT_ONES, blk), lambda h, j: (h, j, 0, 0)),
                  pl.BlockSpec((blk, DH), lambda h, j: (j, 2 * NH + h)),
                  pl.BlockSpec((T, DA), lambda h, j: (0, h)),
                  pl.BlockSpec((T, DH), lambda h, j: (0, h)),
                  rowv, rowv],
        out_specs=[pl.BlockSpec((None, nb, DH + KT_ONES, blk), lambda h, j: (h, 0, 0, 0)),
                   pl.BlockSpec((blk, DH), lambda h, j: (j, h)),
                   pl.BlockSpec((blk, DH), lambda h, j: (j, h)),
                   pl.BlockSpec((None, blk, 1), lambda h, j: (h, j, 0))],
        out_shape=[jax.ShapeDtypeStruct((NH, nb, DH + KT_ONES, blk), F32), jax.ShapeDtypeStruct((T, D), BF),
                   jax.ShapeDtypeStruct((T, D), BF), jax.ShapeDtypeStruct((NH, T, 1), F32)],
        compiler_params=pltpu.CompilerParams(dimension_semantics=("parallel", "arbitrary"),
                                             vmem_limit_bytes=_vmem_limit(vm)),
    )(ka, kt, qkv, qa, do, lrow, drow)


def _norm_fn(ins, cs):
    x, = ins
    g, = cs
    r = lax.rsqrt(jnp.mean(x * x, axis=-1, keepdims=True) + EPS)
    return [x * r * g], []


def _norm_bwd_fn(ins, cs):
    x, dy, dres = ins
    g, = cs
    r = lax.rsqrt(jnp.mean(x * x, axis=-1, keepdims=True) + EPS)
    xh = x * r
    dxh = dy * g
    dx = dres + r * (dxh - xh * jnp.mean(dxh * xh, axis=-1, keepdims=True))
    return [dx], [_colsum(dy * xh)]


def _final_fn(ins, cs):
    x2, tgt = ins
    g, = cs
    r = lax.rsqrt(jnp.mean(x2 * x2, axis=-1, keepdims=True) + EPS)
    xh = x2 * r
    e = xh * g - tgt
    dy = e * (1.0 / D)
    dxh = dy * g
    dx2 = r * (dxh - xh * jnp.mean(dxh * xh, axis=-1, keepdims=True))
    return [dx2], [_colsum(0.5 * e * e * (1.0 / D)), _colsum(dy * xh)]


def _z_fn(ins, cs):
    g, h = ins
    return [_gelu(g) * h], []


def _mix_fn(ins, cs):
    gates, ya, yb = ins
    return [_sig(gates[:, :D]) * ya + _sig(gates[:, D:]) * yb], []


def _relu2_fn(ins, cs):
    hp, = ins
    r = jnp.maximum(hp, 0.0)
    return [r * r], []


def _mix_bwd_fn(ins, cs):
    dmix, gates, ya, yb = ins
    ga = _sig(gates[:, :D])
    gb = _sig(gates[:, D:])
    dgates = jnp.concatenate([dmix * ya * ga * (1.0 - ga), dmix * yb * gb * (1.0 - gb)], axis=1)
    return [dmix * ga, dmix * gb, dgates], []


def _z_bwd_fn(ins, cs):
    dz, g, h = ins
    return [dz * _gelu(g), dz * h * _gelu_grad(g)], []


def _delta_fn(ins, cs):
    do, o = ins
    p = do.astype(F32) * o.astype(F32)
    lane = lax.broadcasted_iota(jnp.int32, (p.shape[0], DH), 1)
    out = jnp.zeros((p.shape[0], DH), F32)
    for hd in range(NH):
        s = jnp.sum(p[:, hd * DH:(hd + 1) * DH], axis=1, keepdims=True)
        out = jnp.where(lane == hd, s, out)
    return [out], []


def _local_step(x, tgt, w, T, blk=512):
    blk = min(blk, T)
    nb = T // blk
    win = w["win"]

    u, = _ew(_norm_fn, T, 512, [(x, D, 0)], [w["g_mix"]], [(D, BF)], [], name="norm_mix")
    xg = _mm(u, win, "nn", T, 2 * D, D, name="proj_lru")
    qkv = _mm(u, win, "nn", T, 3 * D, D, name="proj_qkv", out_dtype=BF, b_off=(0, 2))
    gates = _mm(u, win, "nn", T, 2 * D, D, name="proj_gates", b_off=(0, 5))
    fl = _mm(u, win, "nn", T, DH, D, name="proj_f", tn=DH, b_off=(0, 7 * D // DH))
    fcum = _fgate_fwd(fl, w["fb"], T)
    qa, ka = _ew(_aug_fn, T, 256, [(qkv, D, 0), (qkv, D, 1), (fcum, DH, 0)], [],
                 [(NH * DA, BF), (NH * DA, BF)], [], name="attn_augment")

    def heads_t(a):
        return a.reshape(nb, blk, NH, DH).transpose(2, 0, 3, 1)

    def heads_back(a):
        return a.transpose(1, 3, 0, 2).reshape(T, D)

    kt = heads_t(qkv[:, D:2 * D])
    vt = heads_t(qkv[:, 2 * D:])
    h = _lru_fwd(xg, w["cw"], w["vec"], w["wabd"], w["wxbd"], T)
    ot, lse = _attn_fwd(qa, ka, vt, T, blk)
    ob = heads_back(ot)
    z, = _ew(_z_fn, T, 512, [(xg, D, 1), (h, D, 0)], [], [(D, BF)], [], name="lru_gelu")
    ya = _mm(z, w["wa"], "nn", T, D, D, name="branch_a")
    yb = _mm(ob, w["wb"], "nn", T, D, D, name="branch_b")
    mix, = _ew(_mix_fn, T, 256, [(gates, 2 * D, 0), (ya, D, 0), (yb, D, 0)], [], [(D, BF)], [], name="mix")
    x1 = _mm(mix, w["wout"], "nn", T, D, D, name="out_proj", add=x)
    m, = _ew(_norm_fn, T, 512, [(x1, D, 0)], [w["g_mlp"]], [(D, BF)], [], name="norm_mlp")
    hpre = _mm(m, w["wup"], "nn", T, FF, D, name="mlp_up")
    hh, = _ew(_relu2_fn, T, 256, [(hpre, FF, 0)], [], [(FF, BF)], [], name="relu2")
    x2 = _mm(hh, w["wdown"], "nn", T, D, FF, name="mlp_down", add=x1)
    dx2, loss_vec, dg_fin = _ew(_final_fn, T, 256, [(x2, D, 0), (tgt, D, 0)], [w["g_fin"]], [(D, F32)],
                                [(1, D), (1, D)], name="final_norm_loss")

    dhpre = _mm(dx2, w["wdown"], "nt", T, FF, D, name="mlp_down_bwd", out_dtype=BF,
                epi=lambda acc, hp: acc * (2.0 * jnp.maximum(hp, 0.0)), epi_ins=[hpre])
    dwdown = _mm(hh, dx2, "tn", FF, D, T, name="dw_down", tk=512)
    dwup = _mm(m, dhpre, "tn", D, FF, T, name="dw_up", tk=512)
    dm = _mm(dhpre, w["wup"], "nt", T, D, FF, name="mlp_up_bwd")
    dx1, dg_mlp = _ew(_norm_bwd_fn, T, 256, [(x1, D, 0), (dm, D, 0), (dx2, D, 0)], [w["g_mlp"]], [(D, F32)],
                      [(1, D)], name="norm_mlp_bwd")

    dmix = _mm(dx1, w["wout"], "nt", T, D, D, name="out_proj_bwd")
    dwout = _mm(mix, dx1, "tn", D, D, T, name="dw_out", tk=512)
    dya, dyb, dgates = _ew(_mix_bwd_fn, T, 256, [(dmix, D, 0), (gates, 2 * D, 0), (ya, D, 0), (yb, D, 0)], [],
                           [(D, BF), (D, BF), (2 * D, BF)], [], name="mix_bwd")
    dob = _mm(dyb, w["wb"], "nt", T, D, D, name="branch_b_bwd", out_dtype=BF)
    dwb = _mm(ob, dyb, "tn", D, D, T, name="dw_b", tk=512)
    dz = _mm(dya, w["wa"], "nt", T, D, D, name="branch_a_bwd")
    dwa = _mm(z, dya, "tn", D, D, T, name="dw_a", tk=512)
    dha, dglru = _ew(_z_bwd_fn, T, 256, [(dz, D, 0), (xg, D, 1), (h, D, 0)], [], [(D, F32), (D, BF)], [],
                     name="lru_gelu_bwd")

    delta, = _ew(_delta_fn, T, 512, [(dob, D, 0), (ob, D, 0)], [], [(DH, F32)], [], name="attn_delta")
    drow = delta[:, :NH].T.reshape(NH, nb, 1, blk)
    kt1 = jnp.concatenate([kt, jnp.ones((NH, nb, KT_ONES, blk), BF)], axis=2)
    dqt, dk, dv, dfs = _attn_bwd(qa, ka, kt1, qkv, dob, lse, drow, T, blk)
    dq = heads_back(dqt[:, :, :DH])
    dfcum = jnp.pad((dqt[:, :, DH].reshape(NH, T) - dfs[:, :, 0]).T, ((0, 0), (0, DH - NH)))
    dfl, dfb = _fgate_bwd(dfcum, fl, w["fb"], T)

    dxl, dwabd, dwxbd, lacc = _lru_bwd(xg, h, dha, w["cw"], w["vec"], w["wabd"], w["wxbd"], T)

    du = _mm(dxl, win, "nt", T, D, D, name="du_lru")
    du = _mm(dglru, win, "nt", T, D, D, name="du_glru", b_off=(0, 1), add=du)
    du = _mm(dq, win, "nt", T, D, D, name="du_q", b_off=(0, 2), add=du)
    du = _mm(dk, win, "nt", T, D, D, name="du_k", b_off=(0, 3), add=du)
    du = _mm(dv, win, "nt", T, D, D, name="du_v", b_off=(0, 4), add=du)
    du = _mm(dgates, win, "nt", T, D, 2 * D, name="du_gates", b_off=(0, 5), add=du)
    du = _mm(dfl, win, "nt", T, D, DH, name="du_f", tk=DH, b_off=(0, 7 * D // DH), add=du)
    pieces = [_mm(u, p, "tn", D, p.shape[1], T, name="dw_in_%d" % n, tk=512)
              for n, p in enumerate((dxl, dglru, dq, dk, dv, dgates))]
    pieces.append(_mm(u, dfl, "tn", D, DH, T, name="dw_in_f", tk=512)[:, :NH])
    dwin = jnp.concatenate(pieces, axis=1)
    dx, dg_mix = _ew(_norm_bwd_fn, T, 256, [(x, D, 0), (du, D, 0), (dx1, D, 0)], [w["g_mix"]], [(D, F32)],
                     [(1, D)], name="norm_mix_bwd")

    return dict(dx=dx, dwin=dwin, dwa=dwa, dwb=dwb, dwout=dwout, dwup=dwup, dwdown=dwdown,
                dwabd=dwabd, dwxbd=dwxbd, lacc=lacc, dfb=dfb, dg_mix=dg_mix, dg_mlp=dg_mlp, dg_fin=dg_fin,
                loss_vec=loss_vec)


def _block_diag(w):
    per = BD // LRU_BW
    w4 = w.reshape(NBD, per, LRU_BW, LRU_BW)
    out = jnp.zeros((NBD, per, LRU_BW, per, LRU_BW), w.dtype)
    for b in range(per):
        out = out.at[:, b, :, b, :].set(w4[:, b])
    return out.reshape(NBD, BD, BD)


def _block_diag_extract(wbd):
    per = BD // LRU_BW
    w5 = wbd.reshape(NBD, per, LRU_BW, per, LRU_BW)
    return jnp.stack([w5[:, b, :, b, :] for b in range(per)], axis=1).reshape(LRU_BLOCKS, LRU_BW, LRU_BW)


_ANY = pl.BlockSpec(memory_space=pl.ANY)


def _place():
    x, y, c = lax.axis_index("x"), lax.axis_index("y"), lax.axis_index("c")
    chips = [(1 - x, y), (x, 1 - y), (1 - x, 1 - y)]
    return x, y, c, chips


def _allgather_shards(shards):
    n = len(shards)

    def body(*refs):
        ins, outs = refs[:n], refs[n:2 * n]
        send_sems, recv_sems, local_sems = refs[2 * n:]
        x, y, c, chips = _place()
        me = 2 * x + y
        sibling = (x, y, 1 - c)

        def remote(p, k, src, dst, to):
            return pltpu.make_async_remote_copy(src_ref=src, dst_ref=dst, send_sem=send_sems.at[p, k],
                                                recv_sem=recv_sems.at[p, k], device_id=to, device_id_type=MESH)

        started = []
        for p in range(n):
            mine = pltpu.make_async_copy(ins[p], outs[p].at[me], local_sems.at[p])
            mine.start()
            started.append(mine)
        sent = []
        for p in range(n):
            for k, chip in enumerate(chips):
                cp = remote(p, k, ins[p].at[c], outs[p].at[me, c], (chip[0], chip[1], c))
                cp.start()
                sent.append(cp)
        for p in range(n):
            for k, chip in enumerate(chips):
                half = outs[p].at[2 * chip[0] + chip[1], c]
                remote(p, k, half, half, sibling).wait_recv()
                fwd = remote(p, 3 + k, half, half, sibling)
                fwd.start()
                sent.append(fwd)
        for p in range(n):
            for k, chip in enumerate(chips):
                half = outs[p].at[2 * chip[0] + chip[1], 1 - c]
                remote(p, 3 + k, half, half, sibling).wait_recv()
        for cp in sent:
            cp.wait_send()
        for cp in started:
            cp.wait()

    return pl.pallas_call(
        body, name="allgather_weights",
        in_specs=[_ANY] * n, out_specs=[_ANY] * n,
        out_shape=[jax.ShapeDtypeStruct((NCHIP,) + s.shape, s.dtype) for s in shards],
        scratch_shapes=[pltpu.SemaphoreType.DMA((n, 6)), pltpu.SemaphoreType.DMA((n, 6)),
                        pltpu.SemaphoreType.DMA((n,))],
    )(*shards)


def _exchange_sibling(grads):
    n = len(grads)

    def body(*refs):
        ins, outs = refs[:n], refs[n:2 * n]
        send_sems, recv_sems = refs[2 * n:]
        x, y, c, _ = _place()
        sibling = (x, y, 1 - c)
        copies = []
        for p in range(n):
            for j in range(NCHIP):
                cp = pltpu.make_async_remote_copy(src_ref=ins[p].at[j, 1 - c], dst_ref=outs[p].at[j],
                                                  send_sem=send_sems.at[p, j], recv_sem=recv_sems.at[p, j],
                                                  device_id=sibling, device_id_type=MESH)
                cp.start()
                copies.append(cp)
        for cp in copies:
            cp.wait()

    return pl.pallas_call(
        body, name="reduce_sibling_exchange",
        in_specs=[_ANY] * n, out_specs=[_ANY] * n,
        out_shape=[jax.ShapeDtypeStruct((NCHIP,) + g.shape[2:], g.dtype) for g in grads],
        scratch_shapes=[pltpu.SemaphoreType.DMA((n, NCHIP)), pltpu.SemaphoreType.DMA((n, NCHIP))],
    )(*grads)


def _exchange_chips(sums):
    n = len(sums)

    def body(*refs):
        ins, outs = refs[:n], refs[n:2 * n]
        send_sems, recv_sems = refs[2 * n:]
        x, y, c, chips = _place()
        copies = []
        for p in range(n):
            for k, chip in enumerate(chips):
                cp = pltpu.make_async_remote_copy(src_ref=ins[p].at[2 * chip[0] + chip[1]], dst_ref=outs[p].at[k],
                                                  send_sem=send_sems.at[p, k], recv_sem=recv_sems.at[p, k],
                                                  device_id=(chip[0], chip[1], c), device_id_type=MESH)
                cp.start()
                copies.append(cp)
        for cp in copies:
            cp.wait()

    return pl.pallas_call(
        body, name="reduce_chip_exchange",
        in_specs=[_ANY] * n, out_specs=[_ANY] * n,
        out_shape=[jax.ShapeDtypeStruct((3,) + s.shape[1:], s.dtype) for s in sums],
        scratch_shapes=[pltpu.SemaphoreType.DMA((n, 3)), pltpu.SemaphoreType.DMA((n, 3))],
    )(*sums)


def _share_halves(halves):
    n = len(halves)

    def body(*refs):
        ins, outs = refs[:n], refs[n:2 * n]
        send_sems, recv_sems, local_sems = refs[2 * n:]
        x, y, c, _ = _place()
        sibling = (x, y, 1 - c)
        copies = []
        for p in range(n):
            mine = pltpu.make_async_copy(ins[p], outs[p].at[c], local_sems.at[p])
            mine.start()
            cp = pltpu.make_async_remote_copy(src_ref=ins[p], dst_ref=outs[p].at[c], send_sem=send_sems.at[p],
                                              recv_sem=recv_sems.at[p], device_id=sibling, device_id_type=MESH)
            cp.start()
            copies += [mine, cp]
        for cp in copies:
            cp.wait()

    return pl.pallas_call(
        body, name="reduce_share_halves",
        in_specs=[_ANY] * n, out_specs=[_ANY] * n,
        out_shape=[jax.ShapeDtypeStruct((2,) + h.shape, h.dtype) for h in halves],
        scratch_shapes=[pltpu.SemaphoreType.DMA((n,)), pltpu.SemaphoreType.DMA((n,)), pltpu.SemaphoreType.DMA((n,))],
    )(*halves)


def _row_tile(half, cols):
    th = max(SLAB, min(half, (1 << 18) // cols // SLAB * SLAB))
    while half % th:
        th -= SLAB
    return th


def _add_sibling(g, r, c, name):
    _, _, half, cols = g.shape
    th = _row_tile(half, cols)

    def body(c_ref, g_ref, r_ref, o_ref):
        o_ref[...] = (g_ref[...] + r_ref[...]).astype(o_ref.dtype)

    return pl.pallas_call(
        body, name=name,
        grid_spec=pltpu.PrefetchScalarGridSpec(
            num_scalar_prefetch=1, grid=(NCHIP, half // th),
            in_specs=[pl.BlockSpec((None, None, th, cols), lambda j, i, c_ref: (j, c_ref[0], i, 0)),
                      pl.BlockSpec((None, th, cols), lambda j, i, c_ref: (j, i, 0))],
            out_specs=pl.BlockSpec((None, th, cols), lambda j, i, c_ref: (j, i, 0))),
        out_shape=jax.ShapeDtypeStruct((NCHIP, half, cols), BF),
    )(c, g, r)


def _add_chips(s, rb, me, name):
    _, half, cols = s.shape
    th = _row_tile(half, cols)

    def body(me_ref, s_ref, rb_ref, o_ref):
        o_ref[...] = (((s_ref[...].astype(F32) + rb_ref[0].astype(F32)) + rb_ref[1].astype(F32))
                      + rb_ref[2].astype(F32))

    return pl.pallas_call(
        body, name=name,
        grid_spec=pltpu.PrefetchScalarGridSpec(
            num_scalar_prefetch=1, grid=(half // th,),
            in_specs=[pl.BlockSpec((None, th, cols), lambda i, me_ref: (me_ref[0], i, 0)),
                      pl.BlockSpec((3, th, cols), lambda i, me_ref: (0, i, 0))],
            out_specs=pl.BlockSpec((th, cols), lambda i, me_ref: (i, 0))),
        out_shape=jax.ShapeDtypeStruct((half, cols), F32),
    )(me, s, rb)


def _reduce_scatter(grads, c1, me1):
    names = ["w_in", "w_a", "w_b", "w_out", "w_up", "w_down"]
    from_sibling = _exchange_sibling(grads)
    sums = [_add_sibling(g, r, c1, "add_sibling_" + nm) for g, r, nm in zip(grads, from_sibling, names)]
    from_chips = _exchange_chips(sums)
    halves = [_add_chips(s, rb, me1, "add_chips_" + nm) for s, rb, nm in zip(sums, from_chips, names)]
    full = _share_halves(halves)
    return [f.reshape(2 * f.shape[1], f.shape[2]) for f in full]


N_DEV = 8
SMALL_ROWS = 208


def _allreduce_small(pack):
    def body(x_ref, out_ref, gbuf, send_sems, recv_sems, local_sem):
        x, y, c, chips = _place()
        me, sibling = (x, y, c), (x, y, 1 - c)

        def rows(px, py, pc):
            return gbuf.at[4 * px + 2 * py + pc]

        def copy(k, block, to, src=None):
            return pltpu.make_async_remote_copy(
                src_ref=rows(*block) if src is None else src, dst_ref=rows(*block),
                send_sem=send_sems.at[k], recv_sem=recv_sems.at[k], device_id=to, device_id_type=MESH)

        mine = pltpu.make_async_copy(x_ref, rows(*me), local_sem)
        mine.start()
        first = [copy(0, me, sibling, src=x_ref)]
        first += [copy(1 + j, me, (chip[0], chip[1], c), src=x_ref) for j, chip in enumerate(chips)]
        for cp in first:
            cp.start()
        passed = [copy(4 + j, (chip[0], chip[1], c), sibling) for j, chip in enumerate(chips)]
        for j, chip in enumerate(chips):
            copy(1 + j, (chip[0], chip[1], c), me).wait_recv()
            passed[j].start()
        copy(0, sibling, me).wait_recv()
        for j, chip in enumerate(chips):
            copy(4 + j, (chip[0], chip[1], 1 - c), me).wait_recv()
        for cp in first + passed:
            cp.wait_send()
        mine.wait()
        acc = gbuf[0]
        for d in range(1, N_DEV):
            acc = acc + gbuf[d]
        out_ref[...] = acc

    return pl.pallas_call(
        body, name="allreduce_small",
        in_specs=[pl.BlockSpec(memory_space=pltpu.VMEM)],
        out_specs=pl.BlockSpec(memory_space=pltpu.VMEM),
        out_shape=jax.ShapeDtypeStruct((SMALL_ROWS, D), F32),
        scratch_shapes=[pltpu.VMEM((N_DEV, SMALL_ROWS, D), F32), pltpu.SemaphoreType.DMA((7,)),
                        pltpu.SemaphoreType.DMA((7,)), pltpu.SemaphoreType.DMA],
    )(pack)


def _adamw(w, g, m, v, name):
    rows, cols = w.shape
    th = _row_tile(rows, cols)

    def body(w_ref, g_ref, m_ref, v_ref, d_ref, mo_ref, vo_ref):
        gv = g_ref[...]
        mn = ADAM_B1 * m_ref[...] + (1.0 - ADAM_B1) * gv
        vn = ADAM_B2 * v_ref[...] + (1.0 - ADAM_B2) * (gv * gv)
        m_hat = mn / (1.0 - ADAM_B1 ** ADAM_STEP)
        v_hat = vn / (1.0 - ADAM_B2 ** ADAM_STEP)
        d_ref[...] = -ADAM_LR * (m_hat / (jnp.sqrt(v_hat) + ADAM_EPS) + ADAM_WD * w_ref[...])
        mo_ref[...] = mn
        vo_ref[...] = vn

    spec = pl.BlockSpec((th, cols), lambda i: (i, 0))
    return pl.pallas_call(
        body, name=name, grid=(rows // th,),
        in_specs=[spec] * 4, out_specs=[spec] * 3,
        out_shape=[jax.ShapeDtypeStruct((rows, cols), F32)] * 3,
        compiler_params=pltpu.CompilerParams(dimension_semantics=("parallel",)),
    )(w, g, m, v)


_SMALL = ["norm_mix_g", "norm_mlp_g", "norm_final_g", "conv_b", "lru_ba", "lru_bx", "lru_lambda"]
_ROW_FB, _ROW_CW, _ROW_WA, _ROW_WX, _ROW_LOSS = 56, 64, 72, 136, 200


def _pack_small(vals, col0):
    def slab(a):
        return jnp.pad(a, ((0, -a.shape[0] % SLAB), (0, D - a.shape[1])))

    rows = [slab(vals[n].reshape(1, D)) for n in _SMALL]
    rows.append(slab(vals["forget_b"].reshape(1, NH)))
    if vals["conv_w"].shape[1] == D:
        rows.append(slab(vals["conv_w"]))
    else:
        rows.append(slab(lax.dynamic_update_slice(jnp.zeros((CONV, D), F32), vals["conv_w"], (0, col0))))
    rows.append(vals["lru_wa"].reshape(LRU_BLOCKS * LRU_BW * LRU_BW // D, D))
    rows.append(vals["lru_wx"].reshape(LRU_BLOCKS * LRU_BW * LRU_BW // D, D))
    rows.append(slab(vals["loss"]) if "loss" in vals else jnp.zeros((SLAB, D), F32))
    return jnp.concatenate(rows, axis=0)


def _unpack_small(pack, col0):
    out = {n: pack[SLAB * i] for i, n in enumerate(_SMALL)}
    out["forget_b"] = pack[_ROW_FB, :NH]
    out["conv_w"] = lax.dynamic_slice(pack[_ROW_CW:_ROW_CW + CONV], (0, col0), (CONV, D // NCHIP))
    out["lru_wa"] = pack[_ROW_WA:_ROW_WX].reshape(LRU_BLOCKS, LRU_BW, LRU_BW)
    out["lru_wx"] = pack[_ROW_WX:_ROW_LOSS].reshape(LRU_BLOCKS, LRU_BW, LRU_BW)
    return out


_WEIGHTS = ["norm_mix_g", "w_in", "conv_w", "conv_b", "lru_wa", "lru_ba", "lru_wx", "lru_bx", "lru_lambda",
            "forget_b", "w_branch_a", "w_branch_b", "w_out", "norm_mlp_g", "w_up", "w_down", "norm_final_g"]
_BIG = ["w_in", "w_branch_a", "w_branch_b", "w_out", "w_up", "w_down"]


def _halves(a):
    return a.reshape(2, a.shape[0] // 2, a.shape[1])


def _columns_to_shards(a):
    rows, cols = a.shape[0], a.shape[1] // NCHIP
    return jnp.transpose(a.reshape(rows, NCHIP, cols), (1, 0, 2))


def _shards_to_columns(a):
    n, rows, cols = a.shape
    return jnp.transpose(a, (1, 0, 2)).reshape(rows, n * cols)


def kernel(x, norm_mix_g, w_in, conv_w, conv_b, lru_wa, lru_ba, lru_wx, lru_bx, lru_lambda, forget_b, w_branch_a, w_branch_b, w_out, norm_mlp_g, w_up, w_down, norm_final_g, loss_target, m_norm_mix_g, m_w_in, m_conv_w, m_conv_b, m_lru_wa, m_lru_ba, m_lru_wx, m_lru_bx, m_lru_lambda, m_forget_b, m_w_branch_a, m_w_branch_b, m_w_out, m_norm_mlp_g, m_w_up, m_w_down, m_norm_final_g, v_norm_mix_g, v_w_in, v_conv_w, v_conv_b, v_lru_wa, v_lru_ba, v_lru_wx, v_lru_bx, v_lru_lambda, v_forget_b, v_w_branch_a, v_w_branch_b, v_w_out, v_norm_mlp_g, v_w_up, v_w_down, v_norm_final_g):
    args = dict(locals())
    wts = {n: args[n] for n in _WEIGHTS}
    mom = {n: args["m_" + n] for n in _WEIGHTS}
    var = {n: args["v_" + n] for n in _WEIGHTS}
    T = x.shape[1]
    xi, yi, ci = lax.axis_index("x"), lax.axis_index("y"), lax.axis_index("c")
    me = 2 * xi + yi
    c1 = jnp.reshape(ci, (1,)).astype(jnp.int32)
    me1 = jnp.reshape(me, (1,)).astype(jnp.int32)
    col0 = me * (D // NCHIP)

    cw_pad = jnp.pad(conv_w, ((0, 4 * SLAB - CONV), (0, 0)))
    shards = [_halves(wts[n].astype(BF)) for n in _BIG] + [_halves(cw_pad)]
    g_in, g_a, g_b, g_out, g_up, g_down, g_cw = _allgather_shards(shards)
    cin = DIN // NCHIP
    win = _shards_to_columns(g_in.reshape(NCHIP, D, cin))
    w = dict(
        win=jnp.pad(win, ((0, 0), (0, DINP - DIN))),
        wa=g_a.reshape(D, D), wb=g_b.reshape(D, D), wout=g_out.reshape(D, D),
        wup=_shards_to_columns(g_up.reshape(NCHIP, D, D)), wdown=g_down.reshape(FF, D),
        cw=_shards_to_columns(g_cw.reshape(NCHIP, 4 * SLAB, D // NCHIP)[:, :CONV]),
        vec=jnp.concatenate([conv_b[None], lru_ba[None], lru_bx[None], lru_lambda[None],
                             jnp.zeros((SLAB - 4, D), F32)], axis=0),
        fb=jnp.pad(forget_b[None], ((0, 0), (0, DH - NH))),
        wabd=_block_diag(lru_wa).astype(BF), wxbd=_block_diag(lru_wx).astype(BF),
        g_mix=norm_mix_g[None], g_mlp=norm_mlp_g[None], g_fin=norm_final_g[None])

    r = _local_step(x[0], loss_target[0], w, T)

    big = [_columns_to_shards(r["dwin"]), r["dwa"].reshape(NCHIP, D // NCHIP, D), r["dwb"].reshape(NCHIP, D // NCHIP, D),
           r["dwout"].reshape(NCHIP, D // NCHIP, D), _columns_to_shards(r["dwup"]), r["dwdown"].reshape(NCHIP, D, D)]
    big = [b.reshape(NCHIP, 2, b.shape[1] // 2, b.shape[2]) for b in big]
    gsum = dict(zip(_BIG, _reduce_scatter(big, c1, me1)))
    lacc = r["lacc"]
    small = dict(norm_mix_g=r["dg_mix"], norm_mlp_g=r["dg_mlp"], norm_final_g=r["dg_fin"], conv_b=lacc[3],
                 lru_ba=lacc[0], lru_bx=lacc[1], lru_lambda=lacc[2], forget_b=r["dfb"][0, :NH],
                 conv_w=lacc[4:4 + CONV], lru_wa=_block_diag_extract(r["dwabd"]),
                 lru_wx=_block_diag_extract(r["dwxbd"]), loss=r["loss_vec"])
    gpack = _allreduce_small(_pack_small(small, col0))
    loss = jnp.sum(gpack[_ROW_LOSS])

    grads, delta, new_m, new_v = {}, {}, {}, {}
    for n in _BIG:
        grads[n] = gsum[n]
        delta[n], new_m[n], new_v[n] = _adamw(wts[n], gsum[n], mom[n], var[n], "adamw_" + n)
    dp, mp, vp = _adamw(_pack_small(wts, col0), gpack, _pack_small(mom, col0), _pack_small(var, col0), "adamw_small")
    for dst, pack in ((grads, gpack), (delta, dp), (new_m, mp), (new_v, vp)):
        dst.update(_unpack_small(pack, col0))
    return (loss, r["dx"][None], *[grads[n] for n in _WEIGHTS], *[delta[n] for n in _WEIGHTS],
            *[new_m[n] for n in _WEIGHTS], *[new_v[n] for n in _WEIGHTS])
```

```python
import functools
import math

import jax
import jax.numpy as jnp
import numpy as np
from jax import lax
from jax.experimental import pallas as pl
from jax.experimental.pallas import tpu as pltpu

F32 = jnp.float32
BF = jnp.bfloat16

D = 1024
NH = 8
DH = 128
FF = 4096
CONV = 4
LRU_BLOCKS = 16
LRU_BW = 64
BD = 256
NBD = D // BD
LRU_C = 8.0
EPS = 1e-6
DIN = 7176
DINP = 7296
NCHIP = 4
SLAB = 8
VMEM_CAP = 60 * 1024 * 1024

ADAM_LR = 0.001
ADAM_B1 = 0.9
ADAM_B2 = 0.999
ADAM_EPS = 1e-08
ADAM_WD = 0.01
ADAM_STEP = 10

MESH = pl.DeviceIdType.MESH


def _vmem_limit(nbytes):
    return int(min(VMEM_CAP, max(32 * 1024 * 1024, 3 * nbytes)))


def _nbytes(shape, dtype):
    return int(np.prod(shape)) * jnp.dtype(dtype).itemsize


def _sig(x):
    return 1.0 / (1.0 + jnp.exp(-x))


def _log1p(u):
    w = 1.0 + u
    return jnp.where(w == 1.0, u, jnp.log(w) * (u / (w - 1.0)))


def _expm1(z):
    e = jnp.exp(z)
    le = jnp.log(e)
    return jnp.where(e == 1.0, z, jnp.where(le == 0.0, z, (e - 1.0) * (z / le)))


def _softplus(z):
    return jnp.maximum(z, 0.0) + _log1p(jnp.exp(-jnp.abs(z)))


_GELU_C = math.sqrt(2.0 / math.pi)


def _gelu(x):
    return 0.5 * x * (1.0 + jnp.tanh(_GELU_C * (x + 0.044715 * x * x * x)))


def _gelu_grad(x):
    t = jnp.tanh(_GELU_C * (x + 0.044715 * x * x * x))
    return 0.5 * (1.0 + t) + 0.5 * x * (1.0 - t * t) * _GELU_C * (1.0 + 3.0 * 0.044715 * x * x)


def _shift_down(x, d, prev8):
    n = x.shape[0]
    row8 = lax.broadcasted_iota(jnp.int32, (SLAB, x.shape[1]), 0)
    y = pltpu.roll(x, d, 0)
    top = jnp.where(row8 < d, pltpu.roll(prev8, d, 0), y[0:SLAB])
    if n == SLAB:
        return top
    return jnp.concatenate([top, y[SLAB:]], axis=0)


def _shift_up(x, d, next8):
    n = x.shape[0]
    row8 = lax.broadcasted_iota(jnp.int32, (SLAB, x.shape[1]), 0)
    y = pltpu.roll(x, n - d, 0)
    bottom = jnp.where(row8 >= SLAB - d, pltpu.roll(next8, SLAB - d, 0), y[n - SLAB:])
    if n == SLAB:
        return bottom
    return jnp.concatenate([y[:n - SLAB], bottom], axis=0)


def _slab_scan_fwd(a, b):
    row = lax.broadcasted_iota(jnp.int32, a.shape, 0)
    for k in (1, 2, 4):
        a_s = pltpu.roll(a, k, 0)
        b_s = pltpu.roll(b, k, 0)
        m = row >= k
        b = jnp.where(m, a * b_s + b, b)
        a = jnp.where(m, a * a_s, a)
    return a, b


def _slab_scan_bwd(a, b):
    row = lax.broadcasted_iota(jnp.int32, a.shape, 0)
    for k in (1, 2, 4):
        a_s = pltpu.roll(a, SLAB - k, 0)
        b_s = pltpu.roll(b, SLAB - k, 0)
        m = row < SLAB - k
        b = jnp.where(m, a * b_s + b, b)
        a = jnp.where(m, a * a_s, a)
    return a, b


_DIMS = {"nn": (((1,), (0,)), ((), ())), "nt": (((1,), (1,)), ((), ())), "tn": (((0,), (0,)), ((), ()))}


def _dot(a, b, mode="nn"):
    return lax.dot_general(a, b, _DIMS[mode], preferred_element_type=F32)


def _mm(a, b, mode, M, N, K, *, name, out_dtype=F32, tm=1024, tn=1024, tk=1024,
        a_off=(0, 0), b_off=(0, 0), add=None, epi=None, epi_ins=()):
    tm, tn, tk = min(tm, M), min(tn, N), min(tk, K)
    nk = K // tk
    grid = (M // tm, N // tn, nk)
    if mode == "nn":
        a_spec = pl.BlockSpec((tm, tk), lambda i, j, k: (i + a_off[0], k + a_off[1]))
        b_spec = pl.BlockSpec((tk, tn), lambda i, j, k: (k + b_off[0], j + b_off[1]))
    elif mode == "nt":
        a_spec = pl.BlockSpec((tm, tk), lambda i, j, k: (i + a_off[0], k + a_off[1]))
        b_spec = pl.BlockSpec((tn, tk), lambda i, j, k: (j + b_off[0], k + b_off[1]))
    else:
        a_spec = pl.BlockSpec((tk, tm), lambda i, j, k: (k + a_off[0], i + a_off[1]))
        b_spec = pl.BlockSpec((tk, tn), lambda i, j, k: (k + b_off[0], j + b_off[1]))
    o_spec = pl.BlockSpec((tm, tn), lambda i, j, k: (i, j))
    extra = ([add] if add is not None else []) + list(epi_ins)
    n_extra = len(extra)
    has_add = add is not None

    def body(*refs):
        a_ref, b_ref = refs[0], refs[1]
        ex = refs[2:2 + n_extra]
        o_ref = refs[2 + n_extra]

        def finish(acc):
            if has_add:
                acc = acc + ex[0][...].astype(F32)
            if epi is not None:
                acc = epi(acc, *[e[...] for e in ex[(1 if has_add else 0):]])
            o_ref[...] = acc.astype(o_ref.dtype)

        p = _dot(a_ref[...].astype(BF), b_ref[...].astype(BF), mode)
        if nk == 1:
            finish(p)
        else:
            acc_ref = refs[3 + n_extra]
            k = pl.program_id(2)

            @pl.when(k == 0)
            def _():
                acc_ref[...] = p

            @pl.when(k > 0)
            def _():
                acc_ref[...] += p

            @pl.when(k == nk - 1)
            def _():
                finish(acc_ref[...])

    blk = (_nbytes((tm, tk), a.dtype) + _nbytes((tk, tn), b.dtype) + _nbytes((tm, tn), out_dtype)
           + sum(_nbytes((tm, tn), e.dtype) for e in extra) + 2 * _nbytes((tm, tn), F32))
    return pl.pallas_call(
        body, name=name, grid=grid,
        in_specs=[a_spec, b_spec] + [o_spec] * n_extra,
        out_specs=o_spec,
        out_shape=jax.ShapeDtypeStruct((M, N), out_dtype),
        scratch_shapes=[pltpu.VMEM((tm, tn), F32)] if nk > 1 else [],
        compiler_params=pltpu.CompilerParams(
            dimension_semantics=("parallel", "parallel", "arbitrary"), vmem_limit_bytes=_vmem_limit(blk)),
    )(a, b, *extra)


def _ew(fn, T, tm, ins, consts, outs, accs, *, name, reverse=False):
    tm = min(tm, T)
    nt = T // tm
    n_in, n_c, n_o, n_a = len(ins), len(consts), len(outs), len(accs)

    def row(i):
        return nt - 1 - i if reverse else i

    in_specs = [pl.BlockSpec((tm, w), functools.partial(lambda i, cb: (row(i), cb), cb=cb)) for (_, w, cb) in ins]
    in_specs += [pl.BlockSpec(c.shape, functools.partial(lambda i, nd: (0,) * nd, nd=c.ndim)) for c in consts]
    out_specs = [pl.BlockSpec((tm, w), lambda i: (row(i), 0)) for (w, _) in outs]
    out_specs += [pl.BlockSpec((r, w), lambda i: (0, 0)) for (r, w) in accs]
    out_shape = [jax.ShapeDtypeStruct((T, w), dt) for (w, dt) in outs]
    out_shape += [jax.ShapeDtypeStruct((r, w), F32) for (r, w) in accs]

    def body(*refs):
        in_refs = refs[:n_in]
        c_refs = refs[n_in:n_in + n_c]
        o_refs = refs[n_in + n_c:n_in + n_c + n_o]
        a_refs = refs[n_in + n_c + n_o:]
        ov, av = fn([r[...] for r in in_refs], [r[...] for r in c_refs])
        for r, v in zip(o_refs, ov):
            r[...] = v.astype(r.dtype)
        if n_a:
            i = pl.program_id(0)

            @pl.when(i == 0)
            def _():
                for r, v in zip(a_refs, av):
                    r[...] = v

            @pl.when(i > 0)
            def _():
                for r, v in zip(a_refs, av):
                    r[...] += v

    blk = (sum(_nbytes((tm, w), a.dtype) for (a, w, _) in ins) + sum(_nbytes(c.shape, c.dtype) for c in consts)
           + sum(_nbytes((tm, w), dt) for (w, dt) in outs) + sum(_nbytes(s, F32) for s in accs))
    res = pl.pallas_call(
        body, name=name, grid=(nt,), in_specs=in_specs, out_specs=out_specs, out_shape=out_shape,
        compiler_params=pltpu.CompilerParams(
            dimension_semantics=("arbitrary",), vmem_limit_bytes=_vmem_limit(blk)),
    )(*[a for (a, _, _) in ins], *consts)
    return res


def _colsum(v):
    return jnp.sum(v, axis=0, keepdims=True)


def _fgate_fwd(fl, fb, T, tm=512):
    tm = min(tm, T)

    def body(fl_ref, fb_ref, f_ref, carry_ref):
        i = pl.program_id(0)

        @pl.when(i == 0)
        def _():
            carry_ref[...] = jnp.zeros_like(carry_ref)

        row = lax.broadcasted_iota(jnp.int32, (SLAB, DH), 0)

        def slab(s, carry):
            r0 = pl.multiple_of(s * SLAB, SLAB)
            z = fl_ref[pl.ds(r0, SLAB), :] + fb_ref[...]
            c = jnp.minimum(z, 0.0) - _log1p(jnp.exp(-jnp.abs(z)))
            for k in (1, 2, 4):
                c = c + jnp.where(row >= k, pltpu.roll(c, k, 0), 0.0)
            c = c + carry
            f_ref[pl.ds(r0, SLAB), :] = c
            return c[SLAB - 1:SLAB, :]

        carry_ref[0:1, :] = lax.fori_loop(0, tm // SLAB, slab, carry_ref[0:1, :])

    return pl.pallas_call(
        body, name="fgate_fwd", grid=(T // tm,),
        in_specs=[pl.BlockSpec((tm, DH), lambda i: (i, 0)), pl.BlockSpec((1, DH), lambda i: (0, 0))],
        out_specs=pl.BlockSpec((tm, DH), lambda i: (i, 0)),
        out_shape=jax.ShapeDtypeStruct((T, DH), F32),
        scratch_shapes=[pltpu.VMEM((SLAB, DH), F32)],
        compiler_params=pltpu.CompilerParams(dimension_semantics=("arbitrary",)),
    )(fl, fb)


def _fgate_bwd(dF, fl, fb, T, tm=512):
    tm = min(tm, T)
    nt = T // tm

    def body(df_ref, fl_ref, fb_ref, o_ref, acc_ref, carry_ref):
        i = pl.program_id(0)

        @pl.when(i == 0)
        def _():
            carry_ref[...] = jnp.zeros_like(carry_ref)
            acc_ref[...] = jnp.zeros_like(acc_ref)

        row = lax.broadcasted_iota(jnp.int32, (SLAB, DH), 0)

        def slab(n, carry):
            g_next, acc = carry
            r0 = pl.multiple_of((tm // SLAB - 1 - n) * SLAB, SLAB)
            c = df_ref[pl.ds(r0, SLAB), :]
            for k in (1, 2, 4):
                c = c + jnp.where(row < SLAB - k, pltpu.roll(c, SLAB - k, 0), 0.0)
            c = c + g_next
            z = fl_ref[pl.ds(r0, SLAB), :] + fb_ref[...]
            dfl = c * _sig(-z)
            o_ref[pl.ds(r0, SLAB), :] = dfl.astype(o_ref.dtype)
            return c[0:1, :], acc + _colsum(dfl)

        g, acc = lax.fori_loop(0, tm // SLAB, slab, (carry_ref[0:1, :], jnp.zeros((1, DH), F32)))
        carry_ref[0:1, :] = g
        acc_ref[...] += acc

    return pl.pallas_call(
        body, name="fgate_bwd", grid=(nt,),
        in_specs=[pl.BlockSpec((tm, DH), lambda i: (nt - 1 - i, 0)), pl.BlockSpec((tm, DH), lambda i: (nt - 1 - i, 0)),
                  pl.BlockSpec((1, DH), lambda i: (0, 0))],
        out_specs=[pl.BlockSpec((tm, DH), lambda i: (nt - 1 - i, 0)), pl.BlockSpec((1, DH), lambda i: (0, 0))],
        out_shape=[jax.ShapeDtypeStruct((T, DH), BF), jax.ShapeDtypeStruct((1, DH), F32)],
        scratch_shapes=[pltpu.VMEM((SLAB, DH), F32)],
        compiler_params=pltpu.CompilerParams(dimension_semantics=("arbitrary",)),
    )(dF, fl, fb)


def _conv(x, prev8, cw, cb):
    xs = [x] + [_shift_down(x, d, prev8) for d in (1, 2, 3)]
    xa = cb + cw[3:4, :] * xs[0] + cw[2:3, :] * xs[1] + cw[1:2, :] * xs[2] + cw[0:1, :] * xs[3]
    return xa, xs


def _lru_gates(xa_g, wa_g, wx_g, ba_g, bx_g, sp_g):
    xb = xa_g.astype(BF)
    r = _sig(_dot(xb, wa_g) + ba_g)
    ig = _sig(_dot(xb, wx_g) + bx_g)
    la = -LRU_C * r * sp_g
    a = jnp.exp(la)
    mult = jnp.sqrt(-_expm1(2.0 * la))
    return r, ig, a, mult


def _lru_fwd(xg, cw, vec, wabd, wxbd, T, tm=256):
    tm = min(tm, T)
    nsl = tm // SLAB

    def body(x_ref, xp_ref, cw_ref, vec_ref, wa_ref, wx_ref, h_ref, a_s, b_s, carry_ref):
        i = pl.program_id(0)

        @pl.when(i == 0)
        def _():
            carry_ref[...] = jnp.zeros_like(carry_ref)

        x = x_ref[...]
        prev8 = jnp.where(i > 0, xp_ref[...], 0.0)
        vec_v = vec_ref[...]
        xa, _ = _conv(x, prev8, cw_ref[...], vec_v[0:1, :])
        sp = _softplus(-vec_v[3:4, :])
        for g in range(NBD):
            sl = slice(g * BD, (g + 1) * BD)
            _, ig, a, mult = _lru_gates(xa[:, sl], wa_ref[g], wx_ref[g], vec_v[1:2, sl], vec_v[2:3, sl], sp[:, sl])
            a_s[:, sl] = a
            b_s[:, sl] = mult * ig * xa[:, sl]

        def slab(s, carry):
            r0 = pl.multiple_of(s * SLAB, SLAB)
            A, B = _slab_scan_fwd(a_s[pl.ds(r0, SLAB), :], b_s[pl.ds(r0, SLAB), :])
            h = A * carry + B
            h_ref[pl.ds(r0, SLAB), :] = h
            return h[SLAB - 1:SLAB, :]

        carry_ref[0:1, :] = lax.fori_loop(0, nsl, slab, carry_ref[0:1, :])

    blk = 5 * _nbytes((tm, D), F32) + 2 * _nbytes((NBD, BD, BD), BF)
    return pl.pallas_call(
        body, name="lru_fwd", grid=(T // tm,),
        in_specs=[pl.BlockSpec((tm, D), lambda i: (i, 0)),
                  pl.BlockSpec((SLAB, D), lambda i: (jnp.maximum(i * nsl - 1, 0), 0)),
                  pl.BlockSpec((CONV, D), lambda i: (0, 0)),
                  pl.BlockSpec((SLAB, D), lambda i: (0, 0)),
                  pl.BlockSpec((NBD, BD, BD), lambda i: (0, 0, 0)),
                  pl.BlockSpec((NBD, BD, BD), lambda i: (0, 0, 0))],
        out_specs=pl.BlockSpec((tm, D), lambda i: (i, 0)),
        out_shape=jax.ShapeDtypeStruct((T, D), F32),
        scratch_shapes=[pltpu.VMEM((tm, D), F32), pltpu.VMEM((tm, D), F32), pltpu.VMEM((SLAB, D), F32)],
        compiler_params=pltpu.CompilerParams(dimension_semantics=("arbitrary",), vmem_limit_bytes=_vmem_limit(blk)),
    )(xg, xg, cw, vec, wabd, wxbd)


def _lru_bwd(xg, h, dha, cw, vec, wabd, wxbd, T, tm=256):
    tm = min(tm, T)
    nsl = tm // SLAB
    nt = T // tm

    def body(x_ref, xp_ref, h_ref, hp_ref, dh_ref, cw_ref, vec_ref, wa_ref, wx_ref,
             dx_ref, dwa_ref, dwx_ref, acc_ref, a_s, b_s, g_s, dxa_s, carry_ref, dxan_ref):
        n = pl.program_id(0)
        it = nt - 1 - n

        @pl.when(n == 0)
        def _():
            carry_ref[...] = jnp.zeros_like(carry_ref)
            dxan_ref[...] = jnp.zeros_like(dxan_ref)
            dwa_ref[...] = jnp.zeros_like(dwa_ref)
            dwx_ref[...] = jnp.zeros_like(dwx_ref)
            acc_ref[...] = jnp.zeros_like(acc_ref)

        x = x_ref[...]
        prev8 = jnp.where(it > 0, xp_ref[...], 0.0)
        hprev8 = jnp.where(it > 0, hp_ref[...], 0.0)
        vec_v = vec_ref[...]
        cw_v = cw_ref[...]
        xa, xs = _conv(x, prev8, cw_v, vec_v[0:1, :])
        sp = _softplus(-vec_v[3:4, :])
        gates = []
        for g in range(NBD):
            sl = slice(g * BD, (g + 1) * BD)
            r, ig, a, mult = _lru_gates(xa[:, sl], wa_ref[g], wx_ref[g], vec_v[1:2, sl], vec_v[2:3, sl], sp[:, sl])
            gates.append((r, ig, a, mult))
            a_s[:, sl] = a
        a_next = _shift_up(a_s[...], 1, carry_ref[...])
        a_s[...] = a_next
        b_s[...] = dh_ref[...]

        def slab(m, carry):
            r0 = pl.multiple_of((nsl - 1 - m) * SLAB, SLAB)
            A, B = _slab_scan_bwd(a_s[pl.ds(r0, SLAB), :], b_s[pl.ds(r0, SLAB), :])
            gg = A * carry + B
            g_s[pl.ds(r0, SLAB), :] = gg
            return gg[0:1, :]

        g_first = lax.fori_loop(0, nsl, slab, carry_ref[1:2, :])
        gt = g_s[...]
        h_prev = _shift_down(h_ref[...], 1, hprev8)
        dba = []
        dbx = []
        dsp = []
        for g in range(NBD):
            sl = slice(g * BD, (g + 1) * BD)
            r, ig, a, mult = gates[g]
            xa_g = xa[:, sl]
            g_g = gt[:, sl]
            da = g_g * h_prev[:, sl]
            dmult = g_g * ig * xa_g
            di = g_g * mult * xa_g
            dxa_g = g_g * mult * ig
            dla = da * a - dmult * (a * a / mult)
            dr = dla * (-LRU_C) * sp[:, sl]
            dsp.append(_colsum(dla * (-LRU_C) * r))
            dra = (dr * r * (1.0 - r))
            dix = (di * ig * (1.0 - ig))
            dba.append(_colsum(dra))
            dbx.append(_colsum(dix))
            dra_b = dra.astype(BF)
            dix_b = dix.astype(BF)
            xb = xa_g.astype(BF)
            dxa_g = dxa_g + _dot(dra_b, wa_ref[g], "nt") + _dot(dix_b, wx_ref[g], "nt")
            dwa_ref[g] += _dot(xb, dra_b, "tn")
            dwx_ref[g] += _dot(xb, dix_b, "tn")
            dxa_s[:, sl] = dxa_g
        dxa = dxa_s[...]
        nxt = dxan_ref[...]
        dx = (cw_v[3:4, :] * dxa + cw_v[2:3, :] * _shift_up(dxa, 1, nxt)
              + cw_v[1:2, :] * _shift_up(dxa, 2, nxt) + cw_v[0:1, :] * _shift_up(dxa, 3, nxt))
        dx_ref[...] = dx.astype(dx_ref.dtype)
        acc_ref[0:1, :] += jnp.concatenate(dba, axis=1)
        acc_ref[1:2, :] += jnp.concatenate(dbx, axis=1)
        acc_ref[2:3, :] += jnp.concatenate(dsp, axis=1)
        acc_ref[3:4, :] += _colsum(dxa)
        for k in range(CONV):
            acc_ref[4 + k:5 + k, :] += _colsum(dxa * xs[CONV - 1 - k])
        dxan_ref[...] = dxa[0:SLAB, :]
        a_first = jnp.concatenate([gates[g][2][0:1, :] for g in range(NBD)], axis=1)
        carry_ref[0:1, :] = a_first
        carry_ref[1:2, :] = g_first

        @pl.when(n == nt - 1)
        def _():
            acc_ref[2:3, :] = acc_ref[2:3, :] * (-_sig(-vec_v[3:4, :]))

    rowblk = lambda i: (nt - 1 - i, 0)
    prevblk = lambda i: (jnp.maximum((nt - 1 - i) * nsl - 1, 0), 0)
    c2 = lambda i: (0, 0)
    c3 = lambda i: (0, 0, 0)
    blk = 12 * _nbytes((tm, D), F32) + 6 * _nbytes((NBD, BD, BD), F32)
    return pl.pallas_call(
        body, name="lru_bwd", grid=(nt,),
        in_specs=[pl.BlockSpec((tm, D), rowblk), pl.BlockSpec((SLAB, D), prevblk),
                  pl.BlockSpec((tm, D), rowblk), pl.BlockSpec((SLAB, D), prevblk),
                  pl.BlockSpec((tm, D), rowblk),
                  pl.BlockSpec((CONV, D), c2), pl.BlockSpec((SLAB, D), c2),
                  pl.BlockSpec((NBD, BD, BD), c3), pl.BlockSpec((NBD, BD, BD), c3)],
        out_specs=[pl.BlockSpec((tm, D), rowblk), pl.BlockSpec((NBD, BD, BD), c3), pl.BlockSpec((NBD, BD, BD), c3),
                   pl.BlockSpec((16, D), c2)],
        out_shape=[jax.ShapeDtypeStruct((T, D), BF), jax.ShapeDtypeStruct((NBD, BD, BD), F32),
                   jax.ShapeDtypeStruct((NBD, BD, BD), F32), jax.ShapeDtypeStruct((16, D), F32)],
        scratch_shapes=[pltpu.VMEM((tm, D), F32), pltpu.VMEM((tm, D), F32), pltpu.VMEM((tm, D), F32),
                        pltpu.VMEM((tm, D), F32), pltpu.VMEM((SLAB, D), F32), pltpu.VMEM((SLAB, D), F32)],
        compiler_params=pltpu.CompilerParams(dimension_semantics=("arbitrary",), vmem_limit_bytes=_vmem_limit(blk)),
    )(xg, xg, h, h, dha, cw, vec, wabd, wxbd)


_SCALE = 1.0 / math.sqrt(DH)


DA = 2 * DH
_LOG2E = math.log2(math.e)
_C2 = _SCALE * _LOG2E


def _aug_fn(ins, cs):
    q, k, fcum = ins
    g_all = fcum * (1.0 / _SCALE)
    lane = lax.broadcasted_iota(jnp.int32, (q.shape[0], DH), 1)
    qa, ka = [], []
    for hd in range(NH):
        g = g_all[:, hd:hd + 1]
        hi = g.astype(BF).astype(F32)
        mid = (g - hi).astype(BF).astype(F32)
        lo = ((g - hi) - mid).astype(BF).astype(F32)
        qx = jnp.where(lane == 0, hi, jnp.where(lane == 1, mid, jnp.where(lane == 2, lo,
                                                                          jnp.where(lane < 6, 1.0, 0.0))))
        kx = jnp.where(lane < 3, 1.0, jnp.where(lane == 3, -hi, jnp.where(lane == 4, -mid,
                                                                          jnp.where(lane == 5, -lo, 0.0))))
        qa += [q[:, hd * DH:(hd + 1) * DH], qx.astype(BF)]
        ka += [k[:, hd * DH:(hd + 1) * DH], kx.astype(BF)]
    return [jnp.concatenate(qa, axis=1), jnp.concatenate(ka, axis=1)], []


_KA_ONES = DH + 3
KT_ONES = 16
ATTN_CHAINS = 1


def _attn_fwd(qa, ka, vt, T, blk=512):
    blk = min(blk, T)
    nb = T // blk
    bqs = blk // ATTN_CHAINS

    def body(q_ref, k_ref, v_ref, o_ref, lse_ref):
        i = pl.program_id(1)
        qs = [q_ref[pl.ds(c * bqs, bqs), :] for c in range(ATTN_CHAINS)]

        def scores(j, c):
            r0 = pl.multiple_of(j * blk, blk)
            return _dot(k_ref[pl.ds(r0, blk), :], qs[c], "nt") * _C2

        def update(s, vj, carry):
            m, l, acc = carry
            m_new = jnp.maximum(m, jnp.max(s, axis=0, keepdims=True))
            alpha = jnp.exp2(m - m_new)
            p = jnp.exp2(s - m_new)
            l = alpha * l + jnp.sum(p, axis=0, keepdims=True)
            acc = alpha * acc + _dot(vj, p.astype(BF))
            return m_new, l, acc

        def step(j, carry):
            ss, st = carry
            nxt = tuple(scores(j + 1, c) for c in range(ATTN_CHAINS))
            vj = v_ref[j]
            return nxt, tuple(update(ss[c], vj, st[c]) for c in range(ATTN_CHAINS))

        init = tuple((jnp.full((1, bqs), -jnp.inf, F32), jnp.zeros((1, bqs), F32), jnp.zeros((DH, bqs), F32))
                     for _ in range(ATTN_CHAINS))
        first = tuple(scores(0, c) for c in range(ATTN_CHAINS))
        last, carry = lax.fori_loop(0, i, step, (first, init))
        rk = lax.broadcasted_iota(jnp.int32, (blk, bqs), 0)
        cq = lax.broadcasted_iota(jnp.int32, (blk, bqs), 1)
        vi = v_ref[i]
        for c in range(ATTN_CHAINS):
            s = jnp.where(cq + c * bqs >= rk, last[c], -jnp.inf)
            m, l, acc = update(s, vi, carry[c])
            o_ref[:, c * bqs:(c + 1) * bqs] = (acc / l).astype(o_ref.dtype)
            lse_ref[:, c * bqs:(c + 1) * bqs] = m + jnp.log(l) * _LOG2E

    vm = _nbytes((T, DA), BF) + _nbytes((T, DH), BF) + 6 * _nbytes((blk, blk), F32)
    return pl.pallas_call(
        body, name="attn_fwd", grid=(NH, nb),
        in_specs=[pl.BlockSpec((blk, DA), lambda h, i: (i, h)),
                  pl.BlockSpec((T, DA), lambda h, i: (0, h)),
                  pl.BlockSpec((None, nb, DH, blk), lambda h, i: (h, 0, 0, 0))],
        out_specs=[pl.BlockSpec((None, None, DH, blk), lambda h, i: (h, i, 0, 0)),
                   pl.BlockSpec((None, None, 1, blk), lambda h, i: (h, i, 0, 0))],
        out_shape=[jax.ShapeDtypeStruct((NH, nb, DH, blk), BF), jax.ShapeDtypeStruct((NH, nb, 1, blk), F32)],
        compiler_params=pltpu.CompilerParams(dimension_semantics=("parallel", "arbitrary"),
                                             vmem_limit_bytes=_vmem_limit(vm)),
    )(qa, ka, vt)


def _attn_bwd(qa, ka, kt, qkv, do, lrow, drow, T, blk=512):
    blk = min(blk, T)
    nb = T // blk

    def body(ka_ref, kt_ref, v_ref, qa_ref, do_ref, l_ref, d_ref, dq_ref, dk_ref, dv_ref, dfs_ref):
        j = pl.program_id(1)

        @pl.when(j == 0)
        def _():
            dq_ref[...] = jnp.zeros_like(dq_ref)

        row = lax.broadcasted_iota(jnp.int32, (DH + KT_ONES, blk), 0)
        dq_scale = jnp.where(row < DH, _SCALE, 1.0)

        kaj = ka_ref[...]
        ktj = kt_ref[...]
        vj = v_ref[...]

        def step(i, carry, diag):
            dka, dv = carry
            r0 = pl.multiple_of(i * blk, blk)
            qi = qa_ref[pl.ds(r0, blk), :]
            doi = do_ref[pl.ds(r0, blk), :]
            st = _dot(kaj, qi, "nt") * _C2 - l_ref[i]
            if diag:
                rk = lax.broadcasted_iota(jnp.int32, (blk, blk), 0)
                cq = lax.broadcasted_iota(jnp.int32, (blk, blk), 1)
                st = jnp.where(cq >= rk, st, -jnp.inf)
            pt = jnp.exp2(st)
            dv = dv + _dot(pt.astype(BF), doi)
            dpt = _dot(vj, doi, "nt")
            dst = pt * (dpt - d_ref[i])
            dsb = dst.astype(BF)
            dka = dka + _dot(dsb, qi)
            dq_ref[i] += _dot(ktj, dsb) * dq_scale
            return dka, dv

        init = (jnp.zeros((blk, DA), F32), jnp.zeros((blk, DH), F32))
        carry = step(j, init, True)
        dka, dv = lax.fori_loop(j + 1, nb, lambda i, c: step(i, c, False), carry)
        dk_ref[...] = (dka[:, :DH] * _SCALE).astype(dk_ref.dtype)
        dv_ref[...] = dv.astype(dv_ref.dtype)
        dfs_ref[...] = dka[:, _KA_ONES:_KA_ONES + 1]

    rowv = pl.BlockSpec((None, nb, 1, blk), lambda h, j: (h, 0, 0, 0))
    vm = _nbytes((T, DA), BF) + _nbytes((T, DH), BF) + _nbytes((T, DH), F32) + 8 * _nbytes((blk, blk), F32)
    return pl.pallas_call(
        body, name="attn_bwd", grid=(NH, nb),
        in_specs=[pl.BlockSpec((blk, DA), lambda h, j: (j, h)),
                  pl.BlockSpec((None, None, DH + KT_ONES, blk), lambda h, j: (h, j, 0, 0)),
                  pl.BlockSpec((blk, DH), lambda h, j: (j, 2 * NH + h)),
                  pl.BlockSpec((T, DA), lambda h, j: (0, h)),
                  pl.BlockSpec((T, DH), lambda h, j: (0, h)),
                  rowv, rowv],
        out_specs=[pl.BlockSpec((None, nb, DH + KT_ONES, blk), lambda h, j: (h, 0, 0, 0)),
                   pl.BlockSpec((blk, DH), lambda h, j: (j, h)),
                   pl.BlockSpec((blk, DH), lambda h, j: (j, h)),
                   pl.BlockSpec((None, blk, 1), lambda h, j: (h, j, 0))],
        out_shape=[jax.ShapeDtypeStruct((NH, nb, DH + KT_ONES, blk), F32), jax.ShapeDtypeStruct((T, D), BF),
                   jax.ShapeDtypeStruct((T, D), BF), jax.ShapeDtypeStruct((NH, T, 1), F32)],
        compiler_params=pltpu.CompilerParams(dimension_semantics=("parallel", "arbitrary"),
                                             vmem_limit_bytes=_vmem_limit(vm)),
    )(ka, kt, qkv, qa, do, lrow, drow)


def _norm_fn(ins, cs):
    x, = ins
    g, = cs
    r = lax.rsqrt(jnp.mean(x * x, axis=-1, keepdims=True) + EPS)
    return [x * r * g], []


def _norm_bwd_fn(ins, cs):
    x, dy, dres = ins
    g, = cs
    r = lax.rsqrt(jnp.mean(x * x, axis=-1, keepdims=True) + EPS)
    xh = x * r
    dxh = dy * g
    dx = dres + r * (dxh - xh * jnp.mean(dxh * xh, axis=-1, keepdims=True))
    return [dx], [_colsum(dy * xh)]


def _final_fn(ins, cs):
    x2, tgt = ins
    g, = cs
    r = lax.rsqrt(jnp.mean(x2 * x2, axis=-1, keepdims=True) + EPS)
    xh = x2 * r
    e = xh * g - tgt
    dy = e * (1.0 / D)
    dxh = dy * g
    dx2 = r * (dxh - xh * jnp.mean(dxh * xh, axis=-1, keepdims=True))
    return [dx2], [_colsum(0.5 * e * e * (1.0 / D)), _colsum(dy * xh)]


def _z_fn(ins, cs):
    g, h = ins
    return [_gelu(g) * h], []


def _mix_fn(ins, cs):
    gates, ya, yb = ins
    return [_sig(gates[:, :D]) * ya + _sig(gates[:, D:]) * yb], []


def _mix_bwd_fn(ins, cs):
    dmix, gates, ya, yb = ins
    ga = _sig(gates[:, :D])
    gb = _sig(gates[:, D:])
    dgates = jnp.concatenate([dmix * ya * ga * (1.0 - ga), dmix * yb * gb * (1.0 - gb)], axis=1)
    return [dmix * ga, dmix * gb, dgates], []


def _z_bwd_fn(ins, cs):
    dz, g, h = ins
    return [dz * _gelu(g), dz * h * _gelu_grad(g)], []


def _delta_fn(ins, cs):
    do, o = ins
    p = do.astype(F32) * o.astype(F32)
    lane = lax.broadcasted_iota(jnp.int32, (p.shape[0], DH), 1)
    out = jnp.zeros((p.shape[0], DH), F32)
    for hd in range(NH):
        s = jnp.sum(p[:, hd * DH:(hd + 1) * DH], axis=1, keepdims=True)
        out = jnp.where(lane == hd, s, out)
    return [out], []


def _local_step(x, tgt, w, T, blk=1024):
    blk = min(blk, T)
    nb = T // blk
    win = w["win"]

    u, = _ew(_norm_fn, T, 512, [(x, D, 0)], [w["g_mix"]], [(D, BF)], [], name="norm_mix")
    xg = _mm(u, win, "nn", T, 2 * D, D, name="proj_lru")
    qkv = _mm(u, win, "nn", T, 3 * D, D, name="proj_qkv", out_dtype=BF, b_off=(0, 2))
    gates = _mm(u, win, "nn", T, 2 * D, D, name="proj_gates", b_off=(0, 5))
    fl = _mm(u, win, "nn", T, DH, D, name="proj_f", tn=DH, b_off=(0, 7 * D // DH))
    fcum = _fgate_fwd(fl, w["fb"], T)
    qa, ka = _ew(_aug_fn, T, 256, [(qkv, D, 0), (qkv, D, 1), (fcum, DH, 0)], [],
                 [(NH * DA, BF), (NH * DA, BF)], [], name="attn_augment")

    def heads_t(a):
        return a.reshape(nb, blk, NH, DH).transpose(2, 0, 3, 1)

    def heads_back(a):
        return a.transpose(1, 3, 0, 2).reshape(T, D)

    kt = heads_t(qkv[:, D:2 * D])
    vt = heads_t(qkv[:, 2 * D:])
    h = _lru_fwd(xg, w["cw"], w["vec"], w["wabd"], w["wxbd"], T)
    ot, lse = _attn_fwd(qa, ka, vt, T, blk)
    ob = heads_back(ot)
    z, = _ew(_z_fn, T, 512, [(xg, D, 1), (h, D, 0)], [], [(D, BF)], [], name="lru_gelu")
    ya = _mm(z, w["wa"], "nn", T, D, D, name="branch_a")
    yb = _mm(ob, w["wb"], "nn", T, D, D, name="branch_b")
    mix, = _ew(_mix_fn, T, 256, [(gates, 2 * D, 0), (ya, D, 0), (yb, D, 0)], [], [(D, BF)], [], name="mix")
    x1 = _mm(mix, w["wout"], "nn", T, D, D, name="out_proj", add=x)
    m, = _ew(_norm_fn, T, 512, [(x1, D, 0)], [w["g_mlp"]], [(D, BF)], [], name="norm_mlp")
    hh = _mm(m, w["wup"], "nn", T, FF, D, name="mlp_up", out_dtype=BF,
             epi=lambda acc: jnp.square(jnp.maximum(acc, 0.0)))
    x2 = _mm(hh, w["wdown"], "nn", T, D, FF, name="mlp_down", add=x1)
    dx2, loss_vec, dg_fin = _ew(_final_fn, T, 256, [(x2, D, 0), (tgt, D, 0)], [w["g_fin"]], [(D, F32)],
                                [(1, D), (1, D)], name="final_norm_loss")

    dhpre = _mm(dx2, w["wdown"], "nt", T, FF, D, name="mlp_down_bwd", out_dtype=BF,
                epi=lambda acc, h2: acc * (2.0 * jnp.sqrt(h2.astype(F32))), epi_ins=[hh])
    dwdown = _mm(hh, dx2, "tn", FF, D, T, name="dw_down", tk=512)
    dwup = _mm(m, dhpre, "tn", D, FF, T, name="dw_up", tk=512)
    dm = _mm(dhpre, w["wup"], "nt", T, D, FF, name="mlp_up_bwd")
    dx1, dg_mlp = _ew(_norm_bwd_fn, T, 256, [(x1, D, 0), (dm, D, 0), (dx2, D, 0)], [w["g_mlp"]], [(D, F32)],
                      [(1, D)], name="norm_mlp_bwd")

    dmix = _mm(dx1, w["wout"], "nt", T, D, D, name="out_proj_bwd")
    dwout = _mm(mix, dx1, "tn", D, D, T, name="dw_out", tk=512)
    dya, dyb, dgates = _ew(_mix_bwd_fn, T, 256, [(dmix, D, 0), (gates, 2 * D, 0), (ya, D, 0), (yb, D, 0)], [],
                           [(D, BF), (D, BF), (2 * D, BF)], [], name="mix_bwd")
    dob = _mm(dyb, w["wb"], "nt", T, D, D, name="branch_b_bwd", out_dtype=BF)
    dwb = _mm(ob, dyb, "tn", D, D, T, name="dw_b", tk=512)
    dz = _mm(dya, w["wa"], "nt", T, D, D, name="branch_a_bwd")
    dwa = _mm(z, dya, "tn", D, D, T, name="dw_a", tk=512)
    dha, dglru = _ew(_z_bwd_fn, T, 256, [(dz, D, 0), (xg, D, 1), (h, D, 0)], [], [(D, F32), (D, BF)], [],
                     name="lru_gelu_bwd")

    delta, = _ew(_delta_fn, T, 512, [(dob, D, 0), (ob, D, 0)], [], [(DH, F32)], [], name="attn_delta")
    drow = delta[:, :NH].T.reshape(NH, nb, 1, blk)
    kt1 = jnp.concatenate([kt, jnp.ones((NH, nb, KT_ONES, blk), BF)], axis=2)
    dqt, dk, dv, dfs = _attn_bwd(qa, ka, kt1, qkv, dob, lse, drow, T, blk)
    dq = heads_back(dqt[:, :, :DH])
    dfcum = jnp.pad((dqt[:, :, DH].reshape(NH, T) - dfs[:, :, 0]).T, ((0, 0), (0, DH - NH)))
    dfl, dfb = _fgate_bwd(dfcum, fl, w["fb"], T)

    dxl, dwabd, dwxbd, lacc = _lru_bwd(xg, h, dha, w["cw"], w["vec"], w["wabd"], w["wxbd"], T)

    du = _mm(dxl, win, "nt", T, D, D, name="du_lru")
    du = _mm(dglru, win, "nt", T, D, D, name="du_glru", b_off=(0, 1), add=du)
    du = _mm(dq, win, "nt", T, D, D, name="du_q", b_off=(0, 2), add=du)
    du = _mm(dk, win, "nt", T, D, D, name="du_k", b_off=(0, 3), add=du)
    du = _mm(dv, win, "nt", T, D, D, name="du_v", b_off=(0, 4), add=du)
    du = _mm(dgates, win, "nt", T, D, 2 * D, name="du_gates", b_off=(0, 5), add=du)
    du = _mm(dfl, win, "nt", T, D, DH, name="du_f", tk=DH, b_off=(0, 7 * D // DH), add=du)
    pieces = [_mm(u, p, "tn", D, p.shape[1], T, name="dw_in_%d" % n, tk=512)
              for n, p in enumerate((dxl, dglru, dq, dk, dv, dgates))]
    pieces.append(_mm(u, dfl, "tn", D, DH, T, name="dw_in_f", tk=512)[:, :NH])
    dwin = jnp.concatenate(pieces, axis=1)
    dx, dg_mix = _ew(_norm_bwd_fn, T, 256, [(x, D, 0), (du, D, 0), (dx1, D, 0)], [w["g_mix"]], [(D, F32)],
                     [(1, D)], name="norm_mix_bwd")

    return dict(dx=dx, dwin=dwin, dwa=dwa, dwb=dwb, dwout=dwout, dwup=dwup, dwdown=dwdown,
                dwabd=dwabd, dwxbd=dwxbd, lacc=lacc, dfb=dfb, dg_mix=dg_mix, dg_mlp=dg_mlp, dg_fin=dg_fin,
                loss_vec=loss_vec)


def _block_diag(w):
    per = BD // LRU_BW
    w4 = w.reshape(NBD, per, LRU_BW, LRU_BW)
    out = jnp.zeros((NBD, per, LRU_BW, per, LRU_BW), w.dtype)
    for b in range(per):
        out = out.at[:, b, :, b, :].set(w4[:, b])
    return out.reshape(NBD, BD, BD)


def _block_diag_extract(wbd):
    per = BD // LRU_BW
    w5 = wbd.reshape(NBD, per, LRU_BW, per, LRU_BW)
    return jnp.stack([w5[:, b, :, b, :] for b in range(per)], axis=1).reshape(LRU_BLOCKS, LRU_BW, LRU_BW)


_ANY = pl.BlockSpec(memory_space=pl.ANY)


def _place():
    x, y, c = lax.axis_index("x"), lax.axis_index("y"), lax.axis_index("c")
    chips = [(1 - x, y), (x, 1 - y), (1 - x, 1 - y)]
    return x, y, c, chips


def _allgather_shards(shards):
    n = len(shards)

    def body(*refs):
        ins, outs = refs[:n], refs[n:2 * n]
        send_sems, recv_sems = refs[2 * n:]
        x, y, c, chips = _place()
        me = 2 * x + y
        sibling = (x, y, 1 - c)

        def remote(p, k, src, dst, to):
            return pltpu.make_async_remote_copy(src_ref=src, dst_ref=dst, send_sem=send_sems.at[p, k],
                                                recv_sem=recv_sems.at[p, k], device_id=to, device_id_type=MESH)

        sent = []
        for p in range(n):
            for k, chip in enumerate(chips):
                cp = remote(p, k, ins[p].at[c], outs[p].at[me, c], (chip[0], chip[1], c))
                cp.start()
                sent.append(cp)
        for p in range(n):
            for k, chip in enumerate(chips):
                half = outs[p].at[2 * chip[0] + chip[1], c]
                remote(p, k, half, half, sibling).wait_recv()
                fwd = remote(p, 3 + k, half, half, sibling)
                fwd.start()
                sent.append(fwd)
        for p in range(n):
            for k, chip in enumerate(chips):
                half = outs[p].at[2 * chip[0] + chip[1], 1 - c]
                remote(p, 3 + k, half, half, sibling).wait_recv()
        for cp in sent:
            cp.wait_send()

    gathered = pl.pallas_call(
        body, name="allgather_weights",
        in_specs=[_ANY] * n, out_specs=[_ANY] * n,
        out_shape=[jax.ShapeDtypeStruct((NCHIP,) + s.shape, s.dtype) for s in shards],
        scratch_shapes=[pltpu.SemaphoreType.DMA((n, 6)), pltpu.SemaphoreType.DMA((n, 6))],
    )(*shards)
    me = 2 * lax.axis_index("x") + lax.axis_index("y")
    return [lax.dynamic_update_index_in_dim(g, s, me, 0) for g, s in zip(gathered, shards)]


def _exchange_sibling(grads):
    n = len(grads)

    def body(*refs):
        ins, outs = refs[:n], refs[n:2 * n]
        send_sems, recv_sems = refs[2 * n:]
        x, y, c, _ = _place()
        sibling = (x, y, 1 - c)
        copies = []
        for p in range(n):
            for j in range(NCHIP):
                cp = pltpu.make_async_remote_copy(src_ref=ins[p].at[j, 1 - c], dst_ref=outs[p].at[j],
                                                  send_sem=send_sems.at[p, j], recv_sem=recv_sems.at[p, j],
                                                  device_id=sibling, device_id_type=MESH)
                cp.start()
                copies.append(cp)
        for cp in copies:
            cp.wait()

    return pl.pallas_call(
        body, name="reduce_sibling_exchange",
        in_specs=[_ANY] * n, out_specs=[_ANY] * n,
        out_shape=[jax.ShapeDtypeStruct((NCHIP,) + g.shape[2:], g.dtype) for g in grads],
        scratch_shapes=[pltpu.SemaphoreType.DMA((n, NCHIP)), pltpu.SemaphoreType.DMA((n, NCHIP))],
    )(*grads)


def _exchange_chips(sums):
    n = len(sums)

    def body(*refs):
        ins, outs = refs[:n], refs[n:2 * n]
        send_sems, recv_sems = refs[2 * n:]
        x, y, c, chips = _place()
        copies = []
        for p in range(n):
            for k, chip in enumerate(chips):
                cp = pltpu.make_async_remote_copy(src_ref=ins[p].at[2 * chip[0] + chip[1]], dst_ref=outs[p].at[k],
                                                  send_sem=send_sems.at[p, k], recv_sem=recv_sems.at[p, k],
                                                  device_id=(chip[0], chip[1], c), device_id_type=MESH)
                cp.start()
                copies.append(cp)
        for cp in copies:
            cp.wait()

    return pl.pallas_call(
        body, name="reduce_chip_exchange",
        in_specs=[_ANY] * n, out_specs=[_ANY] * n,
        out_shape=[jax.ShapeDtypeStruct((3,) + s.shape[1:], s.dtype) for s in sums],
        scratch_shapes=[pltpu.SemaphoreType.DMA((n, 3)), pltpu.SemaphoreType.DMA((n, 3))],
    )(*sums)


def _share_halves(halves):
    n = len(halves)

    def body(*refs):
        ins, outs = refs[:n], refs[n:2 * n]
        send_sems, recv_sems = refs[2 * n:]
        x, y, c, _ = _place()
        sibling = (x, y, 1 - c)
        copies = []
        for p in range(n):
            cp = pltpu.make_async_remote_copy(src_ref=ins[p], dst_ref=outs[p], send_sem=send_sems.at[p],
                                              recv_sem=recv_sems.at[p], device_id=sibling, device_id_type=MESH)
            cp.start()
            copies.append(cp)
        for cp in copies:
            cp.wait()

    return pl.pallas_call(
        body, name="reduce_share_halves",
        in_specs=[_ANY] * n, out_specs=[_ANY] * n,
        out_shape=[jax.ShapeDtypeStruct(h.shape, h.dtype) for h in halves],
        scratch_shapes=[pltpu.SemaphoreType.DMA((n,)), pltpu.SemaphoreType.DMA((n,))],
    )(*halves)


def _row_tile(half, cols):
    th = max(SLAB, min(half, (1 << 18) // cols // SLAB * SLAB))
    while half % th:
        th -= SLAB
    return th


def _add_sibling(g, r, c, name):
    _, _, half, cols = g.shape
    th = _row_tile(half, cols)

    def body(c_ref, g_ref, r_ref, o_ref):
        o_ref[...] = (g_ref[...] + r_ref[...]).astype(o_ref.dtype)

    return pl.pallas_call(
        body, name=name,
        grid_spec=pltpu.PrefetchScalarGridSpec(
            num_scalar_prefetch=1, grid=(NCHIP, half // th),
            in_specs=[pl.BlockSpec((None, None, th, cols), lambda j, i, c_ref: (j, c_ref[0], i, 0)),
                      pl.BlockSpec((None, th, cols), lambda j, i, c_ref: (j, i, 0))],
            out_specs=pl.BlockSpec((None, th, cols), lambda j, i, c_ref: (j, i, 0))),
        out_shape=jax.ShapeDtypeStruct((NCHIP, half, cols), BF),
    )(c, g, r)


def _add_chips(s, rb, me, name):
    _, half, cols = s.shape
    th = _row_tile(half, cols)

    def body(me_ref, s_ref, rb_ref, o_ref):
        o_ref[...] = (((s_ref[...].astype(F32) + rb_ref[0].astype(F32)) + rb_ref[1].astype(F32))
                      + rb_ref[2].astype(F32))

    return pl.pallas_call(
        body, name=name,
        grid_spec=pltpu.PrefetchScalarGridSpec(
            num_scalar_prefetch=1, grid=(half // th,),
            in_specs=[pl.BlockSpec((None, th, cols), lambda i, me_ref: (me_ref[0], i, 0)),
                      pl.BlockSpec((3, th, cols), lambda i, me_ref: (0, i, 0))],
            out_specs=pl.BlockSpec((th, cols), lambda i, me_ref: (i, 0))),
        out_shape=jax.ShapeDtypeStruct((half, cols), F32),
    )(me, s, rb)


def _reduce_scatter(grads, c1, me1):
    names = ["w_in", "w_a", "w_b", "w_out", "w_up", "w_down"]
    from_sibling = _exchange_sibling(grads)
    sums = [_add_sibling(g, r, c1, "add_sibling_" + nm) for g, r, nm in zip(grads, from_sibling, names)]
    from_chips = _exchange_chips(sums)
    halves = [_add_chips(s, rb, me1, "add_chips_" + nm) for s, rb, nm in zip(sums, from_chips, names)]
    theirs = _share_halves(halves)
    low = c1[0] == 0
    return [jnp.concatenate([jnp.where(low, h, t), jnp.where(low, t, h)], axis=0) for h, t in zip(halves, theirs)]


N_DEV = 8
SMALL_ROWS = 208


def _allreduce_small(pack):
    def body(x_ref, out_ref, gbuf, send_sems, recv_sems, local_sem):
        x, y, c, chips = _place()
        me, sibling = (x, y, c), (x, y, 1 - c)

        def rows(px, py, pc):
            return gbuf.at[4 * px + 2 * py + pc]

        def copy(k, block, to, src=None):
            return pltpu.make_async_remote_copy(
                src_ref=rows(*block) if src is None else src, dst_ref=rows(*block),
                send_sem=send_sems.at[k], recv_sem=recv_sems.at[k], device_id=to, device_id_type=MESH)

        mine = pltpu.make_async_copy(x_ref, rows(*me), local_sem)
        mine.start()
        first = [copy(0, me, sibling, src=x_ref)]
        first += [copy(1 + j, me, (chip[0], chip[1], c), src=x_ref) for j, chip in enumerate(chips)]
        for cp in first:
            cp.start()
        passed = [copy(4 + j, (chip[0], chip[1], c), sibling) for j, chip in enumerate(chips)]
        for j, chip in enumerate(chips):
            copy(1 + j, (chip[0], chip[1], c), me).wait_recv()
            passed[j].start()
        copy(0, sibling, me).wait_recv()
        for j, chip in enumerate(chips):
            copy(4 + j, (chip[0], chip[1], 1 - c), me).wait_recv()
        for cp in first + passed:
            cp.wait_send()
        mine.wait()
        acc = gbuf[0]
        for d in range(1, N_DEV):
            acc = acc + gbuf[d]
        out_ref[...] = acc

    return pl.pallas_call(
        body, name="allreduce_small",
        in_specs=[pl.BlockSpec(memory_space=pltpu.VMEM)],
        out_specs=pl.BlockSpec(memory_space=pltpu.VMEM),
        out_shape=jax.ShapeDtypeStruct((SMALL_ROWS, D), F32),
        scratch_shapes=[pltpu.VMEM((N_DEV, SMALL_ROWS, D), F32), pltpu.SemaphoreType.DMA((7,)),
                        pltpu.SemaphoreType.DMA((7,)), pltpu.SemaphoreType.DMA],
    )(pack)


def _adamw(w, g, m, v, name):
    rows, cols = w.shape
    th = _row_tile(rows, cols)

    def body(w_ref, g_ref, m_ref, v_ref, d_ref, mo_ref, vo_ref):
        gv = g_ref[...]
        mn = ADAM_B1 * m_ref[...] + (1.0 - ADAM_B1) * gv
        vn = ADAM_B2 * v_ref[...] + (1.0 - ADAM_B2) * (gv * gv)
        m_hat = mn / (1.0 - ADAM_B1 ** ADAM_STEP)
        v_hat = vn / (1.0 - ADAM_B2 ** ADAM_STEP)
        d_ref[...] = -ADAM_LR * (m_hat / (jnp.sqrt(v_hat) + ADAM_EPS) + ADAM_WD * w_ref[...])
        mo_ref[...] = mn
        vo_ref[...] = vn

    spec = pl.BlockSpec((th, cols), lambda i: (i, 0))
    return pl.pallas_call(
        body, name=name, grid=(rows // th,),
        in_specs=[spec] * 4, out_specs=[spec] * 3,
        out_shape=[jax.ShapeDtypeStruct((rows, cols), F32)] * 3,
        compiler_params=pltpu.CompilerParams(dimension_semantics=("parallel",)),
    )(w, g, m, v)


_SMALL = ["norm_mix_g", "norm_mlp_g", "norm_final_g", "conv_b", "lru_ba", "lru_bx", "lru_lambda"]
_ROW_FB, _ROW_CW, _ROW_WA, _ROW_WX, _ROW_LOSS = 56, 64, 72, 136, 200


def _pack_small(vals, col0):
    def slab(a):
        return jnp.pad(a, ((0, -a.shape[0] % SLAB), (0, D - a.shape[1])))

    rows = [slab(vals[n].reshape(1, D)) for n in _SMALL]
    rows.append(slab(vals["forget_b"].reshape(1, NH)))
    if vals["conv_w"].shape[1] == D:
        rows.append(slab(vals["conv_w"]))
    else:
        rows.append(slab(lax.dynamic_update_slice(jnp.zeros((CONV, D), F32), vals["conv_w"], (0, col0))))
    rows.append(vals["lru_wa"].reshape(LRU_BLOCKS * LRU_BW * LRU_BW // D, D))
    rows.append(vals["lru_wx"].reshape(LRU_BLOCKS * LRU_BW * LRU_BW // D, D))
    rows.append(slab(vals["loss"]) if "loss" in vals else jnp.zeros((SLAB, D), F32))
    return jnp.concatenate(rows, axis=0)


def _unpack_small(pack, col0):
    out = {n: pack[SLAB * i] for i, n in enumerate(_SMALL)}
    out["forget_b"] = pack[_ROW_FB, :NH]
    out["conv_w"] = lax.dynamic_slice(pack[_ROW_CW:_ROW_CW + CONV], (0, col0), (CONV, D // NCHIP))
    out["lru_wa"] = pack[_ROW_WA:_ROW_WX].reshape(LRU_BLOCKS, LRU_BW, LRU_BW)
    out["lru_wx"] = pack[_ROW_WX:_ROW_LOSS].reshape(LRU_BLOCKS, LRU_BW, LRU_BW)
    return out


_WEIGHTS = ["norm_mix_g", "w_in", "conv_w", "conv_b", "lru_wa", "lru_ba", "lru_wx", "lru_bx", "lru_lambda",
            "forget_b", "w_branch_a", "w_branch_b", "w_out", "norm_mlp_g", "w_up", "w_down", "norm_final_g"]
_BIG = ["w_in", "w_branch_a", "w_branch_b", "w_out", "w_up", "w_down"]


def _halves(a):
    return a.reshape(2, a.shape[0] // 2, a.shape[1])


def _columns_to_shards(a):
    rows, cols = a.shape[0], a.shape[1] // NCHIP
    return jnp.transpose(a.reshape(rows, NCHIP, cols), (1, 0, 2))


def _shards_to_columns(a):
    n, rows, cols = a.shape
    return jnp.transpose(a, (1, 0, 2)).reshape(rows, n * cols)


def kernel(x, norm_mix_g, w_in, conv_w, conv_b, lru_wa, lru_ba, lru_wx, lru_bx, lru_lambda, forget_b, w_branch_a, w_branch_b, w_out, norm_mlp_g, w_up, w_down, norm_final_g, loss_target, m_norm_mix_g, m_w_in, m_conv_w, m_conv_b, m_lru_wa, m_lru_ba, m_lru_wx, m_lru_bx, m_lru_lambda, m_forget_b, m_w_branch_a, m_w_branch_b, m_w_out, m_norm_mlp_g, m_w_up, m_w_down, m_norm_final_g, v_norm_mix_g, v_w_in, v_conv_w, v_conv_b, v_lru_wa, v_lru_ba, v_lru_wx, v_lru_bx, v_lru_lambda, v_forget_b, v_w_branch_a, v_w_branch_b, v_w_out, v_norm_mlp_g, v_w_up, v_w_down, v_norm_final_g):
    args = dict(locals())
    wts = {n: args[n] for n in _WEIGHTS}
    mom = {n: args["m_" + n] for n in _WEIGHTS}
    var = {n: args["v_" + n] for n in _WEIGHTS}
    T = x.shape[1]
    xi, yi, ci = lax.axis_index("x"), lax.axis_index("y"), lax.axis_index("c")
    me = 2 * xi + yi
    c1 = jnp.reshape(ci, (1,)).astype(jnp.int32)
    me1 = jnp.reshape(me, (1,)).astype(jnp.int32)
    col0 = me * (D // NCHIP)

    cw_pad = jnp.pad(conv_w, ((0, 4 * SLAB - CONV), (0, 0)))
    shards = [_halves(wts[n].astype(BF)) for n in _BIG] + [_halves(cw_pad)]
    g_in, g_a, g_b, g_out, g_up, g_down, g_cw = _allgather_shards(shards)
    cin = DIN // NCHIP
    win = _shards_to_columns(g_in.reshape(NCHIP, D, cin))
    w = dict(
        win=jnp.pad(win, ((0, 0), (0, DINP - DIN))),
        wa=g_a.reshape(D, D), wb=g_b.reshape(D, D), wout=g_out.reshape(D, D),
        wup=_shards_to_columns(g_up.reshape(NCHIP, D, D)), wdown=g_down.reshape(FF, D),
        cw=_shards_to_columns(g_cw.reshape(NCHIP, 4 * SLAB, D // NCHIP)[:, :CONV]),
        vec=jnp.concatenate([conv_b[None], lru_ba[None], lru_bx[None], lru_lambda[None],
                             jnp.zeros((SLAB - 4, D), F32)], axis=0),
        fb=jnp.pad(forget_b[None], ((0, 0), (0, DH - NH))),
        wabd=_block_diag(lru_wa).astype(BF), wxbd=_block_diag(lru_wx).astype(BF),
        g_mix=norm_mix_g[None], g_mlp=norm_mlp_g[None], g_fin=norm_final_g[None])

    r = _local_step(x[0], loss_target[0], w, T)

    big = [_columns_to_shards(r["dwin"]), r["dwa"].reshape(NCHIP, D // NCHIP, D), r["dwb"].reshape(NCHIP, D // NCHIP, D),
           r["dwout"].reshape(NCHIP, D // NCHIP, D), _columns_to_shards(r["dwup"]), r["dwdown"].reshape(NCHIP, D, D)]
    big = [b.reshape(NCHIP, 2, b.shape[1] // 2, b.shape[2]) for b in big]
    gsum = dict(zip(_BIG, _reduce_scatter(big, c1, me1)))
    lacc = r["lacc"]
    small = dict(norm_mix_g=r["dg_mix"], norm_mlp_g=r["dg_mlp"], norm_final_g=r["dg_fin"], conv_b=lacc[3],
                 lru_ba=lacc[0], lru_bx=lacc[1], lru_lambda=lacc[2], forget_b=r["dfb"][0, :NH],
                 conv_w=lacc[4:4 + CONV], lru_wa=_block_diag_extract(r["dwabd"]),
                 lru_wx=_block_diag_extract(r["dwxbd"]), loss=r["loss_vec"])
    gpack = _allreduce_small(_pack_small(small, col0))
    loss = jnp.sum(gpack[_ROW_LOSS])

    grads, delta, new_m, new_v = {}, {}, {}, {}
    for n in _BIG:
        grads[n] = gsum[n]
        delta[n], new_m[n], new_v[n] = _adamw(wts[n], gsum[n], mom[n], var[n], "adamw_" + n)
    dp, mp, vp = _adamw(_pack_small(wts, col0), gpack, _pack_small(mom, col0), _pack_small(var, col0), "adamw_small")
    for dst, pack in ((grads, gpack), (delta, dp), (new_m, mp), (new_v, vp)):
        dst.update(_unpack_small(pack, col0))
    return (loss, r["dx"][None], *[grads[n] for n in _WEIGHTS], *[delta[n] for n in _WEIGHTS],
            *[new_m[n] for n in _WEIGHTS], *[new_v[n] for n in _WEIGHTS])
```

```python
import functools
import math

import jax
import jax.numpy as jnp
import numpy as np
from jax import lax
from jax.experimental import pallas as pl
from jax.experimental.pallas import tpu as pltpu

F32 = jnp.float32
BF = jnp.bfloat16

D = 1024
NH = 8
DH = 128
FF = 4096
CONV = 4
LRU_BLOCKS = 16
LRU_BW = 64
BD = 256
NBD = D // BD
LRU_C = 8.0
EPS = 1e-6
DIN = 7176
DINP = 7296
NCHIP = 4
SLAB = 8
VMEM_CAP = 60 * 1024 * 1024

ADAM_LR = 0.001
ADAM_B1 = 0.9
ADAM_B2 = 0.999
ADAM_EPS = 1e-08
ADAM_WD = 0.01
ADAM_STEP = 10

MESH = pl.DeviceIdType.MESH


def _vmem_limit(nbytes):
    return int(min(VMEM_CAP, max(32 * 1024 * 1024, 3 * nbytes)))


def _nbytes(shape, dtype):
    return int(np.prod(shape)) * jnp.dtype(dtype).itemsize


def _sig(x):
    return 1.0 / (1.0 + jnp.exp(-x))


def _log1p(u):
    w = 1.0 + u
    return jnp.where(w == 1.0, u, jnp.log(w) * (u / (w - 1.0)))


def _expm1(z):
    e = jnp.exp(z)
    le = jnp.log(e)
    return jnp.where(e == 1.0, z, jnp.where(le == 0.0, z, (e - 1.0) * (z / le)))


def _softplus(z):
    return jnp.maximum(z, 0.0) + _log1p(jnp.exp(-jnp.abs(z)))


_GELU_C = math.sqrt(2.0 / math.pi)


def _gelu(x):
    return 0.5 * x * (1.0 + jnp.tanh(_GELU_C * (x + 0.044715 * x * x * x)))


def _gelu_grad(x):
    t = jnp.tanh(_GELU_C * (x + 0.044715 * x * x * x))
    return 0.5 * (1.0 + t) + 0.5 * x * (1.0 - t * t) * _GELU_C * (1.0 + 3.0 * 0.044715 * x * x)


def _shift_down(x, d, prev8):
    n = x.shape[0]
    row8 = lax.broadcasted_iota(jnp.int32, (SLAB, x.shape[1]), 0)
    y = pltpu.roll(x, d, 0)
    top = jnp.where(row8 < d, pltpu.roll(prev8, d, 0), y[0:SLAB])
    if n == SLAB:
        return top
    return jnp.concatenate([top, y[SLAB:]], axis=0)


def _shift_up(x, d, next8):
    n = x.shape[0]
    row8 = lax.broadcasted_iota(jnp.int32, (SLAB, x.shape[1]), 0)
    y = pltpu.roll(x, n - d, 0)
    bottom = jnp.where(row8 >= SLAB - d, pltpu.roll(next8, SLAB - d, 0), y[n - SLAB:])
    if n == SLAB:
        return bottom
    return jnp.concatenate([y[:n - SLAB], bottom], axis=0)


def _slab_scan_fwd(a, b):
    row = lax.broadcasted_iota(jnp.int32, a.shape, 0)
    for k in (1, 2, 4):
        a_s = pltpu.roll(a, k, 0)
        b_s = pltpu.roll(b, k, 0)
        m = row >= k
        b = jnp.where(m, a * b_s + b, b)
        a = jnp.where(m, a * a_s, a)
    return a, b


def _slab_scan_bwd(a, b):
    row = lax.broadcasted_iota(jnp.int32, a.shape, 0)
    for k in (1, 2, 4):
        a_s = pltpu.roll(a, SLAB - k, 0)
        b_s = pltpu.roll(b, SLAB - k, 0)
        m = row < SLAB - k
        b = jnp.where(m, a * b_s + b, b)
        a = jnp.where(m, a * a_s, a)
    return a, b


_DIMS = {"nn": (((1,), (0,)), ((), ())), "nt": (((1,), (1,)), ((), ())), "tn": (((0,), (0,)), ((), ()))}


def _dot(a, b, mode="nn"):
    return lax.dot_general(a, b, _DIMS[mode], preferred_element_type=F32)


def _mm(a, b, mode, M, N, K, *, name, out_dtype=F32, tm=1024, tn=1024, tk=1024,
        a_off=(0, 0), b_off=(0, 0), add=None, epi=None, epi_ins=()):
    tm, tn, tk = min(tm, M), min(tn, N), min(tk, K)
    nk = K // tk
    grid = (M // tm, N // tn, nk)
    if mode == "nn":
        a_spec = pl.BlockSpec((tm, tk), lambda i, j, k: (i + a_off[0], k + a_off[1]))
        b_spec = pl.BlockSpec((tk, tn), lambda i, j, k: (k + b_off[0], j + b_off[1]))
    elif mode == "nt":
        a_spec = pl.BlockSpec((tm, tk), lambda i, j, k: (i + a_off[0], k + a_off[1]))
        b_spec = pl.BlockSpec((tn, tk), lambda i, j, k: (j + b_off[0], k + b_off[1]))
    else:
        a_spec = pl.BlockSpec((tk, tm), lambda i, j, k: (k + a_off[0], i + a_off[1]))
        b_spec = pl.BlockSpec((tk, tn), lambda i, j, k: (k + b_off[0], j + b_off[1]))
    o_spec = pl.BlockSpec((tm, tn), lambda i, j, k: (i, j))
    extra = ([add] if add is not None else []) + list(epi_ins)
    n_extra = len(extra)
    has_add = add is not None

    def body(*refs):
        a_ref, b_ref = refs[0], refs[1]
        ex = refs[2:2 + n_extra]
        o_ref = refs[2 + n_extra]

        def finish(acc):
            if has_add:
                acc = acc + ex[0][...].astype(F32)
            if epi is not None:
                acc = epi(acc, *[e[...] for e in ex[(1 if has_add else 0):]])
            o_ref[...] = acc.astype(o_ref.dtype)

        p = _dot(a_ref[...].astype(BF), b_ref[...].astype(BF), mode)
        if nk == 1:
            finish(p)
        else:
            acc_ref = refs[3 + n_extra]
            k = pl.program_id(2)

            @pl.when(k == 0)
            def _():
                acc_ref[...] = p

            @pl.when(k > 0)
            def _():
                acc_ref[...] += p

            @pl.when(k == nk - 1)
            def _():
                finish(acc_ref[...])

    blk = (_nbytes((tm, tk), a.dtype) + _nbytes((tk, tn), b.dtype) + _nbytes((tm, tn), out_dtype)
           + sum(_nbytes((tm, tn), e.dtype) for e in extra) + 2 * _nbytes((tm, tn), F32))
    return pl.pallas_call(
        body, name=name, grid=grid,
        in_specs=[a_spec, b_spec] + [o_spec] * n_extra,
        out_specs=o_spec,
        out_shape=jax.ShapeDtypeStruct((M, N), out_dtype),
        scratch_shapes=[pltpu.VMEM((tm, tn), F32)] if nk > 1 else [],
        compiler_params=pltpu.CompilerParams(
            dimension_semantics=("parallel", "parallel", "arbitrary"), vmem_limit_bytes=_vmem_limit(blk)),
    )(a, b, *extra)


def _ew(fn, T, tm, ins, consts, outs, accs, *, name, reverse=False):
    tm = min(tm, T)
    nt = T // tm
    n_in, n_c, n_o, n_a = len(ins), len(consts), len(outs), len(accs)

    def row(i):
        return nt - 1 - i if reverse else i

    in_specs = [pl.BlockSpec((tm, w), functools.partial(lambda i, cb: (row(i), cb), cb=cb)) for (_, w, cb) in ins]
    in_specs += [pl.BlockSpec(c.shape, functools.partial(lambda i, nd: (0,) * nd, nd=c.ndim)) for c in consts]
    out_specs = [pl.BlockSpec((tm, w), lambda i: (row(i), 0)) for (w, _) in outs]
    out_specs += [pl.BlockSpec((r, w), lambda i: (0, 0)) for (r, w) in accs]
    out_shape = [jax.ShapeDtypeStruct((T, w), dt) for (w, dt) in outs]
    out_shape += [jax.ShapeDtypeStruct((r, w), F32) for (r, w) in accs]

    def body(*refs):
        in_refs = refs[:n_in]
        c_refs = refs[n_in:n_in + n_c]
        o_refs = refs[n_in + n_c:n_in + n_c + n_o]
        a_refs = refs[n_in + n_c + n_o:]
        ov, av = fn([r[...] for r in in_refs], [r[...] for r in c_refs])
        for r, v in zip(o_refs, ov):
            r[...] = v.astype(r.dtype)
        if n_a:
            i = pl.program_id(0)

            @pl.when(i == 0)
            def _():
                for r, v in zip(a_refs, av):
                    r[...] = v

            @pl.when(i > 0)
            def _():
                for r, v in zip(a_refs, av):
                    r[...] += v

    blk = (sum(_nbytes((tm, w), a.dtype) for (a, w, _) in ins) + sum(_nbytes(c.shape, c.dtype) for c in consts)
           + sum(_nbytes((tm, w), dt) for (w, dt) in outs) + sum(_nbytes(s, F32) for s in accs))
    res = pl.pallas_call(
        body, name=name, grid=(nt,), in_specs=in_specs, out_specs=out_specs, out_shape=out_shape,
        compiler_params=pltpu.CompilerParams(
            dimension_semantics=("arbitrary",), vmem_limit_bytes=_vmem_limit(blk)),
    )(*[a for (a, _, _) in ins], *consts)
    return res


def _colsum(v):
    return jnp.sum(v, axis=0, keepdims=True)


def _fgate_fwd(fl, fb, T, tm=512):
    tm = min(tm, T)

    def body(fl_ref, fb_ref, f_ref, carry_ref):
        i = pl.program_id(0)

        @pl.when(i == 0)
        def _():
            carry_ref[...] = jnp.zeros_like(carry_ref)

        row = lax.broadcasted_iota(jnp.int32, (SLAB, DH), 0)

        def slab(s, carry):
            r0 = pl.multiple_of(s * SLAB, SLAB)
            z = fl_ref[pl.ds(r0, SLAB), :] + fb_ref[...]
            c = jnp.minimum(z, 0.0) - _log1p(jnp.exp(-jnp.abs(z)))
            for k in (1, 2, 4):
                c = c + jnp.where(row >= k, pltpu.roll(c, k, 0), 0.0)
            c = c + carry
            f_ref[pl.ds(r0, SLAB), :] = c
            return c[SLAB - 1:SLAB, :]

        carry_ref[0:1, :] = lax.fori_loop(0, tm // SLAB, slab, carry_ref[0:1, :])

    return pl.pallas_call(
        body, name="fgate_fwd", grid=(T // tm,),
        in_specs=[pl.BlockSpec((tm, DH), lambda i: (i, 0)), pl.BlockSpec((1, DH), lambda i: (0, 0))],
        out_specs=pl.BlockSpec((tm, DH), lambda i: (i, 0)),
        out_shape=jax.ShapeDtypeStruct((T, DH), F32),
        scratch_shapes=[pltpu.VMEM((SLAB, DH), F32)],
        compiler_params=pltpu.CompilerParams(dimension_semantics=("arbitrary",)),
    )(fl, fb)


def _fgate_bwd(dF, fl, fb, T, tm=512):
    tm = min(tm, T)
    nt = T // tm

    def body(df_ref, fl_ref, fb_ref, o_ref, acc_ref, carry_ref):
        i = pl.program_id(0)

        @pl.when(i == 0)
        def _():
            carry_ref[...] = jnp.zeros_like(carry_ref)
            acc_ref[...] = jnp.zeros_like(acc_ref)

        row = lax.broadcasted_iota(jnp.int32, (SLAB, DH), 0)

        def slab(n, carry):
            g_next, acc = carry
            r0 = pl.multiple_of((tm // SLAB - 1 - n) * SLAB, SLAB)
            c = df_ref[pl.ds(r0, SLAB), :]
            for k in (1, 2, 4):
                c = c + jnp.where(row < SLAB - k, pltpu.roll(c, SLAB - k, 0), 0.0)
            c = c + g_next
            z = fl_ref[pl.ds(r0, SLAB), :] + fb_ref[...]
            dfl = c * _sig(-z)
            o_ref[pl.ds(r0, SLAB), :] = dfl.astype(o_ref.dtype)
            return c[0:1, :], acc + _colsum(dfl)

        g, acc = lax.fori_loop(0, tm // SLAB, slab, (carry_ref[0:1, :], jnp.zeros((1, DH), F32)))
        carry_ref[0:1, :] = g
        acc_ref[...] += acc

    return pl.pallas_call(
        body, name="fgate_bwd", grid=(nt,),
        in_specs=[pl.BlockSpec((tm, DH), lambda i: (nt - 1 - i, 0)), pl.BlockSpec((tm, DH), lambda i: (nt - 1 - i, 0)),
                  pl.BlockSpec((1, DH), lambda i: (0, 0))],
        out_specs=[pl.BlockSpec((tm, DH), lambda i: (nt - 1 - i, 0)), pl.BlockSpec((1, DH), lambda i: (0, 0))],
        out_shape=[jax.ShapeDtypeStruct((T, DH), BF), jax.ShapeDtypeStruct((1, DH), F32)],
        scratch_shapes=[pltpu.VMEM((SLAB, DH), F32)],
        compiler_params=pltpu.CompilerParams(dimension_semantics=("arbitrary",)),
    )(dF, fl, fb)


def _conv(x, prev8, cw, cb):
    xs = [x] + [_shift_down(x, d, prev8) for d in (1, 2, 3)]
    xa = cb + cw[3:4, :] * xs[0] + cw[2:3, :] * xs[1] + cw[1:2, :] * xs[2] + cw[0:1, :] * xs[3]
    return xa, xs


def _lru_gates(xa_g, wa_g, wx_g, ba_g, bx_g, sp_g):
    xb = xa_g.astype(BF)
    r = _sig(_dot(xb, wa_g) + ba_g)
    ig = _sig(_dot(xb, wx_g) + bx_g)
    la = -LRU_C * r * sp_g
    a = jnp.exp(la)
    mult = jnp.sqrt(-_expm1(2.0 * la))
    return r, ig, a, mult


def _lru_fwd(xg, cw, vec, wabd, wxbd, T, tm=256):
    tm = min(tm, T)
    nsl = tm // SLAB

    def body(x_ref, xp_ref, cw_ref, vec_ref, wa_ref, wx_ref, h_ref, a_s, b_s, carry_ref):
        i = pl.program_id(0)

        @pl.when(i == 0)
        def _():
            carry_ref[...] = jnp.zeros_like(carry_ref)

        x = x_ref[...]
        prev8 = jnp.where(i > 0, xp_ref[...], 0.0)
        vec_v = vec_ref[...]
        xa, _ = _conv(x, prev8, cw_ref[...], vec_v[0:1, :])
        sp = _softplus(-vec_v[3:4, :])
        for g in range(NBD):
            sl = slice(g * BD, (g + 1) * BD)
            _, ig, a, mult = _lru_gates(xa[:, sl], wa_ref[g], wx_ref[g], vec_v[1:2, sl], vec_v[2:3, sl], sp[:, sl])
            a_s[:, sl] = a
            b_s[:, sl] = mult * ig * xa[:, sl]

        def slab(s, carry):
            r0 = pl.multiple_of(s * SLAB, SLAB)
            A, B = _slab_scan_fwd(a_s[pl.ds(r0, SLAB), :], b_s[pl.ds(r0, SLAB), :])
            h = A * carry + B
            h_ref[pl.ds(r0, SLAB), :] = h
            return h[SLAB - 1:SLAB, :]

        carry_ref[0:1, :] = lax.fori_loop(0, nsl, slab, carry_ref[0:1, :])

    blk = 5 * _nbytes((tm, D), F32) + 2 * _nbytes((NBD, BD, BD), BF)
    return pl.pallas_call(
        body, name="lru_fwd", grid=(T // tm,),
        in_specs=[pl.BlockSpec((tm, D), lambda i: (i, 0)),
                  pl.BlockSpec((SLAB, D), lambda i: (jnp.maximum(i * nsl - 1, 0), 0)),
                  pl.BlockSpec((CONV, D), lambda i: (0, 0)),
                  pl.BlockSpec((SLAB, D), lambda i: (0, 0)),
                  pl.BlockSpec((NBD, BD, BD), lambda i: (0, 0, 0)),
                  pl.BlockSpec((NBD, BD, BD), lambda i: (0, 0, 0))],
        out_specs=pl.BlockSpec((tm, D), lambda i: (i, 0)),
        out_shape=jax.ShapeDtypeStruct((T, D), F32),
        scratch_shapes=[pltpu.VMEM((tm, D), F32), pltpu.VMEM((tm, D), F32), pltpu.VMEM((SLAB, D), F32)],
        compiler_params=pltpu.CompilerParams(dimension_semantics=("arbitrary",), vmem_limit_bytes=_vmem_limit(blk)),
    )(xg, xg, cw, vec, wabd, wxbd)


def _lru_bwd(xg, h, dha, cw, vec, wabd, wxbd, T, tm=256):
    tm = min(tm, T)
    nsl = tm // SLAB
    nt = T // tm

    def body(x_ref, xp_ref, h_ref, hp_ref, dh_ref, cw_ref, vec_ref, wa_ref, wx_ref,
             dx_ref, dwa_ref, dwx_ref, acc_ref, a_s, b_s, g_s, dxa_s, carry_ref, dxan_ref):
        n = pl.program_id(0)
        it = nt - 1 - n

        @pl.when(n == 0)
        def _():
            carry_ref[...] = jnp.zeros_like(carry_ref)
            dxan_ref[...] = jnp.zeros_like(dxan_ref)
            dwa_ref[...] = jnp.zeros_like(dwa_ref)
            dwx_ref[...] = jnp.zeros_like(dwx_ref)
            acc_ref[...] = jnp.zeros_like(acc_ref)

        x = x_ref[...]
        prev8 = jnp.where(it > 0, xp_ref[...], 0.0)
        hprev8 = jnp.where(it > 0, hp_ref[...], 0.0)
        vec_v = vec_ref[...]
        cw_v = cw_ref[...]
        xa, xs = _conv(x, prev8, cw_v, vec_v[0:1, :])
        sp = _softplus(-vec_v[3:4, :])
        gates = []
        for g in range(NBD):
            sl = slice(g * BD, (g + 1) * BD)
            r, ig, a, mult = _lru_gates(xa[:, sl], wa_ref[g], wx_ref[g], vec_v[1:2, sl], vec_v[2:3, sl], sp[:, sl])
            gates.append((r, ig, a, mult))
            a_s[:, sl] = a
        a_next = _shift_up(a_s[...], 1, carry_ref[...])
        a_s[...] = a_next
        b_s[...] = dh_ref[...]

        def slab(m, carry):
            r0 = pl.multiple_of((nsl - 1 - m) * SLAB, SLAB)
            A, B = _slab_scan_bwd(a_s[pl.ds(r0, SLAB), :], b_s[pl.ds(r0, SLAB), :])
            gg = A * carry + B
            g_s[pl.ds(r0, SLAB), :] = gg
            return gg[0:1, :]

        g_first = lax.fori_loop(0, nsl, slab, carry_ref[1:2, :])
        gt = g_s[...]
        h_prev = _shift_down(h_ref[...], 1, hprev8)
        dba = []
        dbx = []
        dsp = []
        for g in range(NBD):
            sl = slice(g * BD, (g + 1) * BD)
            r, ig, a, mult = gates[g]
            xa_g = xa[:, sl]
            g_g = gt[:, sl]
            da = g_g * h_prev[:, sl]
            dmult = g_g * ig * xa_g
            di = g_g * mult * xa_g
            dxa_g = g_g * mult * ig
            dla = da * a - dmult * (a * a / mult)
            dr = dla * (-LRU_C) * sp[:, sl]
            dsp.append(_colsum(dla * (-LRU_C) * r))
            dra = (dr * r * (1.0 - r))
            dix = (di * ig * (1.0 - ig))
            dba.append(_colsum(dra))
            dbx.append(_colsum(dix))
            dra_b = dra.astype(BF)
            dix_b = dix.astype(BF)
            xb = xa_g.astype(BF)
            dxa_g = dxa_g + _dot(dra_b, wa_ref[g], "nt") + _dot(dix_b, wx_ref[g], "nt")
            dwa_ref[g] += _dot(xb, dra_b, "tn")
            dwx_ref[g] += _dot(xb, dix_b, "tn")
            dxa_s[:, sl] = dxa_g
        dxa = dxa_s[...]
        nxt = dxan_ref[...]
        dx = (cw_v[3:4, :] * dxa + cw_v[2:3, :] * _shift_up(dxa, 1, nxt)
              + cw_v[1:2, :] * _shift_up(dxa, 2, nxt) + cw_v[0:1, :] * _shift_up(dxa, 3, nxt))
        dx_ref[...] = dx.astype(dx_ref.dtype)
        acc_ref[0:1, :] += jnp.concatenate(dba, axis=1)
        acc_ref[1:2, :] += jnp.concatenate(dbx, axis=1)
        acc_ref[2:3, :] += jnp.concatenate(dsp, axis=1)
        acc_ref[3:4, :] += _colsum(dxa)
        for k in range(CONV):
            acc_ref[4 + k:5 + k, :] += _colsum(dxa * xs[CONV - 1 - k])
        dxan_ref[...] = dxa[0:SLAB, :]
        a_first = jnp.concatenate([gates[g][2][0:1, :] for g in range(NBD)], axis=1)
        carry_ref[0:1, :] = a_first
        carry_ref[1:2, :] = g_first

        @pl.when(n == nt - 1)
        def _():
            acc_ref[2:3, :] = acc_ref[2:3, :] * (-_sig(-vec_v[3:4, :]))

    rowblk = lambda i: (nt - 1 - i, 0)
    prevblk = lambda i: (jnp.maximum((nt - 1 - i) * nsl - 1, 0), 0)
    c2 = lambda i: (0, 0)
    c3 = lambda i: (0, 0, 0)
    blk = 12 * _nbytes((tm, D), F32) + 6 * _nbytes((NBD, BD, BD), F32)
    return pl.pallas_call(
        body, name="lru_bwd", grid=(nt,),
        in_specs=[pl.BlockSpec((tm, D), rowblk), pl.BlockSpec((SLAB, D), prevblk),
                  pl.BlockSpec((tm, D), rowblk), pl.BlockSpec((SLAB, D), prevblk),
                  pl.BlockSpec((tm, D), rowblk),
                  pl.BlockSpec((CONV, D), c2), pl.BlockSpec((SLAB, D), c2),
                  pl.BlockSpec((NBD, BD, BD), c3), pl.BlockSpec((NBD, BD, BD), c3)],
        out_specs=[pl.BlockSpec((tm, D), rowblk), pl.BlockSpec((NBD, BD, BD), c3), pl.BlockSpec((NBD, BD, BD), c3),
                   pl.BlockSpec((16, D), c2)],
        out_shape=[jax.ShapeDtypeStruct((T, D), BF), jax.ShapeDtypeStruct((NBD, BD, BD), F32),
                   jax.ShapeDtypeStruct((NBD, BD, BD), F32), jax.ShapeDtypeStruct((16, D), F32)],
        scratch_shapes=[pltpu.VMEM((tm, D), F32), pltpu.VMEM((tm, D), F32), pltpu.VMEM((tm, D), F32),
                        pltpu.VMEM((tm, D), F32), pltpu.VMEM((SLAB, D), F32), pltpu.VMEM((SLAB, D), F32)],
        compiler_params=pltpu.CompilerParams(dimension_semantics=("arbitrary",), vmem_limit_bytes=_vmem_limit(blk)),
    )(xg, xg, h, h, dha, cw, vec, wabd, wxbd)


_SCALE = 1.0 / math.sqrt(DH)


_ANY = pl.BlockSpec(memory_space=pl.ANY)


class _Side:
    def __init__(self, srcs, outs, nsem, copies):
        self.srcs, self.outs, self.nsem, self.copies = list(srcs), list(outs), nsem, copies


def _pallas(body, operands, *, name, grid, in_specs, out_specs, out_shape, scratch_shapes=(), semantics,
            vmem=None, side=None):
    if side is None:
        return pl.pallas_call(
            body, name=name, grid=grid, in_specs=in_specs, out_specs=out_specs, out_shape=out_shape,
            scratch_shapes=list(scratch_shapes),
            compiler_params=pltpu.CompilerParams(dimension_semantics=semantics, vmem_limit_bytes=vmem),
        )(*operands)
    n_in, n_out, n_scr = len(in_specs), len(out_specs), len(scratch_shapes)
    ns, no = len(side.srcs), len(side.outs)

    def hosted(*refs):
        ins, refs = refs[:n_in], refs[n_in:]
        sin, refs = refs[:ns], refs[ns:]
        outs, refs = refs[:n_out], refs[n_out:]
        sout, refs = refs[:no], refs[no:]
        scr, (send, recv) = refs[:n_scr], refs[n_scr:]
        ids = [pl.program_id(a) for a in range(len(grid))]
        first = functools.reduce(jnp.logical_and, [i == 0 for i in ids])
        last = functools.reduce(jnp.logical_and, [i == g - 1 for i, g in zip(ids, grid)])

        @pl.when(first)
        def _():
            for cp in side.copies(sin, sout, send, recv):
                cp.start()

        body(*ins, *outs, *scr)

        @pl.when(last)
        def _():
            for cp in side.copies(sin, sout, send, recv):
                cp.wait()

    return pl.pallas_call(
        hosted, name=name, grid=grid, in_specs=list(in_specs) + [_ANY] * ns, out_specs=list(out_specs) + [_ANY] * no,
        out_shape=list(out_shape) + side.outs,
        scratch_shapes=list(scratch_shapes) + [pltpu.SemaphoreType.DMA((side.nsem,)), pltpu.SemaphoreType.DMA((side.nsem,))],
        compiler_params=pltpu.CompilerParams(dimension_semantics=("arbitrary",) * len(grid), vmem_limit_bytes=vmem),
    )(*operands, *side.srcs)


DA = 2 * DH
_LOG2E = math.log2(math.e)
_C2 = _SCALE * _LOG2E


def _aug_fn(ins, cs):
    q, k, fcum = ins
    g_all = fcum * (1.0 / _SCALE)
    lane = lax.broadcasted_iota(jnp.int32, (q.shape[0], DH), 1)
    qa, ka = [], []
    for hd in range(NH):
        g = g_all[:, hd:hd + 1]
        hi = g.astype(BF).astype(F32)
        mid = (g - hi).astype(BF).astype(F32)
        lo = ((g - hi) - mid).astype(BF).astype(F32)
        qx = jnp.where(lane == 0, hi, jnp.where(lane == 1, mid, jnp.where(lane == 2, lo,
                                                                          jnp.where(lane < 6, 1.0, 0.0))))
        kx = jnp.where(lane < 3, 1.0, jnp.where(lane == 3, -hi, jnp.where(lane == 4, -mid,
                                                                          jnp.where(lane == 5, -lo, 0.0))))
        qa += [q[:, hd * DH:(hd + 1) * DH], qx.astype(BF)]
        ka += [k[:, hd * DH:(hd + 1) * DH], kx.astype(BF)]
    return [jnp.concatenate(qa, axis=1), jnp.concatenate(ka, axis=1)], []


_KA_ONES = DH + 3
KT_ONES = 16
ATTN_CHAINS = 1


def _attn_fwd(qa, ka, vt, T, blk=512, side=None):
    blk = min(blk, T)
    nb = T // blk
    bqs = blk // ATTN_CHAINS

    def body(q_ref, k_ref, v_ref, o_ref, lse_ref):
        i = pl.program_id(1)
        qs = [q_ref[pl.ds(c * bqs, bqs), :] for c in range(ATTN_CHAINS)]

        def scores(j, c):
            r0 = pl.multiple_of(j * blk, blk)
            return _dot(k_ref[pl.ds(r0, blk), :], qs[c], "nt") * _C2

        def update(s, vj, carry):
            m, l, acc = carry
            m_new = jnp.maximum(m, jnp.max(s, axis=0, keepdims=True))
            alpha = jnp.exp2(m - m_new)
            p = jnp.exp2(s - m_new)
            l = alpha * l + jnp.sum(p, axis=0, keepdims=True)
            acc = alpha * acc + _dot(vj, p.astype(BF))
            return m_new, l, acc

        def step(j, carry):
            ss, st = carry
            nxt = tuple(scores(j + 1, c) for c in range(ATTN_CHAINS))
            vj = v_ref[j]
            return nxt, tuple(update(ss[c], vj, st[c]) for c in range(ATTN_CHAINS))

        init = tuple((jnp.full((1, bqs), -jnp.inf, F32), jnp.zeros((1, bqs), F32), jnp.zeros((DH, bqs), F32))
                     for _ in range(ATTN_CHAINS))
        first = tuple(scores(0, c) for c in range(ATTN_CHAINS))
        last, carry = lax.fori_loop(0, i, step, (first, init))
        rk = lax.broadcasted_iota(jnp.int32, (blk, bqs), 0)
        cq = lax.broadcasted_iota(jnp.int32, (blk, bqs), 1)
        vi = v_ref[i]
        for c in range(ATTN_CHAINS):
            s = jnp.where(cq + c * bqs >= rk, last[c], -jnp.inf)
            m, l, acc = update(s, vi, carry[c])
            o_ref[:, c * bqs:(c + 1) * bqs] = (acc / l).astype(o_ref.dtype)
            lse_ref[:, c * bqs:(c + 1) * bqs] = m + jnp.log(l) * _LOG2E

    vm = _nbytes((T, DA), BF) + _nbytes((T, DH), BF) + 6 * _nbytes((blk, blk), F32)
    return _pallas(
        body, (qa, ka, vt), name="attn_fwd", grid=(NH, nb),
        in_specs=[pl.BlockSpec((blk, DA), lambda h, i: (i, h)),
                  pl.BlockSpec((T, DA), lambda h, i: (0, h)),
                  pl.BlockSpec((None, nb, DH, blk), lambda h, i: (h, 0, 0, 0))],
        out_specs=[pl.BlockSpec((None, None, DH, blk), lambda h, i: (h, i, 0, 0)),
                   pl.BlockSpec((None, None, 1, blk), lambda h, i: (h, i, 0, 0))],
        out_shape=[jax.ShapeDtypeStruct((NH, nb, DH, blk), BF), jax.ShapeDtypeStruct((NH, nb, 1, blk), F32)],
        semantics=("parallel", "arbitrary"), vmem=_vmem_limit(vm), side=side)


def _attn_bwd(qa, ka, kt, qkv, do, lrow, drow, T, blk=512, side=None):
    blk = min(blk, T)
    nb = T // blk

    def body(ka_ref, kt_ref, v_ref, qa_ref, do_ref, l_ref, d_ref, dq_ref, dk_ref, dv_ref, dfs_ref):
        j = pl.program_id(1)

        @pl.when(j == 0)
        def _():
            dq_ref[...] = jnp.zeros_like(dq_ref)

        row = lax.broadcasted_iota(jnp.int32, (DH + KT_ONES, blk), 0)
        dq_scale = jnp.where(row < DH, _SCALE, 1.0)

        kaj = ka_ref[...]
        ktj = kt_ref[...]
        vj = v_ref[...]

        def step(i, carry, diag):
            dka, dv = carry
            r0 = pl.multiple_of(i * blk, blk)
            qi = qa_ref[pl.ds(r0, blk), :]
            doi = do_ref[pl.ds(r0, blk), :]
            st = _dot(kaj, qi, "nt") * _C2 - l_ref[i]
            if diag:
                rk = lax.broadcasted_iota(jnp.int32, (blk, blk), 0)
                cq = lax.broadcasted_iota(jnp.int32, (blk, blk), 1)
                st = jnp.where(cq >= rk, st, -jnp.inf)
            pt = jnp.exp2(st)
            dv = dv + _dot(pt.astype(BF), doi)
            dpt = _dot(vj, doi, "nt")
            dst = pt * (dpt - d_ref[i])
            dsb = dst.astype(BF)
            dka = dka + _dot(dsb, qi)
            dq_ref[i] += _dot(ktj, dsb) * dq_scale
            return dka, dv

        init = (jnp.zeros((blk, DA), F32), jnp.zeros((blk, DH), F32))
        carry = step(j, init, True)
        dka, dv = lax.fori_loop(j + 1, nb, lambda i, c: step(i, c, False), carry)
        dk_ref[...] = (dka[:, :DH] * _SCALE).astype(dk_ref.dtype)
        dv_ref[...] = dv.astype(dv_ref.dtype)
        dfs_ref[...] = dka[:, DH:].T[_KA_ONES - DH:_KA_ONES - DH + 1, :]

    rowv = pl.BlockSpec((None, nb, 1, blk), lambda h, j: (h, 0, 0, 0))
    vm = _nbytes((T, DA), BF) + _nbytes((T, DH), BF) + _nbytes((T, DH), F32) + 8 * _nbytes((blk, blk), F32)
    return _pallas(
        body, (ka, kt, qkv, qa, do, lrow, drow), name="attn_bwd", grid=(NH, nb),
        in_specs=[pl.BlockSpec((blk, DA), lambda h, j: (j, h)),
                  pl.BlockSpec((None, None, DH + KT_ONES, blk), lambda h, j: (h, j, 0, 0)),
                  pl.BlockSpec((blk, DH), lambda h, j: (j, 2 * NH + h)),
                  pl.BlockSpec((T, DA), lambda h, j: (0, h)),
                  pl.BlockSpec((T, DH), lambda h, j: (0, h)),
                  rowv, rowv],
        out_specs=[pl.BlockSpec((None, nb, DH + KT_ONES, blk), lambda h, j: (h, 0, 0, 0)),
                   pl.BlockSpec((blk, DH), lambda h, j: (j, h)),
                   pl.BlockSpec((blk, DH), lambda h, j: (j, h)),
                   pl.BlockSpec((None, None, 1, blk), lambda h, j: (h, j, 0, 0))],
        out_shape=[jax.ShapeDtypeStruct((NH, nb, DH + KT_ONES, blk), F32), jax.ShapeDtypeStruct((T, D), BF),
                   jax.ShapeDtypeStruct((T, D), BF), jax.ShapeDtypeStruct((NH, nb, 1, blk), F32)],
        semantics=("parallel", "arbitrary"), vmem=_vmem_limit(vm), side=side)


def _norm_fn(ins, cs):
    x, = ins
    g, = cs
    r = lax.rsqrt(jnp.mean(x * x, axis=-1, keepdims=True) + EPS)
    return [x * r * g], []


def _norm_bwd_fn(ins, cs):
    x, dy, dres = ins
    g, = cs
    r = lax.rsqrt(jnp.mean(x * x, axis=-1, keepdims=True) + EPS)
    xh = x * r
    dxh = dy * g
    dx = dres + r * (dxh - xh * jnp.mean(dxh * xh, axis=-1, keepdims=True))
    return [dx], [_colsum(dy * xh)]


def _final_fn(ins, cs):
    x2, tgt = ins
    g, = cs
    r = lax.rsqrt(jnp.mean(x2 * x2, axis=-1, keepdims=True) + EPS)
    xh = x2 * r
    e = xh * g - tgt
    dy = e * (1.0 / D)
    dxh = dy * g
    dx2 = r * (dxh - xh * jnp.mean(dxh * xh, axis=-1, keepdims=True))
    return [dx2], [_colsum(0.5 * e * e * (1.0 / D)), _colsum(dy * xh)]


def _z_fn(ins, cs):
    g, h = ins
    return [_gelu(g) * h], []


def _mix_fn(ins, cs):
    gates, ya, yb = ins
    return [_sig(gates[:, :D]) * ya + _sig(gates[:, D:]) * yb], []


def _mix_bwd_fn(ins, cs):
    dmix, gates, ya, yb = ins
    ga = _sig(gates[:, :D])
    gb = _sig(gates[:, D:])
    dgates = jnp.concatenate([dmix * ya * ga * (1.0 - ga), dmix * yb * gb * (1.0 - gb)], axis=1)
    return [dmix * ga, dmix * gb, dgates], []


def _z_bwd_fn(ins, cs):
    dz, g, h = ins
    return [dz * _gelu(g), dz * h * _gelu_grad(g)], []


def _delta_fn(ins, cs):
    do, o = ins
    p = do.astype(F32) * o.astype(F32)
    lane = lax.broadcasted_iota(jnp.int32, (p.shape[0], DH), 1)
    out = jnp.zeros((p.shape[0], DH), F32)
    for hd in range(NH):
        s = jnp.sum(p[:, hd * DH:(hd + 1) * DH], axis=1, keepdims=True)
        out = jnp.where(lane == hd, s, out)
    return [out], []


def _du_all(pieces, win, T, tm=256, side=None):
    tm = min(tm, T)
    n = len(pieces)

    def body(*refs):
        w_ref, o_ref = refs[n], refs[n + 1]
        acc = None
        for (a, off), a_ref in zip(pieces, refs[:n]):
            d = _dot(a_ref[...].astype(BF), w_ref[:, off:off + a.shape[1]], "nt")
            acc = d if acc is None else acc + d
        o_ref[...] = acc

    vm = (sum(_nbytes((tm, a.shape[1]), a.dtype) for a, _ in pieces) + _nbytes(win.shape, win.dtype)
          + 2 * _nbytes((tm, D), F32))
    return _pallas(
        body, tuple(a for a, _ in pieces) + (win,), name="du_all", grid=(T // tm,),
        in_specs=[pl.BlockSpec((tm, a.shape[1]), lambda i: (i, 0)) for a, _ in pieces]
        + [pl.BlockSpec(win.shape, lambda i: (0, 0))],
        out_specs=[pl.BlockSpec((tm, D), lambda i: (i, 0))],
        out_shape=[jax.ShapeDtypeStruct((T, D), F32)],
        semantics=("arbitrary",), vmem=int(min(VMEM_CAP, 2 * vm + (4 << 20))), side=side)


def _local_step(x, tgt, w, T, blk=1024, dist=None):
    blk = min(blk, T)
    nb = T // blk
    win = w["win"]

    u, = _ew(_norm_fn, T, 512, [(x, D, 0)], [w["g_mix"]], [(D, BF)], [], name="norm_mix")
    xg = _mm(u, win, "nn", T, 2 * D, D, name="proj_lru")
    qkv = _mm(u, win, "nn", T, 3 * D, D, name="proj_qkv", out_dtype=BF, b_off=(0, 2))
    gates = _mm(u, win, "nn", T, 2 * D, D, name="proj_gates", b_off=(0, 5))
    fl = _mm(u, win, "nn", T, DH, D, name="proj_f", tn=DH, b_off=(0, 7 * D // DH))
    fcum = _fgate_fwd(fl, w["fb"], T)
    qa, ka = _ew(_aug_fn, T, 256, [(qkv, D, 0), (qkv, D, 1), (fcum, DH, 0)], [],
                 [(NH * DA, BF), (NH * DA, BF)], [], name="attn_augment")

    def heads_t(a):
        return a.reshape(nb, blk, NH, DH).transpose(2, 0, 3, 1)

    def heads_back(a):
        return a.transpose(1, 3, 0, 2).reshape(T, D)

    kt = heads_t(qkv[:, D:2 * D])
    vt = heads_t(qkv[:, 2 * D:])
    h = _lru_fwd(xg, w["cw"], w["vec"], w["wabd"], w["wxbd"], T)
    ot, lse, *landed = _attn_fwd(qa, ka, vt, T, blk, side=dist.weights_side() if dist else None)
    if dist:
        w = dict(w, **dist.weights_landed(landed))
    ob = heads_back(ot)
    z, = _ew(_z_fn, T, 512, [(xg, D, 1), (h, D, 0)], [], [(D, BF)], [], name="lru_gelu")
    ya = _mm(z, w["wa"], "nn", T, D, D, name="branch_a")
    yb = _mm(ob, w["wb"], "nn", T, D, D, name="branch_b")
    mix, = _ew(_mix_fn, T, 256, [(gates, 2 * D, 0), (ya, D, 0), (yb, D, 0)], [], [(D, BF)], [], name="mix")
    x1 = _mm(mix, w["wout"], "nn", T, D, D, name="out_proj", add=x)
    m, = _ew(_norm_fn, T, 512, [(x1, D, 0)], [w["g_mlp"]], [(D, BF)], [], name="norm_mlp")
    hh = _mm(m, w["wup"], "nn", T, FF, D, name="mlp_up", out_dtype=BF,
             epi=lambda acc: jnp.square(jnp.maximum(acc, 0.0)))
    x2 = _mm(hh, w["wdown"], "nn", T, D, FF, name="mlp_down", add=x1)
    dx2, loss_vec, dg_fin = _ew(_final_fn, T, 256, [(x2, D, 0), (tgt, D, 0)], [w["g_fin"]], [(D, F32)],
                                [(1, D), (1, D)], name="final_norm_loss")

    dhpre = _mm(dx2, w["wdown"], "nt", T, FF, D, name="mlp_down_bwd", out_dtype=BF,
                epi=lambda acc, h2: acc * (2.0 * jnp.sqrt(h2.astype(F32))), epi_ins=[hh])
    dwdown = _mm(hh, dx2, "tn", FF, D, T, name="dw_down", tk=512, out_dtype=BF)
    dwup = _mm(m, dhpre, "tn", D, FF, T, name="dw_up", tk=512, out_dtype=BF)
    dm = _mm(dhpre, w["wup"], "nt", T, D, FF, name="mlp_up_bwd")
    dx1, dg_mlp = _ew(_norm_bwd_fn, T, 256, [(x1, D, 0), (dm, D, 0), (dx2, D, 0)], [w["g_mlp"]], [(D, F32)],
                      [(1, D)], name="norm_mlp_bwd")

    dmix = _mm(dx1, w["wout"], "nt", T, D, D, name="out_proj_bwd")
    dwout = _mm(mix, dx1, "tn", D, D, T, name="dw_out", tk=512, out_dtype=BF)
    dya, dyb, dgates = _ew(_mix_bwd_fn, T, 256, [(dmix, D, 0), (gates, 2 * D, 0), (ya, D, 0), (yb, D, 0)], [],
                           [(D, BF), (D, BF), (2 * D, BF)], [], name="mix_bwd")
    dob = _mm(dyb, w["wb"], "nt", T, D, D, name="branch_b_bwd", out_dtype=BF)
    dwb = _mm(ob, dyb, "tn", D, D, T, name="dw_b", tk=512, out_dtype=BF)
    dz = _mm(dya, w["wa"], "nt", T, D, D, name="branch_a_bwd")
    dwa = _mm(z, dya, "tn", D, D, T, name="dw_a", tk=512, out_dtype=BF)
    dha, dglru = _ew(_z_bwd_fn, T, 256, [(dz, D, 0), (xg, D, 1), (h, D, 0)], [], [(D, F32), (D, BF)], [],
                     name="lru_gelu_bwd")

    delta, = _ew(_delta_fn, T, 512, [(dob, D, 0), (ob, D, 0)], [], [(DH, F32)], [], name="attn_delta")
    drow = delta[:, :NH].T.reshape(NH, nb, 1, blk)
    kt1 = jnp.concatenate([kt, jnp.ones((NH, nb, KT_ONES, blk), BF)], axis=2)
    big = dict(w_branch_a=dwa, w_branch_b=dwb, w_out=dwout, w_up=dwup, w_down=dwdown)
    side = dist.grads_side(big) if dist else None
    dqt, dk, dv, dfs, *landed = _attn_bwd(qa, ka, kt1, qkv, dob, lse, drow, T, blk, side=side)
    if dist:
        big = dist.grads_landed(side, landed)
    dq = heads_back(dqt[:, :, :DH])
    dfcum = jnp.pad((dqt[:, :, DH] - dfs[:, :, 0]).reshape(NH, T).T, ((0, 0), (0, DH - NH)))
    dfl, dfb = _fgate_bwd(dfcum, fl, w["fb"], T)

    dxl, dwabd, dwxbd, lacc = _lru_bwd(xg, h, dha, w["cw"], w["vec"], w["wabd"], w["wxbd"], T)

    dproj = ((dxl, 0), (dglru, D), (dq, 2 * D), (dk, 3 * D), (dv, 4 * D), (dgates, 5 * D), (dfl, 7 * D))
    pieces = [_mm(u, p, "tn", D, p.shape[1], T, name="dw_in_%d" % n, tk=512, out_dtype=BF)
              for n, (p, _) in enumerate(dproj)]
    pieces[-1] = pieces[-1][:, :NH]
    dwin = dict(w_in=jnp.concatenate(pieces, axis=1))
    side = dist.grads_side(dwin) if dist else None
    du, *landed = _du_all(dproj, win, T, side=side)
    big.update(dist.grads_landed(side, landed) if dist else dwin)
    dx, dg_mix = _ew(_norm_bwd_fn, T, 256, [(x, D, 0), (du, D, 0), (dx1, D, 0)], [w["g_mix"]], [(D, F32)],
                     [(1, D)], name="norm_mix_bwd")

    return dict(dx=dx, big=big, dwabd=dwabd, dwxbd=dwxbd, lacc=lacc, dfb=dfb, dg_mix=dg_mix, dg_mlp=dg_mlp,
                dg_fin=dg_fin, loss_vec=loss_vec)


def _block_diag(w):
    per = BD // LRU_BW
    w4 = w.reshape(NBD, per, LRU_BW, LRU_BW)
    out = jnp.zeros((NBD, per, LRU_BW, per, LRU_BW), w.dtype)
    for b in range(per):
        out = out.at[:, b, :, b, :].set(w4[:, b])
    return out.reshape(NBD, BD, BD)


def _block_diag_extract(wbd):
    per = BD // LRU_BW
    w5 = wbd.reshape(NBD, per, LRU_BW, per, LRU_BW)
    return jnp.stack([w5[:, b, :, b, :] for b in range(per)], axis=1).reshape(LRU_BLOCKS, LRU_BW, LRU_BW)


_ANY = pl.BlockSpec(memory_space=pl.ANY)


def _place():
    x, y, c = lax.axis_index("x"), lax.axis_index("y"), lax.axis_index("c")
    chips = [(1 - x, y), (x, 1 - y), (1 - x, 1 - y)]
    return x, y, c, chips


def _allgather_shards(shards):
    n = len(shards)

    def body(*refs):
        ins, outs = refs[:n], refs[n:2 * n]
        send_sems, recv_sems = refs[2 * n:]
        x, y, c, chips = _place()
        me = 2 * x + y
        sibling = (x, y, 1 - c)

        def remote(p, k, src, dst, to):
            return pltpu.make_async_remote_copy(src_ref=src, dst_ref=dst, send_sem=send_sems.at[p, k],
                                                recv_sem=recv_sems.at[p, k], device_id=to, device_id_type=MESH)

        sent = []
        for p in range(n):
            for k, chip in enumerate(chips):
                cp = remote(p, k, ins[p].at[c], outs[p].at[me, c], (chip[0], chip[1], c))
                cp.start()
                sent.append(cp)
        for p in range(n):
            for k, chip in enumerate(chips):
                half = outs[p].at[2 * chip[0] + chip[1], c]
                remote(p, k, half, half, sibling).wait_recv()
                fwd = remote(p, 3 + k, half, half, sibling)
                fwd.start()
                sent.append(fwd)
        for p in range(n):
            for k, chip in enumerate(chips):
                half = outs[p].at[2 * chip[0] + chip[1], 1 - c]
                remote(p, 3 + k, half, half, sibling).wait_recv()
        for cp in sent:
            cp.wait_send()

    gathered = pl.pallas_call(
        body, name="allgather_weights",
        in_specs=[_ANY] * n, out_specs=[_ANY] * n,
        out_shape=[jax.ShapeDtypeStruct((NCHIP,) + s.shape, s.dtype) for s in shards],
        scratch_shapes=[pltpu.SemaphoreType.DMA((n, 6)), pltpu.SemaphoreType.DMA((n, 6))],
    )(*shards)
    me = 2 * lax.axis_index("x") + lax.axis_index("y")
    return [lax.dynamic_update_index_in_dim(g, s, me, 0) for g, s in zip(gathered, shards)]


_LATE =["w_branch_a", "w_branch_b", "w_out", "w_up", "w_down"]
_COLUMN_CUT = ("w_in", "w_up")
N_PEERS = 7


def _shard_major(name, g):
    s = _columns_to_shards(g) if name in _COLUMN_CUT else g.reshape(NCHIP, g.shape[0] // NCHIP, g.shape[1])
    return s.reshape(NCHIP, 2, s.shape[1] // 2, s.shape[2])


class _Exchanges:
    def __init__(self, shards):
        self.shards = shards

    def weights_side(self):
        srcs = [self.shards[n] for n in _LATE]

        def copies(sin, sout, send, recv):
            x, y, c, chips = _place()
            return [pltpu.make_async_remote_copy(
                src_ref=sin[p], dst_ref=sout[p].at[2 * x + y], send_sem=send.at[3 * p + k], recv_sem=recv.at[3 * p + k],
                device_id=(chip[0], chip[1], c), device_id_type=MESH)
                for p in range(len(sin)) for k, chip in enumerate(chips)]

        return _Side(srcs, [jax.ShapeDtypeStruct((NCHIP,) + s.shape, s.dtype) for s in srcs], 3 * len(srcs), copies)

    def weights_landed(self, landed):
        me = 2 * lax.axis_index("x") + lax.axis_index("y")
        full = {n: lax.dynamic_update_index_in_dim(g, self.shards[n], me, 0) for n, g in zip(_LATE, landed)}
        return dict(wa=full["w_branch_a"].reshape(D, D), wb=full["w_branch_b"].reshape(D, D),
                    wout=full["w_out"].reshape(D, D), wup=_shards_to_columns(full["w_up"]),
                    wdown=full["w_down"].reshape(FF, D))

    def grads_side(self, grads):
        side_names = list(grads)
        srcs = [_shard_major(n, grads[n]) for n in side_names]

        def copies(sin, sout, send, recv):
            x, y, c, chips = _place()
            peers = [(x, y, 1 - c)] + [(cx, cy, c) for cx, cy in chips] + [(cx, cy, 1 - c) for cx, cy in chips]
            return [pltpu.make_async_remote_copy(
                src_ref=sin[p].at[2 * px + py, pc], dst_ref=sout[p].at[s], send_sem=send.at[N_PEERS * p + s],
                recv_sem=recv.at[N_PEERS * p + s], device_id=(px, py, pc), device_id_type=MESH)
                for p in range(len(sin)) for s, (px, py, pc) in enumerate(peers)]

        side = _Side(srcs, [jax.ShapeDtypeStruct((N_PEERS,) + s.shape[2:], s.dtype) for s in srcs],
                     N_PEERS * len(srcs), copies)
        side.names = side_names
        return side

    def grads_landed(self, side, landed):
        return {n: (own, got) for n, own, got in zip(side.names, side.srcs, landed)}


def _add8(g, got, me, c, name):
    _, _, half, cols = g.shape
    th = _row_tile(half, 2 * cols)

    def body(me_ref, c_ref, g_ref, r_ref, o_ref):
        acc = g_ref[...].astype(F32)
        for s in range(N_PEERS):
            acc = acc + r_ref[s].astype(F32)
        o_ref[...] = acc

    return pl.pallas_call(
        body, name=name,
        grid_spec=pltpu.PrefetchScalarGridSpec(
            num_scalar_prefetch=2, grid=(half // th,),
            in_specs=[pl.BlockSpec((None, None, th, cols), lambda i, me_ref, c_ref: (me_ref[0], c_ref[0], i, 0)),
                      pl.BlockSpec((N_PEERS, th, cols), lambda i, me_ref, c_ref: (0, i, 0))],
            out_specs=pl.BlockSpec((th, cols), lambda i, me_ref, c_ref: (i, 0))),
        out_shape=jax.ShapeDtypeStruct((half, cols), F32),
    )(me, c, g, got)


def _share_halves(halves):
    n = len(halves)

    def body(*refs):
        ins, outs = refs[:n], refs[n:2 * n]
        send_sems, recv_sems = refs[2 * n:]
        x, y, c, _ = _place()
        sibling = (x, y, 1 - c)
        copies = []
        for p in range(n):
            cp = pltpu.make_async_remote_copy(src_ref=ins[p], dst_ref=outs[p], send_sem=send_sems.at[p],
                                              recv_sem=recv_sems.at[p], device_id=sibling, device_id_type=MESH)
            cp.start()
            copies.append(cp)
        for cp in copies:
            cp.wait()

    return pl.pallas_call(
        body, name="reduce_share_halves",
        in_specs=[_ANY] * n, out_specs=[_ANY] * n,
        out_shape=[jax.ShapeDtypeStruct(h.shape, h.dtype) for h in halves],
        scratch_shapes=[pltpu.SemaphoreType.DMA((n,)), pltpu.SemaphoreType.DMA((n,))],
    )(*halves)


def _row_tile(half, cols):
    th = max(SLAB, min(half, (1 << 18) // cols // SLAB * SLAB))
    while half % th:
        th -= SLAB
    return th


N_DEV = 8
SMALL_ROWS = 208


def _allreduce_small(pack):
    def body(x_ref, out_ref, gbuf, send_sems, recv_sems, local_sem):
        x, y, c, chips = _place()
        me, sibling = (x, y, c), (x, y, 1 - c)

        def rows(px, py, pc):
            return gbuf.at[4 * px + 2 * py + pc]

        def copy(k, block, to, src=None):
            return pltpu.make_async_remote_copy(
                src_ref=rows(*block) if src is None else src, dst_ref=rows(*block),
                send_sem=send_sems.at[k], recv_sem=recv_sems.at[k], device_id=to, device_id_type=MESH)

        mine = pltpu.make_async_copy(x_ref, rows(*me), local_sem)
        mine.start()
        first = [copy(0, me, sibling, src=x_ref)]
        first += [copy(1 + j, me, (chip[0], chip[1], c), src=x_ref) for j, chip in enumerate(chips)]
        for cp in first:
            cp.start()
        passed = [copy(4 + j, (chip[0], chip[1], c), sibling) for j, chip in enumerate(chips)]
        for j, chip in enumerate(chips):
            copy(1 + j, (chip[0], chip[1], c), me).wait_recv()
            passed[j].start()
        copy(0, sibling, me).wait_recv()
        for j, chip in enumerate(chips):
            copy(4 + j, (chip[0], chip[1], 1 - c), me).wait_recv()
        for cp in first + passed:
            cp.wait_send()
        mine.wait()
        acc = gbuf[0]
        for d in range(1, N_DEV):
            acc = acc + gbuf[d]
        out_ref[...] = acc

    return pl.pallas_call(
        body, name="allreduce_small",
        in_specs=[pl.BlockSpec(memory_space=pltpu.VMEM)],
        out_specs=pl.BlockSpec(memory_space=pltpu.VMEM),
        out_shape=jax.ShapeDtypeStruct((SMALL_ROWS, D), F32),
        scratch_shapes=[pltpu.VMEM((N_DEV, SMALL_ROWS, D), F32), pltpu.SemaphoreType.DMA((7,)),
                        pltpu.SemaphoreType.DMA((7,)), pltpu.SemaphoreType.DMA],
    )(pack)


def _adamw(w, g, m, v, name):
    rows, cols = w.shape
    th = _row_tile(rows, cols)

    def body(w_ref, g_ref, m_ref, v_ref, d_ref, mo_ref, vo_ref):
        gv = g_ref[...]
        mn = ADAM_B1 * m_ref[...] + (1.0 - ADAM_B1) * gv
        vn = ADAM_B2 * v_ref[...] + (1.0 - ADAM_B2) * (gv * gv)
        m_hat = mn / (1.0 - ADAM_B1 ** ADAM_STEP)
        v_hat = vn / (1.0 - ADAM_B2 ** ADAM_STEP)
        d_ref[...] = -ADAM_LR * (m_hat / (jnp.sqrt(v_hat) + ADAM_EPS) + ADAM_WD * w_ref[...])
        mo_ref[...] = mn
        vo_ref[...] = vn

    spec = pl.BlockSpec((th, cols), lambda i: (i, 0))
    return pl.pallas_call(
        body, name=name, grid=(rows // th,),
        in_specs=[spec] * 4, out_specs=[spec] * 3,
        out_shape=[jax.ShapeDtypeStruct((rows, cols), F32)] * 3,
        compiler_params=pltpu.CompilerParams(dimension_semantics=("parallel",)),
    )(w, g, m, v)


_SMALL = ["norm_mix_g", "norm_mlp_g", "norm_final_g", "conv_b", "lru_ba", "lru_bx", "lru_lambda"]
_ROW_FB, _ROW_CW, _ROW_WA, _ROW_WX, _ROW_LOSS = 56, 64, 72, 136, 200


def _pack_small(vals, col0):
    def slab(a):
        return jnp.pad(a, ((0, -a.shape[0] % SLAB), (0, D - a.shape[1])))

    rows = [slab(vals[n].reshape(1, D)) for n in _SMALL]
    rows.append(slab(vals["forget_b"].reshape(1, NH)))
    if vals["conv_w"].shape[1] == D:
        rows.append(slab(vals["conv_w"]))
    else:
        rows.append(slab(lax.dynamic_update_slice(jnp.zeros((CONV, D), F32), vals["conv_w"], (0, col0))))
    rows.append(vals["lru_wa"].reshape(LRU_BLOCKS * LRU_BW * LRU_BW // D, D))
    rows.append(vals["lru_wx"].reshape(LRU_BLOCKS * LRU_BW * LRU_BW // D, D))
    rows.append(slab(vals["loss"]) if "loss" in vals else jnp.zeros((SLAB, D), F32))
    return jnp.concatenate(rows, axis=0)


def _unpack_small(pack, col0):
    out = {n: pack[SLAB * i] for i, n in enumerate(_SMALL)}
    out["forget_b"] = pack[_ROW_FB, :NH]
    out["conv_w"] = lax.dynamic_slice(pack[_ROW_CW:_ROW_CW + CONV], (0, col0), (CONV, D // NCHIP))
    out["lru_wa"] = pack[_ROW_WA:_ROW_WX].reshape(LRU_BLOCKS, LRU_BW, LRU_BW)
    out["lru_wx"] = pack[_ROW_WX:_ROW_LOSS].reshape(LRU_BLOCKS, LRU_BW, LRU_BW)
    return out


_WEIGHTS = ["norm_mix_g", "w_in", "conv_w", "conv_b", "lru_wa", "lru_ba", "lru_wx", "lru_bx", "lru_lambda",
            "forget_b", "w_branch_a", "w_branch_b", "w_out", "norm_mlp_g", "w_up", "w_down", "norm_final_g"]
_BIG = ["w_in", "w_branch_a", "w_branch_b", "w_out", "w_up", "w_down"]


def _halves(a):
    return a.reshape(2, a.shape[0] // 2, a.shape[1])


def _columns_to_shards(a):
    rows, cols = a.shape[0], a.shape[1] // NCHIP
    return jnp.transpose(a.reshape(rows, NCHIP, cols), (1, 0, 2))


def _shards_to_columns(a):
    n, rows, cols = a.shape
    return jnp.transpose(a, (1, 0, 2)).reshape(rows, n * cols)


def kernel(x, norm_mix_g, w_in, conv_w, conv_b, lru_wa, lru_ba, lru_wx, lru_bx, lru_lambda, forget_b, w_branch_a, w_branch_b, w_out, norm_mlp_g, w_up, w_down, norm_final_g, loss_target, m_norm_mix_g, m_w_in, m_conv_w, m_conv_b, m_lru_wa, m_lru_ba, m_lru_wx, m_lru_bx, m_lru_lambda, m_forget_b, m_w_branch_a, m_w_branch_b, m_w_out, m_norm_mlp_g, m_w_up, m_w_down, m_norm_final_g, v_norm_mix_g, v_w_in, v_conv_w, v_conv_b, v_lru_wa, v_lru_ba, v_lru_wx, v_lru_bx, v_lru_lambda, v_forget_b, v_w_branch_a, v_w_branch_b, v_w_out, v_norm_mlp_g, v_w_up, v_w_down, v_norm_final_g):
    args = dict(locals())
    wts = {n: args[n] for n in _WEIGHTS}
    mom = {n: args["m_" + n] for n in _WEIGHTS}
    var = {n: args["v_" + n] for n in _WEIGHTS}
    T = x.shape[1]
    xi, yi, ci = lax.axis_index("x"), lax.axis_index("y"), lax.axis_index("c")
    me = 2 * xi + yi
    c1 = jnp.reshape(ci, (1,)).astype(jnp.int32)
    me1 = jnp.reshape(me, (1,)).astype(jnp.int32)
    col0 = me * (D // NCHIP)

    cw_pad = jnp.pad(conv_w, ((0, 4 * SLAB - CONV), (0, 0)))
    g_in, g_cw = _allgather_shards([_halves(w_in.astype(BF)), _halves(cw_pad)])
    cin = DIN // NCHIP
    win = _shards_to_columns(g_in.reshape(NCHIP, D, cin))
    w = dict(
        win=jnp.pad(win, ((0, 0), (0, DINP - DIN))),
        cw=_shards_to_columns(g_cw.reshape(NCHIP, 4 * SLAB, D // NCHIP)[:, :CONV]),
        vec=jnp.concatenate([conv_b[None], lru_ba[None], lru_bx[None], lru_lambda[None],
                             jnp.zeros((SLAB - 4, D), F32)], axis=0),
        fb=jnp.pad(forget_b[None], ((0, 0), (0, DH - NH))),
        wabd=_block_diag(lru_wa).astype(BF), wxbd=_block_diag(lru_wx).astype(BF),
        g_mix=norm_mix_g[None], g_mlp=norm_mlp_g[None], g_fin=norm_final_g[None])

    r = _local_step(x[0], loss_target[0], w, T, dist=_Exchanges({n: wts[n].astype(BF) for n in _LATE}))

    halves = [_add8(*r["big"][n], me1, c1, "add8_" + n) for n in _BIG]
    theirs = _share_halves(halves)
    low = ci == 0
    gsum = {n: jnp.concatenate([jnp.where(low, h, t), jnp.where(low, t, h)], axis=0)
            for n, h, t in zip(_BIG, halves, theirs)}
    lacc = r["lacc"]
    small = dict(norm_mix_g=r["dg_mix"], norm_mlp_g=r["dg_mlp"], norm_final_g=r["dg_fin"], conv_b=lacc[3],
                 lru_ba=lacc[0], lru_bx=lacc[1], lru_lambda=lacc[2], forget_b=r["dfb"][0, :NH],
                 conv_w=lacc[4:4 + CONV], lru_wa=_block_diag_extract(r["dwabd"]),
                 lru_wx=_block_diag_extract(r["dwxbd"]), loss=r["loss_vec"])
    gpack = _allreduce_small(_pack_small(small, col0))
    loss = jnp.sum(gpack[_ROW_LOSS])

    grads, delta, new_m, new_v = {}, {}, {}, {}
    for n in _BIG:
        grads[n] = gsum[n]
        delta[n], new_m[n], new_v[n] = _adamw(wts[n], gsum[n], mom[n], var[n], "adamw_" + n)
    dp, mp, vp = _adamw(_pack_small(wts, col0), gpack, _pack_small(mom, col0), _pack_small(var, col0), "adamw_small")
    for dst, pack in ((grads, gpack), (delta, dp), (new_m, mp), (new_v, vp)):
        dst.update(_unpack_small(pack, col0))
    return (loss, r["dx"][None], *[grads[n] for n in _WEIGHTS], *[delta[n] for n in _WEIGHTS],
            *[new_m[n] for n in _WEIGHTS], *[new_v[n] for n in _WEIGHTS])
```

```python
import functools
import math

import jax
import jax.numpy as jnp
import numpy as np
from jax import lax
from jax.experimental import pallas as pl
from jax.experimental.pallas import tpu as pltpu

F32 = jnp.float32
BF = jnp.bfloat16

D = 1024
NH = 8
DH = 128
FF = 4096
CONV = 4
LRU_BLOCKS = 16
LRU_BW = 64
BD = 256
NBD = D // BD
LRU_C = 8.0
EPS = 1e-6
DIN = 7176
DINP = 7296
NCHIP = 4
SLAB = 8
VMEM_CAP = 60 * 1024 * 1024

ADAM_LR = 0.001
ADAM_B1 = 0.9
ADAM_B2 = 0.999
ADAM_EPS = 1e-08
ADAM_WD = 0.01
ADAM_STEP = 10

MESH = pl.DeviceIdType.MESH


def _vmem_limit(nbytes):
    return int(min(VMEM_CAP, max(32 * 1024 * 1024, 3 * nbytes)))


def _nbytes(shape, dtype):
    return int(np.prod(shape)) * jnp.dtype(dtype).itemsize


def _sig(x):
    return 0.5 * jnp.tanh(0.5 * x) + 0.5


def _log1p(u):
    w = 1.0 + u
    return jnp.where(w == 1.0, u, jnp.log(w) * (u / (w - 1.0)))


def _one_minus_sq(a, la):
    z = 2.0 * la
    series = -z * (1.0 + z * (0.5 + z * ((1.0 / 6.0) + z * ((1.0 / 24.0) + z * (1.0 / 120.0)))))
    return jnp.where(z > -0.125, series, 1.0 - a * a)


def _softplus(z):
    return jnp.maximum(z, 0.0) + _log1p(jnp.exp(-jnp.abs(z)))


_GELU_C = math.sqrt(2.0 / math.pi)


def _gelu(x):
    return 0.5 * x * (1.0 + jnp.tanh(_GELU_C * (x + 0.044715 * x * x * x)))


def _gelu_grad(x):
    t = jnp.tanh(_GELU_C * (x + 0.044715 * x * x * x))
    return 0.5 * (1.0 + t) + 0.5 * x * (1.0 - t * t) * _GELU_C * (1.0 + 3.0 * 0.044715 * x * x)


def _shift_down(x, d, prev8):
    n = x.shape[0]
    row8 = lax.broadcasted_iota(jnp.int32, (SLAB, x.shape[1]), 0)
    y = pltpu.roll(x, d, 0)
    top = jnp.where(row8 < d, pltpu.roll(prev8, d, 0), y[0:SLAB])
    if n == SLAB:
        return top
    return jnp.concatenate([top, y[SLAB:]], axis=0)


def _shift_up(x, d, next8):
    n = x.shape[0]
    row8 = lax.broadcasted_iota(jnp.int32, (SLAB, x.shape[1]), 0)
    y = pltpu.roll(x, n - d, 0)
    bottom = jnp.where(row8 >= SLAB - d, pltpu.roll(next8, SLAB - d, 0), y[n - SLAB:])
    if n == SLAB:
        return bottom
    return jnp.concatenate([y[:n - SLAB], bottom], axis=0)


def _slab_scan_fwd(a, b):
    row = lax.broadcasted_iota(jnp.int32, a.shape, 0)
    for k in (1, 2, 4):
        a_s = pltpu.roll(a, k, 0)
        b_s = pltpu.roll(b, k, 0)
        m = row >= k
        b = jnp.where(m, a * b_s + b, b)
        a = jnp.where(m, a * a_s, a)
    return a, b


def _slab_scan_bwd(a, b):
    row = lax.broadcasted_iota(jnp.int32, a.shape, 0)
    for k in (1, 2, 4):
        a_s = pltpu.roll(a, SLAB - k, 0)
        b_s = pltpu.roll(b, SLAB - k, 0)
        m = row < SLAB - k
        b = jnp.where(m, a * b_s + b, b)
        a = jnp.where(m, a * a_s, a)
    return a, b


_DIMS = {"nn": (((1,), (0,)), ((), ())), "nt": (((1,), (1,)), ((), ())), "tn": (((0,), (0,)), ((), ()))}


def _dot(a, b, mode="nn"):
    return lax.dot_general(a, b, _DIMS[mode], preferred_element_type=F32)


def _mm(a, b, mode, M, N, K, *, name, out_dtype=F32, tm=1024, tn=1024, tk=1024,
        a_off=(0, 0), b_off=(0, 0), add=None, epi=None, epi_ins=()):
    tm, tn, tk = min(tm, M), min(tn, N), min(tk, K)
    nk = K // tk
    grid = (M // tm, N // tn, nk)
    if mode == "nn":
        a_spec = pl.BlockSpec((tm, tk), lambda i, j, k: (i + a_off[0], k + a_off[1]))
        b_spec = pl.BlockSpec((tk, tn), lambda i, j, k: (k + b_off[0], j + b_off[1]))
    elif mode == "nt":
        a_spec = pl.BlockSpec((tm, tk), lambda i, j, k: (i + a_off[0], k + a_off[1]))
        b_spec = pl.BlockSpec((tn, tk), lambda i, j, k: (j + b_off[0], k + b_off[1]))
    else:
        a_spec = pl.BlockSpec((tk, tm), lambda i, j, k: (k + a_off[0], i + a_off[1]))
        b_spec = pl.BlockSpec((tk, tn), lambda i, j, k: (k + b_off[0], j + b_off[1]))
    o_spec = pl.BlockSpec((tm, tn), lambda i, j, k: (i, j))
    extra = ([add] if add is not None else []) + list(epi_ins)
    n_extra = len(extra)
    has_add = add is not None

    def body(*refs):
        a_ref, b_ref = refs[0], refs[1]
        ex = refs[2:2 + n_extra]
        o_ref = refs[2 + n_extra]

        def finish(acc):
            if has_add:
                acc = acc + ex[0][...].astype(F32)
            if epi is not None:
                acc = epi(acc, *[e[...] for e in ex[(1 if has_add else 0):]])
            o_ref[...] = acc.astype(o_ref.dtype)

        p = _dot(a_ref[...].astype(BF), b_ref[...].astype(BF), mode)
        if nk == 1:
            finish(p)
        else:
            acc_ref = refs[3 + n_extra]
            k = pl.program_id(2)

            @pl.when(k == 0)
            def _():
                acc_ref[...] = p

            @pl.when(k > 0)
            def _():
                acc_ref[...] += p

            @pl.when(k == nk - 1)
            def _():
                finish(acc_ref[...])

    blk = (_nbytes((tm, tk), a.dtype) + _nbytes((tk, tn), b.dtype) + _nbytes((tm, tn), out_dtype)
           + sum(_nbytes((tm, tn), e.dtype) for e in extra) + 2 * _nbytes((tm, tn), F32))
    return pl.pallas_call(
        body, name=name, grid=grid,
        in_specs=[a_spec, b_spec] + [o_spec] * n_extra,
        out_specs=o_spec,
        out_shape=jax.ShapeDtypeStruct((M, N), out_dtype),
        scratch_shapes=[pltpu.VMEM((tm, tn), F32)] if nk > 1 else [],
        compiler_params=pltpu.CompilerParams(
            dimension_semantics=("parallel", "parallel", "arbitrary"), vmem_limit_bytes=_vmem_limit(blk)),
    )(a, b, *extra)


def _ew(fn, T, tm, ins, consts, outs, accs, *, name, reverse=False):
    tm = min(tm, T)
    nt = T // tm
    n_in, n_c, n_o, n_a = len(ins), len(consts), len(outs), len(accs)

    def row(i):
        return nt - 1 - i if reverse else i

    in_specs = [pl.BlockSpec((tm, w), functools.partial(lambda i, cb: (row(i), cb), cb=cb)) for (_, w, cb) in ins]
    in_specs += [pl.BlockSpec(c.shape, functools.partial(lambda i, nd: (0,) * nd, nd=c.ndim)) for c in consts]
    out_specs = [pl.BlockSpec((tm, w), lambda i: (row(i), 0)) for (w, _) in outs]
    out_specs += [pl.BlockSpec((r, w), lambda i: (0, 0)) for (r, w) in accs]
    out_shape = [jax.ShapeDtypeStruct((T, w), dt) for (w, dt) in outs]
    out_shape += [jax.ShapeDtypeStruct((r, w), F32) for (r, w) in accs]

    def body(*refs):
        in_refs = refs[:n_in]
        c_refs = refs[n_in:n_in + n_c]
        o_refs = refs[n_in + n_c:n_in + n_c + n_o]
        a_refs = refs[n_in + n_c + n_o:]
        ov, av = fn([r[...] for r in in_refs], [r[...] for r in c_refs])
        for r, v in zip(o_refs, ov):
            r[...] = v.astype(r.dtype)
        if n_a:
            i = pl.program_id(0)

            @pl.when(i == 0)
            def _():
                for r, v in zip(a_refs, av):
                    r[...] = v

            @pl.when(i > 0)
            def _():
                for r, v in zip(a_refs, av):
                    r[...] += v

    blk = (sum(_nbytes((tm, w), a.dtype) for (a, w, _) in ins) + sum(_nbytes(c.shape, c.dtype) for c in consts)
           + sum(_nbytes((tm, w), dt) for (w, dt) in outs) + sum(_nbytes(s, F32) for s in accs))
    res = pl.pallas_call(
        body, name=name, grid=(nt,), in_specs=in_specs, out_specs=out_specs, out_shape=out_shape,
        compiler_params=pltpu.CompilerParams(
            dimension_semantics=("arbitrary",), vmem_limit_bytes=_vmem_limit(blk)),
    )(*[a for (a, _, _) in ins], *consts)
    return res


def _colsum(v):
    return jnp.sum(v, axis=0, keepdims=True)


def _fgate_fwd(fl, fb, T, tm=512):
    tm = min(tm, T)

    def body(fl_ref, fb_ref, f_ref, carry_ref):
        i = pl.program_id(0)

        @pl.when(i == 0)
        def _():
            carry_ref[...] = jnp.zeros_like(carry_ref)

        row = lax.broadcasted_iota(jnp.int32, (SLAB, DH), 0)

        def slab(s, carry):
            r0 = pl.multiple_of(s * SLAB, SLAB)
            z = fl_ref[pl.ds(r0, SLAB), :] + fb_ref[...]
            c = jnp.minimum(z, 0.0) - _log1p(jnp.exp(-jnp.abs(z)))
            for k in (1, 2, 4):
                c = c + jnp.where(row >= k, pltpu.roll(c, k, 0), 0.0)
            c = c + carry
            f_ref[pl.ds(r0, SLAB), :] = c
            return c[SLAB - 1:SLAB, :]

        carry_ref[0:1, :] = lax.fori_loop(0, tm // SLAB, slab, carry_ref[0:1, :])

    return pl.pallas_call(
        body, name="fgate_fwd", grid=(T // tm,),
        in_specs=[pl.BlockSpec((tm, DH), lambda i: (i, 0)), pl.BlockSpec((1, DH), lambda i: (0, 0))],
        out_specs=pl.BlockSpec((tm, DH), lambda i: (i, 0)),
        out_shape=jax.ShapeDtypeStruct((T, DH), F32),
        scratch_shapes=[pltpu.VMEM((SLAB, DH), F32)],
        compiler_params=pltpu.CompilerParams(dimension_semantics=("arbitrary",)),
    )(fl, fb)


def _fgate_bwd(dF, fl, fb, T, tm=512):
    tm = min(tm, T)
    nt = T // tm

    def body(df_ref, fl_ref, fb_ref, o_ref, acc_ref, carry_ref):
        i = pl.program_id(0)

        @pl.when(i == 0)
        def _():
            carry_ref[...] = jnp.zeros_like(carry_ref)
            acc_ref[...] = jnp.zeros_like(acc_ref)

        row = lax.broadcasted_iota(jnp.int32, (SLAB, DH), 0)

        def slab(n, carry):
            g_next, acc = carry
            r0 = pl.multiple_of((tm // SLAB - 1 - n) * SLAB, SLAB)
            c = df_ref[pl.ds(r0, SLAB), :]
            for k in (1, 2, 4):
                c = c + jnp.where(row < SLAB - k, pltpu.roll(c, SLAB - k, 0), 0.0)
            c = c + g_next
            z = fl_ref[pl.ds(r0, SLAB), :] + fb_ref[...]
            dfl = c * _sig(-z)
            o_ref[pl.ds(r0, SLAB), :] = dfl.astype(o_ref.dtype)
            return c[0:1, :], acc + _colsum(dfl)

        g, acc = lax.fori_loop(0, tm // SLAB, slab, (carry_ref[0:1, :], jnp.zeros((1, DH), F32)))
        carry_ref[0:1, :] = g
        acc_ref[...] += acc

    return pl.pallas_call(
        body, name="fgate_bwd", grid=(nt,),
        in_specs=[pl.BlockSpec((tm, DH), lambda i: (nt - 1 - i, 0)), pl.BlockSpec((tm, DH), lambda i: (nt - 1 - i, 0)),
                  pl.BlockSpec((1, DH), lambda i: (0, 0))],
        out_specs=[pl.BlockSpec((tm, DH), lambda i: (nt - 1 - i, 0)), pl.BlockSpec((1, DH), lambda i: (0, 0))],
        out_shape=[jax.ShapeDtypeStruct((T, DH), BF), jax.ShapeDtypeStruct((1, DH), F32)],
        scratch_shapes=[pltpu.VMEM((SLAB, DH), F32)],
        compiler_params=pltpu.CompilerParams(dimension_semantics=("arbitrary",)),
    )(dF, fl, fb)


def _conv(x, prev8, cw, cb):
    xs = [x] + [_shift_down(x, d, prev8) for d in (1, 2, 3)]
    xa = cb + cw[3:4, :] * xs[0] + cw[2:3, :] * xs[1] + cw[1:2, :] * xs[2] + cw[0:1, :] * xs[3]
    return xa, xs


def _lru_gates(xa_g, wa_g, wx_g, ba_g, bx_g, sp_g):
    xb = xa_g.astype(BF)
    r = _sig(_dot(xb, wa_g) + ba_g)
    ig = _sig(_dot(xb, wx_g) + bx_g)
    la = -LRU_C * r * sp_g
    a = jnp.exp(la)
    mult = jnp.sqrt(_one_minus_sq(a, la))
    return r, ig, a, mult


def _lru_fwd(xg, cw, vec, wabd, wxbd, T, tm=256):
    tm = min(tm, T)
    nsl = tm // SLAB

    def body(x_ref, xp_ref, cw_ref, vec_ref, wa_ref, wx_ref, h_ref, a_s, b_s, carry_ref):
        i = pl.program_id(0)

        @pl.when(i == 0)
        def _():
            carry_ref[...] = jnp.zeros_like(carry_ref)

        x = x_ref[...]
        prev8 = jnp.where(i > 0, xp_ref[...], 0.0)
        vec_v = vec_ref[...]
        xa, _ = _conv(x, prev8, cw_ref[...], vec_v[0:1, :])
        sp = _softplus(-vec_v[3:4, :])
        for g in range(NBD):
            sl = slice(g * BD, (g + 1) * BD)
            _, ig, a, mult = _lru_gates(xa[:, sl], wa_ref[g], wx_ref[g], vec_v[1:2, sl], vec_v[2:3, sl], sp[:, sl])
            a_s[:, sl] = a
            b_s[:, sl] = mult * ig * xa[:, sl]

        def slab(s, carry):
            r0 = pl.multiple_of(s * SLAB, SLAB)
            A, B = _slab_scan_fwd(a_s[pl.ds(r0, SLAB), :], b_s[pl.ds(r0, SLAB), :])
            h = A * carry + B
            h_ref[pl.ds(r0, SLAB), :] = h
            return h[SLAB - 1:SLAB, :]

        carry_ref[0:1, :] = lax.fori_loop(0, nsl, slab, carry_ref[0:1, :])

    blk = 5 * _nbytes((tm, D), F32) + 2 * _nbytes((NBD, BD, BD), BF)
    return pl.pallas_call(
        body, name="lru_fwd", grid=(T // tm,),
        in_specs=[pl.BlockSpec((tm, D), lambda i: (i, 0)),
                  pl.BlockSpec((SLAB, D), lambda i: (jnp.maximum(i * nsl - 1, 0), 0)),
                  pl.BlockSpec((CONV, D), lambda i: (0, 0)),
                  pl.BlockSpec((SLAB, D), lambda i: (0, 0)),
                  pl.BlockSpec((NBD, BD, BD), lambda i: (0, 0, 0)),
                  pl.BlockSpec((NBD, BD, BD), lambda i: (0, 0, 0))],
        out_specs=pl.BlockSpec((tm, D), lambda i: (i, 0)),
        out_shape=jax.ShapeDtypeStruct((T, D), F32),
        scratch_shapes=[pltpu.VMEM((tm, D), F32), pltpu.VMEM((tm, D), F32), pltpu.VMEM((SLAB, D), F32)],
        compiler_params=pltpu.CompilerParams(dimension_semantics=("arbitrary",), vmem_limit_bytes=_vmem_limit(blk)),
    )(xg, xg, cw, vec, wabd, wxbd)


def _lru_bwd(xg, h, dha, cw, vec, wabd, wxbd, T, tm=256):
    tm = min(tm, T)
    nsl = tm // SLAB
    nt = T // tm

    def body(x_ref, xp_ref, h_ref, hp_ref, dh_ref, cw_ref, vec_ref, wa_ref, wx_ref,
             dx_ref, dwa_ref, dwx_ref, acc_ref, a_s, b_s, g_s, dxa_s, carry_ref, dxan_ref):
        n = pl.program_id(0)
        it = nt - 1 - n

        @pl.when(n == 0)
        def _():
            carry_ref[...] = jnp.zeros_like(carry_ref)
            dxan_ref[...] = jnp.zeros_like(dxan_ref)
            dwa_ref[...] = jnp.zeros_like(dwa_ref)
            dwx_ref[...] = jnp.zeros_like(dwx_ref)
            acc_ref[...] = jnp.zeros_like(acc_ref)

        x = x_ref[...]
        prev8 = jnp.where(it > 0, xp_ref[...], 0.0)
        hprev8 = jnp.where(it > 0, hp_ref[...], 0.0)
        vec_v = vec_ref[...]
        cw_v = cw_ref[...]
        xa, xs = _conv(x, prev8, cw_v, vec_v[0:1, :])
        sp = _softplus(-vec_v[3:4, :])
        gates = []
        for g in range(NBD):
            sl = slice(g * BD, (g + 1) * BD)
            r, ig, a, mult = _lru_gates(xa[:, sl], wa_ref[g], wx_ref[g], vec_v[1:2, sl], vec_v[2:3, sl], sp[:, sl])
            gates.append((r, ig, a, mult))
            a_s[:, sl] = a
        a_next = _shift_up(a_s[...], 1, carry_ref[...])
        a_s[...] = a_next
        b_s[...] = dh_ref[...]

        def slab(m, carry):
            r0 = pl.multiple_of((nsl - 1 - m) * SLAB, SLAB)
            A, B = _slab_scan_bwd(a_s[pl.ds(r0, SLAB), :], b_s[pl.ds(r0, SLAB), :])
            gg = A * carry + B
            g_s[pl.ds(r0, SLAB), :] = gg
            return gg[0:1, :]

        g_first = lax.fori_loop(0, nsl, slab, carry_ref[1:2, :])
        gt = g_s[...]
        h_prev = _shift_down(h_ref[...], 1, hprev8)
        dba = []
        dbx = []
        dsp = []
        for g in range(NBD):
            sl = slice(g * BD, (g + 1) * BD)
            r, ig, a, mult = gates[g]
            xa_g = xa[:, sl]
            g_g = gt[:, sl]
            da = g_g * h_prev[:, sl]
            dmult = g_g * ig * xa_g
            di = g_g * mult * xa_g
            dxa_g = g_g * mult * ig
            dla = da * a - dmult * (a * a / mult)
            dr = dla * (-LRU_C) * sp[:, sl]
            dsp.append(_colsum(dla * (-LRU_C) * r))
            dra = (dr * r * (1.0 - r))
            dix = (di * ig * (1.0 - ig))
            dba.append(_colsum(dra))
            dbx.append(_colsum(dix))
            dra_b = dra.astype(BF)
            dix_b = dix.astype(BF)
            xb = xa_g.astype(BF)
            dxa_g = dxa_g + _dot(dra_b, wa_ref[g], "nt") + _dot(dix_b, wx_ref[g], "nt")
            dwa_ref[g] += _dot(xb, dra_b, "tn")
            dwx_ref[g] += _dot(xb, dix_b, "tn")
            dxa_s[:, sl] = dxa_g
        dxa = dxa_s[...]
        nxt = dxan_ref[...]
        dx = (cw_v[3:4, :] * dxa + cw_v[2:3, :] * _shift_up(dxa, 1, nxt)
              + cw_v[1:2, :] * _shift_up(dxa, 2, nxt) + cw_v[0:1, :] * _shift_up(dxa, 3, nxt))
        dx_ref[...] = dx.astype(dx_ref.dtype)
        acc_ref[0:1, :] += jnp.concatenate(dba, axis=1)
        acc_ref[1:2, :] += jnp.concatenate(dbx, axis=1)
        acc_ref[2:3, :] += jnp.concatenate(dsp, axis=1)
        acc_ref[3:4, :] += _colsum(dxa)
        for k in range(CONV):
            acc_ref[4 + k:5 + k, :] += _colsum(dxa * xs[CONV - 1 - k])
        dxan_ref[...] = dxa[0:SLAB, :]
        a_first = jnp.concatenate([gates[g][2][0:1, :] for g in range(NBD)], axis=1)
        carry_ref[0:1, :] = a_first
        carry_ref[1:2, :] = g_first

        @pl.when(n == nt - 1)
        def _():
            acc_ref[2:3, :] = acc_ref[2:3, :] * (-_sig(-vec_v[3:4, :]))

    rowblk = lambda i: (nt - 1 - i, 0)
    prevblk = lambda i: (jnp.maximum((nt - 1 - i) * nsl - 1, 0), 0)
    c2 = lambda i: (0, 0)
    c3 = lambda i: (0, 0, 0)
    blk = 12 * _nbytes((tm, D), F32) + 6 * _nbytes((NBD, BD, BD), F32)
    return pl.pallas_call(
        body, name="lru_bwd", grid=(nt,),
        in_specs=[pl.BlockSpec((tm, D), rowblk), pl.BlockSpec((SLAB, D), prevblk),
                  pl.BlockSpec((tm, D), rowblk), pl.BlockSpec((SLAB, D), prevblk),
                  pl.BlockSpec((tm, D), rowblk),
                  pl.BlockSpec((CONV, D), c2), pl.BlockSpec((SLAB, D), c2),
                  pl.BlockSpec((NBD, BD, BD), c3), pl.BlockSpec((NBD, BD, BD), c3)],
        out_specs=[pl.BlockSpec((tm, D), rowblk), pl.BlockSpec((NBD, BD, BD), c3), pl.BlockSpec((NBD, BD, BD), c3),
                   pl.BlockSpec((16, D), c2)],
        out_shape=[jax.ShapeDtypeStruct((T, D), BF), jax.ShapeDtypeStruct((NBD, BD, BD), F32),
                   jax.ShapeDtypeStruct((NBD, BD, BD), F32), jax.ShapeDtypeStruct((16, D), F32)],
        scratch_shapes=[pltpu.VMEM((tm, D), F32), pltpu.VMEM((tm, D), F32), pltpu.VMEM((tm, D), F32),
                        pltpu.VMEM((tm, D), F32), pltpu.VMEM((SLAB, D), F32), pltpu.VMEM((SLAB, D), F32)],
        compiler_params=pltpu.CompilerParams(dimension_semantics=("arbitrary",), vmem_limit_bytes=_vmem_limit(blk)),
    )(xg, xg, h, h, dha, cw, vec, wabd, wxbd)


_SCALE = 1.0 / math.sqrt(DH)


_ANY = pl.BlockSpec(memory_space=pl.ANY)


class _Side:
    def __init__(self, srcs, outs, nsem, copies):
        self.srcs, self.outs, self.nsem, self.copies = list(srcs), list(outs), nsem, copies


def _pallas(body, operands, *, name, grid, in_specs, out_specs, out_shape, scratch_shapes=(), semantics,
            vmem=None, side=None):
    if side is None:
        return pl.pallas_call(
            body, name=name, grid=grid, in_specs=in_specs, out_specs=out_specs, out_shape=out_shape,
            scratch_shapes=list(scratch_shapes),
            compiler_params=pltpu.CompilerParams(dimension_semantics=semantics, vmem_limit_bytes=vmem),
        )(*operands)
    n_in, n_out, n_scr = len(in_specs), len(out_specs), len(scratch_shapes)
    ns, no = len(side.srcs), len(side.outs)

    def hosted(*refs):
        ins, refs = refs[:n_in], refs[n_in:]
        sin, refs = refs[:ns], refs[ns:]
        outs, refs = refs[:n_out], refs[n_out:]
        sout, refs = refs[:no], refs[no:]
        scr, (send, recv) = refs[:n_scr], refs[n_scr:]
        ids = [pl.program_id(a) for a in range(len(grid))]
        first = functools.reduce(jnp.logical_and, [i == 0 for i in ids])
        last = functools.reduce(jnp.logical_and, [i == g - 1 for i, g in zip(ids, grid)])

        @pl.when(first)
        def _():
            for cp in side.copies(sin, sout, send, recv):
                cp.start()

        body(*ins, *outs, *scr)

        @pl.when(last)
        def _():
            for cp in side.copies(sin, sout, send, recv):
                cp.wait()

    return pl.pallas_call(
        hosted, name=name, grid=grid, in_specs=list(in_specs) + [_ANY] * ns, out_specs=list(out_specs) + [_ANY] * no,
        out_shape=list(out_shape) + side.outs,
        scratch_shapes=list(scratch_shapes) + [pltpu.SemaphoreType.DMA((side.nsem,)), pltpu.SemaphoreType.DMA((side.nsem,))],
        compiler_params=pltpu.CompilerParams(dimension_semantics=("arbitrary",) * len(grid), vmem_limit_bytes=vmem),
    )(*operands, *side.srcs)


DA = 2 * DH
_LOG2E = math.log2(math.e)
_C2 = _SCALE * _LOG2E


def _aug_fn(ins, cs):
    q, k, fcum = ins
    g_all = fcum * _LOG2E
    lane = lax.broadcasted_iota(jnp.int32, (q.shape[0], DH), 1)
    qa, ka = [], []
    for hd in range(NH):
        g = g_all[:, hd:hd + 1]
        hi = g.astype(BF).astype(F32)
        mid = (g - hi).astype(BF).astype(F32)
        lo = ((g - hi) - mid).astype(BF).astype(F32)
        qx = jnp.where(lane == 0, hi, jnp.where(lane == 1, mid, jnp.where(lane == 2, lo,
                                                                          jnp.where(lane < 6, 1.0, 0.0))))
        kx = jnp.where(lane < 3, 1.0, jnp.where(lane == 3, -hi, jnp.where(lane == 4, -mid,
                                                                          jnp.where(lane == 5, -lo, 0.0))))
        qa += [q[:, hd * DH:(hd + 1) * DH], qx.astype(BF)]
        ka += [k[:, hd * DH:(hd + 1) * DH], kx.astype(BF)]
    return [jnp.concatenate(qa, axis=1), jnp.concatenate(ka, axis=1)], []


_KA_ONES = DH + 3
KT_ONES = 16
ATTN_CHAINS = 1


def _attn_fwd(qa, ka, vt, T, blk=512, side=None):
    blk = min(blk, T)
    nb = T // blk
    bqs = blk // ATTN_CHAINS

    def body(q_ref, k_ref, v_ref, o_ref, lse_ref):
        i = pl.program_id(1)
        qs = [q_ref[pl.ds(c * bqs, bqs), :] for c in range(ATTN_CHAINS)]

        def scores(j, c):
            r0 = pl.multiple_of(j * blk, blk)
            return _dot(k_ref[pl.ds(r0, blk), :], qs[c], "nt")

        def update(s, vj, carry):
            m, l, acc = carry
            m_new = jnp.maximum(m, jnp.max(s, axis=0, keepdims=True))
            alpha = jnp.exp2(m - m_new)
            p = jnp.exp2(s - m_new)
            l = alpha * l + jnp.sum(p, axis=0, keepdims=True)
            acc = alpha * acc + _dot(vj, p.astype(BF))
            return m_new, l, acc

        def step(j, carry):
            ss, st = carry
            nxt = tuple(scores(j + 1, c) for c in range(ATTN_CHAINS))
            vj = v_ref[j]
            return nxt, tuple(update(ss[c], vj, st[c]) for c in range(ATTN_CHAINS))

        init = tuple((jnp.full((1, bqs), -jnp.inf, F32), jnp.zeros((1, bqs), F32), jnp.zeros((DH, bqs), F32))
                     for _ in range(ATTN_CHAINS))
        first = tuple(scores(0, c) for c in range(ATTN_CHAINS))
        last, carry = lax.fori_loop(0, i, step, (first, init))
        rk = lax.broadcasted_iota(jnp.int32, (blk, bqs), 0)
        cq = lax.broadcasted_iota(jnp.int32, (blk, bqs), 1)
        vi = v_ref[i]
        for c in range(ATTN_CHAINS):
            s = jnp.where(cq + c * bqs >= rk, last[c], -jnp.inf)
            m, l, acc = update(s, vi, carry[c])
            o_ref[:, c * bqs:(c + 1) * bqs] = (acc / l).astype(o_ref.dtype)
            lse_ref[:, c * bqs:(c + 1) * bqs] = m + jnp.log(l) * _LOG2E

    vm = _nbytes((T, DA), BF) + _nbytes((T, DH), BF) + 6 * _nbytes((blk, blk), F32)
    return _pallas(
        body, (qa, ka, vt), name="attn_fwd", grid=(NH, nb),
        in_specs=[pl.BlockSpec((blk, DA), lambda h, i: (i, h)),
                  pl.BlockSpec((T, DA), lambda h, i: (0, h)),
                  pl.BlockSpec((None, nb, DH, blk), lambda h, i: (h, 0, 0, 0))],
        out_specs=[pl.BlockSpec((None, None, DH, blk), lambda h, i: (h, i, 0, 0)),
                   pl.BlockSpec((None, None, 1, blk), lambda h, i: (h, i, 0, 0))],
        out_shape=[jax.ShapeDtypeStruct((NH, nb, DH, blk), BF), jax.ShapeDtypeStruct((NH, nb, 1, blk), F32)],
        semantics=("parallel", "arbitrary"), vmem=_vmem_limit(vm), side=side)


def _attn_bwd(qa, ka, kt, qkv, do, lrow, drow, T, blk=512, side=None):
    blk = min(blk, T)
    nb = T // blk

    def body(ka_ref, kt_ref, v_ref, qa_ref, do_ref, l_ref, d_ref, dq_ref, dk_ref, dv_ref, dfs_ref):
        j = pl.program_id(1)

        @pl.when(j == 0)
        def _():
            dq_ref[...] = jnp.zeros_like(dq_ref)

        row = lax.broadcasted_iota(jnp.int32, (DH + KT_ONES, blk), 0)
        dq_scale = jnp.where(row < DH, _SCALE, 1.0)

        kaj = ka_ref[...]
        ktj = kt_ref[...]
        vj = v_ref[...]

        def step(i, carry, diag):
            dka, dv = carry
            r0 = pl.multiple_of(i * blk, blk)
            qi = qa_ref[pl.ds(r0, blk), :]
            doi = do_ref[pl.ds(r0, blk), :]
            st = _dot(kaj, qi, "nt") - l_ref[i]
            if diag:
                rk = lax.broadcasted_iota(jnp.int32, (blk, blk), 0)
                cq = lax.broadcasted_iota(jnp.int32, (blk, blk), 1)
                st = jnp.where(cq >= rk, st, -jnp.inf)
            pt = jnp.exp2(st)
            dv = dv + _dot(pt.astype(BF), doi)
            dpt = _dot(vj, doi, "nt")
            dst = pt * (dpt - d_ref[i])
            dsb = dst.astype(BF)
            dka = dka + _dot(dsb, qi)
            dq_ref[i] += _dot(ktj, dsb) * dq_scale
            return dka, dv

        init = (jnp.zeros((blk, DA), F32), jnp.zeros((blk, DH), F32))
        carry = step(j, init, True)
        dka, dv = lax.fori_loop(j + 1, nb, lambda i, c: step(i, c, False), carry)
        dk_ref[...] = (dka[:, :DH] * (1.0 / _LOG2E)).astype(dk_ref.dtype)
        dv_ref[...] = dv.astype(dv_ref.dtype)
        dfs_ref[...] = dka[:, DH:].T[_KA_ONES - DH:_KA_ONES - DH + 1, :]

    rowv = pl.BlockSpec((None, nb, 1, blk), lambda h, j: (h, 0, 0, 0))
    vm = _nbytes((T, DA), BF) + _nbytes((T, DH), BF) + _nbytes((T, DH), F32) + 8 * _nbytes((blk, blk), F32)
    return _pallas(
        body, (ka, kt, qkv, qa, do, lrow, drow), name="attn_bwd", grid=(NH, nb),
        in_specs=[pl.BlockSpec((blk, DA), lambda h, j: (j, h)),
                  pl.BlockSpec((None, None, DH + KT_ONES, blk), lambda h, j: (h, j, 0, 0)),
                  pl.BlockSpec((blk, DH), lambda h, j: (j, 2 * NH + h)),
                  pl.BlockSpec((T, DA), lambda h, j: (0, h)),
                  pl.BlockSpec((T, DH), lambda h, j: (0, h)),
                  rowv, rowv],
        out_specs=[pl.BlockSpec((None, nb, DH + KT_ONES, blk), lambda h, j: (h, 0, 0, 0)),
                   pl.BlockSpec((blk, DH), lambda h, j: (j, h)),
                   pl.BlockSpec((blk, DH), lambda h, j: (j, h)),
                   pl.BlockSpec((None, None, 1, blk), lambda h, j: (h, j, 0, 0))],
        out_shape=[jax.ShapeDtypeStruct((NH, nb, DH + KT_ONES, blk), F32), jax.ShapeDtypeStruct((T, D), BF),
                   jax.ShapeDtypeStruct((T, D), BF), jax.ShapeDtypeStruct((NH, nb, 1, blk), F32)],
        semantics=("parallel", "arbitrary"), vmem=_vmem_limit(vm), side=side)


def _norm_fn(ins, cs):
    x, = ins
    g, = cs
    r = lax.rsqrt(jnp.mean(x * x, axis=-1, keepdims=True) + EPS)
    return [x * r * g], []


def _norm_bwd_fn(ins, cs):
    x, dy, dres = ins
    g, = cs
    r = lax.rsqrt(jnp.mean(x * x, axis=-1, keepdims=True) + EPS)
    xh = x * r
    dxh = dy * g
    dx = dres + r * (dxh - xh * jnp.mean(dxh * xh, axis=-1, keepdims=True))
    return [dx], [_colsum(dy * xh)]


def _final_fn(ins, cs):
    x2, tgt = ins
    g, = cs
    r = lax.rsqrt(jnp.mean(x2 * x2, axis=-1, keepdims=True) + EPS)
    xh = x2 * r
    e = xh * g - tgt
    dy = e * (1.0 / D)
    dxh = dy * g
    dx2 = r * (dxh - xh * jnp.mean(dxh * xh, axis=-1, keepdims=True))
    return [dx2], [_colsum(0.5 * e * e * (1.0 / D)), _colsum(dy * xh)]


def _z_fn(ins, cs):
    g, h = ins
    return [_gelu(g) * h], []


def _mix_fn(ins, cs):
    gates, ya, yb = ins
    return [_sig(gates[:, :D]) * ya + _sig(gates[:, D:]) * yb], []


def _mix_bwd_fn(ins, cs):
    dmix, gates, ya, yb = ins
    ga = _sig(gates[:, :D])
    gb = _sig(gates[:, D:])
    dgates = jnp.concatenate([dmix * ya * ga * (1.0 - ga), dmix * yb * gb * (1.0 - gb)], axis=1)
    return [dmix * ga, dmix * gb, dgates], []


def _z_bwd_fn(ins, cs):
    dz, g, h = ins
    return [dz * _gelu(g), dz * h * _gelu_grad(g)], []


def _delta_fn(ins, cs):
    do, o = ins
    p = do.astype(F32) * o.astype(F32)
    lane = lax.broadcasted_iota(jnp.int32, (p.shape[0], DH), 1)
    out = jnp.zeros((p.shape[0], DH), F32)
    for hd in range(NH):
        s = jnp.sum(p[:, hd * DH:(hd + 1) * DH], axis=1, keepdims=True)
        out = jnp.where(lane == hd, s, out)
    return [out], []


def _du_all(pieces, win, T, tm=256, side=None):
    tm = min(tm, T)
    n = len(pieces)

    def body(*refs):
        w_ref, o_ref = refs[n], refs[n + 1]
        acc = None
        for (a, off), a_ref in zip(pieces, refs[:n]):
            d = _dot(a_ref[...].astype(BF), w_ref[:, off:off + a.shape[1]], "nt")
            acc = d if acc is None else acc + d
        o_ref[...] = acc

    vm = (sum(_nbytes((tm, a.shape[1]), a.dtype) for a, _ in pieces) + _nbytes(win.shape, win.dtype)
          + 2 * _nbytes((tm, D), F32))
    return _pallas(
        body, tuple(a for a, _ in pieces) + (win,), name="du_all", grid=(T // tm,),
        in_specs=[pl.BlockSpec((tm, a.shape[1]), lambda i: (i, 0)) for a, _ in pieces]
        + [pl.BlockSpec(win.shape, lambda i: (0, 0))],
        out_specs=[pl.BlockSpec((tm, D), lambda i: (i, 0))],
        out_shape=[jax.ShapeDtypeStruct((T, D), F32)],
        semantics=("arbitrary",), vmem=int(min(VMEM_CAP, 2 * vm + (4 << 20))), side=side)


def _local_step(x, tgt, w, T, blk=1024, dist=None):
    blk = min(blk, T)
    nb = T // blk
    win = w["win"]

    u, = _ew(_norm_fn, T, 512, [(x, D, 0)], [w["g_mix"]], [(D, BF)], [], name="norm_mix")
    xg = _mm(u, win, "nn", T, 2 * D, D, name="proj_lru")
    qkv = _mm(u, win, "nn", T, 3 * D, D, name="proj_qkv", out_dtype=BF, b_off=(0, 2),
              epi=lambda acc: acc * jnp.where(pl.program_id(1) == 0, _C2, 1.0))
    gates = _mm(u, win, "nn", T, 2 * D, D, name="proj_gates", b_off=(0, 5))
    fl = _mm(u, win, "nn", T, DH, D, name="proj_f", tn=DH, b_off=(0, 7 * D // DH))
    fcum = _fgate_fwd(fl, w["fb"], T)
    qa, ka = _ew(_aug_fn, T, 256, [(qkv, D, 0), (qkv, D, 1), (fcum, DH, 0)], [],
                 [(NH * DA, BF), (NH * DA, BF)], [], name="attn_augment")

    def heads_t(a):
        return a.reshape(nb, blk, NH, DH).transpose(2, 0, 3, 1)

    def heads_back(a):
        return a.transpose(1, 3, 0, 2).reshape(T, D)

    kt = heads_t(qkv[:, D:2 * D])
    vt = heads_t(qkv[:, 2 * D:])
    h = _lru_fwd(xg, w["cw"], w["vec"], w["wabd"], w["wxbd"], T)
    ot, lse, *landed = _attn_fwd(qa, ka, vt, T, blk, side=dist.weights_side() if dist else None)
    if dist:
        w = dict(w, **dist.weights_landed(landed))
    ob = heads_back(ot)
    z, = _ew(_z_fn, T, 512, [(xg, D, 1), (h, D, 0)], [], [(D, BF)], [], name="lru_gelu")
    ya = _mm(z, w["wa"], "nn", T, D, D, name="branch_a")
    yb = _mm(ob, w["wb"], "nn", T, D, D, name="branch_b")
    mix, = _ew(_mix_fn, T, 256, [(gates, 2 * D, 0), (ya, D, 0), (yb, D, 0)], [], [(D, BF)], [], name="mix")
    x1 = _mm(mix, w["wout"], "nn", T, D, D, name="out_proj", add=x)
    m, = _ew(_norm_fn, T, 512, [(x1, D, 0)], [w["g_mlp"]], [(D, BF)], [], name="norm_mlp")
    hh = _mm(m, w["wup"], "nn", T, FF, D, name="mlp_up", out_dtype=BF,
             epi=lambda acc: jnp.square(jnp.maximum(acc, 0.0)))
    x2 = _mm(hh, w["wdown"], "nn", T, D, FF, name="mlp_down", add=x1)
    dx2, loss_vec, dg_fin = _ew(_final_fn, T, 256, [(x2, D, 0), (tgt, D, 0)], [w["g_fin"]], [(D, F32)],
                                [(1, D), (1, D)], name="final_norm_loss")

    dhpre = _mm(dx2, w["wdown"], "nt", T, FF, D, name="mlp_down_bwd", out_dtype=BF,
                epi=lambda acc, h2: acc * (2.0 * jnp.sqrt(h2.astype(F32))), epi_ins=[hh])
    dwdown = _mm(hh, dx2, "tn", FF, D, T, name="dw_down", out_dtype=BF)
    dwup = _mm(m, dhpre, "tn", D, FF, T, name="dw_up", out_dtype=BF)
    dm = _mm(dhpre, w["wup"], "nt", T, D, FF, name="mlp_up_bwd")
    dx1, dg_mlp = _ew(_norm_bwd_fn, T, 256, [(x1, D, 0), (dm, D, 0), (dx2, D, 0)], [w["g_mlp"]], [(D, F32)],
                      [(1, D)], name="norm_mlp_bwd")

    dmix = _mm(dx1, w["wout"], "nt", T, D, D, name="out_proj_bwd")
    dwout = _mm(mix, dx1, "tn", D, D, T, name="dw_out", out_dtype=BF)
    dya, dyb, dgates = _ew(_mix_bwd_fn, T, 256, [(dmix, D, 0), (gates, 2 * D, 0), (ya, D, 0), (yb, D, 0)], [],
                           [(D, BF), (D, BF), (2 * D, BF)], [], name="mix_bwd")
    dob = _mm(dyb, w["wb"], "nt", T, D, D, name="branch_b_bwd", out_dtype=BF)
    dwb = _mm(ob, dyb, "tn", D, D, T, name="dw_b", out_dtype=BF)
    dz = _mm(dya, w["wa"], "nt", T, D, D, name="branch_a_bwd")
    dwa = _mm(z, dya, "tn", D, D, T, name="dw_a", out_dtype=BF)
    dha, dglru = _ew(_z_bwd_fn, T, 256, [(dz, D, 0), (xg, D, 1), (h, D, 0)], [], [(D, F32), (D, BF)], [],
                     name="lru_gelu_bwd")

    delta, = _ew(_delta_fn, T, 512, [(dob, D, 0), (ob, D, 0)], [], [(DH, F32)], [], name="attn_delta")
    drow = delta[:, :NH].T.reshape(NH, nb, 1, blk)
    kt1 = jnp.concatenate([kt, jnp.ones((NH, nb, KT_ONES, blk), BF)], axis=2)
    big = dict(w_branch_a=dwa, w_branch_b=dwb, w_out=dwout, w_up=dwup, w_down=dwdown)
    side = dist.grads_side(big) if dist else None
    dqt, dk, dv, dfs, *landed = _attn_bwd(qa, ka, kt1, qkv, dob, lse, drow, T, blk, side=side)
    if dist:
        big = dist.grads_landed(side, landed)
    dq = heads_back(dqt[:, :, :DH])
    dfcum = jnp.pad((dqt[:, :, DH] - dfs[:, :, 0]).reshape(NH, T).T, ((0, 0), (0, DH - NH)))
    dfl, dfb = _fgate_bwd(dfcum, fl, w["fb"], T)

    dxl, dwabd, dwxbd, lacc = _lru_bwd(xg, h, dha, w["cw"], w["vec"], w["wabd"], w["wxbd"], T)

    dproj = ((dxl, 0), (dglru, D), (dq, 2 * D), (dk, 3 * D), (dv, 4 * D), (dgates, 5 * D), (dfl, 7 * D))
    pieces = [_mm(u, p, "tn", D, p.shape[1], T, name="dw_in_%d" % n, out_dtype=BF)
              for n, (p, _) in enumerate(dproj)]
    pieces[-1] = pieces[-1][:, :NH]
    dwin = dict(w_in=jnp.concatenate(pieces, axis=1))
    side = dist.grads_side(dwin) if dist else None
    du, *landed = _du_all(dproj, win, T, side=side)
    big.update(dist.grads_landed(side, landed) if dist else dwin)
    dx, dg_mix = _ew(_norm_bwd_fn, T, 256, [(x, D, 0), (du, D, 0), (dx1, D, 0)], [w["g_mix"]], [(D, F32)],
                     [(1, D)], name="norm_mix_bwd")

    return dict(dx=dx, big=big, dwabd=dwabd, dwxbd=dwxbd, lacc=lacc, dfb=dfb, dg_mix=dg_mix, dg_mlp=dg_mlp,
                dg_fin=dg_fin, loss_vec=loss_vec)


def _block_diag(w):
    per = BD // LRU_BW
    w4 = w.reshape(NBD, per, LRU_BW, LRU_BW)
    out = jnp.zeros((NBD, per, LRU_BW, per, LRU_BW), w.dtype)
    for b in range(per):
        out = out.at[:, b, :, b, :].set(w4[:, b])
    return out.reshape(NBD, BD, BD)


def _block_diag_extract(wbd):
    per = BD // LRU_BW
    w5 = wbd.reshape(NBD, per, LRU_BW, per, LRU_BW)
    return jnp.stack([w5[:, b, :, b, :] for b in range(per)], axis=1).reshape(LRU_BLOCKS, LRU_BW, LRU_BW)


_ANY = pl.BlockSpec(memory_space=pl.ANY)


def _place():
    x, y, c = lax.axis_index("x"), lax.axis_index("y"), lax.axis_index("c")
    chips = [(1 - x, y), (x, 1 - y), (1 - x, 1 - y)]
    return x, y, c, chips


def _allgather_shards(shards):
    n = len(shards)

    def body(*refs):
        ins, outs = refs[:n], refs[n:2 * n]
        send_sems, recv_sems = refs[2 * n:]
        x, y, c, chips = _place()
        me = 2 * x + y
        sibling = (x, y, 1 - c)

        def remote(p, k, src, dst, to):
            return pltpu.make_async_remote_copy(src_ref=src, dst_ref=dst, send_sem=send_sems.at[p, k],
                                                recv_sem=recv_sems.at[p, k], device_id=to, device_id_type=MESH)

        sent = []
        for p in range(n):
            for k, chip in enumerate(chips):
                cp = remote(p, k, ins[p].at[c], outs[p].at[me, c], (chip[0], chip[1], c))
                cp.start()
                sent.append(cp)
        for p in range(n):
            for k, chip in enumerate(chips):
                half = outs[p].at[2 * chip[0] + chip[1], c]
                remote(p, k, half, half, sibling).wait_recv()
                fwd = remote(p, 3 + k, half, half, sibling)
                fwd.start()
                sent.append(fwd)
        for p in range(n):
            for k, chip in enumerate(chips):
                half = outs[p].at[2 * chip[0] + chip[1], 1 - c]
                remote(p, 3 + k, half, half, sibling).wait_recv()
        for cp in sent:
            cp.wait_send()

    gathered = pl.pallas_call(
        body, name="allgather_weights",
        in_specs=[_ANY] * n, out_specs=[_ANY] * n,
        out_shape=[jax.ShapeDtypeStruct((NCHIP,) + s.shape, s.dtype) for s in shards],
        scratch_shapes=[pltpu.SemaphoreType.DMA((n, 6)), pltpu.SemaphoreType.DMA((n, 6))],
    )(*shards)
    me = 2 * lax.axis_index("x") + lax.axis_index("y")
    return [lax.dynamic_update_index_in_dim(g, s, me, 0) for g, s in zip(gathered, shards)]


_LATE =["w_branch_a", "w_branch_b", "w_out", "w_up", "w_down"]
_COLUMN_CUT = ("w_in", "w_up")
N_PEERS = 7


def _shard_major(name, g):
    s = _columns_to_shards(g) if name in _COLUMN_CUT else g.reshape(NCHIP, g.shape[0] // NCHIP, g.shape[1])
    return s.reshape(NCHIP, 2, s.shape[1] // 2, s.shape[2])


class _Exchanges:
    def __init__(self, shards):
        self.shards = shards

    def weights_side(self):
        srcs = [self.shards[n] for n in _LATE]

        def copies(sin, sout, send, recv):
            x, y, c, chips = _place()
            return [pltpu.make_async_remote_copy(
                src_ref=sin[p], dst_ref=sout[p].at[2 * x + y], send_sem=send.at[3 * p + k], recv_sem=recv.at[3 * p + k],
                device_id=(chip[0], chip[1], c), device_id_type=MESH)
                for p in range(len(sin)) for k, chip in enumerate(chips)]

        return _Side(srcs, [jax.ShapeDtypeStruct((NCHIP,) + s.shape, s.dtype) for s in srcs], 3 * len(srcs), copies)

    def weights_landed(self, landed):
        me = 2 * lax.axis_index("x") + lax.axis_index("y")
        full = {n: lax.dynamic_update_index_in_dim(g, self.shards[n], me, 0) for n, g in zip(_LATE, landed)}
        return dict(wa=full["w_branch_a"].reshape(D, D), wb=full["w_branch_b"].reshape(D, D),
                    wout=full["w_out"].reshape(D, D), wup=_shards_to_columns(full["w_up"]),
                    wdown=full["w_down"].reshape(FF, D))

    def grads_side(self, grads):
        side_names = list(grads)
        srcs = [_shard_major(n, grads[n]) for n in side_names]

        def copies(sin, sout, send, recv):
            x, y, c, chips = _place()
            peers = [(x, y, 1 - c)] + [(cx, cy, c) for cx, cy in chips] + [(cx, cy, 1 - c) for cx, cy in chips]
            return [pltpu.make_async_remote_copy(
                src_ref=sin[p].at[2 * px + py, pc], dst_ref=sout[p].at[s], send_sem=send.at[N_PEERS * p + s],
                recv_sem=recv.at[N_PEERS * p + s], device_id=(px, py, pc), device_id_type=MESH)
                for p in range(len(sin)) for s, (px, py, pc) in enumerate(peers)]

        side = _Side(srcs, [jax.ShapeDtypeStruct((N_PEERS,) + s.shape[2:], s.dtype) for s in srcs],
                     N_PEERS * len(srcs), copies)
        side.names = side_names
        return side

    def grads_landed(self, side, landed):
        return {n: (own, got) for n, own, got in zip(side.names, side.srcs, landed)}


def _add8(g, got, me, c, name):
    _, _, half, cols = g.shape
    th = _row_tile(half, 2 * cols)

    def body(me_ref, c_ref, g_ref, r_ref, o_ref):
        acc = g_ref[...].astype(F32)
        for s in range(N_PEERS):
            acc = acc + r_ref[s].astype(F32)
        o_ref[...] = acc

    return pl.pallas_call(
        body, name=name,
        grid_spec=pltpu.PrefetchScalarGridSpec(
            num_scalar_prefetch=2, grid=(half // th,),
            in_specs=[pl.BlockSpec((None, None, th, cols), lambda i, me_ref, c_ref: (me_ref[0], c_ref[0], i, 0)),
                      pl.BlockSpec((N_PEERS, th, cols), lambda i, me_ref, c_ref: (0, i, 0))],
            out_specs=pl.BlockSpec((th, cols), lambda i, me_ref, c_ref: (i, 0))),
        out_shape=jax.ShapeDtypeStruct((half, cols), F32),
    )(me, c, g, got)


def _share_halves(halves):
    n = len(halves)

    def body(*refs):
        ins, outs = refs[:n], refs[n:2 * n]
        send_sems, recv_sems = refs[2 * n:]
        x, y, c, _ = _place()
        sibling = (x, y, 1 - c)
        copies = []
        for p in range(n):
            cp = pltpu.make_async_remote_copy(src_ref=ins[p], dst_ref=outs[p], send_sem=send_sems.at[p],
                                              recv_sem=recv_sems.at[p], device_id=sibling, device_id_type=MESH)
            cp.start()
            copies.append(cp)
        for cp in copies:
            cp.wait()

    return pl.pallas_call(
        body, name="reduce_share_halves",
        in_specs=[_ANY] * n, out_specs=[_ANY] * n,
        out_shape=[jax.ShapeDtypeStruct(h.shape, h.dtype) for h in halves],
        scratch_shapes=[pltpu.SemaphoreType.DMA((n,)), pltpu.SemaphoreType.DMA((n,))],
    )(*halves)


def _row_tile(half, cols):
    th = max(SLAB, min(half, (1 << 18) // cols // SLAB * SLAB))
    while half % th:
        th -= SLAB
    return th


N_DEV = 8
SMALL_ROWS = 208


def _allreduce_small(pack):
    def body(x_ref, out_ref, gbuf, send_sems, recv_sems, local_sem):
        x, y, c, chips = _place()
        me, sibling = (x, y, c), (x, y, 1 - c)

        def rows(px, py, pc):
            return gbuf.at[4 * px + 2 * py + pc]

        def copy(k, block, to, src=None):
            return pltpu.make_async_remote_copy(
                src_ref=rows(*block) if src is None else src, dst_ref=rows(*block),
                send_sem=send_sems.at[k], recv_sem=recv_sems.at[k], device_id=to, device_id_type=MESH)

        mine = pltpu.make_async_copy(x_ref, rows(*me), local_sem)
        mine.start()
        first = [copy(0, me, sibling, src=x_ref)]
        first += [copy(1 + j, me, (chip[0], chip[1], c), src=x_ref) for j, chip in enumerate(chips)]
        for cp in first:
            cp.start()
        passed = [copy(4 + j, (chip[0], chip[1], c), sibling) for j, chip in enumerate(chips)]
        for j, chip in enumerate(chips):
            copy(1 + j, (chip[0], chip[1], c), me).wait_recv()
            passed[j].start()
        copy(0, sibling, me).wait_recv()
        for j, chip in enumerate(chips):
            copy(4 + j, (chip[0], chip[1], 1 - c), me).wait_recv()
        for cp in first + passed:
            cp.wait_send()
        mine.wait()
        acc = gbuf[0]
        for d in range(1, N_DEV):
            acc = acc + gbuf[d]
        out_ref[...] = acc

    return pl.pallas_call(
        body, name="allreduce_small",
        in_specs=[pl.BlockSpec(memory_space=pltpu.VMEM)],
        out_specs=pl.BlockSpec(memory_space=pltpu.VMEM),
        out_shape=jax.ShapeDtypeStruct((SMALL_ROWS, D), F32),
        scratch_shapes=[pltpu.VMEM((N_DEV, SMALL_ROWS, D), F32), pltpu.SemaphoreType.DMA((7,)),
                        pltpu.SemaphoreType.DMA((7,)), pltpu.SemaphoreType.DMA],
    )(pack)


def _adamw(w, g, m, v, name):
    rows, cols = w.shape
    th = _row_tile(rows, cols)

    def body(w_ref, g_ref, m_ref, v_ref, d_ref, mo_ref, vo_ref):
        gv = g_ref[...]
        mn = ADAM_B1 * m_ref[...] + (1.0 - ADAM_B1) * gv
        vn = ADAM_B2 * v_ref[...] + (1.0 - ADAM_B2) * (gv * gv)
        m_hat = mn / (1.0 - ADAM_B1 ** ADAM_STEP)
        v_hat = vn / (1.0 - ADAM_B2 ** ADAM_STEP)
        d_ref[...] = -ADAM_LR * (m_hat / (jnp.sqrt(v_hat) + ADAM_EPS) + ADAM_WD * w_ref[...])
        mo_ref[...] = mn
        vo_ref[...] = vn

    spec = pl.BlockSpec((th, cols), lambda i: (i, 0))
    return pl.pallas_call(
        body, name=name, grid=(rows // th,),
        in_specs=[spec] * 4, out_specs=[spec] * 3,
        out_shape=[jax.ShapeDtypeStruct((rows, cols), F32)] * 3,
        compiler_params=pltpu.CompilerParams(dimension_semantics=("parallel",)),
    )(w, g, m, v)


_SMALL = ["norm_mix_g", "norm_mlp_g", "norm_final_g", "conv_b", "lru_ba", "lru_bx", "lru_lambda"]
_ROW_FB, _ROW_CW, _ROW_WA, _ROW_WX, _ROW_LOSS = 56, 64, 72, 136, 200


def _pack_small(vals, col0):
    def slab(a):
        return jnp.pad(a, ((0, -a.shape[0] % SLAB), (0, D - a.shape[1])))

    rows = [slab(vals[n].reshape(1, D)) for n in _SMALL]
    rows.append(slab(vals["forget_b"].reshape(1, NH)))
    if vals["conv_w"].shape[1] == D:
        rows.append(slab(vals["conv_w"]))
    else:
        rows.append(slab(lax.dynamic_update_slice(jnp.zeros((CONV, D), F32), vals["conv_w"], (0, col0))))
    rows.append(vals["lru_wa"].reshape(LRU_BLOCKS * LRU_BW * LRU_BW // D, D))
    rows.append(vals["lru_wx"].reshape(LRU_BLOCKS * LRU_BW * LRU_BW // D, D))
    rows.append(slab(vals["loss"]) if "loss" in vals else jnp.zeros((SLAB, D), F32))
    return jnp.concatenate(rows, axis=0)


def _unpack_small(pack, col0):
    out = {n: pack[SLAB * i] for i, n in enumerate(_SMALL)}
    out["forget_b"] = pack[_ROW_FB, :NH]
    out["conv_w"] = lax.dynamic_slice(pack[_ROW_CW:_ROW_CW + CONV], (0, col0), (CONV, D // NCHIP))
    out["lru_wa"] = pack[_ROW_WA:_ROW_WX].reshape(LRU_BLOCKS, LRU_BW, LRU_BW)
    out["lru_wx"] = pack[_ROW_WX:_ROW_LOSS].reshape(LRU_BLOCKS, LRU_BW, LRU_BW)
    return out


_WEIGHTS = ["norm_mix_g", "w_in", "conv_w", "conv_b", "lru_wa", "lru_ba", "lru_wx", "lru_bx", "lru_lambda",
            "forget_b", "w_branch_a", "w_branch_b", "w_out", "norm_mlp_g", "w_up", "w_down", "norm_final_g"]
_BIG = ["w_in", "w_branch_a", "w_branch_b", "w_out", "w_up", "w_down"]


def _halves(a):
    return a.reshape(2, a.shape[0] // 2, a.shape[1])


def _columns_to_shards(a):
    rows, cols = a.shape[0], a.shape[1] // NCHIP
    return jnp.transpose(a.reshape(rows, NCHIP, cols), (1, 0, 2))


def _shards_to_columns(a):
    n, rows, cols = a.shape
    return jnp.transpose(a, (1, 0, 2)).reshape(rows, n * cols)


def kernel(x, norm_mix_g, w_in, conv_w, conv_b, lru_wa, lru_ba, lru_wx, lru_bx, lru_lambda, forget_b, w_branch_a, w_branch_b, w_out, norm_mlp_g, w_up, w_down, norm_final_g, loss_target, m_norm_mix_g, m_w_in, m_conv_w, m_conv_b, m_lru_wa, m_lru_ba, m_lru_wx, m_lru_bx, m_lru_lambda, m_forget_b, m_w_branch_a, m_w_branch_b, m_w_out, m_norm_mlp_g, m_w_up, m_w_down, m_norm_final_g, v_norm_mix_g, v_w_in, v_conv_w, v_conv_b, v_lru_wa, v_lru_ba, v_lru_wx, v_lru_bx, v_lru_lambda, v_forget_b, v_w_branch_a, v_w_branch_b, v_w_out, v_norm_mlp_g, v_w_up, v_w_down, v_norm_final_g):
    args = dict(locals())
    wts = {n: args[n] for n in _WEIGHTS}
    mom = {n: args["m_" + n] for n in _WEIGHTS}
    var = {n: args["v_" + n] for n in _WEIGHTS}
    T = x.shape[1]
    xi, yi, ci = lax.axis_index("x"), lax.axis_index("y"), lax.axis_index("c")
    me = 2 * xi + yi
    c1 = jnp.reshape(ci, (1,)).astype(jnp.int32)
    me1 = jnp.reshape(me, (1,)).astype(jnp.int32)
    col0 = me * (D // NCHIP)

    cw_pad = jnp.pad(conv_w, ((0, 4 * SLAB - CONV), (0, 0)))
    g_in, g_cw = _allgather_shards([_halves(w_in.astype(BF)), _halves(cw_pad)])
    cin = DIN // NCHIP
    win = _shards_to_columns(g_in.reshape(NCHIP, D, cin))
    w = dict(
        win=jnp.pad(win, ((0, 0), (0, DINP - DIN))),
        cw=_shards_to_columns(g_cw.reshape(NCHIP, 4 * SLAB, D // NCHIP)[:, :CONV]),
        vec=jnp.concatenate([conv_b[None], lru_ba[None], lru_bx[None], lru_lambda[None],
                             jnp.zeros((SLAB - 4, D), F32)], axis=0),
        fb=jnp.pad(forget_b[None], ((0, 0), (0, DH - NH))),
        wabd=_block_diag(lru_wa).astype(BF), wxbd=_block_diag(lru_wx).astype(BF),
        g_mix=norm_mix_g[None], g_mlp=norm_mlp_g[None], g_fin=norm_final_g[None])

    r = _local_step(x[0], loss_target[0], w, T, dist=_Exchanges({n: wts[n].astype(BF) for n in _LATE}))

    halves = [_add8(*r["big"][n], me1, c1, "add8_" + n) for n in _BIG]
    theirs = _share_halves(halves)
    low = ci == 0
    gsum = {n: jnp.concatenate([jnp.where(low, h, t), jnp.where(low, t, h)], axis=0)
            for n, h, t in zip(_BIG, halves, theirs)}
    lacc = r["lacc"]
    small = dict(norm_mix_g=r["dg_mix"], norm_mlp_g=r["dg_mlp"], norm_final_g=r["dg_fin"], conv_b=lacc[3],
                 lru_ba=lacc[0], lru_bx=lacc[1], lru_lambda=lacc[2], forget_b=r["dfb"][0, :NH],
                 conv_w=lacc[4:4 + CONV], lru_wa=_block_diag_extract(r["dwabd"]),
                 lru_wx=_block_diag_extract(r["dwxbd"]), loss=r["loss_vec"])
    gpack = _allreduce_small(_pack_small(small, col0))
    loss = jnp.sum(gpack[_ROW_LOSS])

    grads, delta, new_m, new_v = {}, {}, {}, {}
    for n in _BIG:
        grads[n] = gsum[n]
        delta[n], new_m[n], new_v[n] = _adamw(wts[n], gsum[n], mom[n], var[n], "adamw_" + n)
    dp, mp, vp = _adamw(_pack_small(wts, col0), gpack, _pack_small(mom, col0), _pack_small(var, col0), "adamw_small")
    for dst, pack in ((grads, gpack), (delta, dp), (new_m, mp), (new_v, vp)):
        dst.update(_unpack_small(pack, col0))
    return (loss, r["dx"][None], *[grads[n] for n in _WEIGHTS], *[delta[n] for n in _WEIGHTS],
            *[new_m[n] for n in _WEIGHTS], *[new_v[n] for n in _WEIGHTS])
```

```python
import functools
import math

import jax
import jax.numpy as jnp
import numpy as np
from jax import lax
from jax.experimental import pallas as pl
from jax.experimental.pallas import tpu as pltpu

F32 = jnp.float32
BF = jnp.bfloat16

D = 1024
NH = 8
DH = 128
FF = 4096
CONV = 4
LRU_BLOCKS = 16
LRU_BW = 64
BD = 256
NBD = D // BD
LRU_C = 8.0
EPS = 1e-6
DIN = 7176
DINP = 7296
NCHIP = 4
SLAB = 8
VMEM_CAP = 60 * 1024 * 1024

ADAM_LR = 0.001
ADAM_B1 = 0.9
ADAM_B2 = 0.999
ADAM_EPS = 1e-08
ADAM_WD = 0.01
ADAM_STEP = 10

MESH = pl.DeviceIdType.MESH


def _vmem_limit(nbytes):
    return int(min(VMEM_CAP, max(32 * 1024 * 1024, 3 * nbytes)))


def _nbytes(shape, dtype):
    return int(np.prod(shape)) * jnp.dtype(dtype).itemsize


def _sig(x):
    return 0.5 * jnp.tanh(0.5 * x) + 0.5


def _log1p(u):
    w = 1.0 + u
    return jnp.where(w == 1.0, u, jnp.log(w) * (u / (w - 1.0)))


def _one_minus_sq(a, la):
    z = 2.0 * la
    series = -z * (1.0 + z * (0.5 + z * ((1.0 / 6.0) + z * ((1.0 / 24.0) + z * (1.0 / 120.0)))))
    return jnp.where(z > -0.125, series, 1.0 - a * a)


def _softplus(z):
    return jnp.maximum(z, 0.0) + _log1p(jnp.exp(-jnp.abs(z)))


_GELU_C = math.sqrt(2.0 / math.pi)


def _gelu(x):
    return 0.5 * x * (1.0 + jnp.tanh(_GELU_C * (x + 0.044715 * x * x * x)))


def _gelu_grad(x):
    t = jnp.tanh(_GELU_C * (x + 0.044715 * x * x * x))
    return 0.5 * (1.0 + t) + 0.5 * x * (1.0 - t * t) * _GELU_C * (1.0 + 3.0 * 0.044715 * x * x)


def _shift_down(x, d, prev8):
    n = x.shape[0]
    row8 = lax.broadcasted_iota(jnp.int32, (SLAB, x.shape[1]), 0)
    y = pltpu.roll(x, d, 0)
    top = jnp.where(row8 < d, pltpu.roll(prev8, d, 0), y[0:SLAB])
    if n == SLAB:
        return top
    return jnp.concatenate([top, y[SLAB:]], axis=0)


def _shift_up(x, d, next8):
    n = x.shape[0]
    row8 = lax.broadcasted_iota(jnp.int32, (SLAB, x.shape[1]), 0)
    y = pltpu.roll(x, n - d, 0)
    bottom = jnp.where(row8 >= SLAB - d, pltpu.roll(next8, SLAB - d, 0), y[n - SLAB:])
    if n == SLAB:
        return bottom
    return jnp.concatenate([y[:n - SLAB], bottom], axis=0)


def _slab_scan_fwd(a, b):
    row = lax.broadcasted_iota(jnp.int32, a.shape, 0)
    for k in (1, 2, 4):
        a_s = pltpu.roll(a, k, 0)
        b_s = pltpu.roll(b, k, 0)
        m = row >= k
        b = jnp.where(m, a * b_s + b, b)
        a = jnp.where(m, a * a_s, a)
    return a, b


def _slab_scan_bwd(a, b):
    row = lax.broadcasted_iota(jnp.int32, a.shape, 0)
    for k in (1, 2, 4):
        a_s = pltpu.roll(a, SLAB - k, 0)
        b_s = pltpu.roll(b, SLAB - k, 0)
        m = row < SLAB - k
        b = jnp.where(m, a * b_s + b, b)
        a = jnp.where(m, a * a_s, a)
    return a, b


_DIMS = {"nn": (((1,), (0,)), ((), ())), "nt": (((1,), (1,)), ((), ())), "tn": (((0,), (0,)), ((), ()))}


def _dot(a, b, mode="nn"):
    return lax.dot_general(a, b, _DIMS[mode], preferred_element_type=F32)


def _mm(a, b, mode, M, N, K, *, name, out_dtype=F32, tm=1024, tn=1024, tk=1024,
        a_off=(0, 0), b_off=(0, 0), add=None, epi=None, epi_ins=()):
    tm, tn, tk = min(tm, M), min(tn, N), min(tk, K)
    nk = K // tk
    grid = (M // tm, N // tn, nk)
    if mode == "nn":
        a_spec = pl.BlockSpec((tm, tk), lambda i, j, k: (i + a_off[0], k + a_off[1]))
        b_spec = pl.BlockSpec((tk, tn), lambda i, j, k: (k + b_off[0], j + b_off[1]))
    elif mode == "nt":
        a_spec = pl.BlockSpec((tm, tk), lambda i, j, k: (i + a_off[0], k + a_off[1]))
        b_spec = pl.BlockSpec((tn, tk), lambda i, j, k: (j + b_off[0], k + b_off[1]))
    else:
        a_spec = pl.BlockSpec((tk, tm), lambda i, j, k: (k + a_off[0], i + a_off[1]))
        b_spec = pl.BlockSpec((tk, tn), lambda i, j, k: (k + b_off[0], j + b_off[1]))
    o_spec = pl.BlockSpec((tm, tn), lambda i, j, k: (i, j))
    extra = ([add] if add is not None else []) + list(epi_ins)
    n_extra = len(extra)
    has_add = add is not None

    def body(*refs):
        a_ref, b_ref = refs[0], refs[1]
        ex = refs[2:2 + n_extra]
        o_ref = refs[2 + n_extra]

        def finish(acc):
            if has_add:
                acc = acc + ex[0][...].astype(F32)
            if epi is not None:
                acc = epi(acc, *[e[...] for e in ex[(1 if has_add else 0):]])
            o_ref[...] = acc.astype(o_ref.dtype)

        p = _dot(a_ref[...].astype(BF), b_ref[...].astype(BF), mode)
        if nk == 1:
            finish(p)
        else:
            acc_ref = refs[3 + n_extra]
            k = pl.program_id(2)

            @pl.when(k == 0)
            def _():
                acc_ref[...] = p

            @pl.when(k > 0)
            def _():
                acc_ref[...] += p

            @pl.when(k == nk - 1)
            def _():
                finish(acc_ref[...])

    blk = (_nbytes((tm, tk), a.dtype) + _nbytes((tk, tn), b.dtype) + _nbytes((tm, tn), out_dtype)
           + sum(_nbytes((tm, tn), e.dtype) for e in extra) + 2 * _nbytes((tm, tn), F32))
    return pl.pallas_call(
        body, name=name, grid=grid,
        in_specs=[a_spec, b_spec] + [o_spec] * n_extra,
        out_specs=o_spec,
        out_shape=jax.ShapeDtypeStruct((M, N), out_dtype),
        scratch_shapes=[pltpu.VMEM((tm, tn), F32)] if nk > 1 else [],
        compiler_params=pltpu.CompilerParams(
            dimension_semantics=("parallel", "parallel", "arbitrary"), vmem_limit_bytes=_vmem_limit(blk)),
    )(a, b, *extra)


def _ew(fn, T, tm, ins, consts, outs, accs, *, name, reverse=False):
    tm = min(tm, T)
    nt = T // tm
    n_in, n_c, n_o, n_a = len(ins), len(consts), len(outs), len(accs)

    def row(i):
        return nt - 1 - i if reverse else i

    in_specs = [pl.BlockSpec((tm, w), functools.partial(lambda i, cb: (row(i), cb), cb=cb)) for (_, w, cb) in ins]
    in_specs += [pl.BlockSpec(c.shape, functools.partial(lambda i, nd: (0,) * nd, nd=c.ndim)) for c in consts]
    out_specs = [pl.BlockSpec((tm, w), lambda i: (row(i), 0)) for (w, _) in outs]
    out_specs += [pl.BlockSpec((r, w), lambda i: (0, 0)) for (r, w) in accs]
    out_shape = [jax.ShapeDtypeStruct((T, w), dt) for (w, dt) in outs]
    out_shape += [jax.ShapeDtypeStruct((r, w), F32) for (r, w) in accs]

    def body(*refs):
        in_refs = refs[:n_in]
        c_refs = refs[n_in:n_in + n_c]
        o_refs = refs[n_in + n_c:n_in + n_c + n_o]
        a_refs = refs[n_in + n_c + n_o:]
        ov, av = fn([r[...] for r in in_refs], [r[...] for r in c_refs])
        for r, v in zip(o_refs, ov):
            r[...] = v.astype(r.dtype)
        if n_a:
            i = pl.program_id(0)

            @pl.when(i == 0)
            def _():
                for r, v in zip(a_refs, av):
                    r[...] = v

            @pl.when(i > 0)
            def _():
                for r, v in zip(a_refs, av):
                    r[...] += v

    blk = (sum(_nbytes((tm, w), a.dtype) for (a, w, _) in ins) + sum(_nbytes(c.shape, c.dtype) for c in consts)
           + sum(_nbytes((tm, w), dt) for (w, dt) in outs) + sum(_nbytes(s, F32) for s in accs))
    res = pl.pallas_call(
        body, name=name, grid=(nt,), in_specs=in_specs, out_specs=out_specs, out_shape=out_shape,
        compiler_params=pltpu.CompilerParams(
            dimension_semantics=("arbitrary",), vmem_limit_bytes=_vmem_limit(blk)),
    )(*[a for (a, _, _) in ins], *consts)
    return res


def _colsum(v):
    return jnp.sum(v, axis=0, keepdims=True)


def _fgate_fwd(fl, fb, T, tm=512):
    tm = min(tm, T)

    def body(fl_ref, fb_ref, f_ref, carry_ref):
        i = pl.program_id(0)

        @pl.when(i == 0)
        def _():
            carry_ref[...] = jnp.zeros_like(carry_ref)

        row = lax.broadcasted_iota(jnp.int32, (SLAB, DH), 0)

        def slab(s, carry):
            r0 = pl.multiple_of(s * SLAB, SLAB)
            z = fl_ref[pl.ds(r0, SLAB), :] + fb_ref[...]
            c = jnp.minimum(z, 0.0) - _log1p(jnp.exp(-jnp.abs(z)))
            for k in (1, 2, 4):
                c = c + jnp.where(row >= k, pltpu.roll(c, k, 0), 0.0)
            c = c + carry
            f_ref[pl.ds(r0, SLAB), :] = c
            return c[SLAB - 1:SLAB, :]

        carry_ref[0:1, :] = lax.fori_loop(0, tm // SLAB, slab, carry_ref[0:1, :])

    return pl.pallas_call(
        body, name="fgate_fwd", grid=(T // tm,),
        in_specs=[pl.BlockSpec((tm, DH), lambda i: (i, 0)), pl.BlockSpec((1, DH), lambda i: (0, 0))],
        out_specs=pl.BlockSpec((tm, DH), lambda i: (i, 0)),
        out_shape=jax.ShapeDtypeStruct((T, DH), F32),
        scratch_shapes=[pltpu.VMEM((SLAB, DH), F32)],
        compiler_params=pltpu.CompilerParams(dimension_semantics=("arbitrary",)),
    )(fl, fb)


def _fgate_bwd(dF, fl, fb, T, tm=512):
    tm = min(tm, T)
    nt = T // tm

    def body(df_ref, fl_ref, fb_ref, o_ref, acc_ref, carry_ref):
        i = pl.program_id(0)

        @pl.when(i == 0)
        def _():
            carry_ref[...] = jnp.zeros_like(carry_ref)
            acc_ref[...] = jnp.zeros_like(acc_ref)

        row = lax.broadcasted_iota(jnp.int32, (SLAB, DH), 0)

        def slab(n, carry):
            g_next, acc = carry
            r0 = pl.multiple_of((tm // SLAB - 1 - n) * SLAB, SLAB)
            c = df_ref[pl.ds(r0, SLAB), :]
            for k in (1, 2, 4):
                c = c + jnp.where(row < SLAB - k, pltpu.roll(c, SLAB - k, 0), 0.0)
            c = c + g_next
            z = fl_ref[pl.ds(r0, SLAB), :] + fb_ref[...]
            dfl = c * _sig(-z)
            o_ref[pl.ds(r0, SLAB), :] = dfl.astype(o_ref.dtype)
            return c[0:1, :], acc + _colsum(dfl)

        g, acc = lax.fori_loop(0, tm // SLAB, slab, (carry_ref[0:1, :], jnp.zeros((1, DH), F32)))
        carry_ref[0:1, :] = g
        acc_ref[...] += acc

    return pl.pallas_call(
        body, name="fgate_bwd", grid=(nt,),
        in_specs=[pl.BlockSpec((tm, DH), lambda i: (nt - 1 - i, 0)), pl.BlockSpec((tm, DH), lambda i: (nt - 1 - i, 0)),
                  pl.BlockSpec((1, DH), lambda i: (0, 0))],
        out_specs=[pl.BlockSpec((tm, DH), lambda i: (nt - 1 - i, 0)), pl.BlockSpec((1, DH), lambda i: (0, 0))],
        out_shape=[jax.ShapeDtypeStruct((T, DH), BF), jax.ShapeDtypeStruct((1, DH), F32)],
        scratch_shapes=[pltpu.VMEM((SLAB, DH), F32)],
        compiler_params=pltpu.CompilerParams(dimension_semantics=("arbitrary",)),
    )(dF, fl, fb)


def _conv(x, prev8, cw, cb):
    xs = [x] + [_shift_down(x, d, prev8) for d in (1, 2, 3)]
    xa = cb + cw[3:4, :] * xs[0] + cw[2:3, :] * xs[1] + cw[1:2, :] * xs[2] + cw[0:1, :] * xs[3]
    return xa, xs


def _lru_gates(xa_g, wa_g, wx_g, ba_g, bx_g, sp_g):
    xb = xa_g.astype(BF)
    r = _sig(_dot(xb, wa_g) + ba_g)
    ig = _sig(_dot(xb, wx_g) + bx_g)
    la = -LRU_C * r * sp_g
    a = jnp.exp(la)
    mult = jnp.sqrt(_one_minus_sq(a, la))
    return r, ig, a, mult


def _lru_fwd(xg, cw, vec, wabd, wxbd, T, tm=256):
    tm = min(tm, T)
    nsl = tm // SLAB

    def body(x_ref, xp_ref, cw_ref, vec_ref, wa_ref, wx_ref, h_ref, a_s, b_s, carry_ref):
        i = pl.program_id(0)

        @pl.when(i == 0)
        def _():
            carry_ref[...] = jnp.zeros_like(carry_ref)

        x = x_ref[...]
        prev8 = jnp.where(i > 0, xp_ref[...], 0.0)
        vec_v = vec_ref[...]
        xa, _ = _conv(x, prev8, cw_ref[...], vec_v[0:1, :])
        sp = _softplus(-vec_v[3:4, :])
        for g in range(NBD):
            sl = slice(g * BD, (g + 1) * BD)
            _, ig, a, mult = _lru_gates(xa[:, sl], wa_ref[g], wx_ref[g], vec_v[1:2, sl], vec_v[2:3, sl], sp[:, sl])
            a_s[:, sl] = a
            b_s[:, sl] = mult * ig * xa[:, sl]

        def slab(s, carry):
            r0 = pl.multiple_of(s * SLAB, SLAB)
            A, B = _slab_scan_fwd(a_s[pl.ds(r0, SLAB), :], b_s[pl.ds(r0, SLAB), :])
            h = A * carry + B
            h_ref[pl.ds(r0, SLAB), :] = h
            return h[SLAB - 1:SLAB, :]

        carry_ref[0:1, :] = lax.fori_loop(0, nsl, slab, carry_ref[0:1, :])

    blk = 5 * _nbytes((tm, D), F32) + 2 * _nbytes((NBD, BD, BD), BF)
    return pl.pallas_call(
        body, name="lru_fwd", grid=(T // tm,),
        in_specs=[pl.BlockSpec((tm, D), lambda i: (i, 0)),
                  pl.BlockSpec((SLAB, D), lambda i: (jnp.maximum(i * nsl - 1, 0), 0)),
                  pl.BlockSpec((CONV, D), lambda i: (0, 0)),
                  pl.BlockSpec((SLAB, D), lambda i: (0, 0)),
                  pl.BlockSpec((NBD, BD, BD), lambda i: (0, 0, 0)),
                  pl.BlockSpec((NBD, BD, BD), lambda i: (0, 0, 0))],
        out_specs=pl.BlockSpec((tm, D), lambda i: (i, 0)),
        out_shape=jax.ShapeDtypeStruct((T, D), F32),
        scratch_shapes=[pltpu.VMEM((tm, D), F32), pltpu.VMEM((tm, D), F32), pltpu.VMEM((SLAB, D), F32)],
        compiler_params=pltpu.CompilerParams(dimension_semantics=("arbitrary",), vmem_limit_bytes=_vmem_limit(blk)),
    )(xg, xg, cw, vec, wabd, wxbd)


def _lru_bwd(xg, h, dha, cw, vec, wabd, wxbd, T, tm=256):
    tm = min(tm, T)
    nsl = tm // SLAB
    nt = T // tm

    def body(x_ref, xp_ref, h_ref, hp_ref, dh_ref, cw_ref, vec_ref, wa_ref, wx_ref,
             dx_ref, dwa_ref, dwx_ref, acc_ref, a_s, b_s, g_s, dxa_s, carry_ref, dxan_ref):
        n = pl.program_id(0)
        it = nt - 1 - n

        @pl.when(n == 0)
        def _():
            carry_ref[...] = jnp.zeros_like(carry_ref)
            dxan_ref[...] = jnp.zeros_like(dxan_ref)
            dwa_ref[...] = jnp.zeros_like(dwa_ref)
            dwx_ref[...] = jnp.zeros_like(dwx_ref)
            acc_ref[...] = jnp.zeros_like(acc_ref)

        x = x_ref[...]
        prev8 = jnp.where(it > 0, xp_ref[...], 0.0)
        hprev8 = jnp.where(it > 0, hp_ref[...], 0.0)
        vec_v = vec_ref[...]
        cw_v = cw_ref[...]
        xa, xs = _conv(x, prev8, cw_v, vec_v[0:1, :])
        sp = _softplus(-vec_v[3:4, :])
        gates = []
        for g in range(NBD):
            sl = slice(g * BD, (g + 1) * BD)
            r, ig, a, mult = _lru_gates(xa[:, sl], wa_ref[g], wx_ref[g], vec_v[1:2, sl], vec_v[2:3, sl], sp[:, sl])
            gates.append((r, ig, a, mult))
            a_s[:, sl] = a
        a_next = _shift_up(a_s[...], 1, carry_ref[...])
        a_s[...] = a_next
        b_s[...] = dh_ref[...]

        def slab(m, carry):
            r0 = pl.multiple_of((nsl - 1 - m) * SLAB, SLAB)
            A, B = _slab_scan_bwd(a_s[pl.ds(r0, SLAB), :], b_s[pl.ds(r0, SLAB), :])
            gg = A * carry + B
            g_s[pl.ds(r0, SLAB), :] = gg
            return gg[0:1, :]

        g_first = lax.fori_loop(0, nsl, slab, carry_ref[1:2, :])
        gt = g_s[...]
        h_prev = _shift_down(h_ref[...], 1, hprev8)
        dba = []
        dbx = []
        dsp = []
        for g in range(NBD):
            sl = slice(g * BD, (g + 1) * BD)
            r, ig, a, mult = gates[g]
            xa_g = xa[:, sl]
            g_g = gt[:, sl]
            da = g_g * h_prev[:, sl]
            dmult = g_g * ig * xa_g
            di = g_g * mult * xa_g
            dxa_g = g_g * mult * ig
            dla = da * a - dmult * (a * a / mult)
            dr = dla * (-LRU_C) * sp[:, sl]
            dsp.append(_colsum(dla * (-LRU_C) * r))
            dra = (dr * r * (1.0 - r))
            dix = (di * ig * (1.0 - ig))
            dba.append(_colsum(dra))
            dbx.append(_colsum(dix))
            dra_b = dra.astype(BF)
            dix_b = dix.astype(BF)
            xb = xa_g.astype(BF)
            dxa_g = dxa_g + _dot(dra_b, wa_ref[g], "nt") + _dot(dix_b, wx_ref[g], "nt")
            dwa_ref[g] += _dot(xb, dra_b, "tn")
            dwx_ref[g] += _dot(xb, dix_b, "tn")
            dxa_s[:, sl] = dxa_g
        dxa = dxa_s[...]
        nxt = dxan_ref[...]
        dx = (cw_v[3:4, :] * dxa + cw_v[2:3, :] * _shift_up(dxa, 1, nxt)
              + cw_v[1:2, :] * _shift_up(dxa, 2, nxt) + cw_v[0:1, :] * _shift_up(dxa, 3, nxt))
        dx_ref[...] = dx.astype(dx_ref.dtype)
        acc_ref[0:1, :] += jnp.concatenate(dba, axis=1)
        acc_ref[1:2, :] += jnp.concatenate(dbx, axis=1)
        acc_ref[2:3, :] += jnp.concatenate(dsp, axis=1)
        acc_ref[3:4, :] += _colsum(dxa)
        for k in range(CONV):
            acc_ref[4 + k:5 + k, :] += _colsum(dxa * xs[CONV - 1 - k])
        dxan_ref[...] = dxa[0:SLAB, :]
        a_first = jnp.concatenate([gates[g][2][0:1, :] for g in range(NBD)], axis=1)
        carry_ref[0:1, :] = a_first
        carry_ref[1:2, :] = g_first

        @pl.when(n == nt - 1)
        def _():
            acc_ref[2:3, :] = acc_ref[2:3, :] * (-_sig(-vec_v[3:4, :]))

    rowblk = lambda i: (nt - 1 - i, 0)
    prevblk = lambda i: (jnp.maximum((nt - 1 - i) * nsl - 1, 0), 0)
    c2 = lambda i: (0, 0)
    c3 = lambda i: (0, 0, 0)
    blk = 12 * _nbytes((tm, D), F32) + 6 * _nbytes((NBD, BD, BD), F32)
    return pl.pallas_call(
        body, name="lru_bwd", grid=(nt,),
        in_specs=[pl.BlockSpec((tm, D), rowblk), pl.BlockSpec((SLAB, D), prevblk),
                  pl.BlockSpec((tm, D), rowblk), pl.BlockSpec((SLAB, D), prevblk),
                  pl.BlockSpec((tm, D), rowblk),
                  pl.BlockSpec((CONV, D), c2), pl.BlockSpec((SLAB, D), c2),
                  pl.BlockSpec((NBD, BD, BD), c3), pl.BlockSpec((NBD, BD, BD), c3)],
        out_specs=[pl.BlockSpec((tm, D), rowblk), pl.BlockSpec((NBD, BD, BD), c3), pl.BlockSpec((NBD, BD, BD), c3),
                   pl.BlockSpec((16, D), c2)],
        out_shape=[jax.ShapeDtypeStruct((T, D), BF), jax.ShapeDtypeStruct((NBD, BD, BD), F32),
                   jax.ShapeDtypeStruct((NBD, BD, BD), F32), jax.ShapeDtypeStruct((16, D), F32)],
        scratch_shapes=[pltpu.VMEM((tm, D), F32), pltpu.VMEM((tm, D), F32), pltpu.VMEM((tm, D), F32),
                        pltpu.VMEM((tm, D), F32), pltpu.VMEM((SLAB, D), F32), pltpu.VMEM((SLAB, D), F32)],
        compiler_params=pltpu.CompilerParams(dimension_semantics=("arbitrary",), vmem_limit_bytes=_vmem_limit(blk)),
    )(xg, xg, h, h, dha, cw, vec, wabd, wxbd)


_SCALE = 1.0 / math.sqrt(DH)


_ANY = pl.BlockSpec(memory_space=pl.ANY)


class _Side:
    def __init__(self, srcs, outs, nsem, copies):
        self.srcs, self.outs, self.nsem, self.copies = list(srcs), list(outs), nsem, copies


def _pallas(body, operands, *, name, grid, in_specs, out_specs, out_shape, scratch_shapes=(), semantics,
            vmem=None, side=None):
    if side is None:
        return pl.pallas_call(
            body, name=name, grid=grid, in_specs=in_specs, out_specs=out_specs, out_shape=out_shape,
            scratch_shapes=list(scratch_shapes),
            compiler_params=pltpu.CompilerParams(dimension_semantics=semantics, vmem_limit_bytes=vmem),
        )(*operands)
    n_in, n_out, n_scr = len(in_specs), len(out_specs), len(scratch_shapes)
    ns, no = len(side.srcs), len(side.outs)

    def hosted(*refs):
        ins, refs = refs[:n_in], refs[n_in:]
        sin, refs = refs[:ns], refs[ns:]
        outs, refs = refs[:n_out], refs[n_out:]
        sout, refs = refs[:no], refs[no:]
        scr, (send, recv) = refs[:n_scr], refs[n_scr:]
        ids = [pl.program_id(a) for a in range(len(grid))]
        first = functools.reduce(jnp.logical_and, [i == 0 for i in ids])
        last = functools.reduce(jnp.logical_and, [i == g - 1 for i, g in zip(ids, grid)])

        @pl.when(first)
        def _():
            for cp in side.copies(sin, sout, send, recv):
                cp.start()

        body(*ins, *outs, *scr)

        @pl.when(last)
        def _():
            for cp in side.copies(sin, sout, send, recv):
                cp.wait()

    return pl.pallas_call(
        hosted, name=name, grid=grid, in_specs=list(in_specs) + [_ANY] * ns, out_specs=list(out_specs) + [_ANY] * no,
        out_shape=list(out_shape) + side.outs,
        scratch_shapes=list(scratch_shapes) + [pltpu.SemaphoreType.DMA((side.nsem,)), pltpu.SemaphoreType.DMA((side.nsem,))],
        compiler_params=pltpu.CompilerParams(dimension_semantics=("arbitrary",) * len(grid), vmem_limit_bytes=vmem),
    )(*operands, *side.srcs)


DA = 2 * DH
_LOG2E = math.log2(math.e)
_C2 = _SCALE * _LOG2E


def _aug_fn(ins, cs):
    q, k, fcum = ins
    g_all = fcum * _LOG2E
    lane = lax.broadcasted_iota(jnp.int32, (q.shape[0], DH), 1)
    qa, ka = [], []
    for hd in range(NH):
        g = g_all[:, hd:hd + 1]
        hi = g.astype(BF).astype(F32)
        mid = (g - hi).astype(BF).astype(F32)
        lo = ((g - hi) - mid).astype(BF).astype(F32)
        qx = jnp.where(lane == 0, hi, jnp.where(lane == 1, mid, jnp.where(lane == 2, lo,
                                                                          jnp.where(lane < 6, 1.0, 0.0))))
        kx = jnp.where(lane < 3, 1.0, jnp.where(lane == 3, -hi, jnp.where(lane == 4, -mid,
                                                                          jnp.where(lane == 5, -lo, 0.0))))
        qa += [q[:, hd * DH:(hd + 1) * DH], qx.astype(BF)]
        ka += [k[:, hd * DH:(hd + 1) * DH], kx.astype(BF)]
    return [jnp.concatenate(qa, axis=1), jnp.concatenate(ka, axis=1)], []


_KA_ONES = DH + 3
KT_ONES = 16
ATTN_CHAINS = 1


def _attn_fwd(qa, ka, qkv, T, blk=512, side=None):
    blk = min(blk, T)
    nb = T // blk
    bqs = blk // ATTN_CHAINS

    def body(q_ref, k_ref, vn_ref, o_ref, lse_ref, v_ref):
        i = pl.program_id(1)

        @pl.when(i == 0)
        def _():
            for jj in range(nb):
                v_ref[jj] = vn_ref[jj * blk:(jj + 1) * blk, :].astype(F32).T.astype(BF)

        qs = [q_ref[pl.ds(c * bqs, bqs), :] for c in range(ATTN_CHAINS)]

        def scores(j, c):
            r0 = pl.multiple_of(j * blk, blk)
            return _dot(k_ref[pl.ds(r0, blk), :], qs[c], "nt")

        def update(s, vj, carry):
            m, l, acc = carry
            m_new = jnp.maximum(m, jnp.max(s, axis=0, keepdims=True))
            alpha = jnp.exp2(m - m_new)
            p = jnp.exp2(s - m_new)
            l = alpha * l + jnp.sum(p, axis=0, keepdims=True)
            acc = alpha * acc + _dot(vj, p.astype(BF))
            return m_new, l, acc

        def step(j, carry):
            ss, st = carry
            nxt = tuple(scores(j + 1, c) for c in range(ATTN_CHAINS))
            vj = v_ref[j]
            return nxt, tuple(update(ss[c], vj, st[c]) for c in range(ATTN_CHAINS))

        init = tuple((jnp.full((1, bqs), -jnp.inf, F32), jnp.zeros((1, bqs), F32), jnp.zeros((DH, bqs), F32))
                     for _ in range(ATTN_CHAINS))
        first = tuple(scores(0, c) for c in range(ATTN_CHAINS))
        last, carry = lax.fori_loop(0, i, step, (first, init))
        rk = lax.broadcasted_iota(jnp.int32, (blk, bqs), 0)
        cq = lax.broadcasted_iota(jnp.int32, (blk, bqs), 1)
        vi = v_ref[i]
        for c in range(ATTN_CHAINS):
            s = jnp.where(cq + c * bqs >= rk, last[c], -jnp.inf)
            m, l, acc = update(s, vi, carry[c])
            o_ref[c * bqs:(c + 1) * bqs, :] = (acc / l).T.astype(o_ref.dtype)
            lse_ref[:, c * bqs:(c + 1) * bqs] = m + jnp.log(l) * _LOG2E

    vm = _nbytes((T, DA), BF) + 2 * _nbytes((T, DH), BF) + 6 * _nbytes((blk, blk), F32)
    return _pallas(
        body, (qa, ka, qkv), name="attn_fwd", grid=(NH, nb),
        in_specs=[pl.BlockSpec((blk, DA), lambda h, i: (i, h)),
                  pl.BlockSpec((T, DA), lambda h, i: (0, h)),
                  pl.BlockSpec((T, DH), lambda h, i: (0, 2 * NH + h))],
        out_specs=[pl.BlockSpec((blk, DH), lambda h, i: (i, h)),
                   pl.BlockSpec((None, None, 1, blk), lambda h, i: (h, i, 0, 0))],
        out_shape=[jax.ShapeDtypeStruct((T, D), BF), jax.ShapeDtypeStruct((NH, nb, 1, blk), F32)],
        scratch_shapes=[pltpu.VMEM((nb, DH, blk), BF)],
        semantics=("parallel", "arbitrary"), vmem=_vmem_limit(vm), side=side)


def _attn_bwd(qa, ka, qkv, do, lrow, drow, T, blk=512, side=None):
    blk = min(blk, T)
    nb = T // blk

    def body(ka_ref, v_ref, qa_ref, do_ref, l_ref, d_ref, dq_ref, dk_ref, dv_ref, dfs_ref, dft_ref, dq_s):
        j = pl.program_id(1)

        @pl.when(j == 0)
        def _():
            dq_s[...] = jnp.zeros_like(dq_s)

        row = lax.broadcasted_iota(jnp.int32, (DH + KT_ONES, blk), 0)
        dq_scale = jnp.where(row < DH, _SCALE, 1.0)

        kaj = ka_ref[...]
        ktj = jnp.concatenate([kaj[:, :DH].astype(F32).T.astype(BF), jnp.ones((KT_ONES, blk), BF)], axis=0)
        vj = v_ref[...]

        def step(i, carry, diag):
            dka, dv = carry
            r0 = pl.multiple_of(i * blk, blk)
            qi = qa_ref[pl.ds(r0, blk), :]
            doi = do_ref[pl.ds(r0, blk), :]
            st = _dot(kaj, qi, "nt") - l_ref[i]
            if diag:
                rk = lax.broadcasted_iota(jnp.int32, (blk, blk), 0)
                cq = lax.broadcasted_iota(jnp.int32, (blk, blk), 1)
                st = jnp.where(cq >= rk, st, -jnp.inf)
            pt = jnp.exp2(st)
            dv = dv + _dot(pt.astype(BF), doi)
            dpt = _dot(vj, doi, "nt")
            dst = pt * (dpt - d_ref[i])
            dsb = dst.astype(BF)
            dka = dka + _dot(dsb, qi)
            dq_s[i] += _dot(ktj, dsb) * dq_scale
            return dka, dv

        init = (jnp.zeros((blk, DA), F32), jnp.zeros((blk, DH), F32))
        carry = step(j, init, True)
        dka, dv = lax.fori_loop(j + 1, nb, lambda i, c: step(i, c, False), carry)
        dk_ref[...] = (dka[:, :DH] * (1.0 / _LOG2E)).astype(dk_ref.dtype)
        dv_ref[...] = dv.astype(dv_ref.dtype)
        dfs_ref[...] = dka[:, DH:].T[_KA_ONES - DH:_KA_ONES - DH + 1, :]

        @pl.when(j == nb - 1)
        def _():
            for ii in range(nb):
                t = dq_s[ii]
                dq_ref[ii * blk:(ii + 1) * blk, :] = t[:DH].T.astype(dq_ref.dtype)
                dft_ref[ii] = t[DH:DH + 1]

    rowv = pl.BlockSpec((None, nb, 1, blk), lambda h, j: (h, 0, 0, 0))
    vm = (_nbytes((T, DA), BF) + 2 * _nbytes((T, DH), BF) + _nbytes((T, DH + KT_ONES), F32)
          + 8 * _nbytes((blk, blk), F32))
    return _pallas(
        body, (ka, qkv, qa, do, lrow, drow), name="attn_bwd", grid=(NH, nb),
        in_specs=[pl.BlockSpec((blk, DA), lambda h, j: (j, h)),
                  pl.BlockSpec((blk, DH), lambda h, j: (j, 2 * NH + h)),
                  pl.BlockSpec((T, DA), lambda h, j: (0, h)),
                  pl.BlockSpec((T, DH), lambda h, j: (0, h)),
                  rowv, rowv],
        out_specs=[pl.BlockSpec((T, DH), lambda h, j: (0, h)),
                   pl.BlockSpec((blk, DH), lambda h, j: (j, h)),
                   pl.BlockSpec((blk, DH), lambda h, j: (j, h)),
                   pl.BlockSpec((None, None, 1, blk), lambda h, j: (h, j, 0, 0)), rowv],
        out_shape=[jax.ShapeDtypeStruct((T, D), BF), jax.ShapeDtypeStruct((T, D), BF),
                   jax.ShapeDtypeStruct((T, D), BF), jax.ShapeDtypeStruct((NH, nb, 1, blk), F32),
                   jax.ShapeDtypeStruct((NH, nb, 1, blk), F32)],
        scratch_shapes=[pltpu.VMEM((nb, DH + KT_ONES, blk), F32)],
        semantics=("parallel", "arbitrary"), vmem=_vmem_limit(vm), side=side)


def _norm_fn(ins, cs):
    x, = ins
    g, = cs
    r = lax.rsqrt(jnp.mean(x * x, axis=-1, keepdims=True) + EPS)
    return [x * r * g], []


def _norm_bwd_fn(ins, cs):
    x, dy, dres = ins
    g, = cs
    r = lax.rsqrt(jnp.mean(x * x, axis=-1, keepdims=True) + EPS)
    xh = x * r
    dxh = dy * g
    dx = dres + r * (dxh - xh * jnp.mean(dxh * xh, axis=-1, keepdims=True))
    return [dx], [_colsum(dy * xh)]


def _final_fn(ins, cs):
    x2, tgt = ins
    g, = cs
    r = lax.rsqrt(jnp.mean(x2 * x2, axis=-1, keepdims=True) + EPS)
    xh = x2 * r
    e = xh * g - tgt
    dy = e * (1.0 / D)
    dxh = dy * g
    dx2 = r * (dxh - xh * jnp.mean(dxh * xh, axis=-1, keepdims=True))
    return [dx2], [_colsum(0.5 * e * e * (1.0 / D)), _colsum(dy * xh)]


def _z_fn(ins, cs):
    g, h = ins
    return [_gelu(g) * h], []


def _mix_fn(ins, cs):
    gates, ya, yb = ins
    return [_sig(gates[:, :D]) * ya + _sig(gates[:, D:]) * yb], []


def _mix_bwd_fn(ins, cs):
    dmix, gates, ya, yb = ins
    ga = _sig(gates[:, :D])
    gb = _sig(gates[:, D:])
    dgates = jnp.concatenate([dmix * ya * ga * (1.0 - ga), dmix * yb * gb * (1.0 - gb)], axis=1)
    return [dmix * ga, dmix * gb, dgates], []


def _z_bwd_fn(ins, cs):
    dz, g, h = ins
    return [dz * _gelu(g), dz * h * _gelu_grad(g)], []


def _delta_fn(ins, cs):
    do, o = ins
    p = do.astype(F32) * o.astype(F32)
    lane = lax.broadcasted_iota(jnp.int32, (p.shape[0], DH), 1)
    out = jnp.zeros((p.shape[0], DH), F32)
    for hd in range(NH):
        s = jnp.sum(p[:, hd * DH:(hd + 1) * DH], axis=1, keepdims=True)
        out = jnp.where(lane == hd, s, out)
    return [out], []


def _du_all(pieces, win, T, tm=256, side=None):
    tm = min(tm, T)
    n = len(pieces)

    def body(*refs):
        w_ref, o_ref = refs[n], refs[n + 1]
        acc = None
        for (a, off), a_ref in zip(pieces, refs[:n]):
            d = _dot(a_ref[...].astype(BF), w_ref[:, off:off + a.shape[1]], "nt")
            acc = d if acc is None else acc + d
        o_ref[...] = acc

    vm = (sum(_nbytes((tm, a.shape[1]), a.dtype) for a, _ in pieces) + _nbytes(win.shape, win.dtype)
          + 2 * _nbytes((tm, D), F32))
    return _pallas(
        body, tuple(a for a, _ in pieces) + (win,), name="du_all", grid=(T // tm,),
        in_specs=[pl.BlockSpec((tm, a.shape[1]), lambda i: (i, 0)) for a, _ in pieces]
        + [pl.BlockSpec(win.shape, lambda i: (0, 0))],
        out_specs=[pl.BlockSpec((tm, D), lambda i: (i, 0))],
        out_shape=[jax.ShapeDtypeStruct((T, D), F32)],
        semantics=("arbitrary",), vmem=int(min(VMEM_CAP, 2 * vm + (4 << 20))), side=side)


def _local_step(x, tgt, w, T, blk=1024, dist=None):
    blk = min(blk, T)
    nb = T // blk
    win = w["win"]

    u, = _ew(_norm_fn, T, 512, [(x, D, 0)], [w["g_mix"]], [(D, BF)], [], name="norm_mix")
    xg = _mm(u, win, "nn", T, 2 * D, D, name="proj_lru")
    qkv = _mm(u, win, "nn", T, 3 * D, D, name="proj_qkv", out_dtype=BF, b_off=(0, 2),
              epi=lambda acc: acc * jnp.where(pl.program_id(1) == 0, _C2, 1.0))
    gates = _mm(u, win, "nn", T, 2 * D, D, name="proj_gates", b_off=(0, 5))
    fl = _mm(u, win, "nn", T, DH, D, name="proj_f", tn=DH, b_off=(0, 7 * D // DH))
    fcum = _fgate_fwd(fl, w["fb"], T)
    qa, ka = _ew(_aug_fn, T, 256, [(qkv, D, 0), (qkv, D, 1), (fcum, DH, 0)], [],
                 [(NH * DA, BF), (NH * DA, BF)], [], name="attn_augment")

    h = _lru_fwd(xg, w["cw"], w["vec"], w["wabd"], w["wxbd"], T)
    ob, lse, *landed = _attn_fwd(qa, ka, qkv, T, blk, side=dist.weights_side() if dist else None)
    if dist:
        w = dict(w, **dist.weights_landed(landed))
    z, = _ew(_z_fn, T, 512, [(xg, D, 1), (h, D, 0)], [], [(D, BF)], [], name="lru_gelu")
    ya = _mm(z, w["wa"], "nn", T, D, D, name="branch_a")
    yb = _mm(ob, w["wb"], "nn", T, D, D, name="branch_b")
    mix, = _ew(_mix_fn, T, 256, [(gates, 2 * D, 0), (ya, D, 0), (yb, D, 0)], [], [(D, BF)], [], name="mix")
    x1 = _mm(mix, w["wout"], "nn", T, D, D, name="out_proj", add=x)
    m, = _ew(_norm_fn, T, 512, [(x1, D, 0)], [w["g_mlp"]], [(D, BF)], [], name="norm_mlp")
    hh = _mm(m, w["wup"], "nn", T, FF, D, name="mlp_up", out_dtype=BF,
             epi=lambda acc: jnp.square(jnp.maximum(acc, 0.0)))
    x2 = _mm(hh, w["wdown"], "nn", T, D, FF, name="mlp_down", add=x1)
    dx2, loss_vec, dg_fin = _ew(_final_fn, T, 256, [(x2, D, 0), (tgt, D, 0)], [w["g_fin"]], [(D, F32)],
                                [(1, D), (1, D)], name="final_norm_loss")

    dhpre = _mm(dx2, w["wdown"], "nt", T, FF, D, name="mlp_down_bwd", out_dtype=BF,
                epi=lambda acc, h2: acc * (2.0 * jnp.sqrt(h2.astype(F32))), epi_ins=[hh])
    dwdown = _mm(hh, dx2, "tn", FF, D, T, name="dw_down", out_dtype=BF)
    dwup = _mm(m, dhpre, "tn", D, FF, T, name="dw_up", out_dtype=BF)
    dm = _mm(dhpre, w["wup"], "nt", T, D, FF, name="mlp_up_bwd")
    dx1, dg_mlp = _ew(_norm_bwd_fn, T, 256, [(x1, D, 0), (dm, D, 0), (dx2, D, 0)], [w["g_mlp"]], [(D, F32)],
                      [(1, D)], name="norm_mlp_bwd")

    dmix = _mm(dx1, w["wout"], "nt", T, D, D, name="out_proj_bwd")
    dwout = _mm(mix, dx1, "tn", D, D, T, name="dw_out", out_dtype=BF)
    dya, dyb, dgates = _ew(_mix_bwd_fn, T, 256, [(dmix, D, 0), (gates, 2 * D, 0), (ya, D, 0), (yb, D, 0)], [],
                           [(D, BF), (D, BF), (2 * D, BF)], [], name="mix_bwd")
    dob = _mm(dyb, w["wb"], "nt", T, D, D, name="branch_b_bwd", out_dtype=BF)
    dwb = _mm(ob, dyb, "tn", D, D, T, name="dw_b", out_dtype=BF)
    dz = _mm(dya, w["wa"], "nt", T, D, D, name="branch_a_bwd")
    dwa = _mm(z, dya, "tn", D, D, T, name="dw_a", out_dtype=BF)
    dha, dglru = _ew(_z_bwd_fn, T, 256, [(dz, D, 0), (xg, D, 1), (h, D, 0)], [], [(D, F32), (D, BF)], [],
                     name="lru_gelu_bwd")

    delta, = _ew(_delta_fn, T, 512, [(dob, D, 0), (ob, D, 0)], [], [(DH, F32)], [], name="attn_delta")
    drow = delta[:, :NH].T.reshape(NH, nb, 1, blk)
    big = dict(w_branch_a=dwa, w_branch_b=dwb, w_out=dwout, w_up=dwup, w_down=dwdown)
    side = dist.grads_side(big) if dist else None
    dq, dk, dv, dfs, dft, *landed = _attn_bwd(qa, ka, qkv, dob, lse, drow, T, blk, side=side)
    if dist:
        big = dist.grads_landed(side, landed)
    dfcum = jnp.pad((dft - dfs).reshape(NH, T).T, ((0, 0), (0, DH - NH)))
    dfl, dfb = _fgate_bwd(dfcum, fl, w["fb"], T)

    dxl, dwabd, dwxbd, lacc = _lru_bwd(xg, h, dha, w["cw"], w["vec"], w["wabd"], w["wxbd"], T)

    dproj = ((dxl, 0), (dglru, D), (dq, 2 * D), (dk, 3 * D), (dv, 4 * D), (dgates, 5 * D), (dfl, 7 * D))
    pieces = [_mm(u, p, "tn", D, p.shape[1], T, name="dw_in_%d" % n, out_dtype=BF)
              for n, (p, _) in enumerate(dproj)]
    pieces[-1] = pieces[-1][:, :NH]
    dwin = dict(w_in=jnp.concatenate(pieces, axis=1))
    side = dist.grads_side(dwin) if dist else None
    du, *landed = _du_all(dproj, win, T, side=side)
    big.update(dist.grads_landed(side, landed) if dist else dwin)
    dx, dg_mix = _ew(_norm_bwd_fn, T, 256, [(x, D, 0), (du, D, 0), (dx1, D, 0)], [w["g_mix"]], [(D, F32)],
                     [(1, D)], name="norm_mix_bwd")

    return dict(dx=dx, big=big, dwabd=dwabd, dwxbd=dwxbd, lacc=lacc, dfb=dfb, dg_mix=dg_mix, dg_mlp=dg_mlp,
                dg_fin=dg_fin, loss_vec=loss_vec)


def _block_diag(w):
    per = BD // LRU_BW
    w4 = w.reshape(NBD, per, LRU_BW, LRU_BW)
    out = jnp.zeros((NBD, per, LRU_BW, per, LRU_BW), w.dtype)
    for b in range(per):
        out = out.at[:, b, :, b, :].set(w4[:, b])
    return out.reshape(NBD, BD, BD)


def _block_diag_extract(wbd):
    per = BD // LRU_BW
    w5 = wbd.reshape(NBD, per, LRU_BW, per, LRU_BW)
    return jnp.stack([w5[:, b, :, b, :] for b in range(per)], axis=1).reshape(LRU_BLOCKS, LRU_BW, LRU_BW)


_ANY = pl.BlockSpec(memory_space=pl.ANY)


def _place():
    x, y, c = lax.axis_index("x"), lax.axis_index("y"), lax.axis_index("c")
    chips = [(1 - x, y), (x, 1 - y), (1 - x, 1 - y)]
    return x, y, c, chips


def _allgather_shards(shards):
    n = len(shards)

    def body(*refs):
        ins, outs = refs[:n], refs[n:2 * n]
        send_sems, recv_sems = refs[2 * n:]
        x, y, c, chips = _place()
        me = 2 * x + y
        sibling = (x, y, 1 - c)

        def remote(p, k, src, dst, to):
            return pltpu.make_async_remote_copy(src_ref=src, dst_ref=dst, send_sem=send_sems.at[p, k],
                                                recv_sem=recv_sems.at[p, k], device_id=to, device_id_type=MESH)

        sent = []
        for p in range(n):
            for k, chip in enumerate(chips):
                cp = remote(p, k, ins[p].at[c], outs[p].at[me, c], (chip[0], chip[1], c))
                cp.start()
                sent.append(cp)
        for p in range(n):
            for k, chip in enumerate(chips):
                half = outs[p].at[2 * chip[0] + chip[1], c]
                remote(p, k, half, half, sibling).wait_recv()
                fwd = remote(p, 3 + k, half, half, sibling)
                fwd.start()
                sent.append(fwd)
        for p in range(n):
            for k, chip in enumerate(chips):
                half = outs[p].at[2 * chip[0] + chip[1], 1 - c]
                remote(p, 3 + k, half, half, sibling).wait_recv()
        for cp in sent:
            cp.wait_send()

    gathered = pl.pallas_call(
        body, name="allgather_weights",
        in_specs=[_ANY] * n, out_specs=[_ANY] * n,
        out_shape=[jax.ShapeDtypeStruct((NCHIP,) + s.shape, s.dtype) for s in shards],
        scratch_shapes=[pltpu.SemaphoreType.DMA((n, 6)), pltpu.SemaphoreType.DMA((n, 6))],
    )(*shards)
    me = 2 * lax.axis_index("x") + lax.axis_index("y")
    return [lax.dynamic_update_index_in_dim(g, s, me, 0) for g, s in zip(gathered, shards)]


_LATE =["w_branch_a", "w_branch_b", "w_out", "w_up", "w_down"]
_COLUMN_CUT = ("w_in", "w_up")
N_PEERS = 7


def _shard_major(name, g):
    s = _columns_to_shards(g) if name in _COLUMN_CUT else g.reshape(NCHIP, g.shape[0] // NCHIP, g.shape[1])
    return s.reshape(NCHIP, 2, s.shape[1] // 2, s.shape[2])


class _Exchanges:
    def __init__(self, shards):
        self.shards = shards

    def weights_side(self):
        srcs = [self.shards[n] for n in _LATE]

        def copies(sin, sout, send, recv):
            x, y, c, chips = _place()
            return [pltpu.make_async_remote_copy(
                src_ref=sin[p], dst_ref=sout[p].at[2 * x + y], send_sem=send.at[3 * p + k], recv_sem=recv.at[3 * p + k],
                device_id=(chip[0], chip[1], c), device_id_type=MESH)
                for p in range(len(sin)) for k, chip in enumerate(chips)]

        return _Side(srcs, [jax.ShapeDtypeStruct((NCHIP,) + s.shape, s.dtype) for s in srcs], 3 * len(srcs), copies)

    def weights_landed(self, landed):
        me = 2 * lax.axis_index("x") + lax.axis_index("y")
        full = {n: lax.dynamic_update_index_in_dim(g, self.shards[n], me, 0) for n, g in zip(_LATE, landed)}
        return dict(wa=full["w_branch_a"].reshape(D, D), wb=full["w_branch_b"].reshape(D, D),
                    wout=full["w_out"].reshape(D, D), wup=_shards_to_columns(full["w_up"]),
                    wdown=full["w_down"].reshape(FF, D))

    def grads_side(self, grads):
        side_names = list(grads)
        srcs = [_shard_major(n, grads[n]) for n in side_names]

        def copies(sin, sout, send, recv):
            x, y, c, chips = _place()
            peers = [(x, y, 1 - c)] + [(cx, cy, c) for cx, cy in chips] + [(cx, cy, 1 - c) for cx, cy in chips]
            return [pltpu.make_async_remote_copy(
                src_ref=sin[p].at[2 * px + py, pc], dst_ref=sout[p].at[s], send_sem=send.at[N_PEERS * p + s],
                recv_sem=recv.at[N_PEERS * p + s], device_id=(px, py, pc), device_id_type=MESH)
                for p in range(len(sin)) for s, (px, py, pc) in enumerate(peers)]

        side = _Side(srcs, [jax.ShapeDtypeStruct((N_PEERS,) + s.shape[2:], s.dtype) for s in srcs],
                     N_PEERS * len(srcs), copies)
        side.names = side_names
        return side

    def grads_landed(self, side, landed):
        return {n: (own, got) for n, own, got in zip(side.names, side.srcs, landed)}


def _add8(g, got, me, c, name):
    _, _, half, cols = g.shape
    th = _row_tile(half, 2 * cols)

    def body(me_ref, c_ref, g_ref, r_ref, o_ref):
        acc = g_ref[...].astype(F32)
        for s in range(N_PEERS):
            acc = acc + r_ref[s].astype(F32)
        o_ref[...] = acc

    return pl.pallas_call(
        body, name=name,
        grid_spec=pltpu.PrefetchScalarGridSpec(
            num_scalar_prefetch=2, grid=(half // th,),
            in_specs=[pl.BlockSpec((None, None, th, cols), lambda i, me_ref, c_ref: (me_ref[0], c_ref[0], i, 0)),
                      pl.BlockSpec((N_PEERS, th, cols), lambda i, me_ref, c_ref: (0, i, 0))],
            out_specs=pl.BlockSpec((th, cols), lambda i, me_ref, c_ref: (i, 0))),
        out_shape=jax.ShapeDtypeStruct((half, cols), F32),
    )(me, c, g, got)


def _share_halves(halves):
    n = len(halves)

    def body(*refs):
        ins, outs = refs[:n], refs[n:2 * n]
        send_sems, recv_sems = refs[2 * n:]
        x, y, c, _ = _place()
        sibling = (x, y, 1 - c)
        copies = []
        for p in range(n):
            cp = pltpu.make_async_remote_copy(src_ref=ins[p], dst_ref=outs[p], send_sem=send_sems.at[p],
                                              recv_sem=recv_sems.at[p], device_id=sibling, device_id_type=MESH)
            cp.start()
            copies.append(cp)
        for cp in copies:
            cp.wait()

    return pl.pallas_call(
        body, name="reduce_share_halves",
        in_specs=[_ANY] * n, out_specs=[_ANY] * n,
        out_shape=[jax.ShapeDtypeStruct(h.shape, h.dtype) for h in halves],
        scratch_shapes=[pltpu.SemaphoreType.DMA((n,)), pltpu.SemaphoreType.DMA((n,))],
    )(*halves)


def _row_tile(half, cols):
    th = max(SLAB, min(half, (1 << 18) // cols // SLAB * SLAB))
    while half % th:
        th -= SLAB
    return th


N_DEV = 8
SMALL_ROWS = 208


def _allreduce_small(pack):
    def body(x_ref, out_ref, gbuf, send_sems, recv_sems, local_sem):
        x, y, c, chips = _place()
        me, sibling = (x, y, c), (x, y, 1 - c)

        def rows(px, py, pc):
            return gbuf.at[4 * px + 2 * py + pc]

        def copy(k, block, to, src=None):
            return pltpu.make_async_remote_copy(
                src_ref=rows(*block) if src is None else src, dst_ref=rows(*block),
                send_sem=send_sems.at[k], recv_sem=recv_sems.at[k], device_id=to, device_id_type=MESH)

        mine = pltpu.make_async_copy(x_ref, rows(*me), local_sem)
        mine.start()
        first = [copy(0, me, sibling, src=x_ref)]
        first += [copy(1 + j, me, (chip[0], chip[1], c), src=x_ref) for j, chip in enumerate(chips)]
        for cp in first:
            cp.start()
        passed = [copy(4 + j, (chip[0], chip[1], c), sibling) for j, chip in enumerate(chips)]
        for j, chip in enumerate(chips):
            copy(1 + j, (chip[0], chip[1], c), me).wait_recv()
            passed[j].start()
        copy(0, sibling, me).wait_recv()
        for j, chip in enumerate(chips):
            copy(4 + j, (chip[0], chip[1], 1 - c), me).wait_recv()
        for cp in first + passed:
            cp.wait_send()
        mine.wait()
        acc = gbuf[0]
        for d in range(1, N_DEV):
            acc = acc + gbuf[d]
        out_ref[...] = acc

    return pl.pallas_call(
        body, name="allreduce_small",
        in_specs=[pl.BlockSpec(memory_space=pltpu.VMEM)],
        out_specs=pl.BlockSpec(memory_space=pltpu.VMEM),
        out_shape=jax.ShapeDtypeStruct((SMALL_ROWS, D), F32),
        scratch_shapes=[pltpu.VMEM((N_DEV, SMALL_ROWS, D), F32), pltpu.SemaphoreType.DMA((7,)),
                        pltpu.SemaphoreType.DMA((7,)), pltpu.SemaphoreType.DMA],
    )(pack)


def _adamw(w, g, m, v, name):
    rows, cols = w.shape
    th = _row_tile(rows, cols)

    def body(w_ref, g_ref, m_ref, v_ref, d_ref, mo_ref, vo_ref):
        gv = g_ref[...]
        mn = ADAM_B1 * m_ref[...] + (1.0 - ADAM_B1) * gv
        vn = ADAM_B2 * v_ref[...] + (1.0 - ADAM_B2) * (gv * gv)
        m_hat = mn / (1.0 - ADAM_B1 ** ADAM_STEP)
        v_hat = vn / (1.0 - ADAM_B2 ** ADAM_STEP)
        d_ref[...] = -ADAM_LR * (m_hat / (jnp.sqrt(v_hat) + ADAM_EPS) + ADAM_WD * w_ref[...])
        mo_ref[...] = mn
        vo_ref[...] = vn

    spec = pl.BlockSpec((th, cols), lambda i: (i, 0))
    return pl.pallas_call(
        body, name=name, grid=(rows // th,),
        in_specs=[spec] * 4, out_specs=[spec] * 3,
        out_shape=[jax.ShapeDtypeStruct((rows, cols), F32)] * 3,
        compiler_params=pltpu.CompilerParams(dimension_semantics=("parallel",)),
    )(w, g, m, v)


_SMALL = ["norm_mix_g", "norm_mlp_g", "norm_final_g", "conv_b", "lru_ba", "lru_bx", "lru_lambda"]
_ROW_FB, _ROW_CW, _ROW_WA, _ROW_WX, _ROW_LOSS = 56, 64, 72, 136, 200


def _pack_small(vals, col0):
    def slab(a):
        return jnp.pad(a, ((0, -a.shape[0] % SLAB), (0, D - a.shape[1])))

    rows = [slab(vals[n].reshape(1, D)) for n in _SMALL]
    rows.append(slab(vals["forget_b"].reshape(1, NH)))
    if vals["conv_w"].shape[1] == D:
        rows.append(slab(vals["conv_w"]))
    else:
        rows.append(slab(lax.dynamic_update_slice(jnp.zeros((CONV, D), F32), vals["conv_w"], (0, col0))))
    rows.append(vals["lru_wa"].reshape(LRU_BLOCKS * LRU_BW * LRU_BW // D, D))
    rows.append(vals["lru_wx"].reshape(LRU_BLOCKS * LRU_BW * LRU_BW // D, D))
    rows.append(slab(vals["loss"]) if "loss" in vals else jnp.zeros((SLAB, D), F32))
    return jnp.concatenate(rows, axis=0)


def _unpack_small(pack, col0):
    out = {n: pack[SLAB * i] for i, n in enumerate(_SMALL)}
    out["forget_b"] = pack[_ROW_FB, :NH]
    out["conv_w"] = lax.dynamic_slice(pack[_ROW_CW:_ROW_CW + CONV], (0, col0), (CONV, D // NCHIP))
    out["lru_wa"] = pack[_ROW_WA:_ROW_WX].reshape(LRU_BLOCKS, LRU_BW, LRU_BW)
    out["lru_wx"] = pack[_ROW_WX:_ROW_LOSS].reshape(LRU_BLOCKS, LRU_BW, LRU_BW)
    return out


_WEIGHTS = ["norm_mix_g", "w_in", "conv_w", "conv_b", "lru_wa", "lru_ba", "lru_wx", "lru_bx", "lru_lambda",
            "forget_b", "w_branch_a", "w_branch_b", "w_out", "norm_mlp_g", "w_up", "w_down", "norm_final_g"]
_BIG = ["w_in", "w_branch_a", "w_branch_b", "w_out", "w_up", "w_down"]


def _halves(a):
    return a.reshape(2, a.shape[0] // 2, a.shape[1])


def _columns_to_shards(a):
    rows, cols = a.shape[0], a.shape[1] // NCHIP
    return jnp.transpose(a.reshape(rows, NCHIP, cols), (1, 0, 2))


def _shards_to_columns(a):
    n, rows, cols = a.shape
    return jnp.transpose(a, (1, 0, 2)).reshape(rows, n * cols)


def kernel(x, norm_mix_g, w_in, conv_w, conv_b, lru_wa, lru_ba, lru_wx, lru_bx, lru_lambda, forget_b, w_branch_a, w_branch_b, w_out, norm_mlp_g, w_up, w_down, norm_final_g, loss_target, m_norm_mix_g, m_w_in, m_conv_w, m_conv_b, m_lru_wa, m_lru_ba, m_lru_wx, m_lru_bx, m_lru_lambda, m_forget_b, m_w_branch_a, m_w_branch_b, m_w_out, m_norm_mlp_g, m_w_up, m_w_down, m_norm_final_g, v_norm_mix_g, v_w_in, v_conv_w, v_conv_b, v_lru_wa, v_lru_ba, v_lru_wx, v_lru_bx, v_lru_lambda, v_forget_b, v_w_branch_a, v_w_branch_b, v_w_out, v_norm_mlp_g, v_w_up, v_w_down, v_norm_final_g):
    args = dict(locals())
    wts = {n: args[n] for n in _WEIGHTS}
    mom = {n: args["m_" + n] for n in _WEIGHTS}
    var = {n: args["v_" + n] for n in _WEIGHTS}
    T = x.shape[1]
    xi, yi, ci = lax.axis_index("x"), lax.axis_index("y"), lax.axis_index("c")
    me = 2 * xi + yi
    c1 = jnp.reshape(ci, (1,)).astype(jnp.int32)
    me1 = jnp.reshape(me, (1,)).astype(jnp.int32)
    col0 = me * (D // NCHIP)

    cw_pad = jnp.pad(conv_w, ((0, 4 * SLAB - CONV), (0, 0)))
    g_in, g_cw = _allgather_shards([_halves(w_in.astype(BF)), _halves(cw_pad)])
    cin = DIN // NCHIP
    win = _shards_to_columns(g_in.reshape(NCHIP, D, cin))
    w = dict(
        win=jnp.pad(win, ((0, 0), (0, DINP - DIN))),
        cw=_shards_to_columns(g_cw.reshape(NCHIP, 4 * SLAB, D // NCHIP)[:, :CONV]),
        vec=jnp.concatenate([conv_b[None], lru_ba[None], lru_bx[None], lru_lambda[None],
                             jnp.zeros((SLAB - 4, D), F32)], axis=0),
        fb=jnp.pad(forget_b[None], ((0, 0), (0, DH - NH))),
        wabd=_block_diag(lru_wa).astype(BF), wxbd=_block_diag(lru_wx).astype(BF),
        g_mix=norm_mix_g[None], g_mlp=norm_mlp_g[None], g_fin=norm_final_g[None])

    r = _local_step(x[0], loss_target[0], w, T, dist=_Exchanges({n: wts[n].astype(BF) for n in _LATE}))

    halves = [_add8(*r["big"][n], me1, c1, "add8_" + n) for n in _BIG]
    theirs = _share_halves(halves)
    low = ci == 0
    gsum = {n: jnp.concatenate([jnp.where(low, h, t), jnp.where(low, t, h)], axis=0)
            for n, h, t in zip(_BIG, halves, theirs)}
    lacc = r["lacc"]
    small = dict(norm_mix_g=r["dg_mix"], norm_mlp_g=r["dg_mlp"], norm_final_g=r["dg_fin"], conv_b=lacc[3],
                 lru_ba=lacc[0], lru_bx=lacc[1], lru_lambda=lacc[2], forget_b=r["dfb"][0, :NH],
                 conv_w=lacc[4:4 + CONV], lru_wa=_block_diag_extract(r["dwabd"]),
                 lru_wx=_block_diag_extract(r["dwxbd"]), loss=r["loss_vec"])
    gpack = _allreduce_small(_pack_small(small, col0))
    loss = jnp.sum(gpack[_ROW_LOSS])

    grads, delta, new_m, new_v = {}, {}, {}, {}
    for n in _BIG:
        grads[n] = gsum[n]
        delta[n], new_m[n], new_v[n] = _adamw(wts[n], gsum[n], mom[n], var[n], "adamw_" + n)
    dp, mp, vp = _adamw(_pack_small(wts, col0), gpack, _pack_small(mom, col0), _pack_small(var, col0), "adamw_small")
    for dst, pack in ((grads, gpack), (delta, dp), (new_m, mp), (new_v, vp)):
        dst.update(_unpack_small(pack, col0))
    return (loss, r["dx"][None], *[grads[n] for n in _WEIGHTS], *[delta[n] for n in _WEIGHTS],
            *[new_m[n] for n in _WEIGHTS], *[new_v[n] for n in _WEIGHTS])
```

```python
import functools
import math

import jax
import jax.numpy as jnp
import numpy as np
from jax import lax
from jax.experimental import pallas as pl
from jax.experimental.pallas import tpu as pltpu

F32 = jnp.float32
BF = jnp.bfloat16

D = 1024
NH = 8
DH = 128
FF = 4096
CONV = 4
LRU_BLOCKS = 16
LRU_BW = 64
BD = 256
NBD = D // BD
LRU_C = 8.0
EPS = 1e-6
DIN = 7176
DINP = 7296
NCHIP = 4
SLAB = 8
VMEM_CAP = 60 * 1024 * 1024

ADAM_LR = 0.001
ADAM_B1 = 0.9
ADAM_B2 = 0.999
ADAM_EPS = 1e-08
ADAM_WD = 0.01
ADAM_STEP = 10

MESH = pl.DeviceIdType.MESH


def _vmem_limit(nbytes):
    return int(min(VMEM_CAP, max(32 * 1024 * 1024, 3 * nbytes)))


def _nbytes(shape, dtype):
    return int(np.prod(shape)) * jnp.dtype(dtype).itemsize


def _sig(x):
    return 0.5 * jnp.tanh(0.5 * x) + 0.5


def _log1p(u):
    w = 1.0 + u
    return jnp.where(w == 1.0, u, jnp.log(w) * (u / (w - 1.0)))


def _one_minus_sq(a, la):
    z = 2.0 * la
    series = -z * (1.0 + z * (0.5 + z * ((1.0 / 6.0) + z * ((1.0 / 24.0) + z * (1.0 / 120.0)))))
    return jnp.where(z > -0.125, series, 1.0 - a * a)


def _softplus(z):
    return jnp.maximum(z, 0.0) + _log1p(jnp.exp(-jnp.abs(z)))


_GELU_C = math.sqrt(2.0 / math.pi)


def _gelu(x):
    return 0.5 * x * (1.0 + jnp.tanh(_GELU_C * (x + 0.044715 * x * x * x)))


def _gelu_grad(x):
    t = jnp.tanh(_GELU_C * (x + 0.044715 * x * x * x))
    return 0.5 * (1.0 + t) + 0.5 * x * (1.0 - t * t) * _GELU_C * (1.0 + 3.0 * 0.044715 * x * x)


def _shift_down(x, d, prev8):
    n = x.shape[0]
    row8 = lax.broadcasted_iota(jnp.int32, (SLAB, x.shape[1]), 0)
    y = pltpu.roll(x, d, 0)
    top = jnp.where(row8 < d, pltpu.roll(prev8, d, 0), y[0:SLAB])
    if n == SLAB:
        return top
    return jnp.concatenate([top, y[SLAB:]], axis=0)


def _shift_up(x, d, next8):
    n = x.shape[0]
    row8 = lax.broadcasted_iota(jnp.int32, (SLAB, x.shape[1]), 0)
    y = pltpu.roll(x, n - d, 0)
    bottom = jnp.where(row8 >= SLAB - d, pltpu.roll(next8, SLAB - d, 0), y[n - SLAB:])
    if n == SLAB:
        return bottom
    return jnp.concatenate([y[:n - SLAB], bottom], axis=0)


def _slab_scan_fwd(a, b):
    row = lax.broadcasted_iota(jnp.int32, a.shape, 0)
    for k in (1, 2, 4):
        a_s = pltpu.roll(a, k, 0)
        b_s = pltpu.roll(b, k, 0)
        m = row >= k
        b = jnp.where(m, a * b_s + b, b)
        a = jnp.where(m, a * a_s, a)
    return a, b


def _slab_scan_bwd(a, b):
    row = lax.broadcasted_iota(jnp.int32, a.shape, 0)
    for k in (1, 2, 4):
        a_s = pltpu.roll(a, SLAB - k, 0)
        b_s = pltpu.roll(b, SLAB - k, 0)
        m = row < SLAB - k
        b = jnp.where(m, a * b_s + b, b)
        a = jnp.where(m, a * a_s, a)
    return a, b


_DIMS = {"nn": (((1,), (0,)), ((), ())), "nt": (((1,), (1,)), ((), ())), "tn": (((0,), (0,)), ((), ()))}


def _dot(a, b, mode="nn"):
    return lax.dot_general(a, b, _DIMS[mode], preferred_element_type=F32)


def _mm(a, b, mode, M, N, K, *, name, out_dtype=F32, tm=1024, tn=1024, tk=1024,
        a_off=(0, 0), b_off=(0, 0), add=None, epi=None, epi_ins=()):
    tm, tn, tk = min(tm, M), min(tn, N), min(tk, K)
    nk = K // tk
    grid = (M // tm, N // tn, nk)
    if mode == "nn":
        a_spec = pl.BlockSpec((tm, tk), lambda i, j, k: (i + a_off[0], k + a_off[1]))
        b_spec = pl.BlockSpec((tk, tn), lambda i, j, k: (k + b_off[0], j + b_off[1]))
    elif mode == "nt":
        a_spec = pl.BlockSpec((tm, tk), lambda i, j, k: (i + a_off[0], k + a_off[1]))
        b_spec = pl.BlockSpec((tn, tk), lambda i, j, k: (j + b_off[0], k + b_off[1]))
    else:
        a_spec = pl.BlockSpec((tk, tm), lambda i, j, k: (k + a_off[0], i + a_off[1]))
        b_spec = pl.BlockSpec((tk, tn), lambda i, j, k: (k + b_off[0], j + b_off[1]))
    o_spec = pl.BlockSpec((tm, tn), lambda i, j, k: (i, j))
    extra = ([add] if add is not None else []) + list(epi_ins)
    n_extra = len(extra)
    has_add = add is not None

    def body(*refs):
        a_ref, b_ref = refs[0], refs[1]
        ex = refs[2:2 + n_extra]
        o_ref = refs[2 + n_extra]

        def finish(acc):
            if has_add:
                acc = acc + ex[0][...].astype(F32)
            if epi is not None:
                acc = epi(acc, *[e[...] for e in ex[(1 if has_add else 0):]])
            o_ref[...] = acc.astype(o_ref.dtype)

        p = _dot(a_ref[...].astype(BF), b_ref[...].astype(BF), mode)
        if nk == 1:
            finish(p)
        else:
            acc_ref = refs[3 + n_extra]
            k = pl.program_id(2)

            @pl.when(k == 0)
            def _():
                acc_ref[...] = p

            @pl.when(k > 0)
            def _():
                acc_ref[...] += p

            @pl.when(k == nk - 1)
            def _():
                finish(acc_ref[...])

    blk = (_nbytes((tm, tk), a.dtype) + _nbytes((tk, tn), b.dtype) + _nbytes((tm, tn), out_dtype)
           + sum(_nbytes((tm, tn), e.dtype) for e in extra) + 2 * _nbytes((tm, tn), F32))
    return pl.pallas_call(
        body, name=name, grid=grid,
        in_specs=[a_spec, b_spec] + [o_spec] * n_extra,
        out_specs=o_spec,
        out_shape=jax.ShapeDtypeStruct((M, N), out_dtype),
        scratch_shapes=[pltpu.VMEM((tm, tn), F32)] if nk > 1 else [],
        compiler_params=pltpu.CompilerParams(
            dimension_semantics=("parallel", "parallel", "arbitrary"), vmem_limit_bytes=_vmem_limit(blk)),
    )(a, b, *extra)


def _ew(fn, T, tm, ins, consts, outs, accs, *, name, reverse=False):
    tm = min(tm, T)
    nt = T // tm
    n_in, n_c, n_o, n_a = len(ins), len(consts), len(outs), len(accs)

    def row(i):
        return nt - 1 - i if reverse else i

    in_specs = [pl.BlockSpec((tm, w), functools.partial(lambda i, cb: (row(i), cb), cb=cb)) for (_, w, cb) in ins]
    in_specs += [pl.BlockSpec(c.shape, functools.partial(lambda i, nd: (0,) * nd, nd=c.ndim)) for c in consts]
    out_specs = [pl.BlockSpec((tm, w), lambda i: (row(i), 0)) for (w, _) in outs]
    out_specs += [pl.BlockSpec((r, w), lambda i: (0, 0)) for (r, w) in accs]
    out_shape = [jax.ShapeDtypeStruct((T, w), dt) for (w, dt) in outs]
    out_shape += [jax.ShapeDtypeStruct((r, w), F32) for (r, w) in accs]

    def body(*refs):
        in_refs = refs[:n_in]
        c_refs = refs[n_in:n_in + n_c]
        o_refs = refs[n_in + n_c:n_in + n_c + n_o]
        a_refs = refs[n_in + n_c + n_o:]
        ov, av = fn([r[...] for r in in_refs], [r[...] for r in c_refs])
        for r, v in zip(o_refs, ov):
            r[...] = v.astype(r.dtype)
        if n_a:
            i = pl.program_id(0)

            @pl.when(i == 0)
            def _():
                for r, v in zip(a_refs, av):
                    r[...] = v

            @pl.when(i > 0)
            def _():
                for r, v in zip(a_refs, av):
                    r[...] += v

    blk = (sum(_nbytes((tm, w), a.dtype) for (a, w, _) in ins) + sum(_nbytes(c.shape, c.dtype) for c in consts)
           + sum(_nbytes((tm, w), dt) for (w, dt) in outs) + sum(_nbytes(s, F32) for s in accs))
    res = pl.pallas_call(
        body, name=name, grid=(nt,), in_specs=in_specs, out_specs=out_specs, out_shape=out_shape,
        compiler_params=pltpu.CompilerParams(
            dimension_semantics=("arbitrary",), vmem_limit_bytes=_vmem_limit(blk)),
    )(*[a for (a, _, _) in ins], *consts)
    return res


def _colsum(v):
    return jnp.sum(v, axis=0, keepdims=True)


def _fgate_fwd(fl, fb, T, tm=512):
    tm = min(tm, T)

    def body(fl_ref, fb_ref, f_ref, carry_ref):
        i = pl.program_id(0)

        @pl.when(i == 0)
        def _():
            carry_ref[...] = jnp.zeros_like(carry_ref)

        row = lax.broadcasted_iota(jnp.int32, (SLAB, DH), 0)

        def slab(s, carry):
            r0 = pl.multiple_of(s * SLAB, SLAB)
            z = fl_ref[pl.ds(r0, SLAB), :] + fb_ref[...]
            c = jnp.minimum(z, 0.0) - _log1p(jnp.exp(-jnp.abs(z)))
            for k in (1, 2, 4):
                c = c + jnp.where(row >= k, pltpu.roll(c, k, 0), 0.0)
            c = c + carry
            f_ref[pl.ds(r0, SLAB), :] = c
            return c[SLAB - 1:SLAB, :]

        carry_ref[0:1, :] = lax.fori_loop(0, tm // SLAB, slab, carry_ref[0:1, :])

    return pl.pallas_call(
        body, name="fgate_fwd", grid=(T // tm,),
        in_specs=[pl.BlockSpec((tm, DH), lambda i: (i, 0)), pl.BlockSpec((1, DH), lambda i: (0, 0))],
        out_specs=pl.BlockSpec((tm, DH), lambda i: (i, 0)),
        out_shape=jax.ShapeDtypeStruct((T, DH), F32),
        scratch_shapes=[pltpu.VMEM((SLAB, DH), F32)],
        compiler_params=pltpu.CompilerParams(dimension_semantics=("arbitrary",)),
    )(fl, fb)


def _fgate_bwd(dF, fl, fb, T, tm=512):
    tm = min(tm, T)
    nt = T // tm

    def body(df_ref, fl_ref, fb_ref, o_ref, acc_ref, carry_ref):
        i = pl.program_id(0)

        @pl.when(i == 0)
        def _():
            carry_ref[...] = jnp.zeros_like(carry_ref)
            acc_ref[...] = jnp.zeros_like(acc_ref)

        row = lax.broadcasted_iota(jnp.int32, (SLAB, DH), 0)

        def slab(n, carry):
            g_next, acc = carry
            r0 = pl.multiple_of((tm // SLAB - 1 - n) * SLAB, SLAB)
            c = df_ref[pl.ds(r0, SLAB), :]
            for k in (1, 2, 4):
                c = c + jnp.where(row < SLAB - k, pltpu.roll(c, SLAB - k, 0), 0.0)
            c = c + g_next
            z = fl_ref[pl.ds(r0, SLAB), :] + fb_ref[...]
            dfl = c * _sig(-z)
            o_ref[pl.ds(r0, SLAB), :] = dfl.astype(o_ref.dtype)
            return c[0:1, :], acc + _colsum(dfl)

        g, acc = lax.fori_loop(0, tm // SLAB, slab, (carry_ref[0:1, :], jnp.zeros((1, DH), F32)))
        carry_ref[0:1, :] = g
        acc_ref[...] += acc

    return pl.pallas_call(
        body, name="fgate_bwd", grid=(nt,),
        in_specs=[pl.BlockSpec((tm, DH), lambda i: (nt - 1 - i, 0)), pl.BlockSpec((tm, DH), lambda i: (nt - 1 - i, 0)),
                  pl.BlockSpec((1, DH), lambda i: (0, 0))],
        out_specs=[pl.BlockSpec((tm, DH), lambda i: (nt - 1 - i, 0)), pl.BlockSpec((1, DH), lambda i: (0, 0))],
        out_shape=[jax.ShapeDtypeStruct((T, DH), BF), jax.ShapeDtypeStruct((1, DH), F32)],
        scratch_shapes=[pltpu.VMEM((SLAB, DH), F32)],
        compiler_params=pltpu.CompilerParams(dimension_semantics=("arbitrary",)),
    )(dF, fl, fb)


def _conv(x, prev8, cw, cb):
    xs = [x] + [_shift_down(x, d, prev8) for d in (1, 2, 3)]
    xa = cb + cw[3:4, :] * xs[0] + cw[2:3, :] * xs[1] + cw[1:2, :] * xs[2] + cw[0:1, :] * xs[3]
    return xa, xs


def _lru_gates(xa_g, wa_g, wx_g, ba_g, bx_g, sp_g):
    xb = xa_g.astype(BF)
    r = _sig(_dot(xb, wa_g) + ba_g)
    ig = _sig(_dot(xb, wx_g) + bx_g)
    la = -LRU_C * r * sp_g
    a = jnp.exp(la)
    mult = jnp.sqrt(_one_minus_sq(a, la))
    return r, ig, a, mult


def _lru_fwd(xg, cw, vec, wabd, wxbd, T, tm=256):
    tm = min(tm, T)
    nsl = tm // SLAB

    def body(x_ref, xp_ref, cw_ref, vec_ref, wa_ref, wx_ref, h_ref, a_s, b_s, carry_ref):
        i = pl.program_id(0)

        @pl.when(i == 0)
        def _():
            carry_ref[...] = jnp.zeros_like(carry_ref)

        x = x_ref[...].astype(F32)
        prev8 = jnp.where(i > 0, xp_ref[SLAB:, :].astype(F32), 0.0)
        vec_v = vec_ref[...]
        xa, _ = _conv(x, prev8, cw_ref[...], vec_v[0:1, :])
        sp = _softplus(-vec_v[3:4, :])
        for g in range(NBD):
            sl = slice(g * BD, (g + 1) * BD)
            _, ig, a, mult = _lru_gates(xa[:, sl], wa_ref[g], wx_ref[g], vec_v[1:2, sl], vec_v[2:3, sl], sp[:, sl])
            a_s[:, sl] = a
            b_s[:, sl] = mult * ig * xa[:, sl]

        def slab(s, carry):
            r0 = pl.multiple_of(s * SLAB, SLAB)
            A, B = _slab_scan_fwd(a_s[pl.ds(r0, SLAB), :], b_s[pl.ds(r0, SLAB), :])
            h = A * carry + B
            h_ref[pl.ds(r0, SLAB), :] = h
            return h[SLAB - 1:SLAB, :]

        carry_ref[0:1, :] = lax.fori_loop(0, nsl, slab, carry_ref[0:1, :])

    blk = 5 * _nbytes((tm, D), F32) + 2 * _nbytes((NBD, BD, BD), BF)
    return pl.pallas_call(
        body, name="lru_fwd", grid=(T // tm,),
        in_specs=[pl.BlockSpec((tm, D), lambda i: (i, 0)),
                  pl.BlockSpec((2 * SLAB, D), lambda i: (jnp.maximum(i * (nsl // 2) - 1, 0), 0)),
                  pl.BlockSpec((CONV, D), lambda i: (0, 0)),
                  pl.BlockSpec((SLAB, D), lambda i: (0, 0)),
                  pl.BlockSpec((NBD, BD, BD), lambda i: (0, 0, 0)),
                  pl.BlockSpec((NBD, BD, BD), lambda i: (0, 0, 0))],
        out_specs=pl.BlockSpec((tm, D), lambda i: (i, 0)),
        out_shape=jax.ShapeDtypeStruct((T, D), F32),
        scratch_shapes=[pltpu.VMEM((tm, D), F32), pltpu.VMEM((tm, D), F32), pltpu.VMEM((SLAB, D), F32)],
        compiler_params=pltpu.CompilerParams(dimension_semantics=("arbitrary",), vmem_limit_bytes=_vmem_limit(blk)),
    )(xg, xg, cw, vec, wabd, wxbd)


def _lru_bwd(xg, h, dha, cw, vec, wabd, wxbd, T, tm=256):
    tm = min(tm, T)
    nsl = tm // SLAB
    nt = T // tm

    def body(x_ref, xp_ref, h_ref, hp_ref, dh_ref, cw_ref, vec_ref, wa_ref, wx_ref,
             dx_ref, dwa_ref, dwx_ref, acc_ref, a_s, b_s, g_s, dxa_s, carry_ref, dxan_ref):
        n = pl.program_id(0)
        it = nt - 1 - n

        @pl.when(n == 0)
        def _():
            carry_ref[...] = jnp.zeros_like(carry_ref)
            dxan_ref[...] = jnp.zeros_like(dxan_ref)
            dwa_ref[...] = jnp.zeros_like(dwa_ref)
            dwx_ref[...] = jnp.zeros_like(dwx_ref)
            acc_ref[...] = jnp.zeros_like(acc_ref)

        x = x_ref[...].astype(F32)
        prev8 = jnp.where(it > 0, xp_ref[SLAB:, :].astype(F32), 0.0)
        hprev8 = jnp.where(it > 0, hp_ref[...], 0.0)
        vec_v = vec_ref[...]
        cw_v = cw_ref[...]
        xa, xs = _conv(x, prev8, cw_v, vec_v[0:1, :])
        sp = _softplus(-vec_v[3:4, :])
        gates = []
        for g in range(NBD):
            sl = slice(g * BD, (g + 1) * BD)
            r, ig, a, mult = _lru_gates(xa[:, sl], wa_ref[g], wx_ref[g], vec_v[1:2, sl], vec_v[2:3, sl], sp[:, sl])
            gates.append((r, ig, a, mult))
            a_s[:, sl] = a
        a_next = _shift_up(a_s[...], 1, carry_ref[...])
        a_s[...] = a_next
        b_s[...] = dh_ref[...]

        def slab(m, carry):
            r0 = pl.multiple_of((nsl - 1 - m) * SLAB, SLAB)
            A, B = _slab_scan_bwd(a_s[pl.ds(r0, SLAB), :], b_s[pl.ds(r0, SLAB), :])
            gg = A * carry + B
            g_s[pl.ds(r0, SLAB), :] = gg
            return gg[0:1, :]

        g_first = lax.fori_loop(0, nsl, slab, carry_ref[1:2, :])
        gt = g_s[...]
        h_prev = _shift_down(h_ref[...], 1, hprev8)
        dba = []
        dbx = []
        dsp = []
        for g in range(NBD):
            sl = slice(g * BD, (g + 1) * BD)
            r, ig, a, mult = gates[g]
            xa_g = xa[:, sl]
            g_g = gt[:, sl]
            da = g_g * h_prev[:, sl]
            dmult = g_g * ig * xa_g
            di = g_g * mult * xa_g
            dxa_g = g_g * mult * ig
            dla = da * a - dmult * (a * a / mult)
            dr = dla * (-LRU_C) * sp[:, sl]
            dsp.append(_colsum(dla * (-LRU_C) * r))
            dra = (dr * r * (1.0 - r))
            dix = (di * ig * (1.0 - ig))
            dba.append(_colsum(dra))
            dbx.append(_colsum(dix))
            dra_b = dra.astype(BF)
            dix_b = dix.astype(BF)
            xb = xa_g.astype(BF)
            dxa_g = dxa_g + _dot(dra_b, wa_ref[g], "nt") + _dot(dix_b, wx_ref[g], "nt")
            dwa_ref[g] += _dot(xb, dra_b, "tn")
            dwx_ref[g] += _dot(xb, dix_b, "tn")
            dxa_s[:, sl] = dxa_g
        dxa = dxa_s[...]
        nxt = dxan_ref[...]
        dx = (cw_v[3:4, :] * dxa + cw_v[2:3, :] * _shift_up(dxa, 1, nxt)
              + cw_v[1:2, :] * _shift_up(dxa, 2, nxt) + cw_v[0:1, :] * _shift_up(dxa, 3, nxt))
        dx_ref[...] = dx.astype(dx_ref.dtype)
        acc_ref[0:1, :] += jnp.concatenate(dba, axis=1)
        acc_ref[1:2, :] += jnp.concatenate(dbx, axis=1)
        acc_ref[2:3, :] += jnp.concatenate(dsp, axis=1)
        acc_ref[3:4, :] += _colsum(dxa)
        for k in range(CONV):
            acc_ref[4 + k:5 + k, :] += _colsum(dxa * xs[CONV - 1 - k])
        dxan_ref[...] = dxa[0:SLAB, :]
        a_first = jnp.concatenate([gates[g][2][0:1, :] for g in range(NBD)], axis=1)
        carry_ref[0:1, :] = a_first
        carry_ref[1:2, :] = g_first

        @pl.when(n == nt - 1)
        def _():
            acc_ref[2:3, :] = acc_ref[2:3, :] * (-_sig(-vec_v[3:4, :]))

    rowblk = lambda i: (nt - 1 - i, 0)
    prevblk = lambda i: (jnp.maximum((nt - 1 - i) * nsl - 1, 0), 0)
    c2 = lambda i: (0, 0)
    c3 = lambda i: (0, 0, 0)
    blk = 12 * _nbytes((tm, D), F32) + 6 * _nbytes((NBD, BD, BD), F32)
    return pl.pallas_call(
        body, name="lru_bwd", grid=(nt,),
        in_specs=[pl.BlockSpec((tm, D), rowblk),
                  pl.BlockSpec((2 * SLAB, D), lambda i: (jnp.maximum((nt - 1 - i) * (nsl // 2) - 1, 0), 0)),
                  pl.BlockSpec((tm, D), rowblk), pl.BlockSpec((SLAB, D), prevblk),
                  pl.BlockSpec((tm, D), rowblk),
                  pl.BlockSpec((CONV, D), c2), pl.BlockSpec((SLAB, D), c2),
                  pl.BlockSpec((NBD, BD, BD), c3), pl.BlockSpec((NBD, BD, BD), c3)],
        out_specs=[pl.BlockSpec((tm, D), rowblk), pl.BlockSpec((NBD, BD, BD), c3), pl.BlockSpec((NBD, BD, BD), c3),
                   pl.BlockSpec((16, D), c2)],
        out_shape=[jax.ShapeDtypeStruct((T, D), BF), jax.ShapeDtypeStruct((NBD, BD, BD), F32),
                   jax.ShapeDtypeStruct((NBD, BD, BD), F32), jax.ShapeDtypeStruct((16, D), F32)],
        scratch_shapes=[pltpu.VMEM((tm, D), F32), pltpu.VMEM((tm, D), F32), pltpu.VMEM((tm, D), F32),
                        pltpu.VMEM((tm, D), F32), pltpu.VMEM((SLAB, D), F32), pltpu.VMEM((SLAB, D), F32)],
        compiler_params=pltpu.CompilerParams(dimension_semantics=("arbitrary",), vmem_limit_bytes=_vmem_limit(blk)),
    )(xg, xg, h, h, dha, cw, vec, wabd, wxbd)


_SCALE = 1.0 / math.sqrt(DH)


_ANY = pl.BlockSpec(memory_space=pl.ANY)


class _Side:
    def __init__(self, srcs, outs, nsem, copies):
        self.srcs, self.outs, self.nsem, self.copies = list(srcs), list(outs), nsem, copies


def _pallas(body, operands, *, name, grid, in_specs, out_specs, out_shape, scratch_shapes=(), semantics,
            vmem=None, side=None):
    if side is None:
        return pl.pallas_call(
            body, name=name, grid=grid, in_specs=in_specs, out_specs=out_specs, out_shape=out_shape,
            scratch_shapes=list(scratch_shapes),
            compiler_params=pltpu.CompilerParams(dimension_semantics=semantics, vmem_limit_bytes=vmem),
        )(*operands)
    n_in, n_out, n_scr = len(in_specs), len(out_specs), len(scratch_shapes)
    ns, no = len(side.srcs), len(side.outs)

    def hosted(*refs):
        ins, refs = refs[:n_in], refs[n_in:]
        sin, refs = refs[:ns], refs[ns:]
        outs, refs = refs[:n_out], refs[n_out:]
        sout, refs = refs[:no], refs[no:]
        scr, (send, recv) = refs[:n_scr], refs[n_scr:]
        ids = [pl.program_id(a) for a in range(len(grid))]
        first = functools.reduce(jnp.logical_and, [i == 0 for i in ids])
        last = functools.reduce(jnp.logical_and, [i == g - 1 for i, g in zip(ids, grid)])

        @pl.when(first)
        def _():
            for cp in side.copies(sin, sout, send, recv):
                cp.start()

        body(*ins, *outs, *scr)

        @pl.when(last)
        def _():
            for cp in side.copies(sin, sout, send, recv):
                cp.wait()

    return pl.pallas_call(
        hosted, name=name, grid=grid, in_specs=list(in_specs) + [_ANY] * ns, out_specs=list(out_specs) + [_ANY] * no,
        out_shape=list(out_shape) + side.outs,
        scratch_shapes=list(scratch_shapes) + [pltpu.SemaphoreType.DMA((side.nsem,)), pltpu.SemaphoreType.DMA((side.nsem,))],
        compiler_params=pltpu.CompilerParams(dimension_semantics=("arbitrary",) * len(grid), vmem_limit_bytes=vmem),
    )(*operands, *side.srcs)


DA = 2 * DH
_LOG2E = math.log2(math.e)
_C2 = _SCALE * _LOG2E


def _aug_fn(ins, cs):
    q, k, fcum = ins
    g_all = fcum * _LOG2E
    lane = lax.broadcasted_iota(jnp.int32, (q.shape[0], DH), 1)
    qa, ka = [], []
    for hd in range(NH):
        g = g_all[:, hd:hd + 1]
        hi = g.astype(BF).astype(F32)
        mid = (g - hi).astype(BF).astype(F32)
        lo = ((g - hi) - mid).astype(BF).astype(F32)
        qx = jnp.where(lane == 0, hi, jnp.where(lane == 1, mid, jnp.where(lane == 2, lo,
                                                                          jnp.where(lane < 6, 1.0, 0.0))))
        kx = jnp.where(lane < 3, 1.0, jnp.where(lane == 3, -hi, jnp.where(lane == 4, -mid,
                                                                          jnp.where(lane == 5, -lo, 0.0))))
        qa += [q[:, hd * DH:(hd + 1) * DH], qx.astype(BF)]
        ka += [k[:, hd * DH:(hd + 1) * DH], kx.astype(BF)]
    return [jnp.concatenate(qa, axis=1), jnp.concatenate(ka, axis=1)], []


_KA_ONES = DH + 3
KT_ONES = 16
ATTN_CHAINS = 1


def _attn_fwd(qa, ka, qkv, T, blk=512, side=None):
    blk = min(blk, T)
    nb = T // blk
    bqs = blk // ATTN_CHAINS

    def body(q_ref, k_ref, vn_ref, o_ref, lse_ref, v_ref):
        i = pl.program_id(1)

        @pl.when(i == 0)
        def _():
            for jj in range(nb):
                v_ref[jj] = vn_ref[jj * blk:(jj + 1) * blk, :].astype(F32).T.astype(BF)

        qs = [q_ref[pl.ds(c * bqs, bqs), :] for c in range(ATTN_CHAINS)]

        def scores(j, c):
            r0 = pl.multiple_of(j * blk, blk)
            return _dot(k_ref[pl.ds(r0, blk), :], qs[c], "nt")

        def update(s, vj, carry):
            m, l, acc = carry
            m_new = jnp.maximum(m, jnp.max(s, axis=0, keepdims=True))
            alpha = jnp.exp2(m - m_new)
            p = jnp.exp2(s - m_new)
            l = alpha * l + jnp.sum(p, axis=0, keepdims=True)
            acc = alpha * acc + _dot(vj, p.astype(BF))
            return m_new, l, acc

        def step(j, carry):
            ss, st = carry
            nxt = tuple(scores(j + 1, c) for c in range(ATTN_CHAINS))
            vj = v_ref[j]
            return nxt, tuple(update(ss[c], vj, st[c]) for c in range(ATTN_CHAINS))

        init = tuple((jnp.full((1, bqs), -jnp.inf, F32), jnp.zeros((1, bqs), F32), jnp.zeros((DH, bqs), F32))
                     for _ in range(ATTN_CHAINS))
        first = tuple(scores(0, c) for c in range(ATTN_CHAINS))
        last, carry = lax.fori_loop(0, i, step, (first, init))
        rk = lax.broadcasted_iota(jnp.int32, (blk, bqs), 0)
        cq = lax.broadcasted_iota(jnp.int32, (blk, bqs), 1)
        vi = v_ref[i]
        for c in range(ATTN_CHAINS):
            s = jnp.where(cq + c * bqs >= rk, last[c], -jnp.inf)
            m, l, acc = update(s, vi, carry[c])
            o_ref[c * bqs:(c + 1) * bqs, :] = (acc / l).T.astype(o_ref.dtype)
            lse_ref[:, c * bqs:(c + 1) * bqs] = m + jnp.log(l) * _LOG2E

    vm = _nbytes((T, DA), BF) + 2 * _nbytes((T, DH), BF) + 6 * _nbytes((blk, blk), F32)
    return _pallas(
        body, (qa, ka, qkv), name="attn_fwd", grid=(NH, nb),
        in_specs=[pl.BlockSpec((blk, DA), lambda h, i: (i, h)),
                  pl.BlockSpec((T, DA), lambda h, i: (0, h)),
                  pl.BlockSpec((T, DH), lambda h, i: (0, 2 * NH + h))],
        out_specs=[pl.BlockSpec((blk, DH), lambda h, i: (i, h)),
                   pl.BlockSpec((None, None, 1, blk), lambda h, i: (h, i, 0, 0))],
        out_shape=[jax.ShapeDtypeStruct((T, D), BF), jax.ShapeDtypeStruct((NH, nb, 1, blk), F32)],
        scratch_shapes=[pltpu.VMEM((nb, DH, blk), BF)],
        semantics=("parallel", "arbitrary"), vmem=_vmem_limit(vm), side=side)


def _attn_bwd(qa, ka, qkv, do, lrow, drow, T, blk=512, side=None):
    blk = min(blk, T)
    nb = T // blk

    def body(ka_ref, v_ref, qa_ref, do_ref, l_ref, d_ref, dq_ref, dk_ref, dv_ref, dfs_ref, dft_ref, dq_s):
        j = pl.program_id(1)

        @pl.when(j == 0)
        def _():
            dq_s[...] = jnp.zeros_like(dq_s)

        row = lax.broadcasted_iota(jnp.int32, (DH + KT_ONES, blk), 0)
        dq_scale = jnp.where(row < DH, _SCALE, 1.0)

        kaj = ka_ref[...]
        ktj = jnp.concatenate([kaj[:, :DH].astype(F32).T.astype(BF), jnp.ones((KT_ONES, blk), BF)], axis=0)
        vj = v_ref[...]

        def step(i, carry, diag):
            dka, dv = carry
            r0 = pl.multiple_of(i * blk, blk)
            qi = qa_ref[pl.ds(r0, blk), :]
            doi = do_ref[pl.ds(r0, blk), :]
            st = _dot(kaj, qi, "nt") - l_ref[i]
            if diag:
                rk = lax.broadcasted_iota(jnp.int32, (blk, blk), 0)
                cq = lax.broadcasted_iota(jnp.int32, (blk, blk), 1)
                st = jnp.where(cq >= rk, st, -jnp.inf)
            pt = jnp.exp2(st)
            dv = dv + _dot(pt.astype(BF), doi)
            dpt = _dot(vj, doi, "nt")
            dst = pt * (dpt - d_ref[i])
            dsb = dst.astype(BF)
            dka = dka + _dot(dsb, qi)
            dq_s[i] += _dot(ktj, dsb) * dq_scale
            return dka, dv

        init = (jnp.zeros((blk, DA), F32), jnp.zeros((blk, DH), F32))
        carry = step(j, init, True)
        dka, dv = lax.fori_loop(j + 1, nb, lambda i, c: step(i, c, False), carry)
        dk_ref[...] = (dka[:, :DH] * (1.0 / _LOG2E)).astype(dk_ref.dtype)
        dv_ref[...] = dv.astype(dv_ref.dtype)
        dfs_ref[...] = dka[:, DH:].T[_KA_ONES - DH:_KA_ONES - DH + 1, :]

        @pl.when(j == nb - 1)
        def _():
            for ii in range(nb):
                t = dq_s[ii]
                dq_ref[ii * blk:(ii + 1) * blk, :] = t[:DH].T.astype(dq_ref.dtype)
                dft_ref[ii] = t[DH:DH + 1]

    rowv = pl.BlockSpec((None, nb, 1, blk), lambda h, j: (h, 0, 0, 0))
    vm = (_nbytes((T, DA), BF) + 2 * _nbytes((T, DH), BF) + _nbytes((T, DH + KT_ONES), F32)
          + 8 * _nbytes((blk, blk), F32))
    return _pallas(
        body, (ka, qkv, qa, do, lrow, drow), name="attn_bwd", grid=(NH, nb),
        in_specs=[pl.BlockSpec((blk, DA), lambda h, j: (j, h)),
                  pl.BlockSpec((blk, DH), lambda h, j: (j, 2 * NH + h)),
                  pl.BlockSpec((T, DA), lambda h, j: (0, h)),
                  pl.BlockSpec((T, DH), lambda h, j: (0, h)),
                  rowv, rowv],
        out_specs=[pl.BlockSpec((T, DH), lambda h, j: (0, h)),
                   pl.BlockSpec((blk, DH), lambda h, j: (j, h)),
                   pl.BlockSpec((blk, DH), lambda h, j: (j, h)),
                   pl.BlockSpec((None, None, 1, blk), lambda h, j: (h, j, 0, 0)), rowv],
        out_shape=[jax.ShapeDtypeStruct((T, D), BF), jax.ShapeDtypeStruct((T, D), BF),
                   jax.ShapeDtypeStruct((T, D), BF), jax.ShapeDtypeStruct((NH, nb, 1, blk), F32),
                   jax.ShapeDtypeStruct((NH, nb, 1, blk), F32)],
        scratch_shapes=[pltpu.VMEM((nb, DH + KT_ONES, blk), F32)],
        semantics=("parallel", "arbitrary"), vmem=_vmem_limit(vm), side=side)


def _norm_fn(ins, cs):
    x, = ins
    g, = cs
    r = lax.rsqrt(jnp.mean(x * x, axis=-1, keepdims=True) + EPS)
    return [x * r * g], []


def _norm_bwd_fn(ins, cs):
    x, dy, dres = ins
    g, = cs
    r = lax.rsqrt(jnp.mean(x * x, axis=-1, keepdims=True) + EPS)
    xh = x * r
    dxh = dy * g
    dx = dres + r * (dxh - xh * jnp.mean(dxh * xh, axis=-1, keepdims=True))
    return [dx], [_colsum(dy * xh)]


def _final_fn(ins, cs):
    x2, tgt = ins
    g, = cs
    r = lax.rsqrt(jnp.mean(x2 * x2, axis=-1, keepdims=True) + EPS)
    xh = x2 * r
    e = xh * g - tgt
    dy = e * (1.0 / D)
    dxh = dy * g
    dx2 = r * (dxh - xh * jnp.mean(dxh * xh, axis=-1, keepdims=True))
    return [dx2], [_colsum(0.5 * e * e * (1.0 / D)), _colsum(dy * xh)]


def _z_fn(ins, cs):
    g, h = [v.astype(F32) for v in ins]
    return [_gelu(g) * h], []


def _mix_fn(ins, cs):
    gates, ya, yb = [v.astype(F32) for v in ins]
    return [_sig(gates[:, :D]) * ya + _sig(gates[:, D:]) * yb], []


def _mix_bwd_fn(ins, cs):
    dmix, gates, ya, yb = [v.astype(F32) for v in ins]
    ga = _sig(gates[:, :D])
    gb = _sig(gates[:, D:])
    dgates = jnp.concatenate([dmix * ya * ga * (1.0 - ga), dmix * yb * gb * (1.0 - gb)], axis=1)
    return [dmix * ga, dmix * gb, dgates], []


def _z_bwd_fn(ins, cs):
    dz, g, h = [v.astype(F32) for v in ins]
    return [dz * _gelu(g), dz * h * _gelu_grad(g)], []


def _delta_fn(ins, cs):
    do, o = ins
    p = do.astype(F32) * o.astype(F32)
    lane = lax.broadcasted_iota(jnp.int32, (p.shape[0], DH), 1)
    out = jnp.zeros((p.shape[0], DH), F32)
    for hd in range(NH):
        s = jnp.sum(p[:, hd * DH:(hd + 1) * DH], axis=1, keepdims=True)
        out = jnp.where(lane == hd, s, out)
    return [out], []


def _du_all(pieces, win, T, tm=256, side=None):
    tm = min(tm, T)
    n = len(pieces)

    def body(*refs):
        w_ref, o_ref = refs[n], refs[n + 1]
        acc = None
        for (a, off), a_ref in zip(pieces, refs[:n]):
            d = _dot(a_ref[...].astype(BF), w_ref[:, off:off + a.shape[1]], "nt")
            acc = d if acc is None else acc + d
        o_ref[...] = acc

    vm = (sum(_nbytes((tm, a.shape[1]), a.dtype) for a, _ in pieces) + _nbytes(win.shape, win.dtype)
          + 2 * _nbytes((tm, D), F32))
    return _pallas(
        body, tuple(a for a, _ in pieces) + (win,), name="du_all", grid=(T // tm,),
        in_specs=[pl.BlockSpec((tm, a.shape[1]), lambda i: (i, 0)) for a, _ in pieces]
        + [pl.BlockSpec(win.shape, lambda i: (0, 0))],
        out_specs=[pl.BlockSpec((tm, D), lambda i: (i, 0))],
        out_shape=[jax.ShapeDtypeStruct((T, D), F32)],
        semantics=("arbitrary",), vmem=int(min(VMEM_CAP, 2 * vm + (4 << 20))), side=side)


def _local_step(x, tgt, w, T, blk=1024, dist=None):
    blk = min(blk, T)
    nb = T // blk
    win = w["win"]

    u, = _ew(_norm_fn, T, 512, [(x, D, 0)], [w["g_mix"]], [(D, BF)], [], name="norm_mix")
    xg = _mm(u, win, "nn", T, 2 * D, D, name="proj_lru", out_dtype=BF)
    qkv = _mm(u, win, "nn", T, 3 * D, D, name="proj_qkv", out_dtype=BF, b_off=(0, 2),
              epi=lambda acc: acc * jnp.where(pl.program_id(1) == 0, _C2, 1.0))
    gates = _mm(u, win, "nn", T, 2 * D, D, name="proj_gates", b_off=(0, 5), out_dtype=BF)
    fl = _mm(u, win, "nn", T, DH, D, name="proj_f", tn=DH, b_off=(0, 7 * D // DH))
    fcum = _fgate_fwd(fl, w["fb"], T)
    qa, ka = _ew(_aug_fn, T, 256, [(qkv, D, 0), (qkv, D, 1), (fcum, DH, 0)], [],
                 [(NH * DA, BF), (NH * DA, BF)], [], name="attn_augment")

    h = _lru_fwd(xg, w["cw"], w["vec"], w["wabd"], w["wxbd"], T)
    ob, lse, *landed = _attn_fwd(qa, ka, qkv, T, blk, side=dist.weights_side() if dist else None)
    if dist:
        w = dict(w, **dist.weights_landed(landed))
    z, = _ew(_z_fn, T, 512, [(xg, D, 1), (h, D, 0)], [], [(D, BF)], [], name="lru_gelu")
    ya = _mm(z, w["wa"], "nn", T, D, D, name="branch_a", out_dtype=BF)
    yb = _mm(ob, w["wb"], "nn", T, D, D, name="branch_b", out_dtype=BF)
    mix, = _ew(_mix_fn, T, 256, [(gates, 2 * D, 0), (ya, D, 0), (yb, D, 0)], [], [(D, BF)], [], name="mix")
    x1 = _mm(mix, w["wout"], "nn", T, D, D, name="out_proj", add=x)
    m, = _ew(_norm_fn, T, 512, [(x1, D, 0)], [w["g_mlp"]], [(D, BF)], [], name="norm_mlp")
    hh = _mm(m, w["wup"], "nn", T, FF, D, name="mlp_up", out_dtype=BF,
             epi=lambda acc: jnp.square(jnp.maximum(acc, 0.0)))
    x2 = _mm(hh, w["wdown"], "nn", T, D, FF, name="mlp_down", add=x1, tk=2048)
    dx2, loss_vec, dg_fin = _ew(_final_fn, T, 256, [(x2, D, 0), (tgt, D, 0)], [w["g_fin"]], [(D, F32)],
                                [(1, D), (1, D)], name="final_norm_loss")

    dhpre = _mm(dx2, w["wdown"], "nt", T, FF, D, name="mlp_down_bwd", out_dtype=BF,
                epi=lambda acc, h2: acc * (2.0 * jnp.sqrt(h2.astype(F32))), epi_ins=[hh])
    dwdown = _mm(hh, dx2, "tn", FF, D, T, name="dw_down", out_dtype=BF)
    dwup = _mm(m, dhpre, "tn", D, FF, T, name="dw_up", out_dtype=BF)
    dm = _mm(dhpre, w["wup"], "nt", T, D, FF, name="mlp_up_bwd", tk=2048)
    dx1, dg_mlp = _ew(_norm_bwd_fn, T, 256, [(x1, D, 0), (dm, D, 0), (dx2, D, 0)], [w["g_mlp"]], [(D, F32)],
                      [(1, D)], name="norm_mlp_bwd")

    dmix = _mm(dx1, w["wout"], "nt", T, D, D, name="out_proj_bwd", out_dtype=BF)
    dwout = _mm(mix, dx1, "tn", D, D, T, name="dw_out", out_dtype=BF)
    dya, dyb, dgates = _ew(_mix_bwd_fn, T, 256, [(dmix, D, 0), (gates, 2 * D, 0), (ya, D, 0), (yb, D, 0)], [],
                           [(D, BF), (D, BF), (2 * D, BF)], [], name="mix_bwd")
    dob = _mm(dyb, w["wb"], "nt", T, D, D, name="branch_b_bwd", out_dtype=BF)
    dwb = _mm(ob, dyb, "tn", D, D, T, name="dw_b", out_dtype=BF)
    dz = _mm(dya, w["wa"], "nt", T, D, D, name="branch_a_bwd", out_dtype=BF)
    dwa = _mm(z, dya, "tn", D, D, T, name="dw_a", out_dtype=BF)
    dha, dglru = _ew(_z_bwd_fn, T, 256, [(dz, D, 0), (xg, D, 1), (h, D, 0)], [], [(D, F32), (D, BF)], [],
                     name="lru_gelu_bwd")

    delta, = _ew(_delta_fn, T, 512, [(dob, D, 0), (ob, D, 0)], [], [(DH, F32)], [], name="attn_delta")
    drow = delta[:, :NH].T.reshape(NH, nb, 1, blk)
    big = dict(w_branch_a=dwa, w_branch_b=dwb, w_out=dwout, w_up=dwup, w_down=dwdown)
    side = dist.grads_side(big) if dist else None
    dq, dk, dv, dfs, dft, *landed = _attn_bwd(qa, ka, qkv, dob, lse, drow, T, blk, side=side)
    if dist:
        big = dist.grads_landed(side, landed)
    dfcum = jnp.pad((dft - dfs).reshape(NH, T).T, ((0, 0), (0, DH - NH)))
    dfl, dfb = _fgate_bwd(dfcum, fl, w["fb"], T)

    dxl, dwabd, dwxbd, lacc = _lru_bwd(xg, h, dha, w["cw"], w["vec"], w["wabd"], w["wxbd"], T)

    dproj = ((dxl, 0), (dglru, D), (dq, 2 * D), (dk, 3 * D), (dv, 4 * D), (dgates, 5 * D), (dfl, 7 * D))
    pieces = [_mm(u, p, "tn", D, p.shape[1], T, name="dw_in_%d" % n, out_dtype=BF)
              for n, (p, _) in enumerate(dproj)]
    pieces[-1] = pieces[-1][:, :NH]
    dwin = dict(w_in=jnp.concatenate(pieces, axis=1))
    side = dist.grads_side(dwin) if dist else None
    du, *landed = _du_all(dproj, win, T, side=side)
    big.update(dist.grads_landed(side, landed) if dist else dwin)
    dx, dg_mix = _ew(_norm_bwd_fn, T, 256, [(x, D, 0), (du, D, 0), (dx1, D, 0)], [w["g_mix"]], [(D, F32)],
                     [(1, D)], name="norm_mix_bwd")

    return dict(dx=dx, big=big, dwabd=dwabd, dwxbd=dwxbd, lacc=lacc, dfb=dfb, dg_mix=dg_mix, dg_mlp=dg_mlp,
                dg_fin=dg_fin, loss_vec=loss_vec)


def _block_diag(w):
    per = BD // LRU_BW
    w4 = w.reshape(NBD, per, LRU_BW, LRU_BW)
    out = jnp.zeros((NBD, per, LRU_BW, per, LRU_BW), w.dtype)
    for b in range(per):
        out = out.at[:, b, :, b, :].set(w4[:, b])
    return out.reshape(NBD, BD, BD)


def _block_diag_extract(wbd):
    per = BD // LRU_BW
    w5 = wbd.reshape(NBD, per, LRU_BW, per, LRU_BW)
    return jnp.stack([w5[:, b, :, b, :] for b in range(per)], axis=1).reshape(LRU_BLOCKS, LRU_BW, LRU_BW)


_ANY = pl.BlockSpec(memory_space=pl.ANY)


def _place():
    x, y, c = lax.axis_index("x"), lax.axis_index("y"), lax.axis_index("c")
    chips = [(1 - x, y), (x, 1 - y), (1 - x, 1 - y)]
    return x, y, c, chips


def _allgather_shards(shards):
    n = len(shards)

    def body(*refs):
        ins, outs = refs[:n], refs[n:2 * n]
        send_sems, recv_sems = refs[2 * n:]
        x, y, c, chips = _place()
        me = 2 * x + y
        sibling = (x, y, 1 - c)

        def remote(p, k, src, dst, to):
            return pltpu.make_async_remote_copy(src_ref=src, dst_ref=dst, send_sem=send_sems.at[p, k],
                                                recv_sem=recv_sems.at[p, k], device_id=to, device_id_type=MESH)

        sent = []
        for p in range(n):
            for k, chip in enumerate(chips):
                cp = remote(p, k, ins[p].at[c], outs[p].at[me, c], (chip[0], chip[1], c))
                cp.start()
                sent.append(cp)
        for p in range(n):
            for k, chip in enumerate(chips):
                half = outs[p].at[2 * chip[0] + chip[1], c]
                remote(p, k, half, half, sibling).wait_recv()
                fwd = remote(p, 3 + k, half, half, sibling)
                fwd.start()
                sent.append(fwd)
        for p in range(n):
            for k, chip in enumerate(chips):
                half = outs[p].at[2 * chip[0] + chip[1], 1 - c]
                remote(p, 3 + k, half, half, sibling).wait_recv()
        for cp in sent:
            cp.wait_send()

    gathered = pl.pallas_call(
        body, name="allgather_weights",
        in_specs=[_ANY] * n, out_specs=[_ANY] * n,
        out_shape=[jax.ShapeDtypeStruct((NCHIP,) + s.shape, s.dtype) for s in shards],
        scratch_shapes=[pltpu.SemaphoreType.DMA((n, 6)), pltpu.SemaphoreType.DMA((n, 6))],
    )(*shards)
    me = 2 * lax.axis_index("x") + lax.axis_index("y")
    return [lax.dynamic_update_index_in_dim(g, s, me, 0) for g, s in zip(gathered, shards)]


_LATE =["w_branch_a", "w_branch_b", "w_out", "w_up", "w_down"]
_COLUMN_CUT = ("w_in", "w_up")
N_PEERS = 7


def _shard_major(name, g):
    s = _columns_to_shards(g) if name in _COLUMN_CUT else g.reshape(NCHIP, g.shape[0] // NCHIP, g.shape[1])
    return s.reshape(NCHIP, 2, s.shape[1] // 2, s.shape[2])


class _Exchanges:
    def __init__(self, shards):
        self.shards = shards

    def weights_side(self):
        srcs = [self.shards[n] for n in _LATE]

        def copies(sin, sout, send, recv):
            x, y, c, chips = _place()
            return [pltpu.make_async_remote_copy(
                src_ref=sin[p], dst_ref=sout[p].at[2 * x + y], send_sem=send.at[3 * p + k], recv_sem=recv.at[3 * p + k],
                device_id=(chip[0], chip[1], c), device_id_type=MESH)
                for p in range(len(sin)) for k, chip in enumerate(chips)]

        return _Side(srcs, [jax.ShapeDtypeStruct((NCHIP,) + s.shape, s.dtype) for s in srcs], 3 * len(srcs), copies)

    def weights_landed(self, landed):
        me = 2 * lax.axis_index("x") + lax.axis_index("y")
        full = {n: lax.dynamic_update_index_in_dim(g, self.shards[n], me, 0) for n, g in zip(_LATE, landed)}
        return dict(wa=full["w_branch_a"].reshape(D, D), wb=full["w_branch_b"].reshape(D, D),
                    wout=full["w_out"].reshape(D, D), wup=_shards_to_columns(full["w_up"]),
                    wdown=full["w_down"].reshape(FF, D))

    def grads_side(self, grads):
        side_names = list(grads)
        srcs = [_shard_major(n, grads[n]) for n in side_names]

        def copies(sin, sout, send, recv):
            x, y, c, chips = _place()
            peers = [(x, y, 1 - c)] + [(cx, cy, c) for cx, cy in chips] + [(cx, cy, 1 - c) for cx, cy in chips]
            return [pltpu.make_async_remote_copy(
                src_ref=sin[p].at[2 * px + py, pc], dst_ref=sout[p].at[s], send_sem=send.at[N_PEERS * p + s],
                recv_sem=recv.at[N_PEERS * p + s], device_id=(px, py, pc), device_id_type=MESH)
                for p in range(len(sin)) for s, (px, py, pc) in enumerate(peers)]

        side = _Side(srcs, [jax.ShapeDtypeStruct((N_PEERS,) + s.shape[2:], s.dtype) for s in srcs],
                     N_PEERS * len(srcs), copies)
        side.names = side_names
        return side

    def grads_landed(self, side, landed):
        return {n: (own, got) for n, own, got in zip(side.names, side.srcs, landed)}


def _add8(g, got, me, c, name):
    _, _, half, cols = g.shape
    th = _row_tile(half, 2 * cols)

    def body(me_ref, c_ref, g_ref, r_ref, o_ref):
        acc = g_ref[...].astype(F32)
        for s in range(N_PEERS):
            acc = acc + r_ref[s].astype(F32)
        o_ref[...] = acc

    return pl.pallas_call(
        body, name=name,
        grid_spec=pltpu.PrefetchScalarGridSpec(
            num_scalar_prefetch=2, grid=(half // th,),
            in_specs=[pl.BlockSpec((None, None, th, cols), lambda i, me_ref, c_ref: (me_ref[0], c_ref[0], i, 0)),
                      pl.BlockSpec((N_PEERS, th, cols), lambda i, me_ref, c_ref: (0, i, 0))],
            out_specs=pl.BlockSpec((th, cols), lambda i, me_ref, c_ref: (i, 0))),
        out_shape=jax.ShapeDtypeStruct((half, cols), F32),
    )(me, c, g, got)


def _share_halves(halves):
    n = len(halves)

    def body(*refs):
        ins, outs = refs[:n], refs[n:2 * n]
        send_sems, recv_sems = refs[2 * n:]
        x, y, c, _ = _place()
        sibling = (x, y, 1 - c)
        copies = []
        for p in range(n):
            cp = pltpu.make_async_remote_copy(src_ref=ins[p], dst_ref=outs[p], send_sem=send_sems.at[p],
                                              recv_sem=recv_sems.at[p], device_id=sibling, device_id_type=MESH)
            cp.start()
            copies.append(cp)
        for cp in copies:
            cp.wait()

    return pl.pallas_call(
        body, name="reduce_share_halves",
        in_specs=[_ANY] * n, out_specs=[_ANY] * n,
        out_shape=[jax.ShapeDtypeStruct(h.shape, h.dtype) for h in halves],
        scratch_shapes=[pltpu.SemaphoreType.DMA((n,)), pltpu.SemaphoreType.DMA((n,))],
    )(*halves)


def _row_tile(half, cols):
    th = max(SLAB, min(half, (1 << 18) // cols // SLAB * SLAB))
    while half % th:
        th -= SLAB
    return th


N_DEV = 8
SMALL_ROWS = 208


def _allreduce_small(pack):
    def body(x_ref, out_ref, gbuf, send_sems, recv_sems, local_sem):
        x, y, c, chips = _place()
        me, sibling = (x, y, c), (x, y, 1 - c)

        def rows(px, py, pc):
            return gbuf.at[4 * px + 2 * py + pc]

        def copy(k, block, to, src=None):
            return pltpu.make_async_remote_copy(
                src_ref=rows(*block) if src is None else src, dst_ref=rows(*block),
                send_sem=send_sems.at[k], recv_sem=recv_sems.at[k], device_id=to, device_id_type=MESH)

        mine = pltpu.make_async_copy(x_ref, rows(*me), local_sem)
        mine.start()
        first = [copy(0, me, sibling, src=x_ref)]
        first += [copy(1 + j, me, (chip[0], chip[1], c), src=x_ref) for j, chip in enumerate(chips)]
        for cp in first:
            cp.start()
        passed = [copy(4 + j, (chip[0], chip[1], c), sibling) for j, chip in enumerate(chips)]
        for j, chip in enumerate(chips):
            copy(1 + j, (chip[0], chip[1], c), me).wait_recv()
            passed[j].start()
        copy(0, sibling, me).wait_recv()
        for j, chip in enumerate(chips):
            copy(4 + j, (chip[0], chip[1], 1 - c), me).wait_recv()
        for cp in first + passed:
            cp.wait_send()
        mine.wait()
        acc = gbuf[0]
        for d in range(1, N_DEV):
            acc = acc + gbuf[d]
        out_ref[...] = acc

    return pl.pallas_call(
        body, name="allreduce_small",
        in_specs=[pl.BlockSpec(memory_space=pltpu.VMEM)],
        out_specs=pl.BlockSpec(memory_space=pltpu.VMEM),
        out_shape=jax.ShapeDtypeStruct((SMALL_ROWS, D), F32),
        scratch_shapes=[pltpu.VMEM((N_DEV, SMALL_ROWS, D), F32), pltpu.SemaphoreType.DMA((7,)),
                        pltpu.SemaphoreType.DMA((7,)), pltpu.SemaphoreType.DMA],
    )(pack)


def _adamw(w, g, m, v, name):
    rows, cols = w.shape
    th = _row_tile(rows, cols)

    def body(w_ref, g_ref, m_ref, v_ref, d_ref, mo_ref, vo_ref):
        gv = g_ref[...]
        mn = ADAM_B1 * m_ref[...] + (1.0 - ADAM_B1) * gv
        vn = ADAM_B2 * v_ref[...] + (1.0 - ADAM_B2) * (gv * gv)
        m_hat = mn / (1.0 - ADAM_B1 ** ADAM_STEP)
        v_hat = vn / (1.0 - ADAM_B2 ** ADAM_STEP)
        d_ref[...] = -ADAM_LR * (m_hat / (jnp.sqrt(v_hat) + ADAM_EPS) + ADAM_WD * w_ref[...])
        mo_ref[...] = mn
        vo_ref[...] = vn

    spec = pl.BlockSpec((th, cols), lambda i: (i, 0))
    return pl.pallas_call(
        body, name=name, grid=(rows // th,),
        in_specs=[spec] * 4, out_specs=[spec] * 3,
        out_shape=[jax.ShapeDtypeStruct((rows, cols), F32)] * 3,
        compiler_params=pltpu.CompilerParams(dimension_semantics=("parallel",)),
    )(w, g, m, v)


_SMALL = ["norm_mix_g", "norm_mlp_g", "norm_final_g", "conv_b", "lru_ba", "lru_bx", "lru_lambda"]
_ROW_FB, _ROW_CW, _ROW_WA, _ROW_WX, _ROW_LOSS = 56, 64, 72, 136, 200


def _pack_small(vals, col0):
    def slab(a):
        return jnp.pad(a, ((0, -a.shape[0] % SLAB), (0, D - a.shape[1])))

    rows = [slab(vals[n].reshape(1, D)) for n in _SMALL]
    rows.append(slab(vals["forget_b"].reshape(1, NH)))
    if vals["conv_w"].shape[1] == D:
        rows.append(slab(vals["conv_w"]))
    else:
        rows.append(slab(lax.dynamic_update_slice(jnp.zeros((CONV, D), F32), vals["conv_w"], (0, col0))))
    rows.append(vals["lru_wa"].reshape(LRU_BLOCKS * LRU_BW * LRU_BW // D, D))
    rows.append(vals["lru_wx"].reshape(LRU_BLOCKS * LRU_BW * LRU_BW // D, D))
    rows.append(slab(vals["loss"]) if "loss" in vals else jnp.zeros((SLAB, D), F32))
    return jnp.concatenate(rows, axis=0)


def _unpack_small(pack, col0):
    out = {n: pack[SLAB * i] for i, n in enumerate(_SMALL)}
    out["forget_b"] = pack[_ROW_FB, :NH]
    out["conv_w"] = lax.dynamic_slice(pack[_ROW_CW:_ROW_CW + CONV], (0, col0), (CONV, D // NCHIP))
    out["lru_wa"] = pack[_ROW_WA:_ROW_WX].reshape(LRU_BLOCKS, LRU_BW, LRU_BW)
    out["lru_wx"] = pack[_ROW_WX:_ROW_LOSS].reshape(LRU_BLOCKS, LRU_BW, LRU_BW)
    return out


_WEIGHTS = ["norm_mix_g", "w_in", "conv_w", "conv_b", "lru_wa", "lru_ba", "lru_wx", "lru_bx", "lru_lambda",
            "forget_b", "w_branch_a", "w_branch_b", "w_out", "norm_mlp_g", "w_up", "w_down", "norm_final_g"]
_BIG = ["w_in", "w_branch_a", "w_branch_b", "w_out", "w_up", "w_down"]


def _halves(a):
    return a.reshape(2, a.shape[0] // 2, a.shape[1])


def _columns_to_shards(a):
    rows, cols = a.shape[0], a.shape[1] // NCHIP
    return jnp.transpose(a.reshape(rows, NCHIP, cols), (1, 0, 2))


def _shards_to_columns(a):
    n, rows, cols = a.shape
    return jnp.transpose(a, (1, 0, 2)).reshape(rows, n * cols)


def kernel(x, norm_mix_g, w_in, conv_w, conv_b, lru_wa, lru_ba, lru_wx, lru_bx, lru_lambda, forget_b, w_branch_a, w_branch_b, w_out, norm_mlp_g, w_up, w_down, norm_final_g, loss_target, m_norm_mix_g, m_w_in, m_conv_w, m_conv_b, m_lru_wa, m_lru_ba, m_lru_wx, m_lru_bx, m_lru_lambda, m_forget_b, m_w_branch_a, m_w_branch_b, m_w_out, m_norm_mlp_g, m_w_up, m_w_down, m_norm_final_g, v_norm_mix_g, v_w_in, v_conv_w, v_conv_b, v_lru_wa, v_lru_ba, v_lru_wx, v_lru_bx, v_lru_lambda, v_forget_b, v_w_branch_a, v_w_branch_b, v_w_out, v_norm_mlp_g, v_w_up, v_w_down, v_norm_final_g):
    args = dict(locals())
    wts = {n: args[n] for n in _WEIGHTS}
    mom = {n: args["m_" + n] for n in _WEIGHTS}
    var = {n: args["v_" + n] for n in _WEIGHTS}
    T = x.shape[1]
    xi, yi, ci = lax.axis_index("x"), lax.axis_index("y"), lax.axis_index("c")
    me = 2 * xi + yi
    c1 = jnp.reshape(ci, (1,)).astype(jnp.int32)
    me1 = jnp.reshape(me, (1,)).astype(jnp.int32)
    col0 = me * (D // NCHIP)

    cw_pad = jnp.pad(conv_w, ((0, 4 * SLAB - CONV), (0, 0)))
    g_in, g_cw = _allgather_shards([_halves(w_in.astype(BF)), _halves(cw_pad)])
    cin = DIN // NCHIP
    win = _shards_to_columns(g_in.reshape(NCHIP, D, cin))
    w = dict(
        win=jnp.pad(win, ((0, 0), (0, DINP - DIN))),
        cw=_shards_to_columns(g_cw.reshape(NCHIP, 4 * SLAB, D // NCHIP)[:, :CONV]),
        vec=jnp.concatenate([conv_b[None], lru_ba[None], lru_bx[None], lru_lambda[None],
                             jnp.zeros((SLAB - 4, D), F32)], axis=0),
        fb=jnp.pad(forget_b[None], ((0, 0), (0, DH - NH))),
        wabd=_block_diag(lru_wa).astype(BF), wxbd=_block_diag(lru_wx).astype(BF),
        g_mix=norm_mix_g[None], g_mlp=norm_mlp_g[None], g_fin=norm_final_g[None])

    r = _local_step(x[0], loss_target[0], w, T, dist=_Exchanges({n: wts[n].astype(BF) for n in _LATE}))

    halves = [_add8(*r["big"][n], me1, c1, "add8_" + n) for n in _BIG]
    theirs = _share_halves(halves)
    low = ci == 0
    gsum = {n: jnp.concatenate([jnp.where(low, h, t), jnp.where(low, t, h)], axis=0)
            for n, h, t in zip(_BIG, halves, theirs)}
    lacc = r["lacc"]
    small = dict(norm_mix_g=r["dg_mix"], norm_mlp_g=r["dg_mlp"], norm_final_g=r["dg_fin"], conv_b=lacc[3],
                 lru_ba=lacc[0], lru_bx=lacc[1], lru_lambda=lacc[2], forget_b=r["dfb"][0, :NH],
                 conv_w=lacc[4:4 + CONV], lru_wa=_block_diag_extract(r["dwabd"]),
                 lru_wx=_block_diag_extract(r["dwxbd"]), loss=r["loss_vec"])
    gpack = _allreduce_small(_pack_small(small, col0))
    loss = jnp.sum(gpack[_ROW_LOSS])

    grads, delta, new_m, new_v = {}, {}, {}, {}
    for n in _BIG:
        grads[n] = gsum[n]
        delta[n], new_m[n], new_v[n] = _adamw(wts[n], gsum[n], mom[n], var[n], "adamw_" + n)
    dp, mp, vp = _adamw(_pack_small(wts, col0), gpack, _pack_small(mom, col0), _pack_small(var, col0), "adamw_small")
    for dst, pack in ((grads, gpack), (delta, dp), (new_m, mp), (new_v, vp)):
        dst.update(_unpack_small(pack, col0))
    return (loss, r["dx"][None], *[grads[n] for n in _WEIGHTS], *[delta[n] for n in _WEIGHTS],
            *[new_m[n] for n in _WEIGHTS], *[new_v[n] for n in _WEIGHTS])
```

```python
import functools
import math

import jax
import jax.numpy as jnp
import numpy as np
from jax import lax
from jax.experimental import pallas as pl
from jax.experimental.pallas import tpu as pltpu

F32 = jnp.float32
BF = jnp.bfloat16

D = 1024
NH = 8
DH = 128
FF = 4096
CONV = 4
LRU_BLOCKS = 16
LRU_BW = 64
BD = 256
NBD = D // BD
LRU_C = 8.0
EPS = 1e-6
DIN = 7176
DINP = 7296
NCHIP = 4
SLAB = 8
VMEM_CAP = 60 * 1024 * 1024

ADAM_LR = 0.001
ADAM_B1 = 0.9
ADAM_B2 = 0.999
ADAM_EPS = 1e-08
ADAM_WD = 0.01
ADAM_STEP = 10

MESH = pl.DeviceIdType.MESH


def _vmem_limit(nbytes):
    return int(min(VMEM_CAP, max(32 * 1024 * 1024, 3 * nbytes)))


def _nbytes(shape, dtype):
    return int(np.prod(shape)) * jnp.dtype(dtype).itemsize


def _sig(x):
    return 0.5 * jnp.tanh(0.5 * x) + 0.5


def _log1p(u):
    w = 1.0 + u
    return jnp.where(w == 1.0, u, jnp.log(w) * (u / (w - 1.0)))


def _one_minus_sq(a, la):
    return jnp.tanh(-la) * (1.0 + a * a)


def _softplus(z):
    return jnp.maximum(z, 0.0) + _log1p(jnp.exp(-jnp.abs(z)))


_GELU_C = math.sqrt(2.0 / math.pi)


def _gelu(x):
    return 0.5 * x * (1.0 + jnp.tanh(_GELU_C * (x + 0.044715 * x * x * x)))


def _gelu_grad(x):
    t = jnp.tanh(_GELU_C * (x + 0.044715 * x * x * x))
    return 0.5 * (1.0 + t) + 0.5 * x * (1.0 - t * t) * _GELU_C * (1.0 + 3.0 * 0.044715 * x * x)


def _shift_down(x, d, prev8):
    n = x.shape[0]
    row8 = lax.broadcasted_iota(jnp.int32, (SLAB, x.shape[1]), 0)
    y = pltpu.roll(x, d, 0)
    top = jnp.where(row8 < d, pltpu.roll(prev8, d, 0), y[0:SLAB])
    if n == SLAB:
        return top
    return jnp.concatenate([top, y[SLAB:]], axis=0)


def _shift_up(x, d, next8):
    n = x.shape[0]
    row8 = lax.broadcasted_iota(jnp.int32, (SLAB, x.shape[1]), 0)
    y = pltpu.roll(x, n - d, 0)
    bottom = jnp.where(row8 >= SLAB - d, pltpu.roll(next8, SLAB - d, 0), y[n - SLAB:])
    if n == SLAB:
        return bottom
    return jnp.concatenate([y[:n - SLAB], bottom], axis=0)


def _slab_scan_fwd(a, b):
    row = lax.broadcasted_iota(jnp.int32, a.shape, 0)
    for k in (1, 2, 4):
        a_s = pltpu.roll(a, k, 0)
        b_s = pltpu.roll(b, k, 0)
        m = row >= k
        b = jnp.where(m, a * b_s + b, b)
        a = jnp.where(m, a * a_s, a)
    return a, b


def _slab_scan_bwd(a, b):
    row = lax.broadcasted_iota(jnp.int32, a.shape, 0)
    for k in (1, 2, 4):
        a_s = pltpu.roll(a, SLAB - k, 0)
        b_s = pltpu.roll(b, SLAB - k, 0)
        m = row < SLAB - k
        b = jnp.where(m, a * b_s + b, b)
        a = jnp.where(m, a * a_s, a)
    return a, b


_DIMS = {"nn": (((1,), (0,)), ((), ())), "nt": (((1,), (1,)), ((), ())), "tn": (((0,), (0,)), ((), ()))}


def _dot(a, b, mode="nn"):
    return lax.dot_general(a, b, _DIMS[mode], preferred_element_type=F32)


def _mm(a, b, mode, M, N, K, *, name, out_dtype=F32, tm=1024, tn=1024, tk=1024,
        a_off=(0, 0), b_off=(0, 0), add=None, epi=None, epi_ins=()):
    tm, tn, tk = min(tm, M), min(tn, N), min(tk, K)
    nk = K // tk
    grid = (M // tm, N // tn, nk)
    if mode == "nn":
        a_spec = pl.BlockSpec((tm, tk), lambda i, j, k: (i + a_off[0], k + a_off[1]))
        b_spec = pl.BlockSpec((tk, tn), lambda i, j, k: (k + b_off[0], j + b_off[1]))
    elif mode == "nt":
        a_spec = pl.BlockSpec((tm, tk), lambda i, j, k: (i + a_off[0], k + a_off[1]))
        b_spec = pl.BlockSpec((tn, tk), lambda i, j, k: (j + b_off[0], k + b_off[1]))
    else:
        a_spec = pl.BlockSpec((tk, tm), lambda i, j, k: (k + a_off[0], i + a_off[1]))
        b_spec = pl.BlockSpec((tk, tn), lambda i, j, k: (k + b_off[0], j + b_off[1]))
    o_spec = pl.BlockSpec((tm, tn), lambda i, j, k: (i, j))
    extra = ([add] if add is not None else []) + list(epi_ins)
    n_extra = len(extra)
    has_add = add is not None

    def body(*refs):
        a_ref, b_ref = refs[0], refs[1]
        ex = refs[2:2 + n_extra]
        o_ref = refs[2 + n_extra]

        def finish(acc):
            if has_add:
                acc = acc + ex[0][...].astype(F32)
            if epi is not None:
                acc = epi(acc, *[e[...] for e in ex[(1 if has_add else 0):]])
            o_ref[...] = acc.astype(o_ref.dtype)

        p = _dot(a_ref[...].astype(BF), b_ref[...].astype(BF), mode)
        if nk == 1:
            finish(p)
        else:
            acc_ref = refs[3 + n_extra]
            k = pl.program_id(2)

            @pl.when(k == 0)
            def _():
                acc_ref[...] = p

            @pl.when(k > 0)
            def _():
                acc_ref[...] += p

            @pl.when(k == nk - 1)
            def _():
                finish(acc_ref[...])

    blk = (_nbytes((tm, tk), a.dtype) + _nbytes((tk, tn), b.dtype) + _nbytes((tm, tn), out_dtype)
           + sum(_nbytes((tm, tn), e.dtype) for e in extra) + 2 * _nbytes((tm, tn), F32))
    return pl.pallas_call(
        body, name=name, grid=grid,
        in_specs=[a_spec, b_spec] + [o_spec] * n_extra,
        out_specs=o_spec,
        out_shape=jax.ShapeDtypeStruct((M, N), out_dtype),
        scratch_shapes=[pltpu.VMEM((tm, tn), F32)] if nk > 1 else [],
        compiler_params=pltpu.CompilerParams(
            dimension_semantics=("parallel", "parallel", "arbitrary"), vmem_limit_bytes=_vmem_limit(blk)),
    )(a, b, *extra)


def _ew(fn, T, tm, ins, consts, outs, accs, *, name, reverse=False):
    tm = min(tm, T)
    nt = T // tm
    n_in, n_c, n_o, n_a = len(ins), len(consts), len(outs), len(accs)

    def row(i):
        return nt - 1 - i if reverse else i

    in_specs = [pl.BlockSpec((tm, w), functools.partial(lambda i, cb: (row(i), cb), cb=cb)) for (_, w, cb) in ins]
    in_specs += [pl.BlockSpec(c.shape, functools.partial(lambda i, nd: (0,) * nd, nd=c.ndim)) for c in consts]
    out_specs = [pl.BlockSpec((tm, w), lambda i: (row(i), 0)) for (w, _) in outs]
    out_specs += [pl.BlockSpec((r, w), lambda i: (0, 0)) for (r, w) in accs]
    out_shape = [jax.ShapeDtypeStruct((T, w), dt) for (w, dt) in outs]
    out_shape += [jax.ShapeDtypeStruct((r, w), F32) for (r, w) in accs]

    def body(*refs):
        in_refs = refs[:n_in]
        c_refs = refs[n_in:n_in + n_c]
        o_refs = refs[n_in + n_c:n_in + n_c + n_o]
        a_refs = refs[n_in + n_c + n_o:]
        ov, av = fn([r[...] for r in in_refs], [r[...] for r in c_refs])
        for r, v in zip(o_refs, ov):
            r[...] = v.astype(r.dtype)
        if n_a:
            i = pl.program_id(0)

            @pl.when(i == 0)
            def _():
                for r, v in zip(a_refs, av):
                    r[...] = v

            @pl.when(i > 0)
            def _():
                for r, v in zip(a_refs, av):
                    r[...] += v

    blk = (sum(_nbytes((tm, w), a.dtype) for (a, w, _) in ins) + sum(_nbytes(c.shape, c.dtype) for c in consts)
           + sum(_nbytes((tm, w), dt) for (w, dt) in outs) + sum(_nbytes(s, F32) for s in accs))
    res = pl.pallas_call(
        body, name=name, grid=(nt,), in_specs=in_specs, out_specs=out_specs, out_shape=out_shape,
        compiler_params=pltpu.CompilerParams(
            dimension_semantics=("arbitrary",), vmem_limit_bytes=_vmem_limit(blk)),
    )(*[a for (a, _, _) in ins], *consts)
    return res


def _colsum(v):
    return jnp.sum(v, axis=0, keepdims=True)


FGATE_GROUP = 4


def _fgate_fwd(fl, fb, T, tm=512):
    tm = min(tm, T)

    def body(fl_ref, fb_ref, f_ref, carry_ref):
        i = pl.program_id(0)

        @pl.when(i == 0)
        def _():
            carry_ref[...] = jnp.zeros_like(carry_ref)

        rows = FGATE_GROUP * SLAB
        sub = lax.broadcasted_iota(jnp.int32, (rows, DH), 0) % SLAB

        def group(s, carry):
            r0 = pl.multiple_of(s * rows, rows)
            z = fl_ref[pl.ds(r0, rows), :] + fb_ref[...]
            c = jnp.minimum(z, 0.0) - _log1p(jnp.exp(-jnp.abs(z)))
            for k in (1, 2, 4):
                c = c + jnp.where(sub >= k, pltpu.roll(c, k, 0), 0.0)
            for u in range(FGATE_GROUP):
                cu = c[u * SLAB:(u + 1) * SLAB] + carry
                f_ref[pl.ds(r0 + u * SLAB, SLAB), :] = cu
                carry = cu[SLAB - 1:SLAB, :]
            return carry

        carry_ref[0:1, :] = lax.fori_loop(0, tm // rows, group, carry_ref[0:1, :])

    return pl.pallas_call(
        body, name="fgate_fwd", grid=(T // tm,),
        in_specs=[pl.BlockSpec((tm, DH), lambda i: (i, 0)), pl.BlockSpec((1, DH), lambda i: (0, 0))],
        out_specs=pl.BlockSpec((tm, DH), lambda i: (i, 0)),
        out_shape=jax.ShapeDtypeStruct((T, DH), F32),
        scratch_shapes=[pltpu.VMEM((SLAB, DH), F32)],
        compiler_params=pltpu.CompilerParams(dimension_semantics=("arbitrary",)),
    )(fl, fb)


def _fgate_bwd(dF, fl, fb, T, tm=512):
    tm = min(tm, T)
    nt = T // tm

    def body(df_ref, fl_ref, fb_ref, o_ref, acc_ref, carry_ref):
        i = pl.program_id(0)

        @pl.when(i == 0)
        def _():
            carry_ref[...] = jnp.zeros_like(carry_ref)
            acc_ref[...] = jnp.zeros_like(acc_ref)

        rows = FGATE_GROUP * SLAB
        sub = lax.broadcasted_iota(jnp.int32, (rows, DH), 0) % SLAB

        def group(n, carry):
            g_next, acc = carry
            r0 = pl.multiple_of((tm // rows - 1 - n) * rows, rows)
            c = df_ref[pl.ds(r0, rows), :]
            for k in (1, 2, 4):
                c = c + jnp.where(sub < SLAB - k, pltpu.roll(c, rows - k, 0), 0.0)
            sg = _sig(-(fl_ref[pl.ds(r0, rows), :] + fb_ref[...]))
            for u in reversed(range(FGATE_GROUP)):
                cu = c[u * SLAB:(u + 1) * SLAB] + g_next
                dfl = cu * sg[u * SLAB:(u + 1) * SLAB]
                o_ref[pl.ds(r0 + u * SLAB, SLAB), :] = dfl.astype(o_ref.dtype)
                acc = acc + _colsum(dfl)
                g_next = cu[0:1, :]
            return g_next, acc

        g, acc = lax.fori_loop(0, tm // rows, group, (carry_ref[0:1, :], jnp.zeros((1, DH), F32)))
        carry_ref[0:1, :] = g
        acc_ref[...] += acc

    return pl.pallas_call(
        body, name="fgate_bwd", grid=(nt,),
        in_specs=[pl.BlockSpec((tm, DH), lambda i: (nt - 1 - i, 0)), pl.BlockSpec((tm, DH), lambda i: (nt - 1 - i, 0)),
                  pl.BlockSpec((1, DH), lambda i: (0, 0))],
        out_specs=[pl.BlockSpec((tm, DH), lambda i: (nt - 1 - i, 0)), pl.BlockSpec((1, DH), lambda i: (0, 0))],
        out_shape=[jax.ShapeDtypeStruct((T, DH), BF), jax.ShapeDtypeStruct((1, DH), F32)],
        scratch_shapes=[pltpu.VMEM((SLAB, DH), F32)],
        compiler_params=pltpu.CompilerParams(dimension_semantics=("arbitrary",)),
    )(dF, fl, fb)


def _conv(x, prev8, cw, cb):
    xs = [x] + [_shift_down(x, d, prev8) for d in (1, 2, 3)]
    xa = cb + cw[3:4, :] * xs[0] + cw[2:3, :] * xs[1] + cw[1:2, :] * xs[2] + cw[0:1, :] * xs[3]
    return xa, xs


def _lru_gates(xa_g, wa_g, wx_g, ba_g, bx_g, sp_g):
    xb = xa_g.astype(BF)
    r = _sig(_dot(xb, wa_g) + ba_g)
    ig = _sig(_dot(xb, wx_g) + bx_g)
    la = -LRU_C * r * sp_g
    a = jnp.exp(la)
    mult = jnp.sqrt(_one_minus_sq(a, la))
    return r, ig, a, mult


def _lru_fwd(xg, cw, vec, wabd, wxbd, T, tm=256):
    tm = min(tm, T)
    nsl = tm // SLAB

    def body(x_ref, xp_ref, cw_ref, vec_ref, wa_ref, wx_ref, h_ref, a_s, b_s, carry_ref):
        i = pl.program_id(0)

        @pl.when(i == 0)
        def _():
            carry_ref[...] = jnp.zeros_like(carry_ref)

        x = x_ref[...].astype(F32)
        prev8 = jnp.where(i > 0, xp_ref[SLAB:, :].astype(F32), 0.0)
        vec_v = vec_ref[...]
        xa, _ = _conv(x, prev8, cw_ref[...], vec_v[0:1, :])
        sp = _softplus(-vec_v[3:4, :])
        for g in range(NBD):
            sl = slice(g * BD, (g + 1) * BD)
            _, ig, a, mult = _lru_gates(xa[:, sl], wa_ref[g], wx_ref[g], vec_v[1:2, sl], vec_v[2:3, sl], sp[:, sl])
            a_s[:, sl] = a
            b_s[:, sl] = mult * ig * xa[:, sl]

        def slab(s, carry):
            r0 = pl.multiple_of(s * SLAB, SLAB)
            A, B = _slab_scan_fwd(a_s[pl.ds(r0, SLAB), :], b_s[pl.ds(r0, SLAB), :])
            h = A * carry + B
            h_ref[pl.ds(r0, SLAB), :] = h
            return h[SLAB - 1:SLAB, :]

        carry_ref[0:1, :] = lax.fori_loop(0, nsl, slab, carry_ref[0:1, :])

    blk = 5 * _nbytes((tm, D), F32) + 2 * _nbytes((NBD, BD, BD), BF)
    return pl.pallas_call(
        body, name="lru_fwd", grid=(T // tm,),
        in_specs=[pl.BlockSpec((tm, D), lambda i: (i, 0)),
                  pl.BlockSpec((2 * SLAB, D), lambda i: (jnp.maximum(i * (nsl // 2) - 1, 0), 0)),
                  pl.BlockSpec((CONV, D), lambda i: (0, 0)),
                  pl.BlockSpec((SLAB, D), lambda i: (0, 0)),
                  pl.BlockSpec((NBD, BD, BD), lambda i: (0, 0, 0)),
                  pl.BlockSpec((NBD, BD, BD), lambda i: (0, 0, 0))],
        out_specs=pl.BlockSpec((tm, D), lambda i: (i, 0)),
        out_shape=jax.ShapeDtypeStruct((T, D), F32),
        scratch_shapes=[pltpu.VMEM((tm, D), F32), pltpu.VMEM((tm, D), F32), pltpu.VMEM((SLAB, D), F32)],
        compiler_params=pltpu.CompilerParams(dimension_semantics=("arbitrary",), vmem_limit_bytes=_vmem_limit(blk)),
    )(xg, xg, cw, vec, wabd, wxbd)


def _lru_bwd(xg, h, dha, cw, vec, wabd, wxbd, T, tm=256):
    tm = min(tm, T)
    nsl = tm // SLAB
    nt = T // tm

    def body(x_ref, xp_ref, h_ref, hp_ref, dh_ref, cw_ref, vec_ref, wa_ref, wx_ref,
             dx_ref, dwa_ref, dwx_ref, acc_ref, a_s, b_s, g_s, dxa_s, carry_ref, dxan_ref):
        n = pl.program_id(0)
        it = nt - 1 - n

        @pl.when(n == 0)
        def _():
            carry_ref[...] = jnp.zeros_like(carry_ref)
            dxan_ref[...] = jnp.zeros_like(dxan_ref)
            dwa_ref[...] = jnp.zeros_like(dwa_ref)
            dwx_ref[...] = jnp.zeros_like(dwx_ref)
            acc_ref[...] = jnp.zeros_like(acc_ref)

        x = x_ref[...].astype(F32)
        prev8 = jnp.where(it > 0, xp_ref[SLAB:, :].astype(F32), 0.0)
        hprev8 = jnp.where(it > 0, hp_ref[...], 0.0)
        vec_v = vec_ref[...]
        cw_v = cw_ref[...]
        xa, xs = _conv(x, prev8, cw_v, vec_v[0:1, :])
        sp = _softplus(-vec_v[3:4, :])
        gates = []
        for g in range(NBD):
            sl = slice(g * BD, (g + 1) * BD)
            r, ig, a, mult = _lru_gates(xa[:, sl], wa_ref[g], wx_ref[g], vec_v[1:2, sl], vec_v[2:3, sl], sp[:, sl])
            gates.append((r, ig, a, mult))
            a_s[:, sl] = a
        a_next = _shift_up(a_s[...], 1, carry_ref[...])
        a_s[...] = a_next
        b_s[...] = dh_ref[...]

        def slab(m, carry):
            r0 = pl.multiple_of((nsl - 1 - m) * SLAB, SLAB)
            A, B = _slab_scan_bwd(a_s[pl.ds(r0, SLAB), :], b_s[pl.ds(r0, SLAB), :])
            gg = A * carry + B
            g_s[pl.ds(r0, SLAB), :] = gg
            return gg[0:1, :]

        g_first = lax.fori_loop(0, nsl, slab, carry_ref[1:2, :])
        gt = g_s[...]
        h_prev = _shift_down(h_ref[...], 1, hprev8)
        dba = []
        dbx = []
        dsp = []
        for g in range(NBD):
            sl = slice(g * BD, (g + 1) * BD)
            r, ig, a, mult = gates[g]
            xa_g = xa[:, sl]
            g_g = gt[:, sl]
            da = g_g * h_prev[:, sl]
            dmult = g_g * ig * xa_g
            di = g_g * mult * xa_g
            dxa_g = g_g * mult * ig
            dla = da * a - dmult * (a * a / mult)
            dr = dla * (-LRU_C) * sp[:, sl]
            dsp.append(_colsum(dla * (-LRU_C) * r))
            dra = (dr * r * (1.0 - r))
            dix = (di * ig * (1.0 - ig))
            dba.append(_colsum(dra))
            dbx.append(_colsum(dix))
            dra_b = dra.astype(BF)
            dix_b = dix.astype(BF)
            xb = xa_g.astype(BF)
            dxa_g = dxa_g + _dot(dra_b, wa_ref[g], "nt") + _dot(dix_b, wx_ref[g], "nt")
            dwa_ref[g] += _dot(xb, dra_b, "tn")
            dwx_ref[g] += _dot(xb, dix_b, "tn")
            dxa_s[:, sl] = dxa_g
        dxa = dxa_s[...]
        nxt = dxan_ref[...]
        dx = (cw_v[3:4, :] * dxa + cw_v[2:3, :] * _shift_up(dxa, 1, nxt)
              + cw_v[1:2, :] * _shift_up(dxa, 2, nxt) + cw_v[0:1, :] * _shift_up(dxa, 3, nxt))
        dx_ref[...] = dx.astype(dx_ref.dtype)
        acc_ref[0:1, :] += jnp.concatenate(dba, axis=1)
        acc_ref[1:2, :] += jnp.concatenate(dbx, axis=1)
        acc_ref[2:3, :] += jnp.concatenate(dsp, axis=1)
        acc_ref[3:4, :] += _colsum(dxa)
        for k in range(CONV):
            acc_ref[4 + k:5 + k, :] += _colsum(dxa * xs[CONV - 1 - k])
        dxan_ref[...] = dxa[0:SLAB, :]
        a_first = jnp.concatenate([gates[g][2][0:1, :] for g in range(NBD)], axis=1)
        carry_ref[0:1, :] = a_first
        carry_ref[1:2, :] = g_first

        @pl.when(n == nt - 1)
        def _():
            acc_ref[2:3, :] = acc_ref[2:3, :] * (-_sig(-vec_v[3:4, :]))

    rowblk = lambda i: (nt - 1 - i, 0)
    prevblk = lambda i: (jnp.maximum((nt - 1 - i) * nsl - 1, 0), 0)
    c2 = lambda i: (0, 0)
    c3 = lambda i: (0, 0, 0)
    blk = 12 * _nbytes((tm, D), F32) + 6 * _nbytes((NBD, BD, BD), F32)
    return pl.pallas_call(
        body, name="lru_bwd", grid=(nt,),
        in_specs=[pl.BlockSpec((tm, D), rowblk),
                  pl.BlockSpec((2 * SLAB, D), lambda i: (jnp.maximum((nt - 1 - i) * (nsl // 2) - 1, 0), 0)),
                  pl.BlockSpec((tm, D), rowblk), pl.BlockSpec((SLAB, D), prevblk),
                  pl.BlockSpec((tm, D), rowblk),
                  pl.BlockSpec((CONV, D), c2), pl.BlockSpec((SLAB, D), c2),
                  pl.BlockSpec((NBD, BD, BD), c3), pl.BlockSpec((NBD, BD, BD), c3)],
        out_specs=[pl.BlockSpec((tm, D), rowblk), pl.BlockSpec((NBD, BD, BD), c3), pl.BlockSpec((NBD, BD, BD), c3),
                   pl.BlockSpec((16, D), c2)],
        out_shape=[jax.ShapeDtypeStruct((T, D), BF), jax.ShapeDtypeStruct((NBD, BD, BD), F32),
                   jax.ShapeDtypeStruct((NBD, BD, BD), F32), jax.ShapeDtypeStruct((16, D), F32)],
        scratch_shapes=[pltpu.VMEM((tm, D), F32), pltpu.VMEM((tm, D), F32), pltpu.VMEM((tm, D), F32),
                        pltpu.VMEM((tm, D), F32), pltpu.VMEM((SLAB, D), F32), pltpu.VMEM((SLAB, D), F32)],
        compiler_params=pltpu.CompilerParams(dimension_semantics=("arbitrary",), vmem_limit_bytes=_vmem_limit(blk)),
    )(xg, xg, h, h, dha, cw, vec, wabd, wxbd)


_SCALE = 1.0 / math.sqrt(DH)


_ANY = pl.BlockSpec(memory_space=pl.ANY)


class _Side:
    def __init__(self, srcs, outs, nsem, copies):
        self.srcs, self.outs, self.nsem, self.copies = list(srcs), list(outs), nsem, copies


def _pallas(body, operands, *, name, grid, in_specs, out_specs, out_shape, scratch_shapes=(), semantics,
            vmem=None, side=None):
    if side is None:
        return pl.pallas_call(
            body, name=name, grid=grid, in_specs=in_specs, out_specs=out_specs, out_shape=out_shape,
            scratch_shapes=list(scratch_shapes),
            compiler_params=pltpu.CompilerParams(dimension_semantics=semantics, vmem_limit_bytes=vmem),
        )(*operands)
    n_in, n_out, n_scr = len(in_specs), len(out_specs), len(scratch_shapes)
    ns, no = len(side.srcs), len(side.outs)

    def hosted(*refs):
        ins, refs = refs[:n_in], refs[n_in:]
        sin, refs = refs[:ns], refs[ns:]
        outs, refs = refs[:n_out], refs[n_out:]
        sout, refs = refs[:no], refs[no:]
        scr, (send, recv) = refs[:n_scr], refs[n_scr:]
        ids = [pl.program_id(a) for a in range(len(grid))]
        first = functools.reduce(jnp.logical_and, [i == 0 for i in ids])
        last = functools.reduce(jnp.logical_and, [i == g - 1 for i, g in zip(ids, grid)])

        @pl.when(first)
        def _():
            for cp in side.copies(sin, sout, send, recv):
                cp.start()

        body(*ins, *outs, *scr)

        @pl.when(last)
        def _():
            for cp in side.copies(sin, sout, send, recv):
                cp.wait()

    return pl.pallas_call(
        hosted, name=name, grid=grid, in_specs=list(in_specs) + [_ANY] * ns, out_specs=list(out_specs) + [_ANY] * no,
        out_shape=list(out_shape) + side.outs,
        scratch_shapes=list(scratch_shapes) + [pltpu.SemaphoreType.DMA((side.nsem,)), pltpu.SemaphoreType.DMA((side.nsem,))],
        compiler_params=pltpu.CompilerParams(dimension_semantics=("arbitrary",) * len(grid), vmem_limit_bytes=vmem),
    )(*operands, *side.srcs)


DA = 2 * DH
_LOG2E = math.log2(math.e)
_C2 = _SCALE * _LOG2E


def _aug_fn(ins, cs):
    q, k, fcum = ins
    g_all = fcum * _LOG2E
    lane = lax.broadcasted_iota(jnp.int32, (q.shape[0], DH), 1)
    qa, ka = [], []
    for hd in range(NH):
        g = g_all[:, hd:hd + 1]
        hi = g.astype(BF).astype(F32)
        mid = (g - hi).astype(BF).astype(F32)
        lo = ((g - hi) - mid).astype(BF).astype(F32)
        qx = jnp.where(lane == 0, hi, jnp.where(lane == 1, mid, jnp.where(lane == 2, lo,
                                                                          jnp.where(lane < 6, 1.0, 0.0))))
        kx = jnp.where(lane < 3, 1.0, jnp.where(lane == 3, -hi, jnp.where(lane == 4, -mid,
                                                                          jnp.where(lane == 5, -lo, 0.0))))
        qa += [q[:, hd * DH:(hd + 1) * DH], qx.astype(BF)]
        ka += [k[:, hd * DH:(hd + 1) * DH], kx.astype(BF)]
    return [jnp.concatenate(qa, axis=1), jnp.concatenate(ka, axis=1)], []


_KA_ONES = DH + 3
KT_ONES = 16
ATTN_CHAINS = 1


def _attn_fwd(qa, ka, qkv, T, blk=512, side=None):
    blk = min(blk, T)
    nb = T // blk
    bqs = blk // ATTN_CHAINS

    def body(q_ref, k_ref, vn_ref, o_ref, lse_ref, v_ref):
        i = pl.program_id(1)

        @pl.when(i == 0)
        def _():
            for jj in range(nb):
                v_ref[jj] = vn_ref[jj * blk:(jj + 1) * blk, :].astype(F32).T.astype(BF)

        qs = [q_ref[pl.ds(c * bqs, bqs), :] for c in range(ATTN_CHAINS)]

        def scores(j, c):
            r0 = pl.multiple_of(j * blk, blk)
            return _dot(k_ref[pl.ds(r0, blk), :], qs[c], "nt")

        def update(s, vj, carry):
            m, l, acc = carry
            m_new = jnp.maximum(m, jnp.max(s, axis=0, keepdims=True))
            alpha = jnp.exp2(m - m_new)
            p = jnp.exp2(s - m_new)
            l = alpha * l + jnp.sum(p, axis=0, keepdims=True)
            acc = alpha * acc + _dot(vj, p.astype(BF))
            return m_new, l, acc

        def step(j, carry):
            ss, st = carry
            nxt = tuple(scores(j + 1, c) for c in range(ATTN_CHAINS))
            vj = v_ref[j]
            return nxt, tuple(update(ss[c], vj, st[c]) for c in range(ATTN_CHAINS))

        init = tuple((jnp.full((1, bqs), -jnp.inf, F32), jnp.zeros((1, bqs), F32), jnp.zeros((DH, bqs), F32))
                     for _ in range(ATTN_CHAINS))
        first = tuple(scores(0, c) for c in range(ATTN_CHAINS))
        last, carry = lax.fori_loop(0, i, step, (first, init))
        rk = lax.broadcasted_iota(jnp.int32, (blk, bqs), 0)
        cq = lax.broadcasted_iota(jnp.int32, (blk, bqs), 1)
        vi = v_ref[i]
        for c in range(ATTN_CHAINS):
            s = jnp.where(cq + c * bqs >= rk, last[c], -jnp.inf)
            m, l, acc = update(s, vi, carry[c])
            o_ref[c * bqs:(c + 1) * bqs, :] = (acc / l).T.astype(o_ref.dtype)
            lse_ref[:, c * bqs:(c + 1) * bqs] = m + jnp.log(l) * _LOG2E

    vm = _nbytes((T, DA), BF) + 2 * _nbytes((T, DH), BF) + 6 * _nbytes((blk, blk), F32)
    return _pallas(
        body, (qa, ka, qkv), name="attn_fwd", grid=(NH, nb),
        in_specs=[pl.BlockSpec((blk, DA), lambda h, i: (i, h)),
                  pl.BlockSpec((T, DA), lambda h, i: (0, h)),
                  pl.BlockSpec((T, DH), lambda h, i: (0, 2 * NH + h))],
        out_specs=[pl.BlockSpec((blk, DH), lambda h, i: (i, h)),
                   pl.BlockSpec((None, None, 1, blk), lambda h, i: (h, i, 0, 0))],
        out_shape=[jax.ShapeDtypeStruct((T, D), BF), jax.ShapeDtypeStruct((NH, nb, 1, blk), F32)],
        scratch_shapes=[pltpu.VMEM((nb, DH, blk), BF)],
        semantics=("parallel", "arbitrary"), vmem=_vmem_limit(vm), side=side)


def _attn_bwd(qa, ka, qkv, do, lrow, drow, T, blk=512, side=None):
    blk = min(blk, T)
    nb = T // blk

    def body(ka_ref, v_ref, qa_ref, do_ref, l_ref, d_ref, dq_ref, dk_ref, dv_ref, dfs_ref, dft_ref, dq_s):
        j = pl.program_id(1)

        @pl.when(j == 0)
        def _():
            dq_s[...] = jnp.zeros_like(dq_s)

        row = lax.broadcasted_iota(jnp.int32, (DH + KT_ONES, blk), 0)
        dq_scale = jnp.where(row < DH, _SCALE, 1.0)

        kaj = ka_ref[...]
        ktj = jnp.concatenate([kaj[:, :DH].astype(F32).T.astype(BF), jnp.ones((KT_ONES, blk), BF)], axis=0)
        vj = v_ref[...]

        def step(i, carry, diag):
            dka, dv = carry
            r0 = pl.multiple_of(i * blk, blk)
            qi = qa_ref[pl.ds(r0, blk), :]
            doi = do_ref[pl.ds(r0, blk), :]
            st = _dot(kaj, qi, "nt") - l_ref[i]
            if diag:
                rk = lax.broadcasted_iota(jnp.int32, (blk, blk), 0)
                cq = lax.broadcasted_iota(jnp.int32, (blk, blk), 1)
                st = jnp.where(cq >= rk, st, -jnp.inf)
            pt = jnp.exp2(st)
            dv = dv + _dot(pt.astype(BF), doi)
            dpt = _dot(vj, doi, "nt")
            dst = pt * (dpt - d_ref[i])
            dsb = dst.astype(BF)
            dka = dka + _dot(dsb, qi)
            dq_s[i] += _dot(ktj, dsb) * dq_scale
            return dka, dv

        init = (jnp.zeros((blk, DA), F32), jnp.zeros((blk, DH), F32))
        carry = step(j, init, True)
        dka, dv = lax.fori_loop(j + 1, nb, lambda i, c: step(i, c, False), carry)
        dk_ref[...] = (dka[:, :DH] * (1.0 / _LOG2E)).astype(dk_ref.dtype)
        dv_ref[...] = dv.astype(dv_ref.dtype)
        dfs_ref[...] = dka[:, DH:].T[_KA_ONES - DH:_KA_ONES - DH + 1, :]

        @pl.when(j == nb - 1)
        def _():
            for ii in range(nb):
                t = dq_s[ii]
                dq_ref[ii * blk:(ii + 1) * blk, :] = t[:DH].T.astype(dq_ref.dtype)
                dft_ref[ii] = t[DH:DH + 1]

    rowv = pl.BlockSpec((None, nb, 1, blk), lambda h, j: (h, 0, 0, 0))
    vm = (_nbytes((T, DA), BF) + 2 * _nbytes((T, DH), BF) + _nbytes((T, DH + KT_ONES), F32)
          + 8 * _nbytes((blk, blk), F32))
    return _pallas(
        body, (ka, qkv, qa, do, lrow, drow), name="attn_bwd", grid=(NH, nb),
        in_specs=[pl.BlockSpec((blk, DA), lambda h, j: (j, h)),
                  pl.BlockSpec((blk, DH), lambda h, j: (j, 2 * NH + h)),
                  pl.BlockSpec((T, DA), lambda h, j: (0, h)),
                  pl.BlockSpec((T, DH), lambda h, j: (0, h)),
                  rowv, rowv],
        out_specs=[pl.BlockSpec((T, DH), lambda h, j: (0, h)),
                   pl.BlockSpec((blk, DH), lambda h, j: (j, h)),
                   pl.BlockSpec((blk, DH), lambda h, j: (j, h)),
                   pl.BlockSpec((None, None, 1, blk), lambda h, j: (h, j, 0, 0)), rowv],
        out_shape=[jax.ShapeDtypeStruct((T, D), BF), jax.ShapeDtypeStruct((T, D), BF),
                   jax.ShapeDtypeStruct((T, D), BF), jax.ShapeDtypeStruct((NH, nb, 1, blk), F32),
                   jax.ShapeDtypeStruct((NH, nb, 1, blk), F32)],
        scratch_shapes=[pltpu.VMEM((nb, DH + KT_ONES, blk), F32)],
        semantics=("parallel", "arbitrary"), vmem=_vmem_limit(vm), side=side)


def _norm_fn(ins, cs):
    x, = ins
    g, = cs
    r = lax.rsqrt(jnp.mean(x * x, axis=-1, keepdims=True) + EPS)
    return [x * r * g], []


def _norm_bwd_fn(ins, cs):
    x, dy, dres = ins
    g, = cs
    r = lax.rsqrt(jnp.mean(x * x, axis=-1, keepdims=True) + EPS)
    xh = x * r
    dxh = dy * g
    dx = dres + r * (dxh - xh * jnp.mean(dxh * xh, axis=-1, keepdims=True))
    return [dx], [_colsum(dy * xh)]


def _final_fn(ins, cs):
    x2, tgt = ins
    g, = cs
    r = lax.rsqrt(jnp.mean(x2 * x2, axis=-1, keepdims=True) + EPS)
    xh = x2 * r
    e = xh * g - tgt
    dy = e * (1.0 / D)
    dxh = dy * g
    dx2 = r * (dxh - xh * jnp.mean(dxh * xh, axis=-1, keepdims=True))
    return [dx2], [_colsum(0.5 * e * e * (1.0 / D)), _colsum(dy * xh)]


def _z_fn(ins, cs):
    g, h = [v.astype(F32) for v in ins]
    return [_gelu(g) * h], []


def _mix_fn(ins, cs):
    gates, ya, yb = [v.astype(F32) for v in ins]
    return [_sig(gates[:, :D]) * ya + _sig(gates[:, D:]) * yb], []


def _mix_bwd_fn(ins, cs):
    dmix, gates, ya, yb = [v.astype(F32) for v in ins]
    ga = _sig(gates[:, :D])
    gb = _sig(gates[:, D:])
    dgates = jnp.concatenate([dmix * ya * ga * (1.0 - ga), dmix * yb * gb * (1.0 - gb)], axis=1)
    return [dmix * ga, dmix * gb, dgates], []


def _z_bwd_fn(ins, cs):
    dz, g, h = [v.astype(F32) for v in ins]
    return [dz * _gelu(g), dz * h * _gelu_grad(g)], []


def _delta_fn(ins, cs):
    do, o = ins
    p = do.astype(F32) * o.astype(F32)
    lane = lax.broadcasted_iota(jnp.int32, (p.shape[0], DH), 1)
    out = jnp.zeros((p.shape[0], DH), F32)
    for hd in range(NH):
        s = jnp.sum(p[:, hd * DH:(hd + 1) * DH], axis=1, keepdims=True)
        out = jnp.where(lane == hd, s, out)
    return [out], []


def _du_all(pieces, win, T, tm=256, side=None):
    tm = min(tm, T)
    n = len(pieces)

    def body(*refs):
        w_ref, o_ref = refs[n], refs[n + 1]
        acc = None
        for (a, off), a_ref in zip(pieces, refs[:n]):
            d = _dot(a_ref[...].astype(BF), w_ref[:, off:off + a.shape[1]], "nt")
            acc = d if acc is None else acc + d
        o_ref[...] = acc

    vm = (sum(_nbytes((tm, a.shape[1]), a.dtype) for a, _ in pieces) + _nbytes(win.shape, win.dtype)
          + 2 * _nbytes((tm, D), F32))
    return _pallas(
        body, tuple(a for a, _ in pieces) + (win,), name="du_all", grid=(T // tm,),
        in_specs=[pl.BlockSpec((tm, a.shape[1]), lambda i: (i, 0)) for a, _ in pieces]
        + [pl.BlockSpec(win.shape, lambda i: (0, 0))],
        out_specs=[pl.BlockSpec((tm, D), lambda i: (i, 0))],
        out_shape=[jax.ShapeDtypeStruct((T, D), F32)],
        semantics=("arbitrary",), vmem=int(min(VMEM_CAP, 2 * vm + (4 << 20))), side=side)


def _local_step(x, tgt, w, T, blk=1024, dist=None):
    blk = min(blk, T)
    nb = T // blk
    win = w["win"]

    u, = _ew(_norm_fn, T, 512, [(x, D, 0)], [w["g_mix"]], [(D, BF)], [], name="norm_mix")
    xg = _mm(u, win, "nn", T, 2 * D, D, name="proj_lru", out_dtype=BF)
    qkv = _mm(u, win, "nn", T, 3 * D, D, name="proj_qkv", out_dtype=BF, b_off=(0, 2),
              epi=lambda acc: acc * jnp.where(pl.program_id(1) == 0, _C2, 1.0))
    gates = _mm(u, win, "nn", T, 2 * D, D, name="proj_gates", b_off=(0, 5), out_dtype=BF)
    fl = _mm(u, win, "nn", T, DH, D, name="proj_f", tn=DH, b_off=(0, 7 * D // DH))
    fcum = _fgate_fwd(fl, w["fb"], T)
    qa, ka = _ew(_aug_fn, T, 256, [(qkv, D, 0), (qkv, D, 1), (fcum, DH, 0)], [],
                 [(NH * DA, BF), (NH * DA, BF)], [], name="attn_augment")

    h = _lru_fwd(xg, w["cw"], w["vec"], w["wabd"], w["wxbd"], T)
    ob, lse, *landed = _attn_fwd(qa, ka, qkv, T, blk, side=dist.weights_side() if dist else None)
    if dist:
        w = dict(w, **dist.weights_landed(landed))
    z, = _ew(_z_fn, T, 512, [(xg, D, 1), (h, D, 0)], [], [(D, BF)], [], name="lru_gelu")
    ya = _mm(z, w["wa"], "nn", T, D, D, name="branch_a", out_dtype=BF)
    yb = _mm(ob, w["wb"], "nn", T, D, D, name="branch_b", out_dtype=BF)
    mix, = _ew(_mix_fn, T, 256, [(gates, 2 * D, 0), (ya, D, 0), (yb, D, 0)], [], [(D, BF)], [], name="mix")
    x1 = _mm(mix, w["wout"], "nn", T, D, D, name="out_proj", add=x)
    m, = _ew(_norm_fn, T, 512, [(x1, D, 0)], [w["g_mlp"]], [(D, BF)], [], name="norm_mlp")
    hh = _mm(m, w["wup"], "nn", T, FF, D, name="mlp_up", out_dtype=BF,
             epi=lambda acc: jnp.square(jnp.maximum(acc, 0.0)))
    x2 = _mm(hh, w["wdown"], "nn", T, D, FF, name="mlp_down", add=x1, tk=2048)
    dx2, loss_vec, dg_fin = _ew(_final_fn, T, 256, [(x2, D, 0), (tgt, D, 0)], [w["g_fin"]], [(D, F32)],
                                [(1, D), (1, D)], name="final_norm_loss")

    dhpre = _mm(dx2, w["wdown"], "nt", T, FF, D, name="mlp_down_bwd", out_dtype=BF,
                epi=lambda acc, h2: acc * (2.0 * jnp.sqrt(h2.astype(F32))), epi_ins=[hh])
    dwdown = _mm(hh, dx2, "tn", FF, D, T, name="dw_down", out_dtype=BF)
    dwup = _mm(m, dhpre, "tn", D, FF, T, name="dw_up", out_dtype=BF)
    dm = _mm(dhpre, w["wup"], "nt", T, D, FF, name="mlp_up_bwd", tk=2048)
    dx1, dg_mlp = _ew(_norm_bwd_fn, T, 256, [(x1, D, 0), (dm, D, 0), (dx2, D, 0)], [w["g_mlp"]], [(D, F32)],
                      [(1, D)], name="norm_mlp_bwd")

    dmix = _mm(dx1, w["wout"], "nt", T, D, D, name="out_proj_bwd", out_dtype=BF)
    dwout = _mm(mix, dx1, "tn", D, D, T, name="dw_out", out_dtype=BF)
    dya, dyb, dgates = _ew(_mix_bwd_fn, T, 256, [(dmix, D, 0), (gates, 2 * D, 0), (ya, D, 0), (yb, D, 0)], [],
                           [(D, BF), (D, BF), (2 * D, BF)], [], name="mix_bwd")
    dob = _mm(dyb, w["wb"], "nt", T, D, D, name="branch_b_bwd", out_dtype=BF)
    dwb = _mm(ob, dyb, "tn", D, D, T, name="dw_b", out_dtype=BF)
    dz = _mm(dya, w["wa"], "nt", T, D, D, name="branch_a_bwd", out_dtype=BF)
    dwa = _mm(z, dya, "tn", D, D, T, name="dw_a", out_dtype=BF)
    dha, dglru = _ew(_z_bwd_fn, T, 256, [(dz, D, 0), (xg, D, 1), (h, D, 0)], [], [(D, F32), (D, BF)], [],
                     name="lru_gelu_bwd")

    delta, = _ew(_delta_fn, T, 512, [(dob, D, 0), (ob, D, 0)], [], [(DH, F32)], [], name="attn_delta")
    drow = delta[:, :NH].T.reshape(NH, nb, 1, blk)
    big = dict(w_branch_a=dwa, w_branch_b=dwb, w_out=dwout, w_up=dwup, w_down=dwdown)
    side = dist.grads_side(big) if dist else None
    dq, dk, dv, dfs, dft, *landed = _attn_bwd(qa, ka, qkv, dob, lse, drow, T, blk, side=side)
    if dist:
        big = dist.grads_landed(side, landed)
    dfcum = jnp.pad((dft - dfs).reshape(NH, T).T, ((0, 0), (0, DH - NH)))
    dfl, dfb = _fgate_bwd(dfcum, fl, w["fb"], T)

    dxl, dwabd, dwxbd, lacc = _lru_bwd(xg, h, dha, w["cw"], w["vec"], w["wabd"], w["wxbd"], T)

    dproj = ((dxl, 0), (dglru, D), (dq, 2 * D), (dk, 3 * D), (dv, 4 * D), (dgates, 5 * D), (dfl, 7 * D))
    pieces = [_mm(u, p, "tn", D, p.shape[1], T, name="dw_in_%d" % n, out_dtype=BF)
              for n, (p, _) in enumerate(dproj)]
    pieces[-1] = pieces[-1][:, :NH]
    dwin = dict(w_in=jnp.concatenate(pieces, axis=1))
    side = dist.grads_side(dwin) if dist else None
    du, *landed = _du_all(dproj, win, T, side=side)
    big.update(dist.grads_landed(side, landed) if dist else dwin)
    dx, dg_mix = _ew(_norm_bwd_fn, T, 256, [(x, D, 0), (du, D, 0), (dx1, D, 0)], [w["g_mix"]], [(D, F32)],
                     [(1, D)], name="norm_mix_bwd")

    return dict(dx=dx, big=big, dwabd=dwabd, dwxbd=dwxbd, lacc=lacc, dfb=dfb, dg_mix=dg_mix, dg_mlp=dg_mlp,
                dg_fin=dg_fin, loss_vec=loss_vec)


def _block_diag(w):
    per = BD // LRU_BW
    w4 = w.reshape(NBD, per, LRU_BW, LRU_BW)
    out = jnp.zeros((NBD, per, LRU_BW, per, LRU_BW), w.dtype)
    for b in range(per):
        out = out.at[:, b, :, b, :].set(w4[:, b])
    return out.reshape(NBD, BD, BD)


def _block_diag_extract(wbd):
    per = BD // LRU_BW
    w5 = wbd.reshape(NBD, per, LRU_BW, per, LRU_BW)
    return jnp.stack([w5[:, b, :, b, :] for b in range(per)], axis=1).reshape(LRU_BLOCKS, LRU_BW, LRU_BW)


_ANY = pl.BlockSpec(memory_space=pl.ANY)


def _place():
    x, y, c = lax.axis_index("x"), lax.axis_index("y"), lax.axis_index("c")
    chips = [(1 - x, y), (x, 1 - y), (1 - x, 1 - y)]
    return x, y, c, chips


def _allgather_shards(shards):
    n = len(shards)

    def body(*refs):
        ins, outs = refs[:n], refs[n:2 * n]
        send_sems, recv_sems = refs[2 * n:]
        x, y, c, chips = _place()
        me = 2 * x + y
        sibling = (x, y, 1 - c)

        def remote(p, k, src, dst, to):
            return pltpu.make_async_remote_copy(src_ref=src, dst_ref=dst, send_sem=send_sems.at[p, k],
                                                recv_sem=recv_sems.at[p, k], device_id=to, device_id_type=MESH)

        sent = []
        for p in range(n):
            for k, chip in enumerate(chips):
                cp = remote(p, k, ins[p].at[c], outs[p].at[me, c], (chip[0], chip[1], c))
                cp.start()
                sent.append(cp)
        for p in range(n):
            for k, chip in enumerate(chips):
                half = outs[p].at[2 * chip[0] + chip[1], c]
                remote(p, k, half, half, sibling).wait_recv()
                fwd = remote(p, 3 + k, half, half, sibling)
                fwd.start()
                sent.append(fwd)
        for p in range(n):
            for k, chip in enumerate(chips):
                half = outs[p].at[2 * chip[0] + chip[1], 1 - c]
                remote(p, 3 + k, half, half, sibling).wait_recv()
        for cp in sent:
            cp.wait_send()

    gathered = pl.pallas_call(
        body, name="allgather_weights",
        in_specs=[_ANY] * n, out_specs=[_ANY] * n,
        out_shape=[jax.ShapeDtypeStruct((NCHIP,) + s.shape, s.dtype) for s in shards],
        scratch_shapes=[pltpu.SemaphoreType.DMA((n, 6)), pltpu.SemaphoreType.DMA((n, 6))],
    )(*shards)
    me = 2 * lax.axis_index("x") + lax.axis_index("y")
    return [lax.dynamic_update_index_in_dim(g, s, me, 0) for g, s in zip(gathered, shards)]


_LATE =["w_branch_a", "w_branch_b", "w_out", "w_up", "w_down"]
_COLUMN_CUT = ("w_in", "w_up")
N_PEERS = 7


def _shard_major(name, g):
    s = _columns_to_shards(g) if name in _COLUMN_CUT else g.reshape(NCHIP, g.shape[0] // NCHIP, g.shape[1])
    return s.reshape(NCHIP, 2, s.shape[1] // 2, s.shape[2])


class _Exchanges:
    def __init__(self, shards):
        self.shards = shards

    def weights_side(self):
        srcs = [self.shards[n] for n in _LATE]

        def copies(sin, sout, send, recv):
            x, y, c, chips = _place()
            return [pltpu.make_async_remote_copy(
                src_ref=sin[p], dst_ref=sout[p].at[2 * x + y], send_sem=send.at[3 * p + k], recv_sem=recv.at[3 * p + k],
                device_id=(chip[0], chip[1], c), device_id_type=MESH)
                for p in range(len(sin)) for k, chip in enumerate(chips)]

        return _Side(srcs, [jax.ShapeDtypeStruct((NCHIP,) + s.shape, s.dtype) for s in srcs], 3 * len(srcs), copies)

    def weights_landed(self, landed):
        me = 2 * lax.axis_index("x") + lax.axis_index("y")
        full = {n: lax.dynamic_update_index_in_dim(g, self.shards[n], me, 0) for n, g in zip(_LATE, landed)}
        return dict(wa=full["w_branch_a"].reshape(D, D), wb=full["w_branch_b"].reshape(D, D),
                    wout=full["w_out"].reshape(D, D), wup=_shards_to_columns(full["w_up"]),
                    wdown=full["w_down"].reshape(FF, D))

    def grads_side(self, grads):
        side_names = list(grads)
        srcs = [_shard_major(n, grads[n]) for n in side_names]

        def copies(sin, sout, send, recv):
            x, y, c, chips = _place()
            peers = [(x, y, 1 - c)] + [(cx, cy, c) for cx, cy in chips] + [(cx, cy, 1 - c) for cx, cy in chips]
            return [pltpu.make_async_remote_copy(
                src_ref=sin[p].at[2 * px + py, pc], dst_ref=sout[p].at[s], send_sem=send.at[N_PEERS * p + s],
                recv_sem=recv.at[N_PEERS * p + s], device_id=(px, py, pc), device_id_type=MESH)
                for p in range(len(sin)) for s, (px, py, pc) in enumerate(peers)]

        side = _Side(srcs, [jax.ShapeDtypeStruct((N_PEERS,) + s.shape[2:], s.dtype) for s in srcs],
                     N_PEERS * len(srcs), copies)
        side.names = side_names
        return side

    def grads_landed(self, side, landed):
        return {n: (own, got) for n, own, got in zip(side.names, side.srcs, landed)}


def _add8(g, got, me, c, name):
    _, _, half, cols = g.shape
    th = _row_tile(half, 2 * cols)

    def body(me_ref, c_ref, g_ref, r_ref, o_ref):
        acc = g_ref[...].astype(F32)
        for s in range(N_PEERS):
            acc = acc + r_ref[s].astype(F32)
        o_ref[...] = acc

    return pl.pallas_call(
        body, name=name,
        grid_spec=pltpu.PrefetchScalarGridSpec(
            num_scalar_prefetch=2, grid=(half // th,),
            in_specs=[pl.BlockSpec((None, None, th, cols), lambda i, me_ref, c_ref: (me_ref[0], c_ref[0], i, 0)),
                      pl.BlockSpec((N_PEERS, th, cols), lambda i, me_ref, c_ref: (0, i, 0))],
            out_specs=pl.BlockSpec((th, cols), lambda i, me_ref, c_ref: (i, 0))),
        out_shape=jax.ShapeDtypeStruct((half, cols), F32),
    )(me, c, g, got)


def _share_halves(halves):
    n = len(halves)

    def body(*refs):
        ins, outs = refs[:n], refs[n:2 * n]
        send_sems, recv_sems = refs[2 * n:]
        x, y, c, _ = _place()
        sibling = (x, y, 1 - c)
        copies = []
        for p in range(n):
            cp = pltpu.make_async_remote_copy(src_ref=ins[p], dst_ref=outs[p], send_sem=send_sems.at[p],
                                              recv_sem=recv_sems.at[p], device_id=sibling, device_id_type=MESH)
            cp.start()
            copies.append(cp)
        for cp in copies:
            cp.wait()

    return pl.pallas_call(
        body, name="reduce_share_halves",
        in_specs=[_ANY] * n, out_specs=[_ANY] * n,
        out_shape=[jax.ShapeDtypeStruct(h.shape, h.dtype) for h in halves],
        scratch_shapes=[pltpu.SemaphoreType.DMA((n,)), pltpu.SemaphoreType.DMA((n,))],
    )(*halves)


def _row_tile(half, cols):
    th = max(SLAB, min(half, (1 << 18) // cols // SLAB * SLAB))
    while half % th:
        th -= SLAB
    return th


N_DEV = 8
SMALL_ROWS = 208


def _allreduce_small(pack):
    def body(x_ref, out_ref, gbuf, send_sems, recv_sems, local_sem):
        x, y, c, chips = _place()
        me, sibling = (x, y, c), (x, y, 1 - c)

        def rows(px, py, pc):
            return gbuf.at[4 * px + 2 * py + pc]

        def copy(k, block, to, src=None):
            return pltpu.make_async_remote_copy(
                src_ref=rows(*block) if src is None else src, dst_ref=rows(*block),
                send_sem=send_sems.at[k], recv_sem=recv_sems.at[k], device_id=to, device_id_type=MESH)

        mine = pltpu.make_async_copy(x_ref, rows(*me), local_sem)
        mine.start()
        first = [copy(0, me, sibling, src=x_ref)]
        first += [copy(1 + j, me, (chip[0], chip[1], c), src=x_ref) for j, chip in enumerate(chips)]
        for cp in first:
            cp.start()
        passed = [copy(4 + j, (chip[0], chip[1], c), sibling) for j, chip in enumerate(chips)]
        for j, chip in enumerate(chips):
            copy(1 + j, (chip[0], chip[1], c), me).wait_recv()
            passed[j].start()
        copy(0, sibling, me).wait_recv()
        for j, chip in enumerate(chips):
            copy(4 + j, (chip[0], chip[1], 1 - c), me).wait_recv()
        for cp in first + passed:
            cp.wait_send()
        mine.wait()
        acc = gbuf[0]
        for d in range(1, N_DEV):
            acc = acc + gbuf[d]
        out_ref[...] = acc

    return pl.pallas_call(
        body, name="allreduce_small",
        in_specs=[pl.BlockSpec(memory_space=pltpu.VMEM)],
        out_specs=pl.BlockSpec(memory_space=pltpu.VMEM),
        out_shape=jax.ShapeDtypeStruct((SMALL_ROWS, D), F32),
        scratch_shapes=[pltpu.VMEM((N_DEV, SMALL_ROWS, D), F32), pltpu.SemaphoreType.DMA((7,)),
                        pltpu.SemaphoreType.DMA((7,)), pltpu.SemaphoreType.DMA],
    )(pack)


def _adamw(w, g, m, v, name):
    rows, cols = w.shape

    def body(w_ref, g_ref, m_ref, v_ref, d_ref, mo_ref, vo_ref):
        gv = g_ref[...]
        mn = ADAM_B1 * m_ref[...] + (1.0 - ADAM_B1) * gv
        vn = ADAM_B2 * v_ref[...] + (1.0 - ADAM_B2) * (gv * gv)
        m_hat = mn / (1.0 - ADAM_B1 ** ADAM_STEP)
        v_hat = vn / (1.0 - ADAM_B2 ** ADAM_STEP)
        d_ref[...] = -ADAM_LR * (m_hat / (jnp.sqrt(v_hat) + ADAM_EPS) + ADAM_WD * w_ref[...])
        mo_ref[...] = mn
        vo_ref[...] = vn

    if rows % SLAB:
        spec, steps = pl.BlockSpec((rows, DH), lambda i: (0, i)), cols // DH
    else:
        th = _row_tile(rows, cols)
        spec, steps = pl.BlockSpec((th, cols), lambda i: (i, 0)), rows // th
    return pl.pallas_call(
        body, name=name, grid=(steps,),
        in_specs=[spec] * 4, out_specs=[spec] * 3,
        out_shape=[jax.ShapeDtypeStruct((rows, cols), F32)] * 3,
        compiler_params=pltpu.CompilerParams(dimension_semantics=("parallel",)),
    )(w, g, m, v)


_SMALL = ["norm_mix_g", "norm_mlp_g", "norm_final_g", "conv_b", "lru_ba", "lru_bx", "lru_lambda"]
_ROW_FB, _ROW_CW, _ROW_WA, _ROW_WX, _ROW_LOSS = 56, 64, 72, 136, 200


def _pack_small(vals, col0):
    def slab(a):
        return jnp.pad(a, ((0, -a.shape[0] % SLAB), (0, D - a.shape[1])))

    rows = [slab(vals[n].reshape(1, D)) for n in _SMALL]
    rows.append(slab(vals["forget_b"].reshape(1, NH)))
    if vals["conv_w"].shape[1] == D:
        rows.append(slab(vals["conv_w"]))
    else:
        rows.append(slab(lax.dynamic_update_slice(jnp.zeros((CONV, D), F32), vals["conv_w"], (0, col0))))
    rows.append(vals["lru_wa"].reshape(LRU_BLOCKS * LRU_BW * LRU_BW // D, D))
    rows.append(vals["lru_wx"].reshape(LRU_BLOCKS * LRU_BW * LRU_BW // D, D))
    rows.append(slab(vals["loss"]) if "loss" in vals else jnp.zeros((SLAB, D), F32))
    return jnp.concatenate(rows, axis=0)


def _unpack_small(pack, col0):
    out = {n: pack[SLAB * i] for i, n in enumerate(_SMALL)}
    out["forget_b"] = pack[_ROW_FB, :NH]
    out["conv_w"] = lax.dynamic_slice(pack[_ROW_CW:_ROW_CW + CONV], (0, col0), (CONV, D // NCHIP))
    out["lru_wa"] = pack[_ROW_WA:_ROW_WX].reshape(LRU_BLOCKS, LRU_BW, LRU_BW)
    out["lru_wx"] = pack[_ROW_WX:_ROW_LOSS].reshape(LRU_BLOCKS, LRU_BW, LRU_BW)
    return out


_WEIGHTS = ["norm_mix_g", "w_in", "conv_w", "conv_b", "lru_wa", "lru_ba", "lru_wx", "lru_bx", "lru_lambda",
            "forget_b", "w_branch_a", "w_branch_b", "w_out", "norm_mlp_g", "w_up", "w_down", "norm_final_g"]
_BIG = ["w_in", "w_branch_a", "w_branch_b", "w_out", "w_up", "w_down"]


def _halves(a):
    return a.reshape(2, a.shape[0] // 2, a.shape[1])


def _columns_to_shards(a):
    rows, cols = a.shape[0], a.shape[1] // NCHIP
    return jnp.transpose(a.reshape(rows, NCHIP, cols), (1, 0, 2))


def _shards_to_columns(a):
    n, rows, cols = a.shape
    return jnp.transpose(a, (1, 0, 2)).reshape(rows, n * cols)


def kernel(x, norm_mix_g, w_in, conv_w, conv_b, lru_wa, lru_ba, lru_wx, lru_bx, lru_lambda, forget_b, w_branch_a, w_branch_b, w_out, norm_mlp_g, w_up, w_down, norm_final_g, loss_target, m_norm_mix_g, m_w_in, m_conv_w, m_conv_b, m_lru_wa, m_lru_ba, m_lru_wx, m_lru_bx, m_lru_lambda, m_forget_b, m_w_branch_a, m_w_branch_b, m_w_out, m_norm_mlp_g, m_w_up, m_w_down, m_norm_final_g, v_norm_mix_g, v_w_in, v_conv_w, v_conv_b, v_lru_wa, v_lru_ba, v_lru_wx, v_lru_bx, v_lru_lambda, v_forget_b, v_w_branch_a, v_w_branch_b, v_w_out, v_norm_mlp_g, v_w_up, v_w_down, v_norm_final_g):
    args = dict(locals())
    wts = {n: args[n] for n in _WEIGHTS}
    mom = {n: args["m_" + n] for n in _WEIGHTS}
    var = {n: args["v_" + n] for n in _WEIGHTS}
    T = x.shape[1]
    xi, yi, ci = lax.axis_index("x"), lax.axis_index("y"), lax.axis_index("c")
    me = 2 * xi + yi
    c1 = jnp.reshape(ci, (1,)).astype(jnp.int32)
    me1 = jnp.reshape(me, (1,)).astype(jnp.int32)
    col0 = me * (D // NCHIP)

    cw_pad = jnp.pad(conv_w, ((0, 4 * SLAB - CONV), (0, 0)))
    g_in, g_cw = _allgather_shards([_halves(w_in.astype(BF)), _halves(cw_pad)])
    cin = DIN // NCHIP
    win = _shards_to_columns(g_in.reshape(NCHIP, D, cin))
    w = dict(
        win=jnp.pad(win, ((0, 0), (0, DINP - DIN))),
        cw=_shards_to_columns(g_cw.reshape(NCHIP, 4 * SLAB, D // NCHIP)[:, :CONV]),
        vec=jnp.concatenate([conv_b[None], lru_ba[None], lru_bx[None], lru_lambda[None],
                             jnp.zeros((SLAB - 4, D), F32)], axis=0),
        fb=jnp.pad(forget_b[None], ((0, 0), (0, DH - NH))),
        wabd=_block_diag(lru_wa).astype(BF), wxbd=_block_diag(lru_wx).astype(BF),
        g_mix=norm_mix_g[None], g_mlp=norm_mlp_g[None], g_fin=norm_final_g[None])

    r = _local_step(x[0], loss_target[0], w, T, dist=_Exchanges({n: wts[n].astype(BF) for n in _LATE}))

    halves = [_add8(*r["big"][n], me1, c1, "add8_" + n) for n in _BIG]
    theirs = _share_halves(halves)
    low = ci == 0
    gsum = {n: jnp.concatenate([jnp.where(low, h, t), jnp.where(low, t, h)], axis=0)
            for n, h, t in zip(_BIG, halves, theirs)}
    lacc = r["lacc"]
    small = dict(norm_mix_g=r["dg_mix"], norm_mlp_g=r["dg_mlp"], norm_final_g=r["dg_fin"], conv_b=lacc[3],
                 lru_ba=lacc[0], lru_bx=lacc[1], lru_lambda=lacc[2], forget_b=r["dfb"][0, :NH],
                 conv_w=lacc[4:4 + CONV], lru_wa=_block_diag_extract(r["dwabd"]),
                 lru_wx=_block_diag_extract(r["dwxbd"]), loss=r["loss_vec"])
    gpack = _allreduce_small(_pack_small(small, col0))
    loss = jnp.sum(gpack[_ROW_LOSS])

    grads, delta, new_m, new_v = {}, {}, {}, {}
    for n in _BIG:
        if n == "w_in":
            gt = gsum[n].T
            grads[n] = gt.T
            delta[n], new_m[n], new_v[n] = [a.T for a in _adamw(wts[n].T, gt, mom[n].T, var[n].T, "adamw_" + n)]
        else:
            grads[n] = gsum[n]
            delta[n], new_m[n], new_v[n] = _adamw(wts[n], gsum[n], mom[n], var[n], "adamw_" + n)
    dp, mp, vp = _adamw(_pack_small(wts, col0), gpack, _pack_small(mom, col0), _pack_small(var, col0), "adamw_small")
    for dst, pack in ((grads, gpack), (delta, dp), (new_m, mp), (new_v, vp)):
        dst.update(_unpack_small(pack, col0))
    return (loss, r["dx"][None], *[grads[n] for n in _WEIGHTS], *[delta[n] for n in _WEIGHTS],
            *[new_m[n] for n in _WEIGHTS], *[new_v[n] for n in _WEIGHTS])
```

```python
import functools
import math

import jax
import jax.numpy as jnp
import numpy as np
from jax import lax
from jax.experimental import pallas as pl
from jax.experimental.pallas import tpu as pltpu

F32 = jnp.float32
BF = jnp.bfloat16

D = 1024
NH = 8
DH = 128
FF = 4096
CONV = 4
LRU_BLOCKS = 16
LRU_BW = 64
BD = 256
NBD = D // BD
LRU_C = 8.0
EPS = 1e-6
DIN = 7176
DINP = 7296
NCHIP = 4
SLAB = 8
VMEM_CAP = 60 * 1024 * 1024

ADAM_LR = 0.001
ADAM_B1 = 0.9
ADAM_B2 = 0.999
ADAM_EPS = 1e-08
ADAM_WD = 0.01
ADAM_STEP = 10

MESH = pl.DeviceIdType.MESH


def _vmem_limit(nbytes):
    return int(min(VMEM_CAP, max(32 * 1024 * 1024, 3 * nbytes)))


def _nbytes(shape, dtype):
    return int(np.prod(shape)) * jnp.dtype(dtype).itemsize


def _sig(x):
    return 0.5 * jnp.tanh(0.5 * x) + 0.5


def _log1p(u):
    w = 1.0 + u
    return jnp.where(w == 1.0, u, jnp.log(w) * (u / (w - 1.0)))


def _one_minus_sq(a, la):
    return jnp.tanh(-la) * (1.0 + a * a)


def _softplus(z):
    return jnp.maximum(z, 0.0) + _log1p(jnp.exp(-jnp.abs(z)))


_GELU_C = math.sqrt(2.0 / math.pi)


def _gelu(x):
    return 0.5 * x * (1.0 + jnp.tanh(_GELU_C * (x + 0.044715 * x * x * x)))


def _gelu_grad(x):
    t = jnp.tanh(_GELU_C * (x + 0.044715 * x * x * x))
    return 0.5 * (1.0 + t) + 0.5 * x * (1.0 - t * t) * _GELU_C * (1.0 + 3.0 * 0.044715 * x * x)


def _shift_down(x, d, prev8):
    n = x.shape[0]
    row8 = lax.broadcasted_iota(jnp.int32, (SLAB, x.shape[1]), 0)
    y = pltpu.roll(x, d, 0)
    top = jnp.where(row8 < d, pltpu.roll(prev8, d, 0), y[0:SLAB])
    if n == SLAB:
        return top
    return jnp.concatenate([top, y[SLAB:]], axis=0)


def _shift_up(x, d, next8):
    n = x.shape[0]
    row8 = lax.broadcasted_iota(jnp.int32, (SLAB, x.shape[1]), 0)
    y = pltpu.roll(x, n - d, 0)
    bottom = jnp.where(row8 >= SLAB - d, pltpu.roll(next8, SLAB - d, 0), y[n - SLAB:])
    if n == SLAB:
        return bottom
    return jnp.concatenate([y[:n - SLAB], bottom], axis=0)


def _slab_scan_fwd(a, b):
    row = lax.broadcasted_iota(jnp.int32, a.shape, 0)
    for k in (1, 2, 4):
        a_s = pltpu.roll(a, k, 0)
        b_s = pltpu.roll(b, k, 0)
        m = row >= k
        b = jnp.where(m, a * b_s + b, b)
        a = jnp.where(m, a * a_s, a)
    return a, b


def _slab_scan_bwd(a, b):
    row = lax.broadcasted_iota(jnp.int32, a.shape, 0)
    for k in (1, 2, 4):
        a_s = pltpu.roll(a, SLAB - k, 0)
        b_s = pltpu.roll(b, SLAB - k, 0)
        m = row < SLAB - k
        b = jnp.where(m, a * b_s + b, b)
        a = jnp.where(m, a * a_s, a)
    return a, b


_DIMS = {"nn": (((1,), (0,)), ((), ())), "nt": (((1,), (1,)), ((), ())), "tn": (((0,), (0,)), ((), ()))}


def _dot(a, b, mode="nn"):
    return lax.dot_general(a, b, _DIMS[mode], preferred_element_type=F32)


def _mm(a, b, mode, M, N, K, *, name, out_dtype=F32, tm=1024, tn=1024, tk=1024,
        a_off=(0, 0), b_off=(0, 0), add=None, epi=None, epi_ins=()):
    tm, tn, tk = min(tm, M), min(tn, N), min(tk, K)
    nk = K // tk
    grid = (M // tm, N // tn, nk)
    if mode == "nn":
        a_spec = pl.BlockSpec((tm, tk), lambda i, j, k: (i + a_off[0], k + a_off[1]))
        b_spec = pl.BlockSpec((tk, tn), lambda i, j, k: (k + b_off[0], j + b_off[1]))
    elif mode == "nt":
        a_spec = pl.BlockSpec((tm, tk), lambda i, j, k: (i + a_off[0], k + a_off[1]))
        b_spec = pl.BlockSpec((tn, tk), lambda i, j, k: (j + b_off[0], k + b_off[1]))
    else:
        a_spec = pl.BlockSpec((tk, tm), lambda i, j, k: (k + a_off[0], i + a_off[1]))
        b_spec = pl.BlockSpec((tk, tn), lambda i, j, k: (k + b_off[0], j + b_off[1]))
    o_spec = pl.BlockSpec((tm, tn), lambda i, j, k: (i, j))
    extra = ([add] if add is not None else []) + list(epi_ins)
    n_extra = len(extra)
    has_add = add is not None

    def body(*refs):
        a_ref, b_ref = refs[0], refs[1]
        ex = refs[2:2 + n_extra]
        o_ref = refs[2 + n_extra]

        def finish(acc):
            if has_add:
                acc = acc + ex[0][...].astype(F32)
            if epi is not None:
                acc = epi(acc, *[e[...] for e in ex[(1 if has_add else 0):]])
            o_ref[...] = acc.astype(o_ref.dtype)

        p = _dot(a_ref[...].astype(BF), b_ref[...].astype(BF), mode)
        if nk == 1:
            finish(p)
        else:
            acc_ref = refs[3 + n_extra]
            k = pl.program_id(2)

            @pl.when(k == 0)
            def _():
                acc_ref[...] = p

            @pl.when(k > 0)
            def _():
                acc_ref[...] += p

            @pl.when(k == nk - 1)
            def _():
                finish(acc_ref[...])

    blk = (_nbytes((tm, tk), a.dtype) + _nbytes((tk, tn), b.dtype) + _nbytes((tm, tn), out_dtype)
           + sum(_nbytes((tm, tn), e.dtype) for e in extra) + 2 * _nbytes((tm, tn), F32))
    return pl.pallas_call(
        body, name=name, grid=grid,
        in_specs=[a_spec, b_spec] + [o_spec] * n_extra,
        out_specs=o_spec,
        out_shape=jax.ShapeDtypeStruct((M, N), out_dtype),
        scratch_shapes=[pltpu.VMEM((tm, tn), F32)] if nk > 1 else [],
        compiler_params=pltpu.CompilerParams(
            dimension_semantics=("parallel", "parallel", "arbitrary"), vmem_limit_bytes=_vmem_limit(blk)),
    )(a, b, *extra)


def _ew(fn, T, tm, ins, consts, outs, accs, *, name, reverse=False):
    tm = min(tm, T)
    nt = T // tm
    n_in, n_c, n_o, n_a = len(ins), len(consts), len(outs), len(accs)

    def row(i):
        return nt - 1 - i if reverse else i

    in_specs = [pl.BlockSpec((tm, w), functools.partial(lambda i, cb: (row(i), cb), cb=cb)) for (_, w, cb) in ins]
    in_specs += [pl.BlockSpec(c.shape, functools.partial(lambda i, nd: (0,) * nd, nd=c.ndim)) for c in consts]
    out_specs = [pl.BlockSpec((tm, w), lambda i: (row(i), 0)) for (w, _) in outs]
    out_specs += [pl.BlockSpec((r, w), lambda i: (0, 0)) for (r, w) in accs]
    out_shape = [jax.ShapeDtypeStruct((T, w), dt) for (w, dt) in outs]
    out_shape += [jax.ShapeDtypeStruct((r, w), F32) for (r, w) in accs]

    def body(*refs):
        in_refs = refs[:n_in]
        c_refs = refs[n_in:n_in + n_c]
        o_refs = refs[n_in + n_c:n_in + n_c + n_o]
        a_refs = refs[n_in + n_c + n_o:]
        ov, av = fn([r[...] for r in in_refs], [r[...] for r in c_refs])
        for r, v in zip(o_refs, ov):
            r[...] = v.astype(r.dtype)
        if n_a:
            i = pl.program_id(0)

            @pl.when(i == 0)
            def _():
                for r, v in zip(a_refs, av):
                    r[...] = v

            @pl.when(i > 0)
            def _():
                for r, v in zip(a_refs, av):
                    r[...] += v

    blk = (sum(_nbytes((tm, w), a.dtype) for (a, w, _) in ins) + sum(_nbytes(c.shape, c.dtype) for c in consts)
           + sum(_nbytes((tm, w), dt) for (w, dt) in outs) + sum(_nbytes(s, F32) for s in accs))
    res = pl.pallas_call(
        body, name=name, grid=(nt,), in_specs=in_specs, out_specs=out_specs, out_shape=out_shape,
        compiler_params=pltpu.CompilerParams(
            dimension_semantics=("arbitrary",), vmem_limit_bytes=_vmem_limit(blk)),
    )(*[a for (a, _, _) in ins], *consts)
    return res


def _colsum(v):
    return jnp.sum(v, axis=0, keepdims=True)


FGATE_GROUP = 4


def _fgate_fwd(fl, fb, T, tm=512):
    tm = min(tm, T)

    def body(fl_ref, fb_ref, f_ref, carry_ref):
        i = pl.program_id(0)

        @pl.when(i == 0)
        def _():
            carry_ref[...] = jnp.zeros_like(carry_ref)

        rows = FGATE_GROUP * SLAB
        sub = lax.broadcasted_iota(jnp.int32, (rows, DH), 0) % SLAB

        def group(s, carry):
            r0 = pl.multiple_of(s * rows, rows)
            z = fl_ref[pl.ds(r0, rows), :] + fb_ref[...]
            c = jnp.minimum(z, 0.0) - _log1p(jnp.exp(-jnp.abs(z)))
            for k in (1, 2, 4):
                c = c + jnp.where(sub >= k, pltpu.roll(c, k, 0), 0.0)
            for u in range(FGATE_GROUP):
                cu = c[u * SLAB:(u + 1) * SLAB] + carry
                f_ref[pl.ds(r0 + u * SLAB, SLAB), :] = cu
                carry = cu[SLAB - 1:SLAB, :]
            return carry

        carry_ref[0:1, :] = lax.fori_loop(0, tm // rows, group, carry_ref[0:1, :])

    return pl.pallas_call(
        body, name="fgate_fwd", grid=(T // tm,),
        in_specs=[pl.BlockSpec((tm, DH), lambda i: (i, 0)), pl.BlockSpec((1, DH), lambda i: (0, 0))],
        out_specs=pl.BlockSpec((tm, DH), lambda i: (i, 0)),
        out_shape=jax.ShapeDtypeStruct((T, DH), F32),
        scratch_shapes=[pltpu.VMEM((SLAB, DH), F32)],
        compiler_params=pltpu.CompilerParams(dimension_semantics=("arbitrary",)),
    )(fl, fb)


def _fgate_bwd(dF, fl, fb, T, tm=512):
    tm = min(tm, T)
    nt = T // tm

    def body(df_ref, fl_ref, fb_ref, o_ref, acc_ref, carry_ref):
        i = pl.program_id(0)

        @pl.when(i == 0)
        def _():
            carry_ref[...] = jnp.zeros_like(carry_ref)
            acc_ref[...] = jnp.zeros_like(acc_ref)

        rows = FGATE_GROUP * SLAB
        sub = lax.broadcasted_iota(jnp.int32, (rows, DH), 0) % SLAB

        def group(n, carry):
            g_next, acc = carry
            r0 = pl.multiple_of((tm // rows - 1 - n) * rows, rows)
            c = df_ref[pl.ds(r0, rows), :]
            for k in (1, 2, 4):
                c = c + jnp.where(sub < SLAB - k, pltpu.roll(c, rows - k, 0), 0.0)
            sg = _sig(-(fl_ref[pl.ds(r0, rows), :] + fb_ref[...]))
            for u in reversed(range(FGATE_GROUP)):
                cu = c[u * SLAB:(u + 1) * SLAB] + g_next
                dfl = cu * sg[u * SLAB:(u + 1) * SLAB]
                o_ref[pl.ds(r0 + u * SLAB, SLAB), :] = dfl.astype(o_ref.dtype)
                acc = acc + _colsum(dfl)
                g_next = cu[0:1, :]
            return g_next, acc

        g, acc = lax.fori_loop(0, tm // rows, group, (carry_ref[0:1, :], jnp.zeros((1, DH), F32)))
        carry_ref[0:1, :] = g
        acc_ref[...] += acc

    return pl.pallas_call(
        body, name="fgate_bwd", grid=(nt,),
        in_specs=[pl.BlockSpec((tm, DH), lambda i: (nt - 1 - i, 0)), pl.BlockSpec((tm, DH), lambda i: (nt - 1 - i, 0)),
                  pl.BlockSpec((1, DH), lambda i: (0, 0))],
        out_specs=[pl.BlockSpec((tm, DH), lambda i: (nt - 1 - i, 0)), pl.BlockSpec((1, DH), lambda i: (0, 0))],
        out_shape=[jax.ShapeDtypeStruct((T, DH), BF), jax.ShapeDtypeStruct((1, DH), F32)],
        scratch_shapes=[pltpu.VMEM((SLAB, DH), F32)],
        compiler_params=pltpu.CompilerParams(dimension_semantics=("arbitrary",)),
    )(dF, fl, fb)


def _conv(x, prev8, cw, cb):
    xs = [x] + [_shift_down(x, d, prev8) for d in (1, 2, 3)]
    xa = cb + cw[3:4, :] * xs[0] + cw[2:3, :] * xs[1] + cw[1:2, :] * xs[2] + cw[0:1, :] * xs[3]
    return xa, xs


def _lru_gates(xa_g, wa_g, wx_g, ba_g, bx_g, sp_g):
    xb = xa_g.astype(BF)
    r = _sig(_dot(xb, wa_g) + ba_g)
    ig = _sig(_dot(xb, wx_g) + bx_g)
    la = -LRU_C * r * sp_g
    a = jnp.exp(la)
    mult = jnp.sqrt(_one_minus_sq(a, la))
    return r, ig, a, mult


def _lru_fwd(xg, cw, vec, wabd, wxbd, T, tm=256):
    tm = min(tm, T)
    nsl = tm // SLAB

    def body(x_ref, xp_ref, cw_ref, vec_ref, wa_ref, wx_ref, h_ref, a_s, b_s, carry_ref):
        i = pl.program_id(0)

        @pl.when(i == 0)
        def _():
            carry_ref[...] = jnp.zeros_like(carry_ref)

        x = x_ref[...].astype(F32)
        prev8 = jnp.where(i > 0, xp_ref[SLAB:, :].astype(F32), 0.0)
        vec_v = vec_ref[...]
        xa, _ = _conv(x, prev8, cw_ref[...], vec_v[0:1, :])
        sp = _softplus(-vec_v[3:4, :])
        for g in range(NBD):
            sl = slice(g * BD, (g + 1) * BD)
            _, ig, a, mult = _lru_gates(xa[:, sl], wa_ref[g], wx_ref[g], vec_v[1:2, sl], vec_v[2:3, sl], sp[:, sl])
            a_s[:, sl] = a
            b_s[:, sl] = mult * ig * xa[:, sl]

        def slab(s, carry):
            r0 = pl.multiple_of(s * SLAB, SLAB)
            A, B = _slab_scan_fwd(a_s[pl.ds(r0, SLAB), :], b_s[pl.ds(r0, SLAB), :])
            h = A * carry + B
            h_ref[pl.ds(r0, SLAB), :] = h
            return h[SLAB - 1:SLAB, :]

        carry_ref[0:1, :] = lax.fori_loop(0, nsl, slab, carry_ref[0:1, :])

    blk = 5 * _nbytes((tm, D), F32) + 2 * _nbytes((NBD, BD, BD), BF)
    return pl.pallas_call(
        body, name="lru_fwd", grid=(T // tm,),
        in_specs=[pl.BlockSpec((tm, D), lambda i: (i, 0)),
                  pl.BlockSpec((2 * SLAB, D), lambda i: (jnp.maximum(i * (nsl // 2) - 1, 0), 0)),
                  pl.BlockSpec((CONV, D), lambda i: (0, 0)),
                  pl.BlockSpec((SLAB, D), lambda i: (0, 0)),
                  pl.BlockSpec((NBD, BD, BD), lambda i: (0, 0, 0)),
                  pl.BlockSpec((NBD, BD, BD), lambda i: (0, 0, 0))],
        out_specs=pl.BlockSpec((tm, D), lambda i: (i, 0)),
        out_shape=jax.ShapeDtypeStruct((T, D), F32),
        scratch_shapes=[pltpu.VMEM((tm, D), F32), pltpu.VMEM((tm, D), F32), pltpu.VMEM((SLAB, D), F32)],
        compiler_params=pltpu.CompilerParams(dimension_semantics=("arbitrary",), vmem_limit_bytes=_vmem_limit(blk)),
    )(xg, xg, cw, vec, wabd, wxbd)


def _lru_bwd(xg, h, dha, cw, vec, wabd, wxbd, T, tm=256):
    tm = min(tm, T)
    nsl = tm // SLAB
    nt = T // tm

    def body(x_ref, xp_ref, h_ref, hp_ref, dh_ref, cw_ref, vec_ref, wa_ref, wx_ref,
             dx_ref, dwa_ref, dwx_ref, acc_ref, a_s, b_s, g_s, dxa_s, carry_ref, dxan_ref):
        n = pl.program_id(0)
        it = nt - 1 - n

        @pl.when(n == 0)
        def _():
            carry_ref[...] = jnp.zeros_like(carry_ref)
            dxan_ref[...] = jnp.zeros_like(dxan_ref)
            dwa_ref[...] = jnp.zeros_like(dwa_ref)
            dwx_ref[...] = jnp.zeros_like(dwx_ref)
            acc_ref[...] = jnp.zeros_like(acc_ref)

        x = x_ref[...].astype(F32)
        prev8 = jnp.where(it > 0, xp_ref[SLAB:, :].astype(F32), 0.0)
        hprev8 = jnp.where(it > 0, hp_ref[...], 0.0)
        vec_v = vec_ref[...]
        cw_v = cw_ref[...]
        xa, xs = _conv(x, prev8, cw_v, vec_v[0:1, :])
        sp = _softplus(-vec_v[3:4, :])
        gates = []
        for g in range(NBD):
            sl = slice(g * BD, (g + 1) * BD)
            r, ig, a, mult = _lru_gates(xa[:, sl], wa_ref[g], wx_ref[g], vec_v[1:2, sl], vec_v[2:3, sl], sp[:, sl])
            gates.append((r, ig, a, mult))
            a_s[:, sl] = a
        a_next = _shift_up(a_s[...], 1, carry_ref[...])
        a_s[...] = a_next
        b_s[...] = dh_ref[...]

        def slab(m, carry):
            r0 = pl.multiple_of((nsl - 1 - m) * SLAB, SLAB)
            A, B = _slab_scan_bwd(a_s[pl.ds(r0, SLAB), :], b_s[pl.ds(r0, SLAB), :])
            gg = A * carry + B
            g_s[pl.ds(r0, SLAB), :] = gg
            return gg[0:1, :]

        g_first = lax.fori_loop(0, nsl, slab, carry_ref[1:2, :])
        gt = g_s[...]
        h_prev = _shift_down(h_ref[...], 1, hprev8)
        dba = []
        dbx = []
        dsp = []
        for g in range(NBD):
            sl = slice(g * BD, (g + 1) * BD)
            r, ig, a, mult = gates[g]
            xa_g = xa[:, sl]
            g_g = gt[:, sl]
            da = g_g * h_prev[:, sl]
            dmult = g_g * ig * xa_g
            di = g_g * mult * xa_g
            dxa_g = g_g * mult * ig
            dla = da * a - dmult * (a * a / mult)
            dr = dla * (-LRU_C) * sp[:, sl]
            dsp.append(_colsum(dla * (-LRU_C) * r))
            dra = (dr * r * (1.0 - r))
            dix = (di * ig * (1.0 - ig))
            dba.append(_colsum(dra))
            dbx.append(_colsum(dix))
            dra_b = dra.astype(BF)
            dix_b = dix.astype(BF)
            xb = xa_g.astype(BF)
            dxa_g = dxa_g + _dot(dra_b, wa_ref[g], "nt") + _dot(dix_b, wx_ref[g], "nt")
            dwa_ref[g] += _dot(xb, dra_b, "tn")
            dwx_ref[g] += _dot(xb, dix_b, "tn")
            dxa_s[:, sl] = dxa_g
        dxa = dxa_s[...]
        nxt = dxan_ref[...]
        dx = (cw_v[3:4, :] * dxa + cw_v[2:3, :] * _shift_up(dxa, 1, nxt)
              + cw_v[1:2, :] * _shift_up(dxa, 2, nxt) + cw_v[0:1, :] * _shift_up(dxa, 3, nxt))
        dx_ref[...] = dx.astype(dx_ref.dtype)
        acc_ref[0:1, :] += jnp.concatenate(dba, axis=1)
        acc_ref[1:2, :] += jnp.concatenate(dbx, axis=1)
        acc_ref[2:3, :] += jnp.concatenate(dsp, axis=1)
        acc_ref[3:4, :] += _colsum(dxa)
        for k in range(CONV):
            acc_ref[4 + k:5 + k, :] += _colsum(dxa * xs[CONV - 1 - k])
        dxan_ref[...] = dxa[0:SLAB, :]
        a_first = jnp.concatenate([gates[g][2][0:1, :] for g in range(NBD)], axis=1)
        carry_ref[0:1, :] = a_first
        carry_ref[1:2, :] = g_first

        @pl.when(n == nt - 1)
        def _():
            acc_ref[2:3, :] = acc_ref[2:3, :] * (-_sig(-vec_v[3:4, :]))

    rowblk = lambda i: (nt - 1 - i, 0)
    prevblk = lambda i: (jnp.maximum((nt - 1 - i) * nsl - 1, 0), 0)
    c2 = lambda i: (0, 0)
    c3 = lambda i: (0, 0, 0)
    blk = 12 * _nbytes((tm, D), F32) + 6 * _nbytes((NBD, BD, BD), F32)
    return pl.pallas_call(
        body, name="lru_bwd", grid=(nt,),
        in_specs=[pl.BlockSpec((tm, D), rowblk),
                  pl.BlockSpec((2 * SLAB, D), lambda i: (jnp.maximum((nt - 1 - i) * (nsl // 2) - 1, 0), 0)),
                  pl.BlockSpec((tm, D), rowblk), pl.BlockSpec((SLAB, D), prevblk),
                  pl.BlockSpec((tm, D), rowblk),
                  pl.BlockSpec((CONV, D), c2), pl.BlockSpec((SLAB, D), c2),
                  pl.BlockSpec((NBD, BD, BD), c3), pl.BlockSpec((NBD, BD, BD), c3)],
        out_specs=[pl.BlockSpec((tm, D), rowblk), pl.BlockSpec((NBD, BD, BD), c3), pl.BlockSpec((NBD, BD, BD), c3),
                   pl.BlockSpec((16, D), c2)],
        out_shape=[jax.ShapeDtypeStruct((T, D), BF), jax.ShapeDtypeStruct((NBD, BD, BD), F32),
                   jax.ShapeDtypeStruct((NBD, BD, BD), F32), jax.ShapeDtypeStruct((16, D), F32)],
        scratch_shapes=[pltpu.VMEM((tm, D), F32), pltpu.VMEM((tm, D), F32), pltpu.VMEM((tm, D), F32),
                        pltpu.VMEM((tm, D), F32), pltpu.VMEM((SLAB, D), F32), pltpu.VMEM((SLAB, D), F32)],
        compiler_params=pltpu.CompilerParams(dimension_semantics=("arbitrary",), vmem_limit_bytes=_vmem_limit(blk)),
    )(xg, xg, h, h, dha, cw, vec, wabd, wxbd)


_SCALE = 1.0 / math.sqrt(DH)


_ANY = pl.BlockSpec(memory_space=pl.ANY)


class _Side:
    def __init__(self, srcs, outs, nsem, copies):
        self.srcs, self.outs, self.nsem, self.copies = list(srcs), list(outs), nsem, copies


def _pallas(body, operands, *, name, grid, in_specs, out_specs, out_shape, scratch_shapes=(), semantics,
            vmem=None, side=None):
    if side is None:
        return pl.pallas_call(
            body, name=name, grid=grid, in_specs=in_specs, out_specs=out_specs, out_shape=out_shape,
            scratch_shapes=list(scratch_shapes),
            compiler_params=pltpu.CompilerParams(dimension_semantics=semantics, vmem_limit_bytes=vmem),
        )(*operands)
    n_in, n_out, n_scr = len(in_specs), len(out_specs), len(scratch_shapes)
    ns, no = len(side.srcs), len(side.outs)

    def hosted(*refs):
        ins, refs = refs[:n_in], refs[n_in:]
        sin, refs = refs[:ns], refs[ns:]
        outs, refs = refs[:n_out], refs[n_out:]
        sout, refs = refs[:no], refs[no:]
        scr, (send, recv) = refs[:n_scr], refs[n_scr:]
        ids = [pl.program_id(a) for a in range(len(grid))]
        first = functools.reduce(jnp.logical_and, [i == 0 for i in ids])
        last = functools.reduce(jnp.logical_and, [i == g - 1 for i, g in zip(ids, grid)])

        @pl.when(first)
        def _():
            for cp in side.copies(sin, sout, send, recv):
                cp.start()

        body(*ins, *outs, *scr)

        @pl.when(last)
        def _():
            for cp in side.copies(sin, sout, send, recv):
                cp.wait()

    return pl.pallas_call(
        hosted, name=name, grid=grid, in_specs=list(in_specs) + [_ANY] * ns, out_specs=list(out_specs) + [_ANY] * no,
        out_shape=list(out_shape) + side.outs,
        scratch_shapes=list(scratch_shapes) + [pltpu.SemaphoreType.DMA((side.nsem,)), pltpu.SemaphoreType.DMA((side.nsem,))],
        compiler_params=pltpu.CompilerParams(dimension_semantics=("arbitrary",) * len(grid), vmem_limit_bytes=vmem),
    )(*operands, *side.srcs)


DA = 2 * DH
_LOG2E = math.log2(math.e)
_C2 = _SCALE * _LOG2E


def _aug_fn(ins, cs):
    q, k, fcum = ins
    g_all = fcum * _LOG2E
    lane = lax.broadcasted_iota(jnp.int32, (q.shape[0], DH), 1)
    qa, ka = [], []
    for hd in range(NH):
        g = g_all[:, hd:hd + 1]
        hi = g.astype(BF).astype(F32)
        mid = (g - hi).astype(BF).astype(F32)
        lo = ((g - hi) - mid).astype(BF).astype(F32)
        qx = jnp.where(lane == 0, hi, jnp.where(lane == 1, mid, jnp.where(lane == 2, lo,
                                                                          jnp.where(lane < 6, 1.0, 0.0))))
        kx = jnp.where(lane < 3, 1.0, jnp.where(lane == 3, -hi, jnp.where(lane == 4, -mid,
                                                                          jnp.where(lane == 5, -lo, 0.0))))
        qa += [q[:, hd * DH:(hd + 1) * DH], qx.astype(BF)]
        ka += [k[:, hd * DH:(hd + 1) * DH], kx.astype(BF)]
    return [jnp.concatenate(qa, axis=1), jnp.concatenate(ka, axis=1)], []


_KA_ONES = DH + 3
KT_ONES = 16


def _attn_fwd(qa, ka, qkv, T, blk=512, side=None):
    blk = min(blk, T)
    nb = T // blk
    half = blk // 2

    def body(q_ref, k_ref, vn_ref, o_ref, lse_ref, v_ref):
        i = pl.program_id(1)

        @pl.when(i == 0)
        def _():
            for jj in range(nb):
                v_ref[jj] = vn_ref[jj * blk:(jj + 1) * blk, :].astype(F32).T.astype(BF)

        q = q_ref[...]

        def scores(j):
            r0 = pl.multiple_of(j * blk, blk)
            return _dot(k_ref[pl.ds(r0, blk), :], q, "nt")

        def update(s, vj, carry):
            m, l, acc = carry
            m_new = jnp.maximum(m, jnp.max(s, axis=0, keepdims=True))
            alpha = jnp.exp2(m - m_new)
            p = jnp.exp2(s - m_new)
            l = alpha * l + jnp.sum(p, axis=0, keepdims=True)
            acc = alpha * acc + _dot(vj, p.astype(BF))
            return m_new, l, acc

        def step(j, carry):
            s, st = carry
            return scores(j + 1), update(s, v_ref[j], st)

        init = (jnp.full((1, blk), -jnp.inf, F32), jnp.zeros((1, blk), F32), jnp.zeros((DH, blk), F32))
        last, carry = lax.fori_loop(0, i, step, (scores(0), init))
        vi = v_ref[i]
        rk = lax.broadcasted_iota(jnp.int32, (half, blk), 0)
        cq = lax.broadcasted_iota(jnp.int32, (half, blk), 1)
        m, l, acc = update(jnp.where(cq >= rk, last[:half], -jnp.inf), vi[:, :half], carry)
        s2 = jnp.where(cq[:, :half] >= rk[:, :half], last[half:, half:], -jnp.inf)
        m2, l2, acc2 = update(s2, vi[:, half:], (m[:, half:], l[:, half:], acc[:, half:]))
        m = jnp.concatenate([m[:, :half], m2], axis=1)
        l = jnp.concatenate([l[:, :half], l2], axis=1)
        acc = jnp.concatenate([acc[:, :half], acc2], axis=1)
        o_ref[...] = (acc / l).T.astype(o_ref.dtype)
        lse_ref[...] = m + jnp.log(l) * _LOG2E

    vm = _nbytes((T, DA), BF) + 2 * _nbytes((T, DH), BF) + 6 * _nbytes((blk, blk), F32)
    return _pallas(
        body, (qa, ka, qkv), name="attn_fwd", grid=(NH, nb),
        in_specs=[pl.BlockSpec((blk, DA), lambda h, i: (i, h)),
                  pl.BlockSpec((T, DA), lambda h, i: (0, h)),
                  pl.BlockSpec((T, DH), lambda h, i: (0, 2 * NH + h))],
        out_specs=[pl.BlockSpec((blk, DH), lambda h, i: (i, h)),
                   pl.BlockSpec((None, None, 1, blk), lambda h, i: (h, i, 0, 0))],
        out_shape=[jax.ShapeDtypeStruct((T, D), BF), jax.ShapeDtypeStruct((NH, nb, 1, blk), F32)],
        scratch_shapes=[pltpu.VMEM((nb, DH, blk), BF)],
        semantics=("parallel", "arbitrary"), vmem=_vmem_limit(vm), side=side)


def _attn_bwd(qa, ka, qkv, do, lrow, drow, T, blk=512, side=None):
    blk = min(blk, T)
    nb = T // blk

    def body(ka_ref, v_ref, qa_ref, do_ref, l_ref, d_ref, dq_ref, dk_ref, dv_ref, dfs_ref, dft_ref, dq_s):
        j = pl.program_id(1)

        @pl.when(j == 0)
        def _():
            dq_s[...] = jnp.zeros_like(dq_s)

        row = lax.broadcasted_iota(jnp.int32, (DH + KT_ONES, blk), 0)
        dq_scale = jnp.where(row < DH, _SCALE, 1.0)

        kaj = ka_ref[...]
        ktj = jnp.concatenate([kaj[:, :DH].astype(F32).T.astype(BF), jnp.ones((KT_ONES, blk), BF)], axis=0)
        vj = v_ref[...]

        def step(i, carry):
            dka, dv = carry
            r0 = pl.multiple_of(i * blk, blk)
            qi = qa_ref[pl.ds(r0, blk), :]
            doi = do_ref[pl.ds(r0, blk), :]
            pt = jnp.exp2(_dot(kaj, qi, "nt") - l_ref[i])
            dv = dv + _dot(pt.astype(BF), doi)
            dpt = _dot(vj, doi, "nt")
            dst = pt * (dpt - d_ref[i])
            dsb = dst.astype(BF)
            dka = dka + _dot(dsb, qi)
            dq_s[i] += _dot(ktj, dsb) * dq_scale
            return dka, dv

        def diagonal():
            half = blk // 2
            r0 = pl.multiple_of(j * blk, blk)
            qi = qa_ref[pl.ds(r0, blk), :]
            doi = do_ref[pl.ds(r0, blk), :]
            lr, dr = l_ref[j], d_ref[j]
            rk = lax.broadcasted_iota(jnp.int32, (half, blk), 0)
            cq = lax.broadcasted_iota(jnp.int32, (half, blk), 1)

            def quarter(ka, v, q, do, l2, d2, keep):
                pt = jnp.exp2(jnp.where(keep, _dot(ka, q, "nt") - l2, -jnp.inf))
                dsb = (pt * (_dot(v, do, "nt") - d2)).astype(BF)
                return _dot(dsb, q), _dot(pt.astype(BF), do), dsb

            dka1, dv1, ds1 = quarter(kaj[:half], vj[:half], qi, doi, lr, dr, cq >= rk)
            dka2, dv2, ds2 = quarter(kaj[half:], vj[half:], qi[half:], doi[half:], lr[:, half:], dr[:, half:],
                                     cq[:, :half] >= rk[:, :half])
            dq2 = jnp.concatenate([jnp.zeros((DH + KT_ONES, half), F32), _dot(ktj[:, half:], ds2)], axis=1)
            dq_s[j] += (_dot(ktj[:, :half], ds1) + dq2) * dq_scale
            return jnp.concatenate([dka1, dka2], axis=0), jnp.concatenate([dv1, dv2], axis=0)

        dka, dv = lax.fori_loop(j + 1, nb, step, diagonal())
        dk_ref[...] = (dka[:, :DH] * (1.0 / _LOG2E)).astype(dk_ref.dtype)
        dv_ref[...] = dv.astype(dv_ref.dtype)
        dfs_ref[...] = dka[:, DH:].T[_KA_ONES - DH:_KA_ONES - DH + 1, :]

        @pl.when(j == nb - 1)
        def _():
            for ii in range(nb):
                t = dq_s[ii]
                dq_ref[ii * blk:(ii + 1) * blk, :] = t[:DH].T.astype(dq_ref.dtype)
                dft_ref[ii] = t[DH:DH + 1]

    rowv = pl.BlockSpec((None, nb, 1, blk), lambda h, j: (h, 0, 0, 0))
    vm = (_nbytes((T, DA), BF) + 2 * _nbytes((T, DH), BF) + _nbytes((T, DH + KT_ONES), F32)
          + 8 * _nbytes((blk, blk), F32))
    return _pallas(
        body, (ka, qkv, qa, do, lrow, drow), name="attn_bwd", grid=(NH, nb),
        in_specs=[pl.BlockSpec((blk, DA), lambda h, j: (j, h)),
                  pl.BlockSpec((blk, DH), lambda h, j: (j, 2 * NH + h)),
                  pl.BlockSpec((T, DA), lambda h, j: (0, h)),
                  pl.BlockSpec((T, DH), lambda h, j: (0, h)),
                  rowv, rowv],
        out_specs=[pl.BlockSpec((T, DH), lambda h, j: (0, h)),
                   pl.BlockSpec((blk, DH), lambda h, j: (j, h)),
                   pl.BlockSpec((blk, DH), lambda h, j: (j, h)),
                   pl.BlockSpec((None, None, 1, blk), lambda h, j: (h, j, 0, 0)), rowv],
        out_shape=[jax.ShapeDtypeStruct((T, D), BF), jax.ShapeDtypeStruct((T, D), BF),
                   jax.ShapeDtypeStruct((T, D), BF), jax.ShapeDtypeStruct((NH, nb, 1, blk), F32),
                   jax.ShapeDtypeStruct((NH, nb, 1, blk), F32)],
        scratch_shapes=[pltpu.VMEM((nb, DH + KT_ONES, blk), F32)],
        semantics=("parallel", "arbitrary"), vmem=_vmem_limit(vm), side=side)


def _norm_fn(ins, cs):
    x, = ins
    g, = cs
    r = lax.rsqrt(jnp.mean(x * x, axis=-1, keepdims=True) + EPS)
    return [x * r * g], []


def _norm_bwd_fn(ins, cs):
    x, dy, dres = ins
    g, = cs
    r = lax.rsqrt(jnp.mean(x * x, axis=-1, keepdims=True) + EPS)
    xh = x * r
    dxh = dy * g
    dx = dres + r * (dxh - xh * jnp.mean(dxh * xh, axis=-1, keepdims=True))
    return [dx], [_colsum(dy * xh)]


def _final_fn(ins, cs):
    x2, tgt = ins
    g, = cs
    r = lax.rsqrt(jnp.mean(x2 * x2, axis=-1, keepdims=True) + EPS)
    xh = x2 * r
    e = xh * g - tgt
    dy = e * (1.0 / D)
    dxh = dy * g
    dx2 = r * (dxh - xh * jnp.mean(dxh * xh, axis=-1, keepdims=True))
    return [dx2], [_colsum(0.5 * e * e * (1.0 / D)), _colsum(dy * xh)]


def _z_fn(ins, cs):
    g, h = [v.astype(F32) for v in ins]
    return [_gelu(g) * h], []


def _mix_fn(ins, cs):
    gates, ya, yb = [v.astype(F32) for v in ins]
    return [_sig(gates[:, :D]) * ya + _sig(gates[:, D:]) * yb], []


def _mix_bwd_fn(ins, cs):
    dmix, gates, ya, yb = [v.astype(F32) for v in ins]
    ga = _sig(gates[:, :D])
    gb = _sig(gates[:, D:])
    dgates = jnp.concatenate([dmix * ya * ga * (1.0 - ga), dmix * yb * gb * (1.0 - gb)], axis=1)
    return [dmix * ga, dmix * gb, dgates], []


def _z_bwd_fn(ins, cs):
    dz, g, h = [v.astype(F32) for v in ins]
    return [dz * _gelu(g), dz * h * _gelu_grad(g)], []


def _delta_fn(ins, cs):
    do, o = ins
    p = do.astype(F32) * o.astype(F32)
    lane = lax.broadcasted_iota(jnp.int32, (p.shape[0], DH), 1)
    out = jnp.zeros((p.shape[0], DH), F32)
    for hd in range(NH):
        s = jnp.sum(p[:, hd * DH:(hd + 1) * DH], axis=1, keepdims=True)
        out = jnp.where(lane == hd, s, out)
    return [out], []


def _du_all(pieces, win, T, tm=256, side=None):
    tm = min(tm, T)
    n = len(pieces)

    def body(*refs):
        w_ref, o_ref = refs[n], refs[n + 1]
        acc = None
        for (a, off), a_ref in zip(pieces, refs[:n]):
            d = _dot(a_ref[...].astype(BF), w_ref[:, off:off + a.shape[1]], "nt")
            acc = d if acc is None else acc + d
        o_ref[...] = acc

    vm = (sum(_nbytes((tm, a.shape[1]), a.dtype) for a, _ in pieces) + _nbytes(win.shape, win.dtype)
          + 2 * _nbytes((tm, D), F32))
    return _pallas(
        body, tuple(a for a, _ in pieces) + (win,), name="du_all", grid=(T // tm,),
        in_specs=[pl.BlockSpec((tm, a.shape[1]), lambda i: (i, 0)) for a, _ in pieces]
        + [pl.BlockSpec(win.shape, lambda i: (0, 0))],
        out_specs=[pl.BlockSpec((tm, D), lambda i: (i, 0))],
        out_shape=[jax.ShapeDtypeStruct((T, D), F32)],
        semantics=("arbitrary",), vmem=int(min(VMEM_CAP, 2 * vm + (4 << 20))), side=side)


def _local_step(x, tgt, w, T, blk=1024, dist=None):
    blk = min(blk, T)
    nb = T // blk
    win = w["win"]

    u, = _ew(_norm_fn, T, 512, [(x, D, 0)], [w["g_mix"]], [(D, BF)], [], name="norm_mix")
    xg = _mm(u, win, "nn", T, 2 * D, D, name="proj_lru", out_dtype=BF)
    qkv = _mm(u, win, "nn", T, 3 * D, D, name="proj_qkv", out_dtype=BF, b_off=(0, 2),
              epi=lambda acc: acc * jnp.where(pl.program_id(1) == 0, _C2, 1.0))
    gates = _mm(u, win, "nn", T, 2 * D, D, name="proj_gates", b_off=(0, 5), out_dtype=BF)
    fl = _mm(u, win, "nn", T, DH, D, name="proj_f", tn=DH, b_off=(0, 7 * D // DH))
    fcum = _fgate_fwd(fl, w["fb"], T)
    qa, ka = _ew(_aug_fn, T, 256, [(qkv, D, 0), (qkv, D, 1), (fcum, DH, 0)], [],
                 [(NH * DA, BF), (NH * DA, BF)], [], name="attn_augment")

    h = _lru_fwd(xg, w["cw"], w["vec"], w["wabd"], w["wxbd"], T)
    ob, lse, *landed = _attn_fwd(qa, ka, qkv, T, blk, side=dist.weights_side() if dist else None)
    if dist:
        w = dict(w, **dist.weights_landed(landed))
    z, = _ew(_z_fn, T, 512, [(xg, D, 1), (h, D, 0)], [], [(D, BF)], [], name="lru_gelu")
    ya = _mm(z, w["wa"], "nn", T, D, D, name="branch_a", out_dtype=BF)
    yb = _mm(ob, w["wb"], "nn", T, D, D, name="branch_b", out_dtype=BF)
    mix, = _ew(_mix_fn, T, 256, [(gates, 2 * D, 0), (ya, D, 0), (yb, D, 0)], [], [(D, BF)], [], name="mix")
    x1 = _mm(mix, w["wout"], "nn", T, D, D, name="out_proj", add=x)
    m, = _ew(_norm_fn, T, 512, [(x1, D, 0)], [w["g_mlp"]], [(D, BF)], [], name="norm_mlp")
    hh = _mm(m, w["wup"], "nn", T, FF, D, name="mlp_up", out_dtype=BF,
             epi=lambda acc: jnp.square(jnp.maximum(acc, 0.0)))
    x2 = _mm(hh, w["wdown"], "nn", T, D, FF, name="mlp_down", add=x1, tk=2048)
    dx2, loss_vec, dg_fin = _ew(_final_fn, T, 256, [(x2, D, 0), (tgt, D, 0)], [w["g_fin"]], [(D, F32)],
                                [(1, D), (1, D)], name="final_norm_loss")

    dhpre = _mm(dx2, w["wdown"], "nt", T, FF, D, name="mlp_down_bwd", out_dtype=BF,
                epi=lambda acc, h2: acc * (2.0 * jnp.sqrt(h2.astype(F32))), epi_ins=[hh])
    dwdown = _mm(hh, dx2, "tn", FF, D, T, name="dw_down", out_dtype=BF)
    dwup = _mm(m, dhpre, "tn", D, FF, T, name="dw_up", out_dtype=BF)
    dm = _mm(dhpre, w["wup"], "nt", T, D, FF, name="mlp_up_bwd", tk=2048)
    dx1, dg_mlp = _ew(_norm_bwd_fn, T, 256, [(x1, D, 0), (dm, D, 0), (dx2, D, 0)], [w["g_mlp"]], [(D, F32)],
                      [(1, D)], name="norm_mlp_bwd")

    dmix = _mm(dx1, w["wout"], "nt", T, D, D, name="out_proj_bwd", out_dtype=BF)
    dwout = _mm(mix, dx1, "tn", D, D, T, name="dw_out", out_dtype=BF)
    dya, dyb, dgates = _ew(_mix_bwd_fn, T, 256, [(dmix, D, 0), (gates, 2 * D, 0), (ya, D, 0), (yb, D, 0)], [],
                           [(D, BF), (D, BF), (2 * D, BF)], [], name="mix_bwd")
    dob = _mm(dyb, w["wb"], "nt", T, D, D, name="branch_b_bwd", out_dtype=BF)
    dwb = _mm(ob, dyb, "tn", D, D, T, name="dw_b", out_dtype=BF)
    dz = _mm(dya, w["wa"], "nt", T, D, D, name="branch_a_bwd", out_dtype=BF)
    dwa = _mm(z, dya, "tn", D, D, T, name="dw_a", out_dtype=BF)
    dha, dglru = _ew(_z_bwd_fn, T, 256, [(dz, D, 0), (xg, D, 1), (h, D, 0)], [], [(D, F32), (D, BF)], [],
                     name="lru_gelu_bwd")

    delta, = _ew(_delta_fn, T, 512, [(dob, D, 0), (ob, D, 0)], [], [(DH, F32)], [], name="attn_delta")
    drow = delta[:, :NH].T.reshape(NH, nb, 1, blk)
    big = dict(w_branch_a=dwa, w_branch_b=dwb, w_out=dwout, w_up=dwup, w_down=dwdown)
    side = dist.grads_side(big) if dist else None
    dq, dk, dv, dfs, dft, *landed = _attn_bwd(qa, ka, qkv, dob, lse, drow, T, blk, side=side)
    if dist:
        big = dist.grads_landed(side, landed)
    dfcum = jnp.pad((dft - dfs).reshape(NH, T).T, ((0, 0), (0, DH - NH)))
    dfl, dfb = _fgate_bwd(dfcum, fl, w["fb"], T)

    dxl, dwabd, dwxbd, lacc = _lru_bwd(xg, h, dha, w["cw"], w["vec"], w["wabd"], w["wxbd"], T)

    dproj = ((dxl, 0), (dglru, D), (dq, 2 * D), (dk, 3 * D), (dv, 4 * D), (dgates, 5 * D), (dfl, 7 * D))
    pieces = [_mm(u, p, "tn", D, p.shape[1], T, name="dw_in_%d" % n, out_dtype=BF)
              for n, (p, _) in enumerate(dproj)]
    pieces[-1] = pieces[-1][:, :NH]
    dwin = dict(w_in=jnp.concatenate(pieces, axis=1))
    side = dist.grads_side(dwin) if dist else None
    du, *landed = _du_all(dproj, win, T, side=side)
    big.update(dist.grads_landed(side, landed) if dist else dwin)
    dx, dg_mix = _ew(_norm_bwd_fn, T, 256, [(x, D, 0), (du, D, 0), (dx1, D, 0)], [w["g_mix"]], [(D, F32)],
                     [(1, D)], name="norm_mix_bwd")

    return dict(dx=dx, big=big, dwabd=dwabd, dwxbd=dwxbd, lacc=lacc, dfb=dfb, dg_mix=dg_mix, dg_mlp=dg_mlp,
                dg_fin=dg_fin, loss_vec=loss_vec)


def _block_diag(w):
    per = BD // LRU_BW
    w4 = w.reshape(NBD, per, LRU_BW, LRU_BW)
    out = jnp.zeros((NBD, per, LRU_BW, per, LRU_BW), w.dtype)
    for b in range(per):
        out = out.at[:, b, :, b, :].set(w4[:, b])
    return out.reshape(NBD, BD, BD)


def _block_diag_extract(wbd):
    per = BD // LRU_BW
    w5 = wbd.reshape(NBD, per, LRU_BW, per, LRU_BW)
    return jnp.stack([w5[:, b, :, b, :] for b in range(per)], axis=1).reshape(LRU_BLOCKS, LRU_BW, LRU_BW)


_ANY = pl.BlockSpec(memory_space=pl.ANY)


def _place():
    x, y, c = lax.axis_index("x"), lax.axis_index("y"), lax.axis_index("c")
    chips = [(1 - x, y), (x, 1 - y), (1 - x, 1 - y)]
    return x, y, c, chips


def _allgather_shards(shards):
    n = len(shards)

    def body(*refs):
        ins, outs = refs[:n], refs[n:2 * n]
        send_sems, recv_sems = refs[2 * n:]
        x, y, c, chips = _place()
        me = 2 * x + y
        sibling = (x, y, 1 - c)

        def remote(p, k, src, dst, to):
            return pltpu.make_async_remote_copy(src_ref=src, dst_ref=dst, send_sem=send_sems.at[p, k],
                                                recv_sem=recv_sems.at[p, k], device_id=to, device_id_type=MESH)

        sent = []
        for p in range(n):
            for k, chip in enumerate(chips):
                cp = remote(p, k, ins[p].at[c], outs[p].at[me, c], (chip[0], chip[1], c))
                cp.start()
                sent.append(cp)
        for p in range(n):
            for k, chip in enumerate(chips):
                half = outs[p].at[2 * chip[0] + chip[1], c]
                remote(p, k, half, half, sibling).wait_recv()
                fwd = remote(p, 3 + k, half, half, sibling)
                fwd.start()
                sent.append(fwd)
        for p in range(n):
            for k, chip in enumerate(chips):
                half = outs[p].at[2 * chip[0] + chip[1], 1 - c]
                remote(p, 3 + k, half, half, sibling).wait_recv()
        for cp in sent:
            cp.wait_send()

    gathered = pl.pallas_call(
        body, name="allgather_weights",
        in_specs=[_ANY] * n, out_specs=[_ANY] * n,
        out_shape=[jax.ShapeDtypeStruct((NCHIP,) + s.shape, s.dtype) for s in shards],
        scratch_shapes=[pltpu.SemaphoreType.DMA((n, 6)), pltpu.SemaphoreType.DMA((n, 6))],
    )(*shards)
    me = 2 * lax.axis_index("x") + lax.axis_index("y")
    return [lax.dynamic_update_index_in_dim(g, s, me, 0) for g, s in zip(gathered, shards)]


_LATE =["w_branch_a", "w_branch_b", "w_out", "w_up", "w_down"]
_COLUMN_CUT = ("w_in", "w_up")
N_PEERS = 7


def _shard_major(name, g):
    s = _columns_to_shards(g) if name in _COLUMN_CUT else g.reshape(NCHIP, g.shape[0] // NCHIP, g.shape[1])
    return s.reshape(NCHIP, 2, s.shape[1] // 2, s.shape[2])


class _Exchanges:
    def __init__(self, shards):
        self.shards = shards

    def weights_side(self):
        srcs = [self.shards[n] for n in _LATE]

        def copies(sin, sout, send, recv):
            x, y, c, chips = _place()
            return [pltpu.make_async_remote_copy(
                src_ref=sin[p], dst_ref=sout[p].at[2 * x + y], send_sem=send.at[3 * p + k], recv_sem=recv.at[3 * p + k],
                device_id=(chip[0], chip[1], c), device_id_type=MESH)
                for p in range(len(sin)) for k, chip in enumerate(chips)]

        return _Side(srcs, [jax.ShapeDtypeStruct((NCHIP,) + s.shape, s.dtype) for s in srcs], 3 * len(srcs), copies)

    def weights_landed(self, landed):
        me = 2 * lax.axis_index("x") + lax.axis_index("y")
        full = {n: lax.dynamic_update_index_in_dim(g, self.shards[n], me, 0) for n, g in zip(_LATE, landed)}
        return dict(wa=full["w_branch_a"].reshape(D, D), wb=full["w_branch_b"].reshape(D, D),
                    wout=full["w_out"].reshape(D, D), wup=_shards_to_columns(full["w_up"]),
                    wdown=full["w_down"].reshape(FF, D))

    def grads_side(self, grads):
        side_names = list(grads)
        srcs = [_shard_major(n, grads[n]) for n in side_names]

        def copies(sin, sout, send, recv):
            x, y, c, chips = _place()
            peers = [(x, y, 1 - c)] + [(cx, cy, c) for cx, cy in chips] + [(cx, cy, 1 - c) for cx, cy in chips]
            return [pltpu.make_async_remote_copy(
                src_ref=sin[p].at[2 * px + py, pc], dst_ref=sout[p].at[s], send_sem=send.at[N_PEERS * p + s],
                recv_sem=recv.at[N_PEERS * p + s], device_id=(px, py, pc), device_id_type=MESH)
                for p in range(len(sin)) for s, (px, py, pc) in enumerate(peers)]

        side = _Side(srcs, [jax.ShapeDtypeStruct((N_PEERS,) + s.shape[2:], s.dtype) for s in srcs],
                     N_PEERS * len(srcs), copies)
        side.names = side_names
        return side

    def grads_landed(self, side, landed):
        return {n: (own, got) for n, own, got in zip(side.names, side.srcs, landed)}


def _add8(g, got, me, c, name):
    _, _, half, cols = g.shape
    th = _row_tile(half, 2 * cols)

    def body(me_ref, c_ref, g_ref, r_ref, o_ref):
        acc = g_ref[...].astype(F32)
        for s in range(N_PEERS):
            acc = acc + r_ref[s].astype(F32)
        o_ref[...] = acc

    return pl.pallas_call(
        body, name=name,
        grid_spec=pltpu.PrefetchScalarGridSpec(
            num_scalar_prefetch=2, grid=(half // th,),
            in_specs=[pl.BlockSpec((None, None, th, cols), lambda i, me_ref, c_ref: (me_ref[0], c_ref[0], i, 0)),
                      pl.BlockSpec((N_PEERS, th, cols), lambda i, me_ref, c_ref: (0, i, 0))],
            out_specs=pl.BlockSpec((th, cols), lambda i, me_ref, c_ref: (i, 0))),
        out_shape=jax.ShapeDtypeStruct((half, cols), F32),
    )(me, c, g, got)


def _share_halves(halves):
    n = len(halves)

    def body(*refs):
        ins, outs = refs[:n], refs[n:2 * n]
        send_sems, recv_sems = refs[2 * n:]
        x, y, c, _ = _place()
        sibling = (x, y, 1 - c)
        copies = []
        for p in range(n):
            cp = pltpu.make_async_remote_copy(src_ref=ins[p], dst_ref=outs[p], send_sem=send_sems.at[p],
                                              recv_sem=recv_sems.at[p], device_id=sibling, device_id_type=MESH)
            cp.start()
            copies.append(cp)
        for cp in copies:
            cp.wait()

    return pl.pallas_call(
        body, name="reduce_share_halves",
        in_specs=[_ANY] * n, out_specs=[_ANY] * n,
        out_shape=[jax.ShapeDtypeStruct(h.shape, h.dtype) for h in halves],
        scratch_shapes=[pltpu.SemaphoreType.DMA((n,)), pltpu.SemaphoreType.DMA((n,))],
    )(*halves)


def _row_tile(half, cols):
    th = max(SLAB, min(half, (1 << 18) // cols // SLAB * SLAB))
    while half % th:
        th -= SLAB
    return th


N_DEV = 8
SMALL_ROWS = 208


def _allreduce_small(pack):
    def body(x_ref, out_ref, gbuf, send_sems, recv_sems, local_sem):
        x, y, c, chips = _place()
        me, sibling = (x, y, c), (x, y, 1 - c)

        def rows(px, py, pc):
            return gbuf.at[4 * px + 2 * py + pc]

        def copy(k, block, to, src=None):
            return pltpu.make_async_remote_copy(
                src_ref=rows(*block) if src is None else src, dst_ref=rows(*block),
                send_sem=send_sems.at[k], recv_sem=recv_sems.at[k], device_id=to, device_id_type=MESH)

        mine = pltpu.make_async_copy(x_ref, rows(*me), local_sem)
        mine.start()
        first = [copy(0, me, sibling, src=x_ref)]
        first += [copy(1 + j, me, (chip[0], chip[1], c), src=x_ref) for j, chip in enumerate(chips)]
        for cp in first:
            cp.start()
        passed = [copy(4 + j, (chip[0], chip[1], c), sibling) for j, chip in enumerate(chips)]
        for j, chip in enumerate(chips):
            copy(1 + j, (chip[0], chip[1], c), me).wait_recv()
            passed[j].start()
        copy(0, sibling, me).wait_recv()
        for j, chip in enumerate(chips):
            copy(4 + j, (chip[0], chip[1], 1 - c), me).wait_recv()
        for cp in first + passed:
            cp.wait_send()
        mine.wait()
        acc = gbuf[0]
        for d in range(1, N_DEV):
            acc = acc + gbuf[d]
        out_ref[...] = acc

    return pl.pallas_call(
        body, name="allreduce_small",
        in_specs=[pl.BlockSpec(memory_space=pltpu.VMEM)],
        out_specs=pl.BlockSpec(memory_space=pltpu.VMEM),
        out_shape=jax.ShapeDtypeStruct((SMALL_ROWS, D), F32),
        scratch_shapes=[pltpu.VMEM((N_DEV, SMALL_ROWS, D), F32), pltpu.SemaphoreType.DMA((7,)),
                        pltpu.SemaphoreType.DMA((7,)), pltpu.SemaphoreType.DMA],
    )(pack)


def _adamw(w, g, m, v, name):
    rows, cols = w.shape

    def body(w_ref, g_ref, m_ref, v_ref, d_ref, mo_ref, vo_ref):
        gv = g_ref[...]
        mn = ADAM_B1 * m_ref[...] + (1.0 - ADAM_B1) * gv
        vn = ADAM_B2 * v_ref[...] + (1.0 - ADAM_B2) * (gv * gv)
        m_hat = mn / (1.0 - ADAM_B1 ** ADAM_STEP)
        v_hat = vn / (1.0 - ADAM_B2 ** ADAM_STEP)
        d_ref[...] = -ADAM_LR * (m_hat / (jnp.sqrt(v_hat) + ADAM_EPS) + ADAM_WD * w_ref[...])
        mo_ref[...] = mn
        vo_ref[...] = vn

    if rows % SLAB:
        spec, steps = pl.BlockSpec((rows, DH), lambda i: (0, i)), cols // DH
    else:
        th = _row_tile(rows, cols)
        spec, steps = pl.BlockSpec((th, cols), lambda i: (i, 0)), rows // th
    return pl.pallas_call(
        body, name=name, grid=(steps,),
        in_specs=[spec] * 4, out_specs=[spec] * 3,
        out_shape=[jax.ShapeDtypeStruct((rows, cols), F32)] * 3,
        compiler_params=pltpu.CompilerParams(dimension_semantics=("parallel",)),
    )(w, g, m, v)


_SMALL = ["norm_mix_g", "norm_mlp_g", "norm_final_g", "conv_b", "lru_ba", "lru_bx", "lru_lambda"]
_ROW_FB, _ROW_CW, _ROW_WA, _ROW_WX, _ROW_LOSS = 56, 64, 72, 136, 200


def _pack_small(vals, col0):
    def slab(a):
        return jnp.pad(a, ((0, -a.shape[0] % SLAB), (0, D - a.shape[1])))

    rows = [slab(vals[n].reshape(1, D)) for n in _SMALL]
    rows.append(slab(vals["forget_b"].reshape(1, NH)))
    if vals["conv_w"].shape[1] == D:
        rows.append(slab(vals["conv_w"]))
    else:
        rows.append(slab(lax.dynamic_update_slice(jnp.zeros((CONV, D), F32), vals["conv_w"], (0, col0))))
    rows.append(vals["lru_wa"].reshape(LRU_BLOCKS * LRU_BW * LRU_BW // D, D))
    rows.append(vals["lru_wx"].reshape(LRU_BLOCKS * LRU_BW * LRU_BW // D, D))
    rows.append(slab(vals["loss"]) if "loss" in vals else jnp.zeros((SLAB, D), F32))
    return jnp.concatenate(rows, axis=0)


def _unpack_small(pack, col0):
    out = {n: pack[SLAB * i] for i, n in enumerate(_SMALL)}
    out["forget_b"] = pack[_ROW_FB, :NH]
    out["conv_w"] = lax.dynamic_slice(pack[_ROW_CW:_ROW_CW + CONV], (0, col0), (CONV, D // NCHIP))
    out["lru_wa"] = pack[_ROW_WA:_ROW_WX].reshape(LRU_BLOCKS, LRU_BW, LRU_BW)
    out["lru_wx"] = pack[_ROW_WX:_ROW_LOSS].reshape(LRU_BLOCKS, LRU_BW, LRU_BW)
    return out


_WEIGHTS = ["norm_mix_g", "w_in", "conv_w", "conv_b", "lru_wa", "lru_ba", "lru_wx", "lru_bx", "lru_lambda",
            "forget_b", "w_branch_a", "w_branch_b", "w_out", "norm_mlp_g", "w_up", "w_down", "norm_final_g"]
_BIG = ["w_in", "w_branch_a", "w_branch_b", "w_out", "w_up", "w_down"]


def _halves(a):
    return a.reshape(2, a.shape[0] // 2, a.shape[1])


def _columns_to_shards(a):
    rows, cols = a.shape[0], a.shape[1] // NCHIP
    return jnp.transpose(a.reshape(rows, NCHIP, cols), (1, 0, 2))


def _shards_to_columns(a):
    n, rows, cols = a.shape
    return jnp.transpose(a, (1, 0, 2)).reshape(rows, n * cols)


def kernel(x, norm_mix_g, w_in, conv_w, conv_b, lru_wa, lru_ba, lru_wx, lru_bx, lru_lambda, forget_b, w_branch_a, w_branch_b, w_out, norm_mlp_g, w_up, w_down, norm_final_g, loss_target, m_norm_mix_g, m_w_in, m_conv_w, m_conv_b, m_lru_wa, m_lru_ba, m_lru_wx, m_lru_bx, m_lru_lambda, m_forget_b, m_w_branch_a, m_w_branch_b, m_w_out, m_norm_mlp_g, m_w_up, m_w_down, m_norm_final_g, v_norm_mix_g, v_w_in, v_conv_w, v_conv_b, v_lru_wa, v_lru_ba, v_lru_wx, v_lru_bx, v_lru_lambda, v_forget_b, v_w_branch_a, v_w_branch_b, v_w_out, v_norm_mlp_g, v_w_up, v_w_down, v_norm_final_g):
    args = dict(locals())
    wts = {n: args[n] for n in _WEIGHTS}
    mom = {n: args["m_" + n] for n in _WEIGHTS}
    var = {n: args["v_" + n] for n in _WEIGHTS}
    T = x.shape[1]
    xi, yi, ci = lax.axis_index("x"), lax.axis_index("y"), lax.axis_index("c")
    me = 2 * xi + yi
    c1 = jnp.reshape(ci, (1,)).astype(jnp.int32)
    me1 = jnp.reshape(me, (1,)).astype(jnp.int32)
    col0 = me * (D // NCHIP)

    cw_pad = jnp.pad(conv_w, ((0, 4 * SLAB - CONV), (0, 0)))
    g_in, g_cw = _allgather_shards([_halves(w_in.astype(BF)), _halves(cw_pad)])
    cin = DIN // NCHIP
    win = _shards_to_columns(g_in.reshape(NCHIP, D, cin))
    w = dict(
        win=jnp.pad(win, ((0, 0), (0, DINP - DIN))),
        cw=_shards_to_columns(g_cw.reshape(NCHIP, 4 * SLAB, D // NCHIP)[:, :CONV]),
        vec=jnp.concatenate([conv_b[None], lru_ba[None], lru_bx[None], lru_lambda[None],
                             jnp.zeros((SLAB - 4, D), F32)], axis=0),
        fb=jnp.pad(forget_b[None], ((0, 0), (0, DH - NH))),
        wabd=_block_diag(lru_wa).astype(BF), wxbd=_block_diag(lru_wx).astype(BF),
        g_mix=norm_mix_g[None], g_mlp=norm_mlp_g[None], g_fin=norm_final_g[None])

    r = _local_step(x[0], loss_target[0], w, T, dist=_Exchanges({n: wts[n].astype(BF) for n in _LATE}))

    halves = [_add8(*r["big"][n], me1, c1, "add8_" + n) for n in _BIG]
    theirs = _share_halves(halves)
    low = ci == 0
    gsum = {n: jnp.concatenate([jnp.where(low, h, t), jnp.where(low, t, h)], axis=0)
            for n, h, t in zip(_BIG, halves, theirs)}
    lacc = r["lacc"]
    small = dict(norm_mix_g=r["dg_mix"], norm_mlp_g=r["dg_mlp"], norm_final_g=r["dg_fin"], conv_b=lacc[3],
                 lru_ba=lacc[0], lru_bx=lacc[1], lru_lambda=lacc[2], forget_b=r["dfb"][0, :NH],
                 conv_w=lacc[4:4 + CONV], lru_wa=_block_diag_extract(r["dwabd"]),
                 lru_wx=_block_diag_extract(r["dwxbd"]), loss=r["loss_vec"])
    gpack = _allreduce_small(_pack_small(small, col0))
    loss = jnp.sum(gpack[_ROW_LOSS])

    grads, delta, new_m, new_v = {}, {}, {}, {}
    for n in _BIG:
        if n == "w_in":
            gt = gsum[n].T
            grads[n] = gt.T
            delta[n], new_m[n], new_v[n] = [a.T for a in _adamw(wts[n].T, gt, mom[n].T, var[n].T, "adamw_" + n)]
        else:
            grads[n] = gsum[n]
            delta[n], new_m[n], new_v[n] = _adamw(wts[n], gsum[n], mom[n], var[n], "adamw_" + n)
    dp, mp, vp = _adamw(_pack_small(wts, col0), gpack, _pack_small(mom, col0), _pack_small(var, col0), "adamw_small")
    for dst, pack in ((grads, gpack), (delta, dp), (new_m, mp), (new_v, vp)):
        dst.update(_unpack_small(pack, col0))
    return (loss, r["dx"][None], *[grads[n] for n in _WEIGHTS], *[delta[n] for n in _WEIGHTS],
            *[new_m[n] for n in _WEIGHTS], *[new_v[n] for n in _WEIGHTS])
```

```python
import functools
import math

import jax
import jax.numpy as jnp
import numpy as np
from jax import lax
from jax.experimental import pallas as pl
from jax.experimental.pallas import tpu as pltpu

F32 = jnp.float32
BF = jnp.bfloat16

D = 1024
NH = 8
DH = 128
FF = 4096
CONV = 4
LRU_BLOCKS = 16
LRU_BW = 64
BD = 256
NBD = D // BD
LRU_C = 8.0
EPS = 1e-6
DIN = 7176
DINP = 7296
NCHIP = 4
SLAB = 8
VMEM_CAP = 60 * 1024 * 1024

ADAM_LR = 0.001
ADAM_B1 = 0.9
ADAM_B2 = 0.999
ADAM_EPS = 1e-08
ADAM_WD = 0.01
ADAM_STEP = 10

MESH = pl.DeviceIdType.MESH


def _vmem_limit(nbytes):
    return int(min(VMEM_CAP, max(32 * 1024 * 1024, 3 * nbytes)))


def _nbytes(shape, dtype):
    return int(np.prod(shape)) * jnp.dtype(dtype).itemsize


def _sig(x):
    return 0.5 * jnp.tanh(0.5 * x) + 0.5


def _log1p(u):
    w = 1.0 + u
    return jnp.where(w == 1.0, u, jnp.log(w) * (u / (w - 1.0)))


def _one_minus_sq(a, la):
    return jnp.tanh(-la) * (1.0 + a * a)


def _softplus(z):
    return jnp.maximum(z, 0.0) + _log1p(jnp.exp(-jnp.abs(z)))


_GELU_C = math.sqrt(2.0 / math.pi)


def _gelu(x):
    return 0.5 * x * (1.0 + jnp.tanh(_GELU_C * (x + 0.044715 * x * x * x)))


def _gelu_grad(x):
    t = jnp.tanh(_GELU_C * (x + 0.044715 * x * x * x))
    return 0.5 * (1.0 + t) + 0.5 * x * (1.0 - t * t) * _GELU_C * (1.0 + 3.0 * 0.044715 * x * x)


def _shift_down(x, d, prev8):
    n = x.shape[0]
    row8 = lax.broadcasted_iota(jnp.int32, (SLAB, x.shape[1]), 0)
    y = pltpu.roll(x, d, 0)
    top = jnp.where(row8 < d, pltpu.roll(prev8, d, 0), y[0:SLAB])
    if n == SLAB:
        return top
    return jnp.concatenate([top, y[SLAB:]], axis=0)


def _shift_up(x, d, next8):
    n = x.shape[0]
    row8 = lax.broadcasted_iota(jnp.int32, (SLAB, x.shape[1]), 0)
    y = pltpu.roll(x, n - d, 0)
    bottom = jnp.where(row8 >= SLAB - d, pltpu.roll(next8, SLAB - d, 0), y[n - SLAB:])
    if n == SLAB:
        return bottom
    return jnp.concatenate([y[:n - SLAB], bottom], axis=0)


def _slab_scan_fwd(a, b):
    row = lax.broadcasted_iota(jnp.int32, a.shape, 0)
    for k in (1, 2, 4):
        a_s = pltpu.roll(a, k, 0)
        b_s = pltpu.roll(b, k, 0)
        m = row >= k
        b = jnp.where(m, a * b_s + b, b)
        a = jnp.where(m, a * a_s, a)
    return a, b


def _slab_scan_bwd(a, b):
    row = lax.broadcasted_iota(jnp.int32, a.shape, 0)
    for k in (1, 2, 4):
        a_s = pltpu.roll(a, SLAB - k, 0)
        b_s = pltpu.roll(b, SLAB - k, 0)
        m = row < SLAB - k
        b = jnp.where(m, a * b_s + b, b)
        a = jnp.where(m, a * a_s, a)
    return a, b


_DIMS = {"nn": (((1,), (0,)), ((), ())), "nt": (((1,), (1,)), ((), ())), "tn": (((0,), (0,)), ((), ()))}


def _dot(a, b, mode="nn"):
    return lax.dot_general(a, b, _DIMS[mode], preferred_element_type=F32)


def _mm(a, b, mode, M, N, K, *, name, out_dtype=F32, tm=None, tn=1024, tk=1024,
        a_off=(0, 0), b_off=(0, 0), add=None, epi=None, epi_ins=()):
    if tm is None:
        tm = 2048 if (K <= tk and out_dtype == BF and add is None) else 1024
    tm, tn, tk = min(tm, M), min(tn, N), min(tk, K)
    nk = K // tk
    grid = (M // tm, N // tn, nk)
    if mode == "nn":
        a_spec = pl.BlockSpec((tm, tk), lambda i, j, k: (i + a_off[0], k + a_off[1]))
        b_spec = pl.BlockSpec((tk, tn), lambda i, j, k: (k + b_off[0], j + b_off[1]))
    elif mode == "nt":
        a_spec = pl.BlockSpec((tm, tk), lambda i, j, k: (i + a_off[0], k + a_off[1]))
        b_spec = pl.BlockSpec((tn, tk), lambda i, j, k: (j + b_off[0], k + b_off[1]))
    else:
        a_spec = pl.BlockSpec((tk, tm), lambda i, j, k: (k + a_off[0], i + a_off[1]))
        b_spec = pl.BlockSpec((tk, tn), lambda i, j, k: (k + b_off[0], j + b_off[1]))
    o_spec = pl.BlockSpec((tm, tn), lambda i, j, k: (i, j))
    extra = ([add] if add is not None else []) + list(epi_ins)
    n_extra = len(extra)
    has_add = add is not None

    def body(*refs):
        a_ref, b_ref = refs[0], refs[1]
        ex = refs[2:2 + n_extra]
        o_ref = refs[2 + n_extra]

        def finish(acc):
            if has_add:
                acc = acc + ex[0][...].astype(F32)
            if epi is not None:
                acc = epi(acc, *[e[...] for e in ex[(1 if has_add else 0):]])
            o_ref[...] = acc.astype(o_ref.dtype)

        p = _dot(a_ref[...].astype(BF), b_ref[...].astype(BF), mode)
        if nk == 1:
            finish(p)
        else:
            acc_ref = refs[3 + n_extra]
            k = pl.program_id(2)

            @pl.when(k == 0)
            def _():
                acc_ref[...] = p

            @pl.when(k > 0)
            def _():
                acc_ref[...] += p

            @pl.when(k == nk - 1)
            def _():
                finish(acc_ref[...])

    blk = (_nbytes((tm, tk), a.dtype) + _nbytes((tk, tn), b.dtype) + _nbytes((tm, tn), out_dtype)
           + sum(_nbytes((tm, tn), e.dtype) for e in extra) + 2 * _nbytes((tm, tn), F32))
    return pl.pallas_call(
        body, name=name, grid=grid,
        in_specs=[a_spec, b_spec] + [o_spec] * n_extra,
        out_specs=o_spec,
        out_shape=jax.ShapeDtypeStruct((M, N), out_dtype),
        scratch_shapes=[pltpu.VMEM((tm, tn), F32)] if nk > 1 else [],
        compiler_params=pltpu.CompilerParams(
            dimension_semantics=("parallel", "parallel", "arbitrary"), vmem_limit_bytes=_vmem_limit(blk)),
    )(a, b, *extra)


def _ew(fn, T, tm, ins, consts, outs, accs, *, name, reverse=False):
    tm = min(tm, T)
    nt = T // tm
    n_in, n_c, n_o, n_a = len(ins), len(consts), len(outs), len(accs)

    def row(i):
        return nt - 1 - i if reverse else i

    in_specs = [pl.BlockSpec((tm, w), functools.partial(lambda i, cb: (row(i), cb), cb=cb)) for (_, w, cb) in ins]
    in_specs += [pl.BlockSpec(c.shape, functools.partial(lambda i, nd: (0,) * nd, nd=c.ndim)) for c in consts]
    out_specs = [pl.BlockSpec((tm, w), lambda i: (row(i), 0)) for (w, _) in outs]
    out_specs += [pl.BlockSpec((r, w), lambda i: (0, 0)) for (r, w) in accs]
    out_shape = [jax.ShapeDtypeStruct((T, w), dt) for (w, dt) in outs]
    out_shape += [jax.ShapeDtypeStruct((r, w), F32) for (r, w) in accs]

    def body(*refs):
        in_refs = refs[:n_in]
        c_refs = refs[n_in:n_in + n_c]
        o_refs = refs[n_in + n_c:n_in + n_c + n_o]
        a_refs = refs[n_in + n_c + n_o:]
        ov, av = fn([r[...] for r in in_refs], [r[...] for r in c_refs])
        for r, v in zip(o_refs, ov):
            r[...] = v.astype(r.dtype)
        if n_a:
            i = pl.program_id(0)

            @pl.when(i == 0)
            def _():
                for r, v in zip(a_refs, av):
                    r[...] = v

            @pl.when(i > 0)
            def _():
                for r, v in zip(a_refs, av):
                    r[...] += v

    blk = (sum(_nbytes((tm, w), a.dtype) for (a, w, _) in ins) + sum(_nbytes(c.shape, c.dtype) for c in consts)
           + sum(_nbytes((tm, w), dt) for (w, dt) in outs) + sum(_nbytes(s, F32) for s in accs))
    res = pl.pallas_call(
        body, name=name, grid=(nt,), in_specs=in_specs, out_specs=out_specs, out_shape=out_shape,
        compiler_params=pltpu.CompilerParams(
            dimension_semantics=("arbitrary",), vmem_limit_bytes=_vmem_limit(blk)),
    )(*[a for (a, _, _) in ins], *consts)
    return res


def _colsum(v):
    return jnp.sum(v, axis=0, keepdims=True)


FGATE_GROUP = 4


def _fgate_fwd(fl, fb, T, tm=512):
    tm = min(tm, T)

    def body(fl_ref, fb_ref, f_ref, carry_ref):
        i = pl.program_id(0)

        @pl.when(i == 0)
        def _():
            carry_ref[...] = jnp.zeros_like(carry_ref)

        rows = FGATE_GROUP * SLAB
        sub = lax.broadcasted_iota(jnp.int32, (rows, DH), 0) % SLAB

        def group(s, carry):
            r0 = pl.multiple_of(s * rows, rows)
            z = fl_ref[pl.ds(r0, rows), :] + fb_ref[...]
            c = jnp.minimum(z, 0.0) - _log1p(jnp.exp(-jnp.abs(z)))
            for k in (1, 2, 4):
                c = c + jnp.where(sub >= k, pltpu.roll(c, k, 0), 0.0)
            for u in range(FGATE_GROUP):
                cu = c[u * SLAB:(u + 1) * SLAB] + carry
                f_ref[pl.ds(r0 + u * SLAB, SLAB), :] = cu
                carry = cu[SLAB - 1:SLAB, :]
            return carry

        carry_ref[0:1, :] = lax.fori_loop(0, tm // rows, group, carry_ref[0:1, :])

    return pl.pallas_call(
        body, name="fgate_fwd", grid=(T // tm,),
        in_specs=[pl.BlockSpec((tm, DH), lambda i: (i, 0)), pl.BlockSpec((1, DH), lambda i: (0, 0))],
        out_specs=pl.BlockSpec((tm, DH), lambda i: (i, 0)),
        out_shape=jax.ShapeDtypeStruct((T, DH), F32),
        scratch_shapes=[pltpu.VMEM((SLAB, DH), F32)],
        compiler_params=pltpu.CompilerParams(dimension_semantics=("arbitrary",)),
    )(fl, fb)


def _fgate_bwd(dF, fl, fb, T, tm=512):
    tm = min(tm, T)
    nt = T // tm

    def body(df_ref, fl_ref, fb_ref, o_ref, acc_ref, carry_ref):
        i = pl.program_id(0)

        @pl.when(i == 0)
        def _():
            carry_ref[...] = jnp.zeros_like(carry_ref)
            acc_ref[...] = jnp.zeros_like(acc_ref)

        rows = FGATE_GROUP * SLAB
        sub = lax.broadcasted_iota(jnp.int32, (rows, DH), 0) % SLAB

        def group(n, carry):
            g_next, acc = carry
            r0 = pl.multiple_of((tm // rows - 1 - n) * rows, rows)
            c = df_ref[pl.ds(r0, rows), :]
            for k in (1, 2, 4):
                c = c + jnp.where(sub < SLAB - k, pltpu.roll(c, rows - k, 0), 0.0)
            sg = _sig(-(fl_ref[pl.ds(r0, rows), :] + fb_ref[...]))
            for u in reversed(range(FGATE_GROUP)):
                cu = c[u * SLAB:(u + 1) * SLAB] + g_next
                dfl = cu * sg[u * SLAB:(u + 1) * SLAB]
                o_ref[pl.ds(r0 + u * SLAB, SLAB), :] = dfl.astype(o_ref.dtype)
                acc = acc + _colsum(dfl)
                g_next = cu[0:1, :]
            return g_next, acc

        g, acc = lax.fori_loop(0, tm // rows, group, (carry_ref[0:1, :], jnp.zeros((1, DH), F32)))
        carry_ref[0:1, :] = g
        acc_ref[...] += acc

    return pl.pallas_call(
        body, name="fgate_bwd", grid=(nt,),
        in_specs=[pl.BlockSpec((tm, DH), lambda i: (nt - 1 - i, 0)), pl.BlockSpec((tm, DH), lambda i: (nt - 1 - i, 0)),
                  pl.BlockSpec((1, DH), lambda i: (0, 0))],
        out_specs=[pl.BlockSpec((tm, DH), lambda i: (nt - 1 - i, 0)), pl.BlockSpec((1, DH), lambda i: (0, 0))],
        out_shape=[jax.ShapeDtypeStruct((T, DH), BF), jax.ShapeDtypeStruct((1, DH), F32)],
        scratch_shapes=[pltpu.VMEM((SLAB, DH), F32)],
        compiler_params=pltpu.CompilerParams(dimension_semantics=("arbitrary",)),
    )(dF, fl, fb)


def _conv(x, prev8, cw, cb):
    xs = [x] + [_shift_down(x, d, prev8) for d in (1, 2, 3)]
    xa = cb + cw[3:4, :] * xs[0] + cw[2:3, :] * xs[1] + cw[1:2, :] * xs[2] + cw[0:1, :] * xs[3]
    return xa, xs


def _lru_gates(xa_g, wa_g, wx_g, ba_g, bx_g, sp_g):
    xb = xa_g.astype(BF)
    r = _sig(_dot(xb, wa_g) + ba_g)
    ig = _sig(_dot(xb, wx_g) + bx_g)
    la = -LRU_C * r * sp_g
    a = jnp.exp(la)
    mult = jnp.sqrt(_one_minus_sq(a, la))
    return r, ig, a, mult


def _lru_fwd(xg, cw, vec, wabd, wxbd, T, tm=256):
    tm = min(tm, T)
    nsl = tm // SLAB

    def body(x_ref, xp_ref, cw_ref, vec_ref, wa_ref, wx_ref, h_ref, a_s, b_s, carry_ref):
        i = pl.program_id(0)

        @pl.when(i == 0)
        def _():
            carry_ref[...] = jnp.zeros_like(carry_ref)

        x = x_ref[...].astype(F32)
        prev8 = jnp.where(i > 0, xp_ref[SLAB:, :].astype(F32), 0.0)
        vec_v = vec_ref[...]
        xa, _ = _conv(x, prev8, cw_ref[...], vec_v[0:1, :])
        sp = _softplus(-vec_v[3:4, :])
        for g in range(NBD):
            sl = slice(g * BD, (g + 1) * BD)
            _, ig, a, mult = _lru_gates(xa[:, sl], wa_ref[g], wx_ref[g], vec_v[1:2, sl], vec_v[2:3, sl], sp[:, sl])
            a_s[:, sl] = a
            b_s[:, sl] = mult * ig * xa[:, sl]

        def slab(s, carry):
            r0 = pl.multiple_of(s * SLAB, SLAB)
            A, B = _slab_scan_fwd(a_s[pl.ds(r0, SLAB), :], b_s[pl.ds(r0, SLAB), :])
            h = A * carry + B
            h_ref[pl.ds(r0, SLAB), :] = h
            return h[SLAB - 1:SLAB, :]

        carry_ref[0:1, :] = lax.fori_loop(0, nsl, slab, carry_ref[0:1, :])

    blk = 5 * _nbytes((tm, D), F32) + 2 * _nbytes((NBD, BD, BD), BF)
    return pl.pallas_call(
        body, name="lru_fwd", grid=(T // tm,),
        in_specs=[pl.BlockSpec((tm, D), lambda i: (i, 0)),
                  pl.BlockSpec((2 * SLAB, D), lambda i: (jnp.maximum(i * (nsl // 2) - 1, 0), 0)),
                  pl.BlockSpec((CONV, D), lambda i: (0, 0)),
                  pl.BlockSpec((SLAB, D), lambda i: (0, 0)),
                  pl.BlockSpec((NBD, BD, BD), lambda i: (0, 0, 0)),
                  pl.BlockSpec((NBD, BD, BD), lambda i: (0, 0, 0))],
        out_specs=pl.BlockSpec((tm, D), lambda i: (i, 0)),
        out_shape=jax.ShapeDtypeStruct((T, D), F32),
        scratch_shapes=[pltpu.VMEM((tm, D), F32), pltpu.VMEM((tm, D), F32), pltpu.VMEM((SLAB, D), F32)],
        compiler_params=pltpu.CompilerParams(dimension_semantics=("arbitrary",), vmem_limit_bytes=_vmem_limit(blk)),
    )(xg, xg, cw, vec, wabd, wxbd)


def _lru_bwd(xg, h, dha, cw, vec, wabd, wxbd, T, tm=256):
    tm = min(tm, T)
    nsl = tm // SLAB
    nt = T // tm

    def body(x_ref, xp_ref, h_ref, hp_ref, dh_ref, cw_ref, vec_ref, wa_ref, wx_ref,
             dx_ref, dwa_ref, dwx_ref, acc_ref, a_s, b_s, g_s, dxa_s, carry_ref, dxan_ref):
        n = pl.program_id(0)
        it = nt - 1 - n

        @pl.when(n == 0)
        def _():
            carry_ref[...] = jnp.zeros_like(carry_ref)
            dxan_ref[...] = jnp.zeros_like(dxan_ref)
            dwa_ref[...] = jnp.zeros_like(dwa_ref)
            dwx_ref[...] = jnp.zeros_like(dwx_ref)
            acc_ref[...] = jnp.zeros_like(acc_ref)

        x = x_ref[...].astype(F32)
        prev8 = jnp.where(it > 0, xp_ref[SLAB:, :].astype(F32), 0.0)
        hprev8 = jnp.where(it > 0, hp_ref[...], 0.0)
        vec_v = vec_ref[...]
        cw_v = cw_ref[...]
        xa, xs = _conv(x, prev8, cw_v, vec_v[0:1, :])
        sp = _softplus(-vec_v[3:4, :])
        gates = []
        for g in range(NBD):
            sl = slice(g * BD, (g + 1) * BD)
            r, ig, a, mult = _lru_gates(xa[:, sl], wa_ref[g], wx_ref[g], vec_v[1:2, sl], vec_v[2:3, sl], sp[:, sl])
            gates.append((r, ig, a, mult))
            a_s[:, sl] = a
        a_next = _shift_up(a_s[...], 1, carry_ref[...])
        a_s[...] = a_next
        b_s[...] = dh_ref[...]

        def slab(m, carry):
            r0 = pl.multiple_of((nsl - 1 - m) * SLAB, SLAB)
            A, B = _slab_scan_bwd(a_s[pl.ds(r0, SLAB), :], b_s[pl.ds(r0, SLAB), :])
            gg = A * carry + B
            g_s[pl.ds(r0, SLAB), :] = gg
            return gg[0:1, :]

        g_first = lax.fori_loop(0, nsl, slab, carry_ref[1:2, :])
        gt = g_s[...]
        h_prev = _shift_down(h_ref[...], 1, hprev8)
        dba = []
        dbx = []
        dsp = []
        for g in range(NBD):
            sl = slice(g * BD, (g + 1) * BD)
            r, ig, a, mult = gates[g]
            xa_g = xa[:, sl]
            g_g = gt[:, sl]
            da = g_g * h_prev[:, sl]
            dmult = g_g * ig * xa_g
            di = g_g * mult * xa_g
            dxa_g = g_g * mult * ig
            dla = da * a - dmult * (a * a / mult)
            dr = dla * (-LRU_C) * sp[:, sl]
            dsp.append(_colsum(dla * (-LRU_C) * r))
            dra = (dr * r * (1.0 - r))
            dix = (di * ig * (1.0 - ig))
            dba.append(_colsum(dra))
            dbx.append(_colsum(dix))
            dra_b = dra.astype(BF)
            dix_b = dix.astype(BF)
            xb = xa_g.astype(BF)
            dxa_g = dxa_g + _dot(dra_b, wa_ref[g], "nt") + _dot(dix_b, wx_ref[g], "nt")
            dwa_ref[g] += _dot(xb, dra_b, "tn")
            dwx_ref[g] += _dot(xb, dix_b, "tn")
            dxa_s[:, sl] = dxa_g
        dxa = dxa_s[...]
        nxt = dxan_ref[...]
        dx = (cw_v[3:4, :] * dxa + cw_v[2:3, :] * _shift_up(dxa, 1, nxt)
              + cw_v[1:2, :] * _shift_up(dxa, 2, nxt) + cw_v[0:1, :] * _shift_up(dxa, 3, nxt))
        dx_ref[...] = dx.astype(dx_ref.dtype)
        acc_ref[0:1, :] += jnp.concatenate(dba, axis=1)
        acc_ref[1:2, :] += jnp.concatenate(dbx, axis=1)
        acc_ref[2:3, :] += jnp.concatenate(dsp, axis=1)
        acc_ref[3:4, :] += _colsum(dxa)
        for k in range(CONV):
            acc_ref[4 + k:5 + k, :] += _colsum(dxa * xs[CONV - 1 - k])
        dxan_ref[...] = dxa[0:SLAB, :]
        a_first = jnp.concatenate([gates[g][2][0:1, :] for g in range(NBD)], axis=1)
        carry_ref[0:1, :] = a_first
        carry_ref[1:2, :] = g_first

        @pl.when(n == nt - 1)
        def _():
            acc_ref[2:3, :] = acc_ref[2:3, :] * (-_sig(-vec_v[3:4, :]))

    rowblk = lambda i: (nt - 1 - i, 0)
    prevblk = lambda i: (jnp.maximum((nt - 1 - i) * nsl - 1, 0), 0)
    c2 = lambda i: (0, 0)
    c3 = lambda i: (0, 0, 0)
    blk = 12 * _nbytes((tm, D), F32) + 6 * _nbytes((NBD, BD, BD), F32)
    return pl.pallas_call(
        body, name="lru_bwd", grid=(nt,),
        in_specs=[pl.BlockSpec((tm, D), rowblk),
                  pl.BlockSpec((2 * SLAB, D), lambda i: (jnp.maximum((nt - 1 - i) * (nsl // 2) - 1, 0), 0)),
                  pl.BlockSpec((tm, D), rowblk), pl.BlockSpec((SLAB, D), prevblk),
                  pl.BlockSpec((tm, D), rowblk),
                  pl.BlockSpec((CONV, D), c2), pl.BlockSpec((SLAB, D), c2),
                  pl.BlockSpec((NBD, BD, BD), c3), pl.BlockSpec((NBD, BD, BD), c3)],
        out_specs=[pl.BlockSpec((tm, D), rowblk), pl.BlockSpec((NBD, BD, BD), c3), pl.BlockSpec((NBD, BD, BD), c3),
                   pl.BlockSpec((16, D), c2)],
        out_shape=[jax.ShapeDtypeStruct((T, D), BF), jax.ShapeDtypeStruct((NBD, BD, BD), F32),
                   jax.ShapeDtypeStruct((NBD, BD, BD), F32), jax.ShapeDtypeStruct((16, D), F32)],
        scratch_shapes=[pltpu.VMEM((tm, D), F32), pltpu.VMEM((tm, D), F32), pltpu.VMEM((tm, D), F32),
                        pltpu.VMEM((tm, D), F32), pltpu.VMEM((SLAB, D), F32), pltpu.VMEM((SLAB, D), F32)],
        compiler_params=pltpu.CompilerParams(dimension_semantics=("arbitrary",), vmem_limit_bytes=_vmem_limit(blk)),
    )(xg, xg, h, h, dha, cw, vec, wabd, wxbd)


_SCALE = 1.0 / math.sqrt(DH)


_ANY = pl.BlockSpec(memory_space=pl.ANY)


class _Side:
    def __init__(self, srcs, outs, nsem, copies):
        self.srcs, self.outs, self.nsem, self.copies = list(srcs), list(outs), nsem, copies


def _pallas(body, operands, *, name, grid, in_specs, out_specs, out_shape, scratch_shapes=(), semantics,
            vmem=None, side=None):
    if side is None:
        return pl.pallas_call(
            body, name=name, grid=grid, in_specs=in_specs, out_specs=out_specs, out_shape=out_shape,
            scratch_shapes=list(scratch_shapes),
            compiler_params=pltpu.CompilerParams(dimension_semantics=semantics, vmem_limit_bytes=vmem),
        )(*operands)
    n_in, n_out, n_scr = len(in_specs), len(out_specs), len(scratch_shapes)
    ns, no = len(side.srcs), len(side.outs)

    def hosted(*refs):
        ins, refs = refs[:n_in], refs[n_in:]
        sin, refs = refs[:ns], refs[ns:]
        outs, refs = refs[:n_out], refs[n_out:]
        sout, refs = refs[:no], refs[no:]
        scr, (send, recv) = refs[:n_scr], refs[n_scr:]
        ids = [pl.program_id(a) for a in range(len(grid))]
        first = functools.reduce(jnp.logical_and, [i == 0 for i in ids])
        last = functools.reduce(jnp.logical_and, [i == g - 1 for i, g in zip(ids, grid)])

        @pl.when(first)
        def _():
            for cp in side.copies(sin, sout, send, recv):
                cp.start()

        body(*ins, *outs, *scr)

        @pl.when(last)
        def _():
            for cp in side.copies(sin, sout, send, recv):
                cp.wait()

    return pl.pallas_call(
        hosted, name=name, grid=grid, in_specs=list(in_specs) + [_ANY] * ns, out_specs=list(out_specs) + [_ANY] * no,
        out_shape=list(out_shape) + side.outs,
        scratch_shapes=list(scratch_shapes) + [pltpu.SemaphoreType.DMA((side.nsem,)), pltpu.SemaphoreType.DMA((side.nsem,))],
        compiler_params=pltpu.CompilerParams(dimension_semantics=("arbitrary",) * len(grid), vmem_limit_bytes=vmem),
    )(*operands, *side.srcs)


DA = 2 * DH
_LOG2E = math.log2(math.e)
_C2 = _SCALE * _LOG2E


def _aug_fn(ins, cs):
    q, k, fcum = ins
    g_all = fcum * _LOG2E
    lane = lax.broadcasted_iota(jnp.int32, (q.shape[0], DH), 1)
    qa, ka = [], []
    for hd in range(NH):
        g = g_all[:, hd:hd + 1]
        hi = g.astype(BF).astype(F32)
        mid = (g - hi).astype(BF).astype(F32)
        lo = ((g - hi) - mid).astype(BF).astype(F32)
        qx = jnp.where(lane == 0, hi, jnp.where(lane == 1, mid, jnp.where(lane == 2, lo,
                                                                          jnp.where(lane < 6, 1.0, 0.0))))
        kx = jnp.where(lane < 3, 1.0, jnp.where(lane == 3, -hi, jnp.where(lane == 4, -mid,
                                                                          jnp.where(lane == 5, -lo, 0.0))))
        qa += [q[:, hd * DH:(hd + 1) * DH], qx.astype(BF)]
        ka += [k[:, hd * DH:(hd + 1) * DH], kx.astype(BF)]
    return [jnp.concatenate(qa, axis=1), jnp.concatenate(ka, axis=1)], []


_KA_ONES = DH + 3
KT_ONES = 16


def _attn_fwd(qa, ka, qkv, T, blk=512, side=None):
    blk = min(blk, T)
    nb = T // blk
    half = blk // 2

    def body(q_ref, k_ref, vn_ref, o_ref, lse_ref, v_ref):
        i = pl.program_id(1)

        @pl.when(i == 0)
        def _():
            for jj in range(nb):
                v_ref[jj] = vn_ref[jj * blk:(jj + 1) * blk, :].astype(F32).T.astype(BF)

        q = q_ref[...]

        def scores(j):
            r0 = pl.multiple_of(j * blk, blk)
            return _dot(k_ref[pl.ds(r0, blk), :], q, "nt")

        def update(s, vj, carry):
            m, l, acc = carry
            m_new = jnp.maximum(m, jnp.max(s, axis=0, keepdims=True))
            alpha = jnp.exp2(m - m_new)
            p = jnp.exp2(s - m_new)
            l = alpha * l + jnp.sum(p, axis=0, keepdims=True)
            acc = alpha * acc + _dot(vj, p.astype(BF))
            return m_new, l, acc

        def step(j, st):
            return update(scores(j), v_ref[j], st)

        init = (jnp.full((1, blk), -jnp.inf, F32), jnp.zeros((1, blk), F32), jnp.zeros((DH, blk), F32))
        carry = lax.fori_loop(0, i, step, init)
        last = scores(i)
        vi = v_ref[i]
        rk = lax.broadcasted_iota(jnp.int32, (half, blk), 0)
        cq = lax.broadcasted_iota(jnp.int32, (half, blk), 1)
        m, l, acc = update(jnp.where(cq >= rk, last[:half], -jnp.inf), vi[:, :half], carry)
        s2 = jnp.where(cq[:, :half] >= rk[:, :half], last[half:, half:], -jnp.inf)
        m2, l2, acc2 = update(s2, vi[:, half:], (m[:, half:], l[:, half:], acc[:, half:]))
        m = jnp.concatenate([m[:, :half], m2], axis=1)
        l = jnp.concatenate([l[:, :half], l2], axis=1)
        acc = jnp.concatenate([acc[:, :half], acc2], axis=1)
        o_ref[...] = (acc / l).T.astype(o_ref.dtype)
        lse_ref[...] = m + jnp.log(l) * _LOG2E

    vm = _nbytes((T, DA), BF) + 2 * _nbytes((T, DH), BF) + 6 * _nbytes((blk, blk), F32)
    return _pallas(
        body, (qa, ka, qkv), name="attn_fwd", grid=(NH, nb),
        in_specs=[pl.BlockSpec((blk, DA), lambda h, i: (i, h)),
                  pl.BlockSpec((T, DA), lambda h, i: (0, h)),
                  pl.BlockSpec((T, DH), lambda h, i: (0, 2 * NH + h))],
        out_specs=[pl.BlockSpec((blk, DH), lambda h, i: (i, h)),
                   pl.BlockSpec((None, None, 1, blk), lambda h, i: (h, i, 0, 0))],
        out_shape=[jax.ShapeDtypeStruct((T, D), BF), jax.ShapeDtypeStruct((NH, nb, 1, blk), F32)],
        scratch_shapes=[pltpu.VMEM((nb, DH, blk), BF)],
        semantics=("parallel", "arbitrary"), vmem=_vmem_limit(vm), side=side)


def _attn_bwd(qa, ka, qkv, do, lrow, drow, T, blk=512, side=None):
    blk = min(blk, T)
    nb = T // blk

    def body(ka_ref, v_ref, qa_ref, do_ref, l_ref, d_ref, dq_ref, dk_ref, dv_ref, dfs_ref, dft_ref, dq_s):
        j = pl.program_id(1)

        @pl.when(j == 0)
        def _():
            dq_s[...] = jnp.zeros_like(dq_s)

        row = lax.broadcasted_iota(jnp.int32, (DH + KT_ONES, blk), 0)
        dq_scale = jnp.where(row < DH, _SCALE, 1.0)

        kaj = ka_ref[...]
        ktj = jnp.concatenate([kaj[:, :DH].astype(F32).T.astype(BF), jnp.ones((KT_ONES, blk), BF)], axis=0)
        vj = v_ref[...]

        def step(i, carry):
            dka, dv = carry
            r0 = pl.multiple_of(i * blk, blk)
            qi = qa_ref[pl.ds(r0, blk), :]
            doi = do_ref[pl.ds(r0, blk), :]
            pt = jnp.exp2(_dot(kaj, qi, "nt") - l_ref[i])
            dv = dv + _dot(pt.astype(BF), doi)
            dpt = _dot(vj, doi, "nt")
            dst = pt * (dpt - d_ref[i])
            dsb = dst.astype(BF)
            dka = dka + _dot(dsb, qi)
            dq_s[i] += _dot(ktj, dsb) * dq_scale
            return dka, dv

        def diagonal():
            half = blk // 2
            r0 = pl.multiple_of(j * blk, blk)
            qi = qa_ref[pl.ds(r0, blk), :]
            doi = do_ref[pl.ds(r0, blk), :]
            lr, dr = l_ref[j], d_ref[j]
            rk = lax.broadcasted_iota(jnp.int32, (half, blk), 0)
            cq = lax.broadcasted_iota(jnp.int32, (half, blk), 1)

            def quarter(ka, v, q, do, l2, d2, keep):
                pt = jnp.exp2(jnp.where(keep, _dot(ka, q, "nt") - l2, -jnp.inf))
                dsb = (pt * (_dot(v, do, "nt") - d2)).astype(BF)
                return _dot(dsb, q), _dot(pt.astype(BF), do), dsb

            dka1, dv1, ds1 = quarter(kaj[:half], vj[:half], qi, doi, lr, dr, cq >= rk)
            dka2, dv2, ds2 = quarter(kaj[half:], vj[half:], qi[half:], doi[half:], lr[:, half:], dr[:, half:],
                                     cq[:, :half] >= rk[:, :half])
            dq2 = jnp.concatenate([jnp.zeros((DH + KT_ONES, half), F32), _dot(ktj[:, half:], ds2)], axis=1)
            dq_s[j] += (_dot(ktj[:, :half], ds1) + dq2) * dq_scale
            return jnp.concatenate([dka1, dka2], axis=0), jnp.concatenate([dv1, dv2], axis=0)

        dka, dv = lax.fori_loop(j + 1, nb, step, diagonal())
        dk_ref[...] = (dka[:, :DH] * (1.0 / _LOG2E)).astype(dk_ref.dtype)
        dv_ref[...] = dv.astype(dv_ref.dtype)
        dfs_ref[...] = dka[:, DH:].T[_KA_ONES - DH:_KA_ONES - DH + 1, :]

        @pl.when(j == nb - 1)
        def _():
            for ii in range(nb):
                t = dq_s[ii]
                dq_ref[ii * blk:(ii + 1) * blk, :] = t[:DH].T.astype(dq_ref.dtype)
                dft_ref[ii] = t[DH:DH + 1]

    rowv = pl.BlockSpec((None, nb, 1, blk), lambda h, j: (h, 0, 0, 0))
    vm = (_nbytes((T, DA), BF) + 2 * _nbytes((T, DH), BF) + _nbytes((T, DH + KT_ONES), F32)
          + 8 * _nbytes((blk, blk), F32))
    return _pallas(
        body, (ka, qkv, qa, do, lrow, drow), name="attn_bwd", grid=(NH, nb),
        in_specs=[pl.BlockSpec((blk, DA), lambda h, j: (j, h)),
                  pl.BlockSpec((blk, DH), lambda h, j: (j, 2 * NH + h)),
                  pl.BlockSpec((T, DA), lambda h, j: (0, h)),
                  pl.BlockSpec((T, DH), lambda h, j: (0, h)),
                  rowv, rowv],
        out_specs=[pl.BlockSpec((T, DH), lambda h, j: (0, h)),
                   pl.BlockSpec((blk, DH), lambda h, j: (j, h)),
                   pl.BlockSpec((blk, DH), lambda h, j: (j, h)),
                   pl.BlockSpec((None, None, 1, blk), lambda h, j: (h, j, 0, 0)), rowv],
        out_shape=[jax.ShapeDtypeStruct((T, D), BF), jax.ShapeDtypeStruct((T, D), BF),
                   jax.ShapeDtypeStruct((T, D), BF), jax.ShapeDtypeStruct((NH, nb, 1, blk), F32),
                   jax.ShapeDtypeStruct((NH, nb, 1, blk), F32)],
        scratch_shapes=[pltpu.VMEM((nb, DH + KT_ONES, blk), F32)],
        semantics=("parallel", "arbitrary"), vmem=_vmem_limit(vm), side=side)


def _norm_fn(ins, cs):
    x, = ins
    g, = cs
    r = lax.rsqrt(jnp.mean(x * x, axis=-1, keepdims=True) + EPS)
    return [x * r * g], []


def _norm_bwd_fn(ins, cs):
    x, dy, dres = ins
    g, = cs
    r = lax.rsqrt(jnp.mean(x * x, axis=-1, keepdims=True) + EPS)
    xh = x * r
    dxh = dy * g
    dx = dres + r * (dxh - xh * jnp.mean(dxh * xh, axis=-1, keepdims=True))
    return [dx], [_colsum(dy * xh)]


def _final_fn(ins, cs):
    x2, tgt = ins
    g, = cs
    r = lax.rsqrt(jnp.mean(x2 * x2, axis=-1, keepdims=True) + EPS)
    xh = x2 * r
    e = xh * g - tgt
    dy = e * (1.0 / D)
    dxh = dy * g
    dx2 = r * (dxh - xh * jnp.mean(dxh * xh, axis=-1, keepdims=True))
    return [dx2], [_colsum(0.5 * e * e * (1.0 / D)), _colsum(dy * xh)]


def _z_fn(ins, cs):
    g, h = [v.astype(F32) for v in ins]
    return [_gelu(g) * h], []


def _mix_fn(ins, cs):
    gates, ya, yb = [v.astype(F32) for v in ins]
    return [_sig(gates[:, :D]) * ya + _sig(gates[:, D:]) * yb], []


def _mix_bwd_fn(ins, cs):
    dmix, gates, ya, yb = [v.astype(F32) for v in ins]
    ga = _sig(gates[:, :D])
    gb = _sig(gates[:, D:])
    dgates = jnp.concatenate([dmix * ya * ga * (1.0 - ga), dmix * yb * gb * (1.0 - gb)], axis=1)
    return [dmix * ga, dmix * gb, dgates], []


def _z_bwd_fn(ins, cs):
    dz, g, h = [v.astype(F32) for v in ins]
    return [dz * _gelu(g), dz * h * _gelu_grad(g)], []


def _delta_fn(ins, cs):
    do, o = ins
    p = do.astype(F32) * o.astype(F32)
    lane = lax.broadcasted_iota(jnp.int32, (p.shape[0], DH), 1)
    out = jnp.zeros((p.shape[0], DH), F32)
    for hd in range(NH):
        s = jnp.sum(p[:, hd * DH:(hd + 1) * DH], axis=1, keepdims=True)
        out = jnp.where(lane == hd, s, out)
    return [out], []


def _du_all(pieces, win, T, tm=256, side=None):
    tm = min(tm, T)
    n = len(pieces)

    def body(*refs):
        w_ref, o_ref = refs[n], refs[n + 1]
        acc = None
        for (a, off), a_ref in zip(pieces, refs[:n]):
            d = _dot(a_ref[...].astype(BF), w_ref[:, off:off + a.shape[1]], "nt")
            acc = d if acc is None else acc + d
        o_ref[...] = acc

    vm = (sum(_nbytes((tm, a.shape[1]), a.dtype) for a, _ in pieces) + _nbytes(win.shape, win.dtype)
          + 2 * _nbytes((tm, D), F32))
    return _pallas(
        body, tuple(a for a, _ in pieces) + (win,), name="du_all", grid=(T // tm,),
        in_specs=[pl.BlockSpec((tm, a.shape[1]), lambda i: (i, 0)) for a, _ in pieces]
        + [pl.BlockSpec(win.shape, lambda i: (0, 0))],
        out_specs=[pl.BlockSpec((tm, D), lambda i: (i, 0))],
        out_shape=[jax.ShapeDtypeStruct((T, D), F32)],
        semantics=("arbitrary",), vmem=int(min(VMEM_CAP, 2 * vm + (4 << 20))), side=side)


def _local_step(x, tgt, w, T, blk=1024, dist=None):
    blk = min(blk, T)
    nb = T // blk
    win = w["win"]

    u, = _ew(_norm_fn, T, 512, [(x, D, 0)], [w["g_mix"]], [(D, BF)], [], name="norm_mix")
    xg = _mm(u, win, "nn", T, 2 * D, D, name="proj_lru", out_dtype=BF)
    qkv = _mm(u, win, "nn", T, 3 * D, D, name="proj_qkv", out_dtype=BF, b_off=(0, 2),
              epi=lambda acc: acc * jnp.where(pl.program_id(1) == 0, _C2, 1.0))
    gates = _mm(u, win, "nn", T, 2 * D, D, name="proj_gates", b_off=(0, 5), out_dtype=BF)
    fl = _mm(u, win, "nn", T, DH, D, name="proj_f", tn=DH, b_off=(0, 7 * D // DH))
    fcum = _fgate_fwd(fl, w["fb"], T)
    qa, ka = _ew(_aug_fn, T, 256, [(qkv, D, 0), (qkv, D, 1), (fcum, DH, 0)], [],
                 [(NH * DA, BF), (NH * DA, BF)], [], name="attn_augment")

    h = _lru_fwd(xg, w["cw"], w["vec"], w["wabd"], w["wxbd"], T)
    ob, lse, *landed = _attn_fwd(qa, ka, qkv, T, blk, side=dist.weights_side() if dist else None)
    if dist:
        w = dict(w, **dist.weights_landed(landed))
    z, = _ew(_z_fn, T, 512, [(xg, D, 1), (h, D, 0)], [], [(D, BF)], [], name="lru_gelu")
    ya = _mm(z, w["wa"], "nn", T, D, D, name="branch_a", out_dtype=BF)
    yb = _mm(ob, w["wb"], "nn", T, D, D, name="branch_b", out_dtype=BF)
    mix, = _ew(_mix_fn, T, 256, [(gates, 2 * D, 0), (ya, D, 0), (yb, D, 0)], [], [(D, BF)], [], name="mix")
    x1 = _mm(mix, w["wout"], "nn", T, D, D, name="out_proj", add=x)
    m, = _ew(_norm_fn, T, 512, [(x1, D, 0)], [w["g_mlp"]], [(D, BF)], [], name="norm_mlp")
    hh = _mm(m, w["wup"], "nn", T, FF, D, name="mlp_up", out_dtype=BF,
             epi=lambda acc: jnp.square(jnp.maximum(acc, 0.0)))
    x2 = _mm(hh, w["wdown"], "nn", T, D, FF, name="mlp_down", add=x1, tk=2048)
    dx2, loss_vec, dg_fin = _ew(_final_fn, T, 256, [(x2, D, 0), (tgt, D, 0)], [w["g_fin"]], [(D, F32)],
                                [(1, D), (1, D)], name="final_norm_loss")

    dhpre = _mm(dx2, w["wdown"], "nt", T, FF, D, name="mlp_down_bwd", out_dtype=BF,
                epi=lambda acc, h2: acc * (2.0 * jnp.sqrt(h2.astype(F32))), epi_ins=[hh])
    dwdown = _mm(hh, dx2, "tn", FF, D, T, name="dw_down", out_dtype=BF)
    dwup = _mm(m, dhpre, "tn", D, FF, T, name="dw_up", out_dtype=BF)
    dm = _mm(dhpre, w["wup"], "nt", T, D, FF, name="mlp_up_bwd", tk=2048)
    dx1, dg_mlp = _ew(_norm_bwd_fn, T, 256, [(x1, D, 0), (dm, D, 0), (dx2, D, 0)], [w["g_mlp"]], [(D, F32)],
                      [(1, D)], name="norm_mlp_bwd")

    dmix = _mm(dx1, w["wout"], "nt", T, D, D, name="out_proj_bwd", out_dtype=BF)
    dwout = _mm(mix, dx1, "tn", D, D, T, name="dw_out", out_dtype=BF)
    dya, dyb, dgates = _ew(_mix_bwd_fn, T, 256, [(dmix, D, 0), (gates, 2 * D, 0), (ya, D, 0), (yb, D, 0)], [],
                           [(D, BF), (D, BF), (2 * D, BF)], [], name="mix_bwd")
    dob = _mm(dyb, w["wb"], "nt", T, D, D, name="branch_b_bwd", out_dtype=BF)
    dwb = _mm(ob, dyb, "tn", D, D, T, name="dw_b", out_dtype=BF)
    dz = _mm(dya, w["wa"], "nt", T, D, D, name="branch_a_bwd", out_dtype=BF)
    dwa = _mm(z, dya, "tn", D, D, T, name="dw_a", out_dtype=BF)
    dha, dglru = _ew(_z_bwd_fn, T, 256, [(dz, D, 0), (xg, D, 1), (h, D, 0)], [], [(D, F32), (D, BF)], [],
                     name="lru_gelu_bwd")

    delta, = _ew(_delta_fn, T, 512, [(dob, D, 0), (ob, D, 0)], [], [(DH, F32)], [], name="attn_delta")
    drow = delta[:, :NH].T.reshape(NH, nb, 1, blk)
    big = dict(w_branch_a=dwa, w_branch_b=dwb, w_out=dwout, w_up=dwup, w_down=dwdown)
    side = dist.grads_side(big) if dist else None
    dq, dk, dv, dfs, dft, *landed = _attn_bwd(qa, ka, qkv, dob, lse, drow, T, blk, side=side)
    if dist:
        big = dist.grads_landed(side, landed)
    dfcum = jnp.pad((dft - dfs).reshape(NH, T).T, ((0, 0), (0, DH - NH)))
    dfl, dfb = _fgate_bwd(dfcum, fl, w["fb"], T)

    dxl, dwabd, dwxbd, lacc = _lru_bwd(xg, h, dha, w["cw"], w["vec"], w["wabd"], w["wxbd"], T)

    dproj = ((dxl, 0), (dglru, D), (dq, 2 * D), (dk, 3 * D), (dv, 4 * D), (dgates, 5 * D), (dfl, 7 * D))
    pieces = [_mm(u, p, "tn", D, p.shape[1], T, name="dw_in_%d" % n, out_dtype=BF)
              for n, (p, _) in enumerate(dproj)]
    pieces[-1] = pieces[-1][:, :NH]
    dwin = dict(w_in=jnp.concatenate(pieces, axis=1))
    side = dist.grads_side(dwin) if dist else None
    du, *landed = _du_all(dproj, win, T, side=side)
    big.update(dist.grads_landed(side, landed) if dist else dwin)
    dx, dg_mix = _ew(_norm_bwd_fn, T, 256, [(x, D, 0), (du, D, 0), (dx1, D, 0)], [w["g_mix"]], [(D, F32)],
                     [(1, D)], name="norm_mix_bwd")

    return dict(dx=dx, big=big, dwabd=dwabd, dwxbd=dwxbd, lacc=lacc, dfb=dfb, dg_mix=dg_mix, dg_mlp=dg_mlp,
                dg_fin=dg_fin, loss_vec=loss_vec)


def _block_diag(w):
    per = BD // LRU_BW
    w4 = w.reshape(NBD, per, LRU_BW, LRU_BW)
    out = jnp.zeros((NBD, per, LRU_BW, per, LRU_BW), w.dtype)
    for b in range(per):
        out = out.at[:, b, :, b, :].set(w4[:, b])
    return out.reshape(NBD, BD, BD)


def _block_diag_extract(wbd):
    per = BD // LRU_BW
    w5 = wbd.reshape(NBD, per, LRU_BW, per, LRU_BW)
    return jnp.stack([w5[:, b, :, b, :] for b in range(per)], axis=1).reshape(LRU_BLOCKS, LRU_BW, LRU_BW)


_ANY = pl.BlockSpec(memory_space=pl.ANY)


def _place():
    x, y, c = lax.axis_index("x"), lax.axis_index("y"), lax.axis_index("c")
    chips = [(1 - x, y), (x, 1 - y), (1 - x, 1 - y)]
    return x, y, c, chips


def _allgather_shards(shards):
    n = len(shards)

    def body(*refs):
        ins, outs = refs[:n], refs[n:2 * n]
        send_sems, recv_sems = refs[2 * n:]
        x, y, c, chips = _place()
        me = 2 * x + y
        sibling = (x, y, 1 - c)

        def remote(p, k, src, dst, to):
            return pltpu.make_async_remote_copy(src_ref=src, dst_ref=dst, send_sem=send_sems.at[p, k],
                                                recv_sem=recv_sems.at[p, k], device_id=to, device_id_type=MESH)

        sent = []
        for p in range(n):
            for k, chip in enumerate(chips):
                cp = remote(p, k, ins[p].at[c], outs[p].at[me, c], (chip[0], chip[1], c))
                cp.start()
                sent.append(cp)
        for p in range(n):
            for k, chip in enumerate(chips):
                half = outs[p].at[2 * chip[0] + chip[1], c]
                remote(p, k, half, half, sibling).wait_recv()
                fwd = remote(p, 3 + k, half, half, sibling)
                fwd.start()
                sent.append(fwd)
        for p in range(n):
            for k, chip in enumerate(chips):
                half = outs[p].at[2 * chip[0] + chip[1], 1 - c]
                remote(p, 3 + k, half, half, sibling).wait_recv()
        for cp in sent:
            cp.wait_send()

    gathered = pl.pallas_call(
        body, name="allgather_weights",
        in_specs=[_ANY] * n, out_specs=[_ANY] * n,
        out_shape=[jax.ShapeDtypeStruct((NCHIP,) + s.shape, s.dtype) for s in shards],
        scratch_shapes=[pltpu.SemaphoreType.DMA((n, 6)), pltpu.SemaphoreType.DMA((n, 6))],
    )(*shards)
    me = 2 * lax.axis_index("x") + lax.axis_index("y")
    return [lax.dynamic_update_index_in_dim(g, s, me, 0) for g, s in zip(gathered, shards)]


_LATE =["w_branch_a", "w_branch_b", "w_out", "w_up", "w_down"]
_COLUMN_CUT = ("w_in", "w_up")
N_PEERS = 7


def _shard_major(name, g):
    s = _columns_to_shards(g) if name in _COLUMN_CUT else g.reshape(NCHIP, g.shape[0] // NCHIP, g.shape[1])
    return s.reshape(NCHIP, 2, s.shape[1] // 2, s.shape[2])


class _Exchanges:
    def __init__(self, shards):
        self.shards = shards

    def weights_side(self):
        srcs = [self.shards[n] for n in _LATE]

        def copies(sin, sout, send, recv):
            x, y, c, chips = _place()
            return [pltpu.make_async_remote_copy(
                src_ref=sin[p], dst_ref=sout[p].at[2 * x + y], send_sem=send.at[3 * p + k], recv_sem=recv.at[3 * p + k],
                device_id=(chip[0], chip[1], c), device_id_type=MESH)
                for p in range(len(sin)) for k, chip in enumerate(chips)]

        return _Side(srcs, [jax.ShapeDtypeStruct((NCHIP,) + s.shape, s.dtype) for s in srcs], 3 * len(srcs), copies)

    def weights_landed(self, landed):
        me = 2 * lax.axis_index("x") + lax.axis_index("y")
        full = {n: lax.dynamic_update_index_in_dim(g, self.shards[n], me, 0) for n, g in zip(_LATE, landed)}
        return dict(wa=full["w_branch_a"].reshape(D, D), wb=full["w_branch_b"].reshape(D, D),
                    wout=full["w_out"].reshape(D, D), wup=_shards_to_columns(full["w_up"]),
                    wdown=full["w_down"].reshape(FF, D))

    def grads_side(self, grads):
        side_names = list(grads)
        srcs = [_shard_major(n, grads[n]) for n in side_names]

        def copies(sin, sout, send, recv):
            x, y, c, chips = _place()
            peers = [(x, y, 1 - c)] + [(cx, cy, c) for cx, cy in chips] + [(cx, cy, 1 - c) for cx, cy in chips]
            return [pltpu.make_async_remote_copy(
                src_ref=sin[p].at[2 * px + py, pc], dst_ref=sout[p].at[s], send_sem=send.at[N_PEERS * p + s],
                recv_sem=recv.at[N_PEERS * p + s], device_id=(px, py, pc), device_id_type=MESH)
                for p in range(len(sin)) for s, (px, py, pc) in enumerate(peers)]

        side = _Side(srcs, [jax.ShapeDtypeStruct((N_PEERS,) + s.shape[2:], s.dtype) for s in srcs],
                     N_PEERS * len(srcs), copies)
        side.names = side_names
        return side

    def grads_landed(self, side, landed):
        return {n: (own, got) for n, own, got in zip(side.names, side.srcs, landed)}


def _add8(g, got, me, c, name):
    _, _, half, cols = g.shape
    th = _row_tile(half, 2 * cols)

    def body(me_ref, c_ref, g_ref, r_ref, o_ref):
        acc = g_ref[...].astype(F32)
        for s in range(N_PEERS):
            acc = acc + r_ref[s].astype(F32)
        o_ref[...] = acc

    return pl.pallas_call(
        body, name=name,
        grid_spec=pltpu.PrefetchScalarGridSpec(
            num_scalar_prefetch=2, grid=(half // th,),
            in_specs=[pl.BlockSpec((None, None, th, cols), lambda i, me_ref, c_ref: (me_ref[0], c_ref[0], i, 0)),
                      pl.BlockSpec((N_PEERS, th, cols), lambda i, me_ref, c_ref: (0, i, 0))],
            out_specs=pl.BlockSpec((th, cols), lambda i, me_ref, c_ref: (i, 0))),
        out_shape=jax.ShapeDtypeStruct((half, cols), F32),
    )(me, c, g, got)


def _share_halves(halves):
    n = len(halves)

    def body(*refs):
        ins, outs = refs[:n], refs[n:2 * n]
        send_sems, recv_sems = refs[2 * n:]
        x, y, c, _ = _place()
        sibling = (x, y, 1 - c)
        copies = []
        for p in range(n):
            cp = pltpu.make_async_remote_copy(src_ref=ins[p], dst_ref=outs[p], send_sem=send_sems.at[p],
                                              recv_sem=recv_sems.at[p], device_id=sibling, device_id_type=MESH)
            cp.start()
            copies.append(cp)
        for cp in copies:
            cp.wait()

    return pl.pallas_call(
        body, name="reduce_share_halves",
        in_specs=[_ANY] * n, out_specs=[_ANY] * n,
        out_shape=[jax.ShapeDtypeStruct(h.shape, h.dtype) for h in halves],
        scratch_shapes=[pltpu.SemaphoreType.DMA((n,)), pltpu.SemaphoreType.DMA((n,))],
    )(*halves)


def _row_tile(half, cols):
    th = max(SLAB, min(half, (1 << 18) // cols // SLAB * SLAB))
    while half % th:
        th -= SLAB
    return th


N_DEV = 8
SMALL_ROWS = 208


def _allreduce_small(pack):
    def body(x_ref, out_ref, gbuf, send_sems, recv_sems, local_sem):
        x, y, c, chips = _place()
        me, sibling = (x, y, c), (x, y, 1 - c)

        def rows(px, py, pc):
            return gbuf.at[4 * px + 2 * py + pc]

        def copy(k, block, to, src=None):
            return pltpu.make_async_remote_copy(
                src_ref=rows(*block) if src is None else src, dst_ref=rows(*block),
                send_sem=send_sems.at[k], recv_sem=recv_sems.at[k], device_id=to, device_id_type=MESH)

        mine = pltpu.make_async_copy(x_ref, rows(*me), local_sem)
        mine.start()
        first = [copy(0, me, sibling, src=x_ref)]
        first += [copy(1 + j, me, (chip[0], chip[1], c), src=x_ref) for j, chip in enumerate(chips)]
        for cp in first:
            cp.start()
        passed = [copy(4 + j, (chip[0], chip[1], c), sibling) for j, chip in enumerate(chips)]
        for j, chip in enumerate(chips):
            copy(1 + j, (chip[0], chip[1], c), me).wait_recv()
            passed[j].start()
        copy(0, sibling, me).wait_recv()
        for j, chip in enumerate(chips):
            copy(4 + j, (chip[0], chip[1], 1 - c), me).wait_recv()
        for cp in first + passed:
            cp.wait_send()
        mine.wait()
        acc = gbuf[0]
        for d in range(1, N_DEV):
            acc = acc + gbuf[d]
        out_ref[...] = acc

    return pl.pallas_call(
        body, name="allreduce_small",
        in_specs=[pl.BlockSpec(memory_space=pltpu.VMEM)],
        out_specs=pl.BlockSpec(memory_space=pltpu.VMEM),
        out_shape=jax.ShapeDtypeStruct((SMALL_ROWS, D), F32),
        scratch_shapes=[pltpu.VMEM((N_DEV, SMALL_ROWS, D), F32), pltpu.SemaphoreType.DMA((7,)),
                        pltpu.SemaphoreType.DMA((7,)), pltpu.SemaphoreType.DMA],
    )(pack)


def _adamw(w, g, m, v, name):
    rows, cols = w.shape

    def body(w_ref, g_ref, m_ref, v_ref, d_ref, mo_ref, vo_ref):
        gv = g_ref[...]
        mn = ADAM_B1 * m_ref[...] + (1.0 - ADAM_B1) * gv
        vn = ADAM_B2 * v_ref[...] + (1.0 - ADAM_B2) * (gv * gv)
        m_hat = mn / (1.0 - ADAM_B1 ** ADAM_STEP)
        v_hat = vn / (1.0 - ADAM_B2 ** ADAM_STEP)
        d_ref[...] = -ADAM_LR * (m_hat / (jnp.sqrt(v_hat) + ADAM_EPS) + ADAM_WD * w_ref[...])
        mo_ref[...] = mn
        vo_ref[...] = vn

    if rows % SLAB:
        spec, steps = pl.BlockSpec((rows, DH), lambda i: (0, i)), cols // DH
    else:
        th = _row_tile(rows, cols)
        spec, steps = pl.BlockSpec((th, cols), lambda i: (i, 0)), rows // th
    return pl.pallas_call(
        body, name=name, grid=(steps,),
        in_specs=[spec] * 4, out_specs=[spec] * 3,
        out_shape=[jax.ShapeDtypeStruct((rows, cols), F32)] * 3,
        compiler_params=pltpu.CompilerParams(dimension_semantics=("parallel",)),
    )(w, g, m, v)


_SMALL = ["norm_mix_g", "norm_mlp_g", "norm_final_g", "conv_b", "lru_ba", "lru_bx", "lru_lambda"]
_ROW_FB, _ROW_CW, _ROW_WA, _ROW_WX, _ROW_LOSS = 56, 64, 72, 136, 200


def _pack_small(vals, col0):
    def slab(a):
        return jnp.pad(a, ((0, -a.shape[0] % SLAB), (0, D - a.shape[1])))

    rows = [slab(vals[n].reshape(1, D)) for n in _SMALL]
    rows.append(slab(vals["forget_b"].reshape(1, NH)))
    if vals["conv_w"].shape[1] == D:
        rows.append(slab(vals["conv_w"]))
    else:
        rows.append(slab(lax.dynamic_update_slice(jnp.zeros((CONV, D), F32), vals["conv_w"], (0, col0))))
    rows.append(vals["lru_wa"].reshape(LRU_BLOCKS * LRU_BW * LRU_BW // D, D))
    rows.append(vals["lru_wx"].reshape(LRU_BLOCKS * LRU_BW * LRU_BW // D, D))
    rows.append(slab(vals["loss"]) if "loss" in vals else jnp.zeros((SLAB, D), F32))
    return jnp.concatenate(rows, axis=0)


def _unpack_small(pack, col0):
    out = {n: pack[SLAB * i] for i, n in enumerate(_SMALL)}
    out["forget_b"] = pack[_ROW_FB, :NH]
    out["conv_w"] = lax.dynamic_slice(pack[_ROW_CW:_ROW_CW + CONV], (0, col0), (CONV, D // NCHIP))
    out["lru_wa"] = pack[_ROW_WA:_ROW_WX].reshape(LRU_BLOCKS, LRU_BW, LRU_BW)
    out["lru_wx"] = pack[_ROW_WX:_ROW_LOSS].reshape(LRU_BLOCKS, LRU_BW, LRU_BW)
    return out


_WEIGHTS = ["norm_mix_g", "w_in", "conv_w", "conv_b", "lru_wa", "lru_ba", "lru_wx", "lru_bx", "lru_lambda",
            "forget_b", "w_branch_a", "w_branch_b", "w_out", "norm_mlp_g", "w_up", "w_down", "norm_final_g"]
_BIG = ["w_in", "w_branch_a", "w_branch_b", "w_out", "w_up", "w_down"]


def _halves(a):
    return a.reshape(2, a.shape[0] // 2, a.shape[1])


def _columns_to_shards(a):
    rows, cols = a.shape[0], a.shape[1] // NCHIP
    return jnp.transpose(a.reshape(rows, NCHIP, cols), (1, 0, 2))


def _shards_to_columns(a):
    n, rows, cols = a.shape
    return jnp.transpose(a, (1, 0, 2)).reshape(rows, n * cols)


def kernel(x, norm_mix_g, w_in, conv_w, conv_b, lru_wa, lru_ba, lru_wx, lru_bx, lru_lambda, forget_b, w_branch_a, w_branch_b, w_out, norm_mlp_g, w_up, w_down, norm_final_g, loss_target, m_norm_mix_g, m_w_in, m_conv_w, m_conv_b, m_lru_wa, m_lru_ba, m_lru_wx, m_lru_bx, m_lru_lambda, m_forget_b, m_w_branch_a, m_w_branch_b, m_w_out, m_norm_mlp_g, m_w_up, m_w_down, m_norm_final_g, v_norm_mix_g, v_w_in, v_conv_w, v_conv_b, v_lru_wa, v_lru_ba, v_lru_wx, v_lru_bx, v_lru_lambda, v_forget_b, v_w_branch_a, v_w_branch_b, v_w_out, v_norm_mlp_g, v_w_up, v_w_down, v_norm_final_g):
    args = dict(locals())
    wts = {n: args[n] for n in _WEIGHTS}
    mom = {n: args["m_" + n] for n in _WEIGHTS}
    var = {n: args["v_" + n] for n in _WEIGHTS}
    T = x.shape[1]
    xi, yi, ci = lax.axis_index("x"), lax.axis_index("y"), lax.axis_index("c")
    me = 2 * xi + yi
    c1 = jnp.reshape(ci, (1,)).astype(jnp.int32)
    me1 = jnp.reshape(me, (1,)).astype(jnp.int32)
    col0 = me * (D // NCHIP)

    cw_pad = jnp.pad(conv_w, ((0, 4 * SLAB - CONV), (0, 0)))
    g_in, g_cw = _allgather_shards([_halves(w_in.astype(BF)), _halves(cw_pad)])
    cin = DIN // NCHIP
    win = _shards_to_columns(g_in.reshape(NCHIP, D, cin))
    w = dict(
        win=jnp.pad(win, ((0, 0), (0, DINP - DIN))),
        cw=_shards_to_columns(g_cw.reshape(NCHIP, 4 * SLAB, D // NCHIP)[:, :CONV]),
        vec=jnp.concatenate([conv_b[None], lru_ba[None], lru_bx[None], lru_lambda[None],
                             jnp.zeros((SLAB - 4, D), F32)], axis=0),
        fb=jnp.pad(forget_b[None], ((0, 0), (0, DH - NH))),
        wabd=_block_diag(lru_wa).astype(BF), wxbd=_block_diag(lru_wx).astype(BF),
        g_mix=norm_mix_g[None], g_mlp=norm_mlp_g[None], g_fin=norm_final_g[None])

    r = _local_step(x[0], loss_target[0], w, T, dist=_Exchanges({n: wts[n].astype(BF) for n in _LATE}))

    halves = [_add8(*r["big"][n], me1, c1, "add8_" + n) for n in _BIG]
    theirs = _share_halves(halves)
    low = ci == 0
    gsum = {n: jnp.concatenate([jnp.where(low, h, t), jnp.where(low, t, h)], axis=0)
            for n, h, t in zip(_BIG, halves, theirs)}
    lacc = r["lacc"]
    small = dict(norm_mix_g=r["dg_mix"], norm_mlp_g=r["dg_mlp"], norm_final_g=r["dg_fin"], conv_b=lacc[3],
                 lru_ba=lacc[0], lru_bx=lacc[1], lru_lambda=lacc[2], forget_b=r["dfb"][0, :NH],
                 conv_w=lacc[4:4 + CONV], lru_wa=_block_diag_extract(r["dwabd"]),
                 lru_wx=_block_diag_extract(r["dwxbd"]), loss=r["loss_vec"])
    gpack = _allreduce_small(_pack_small(small, col0))
    loss = jnp.sum(gpack[_ROW_LOSS])

    grads, delta, new_m, new_v = {}, {}, {}, {}
    for n in _BIG:
        if n == "w_in":
            gt = gsum[n].T
            grads[n] = gt.T
            delta[n], new_m[n], new_v[n] = [a.T for a in _adamw(wts[n].T, gt, mom[n].T, var[n].T, "adamw_" + n)]
        else:
            grads[n] = gsum[n]
            delta[n], new_m[n], new_v[n] = _adamw(wts[n], gsum[n], mom[n], var[n], "adamw_" + n)
    dp, mp, vp = _adamw(_pack_small(wts, col0), gpack, _pack_small(mom, col0), _pack_small(var, col0), "adamw_small")
    for dst, pack in ((grads, gpack), (delta, dp), (new_m, mp), (new_v, vp)):
        dst.update(_unpack_small(pack, col0))
    return (loss, r["dx"][None], *[grads[n] for n in _WEIGHTS], *[delta[n] for n in _WEIGHTS],
            *[new_m[n] for n in _WEIGHTS], *[new_v[n] for n in _WEIGHTS])
```

```python
import functools
import math

import jax
import jax.numpy as jnp
import numpy as np
from jax import lax
from jax.experimental import pallas as pl
from jax.experimental.pallas import tpu as pltpu

F32 = jnp.float32
BF = jnp.bfloat16

D = 1024
NH = 8
DH = 128
FF = 4096
CONV = 4
LRU_BLOCKS = 16
LRU_BW = 64
BD = 256
NBD = D // BD
LRU_C = 8.0
EPS = 1e-6
DIN = 7176
DINP = 7296
NCHIP = 4
SLAB = 8
VMEM_CAP = 60 * 1024 * 1024

ADAM_LR = 0.001
ADAM_B1 = 0.9
ADAM_B2 = 0.999
ADAM_EPS = 1e-08
ADAM_WD = 0.01
ADAM_STEP = 10

MESH = pl.DeviceIdType.MESH


def _vmem_limit(nbytes):
    return int(min(VMEM_CAP, max(32 * 1024 * 1024, 3 * nbytes)))


def _nbytes(shape, dtype):
    return int(np.prod(shape)) * jnp.dtype(dtype).itemsize


def _sig(x):
    return 0.5 * jnp.tanh(0.5 * x) + 0.5


def _log1p(u):
    w = 1.0 + u
    return jnp.where(w == 1.0, u, jnp.log(w) * (u / (w - 1.0)))


def _one_minus_sq(a, la):
    return jnp.tanh(-la) * (1.0 + a * a)


def _softplus(z):
    return jnp.maximum(z, 0.0) + _log1p(jnp.exp(-jnp.abs(z)))


_GELU_C = math.sqrt(2.0 / math.pi)


def _gelu(x):
    return 0.5 * x * (1.0 + jnp.tanh(_GELU_C * (x + 0.044715 * x * x * x)))


def _gelu_grad(x):
    t = jnp.tanh(_GELU_C * (x + 0.044715 * x * x * x))
    return 0.5 * (1.0 + t) + 0.5 * x * (1.0 - t * t) * _GELU_C * (1.0 + 3.0 * 0.044715 * x * x)


def _slab_scan_fwd(a, b):
    row = lax.broadcasted_iota(jnp.int32, a.shape, 0)
    for k in (1, 2, 4):
        a_s = pltpu.roll(a, k, 0)
        b_s = pltpu.roll(b, k, 0)
        m = row >= k
        b = jnp.where(m, a * b_s + b, b)
        a = jnp.where(m, a * a_s, a)
    return a, b


def _slab_scan_bwd(a, b):
    row = lax.broadcasted_iota(jnp.int32, a.shape, 0)
    for k in (1, 2, 4):
        a_s = pltpu.roll(a, SLAB - k, 0)
        b_s = pltpu.roll(b, SLAB - k, 0)
        m = row < SLAB - k
        b = jnp.where(m, a * b_s + b, b)
        a = jnp.where(m, a * a_s, a)
    return a, b


_DIMS = {"nn": (((1,), (0,)), ((), ())), "nt": (((1,), (1,)), ((), ())), "tn": (((0,), (0,)), ((), ()))}


def _dot(a, b, mode="nn"):
    return lax.dot_general(a, b, _DIMS[mode], preferred_element_type=F32)


def _mm(a, b, mode, M, N, K, *, name, out_dtype=F32, tm=None, tn=1024, tk=1024,
        a_off=(0, 0), b_off=(0, 0), add=None, epi=None, epi_ins=()):
    if tm is None:
        tm = 2048 if (K <= tk and out_dtype == BF and add is None) else 1024
    tm, tn, tk = min(tm, M), min(tn, N), min(tk, K)
    nk = K // tk
    grid = (M // tm, N // tn, nk)
    if mode == "nn":
        a_spec = pl.BlockSpec((tm, tk), lambda i, j, k: (i + a_off[0], k + a_off[1]))
        b_spec = pl.BlockSpec((tk, tn), lambda i, j, k: (k + b_off[0], j + b_off[1]))
    elif mode == "nt":
        a_spec = pl.BlockSpec((tm, tk), lambda i, j, k: (i + a_off[0], k + a_off[1]))
        b_spec = pl.BlockSpec((tn, tk), lambda i, j, k: (j + b_off[0], k + b_off[1]))
    else:
        a_spec = pl.BlockSpec((tk, tm), lambda i, j, k: (k + a_off[0], i + a_off[1]))
        b_spec = pl.BlockSpec((tk, tn), lambda i, j, k: (k + b_off[0], j + b_off[1]))
    o_spec = pl.BlockSpec((tm, tn), lambda i, j, k: (i, j))
    extra = ([add] if add is not None else []) + list(epi_ins)
    n_extra = len(extra)
    has_add = add is not None

    def body(*refs):
        a_ref, b_ref = refs[0], refs[1]
        ex = refs[2:2 + n_extra]
        o_ref = refs[2 + n_extra]

        def finish(acc):
            if has_add:
                acc = acc + ex[0][...].astype(F32)
            if epi is not None:
                acc = epi(acc, *[e[...] for e in ex[(1 if has_add else 0):]])
            o_ref[...] = acc.astype(o_ref.dtype)

        p = _dot(a_ref[...].astype(BF), b_ref[...].astype(BF), mode)
        if nk == 1:
            finish(p)
        else:
            acc_ref = refs[3 + n_extra]
            k = pl.program_id(2)

            @pl.when(k == 0)
            def _():
                acc_ref[...] = p

            @pl.when(k > 0)
            def _():
                acc_ref[...] += p

            @pl.when(k == nk - 1)
            def _():
                finish(acc_ref[...])

    blk = (_nbytes((tm, tk), a.dtype) + _nbytes((tk, tn), b.dtype) + _nbytes((tm, tn), out_dtype)
           + sum(_nbytes((tm, tn), e.dtype) for e in extra) + 2 * _nbytes((tm, tn), F32))
    return pl.pallas_call(
        body, name=name, grid=grid,
        in_specs=[a_spec, b_spec] + [o_spec] * n_extra,
        out_specs=o_spec,
        out_shape=jax.ShapeDtypeStruct((M, N), out_dtype),
        scratch_shapes=[pltpu.VMEM((tm, tn), F32)] if nk > 1 else [],
        compiler_params=pltpu.CompilerParams(
            dimension_semantics=("parallel", "parallel", "arbitrary"), vmem_limit_bytes=_vmem_limit(blk)),
    )(a, b, *extra)


def _ew(fn, T, tm, ins, consts, outs, accs, *, name, reverse=False):
    tm = min(tm, T)
    nt = T // tm
    n_in, n_c, n_o, n_a = len(ins), len(consts), len(outs), len(accs)

    def row(i):
        return nt - 1 - i if reverse else i

    in_specs = [pl.BlockSpec((tm, w), functools.partial(lambda i, cb: (row(i), cb), cb=cb)) for (_, w, cb) in ins]
    in_specs += [pl.BlockSpec(c.shape, functools.partial(lambda i, nd: (0,) * nd, nd=c.ndim)) for c in consts]
    out_specs = [pl.BlockSpec((tm, w), lambda i: (row(i), 0)) for (w, _) in outs]
    out_specs += [pl.BlockSpec((r, w), lambda i: (0, 0)) for (r, w) in accs]
    out_shape = [jax.ShapeDtypeStruct((T, w), dt) for (w, dt) in outs]
    out_shape += [jax.ShapeDtypeStruct((r, w), F32) for (r, w) in accs]

    def body(*refs):
        in_refs = refs[:n_in]
        c_refs = refs[n_in:n_in + n_c]
        o_refs = refs[n_in + n_c:n_in + n_c + n_o]
        a_refs = refs[n_in + n_c + n_o:]
        ov, av = fn([r[...] for r in in_refs], [r[...] for r in c_refs])
        for r, v in zip(o_refs, ov):
            r[...] = v.astype(r.dtype)
        if n_a:
            i = pl.program_id(0)

            @pl.when(i == 0)
            def _():
                for r, v in zip(a_refs, av):
                    r[...] = v

            @pl.when(i > 0)
            def _():
                for r, v in zip(a_refs, av):
                    r[...] += v

    blk = (sum(_nbytes((tm, w), a.dtype) for (a, w, _) in ins) + sum(_nbytes(c.shape, c.dtype) for c in consts)
           + sum(_nbytes((tm, w), dt) for (w, dt) in outs) + sum(_nbytes(s, F32) for s in accs))
    res = pl.pallas_call(
        body, name=name, grid=(nt,), in_specs=in_specs, out_specs=out_specs, out_shape=out_shape,
        compiler_params=pltpu.CompilerParams(
            dimension_semantics=("arbitrary",), vmem_limit_bytes=_vmem_limit(blk)),
    )(*[a for (a, _, _) in ins], *consts)
    return res


def _colsum(v):
    return jnp.sum(v, axis=0, keepdims=True)


FGATE_GROUP = 4


def _fgate_fwd(fl, fb, T, tm=512):
    tm = min(tm, T)

    def body(fl_ref, fb_ref, f_ref, carry_ref):
        i = pl.program_id(0)

        @pl.when(i == 0)
        def _():
            carry_ref[...] = jnp.zeros_like(carry_ref)

        rows = FGATE_GROUP * SLAB
        sub = lax.broadcasted_iota(jnp.int32, (rows, DH), 0) % SLAB

        def group(s, carry):
            r0 = pl.multiple_of(s * rows, rows)
            z = fl_ref[pl.ds(r0, rows), :] + fb_ref[...]
            c = jnp.minimum(z, 0.0) - _log1p(jnp.exp(-jnp.abs(z)))
            for k in (1, 2, 4):
                c = c + jnp.where(sub >= k, pltpu.roll(c, k, 0), 0.0)
            for u in range(FGATE_GROUP):
                cu = c[u * SLAB:(u + 1) * SLAB] + carry
                f_ref[pl.ds(r0 + u * SLAB, SLAB), :] = cu
                carry = cu[SLAB - 1:SLAB, :]
            return carry

        carry_ref[0:1, :] = lax.fori_loop(0, tm // rows, group, carry_ref[0:1, :])

    return pl.pallas_call(
        body, name="fgate_fwd", grid=(T // tm,),
        in_specs=[pl.BlockSpec((tm, DH), lambda i: (i, 0)), pl.BlockSpec((1, DH), lambda i: (0, 0))],
        out_specs=pl.BlockSpec((tm, DH), lambda i: (i, 0)),
        out_shape=jax.ShapeDtypeStruct((T, DH), F32),
        scratch_shapes=[pltpu.VMEM((SLAB, DH), F32)],
        compiler_params=pltpu.CompilerParams(dimension_semantics=("arbitrary",)),
    )(fl, fb)


def _fgate_bwd(dF, fl, fb, T, tm=512):
    tm = min(tm, T)
    nt = T // tm

    def body(df_ref, fl_ref, fb_ref, o_ref, acc_ref, carry_ref):
        i = pl.program_id(0)

        @pl.when(i == 0)
        def _():
            carry_ref[...] = jnp.zeros_like(carry_ref)
            acc_ref[...] = jnp.zeros_like(acc_ref)

        rows = FGATE_GROUP * SLAB
        sub = lax.broadcasted_iota(jnp.int32, (rows, DH), 0) % SLAB

        def group(n, carry):
            g_next, acc = carry
            r0 = pl.multiple_of((tm // rows - 1 - n) * rows, rows)
            c = df_ref[pl.ds(r0, rows), :]
            for k in (1, 2, 4):
                c = c + jnp.where(sub < SLAB - k, pltpu.roll(c, rows - k, 0), 0.0)
            sg = _sig(-(fl_ref[pl.ds(r0, rows), :] + fb_ref[...]))
            for u in reversed(range(FGATE_GROUP)):
                cu = c[u * SLAB:(u + 1) * SLAB] + g_next
                dfl = cu * sg[u * SLAB:(u + 1) * SLAB]
                o_ref[pl.ds(r0 + u * SLAB, SLAB), :] = dfl.astype(o_ref.dtype)
                acc = acc + _colsum(dfl)
                g_next = cu[0:1, :]
            return g_next, acc

        g, acc = lax.fori_loop(0, tm // rows, group, (carry_ref[0:1, :], jnp.zeros((1, DH), F32)))
        carry_ref[0:1, :] = g
        acc_ref[...] += acc

    return pl.pallas_call(
        body, name="fgate_bwd", grid=(nt,),
        in_specs=[pl.BlockSpec((tm, DH), lambda i: (nt - 1 - i, 0)), pl.BlockSpec((tm, DH), lambda i: (nt - 1 - i, 0)),
                  pl.BlockSpec((1, DH), lambda i: (0, 0))],
        out_specs=[pl.BlockSpec((tm, DH), lambda i: (nt - 1 - i, 0)), pl.BlockSpec((1, DH), lambda i: (0, 0))],
        out_shape=[jax.ShapeDtypeStruct((T, DH), BF), jax.ShapeDtypeStruct((1, DH), F32)],
        scratch_shapes=[pltpu.VMEM((SLAB, DH), F32)],
        compiler_params=pltpu.CompilerParams(dimension_semantics=("arbitrary",)),
    )(dF, fl, fb)


def _shifts_down(stage, x, prev8, shifts):
    n = x.shape[0]
    stage[0:SLAB, :] = prev8
    stage[SLAB:SLAB + n, :] = x
    return [stage[SLAB - d:SLAB - d + n, :] for d in shifts]


def _shifts_up(stage, x, next8, shifts):
    n = x.shape[0]
    stage[0:n, :] = x
    stage[n:n + SLAB, :] = next8
    return [stage[d:d + n, :] for d in shifts]


def _conv(x, prev8, cw, cb, stage):
    xs = [x] + _shifts_down(stage, x, prev8, (1, 2, 3))
    xa = cb + cw[3:4, :] * xs[0] + cw[2:3, :] * xs[1] + cw[1:2, :] * xs[2] + cw[0:1, :] * xs[3]
    return xa, xs


def _lru_gates(xa_g, wa_g, wx_g, ba_g, bx_g, sp_g):
    xb = xa_g.astype(BF)
    r = _sig(_dot(xb, wa_g) + ba_g)
    ig = _sig(_dot(xb, wx_g) + bx_g)
    la = -LRU_C * r * sp_g
    a = jnp.exp(la)
    mult = jnp.sqrt(_one_minus_sq(a, la))
    return r, ig, a, mult


def _lru_fwd(xg, cw, vec, wabd, wxbd, T, tm=256):
    tm = min(tm, T)
    nsl = tm // SLAB

    def body(x_ref, xp_ref, cw_ref, vec_ref, wa_ref, wx_ref, h_ref, a_s, b_s, carry_ref, stage):
        i = pl.program_id(0)

        @pl.when(i == 0)
        def _():
            carry_ref[...] = jnp.zeros_like(carry_ref)

        x = x_ref[...].astype(F32)
        prev8 = jnp.where(i > 0, xp_ref[SLAB:, :].astype(F32), 0.0)
        vec_v = vec_ref[...]
        xa, _ = _conv(x, prev8, cw_ref[...], vec_v[0:1, :], stage)
        sp = _softplus(-vec_v[3:4, :])
        for g in range(NBD):
            sl = slice(g * BD, (g + 1) * BD)
            _, ig, a, mult = _lru_gates(xa[:, sl], wa_ref[g], wx_ref[g], vec_v[1:2, sl], vec_v[2:3, sl], sp[:, sl])
            a_s[:, sl] = a
            b_s[:, sl] = mult * ig * xa[:, sl]

        def slab(s, carry):
            r0 = pl.multiple_of(s * SLAB, SLAB)
            A, B = _slab_scan_fwd(a_s[pl.ds(r0, SLAB), :], b_s[pl.ds(r0, SLAB), :])
            h = A * carry + B
            h_ref[pl.ds(r0, SLAB), :] = h
            return h[SLAB - 1:SLAB, :]

        carry_ref[0:1, :] = lax.fori_loop(0, nsl, slab, carry_ref[0:1, :])

    blk = 5 * _nbytes((tm, D), F32) + 2 * _nbytes((NBD, BD, BD), BF)
    return pl.pallas_call(
        body, name="lru_fwd", grid=(T // tm,),
        in_specs=[pl.BlockSpec((tm, D), lambda i: (i, 0)),
                  pl.BlockSpec((2 * SLAB, D), lambda i: (jnp.maximum(i * (nsl // 2) - 1, 0), 0)),
                  pl.BlockSpec((CONV, D), lambda i: (0, 0)),
                  pl.BlockSpec((SLAB, D), lambda i: (0, 0)),
                  pl.BlockSpec((NBD, BD, BD), lambda i: (0, 0, 0)),
                  pl.BlockSpec((NBD, BD, BD), lambda i: (0, 0, 0))],
        out_specs=pl.BlockSpec((tm, D), lambda i: (i, 0)),
        out_shape=jax.ShapeDtypeStruct((T, D), F32),
        scratch_shapes=[pltpu.VMEM((tm, D), F32), pltpu.VMEM((tm, D), F32), pltpu.VMEM((SLAB, D), F32),
                        pltpu.VMEM((tm + SLAB, D), F32)],
        compiler_params=pltpu.CompilerParams(dimension_semantics=("arbitrary",), vmem_limit_bytes=_vmem_limit(blk)),
    )(xg, xg, cw, vec, wabd, wxbd)


def _lru_bwd(xg, h, dha, cw, vec, wabd, wxbd, T, tm=256):
    tm = min(tm, T)
    nsl = tm // SLAB
    nt = T // tm

    def body(x_ref, xp_ref, h_ref, hp_ref, dh_ref, cw_ref, vec_ref, wa_ref, wx_ref,
             dx_ref, dwa_ref, dwx_ref, acc_ref, a_s, b_s, g_s, dxa_s, carry_ref, dxan_ref, stage):
        n = pl.program_id(0)
        it = nt - 1 - n

        @pl.when(n == 0)
        def _():
            carry_ref[...] = jnp.zeros_like(carry_ref)
            dxan_ref[...] = jnp.zeros_like(dxan_ref)
            dwa_ref[...] = jnp.zeros_like(dwa_ref)
            dwx_ref[...] = jnp.zeros_like(dwx_ref)
            acc_ref[...] = jnp.zeros_like(acc_ref)

        x = x_ref[...].astype(F32)
        prev8 = jnp.where(it > 0, xp_ref[SLAB:, :].astype(F32), 0.0)
        hprev8 = jnp.where(it > 0, hp_ref[...], 0.0)
        vec_v = vec_ref[...]
        cw_v = cw_ref[...]
        xa, xs = _conv(x, prev8, cw_v, vec_v[0:1, :], stage)
        sp = _softplus(-vec_v[3:4, :])
        gates = []
        for g in range(NBD):
            sl = slice(g * BD, (g + 1) * BD)
            r, ig, a, mult = _lru_gates(xa[:, sl], wa_ref[g], wx_ref[g], vec_v[1:2, sl], vec_v[2:3, sl], sp[:, sl])
            gates.append((r, ig, a, mult))
            a_s[:, sl] = a
        a_s[...] = _shifts_up(stage, a_s[...], carry_ref[...], (1,))[0]
        b_s[...] = dh_ref[...]

        def slab(m, carry):
            r0 = pl.multiple_of((nsl - 1 - m) * SLAB, SLAB)
            A, B = _slab_scan_bwd(a_s[pl.ds(r0, SLAB), :], b_s[pl.ds(r0, SLAB), :])
            gg = A * carry + B
            g_s[pl.ds(r0, SLAB), :] = gg
            return gg[0:1, :]

        g_first = lax.fori_loop(0, nsl, slab, carry_ref[1:2, :])
        gt = g_s[...]
        h_prev = _shifts_down(stage, h_ref[...], hprev8, (1,))[0]
        dba = []
        dbx = []
        dsp = []
        for g in range(NBD):
            sl = slice(g * BD, (g + 1) * BD)
            r, ig, a, mult = gates[g]
            xa_g = xa[:, sl]
            g_g = gt[:, sl]
            da = g_g * h_prev[:, sl]
            dmult = g_g * ig * xa_g
            di = g_g * mult * xa_g
            dxa_g = g_g * mult * ig
            dla = da * a - dmult * (a * a / mult)
            dr = dla * (-LRU_C) * sp[:, sl]
            dsp.append(_colsum(dla * (-LRU_C) * r))
            dra = (dr * r * (1.0 - r))
            dix = (di * ig * (1.0 - ig))
            dba.append(_colsum(dra))
            dbx.append(_colsum(dix))
            dra_b = dra.astype(BF)
            dix_b = dix.astype(BF)
            xb = xa_g.astype(BF)
            dxa_g = dxa_g + _dot(dra_b, wa_ref[g], "nt") + _dot(dix_b, wx_ref[g], "nt")
            dwa_ref[g] += _dot(xb, dra_b, "tn")
            dwx_ref[g] += _dot(xb, dix_b, "tn")
            dxa_s[:, sl] = dxa_g
        dxa = dxa_s[...]
        nxt = dxan_ref[...]
        up = _shifts_up(stage, dxa, nxt, (1, 2, 3))
        dx = cw_v[3:4, :] * dxa + cw_v[2:3, :] * up[0] + cw_v[1:2, :] * up[1] + cw_v[0:1, :] * up[2]
        dx_ref[...] = dx.astype(dx_ref.dtype)
        acc_ref[0:1, :] += jnp.concatenate(dba, axis=1)
        acc_ref[1:2, :] += jnp.concatenate(dbx, axis=1)
        acc_ref[2:3, :] += jnp.concatenate(dsp, axis=1)
        acc_ref[3:4, :] += _colsum(dxa)
        for k in range(CONV):
            acc_ref[4 + k:5 + k, :] += _colsum(dxa * xs[CONV - 1 - k])
        dxan_ref[...] = dxa[0:SLAB, :]
        a_first = jnp.concatenate([gates[g][2][0:1, :] for g in range(NBD)], axis=1)
        carry_ref[0:1, :] = a_first
        carry_ref[1:2, :] = g_first

        @pl.when(n == nt - 1)
        def _():
            acc_ref[2:3, :] = acc_ref[2:3, :] * (-_sig(-vec_v[3:4, :]))

    rowblk = lambda i: (nt - 1 - i, 0)
    prevblk = lambda i: (jnp.maximum((nt - 1 - i) * nsl - 1, 0), 0)
    c2 = lambda i: (0, 0)
    c3 = lambda i: (0, 0, 0)
    blk = 12 * _nbytes((tm, D), F32) + 6 * _nbytes((NBD, BD, BD), F32)
    return pl.pallas_call(
        body, name="lru_bwd", grid=(nt,),
        in_specs=[pl.BlockSpec((tm, D), rowblk),
                  pl.BlockSpec((2 * SLAB, D), lambda i: (jnp.maximum((nt - 1 - i) * (nsl // 2) - 1, 0), 0)),
                  pl.BlockSpec((tm, D), rowblk), pl.BlockSpec((SLAB, D), prevblk),
                  pl.BlockSpec((tm, D), rowblk),
                  pl.BlockSpec((CONV, D), c2), pl.BlockSpec((SLAB, D), c2),
                  pl.BlockSpec((NBD, BD, BD), c3), pl.BlockSpec((NBD, BD, BD), c3)],
        out_specs=[pl.BlockSpec((tm, D), rowblk), pl.BlockSpec((NBD, BD, BD), c3), pl.BlockSpec((NBD, BD, BD), c3),
                   pl.BlockSpec((16, D), c2)],
        out_shape=[jax.ShapeDtypeStruct((T, D), BF), jax.ShapeDtypeStruct((NBD, BD, BD), F32),
                   jax.ShapeDtypeStruct((NBD, BD, BD), F32), jax.ShapeDtypeStruct((16, D), F32)],
        scratch_shapes=[pltpu.VMEM((tm, D), F32), pltpu.VMEM((tm, D), F32), pltpu.VMEM((tm, D), F32),
                        pltpu.VMEM((tm, D), F32), pltpu.VMEM((SLAB, D), F32), pltpu.VMEM((SLAB, D), F32),
                        pltpu.VMEM((tm + SLAB, D), F32)],
        compiler_params=pltpu.CompilerParams(dimension_semantics=("arbitrary",), vmem_limit_bytes=_vmem_limit(blk)),
    )(xg, xg, h, h, dha, cw, vec, wabd, wxbd)


_SCALE = 1.0 / math.sqrt(DH)


_ANY = pl.BlockSpec(memory_space=pl.ANY)


class _Side:
    def __init__(self, srcs, outs, nsem, copies):
        self.srcs, self.outs, self.nsem, self.copies = list(srcs), list(outs), nsem, copies


def _pallas(body, operands, *, name, grid, in_specs, out_specs, out_shape, scratch_shapes=(), semantics,
            vmem=None, side=None):
    if side is None:
        return pl.pallas_call(
            body, name=name, grid=grid, in_specs=in_specs, out_specs=out_specs, out_shape=out_shape,
            scratch_shapes=list(scratch_shapes),
            compiler_params=pltpu.CompilerParams(dimension_semantics=semantics, vmem_limit_bytes=vmem),
        )(*operands)
    n_in, n_out, n_scr = len(in_specs), len(out_specs), len(scratch_shapes)
    ns, no = len(side.srcs), len(side.outs)

    def hosted(*refs):
        ins, refs = refs[:n_in], refs[n_in:]
        sin, refs = refs[:ns], refs[ns:]
        outs, refs = refs[:n_out], refs[n_out:]
        sout, refs = refs[:no], refs[no:]
        scr, (send, recv) = refs[:n_scr], refs[n_scr:]
        ids = [pl.program_id(a) for a in range(len(grid))]
        first = functools.reduce(jnp.logical_and, [i == 0 for i in ids])
        last = functools.reduce(jnp.logical_and, [i == g - 1 for i, g in zip(ids, grid)])

        @pl.when(first)
        def _():
            for cp in side.copies(sin, sout, send, recv):
                cp.start()

        body(*ins, *outs, *scr)

        @pl.when(last)
        def _():
            for cp in side.copies(sin, sout, send, recv):
                cp.wait()

    return pl.pallas_call(
        hosted, name=name, grid=grid, in_specs=list(in_specs) + [_ANY] * ns, out_specs=list(out_specs) + [_ANY] * no,
        out_shape=list(out_shape) + side.outs,
        scratch_shapes=list(scratch_shapes) + [pltpu.SemaphoreType.DMA((side.nsem,)), pltpu.SemaphoreType.DMA((side.nsem,))],
        compiler_params=pltpu.CompilerParams(dimension_semantics=("arbitrary",) * len(grid), vmem_limit_bytes=vmem),
    )(*operands, *side.srcs)


DA = 2 * DH
_LOG2E = math.log2(math.e)
_C2 = _SCALE * _LOG2E


def _aug_fn(ins, cs):
    q, k, fcum = ins
    g_all = fcum * _LOG2E
    lane = lax.broadcasted_iota(jnp.int32, (q.shape[0], DH), 1)
    qa, ka = [], []
    for hd in range(NH):
        g = g_all[:, hd:hd + 1]
        hi = g.astype(BF).astype(F32)
        mid = (g - hi).astype(BF).astype(F32)
        lo = ((g - hi) - mid).astype(BF).astype(F32)
        qx = jnp.where(lane == 0, hi, jnp.where(lane == 1, mid, jnp.where(lane == 2, lo,
                                                                          jnp.where(lane < 6, 1.0, 0.0))))
        kx = jnp.where(lane < 3, 1.0, jnp.where(lane == 3, -hi, jnp.where(lane == 4, -mid,
                                                                          jnp.where(lane == 5, -lo, 0.0))))
        qa += [q[:, hd * DH:(hd + 1) * DH], qx.astype(BF)]
        ka += [k[:, hd * DH:(hd + 1) * DH], kx.astype(BF)]
    return [jnp.concatenate(qa, axis=1), jnp.concatenate(ka, axis=1)], []


_KA_ONES = DH + 3
KT_ONES = 16


def _attn_fwd(qa, ka, qkv, T, blk=512, side=None):
    blk = min(blk, T)
    nb = T // blk
    half = blk // 2

    def body(q_ref, k_ref, vn_ref, o_ref, lse_ref, v_ref):
        i = pl.program_id(1)

        @pl.when(i == 0)
        def _():
            for jj in range(nb):
                v_ref[jj] = vn_ref[jj * blk:(jj + 1) * blk, :].astype(F32).T.astype(BF)

        q = q_ref[...]

        def scores(j):
            r0 = pl.multiple_of(j * blk, blk)
            return _dot(k_ref[pl.ds(r0, blk), :], q, "nt")

        def update(s, vj, carry):
            m, l, acc = carry
            m_new = jnp.maximum(m, jnp.max(s, axis=0, keepdims=True))
            alpha = jnp.exp2(m - m_new)
            p = jnp.exp2(s - m_new)
            l = alpha * l + jnp.sum(p, axis=0, keepdims=True)
            acc = alpha * acc + _dot(vj, p.astype(BF))
            return m_new, l, acc

        def step(j, st):
            return update(scores(j), v_ref[j], st)

        init = (jnp.full((1, blk), -jnp.inf, F32), jnp.zeros((1, blk), F32), jnp.zeros((DH, blk), F32))
        carry = lax.fori_loop(0, i, step, init)
        last = scores(i)
        vi = v_ref[i]
        rk = lax.broadcasted_iota(jnp.int32, (half, blk), 0)
        cq = lax.broadcasted_iota(jnp.int32, (half, blk), 1)
        m, l, acc = update(jnp.where(cq >= rk, last[:half], -jnp.inf), vi[:, :half], carry)
        s2 = jnp.where(cq[:, :half] >= rk[:, :half], last[half:, half:], -jnp.inf)
        m2, l2, acc2 = update(s2, vi[:, half:], (m[:, half:], l[:, half:], acc[:, half:]))
        m = jnp.concatenate([m[:, :half], m2], axis=1)
        l = jnp.concatenate([l[:, :half], l2], axis=1)
        acc = jnp.concatenate([acc[:, :half], acc2], axis=1)
        o_ref[...] = (acc / l).T.astype(o_ref.dtype)
        lse_ref[...] = m + jnp.log(l) * _LOG2E

    vm = _nbytes((T, DA), BF) + 2 * _nbytes((T, DH), BF) + 6 * _nbytes((blk, blk), F32)
    return _pallas(
        body, (qa, ka, qkv), name="attn_fwd", grid=(NH, nb),
        in_specs=[pl.BlockSpec((blk, DA), lambda h, i: (i, h)),
                  pl.BlockSpec((T, DA), lambda h, i: (0, h)),
                  pl.BlockSpec((T, DH), lambda h, i: (0, 2 * NH + h))],
        out_specs=[pl.BlockSpec((blk, DH), lambda h, i: (i, h)),
                   pl.BlockSpec((None, None, 1, blk), lambda h, i: (h, i, 0, 0))],
        out_shape=[jax.ShapeDtypeStruct((T, D), BF), jax.ShapeDtypeStruct((NH, nb, 1, blk), F32)],
        scratch_shapes=[pltpu.VMEM((nb, DH, blk), BF)],
        semantics=("parallel", "arbitrary"), vmem=_vmem_limit(vm), side=side)


def _attn_bwd(qa, ka, qkv, do, lrow, drow, T, blk=512, side=None):
    blk = min(blk, T)
    nb = T // blk

    def body(ka_ref, v_ref, qa_ref, do_ref, l_ref, d_ref, dq_ref, dk_ref, dv_ref, dfs_ref, dft_ref, dq_s):
        j = pl.program_id(1)

        @pl.when(j == 0)
        def _():
            dq_s[...] = jnp.zeros_like(dq_s)

        row = lax.broadcasted_iota(jnp.int32, (DH + KT_ONES, blk), 0)
        dq_scale = jnp.where(row < DH, _SCALE, 1.0)

        kaj = ka_ref[...]
        ktj = jnp.concatenate([kaj[:, :DH].astype(F32).T.astype(BF), jnp.ones((KT_ONES, blk), BF)], axis=0)
        vj = v_ref[...]

        def step(i, carry):
            dka, dv = carry
            r0 = pl.multiple_of(i * blk, blk)
            qi = qa_ref[pl.ds(r0, blk), :]
            doi = do_ref[pl.ds(r0, blk), :]
            pt = jnp.exp2(_dot(kaj, qi, "nt") - l_ref[i])
            dv = dv + _dot(pt.astype(BF), doi)
            dpt = _dot(vj, doi, "nt")
            dst = pt * (dpt - d_ref[i])
            dsb = dst.astype(BF)
            dka = dka + _dot(dsb, qi)
            dq_s[i] += _dot(ktj, dsb) * dq_scale
            return dka, dv

        def diagonal():
            half = blk // 2
            r0 = pl.multiple_of(j * blk, blk)
            qi = qa_ref[pl.ds(r0, blk), :]
            doi = do_ref[pl.ds(r0, blk), :]
            lr, dr = l_ref[j], d_ref[j]
            rk = lax.broadcasted_iota(jnp.int32, (half, blk), 0)
            cq = lax.broadcasted_iota(jnp.int32, (half, blk), 1)

            def quarter(ka, v, q, do, l2, d2, keep):
                pt = jnp.exp2(jnp.where(keep, _dot(ka, q, "nt") - l2, -jnp.inf))
                dsb = (pt * (_dot(v, do, "nt") - d2)).astype(BF)
                return _dot(dsb, q), _dot(pt.astype(BF), do), dsb

            dka1, dv1, ds1 = quarter(kaj[:half], vj[:half], qi, doi, lr, dr, cq >= rk)
            dka2, dv2, ds2 = quarter(kaj[half:], vj[half:], qi[half:], doi[half:], lr[:, half:], dr[:, half:],
                                     cq[:, :half] >= rk[:, :half])
            dq2 = jnp.concatenate([jnp.zeros((DH + KT_ONES, half), F32), _dot(ktj[:, half:], ds2)], axis=1)
            dq_s[j] += (_dot(ktj[:, :half], ds1) + dq2) * dq_scale
            return jnp.concatenate([dka1, dka2], axis=0), jnp.concatenate([dv1, dv2], axis=0)

        dka, dv = lax.fori_loop(j + 1, nb, step, diagonal())
        dk_ref[...] = (dka[:, :DH] * (1.0 / _LOG2E)).astype(dk_ref.dtype)
        dv_ref[...] = dv.astype(dv_ref.dtype)
        dfs_ref[...] = dka[:, DH:].T[_KA_ONES - DH:_KA_ONES - DH + 1, :]

        @pl.when(j == nb - 1)
        def _():
            for ii in range(nb):
                t = dq_s[ii]
                dq_ref[ii * blk:(ii + 1) * blk, :] = t[:DH].T.astype(dq_ref.dtype)
                dft_ref[ii] = t[DH:DH + 1]

    rowv = pl.BlockSpec((None, nb, 1, blk), lambda h, j: (h, 0, 0, 0))
    vm = (_nbytes((T, DA), BF) + 2 * _nbytes((T, DH), BF) + _nbytes((T, DH + KT_ONES), F32)
          + 8 * _nbytes((blk, blk), F32))
    return _pallas(
        body, (ka, qkv, qa, do, lrow, drow), name="attn_bwd", grid=(NH, nb),
        in_specs=[pl.BlockSpec((blk, DA), lambda h, j: (j, h)),
                  pl.BlockSpec((blk, DH), lambda h, j: (j, 2 * NH + h)),
                  pl.BlockSpec((T, DA), lambda h, j: (0, h)),
                  pl.BlockSpec((T, DH), lambda h, j: (0, h)),
                  rowv, rowv],
        out_specs=[pl.BlockSpec((T, DH), lambda h, j: (0, h)),
                   pl.BlockSpec((blk, DH), lambda h, j: (j, h)),
                   pl.BlockSpec((blk, DH), lambda h, j: (j, h)),
                   pl.BlockSpec((None, None, 1, blk), lambda h, j: (h, j, 0, 0)), rowv],
        out_shape=[jax.ShapeDtypeStruct((T, D), BF), jax.ShapeDtypeStruct((T, D), BF),
                   jax.ShapeDtypeStruct((T, D), BF), jax.ShapeDtypeStruct((NH, nb, 1, blk), F32),
                   jax.ShapeDtypeStruct((NH, nb, 1, blk), F32)],
        scratch_shapes=[pltpu.VMEM((nb, DH + KT_ONES, blk), F32)],
        semantics=("parallel", "arbitrary"), vmem=_vmem_limit(vm), side=side)


def _norm_fn(ins, cs):
    x, = ins
    g, = cs
    r = lax.rsqrt(jnp.mean(x * x, axis=-1, keepdims=True) + EPS)
    return [x * r * g], []


def _norm_bwd_fn(ins, cs):
    x, dy, dres = ins
    g, = cs
    r = lax.rsqrt(jnp.mean(x * x, axis=-1, keepdims=True) + EPS)
    xh = x * r
    dxh = dy * g
    dx = dres + r * (dxh - xh * jnp.mean(dxh * xh, axis=-1, keepdims=True))
    return [dx], [_colsum(dy * xh)]


def _final_fn(ins, cs):
    x2, tgt = ins
    g, = cs
    r = lax.rsqrt(jnp.mean(x2 * x2, axis=-1, keepdims=True) + EPS)
    xh = x2 * r
    e = xh * g - tgt
    dy = e * (1.0 / D)
    dxh = dy * g
    dx2 = r * (dxh - xh * jnp.mean(dxh * xh, axis=-1, keepdims=True))
    return [dx2], [_colsum(0.5 * e * e * (1.0 / D)), _colsum(dy * xh)]


def _z_fn(ins, cs):
    g, h = [v.astype(F32) for v in ins]
    return [_gelu(g) * h], []


def _mix_fn(ins, cs):
    gates, ya, yb = [v.astype(F32) for v in ins]
    return [_sig(gates[:, :D]) * ya + _sig(gates[:, D:]) * yb], []


def _mix_bwd_fn(ins, cs):
    dmix, gates, ya, yb = [v.astype(F32) for v in ins]
    ga = _sig(gates[:, :D])
    gb = _sig(gates[:, D:])
    dgates = jnp.concatenate([dmix * ya * ga * (1.0 - ga), dmix * yb * gb * (1.0 - gb)], axis=1)
    return [dmix * ga, dmix * gb, dgates], []


def _z_bwd_fn(ins, cs):
    dz, g, h = [v.astype(F32) for v in ins]
    return [dz * _gelu(g), dz * h * _gelu_grad(g)], []


def _delta_fn(ins, cs):
    do, o = ins
    p = do.astype(F32) * o.astype(F32)
    lane = lax.broadcasted_iota(jnp.int32, (p.shape[0], DH), 1)
    out = jnp.zeros((p.shape[0], DH), F32)
    for hd in range(NH):
        s = jnp.sum(p[:, hd * DH:(hd + 1) * DH], axis=1, keepdims=True)
        out = jnp.where(lane == hd, s, out)
    return [out], []


def _du_all(pieces, win, T, tm=512, side=None):
    tm = min(tm, T)
    n = len(pieces)

    def body(*refs):
        w_ref, o_ref = refs[n], refs[n + 1]
        acc = None
        for (a, off), a_ref in zip(pieces, refs[:n]):
            d = _dot(a_ref[...].astype(BF), w_ref[:, off:off + a.shape[1]], "nt")
            acc = d if acc is None else acc + d
        o_ref[...] = acc

    vm = (sum(_nbytes((tm, a.shape[1]), a.dtype) for a, _ in pieces) + _nbytes(win.shape, win.dtype)
          + 2 * _nbytes((tm, D), F32))
    return _pallas(
        body, tuple(a for a, _ in pieces) + (win,), name="du_all", grid=(T // tm,),
        in_specs=[pl.BlockSpec((tm, a.shape[1]), lambda i: (i, 0)) for a, _ in pieces]
        + [pl.BlockSpec(win.shape, lambda i: (0, 0), pipeline_mode=pl.Buffered(1))],
        out_specs=[pl.BlockSpec((tm, D), lambda i: (i, 0))],
        out_shape=[jax.ShapeDtypeStruct((T, D), F32)],
        semantics=("arbitrary",), vmem=int(min(VMEM_CAP, 2 * vm + (4 << 20))), side=side)


def _local_step(x, tgt, w, T, blk=1024, dist=None):
    blk = min(blk, T)
    nb = T // blk
    win = w["win"]

    u, = _ew(_norm_fn, T, 1024, [(x, D, 0)], [w["g_mix"]], [(D, BF)], [], name="norm_mix")
    xg = _mm(u, win, "nn", T, 2 * D, D, name="proj_lru", out_dtype=BF)
    qkv = _mm(u, win, "nn", T, 3 * D, D, name="proj_qkv", out_dtype=BF, b_off=(0, 2),
              epi=lambda acc: acc * jnp.where(pl.program_id(1) == 0, _C2, 1.0))
    gates = _mm(u, win, "nn", T, 2 * D, D, name="proj_gates", b_off=(0, 5), out_dtype=BF)
    fl = _mm(u, win, "nn", T, DH, D, name="proj_f", tn=DH, b_off=(0, 7 * D // DH))
    fcum = _fgate_fwd(fl, w["fb"], T)
    qa, ka = _ew(_aug_fn, T, 512, [(qkv, D, 0), (qkv, D, 1), (fcum, DH, 0)], [],
                 [(NH * DA, BF), (NH * DA, BF)], [], name="attn_augment")

    h = _lru_fwd(xg, w["cw"], w["vec"], w["wabd"], w["wxbd"], T)
    ob, lse, *landed = _attn_fwd(qa, ka, qkv, T, blk, side=dist.weights_side() if dist else None)
    if dist:
        w = dict(w, **dist.weights_landed(landed))
    z, = _ew(_z_fn, T, 1024, [(xg, D, 1), (h, D, 0)], [], [(D, BF)], [], name="lru_gelu")
    ya = _mm(z, w["wa"], "nn", T, D, D, name="branch_a", out_dtype=BF)
    yb = _mm(ob, w["wb"], "nn", T, D, D, name="branch_b", out_dtype=BF)
    mix, = _ew(_mix_fn, T, 512, [(gates, 2 * D, 0), (ya, D, 0), (yb, D, 0)], [], [(D, BF)], [], name="mix")
    x1 = _mm(mix, w["wout"], "nn", T, D, D, name="out_proj", add=x)
    m, = _ew(_norm_fn, T, 1024, [(x1, D, 0)], [w["g_mlp"]], [(D, BF)], [], name="norm_mlp")
    hh = _mm(m, w["wup"], "nn", T, FF, D, name="mlp_up", out_dtype=BF,
             epi=lambda acc: jnp.square(jnp.maximum(acc, 0.0)))
    x2 = _mm(hh, w["wdown"], "nn", T, D, FF, name="mlp_down", add=x1, tk=2048)
    dx2, loss_vec, dg_fin = _ew(_final_fn, T, 512, [(x2, D, 0), (tgt, D, 0)], [w["g_fin"]], [(D, F32)],
                                [(1, D), (1, D)], name="final_norm_loss")

    dhpre = _mm(dx2, w["wdown"], "nt", T, FF, D, name="mlp_down_bwd", out_dtype=BF,
                epi=lambda acc, h2: acc * (2.0 * jnp.sqrt(h2.astype(F32))), epi_ins=[hh])
    dwdown = _mm(hh, dx2, "tn", FF, D, T, name="dw_down", out_dtype=BF)
    dwup = _mm(m, dhpre, "tn", D, FF, T, name="dw_up", out_dtype=BF)
    dm = _mm(dhpre, w["wup"], "nt", T, D, FF, name="mlp_up_bwd", tk=2048)
    dx1, dg_mlp = _ew(_norm_bwd_fn, T, 512, [(x1, D, 0), (dm, D, 0), (dx2, D, 0)], [w["g_mlp"]], [(D, F32)],
                      [(1, D)], name="norm_mlp_bwd")

    dmix = _mm(dx1, w["wout"], "nt", T, D, D, name="out_proj_bwd", out_dtype=BF)
    dwout = _mm(mix, dx1, "tn", D, D, T, name="dw_out", out_dtype=BF)
    dya, dyb, dgates = _ew(_mix_bwd_fn, T, 512, [(dmix, D, 0), (gates, 2 * D, 0), (ya, D, 0), (yb, D, 0)], [],
                           [(D, BF), (D, BF), (2 * D, BF)], [], name="mix_bwd")
    dob = _mm(dyb, w["wb"], "nt", T, D, D, name="branch_b_bwd", out_dtype=BF)
    dwb = _mm(ob, dyb, "tn", D, D, T, name="dw_b", out_dtype=BF)
    dz = _mm(dya, w["wa"], "nt", T, D, D, name="branch_a_bwd", out_dtype=BF)
    dwa = _mm(z, dya, "tn", D, D, T, name="dw_a", out_dtype=BF)
    dha, dglru = _ew(_z_bwd_fn, T, 512, [(dz, D, 0), (xg, D, 1), (h, D, 0)], [], [(D, F32), (D, BF)], [],
                     name="lru_gelu_bwd")

    delta, = _ew(_delta_fn, T, 1024, [(dob, D, 0), (ob, D, 0)], [], [(DH, F32)], [], name="attn_delta")
    drow = delta[:, :NH].T.reshape(NH, nb, 1, blk)
    big = dict(w_branch_a=dwa, w_branch_b=dwb, w_out=dwout, w_up=dwup, w_down=dwdown)
    side = dist.grads_side(big) if dist else None
    dq, dk, dv, dfs, dft, *landed = _attn_bwd(qa, ka, qkv, dob, lse, drow, T, blk, side=side)
    if dist:
        big = dist.grads_landed(side, landed)
    dfcum = jnp.pad((dft - dfs).reshape(NH, T).T, ((0, 0), (0, DH - NH)))
    dfl, dfb = _fgate_bwd(dfcum, fl, w["fb"], T)

    dxl, dwabd, dwxbd, lacc = _lru_bwd(xg, h, dha, w["cw"], w["vec"], w["wabd"], w["wxbd"], T)

    dproj = ((dxl, 0), (dglru, D), (dq, 2 * D), (dk, 3 * D), (dv, 4 * D), (dgates, 5 * D), (dfl, 7 * D))
    pieces = [_mm(u, p, "tn", D, p.shape[1], T, name="dw_in_%d" % n, out_dtype=BF)
              for n, (p, _) in enumerate(dproj)]
    pieces[-1] = pieces[-1][:, :NH]
    dwin = dict(w_in=jnp.concatenate(pieces, axis=1))
    side = dist.grads_side(dwin) if dist else None
    du, *landed = _du_all(dproj, win, T, side=side)
    big.update(dist.grads_landed(side, landed) if dist else dwin)
    dx, dg_mix = _ew(_norm_bwd_fn, T, 512, [(x, D, 0), (du, D, 0), (dx1, D, 0)], [w["g_mix"]], [(D, F32)],
                     [(1, D)], name="norm_mix_bwd")

    return dict(dx=dx, big=big, dwabd=dwabd, dwxbd=dwxbd, lacc=lacc, dfb=dfb, dg_mix=dg_mix, dg_mlp=dg_mlp,
                dg_fin=dg_fin, loss_vec=loss_vec)


def _block_diag(w):
    per = BD // LRU_BW
    w4 = w.reshape(NBD, per, LRU_BW, LRU_BW)
    out = jnp.zeros((NBD, per, LRU_BW, per, LRU_BW), w.dtype)
    for b in range(per):
        out = out.at[:, b, :, b, :].set(w4[:, b])
    return out.reshape(NBD, BD, BD)


def _block_diag_extract(wbd):
    per = BD // LRU_BW
    w5 = wbd.reshape(NBD, per, LRU_BW, per, LRU_BW)
    return jnp.stack([w5[:, b, :, b, :] for b in range(per)], axis=1).reshape(LRU_BLOCKS, LRU_BW, LRU_BW)


_ANY = pl.BlockSpec(memory_space=pl.ANY)


def _place():
    x, y, c = lax.axis_index("x"), lax.axis_index("y"), lax.axis_index("c")
    chips = [(1 - x, y), (x, 1 - y), (1 - x, 1 - y)]
    return x, y, c, chips


def _allgather_shards(shards):
    n = len(shards)

    def body(*refs):
        ins, outs = refs[:n], refs[n:2 * n]
        send_sems, recv_sems = refs[2 * n:]
        x, y, c, chips = _place()
        me = 2 * x + y
        sibling = (x, y, 1 - c)

        def remote(p, k, src, dst, to):
            return pltpu.make_async_remote_copy(src_ref=src, dst_ref=dst, send_sem=send_sems.at[p, k],
                                                recv_sem=recv_sems.at[p, k], device_id=to, device_id_type=MESH)

        sent = []
        for p in range(n):
            for k, chip in enumerate(chips):
                cp = remote(p, k, ins[p].at[c], outs[p].at[me, c], (chip[0], chip[1], c))
                cp.start()
                sent.append(cp)
        for p in range(n):
            for k, chip in enumerate(chips):
                half = outs[p].at[2 * chip[0] + chip[1], c]
                remote(p, k, half, half, sibling).wait_recv()
                fwd = remote(p, 3 + k, half, half, sibling)
                fwd.start()
                sent.append(fwd)
        for p in range(n):
            for k, chip in enumerate(chips):
                half = outs[p].at[2 * chip[0] + chip[1], 1 - c]
                remote(p, 3 + k, half, half, sibling).wait_recv()
        for cp in sent:
            cp.wait_send()

    gathered = pl.pallas_call(
        body, name="allgather_weights",
        in_specs=[_ANY] * n, out_specs=[_ANY] * n,
        out_shape=[jax.ShapeDtypeStruct((NCHIP,) + s.shape, s.dtype) for s in shards],
        scratch_shapes=[pltpu.SemaphoreType.DMA((n, 6)), pltpu.SemaphoreType.DMA((n, 6))],
    )(*shards)
    me = 2 * lax.axis_index("x") + lax.axis_index("y")
    return [lax.dynamic_update_index_in_dim(g, s, me, 0) for g, s in zip(gathered, shards)]


_LATE =["w_branch_a", "w_branch_b", "w_out", "w_up", "w_down"]
_COLUMN_CUT = ("w_in", "w_up")
N_PEERS = 7


def _shard_major(name, g):
    s = _columns_to_shards(g) if name in _COLUMN_CUT else g.reshape(NCHIP, g.shape[0] // NCHIP, g.shape[1])
    return s.reshape(NCHIP, 2, s.shape[1] // 2, s.shape[2])


class _Exchanges:
    def __init__(self, shards):
        self.shards = shards

    def weights_side(self):
        srcs = [self.shards[n] for n in _LATE]

        def copies(sin, sout, send, recv):
            x, y, c, chips = _place()
            return [pltpu.make_async_remote_copy(
                src_ref=sin[p], dst_ref=sout[p].at[2 * x + y], send_sem=send.at[3 * p + k], recv_sem=recv.at[3 * p + k],
                device_id=(chip[0], chip[1], c), device_id_type=MESH)
                for p in range(len(sin)) for k, chip in enumerate(chips)]

        return _Side(srcs, [jax.ShapeDtypeStruct((NCHIP,) + s.shape, s.dtype) for s in srcs], 3 * len(srcs), copies)

    def weights_landed(self, landed):
        me = 2 * lax.axis_index("x") + lax.axis_index("y")
        full = {n: lax.dynamic_update_index_in_dim(g, self.shards[n], me, 0) for n, g in zip(_LATE, landed)}
        return dict(wa=full["w_branch_a"].reshape(D, D), wb=full["w_branch_b"].reshape(D, D),
                    wout=full["w_out"].reshape(D, D), wup=_shards_to_columns(full["w_up"]),
                    wdown=full["w_down"].reshape(FF, D))

    def grads_side(self, grads):
        side_names = list(grads)
        srcs = [_shard_major(n, grads[n]) for n in side_names]

        def copies(sin, sout, send, recv):
            x, y, c, chips = _place()
            peers = [(x, y, 1 - c)] + [(cx, cy, c) for cx, cy in chips] + [(cx, cy, 1 - c) for cx, cy in chips]
            return [pltpu.make_async_remote_copy(
                src_ref=sin[p].at[2 * px + py, pc], dst_ref=sout[p].at[s], send_sem=send.at[N_PEERS * p + s],
                recv_sem=recv.at[N_PEERS * p + s], device_id=(px, py, pc), device_id_type=MESH)
                for p in range(len(sin)) for s, (px, py, pc) in enumerate(peers)]

        side = _Side(srcs, [jax.ShapeDtypeStruct((N_PEERS,) + s.shape[2:], s.dtype) for s in srcs],
                     N_PEERS * len(srcs), copies)
        side.names = side_names
        return side

    def grads_landed(self, side, landed):
        return {n: (own, got) for n, own, got in zip(side.names, side.srcs, landed)}


def _add8(g, got, me, c, name):
    _, _, half, cols = g.shape
    th = _row_tile(half, 2 * cols)

    def body(me_ref, c_ref, g_ref, r_ref, o_ref):
        acc = g_ref[...].astype(F32)
        for s in range(N_PEERS):
            acc = acc + r_ref[s].astype(F32)
        o_ref[...] = acc

    return pl.pallas_call(
        body, name=name,
        grid_spec=pltpu.PrefetchScalarGridSpec(
            num_scalar_prefetch=2, grid=(half // th,),
            in_specs=[pl.BlockSpec((None, None, th, cols), lambda i, me_ref, c_ref: (me_ref[0], c_ref[0], i, 0)),
                      pl.BlockSpec((N_PEERS, th, cols), lambda i, me_ref, c_ref: (0, i, 0))],
            out_specs=pl.BlockSpec((th, cols), lambda i, me_ref, c_ref: (i, 0))),
        out_shape=jax.ShapeDtypeStruct((half, cols), F32),
    )(me, c, g, got)


def _share_halves(halves):
    n = len(halves)

    def body(*refs):
        ins, outs = refs[:n], refs[n:2 * n]
        send_sems, recv_sems = refs[2 * n:]
        x, y, c, _ = _place()
        sibling = (x, y, 1 - c)
        copies = []
        for p in range(n):
            cp = pltpu.make_async_remote_copy(src_ref=ins[p], dst_ref=outs[p], send_sem=send_sems.at[p],
                                              recv_sem=recv_sems.at[p], device_id=sibling, device_id_type=MESH)
            cp.start()
            copies.append(cp)
        for cp in copies:
            cp.wait()

    return pl.pallas_call(
        body, name="reduce_share_halves",
        in_specs=[_ANY] * n, out_specs=[_ANY] * n,
        out_shape=[jax.ShapeDtypeStruct(h.shape, h.dtype) for h in halves],
        scratch_shapes=[pltpu.SemaphoreType.DMA((n,)), pltpu.SemaphoreType.DMA((n,))],
    )(*halves)


def _row_tile(half, cols):
    th = max(SLAB, min(half, (1 << 18) // cols // SLAB * SLAB))
    while half % th:
        th -= SLAB
    return th


N_DEV = 8
SMALL_ROWS = 208


def _allreduce_small(pack):
    def body(x_ref, out_ref, gbuf, send_sems, recv_sems, local_sem):
        x, y, c, chips = _place()
        me, sibling = (x, y, c), (x, y, 1 - c)

        def rows(px, py, pc):
            return gbuf.at[4 * px + 2 * py + pc]

        def copy(k, block, to, src=None):
            return pltpu.make_async_remote_copy(
                src_ref=rows(*block) if src is None else src, dst_ref=rows(*block),
                send_sem=send_sems.at[k], recv_sem=recv_sems.at[k], device_id=to, device_id_type=MESH)

        mine = pltpu.make_async_copy(x_ref, rows(*me), local_sem)
        mine.start()
        first = [copy(0, me, sibling, src=x_ref)]
        first += [copy(1 + j, me, (chip[0], chip[1], c), src=x_ref) for j, chip in enumerate(chips)]
        for cp in first:
            cp.start()
        passed = [copy(4 + j, (chip[0], chip[1], c), sibling) for j, chip in enumerate(chips)]
        for j, chip in enumerate(chips):
            copy(1 + j, (chip[0], chip[1], c), me).wait_recv()
            passed[j].start()
        copy(0, sibling, me).wait_recv()
        for j, chip in enumerate(chips):
            copy(4 + j, (chip[0], chip[1], 1 - c), me).wait_recv()
        for cp in first + passed:
            cp.wait_send()
        mine.wait()
        acc = gbuf[0]
        for d in range(1, N_DEV):
            acc = acc + gbuf[d]
        out_ref[...] = acc

    return pl.pallas_call(
        body, name="allreduce_small",
        in_specs=[pl.BlockSpec(memory_space=pltpu.VMEM)],
        out_specs=pl.BlockSpec(memory_space=pltpu.VMEM),
        out_shape=jax.ShapeDtypeStruct((SMALL_ROWS, D), F32),
        scratch_shapes=[pltpu.VMEM((N_DEV, SMALL_ROWS, D), F32), pltpu.SemaphoreType.DMA((7,)),
                        pltpu.SemaphoreType.DMA((7,)), pltpu.SemaphoreType.DMA],
    )(pack)


def _adamw(w, g, m, v, name):
    rows, cols = w.shape

    def body(w_ref, g_ref, m_ref, v_ref, d_ref, mo_ref, vo_ref):
        gv = g_ref[...]
        mn = ADAM_B1 * m_ref[...] + (1.0 - ADAM_B1) * gv
        vn = ADAM_B2 * v_ref[...] + (1.0 - ADAM_B2) * (gv * gv)
        m_hat = mn / (1.0 - ADAM_B1 ** ADAM_STEP)
        v_hat = vn / (1.0 - ADAM_B2 ** ADAM_STEP)
        d_ref[...] = -ADAM_LR * (m_hat / (jnp.sqrt(v_hat) + ADAM_EPS) + ADAM_WD * w_ref[...])
        mo_ref[...] = mn
        vo_ref[...] = vn

    if rows % SLAB:
        spec, steps = pl.BlockSpec((rows, DH), lambda i: (0, i)), cols // DH
    else:
        th = _row_tile(rows, cols)
        spec, steps = pl.BlockSpec((th, cols), lambda i: (i, 0)), rows // th
    return pl.pallas_call(
        body, name=name, grid=(steps,),
        in_specs=[spec] * 4, out_specs=[spec] * 3,
        out_shape=[jax.ShapeDtypeStruct((rows, cols), F32)] * 3,
        compiler_params=pltpu.CompilerParams(dimension_semantics=("parallel",)),
    )(w, g, m, v)


_SMALL = ["norm_mix_g", "norm_mlp_g", "norm_final_g", "conv_b", "lru_ba", "lru_bx", "lru_lambda"]
_ROW_FB, _ROW_CW, _ROW_WA, _ROW_WX, _ROW_LOSS = 56, 64, 72, 136, 200


def _pack_small(vals, col0):
    def slab(a):
        return jnp.pad(a, ((0, -a.shape[0] % SLAB), (0, D - a.shape[1])))

    rows = [slab(vals[n].reshape(1, D)) for n in _SMALL]
    rows.append(slab(vals["forget_b"].reshape(1, NH)))
    if vals["conv_w"].shape[1] == D:
        rows.append(slab(vals["conv_w"]))
    else:
        rows.append(slab(lax.dynamic_update_slice(jnp.zeros((CONV, D), F32), vals["conv_w"], (0, col0))))
    rows.append(vals["lru_wa"].reshape(LRU_BLOCKS * LRU_BW * LRU_BW // D, D))
    rows.append(vals["lru_wx"].reshape(LRU_BLOCKS * LRU_BW * LRU_BW // D, D))
    rows.append(slab(vals["loss"]) if "loss" in vals else jnp.zeros((SLAB, D), F32))
    return jnp.concatenate(rows, axis=0)


def _unpack_small(pack, col0):
    out = {n: pack[SLAB * i] for i, n in enumerate(_SMALL)}
    out["forget_b"] = pack[_ROW_FB, :NH]
    out["conv_w"] = lax.dynamic_slice(pack[_ROW_CW:_ROW_CW + CONV], (0, col0), (CONV, D // NCHIP))
    out["lru_wa"] = pack[_ROW_WA:_ROW_WX].reshape(LRU_BLOCKS, LRU_BW, LRU_BW)
    out["lru_wx"] = pack[_ROW_WX:_ROW_LOSS].reshape(LRU_BLOCKS, LRU_BW, LRU_BW)
    return out


_WEIGHTS = ["norm_mix_g", "w_in", "conv_w", "conv_b", "lru_wa", "lru_ba", "lru_wx", "lru_bx", "lru_lambda",
            "forget_b", "w_branch_a", "w_branch_b", "w_out", "norm_mlp_g", "w_up", "w_down", "norm_final_g"]
_BIG = ["w_in", "w_branch_a", "w_branch_b", "w_out", "w_up", "w_down"]


def _halves(a):
    return a.reshape(2, a.shape[0] // 2, a.shape[1])


def _columns_to_shards(a):
    rows, cols = a.shape[0], a.shape[1] // NCHIP
    return jnp.transpose(a.reshape(rows, NCHIP, cols), (1, 0, 2))


def _shards_to_columns(a):
    n, rows, cols = a.shape
    return jnp.transpose(a, (1, 0, 2)).reshape(rows, n * cols)


def kernel(x, norm_mix_g, w_in, conv_w, conv_b, lru_wa, lru_ba, lru_wx, lru_bx, lru_lambda, forget_b, w_branch_a, w_branch_b, w_out, norm_mlp_g, w_up, w_down, norm_final_g, loss_target, m_norm_mix_g, m_w_in, m_conv_w, m_conv_b, m_lru_wa, m_lru_ba, m_lru_wx, m_lru_bx, m_lru_lambda, m_forget_b, m_w_branch_a, m_w_branch_b, m_w_out, m_norm_mlp_g, m_w_up, m_w_down, m_norm_final_g, v_norm_mix_g, v_w_in, v_conv_w, v_conv_b, v_lru_wa, v_lru_ba, v_lru_wx, v_lru_bx, v_lru_lambda, v_forget_b, v_w_branch_a, v_w_branch_b, v_w_out, v_norm_mlp_g, v_w_up, v_w_down, v_norm_final_g):
    args = dict(locals())
    wts = {n: args[n] for n in _WEIGHTS}
    mom = {n: args["m_" + n] for n in _WEIGHTS}
    var = {n: args["v_" + n] for n in _WEIGHTS}
    T = x.shape[1]
    xi, yi, ci = lax.axis_index("x"), lax.axis_index("y"), lax.axis_index("c")
    me = 2 * xi + yi
    c1 = jnp.reshape(ci, (1,)).astype(jnp.int32)
    me1 = jnp.reshape(me, (1,)).astype(jnp.int32)
    col0 = me * (D // NCHIP)

    cw_pad = jnp.pad(conv_w, ((0, 4 * SLAB - CONV), (0, 0)))
    g_in, g_cw = _allgather_shards([_halves(w_in.astype(BF)), _halves(cw_pad)])
    cin = DIN // NCHIP
    win = _shards_to_columns(g_in.reshape(NCHIP, D, cin))
    w = dict(
        win=jnp.pad(win, ((0, 0), (0, DINP - DIN))),
        cw=_shards_to_columns(g_cw.reshape(NCHIP, 4 * SLAB, D // NCHIP)[:, :CONV]),
        vec=jnp.concatenate([conv_b[None], lru_ba[None], lru_bx[None], lru_lambda[None],
                             jnp.zeros((SLAB - 4, D), F32)], axis=0),
        fb=jnp.pad(forget_b[None], ((0, 0), (0, DH - NH))),
        wabd=_block_diag(lru_wa).astype(BF), wxbd=_block_diag(lru_wx).astype(BF),
        g_mix=norm_mix_g[None], g_mlp=norm_mlp_g[None], g_fin=norm_final_g[None])

    r = _local_step(x[0], loss_target[0], w, T, dist=_Exchanges({n: wts[n].astype(BF) for n in _LATE}))

    halves = [_add8(*r["big"][n], me1, c1, "add8_" + n) for n in _BIG]
    theirs = _share_halves(halves)
    low = ci == 0
    gsum = {n: jnp.concatenate([jnp.where(low, h, t), jnp.where(low, t, h)], axis=0)
            for n, h, t in zip(_BIG, halves, theirs)}
    lacc = r["lacc"]
    small = dict(norm_mix_g=r["dg_mix"], norm_mlp_g=r["dg_mlp"], norm_final_g=r["dg_fin"], conv_b=lacc[3],
                 lru_ba=lacc[0], lru_bx=lacc[1], lru_lambda=lacc[2], forget_b=r["dfb"][0, :NH],
                 conv_w=lacc[4:4 + CONV], lru_wa=_block_diag_extract(r["dwabd"]),
                 lru_wx=_block_diag_extract(r["dwxbd"]), loss=r["loss_vec"])
    gpack = _allreduce_small(_pack_small(small, col0))
    loss = jnp.sum(gpack[_ROW_LOSS])

    grads, delta, new_m, new_v = {}, {}, {}, {}
    for n in _BIG:
        if n == "w_in":
            gt = gsum[n].T
            grads[n] = gt.T
            delta[n], new_m[n], new_v[n] = [a.T for a in _adamw(wts[n].T, gt, mom[n].T, var[n].T, "adamw_" + n)]
        else:
            grads[n] = gsum[n]
            delta[n], new_m[n], new_v[n] = _adamw(wts[n], gsum[n], mom[n], var[n], "adamw_" + n)
    dp, mp, vp = _adamw(_pack_small(wts, col0), gpack, _pack_small(mom, col0), _pack_small(var, col0), "adamw_small")
    for dst, pack in ((grads, gpack), (delta, dp), (new_m, mp), (new_v, vp)):
        dst.update(_unpack_small(pack, col0))
    return (loss, r["dx"][None], *[grads[n] for n in _WEIGHTS], *[delta[n] for n in _WEIGHTS],
            *[new_m[n] for n in _WEIGHTS], *[new_v[n] for n in _WEIGHTS])
```

```python
import functools
import math

import jax
import jax.numpy as jnp
import numpy as np
from jax import lax
from jax.experimental import pallas as pl
from jax.experimental.pallas import tpu as pltpu

F32 = jnp.float32
BF = jnp.bfloat16

D = 1024
NH = 8
DH = 128
FF = 4096
CONV = 4
LRU_BLOCKS = 16
LRU_BW = 64
BD = 256
NBD = D // BD
LRU_C = 8.0
EPS = 1e-6
DIN = 7176
DINP = 7296
NCHIP = 4
SLAB = 8
VMEM_CAP = 60 * 1024 * 1024

ADAM_LR = 0.001
ADAM_B1 = 0.9
ADAM_B2 = 0.999
ADAM_EPS = 1e-08
ADAM_WD = 0.01
ADAM_STEP = 10

MESH = pl.DeviceIdType.MESH


def _vmem_limit(nbytes):
    return int(min(VMEM_CAP, max(32 * 1024 * 1024, 3 * nbytes)))


def _nbytes(shape, dtype):
    return int(np.prod(shape)) * jnp.dtype(dtype).itemsize


def _sig(x):
    return 0.5 * jnp.tanh(0.5 * x) + 0.5


def _log1p(u):
    w = 1.0 + u
    return jnp.where(w == 1.0, u, jnp.log(w) * (u / (w - 1.0)))


def _one_minus_sq(a, la):
    return jnp.tanh(-la) * (1.0 + a * a)


def _softplus(z):
    return jnp.maximum(z, 0.0) + _log1p(jnp.exp(-jnp.abs(z)))


_GELU_C = math.sqrt(2.0 / math.pi)


def _gelu(x):
    return 0.5 * x * (1.0 + jnp.tanh(_GELU_C * (x + 0.044715 * x * x * x)))


def _gelu_grad(x):
    t = jnp.tanh(_GELU_C * (x + 0.044715 * x * x * x))
    return 0.5 * (1.0 + t) + 0.5 * x * (1.0 - t * t) * _GELU_C * (1.0 + 3.0 * 0.044715 * x * x)


def _slab_scan_fwd(a, b):
    row = lax.broadcasted_iota(jnp.int32, a.shape, 0)
    for k in (1, 2, 4):
        a_s = pltpu.roll(a, k, 0)
        b_s = pltpu.roll(b, k, 0)
        m = row >= k
        b = jnp.where(m, a * b_s + b, b)
        a = jnp.where(m, a * a_s, a)
    return a, b


def _slab_scan_bwd(a, b):
    row = lax.broadcasted_iota(jnp.int32, a.shape, 0)
    for k in (1, 2, 4):
        a_s = pltpu.roll(a, SLAB - k, 0)
        b_s = pltpu.roll(b, SLAB - k, 0)
        m = row < SLAB - k
        b = jnp.where(m, a * b_s + b, b)
        a = jnp.where(m, a * a_s, a)
    return a, b


_DIMS = {"nn": (((1,), (0,)), ((), ())), "nt": (((1,), (1,)), ((), ())), "tn": (((0,), (0,)), ((), ()))}


def _dot(a, b, mode="nn"):
    return lax.dot_general(a, b, _DIMS[mode], preferred_element_type=F32)


def _mm(a, b, mode, M, N, K, *, name, out_dtype=F32, tm=None, tn=1024, tk=1024,
        a_off=(0, 0), b_off=(0, 0), add=None, epi=None, epi_ins=()):
    if tm is None:
        tm = 2048 if (K <= tk and out_dtype == BF and add is None) else 1024
    tm, tn, tk = min(tm, M), min(tn, N), min(tk, K)
    nk = K // tk
    grid = (M // tm, N // tn, nk)
    if mode == "nn":
        a_spec = pl.BlockSpec((tm, tk), lambda i, j, k: (i + a_off[0], k + a_off[1]))
        b_spec = pl.BlockSpec((tk, tn), lambda i, j, k: (k + b_off[0], j + b_off[1]))
    elif mode == "nt":
        a_spec = pl.BlockSpec((tm, tk), lambda i, j, k: (i + a_off[0], k + a_off[1]))
        b_spec = pl.BlockSpec((tn, tk), lambda i, j, k: (j + b_off[0], k + b_off[1]))
    else:
        a_spec = pl.BlockSpec((tk, tm), lambda i, j, k: (k + a_off[0], i + a_off[1]))
        b_spec = pl.BlockSpec((tk, tn), lambda i, j, k: (k + b_off[0], j + b_off[1]))
    o_spec = pl.BlockSpec((tm, tn), lambda i, j, k: (i, j))
    extra = ([add] if add is not None else []) + list(epi_ins)
    n_extra = len(extra)
    has_add = add is not None

    def body(*refs):
        a_ref, b_ref = refs[0], refs[1]
        ex = refs[2:2 + n_extra]
        o_ref = refs[2 + n_extra]

        def finish(acc):
            if has_add:
                acc = acc + ex[0][...].astype(F32)
            if epi is not None:
                acc = epi(acc, *[e[...] for e in ex[(1 if has_add else 0):]])
            o_ref[...] = acc.astype(o_ref.dtype)

        p = _dot(a_ref[...].astype(BF), b_ref[...].astype(BF), mode)
        if nk == 1:
            finish(p)
        else:
            acc_ref = refs[3 + n_extra]
            k = pl.program_id(2)

            @pl.when(k == 0)
            def _():
                acc_ref[...] = p

            @pl.when(k > 0)
            def _():
                acc_ref[...] += p

            @pl.when(k == nk - 1)
            def _():
                finish(acc_ref[...])

    blk = (_nbytes((tm, tk), a.dtype) + _nbytes((tk, tn), b.dtype) + _nbytes((tm, tn), out_dtype)
           + sum(_nbytes((tm, tn), e.dtype) for e in extra) + 2 * _nbytes((tm, tn), F32))
    return pl.pallas_call(
        body, name=name, grid=grid,
        in_specs=[a_spec, b_spec] + [o_spec] * n_extra,
        out_specs=o_spec,
        out_shape=jax.ShapeDtypeStruct((M, N), out_dtype),
        scratch_shapes=[pltpu.VMEM((tm, tn), F32)] if nk > 1 else [],
        compiler_params=pltpu.CompilerParams(
            dimension_semantics=("parallel", "parallel", "arbitrary"), vmem_limit_bytes=_vmem_limit(blk)),
    )(a, b, *extra)


def _ew(fn, T, tm, ins, consts, outs, accs, *, name, reverse=False):
    tm = min(tm, T)
    nt = T // tm
    n_in, n_c, n_o, n_a = len(ins), len(consts), len(outs), len(accs)

    def row(i):
        return nt - 1 - i if reverse else i

    in_specs = [pl.BlockSpec((tm, w), functools.partial(lambda i, cb: (row(i), cb), cb=cb)) for (_, w, cb) in ins]
    in_specs += [pl.BlockSpec(c.shape, functools.partial(lambda i, nd: (0,) * nd, nd=c.ndim)) for c in consts]
    out_specs = [pl.BlockSpec((tm, w), lambda i: (row(i), 0)) for (w, _) in outs]
    out_specs += [pl.BlockSpec((r, w), lambda i: (0, 0)) for (r, w) in accs]
    out_shape = [jax.ShapeDtypeStruct((T, w), dt) for (w, dt) in outs]
    out_shape += [jax.ShapeDtypeStruct((r, w), F32) for (r, w) in accs]

    def body(*refs):
        in_refs = refs[:n_in]
        c_refs = refs[n_in:n_in + n_c]
        o_refs = refs[n_in + n_c:n_in + n_c + n_o]
        a_refs = refs[n_in + n_c + n_o:]
        ov, av = fn([r[...] for r in in_refs], [r[...] for r in c_refs])
        for r, v in zip(o_refs, ov):
            r[...] = v.astype(r.dtype)
        if n_a:
            i = pl.program_id(0)

            @pl.when(i == 0)
            def _():
                for r, v in zip(a_refs, av):
                    r[...] = v

            @pl.when(i > 0)
            def _():
                for r, v in zip(a_refs, av):
                    r[...] += v

    blk = (sum(_nbytes((tm, w), a.dtype) for (a, w, _) in ins) + sum(_nbytes(c.shape, c.dtype) for c in consts)
           + sum(_nbytes((tm, w), dt) for (w, dt) in outs) + sum(_nbytes(s, F32) for s in accs))
    res = pl.pallas_call(
        body, name=name, grid=(nt,), in_specs=in_specs, out_specs=out_specs, out_shape=out_shape,
        compiler_params=pltpu.CompilerParams(
            dimension_semantics=("arbitrary",), vmem_limit_bytes=_vmem_limit(blk)),
    )(*[a for (a, _, _) in ins], *consts)
    return res


def _colsum(v):
    return jnp.sum(v, axis=0, keepdims=True)


FGATE_GROUP = 4


def _fgate_fwd(fl, fb, T, tm=512):
    tm = min(tm, T)

    def body(fl_ref, fb_ref, f_ref, carry_ref):
        i = pl.program_id(0)

        @pl.when(i == 0)
        def _():
            carry_ref[...] = jnp.zeros_like(carry_ref)

        rows = FGATE_GROUP * SLAB
        sub = lax.broadcasted_iota(jnp.int32, (rows, DH), 0) % SLAB

        def group(s, carry):
            r0 = pl.multiple_of(s * rows, rows)
            z = fl_ref[pl.ds(r0, rows), :] + fb_ref[...]
            c = jnp.minimum(z, 0.0) - _log1p(jnp.exp(-jnp.abs(z)))
            for k in (1, 2, 4):
                c = c + jnp.where(sub >= k, pltpu.roll(c, k, 0), 0.0)
            for u in range(FGATE_GROUP):
                cu = c[u * SLAB:(u + 1) * SLAB] + carry
                f_ref[pl.ds(r0 + u * SLAB, SLAB), :] = cu
                carry = cu[SLAB - 1:SLAB, :]
            return carry

        carry_ref[0:1, :] = lax.fori_loop(0, tm // rows, group, carry_ref[0:1, :])

    return pl.pallas_call(
        body, name="fgate_fwd", grid=(T // tm,),
        in_specs=[pl.BlockSpec((tm, DH), lambda i: (i, 0)), pl.BlockSpec((1, DH), lambda i: (0, 0))],
        out_specs=pl.BlockSpec((tm, DH), lambda i: (i, 0)),
        out_shape=jax.ShapeDtypeStruct((T, DH), F32),
        scratch_shapes=[pltpu.VMEM((SLAB, DH), F32)],
        compiler_params=pltpu.CompilerParams(dimension_semantics=("arbitrary",)),
    )(fl, fb)


def _fgate_bwd(dF, fl, fb, T, tm=512):
    tm = min(tm, T)
    nt = T // tm

    def body(df_ref, fl_ref, fb_ref, o_ref, acc_ref, carry_ref):
        i = pl.program_id(0)

        @pl.when(i == 0)
        def _():
            carry_ref[...] = jnp.zeros_like(carry_ref)
            acc_ref[...] = jnp.zeros_like(acc_ref)

        rows = FGATE_GROUP * SLAB
        sub = lax.broadcasted_iota(jnp.int32, (rows, DH), 0) % SLAB

        def group(n, carry):
            g_next, acc = carry
            r0 = pl.multiple_of((tm // rows - 1 - n) * rows, rows)
            c = df_ref[pl.ds(r0, rows), :]
            for k in (1, 2, 4):
                c = c + jnp.where(sub < SLAB - k, pltpu.roll(c, rows - k, 0), 0.0)
            sg = _sig(-(fl_ref[pl.ds(r0, rows), :] + fb_ref[...]))
            for u in reversed(range(FGATE_GROUP)):
                cu = c[u * SLAB:(u + 1) * SLAB] + g_next
                dfl = cu * sg[u * SLAB:(u + 1) * SLAB]
                o_ref[pl.ds(r0 + u * SLAB, SLAB), :] = dfl.astype(o_ref.dtype)
                acc = acc + _colsum(dfl)
                g_next = cu[0:1, :]
            return g_next, acc

        g, acc = lax.fori_loop(0, tm // rows, group, (carry_ref[0:1, :], jnp.zeros((1, DH), F32)))
        carry_ref[0:1, :] = g
        acc_ref[...] += acc

    return pl.pallas_call(
        body, name="fgate_bwd", grid=(nt,),
        in_specs=[pl.BlockSpec((tm, DH), lambda i: (nt - 1 - i, 0)), pl.BlockSpec((tm, DH), lambda i: (nt - 1 - i, 0)),
                  pl.BlockSpec((1, DH), lambda i: (0, 0))],
        out_specs=[pl.BlockSpec((tm, DH), lambda i: (nt - 1 - i, 0)), pl.BlockSpec((1, DH), lambda i: (0, 0))],
        out_shape=[jax.ShapeDtypeStruct((T, DH), BF), jax.ShapeDtypeStruct((1, DH), F32)],
        scratch_shapes=[pltpu.VMEM((SLAB, DH), F32)],
        compiler_params=pltpu.CompilerParams(dimension_semantics=("arbitrary",)),
    )(dF, fl, fb)


def _shift_down(x, d, prev8):
    n = x.shape[0]
    row8 = lax.broadcasted_iota(jnp.int32, (SLAB, x.shape[1]), 0)
    y = pltpu.roll(x, d, 0)
    top = jnp.where(row8 < d, pltpu.roll(prev8, d, 0), y[0:SLAB])
    return jnp.concatenate([top, y[SLAB:]], axis=0)


def _shift_up(x, d, next8):
    n = x.shape[0]
    row8 = lax.broadcasted_iota(jnp.int32, (SLAB, x.shape[1]), 0)
    y = pltpu.roll(x, n - d, 0)
    bottom = jnp.where(row8 >= SLAB - d, pltpu.roll(next8, SLAB - d, 0), y[n - SLAB:])
    return jnp.concatenate([y[:n - SLAB], bottom], axis=0)


def _conv(x, prev8, cw, cb):
    xs = [x] + [_shift_down(x, d, prev8) for d in (1, 2, 3)]
    xa = cb + cw[3:4, :] * xs[0] + cw[2:3, :] * xs[1] + cw[1:2, :] * xs[2] + cw[0:1, :] * xs[3]
    return xa, xs


def _lru_gates(xa_g, wa_g, wx_g, ba_g, bx_g, sp_g):
    xb = xa_g.astype(BF)
    r = _sig(_dot(xb, wa_g) + ba_g)
    ig = _sig(_dot(xb, wx_g) + bx_g)
    la = -LRU_C * r * sp_g
    a = jnp.exp(la)
    mult = jnp.sqrt(_one_minus_sq(a, la))
    return r, ig, a, mult


def _lru_fwd(xg, cw, vec, wabd, wxbd, T, tm=256):
    tm = min(tm, T)
    nsl = tm // SLAB

    def body(x_ref, xp_ref, cw_ref, vec_ref, wa_ref, wx_ref, h_ref, a_s, b_s, carry_ref):
        i = pl.program_id(0)

        @pl.when(i == 0)
        def _():
            carry_ref[...] = jnp.zeros_like(carry_ref)

        x = x_ref[...].astype(F32)
        prev8 = jnp.where(i > 0, xp_ref[SLAB:, :].astype(F32), 0.0)
        vec_v = vec_ref[...]
        xa, _ = _conv(x, prev8, cw_ref[...], vec_v[0:1, :])
        sp = _softplus(-vec_v[3:4, :])
        for g in range(NBD):
            sl = slice(g * BD, (g + 1) * BD)
            _, ig, a, mult = _lru_gates(xa[:, sl], wa_ref[g], wx_ref[g], vec_v[1:2, sl], vec_v[2:3, sl], sp[:, sl])
            a_s[:, sl] = a
            b_s[:, sl] = mult * ig * xa[:, sl]

        def slab(s, carry):
            r0 = pl.multiple_of(s * SLAB, SLAB)
            A, B = _slab_scan_fwd(a_s[pl.ds(r0, SLAB), :], b_s[pl.ds(r0, SLAB), :])
            h = A * carry + B
            h_ref[pl.ds(r0, SLAB), :] = h
            return h[SLAB - 1:SLAB, :]

        carry_ref[0:1, :] = lax.fori_loop(0, nsl, slab, carry_ref[0:1, :])

    blk = 5 * _nbytes((tm, D), F32) + 2 * _nbytes((NBD, BD, BD), BF)
    return pl.pallas_call(
        body, name="lru_fwd", grid=(T // tm,),
        in_specs=[pl.BlockSpec((tm, D), lambda i: (i, 0)),
                  pl.BlockSpec((2 * SLAB, D), lambda i: (jnp.maximum(i * (nsl // 2) - 1, 0), 0)),
                  pl.BlockSpec((CONV, D), lambda i: (0, 0)),
                  pl.BlockSpec((SLAB, D), lambda i: (0, 0)),
                  pl.BlockSpec((NBD, BD, BD), lambda i: (0, 0, 0)),
                  pl.BlockSpec((NBD, BD, BD), lambda i: (0, 0, 0))],
        out_specs=pl.BlockSpec((tm, D), lambda i: (i, 0)),
        out_shape=jax.ShapeDtypeStruct((T, D), F32),
        scratch_shapes=[pltpu.VMEM((tm, D), F32), pltpu.VMEM((tm, D), F32), pltpu.VMEM((SLAB, D), F32)],
        compiler_params=pltpu.CompilerParams(dimension_semantics=("arbitrary",), vmem_limit_bytes=_vmem_limit(blk)),
    )(xg, xg, cw, vec, wabd, wxbd)


def _lru_bwd(xg, h, dha, cw, vec, wabd, wxbd, T, tm=256):
    tm = min(tm, T)
    nsl = tm // SLAB
    nt = T // tm

    def body(x_ref, xp_ref, h_ref, hp_ref, dh_ref, cw_ref, vec_ref, wa_ref, wx_ref,
             dx_ref, dwa_ref, dwx_ref, acc_ref, a_s, b_s, g_s, dxa_s, carry_ref, dxan_ref):
        n = pl.program_id(0)
        it = nt - 1 - n

        @pl.when(n == 0)
        def _():
            carry_ref[...] = jnp.zeros_like(carry_ref)
            dxan_ref[...] = jnp.zeros_like(dxan_ref)
            dwa_ref[...] = jnp.zeros_like(dwa_ref)
            dwx_ref[...] = jnp.zeros_like(dwx_ref)
            acc_ref[...] = jnp.zeros_like(acc_ref)

        x = x_ref[...].astype(F32)
        prev8 = jnp.where(it > 0, xp_ref[SLAB:, :].astype(F32), 0.0)
        hprev8 = jnp.where(it > 0, hp_ref[...], 0.0)
        vec_v = vec_ref[...]
        cw_v = cw_ref[...]
        xa, xs = _conv(x, prev8, cw_v, vec_v[0:1, :])
        sp = _softplus(-vec_v[3:4, :])
        gates = []
        for g in range(NBD):
            sl = slice(g * BD, (g + 1) * BD)
            r, ig, a, mult = _lru_gates(xa[:, sl], wa_ref[g], wx_ref[g], vec_v[1:2, sl], vec_v[2:3, sl], sp[:, sl])
            gates.append((r, ig, a, mult))
            a_s[:, sl] = a
        a_s[...] = _shift_up(a_s[...], 1, carry_ref[...])
        b_s[...] = dh_ref[...]

        def slab(m, carry):
            r0 = pl.multiple_of((nsl - 1 - m) * SLAB, SLAB)
            A, B = _slab_scan_bwd(a_s[pl.ds(r0, SLAB), :], b_s[pl.ds(r0, SLAB), :])
            gg = A * carry + B
            g_s[pl.ds(r0, SLAB), :] = gg
            return gg[0:1, :]

        g_first = lax.fori_loop(0, nsl, slab, carry_ref[1:2, :])
        gt = g_s[...]
        h_prev = _shift_down(h_ref[...], 1, hprev8)
        dba = []
        dbx = []
        dsp = []
        for g in range(NBD):
            sl = slice(g * BD, (g + 1) * BD)
            r, ig, a, mult = gates[g]
            xa_g = xa[:, sl]
            g_g = gt[:, sl]
            da = g_g * h_prev[:, sl]
            dmult = g_g * ig * xa_g
            di = g_g * mult * xa_g
            dxa_g = g_g * mult * ig
            dla = da * a - dmult * (a * a / mult)
            dr = dla * (-LRU_C) * sp[:, sl]
            dsp.append(_colsum(dla * (-LRU_C) * r))
            dra = (dr * r * (1.0 - r))
            dix = (di * ig * (1.0 - ig))
            dba.append(_colsum(dra))
            dbx.append(_colsum(dix))
            dra_b = dra.astype(BF)
            dix_b = dix.astype(BF)
            xb = xa_g.astype(BF)
            dxa_g = dxa_g + _dot(dra_b, wa_ref[g], "nt") + _dot(dix_b, wx_ref[g], "nt")
            dwa_ref[g] += _dot(xb, dra_b, "tn")
            dwx_ref[g] += _dot(xb, dix_b, "tn")
            dxa_s[:, sl] = dxa_g
        dxa = dxa_s[...]
        nxt = dxan_ref[...]
        dx = (cw_v[3:4, :] * dxa + cw_v[2:3, :] * _shift_up(dxa, 1, nxt)
              + cw_v[1:2, :] * _shift_up(dxa, 2, nxt) + cw_v[0:1, :] * _shift_up(dxa, 3, nxt))
        dx_ref[...] = dx.astype(dx_ref.dtype)
        acc_ref[0:1, :] += jnp.concatenate(dba, axis=1)
        acc_ref[1:2, :] += jnp.concatenate(dbx, axis=1)
        acc_ref[2:3, :] += jnp.concatenate(dsp, axis=1)
        acc_ref[3:4, :] += _colsum(dxa)
        for k in range(CONV):
            acc_ref[4 + k:5 + k, :] += _colsum(dxa * xs[CONV - 1 - k])
        dxan_ref[...] = dxa[0:SLAB, :]
        a_first = jnp.concatenate([gates[g][2][0:1, :] for g in range(NBD)], axis=1)
        carry_ref[0:1, :] = a_first
        carry_ref[1:2, :] = g_first

        @pl.when(n == nt - 1)
        def _():
            acc_ref[2:3, :] = acc_ref[2:3, :] * (-_sig(-vec_v[3:4, :]))

    rowblk = lambda i: (nt - 1 - i, 0)
    prevblk = lambda i: (jnp.maximum((nt - 1 - i) * nsl - 1, 0), 0)
    c2 = lambda i: (0, 0)
    c3 = lambda i: (0, 0, 0)
    blk = 12 * _nbytes((tm, D), F32) + 6 * _nbytes((NBD, BD, BD), F32)
    return pl.pallas_call(
        body, name="lru_bwd", grid=(nt,),
        in_specs=[pl.BlockSpec((tm, D), rowblk),
                  pl.BlockSpec((2 * SLAB, D), lambda i: (jnp.maximum((nt - 1 - i) * (nsl // 2) - 1, 0), 0)),
                  pl.BlockSpec((tm, D), rowblk), pl.BlockSpec((SLAB, D), prevblk),
                  pl.BlockSpec((tm, D), rowblk),
                  pl.BlockSpec((CONV, D), c2), pl.BlockSpec((SLAB, D), c2),
                  pl.BlockSpec((NBD, BD, BD), c3), pl.BlockSpec((NBD, BD, BD), c3)],
        out_specs=[pl.BlockSpec((tm, D), rowblk), pl.BlockSpec((NBD, BD, BD), c3), pl.BlockSpec((NBD, BD, BD), c3),
                   pl.BlockSpec((16, D), c2)],
        out_shape=[jax.ShapeDtypeStruct((T, D), BF), jax.ShapeDtypeStruct((NBD, BD, BD), F32),
                   jax.ShapeDtypeStruct((NBD, BD, BD), F32), jax.ShapeDtypeStruct((16, D), F32)],
        scratch_shapes=[pltpu.VMEM((tm, D), F32), pltpu.VMEM((tm, D), F32), pltpu.VMEM((tm, D), F32),
                        pltpu.VMEM((tm, D), F32), pltpu.VMEM((SLAB, D), F32), pltpu.VMEM((SLAB, D), F32)],
        compiler_params=pltpu.CompilerParams(dimension_semantics=("arbitrary",), vmem_limit_bytes=_vmem_limit(blk)),
    )(xg, xg, h, h, dha, cw, vec, wabd, wxbd)


_SCALE = 1.0 / math.sqrt(DH)


_ANY = pl.BlockSpec(memory_space=pl.ANY)


class _Side:
    def __init__(self, srcs, outs, nsem, copies):
        self.srcs, self.outs, self.nsem, self.copies = list(srcs), list(outs), nsem, copies


def _pallas(body, operands, *, name, grid, in_specs, out_specs, out_shape, scratch_shapes=(), semantics,
            vmem=None, side=None):
    if side is None:
        return pl.pallas_call(
            body, name=name, grid=grid, in_specs=in_specs, out_specs=out_specs, out_shape=out_shape,
            scratch_shapes=list(scratch_shapes),
            compiler_params=pltpu.CompilerParams(dimension_semantics=semantics, vmem_limit_bytes=vmem),
        )(*operands)
    n_in, n_out, n_scr = len(in_specs), len(out_specs), len(scratch_shapes)
    ns, no = len(side.srcs), len(side.outs)

    def hosted(*refs):
        ins, refs = refs[:n_in], refs[n_in:]
        sin, refs = refs[:ns], refs[ns:]
        outs, refs = refs[:n_out], refs[n_out:]
        sout, refs = refs[:no], refs[no:]
        scr, (send, recv) = refs[:n_scr], refs[n_scr:]
        ids = [pl.program_id(a) for a in range(len(grid))]
        first = functools.reduce(jnp.logical_and, [i == 0 for i in ids])
        last = functools.reduce(jnp.logical_and, [i == g - 1 for i, g in zip(ids, grid)])

        @pl.when(first)
        def _():
            for cp in side.copies(sin, sout, send, recv):
                cp.start()

        body(*ins, *outs, *scr)

        @pl.when(last)
        def _():
            for cp in side.copies(sin, sout, send, recv):
                cp.wait()

    return pl.pallas_call(
        hosted, name=name, grid=grid, in_specs=list(in_specs) + [_ANY] * ns, out_specs=list(out_specs) + [_ANY] * no,
        out_shape=list(out_shape) + side.outs,
        scratch_shapes=list(scratch_shapes) + [pltpu.SemaphoreType.DMA((side.nsem,)), pltpu.SemaphoreType.DMA((side.nsem,))],
        compiler_params=pltpu.CompilerParams(dimension_semantics=("arbitrary",) * len(grid), vmem_limit_bytes=vmem),
    )(*operands, *side.srcs)


DA = 2 * DH
_LOG2E = math.log2(math.e)
_C2 = _SCALE * _LOG2E


def _aug_fn(ins, cs):
    q, k, fcum = ins
    g_all = fcum * _LOG2E
    lane = lax.broadcasted_iota(jnp.int32, (q.shape[0], DH), 1)
    qa, ka = [], []
    for hd in range(NH):
        g = g_all[:, hd:hd + 1]
        hi = g.astype(BF).astype(F32)
        mid = (g - hi).astype(BF).astype(F32)
        lo = ((g - hi) - mid).astype(BF).astype(F32)
        qx = jnp.where(lane == 0, hi, jnp.where(lane == 1, mid, jnp.where(lane == 2, lo,
                                                                          jnp.where(lane < 6, 1.0, 0.0))))
        kx = jnp.where(lane < 3, 1.0, jnp.where(lane == 3, -hi, jnp.where(lane == 4, -mid,
                                                                          jnp.where(lane == 5, -lo, 0.0))))
        qa += [q[:, hd * DH:(hd + 1) * DH], qx.astype(BF)]
        ka += [k[:, hd * DH:(hd + 1) * DH], kx.astype(BF)]
    return [jnp.concatenate(qa, axis=1), jnp.concatenate(ka, axis=1)], []


_KA_ONES = DH + 3
KT_ONES = 16
FWD_KEY_PARTS = 2


def _attn_fwd(qa, ka, qkv, T, blk=512, side=None):
    blk = min(blk, T)
    nb = T // blk
    half = blk // 2

    def body(q_ref, k_ref, vn_ref, o_ref, lse_ref, v_ref):
        i = pl.program_id(1)

        @pl.when(i == 0)
        def _():
            for jj in range(nb):
                v_ref[jj] = jnp.concatenate([vn_ref[jj * blk:(jj + 1) * blk, :].astype(F32).T.astype(BF),
                                             jnp.ones((KT_ONES, blk), BF)], axis=0)

        q = q_ref[...]

        def scores(j):
            r0 = pl.multiple_of(j * blk, blk)
            return _dot(k_ref[pl.ds(r0, blk), :], q, "nt")

        def update(s, vj, carry):
            m, acc = carry
            m_new = jnp.maximum(m, jnp.max(s, axis=0, keepdims=True))
            alpha = jnp.exp2(m - m_new)
            acc = alpha * acc + _dot(vj, jnp.exp2(s - m_new).astype(BF))
            return m_new, acc

        def step(j, st):
            r0 = pl.multiple_of(j * blk, blk)
            part = blk // FWD_KEY_PARTS
            ss = [_dot(k_ref[pl.ds(r0 + n * part, part), :], q, "nt") for n in range(FWD_KEY_PARTS)]
            vj = v_ref[j]
            for n in range(FWD_KEY_PARTS):
                st = update(ss[n], vj[:, n * part:(n + 1) * part], st)
            return st

        init = (jnp.full((1, blk), -jnp.inf, F32), jnp.zeros((DH + KT_ONES, blk), F32))
        carry = lax.fori_loop(0, i, step, init)
        last = scores(i)
        vi = v_ref[i]
        rk = lax.broadcasted_iota(jnp.int32, (half, blk), 0)
        cq = lax.broadcasted_iota(jnp.int32, (half, blk), 1)
        m, acc = update(jnp.where(cq >= rk, last[:half], -jnp.inf), vi[:, :half], carry)
        s2 = jnp.where(cq[:, :half] >= rk[:, :half], last[half:, half:], -jnp.inf)
        m2, acc2 = update(s2, vi[:, half:], (m[:, half:], acc[:, half:]))
        m = jnp.concatenate([m[:, :half], m2], axis=1)
        acc = jnp.concatenate([acc[:, :half], acc2], axis=1)
        l = acc[DH:DH + 1]
        o_ref[...] = (acc[:DH] / l).T.astype(o_ref.dtype)
        lse_ref[...] = m + jnp.log(l) * _LOG2E

    vm = _nbytes((T, DA), BF) + 2 * _nbytes((T, DH), BF) + 6 * _nbytes((blk, blk), F32)
    return _pallas(
        body, (qa, ka, qkv), name="attn_fwd", grid=(NH, nb),
        in_specs=[pl.BlockSpec((blk, DA), lambda h, i: (i, h)),
                  pl.BlockSpec((T, DA), lambda h, i: (0, h)),
                  pl.BlockSpec((T, DH), lambda h, i: (0, 2 * NH + h))],
        out_specs=[pl.BlockSpec((blk, DH), lambda h, i: (i, h)),
                   pl.BlockSpec((None, None, 1, blk), lambda h, i: (h, i, 0, 0))],
        out_shape=[jax.ShapeDtypeStruct((T, D), BF), jax.ShapeDtypeStruct((NH, nb, 1, blk), F32)],
        scratch_shapes=[pltpu.VMEM((nb, DH + KT_ONES, blk), BF)],
        semantics=("parallel", "arbitrary"), vmem=_vmem_limit(vm), side=side)


def _attn_bwd(qa, ka, qkv, do, lrow, drow, T, blk=512, side=None):
    blk = min(blk, T)
    nb = T // blk

    def body(ka_ref, v_ref, qa_ref, do_ref, l_ref, d_ref, dq_ref, dk_ref, dv_ref, dfs_ref, dft_ref, dq_s):
        j = pl.program_id(1)

        @pl.when(j == 0)
        def _():
            dq_s[...] = jnp.zeros_like(dq_s)

        row = lax.broadcasted_iota(jnp.int32, (DH + KT_ONES, blk), 0)
        dq_scale = jnp.where(row < DH, _SCALE, 1.0)

        kaj = ka_ref[...]
        ktj = jnp.concatenate([kaj[:, :DH].astype(F32).T.astype(BF), jnp.ones((KT_ONES, blk), BF)], axis=0)
        vj = v_ref[...]

        def step(i, carry):
            dka, dv = carry
            r0 = pl.multiple_of(i * blk, blk)
            qi = qa_ref[pl.ds(r0, blk), :]
            doi = do_ref[pl.ds(r0, blk), :]
            pt = jnp.exp2(_dot(kaj, qi, "nt") - l_ref[i])
            dv = dv + _dot(pt.astype(BF), doi)
            dpt = _dot(vj, doi, "nt")
            dst = pt * (dpt - d_ref[i])
            dsb = dst.astype(BF)
            dka = dka + _dot(dsb, qi)
            dq_s[i] += _dot(ktj, dsb) * dq_scale
            return dka, dv

        def diagonal():
            half = blk // 2
            r0 = pl.multiple_of(j * blk, blk)
            qi = qa_ref[pl.ds(r0, blk), :]
            doi = do_ref[pl.ds(r0, blk), :]
            lr, dr = l_ref[j], d_ref[j]
            rk = lax.broadcasted_iota(jnp.int32, (half, blk), 0)
            cq = lax.broadcasted_iota(jnp.int32, (half, blk), 1)

            def quarter(ka, v, q, do, l2, d2, keep):
                pt = jnp.exp2(jnp.where(keep, _dot(ka, q, "nt") - l2, -jnp.inf))
                dsb = (pt * (_dot(v, do, "nt") - d2)).astype(BF)
                return _dot(dsb, q), _dot(pt.astype(BF), do), dsb

            dka1, dv1, ds1 = quarter(kaj[:half], vj[:half], qi, doi, lr, dr, cq >= rk)
            dka2, dv2, ds2 = quarter(kaj[half:], vj[half:], qi[half:], doi[half:], lr[:, half:], dr[:, half:],
                                     cq[:, :half] >= rk[:, :half])
            dq2 = jnp.concatenate([jnp.zeros((DH + KT_ONES, half), F32), _dot(ktj[:, half:], ds2)], axis=1)
            dq_s[j] += (_dot(ktj[:, :half], ds1) + dq2) * dq_scale
            return jnp.concatenate([dka1, dka2], axis=0), jnp.concatenate([dv1, dv2], axis=0)

        dka, dv = lax.fori_loop(j + 1, nb, step, diagonal())
        dk_ref[...] = (dka[:, :DH] * (1.0 / _LOG2E)).astype(dk_ref.dtype)
        dv_ref[...] = dv.astype(dv_ref.dtype)
        dfs_ref[...] = dka[:, DH:].T[_KA_ONES - DH:_KA_ONES - DH + 1, :]

        @pl.when(j == nb - 1)
        def _():
            for ii in range(nb):
                t = dq_s[ii]
                dq_ref[ii * blk:(ii + 1) * blk, :] = t[:DH].T.astype(dq_ref.dtype)
                dft_ref[ii] = t[DH:DH + 1]

    rowv = pl.BlockSpec((None, nb, 1, blk), lambda h, j: (h, 0, 0, 0))
    vm = (_nbytes((T, DA), BF) + 2 * _nbytes((T, DH), BF) + _nbytes((T, DH + KT_ONES), F32)
          + 8 * _nbytes((blk, blk), F32))
    return _pallas(
        body, (ka, qkv, qa, do, lrow, drow), name="attn_bwd", grid=(NH, nb),
        in_specs=[pl.BlockSpec((blk, DA), lambda h, j: (j, h)),
                  pl.BlockSpec((blk, DH), lambda h, j: (j, 2 * NH + h)),
                  pl.BlockSpec((T, DA), lambda h, j: (0, h)),
                  pl.BlockSpec((T, DH), lambda h, j: (0, h)),
                  rowv, rowv],
        out_specs=[pl.BlockSpec((T, DH), lambda h, j: (0, h)),
                   pl.BlockSpec((blk, DH), lambda h, j: (j, h)),
                   pl.BlockSpec((blk, DH), lambda h, j: (j, h)),
                   pl.BlockSpec((None, None, 1, blk), lambda h, j: (h, j, 0, 0)), rowv],
        out_shape=[jax.ShapeDtypeStruct((T, D), BF), jax.ShapeDtypeStruct((T, D), BF),
                   jax.ShapeDtypeStruct((T, D), BF), jax.ShapeDtypeStruct((NH, nb, 1, blk), F32),
                   jax.ShapeDtypeStruct((NH, nb, 1, blk), F32)],
        scratch_shapes=[pltpu.VMEM((nb, DH + KT_ONES, blk), F32)],
        semantics=("parallel", "arbitrary"), vmem=_vmem_limit(vm), side=side)


def _norm_fn(ins, cs):
    x, = ins
    g, = cs
    r = lax.rsqrt(jnp.mean(x * x, axis=-1, keepdims=True) + EPS)
    return [x * r * g], []


def _norm_bwd_fn(ins, cs):
    x, dy, dres = ins
    g, = cs
    r = lax.rsqrt(jnp.mean(x * x, axis=-1, keepdims=True) + EPS)
    xh = x * r
    dxh = dy * g
    dx = dres + r * (dxh - xh * jnp.mean(dxh * xh, axis=-1, keepdims=True))
    return [dx], [_colsum(dy * xh)]


def _final_fn(ins, cs):
    x2, tgt = ins
    g, = cs
    r = lax.rsqrt(jnp.mean(x2 * x2, axis=-1, keepdims=True) + EPS)
    xh = x2 * r
    e = xh * g - tgt
    dy = e * (1.0 / D)
    dxh = dy * g
    dx2 = r * (dxh - xh * jnp.mean(dxh * xh, axis=-1, keepdims=True))
    return [dx2], [_colsum(0.5 * e * e * (1.0 / D)), _colsum(dy * xh)]


def _z_fn(ins, cs):
    g, h = [v.astype(F32) for v in ins]
    return [_gelu(g) * h], []


def _mix_fn(ins, cs):
    gates, ya, yb = [v.astype(F32) for v in ins]
    return [_sig(gates[:, :D]) * ya + _sig(gates[:, D:]) * yb], []


def _mix_bwd_fn(ins, cs):
    dmix, gates, ya, yb = [v.astype(F32) for v in ins]
    ga = _sig(gates[:, :D])
    gb = _sig(gates[:, D:])
    dgates = jnp.concatenate([dmix * ya * ga * (1.0 - ga), dmix * yb * gb * (1.0 - gb)], axis=1)
    return [dmix * ga, dmix * gb, dgates], []


def _z_bwd_fn(ins, cs):
    dz, g, h = [v.astype(F32) for v in ins]
    return [dz * _gelu(g), dz * h * _gelu_grad(g)], []


def _delta_fn(ins, cs):
    do, o = ins
    p = do.astype(F32) * o.astype(F32)
    lane = lax.broadcasted_iota(jnp.int32, (p.shape[0], DH), 1)
    out = jnp.zeros((p.shape[0], DH), F32)
    for hd in range(NH):
        s = jnp.sum(p[:, hd * DH:(hd + 1) * DH], axis=1, keepdims=True)
        out = jnp.where(lane == hd, s, out)
    return [out], []


def _du_all(pieces, win, T, tm=256, side=None):
    tm = min(tm, T)
    n = len(pieces)

    def body(*refs):
        w_ref, o_ref = refs[n], refs[n + 1]
        acc = None
        for (a, off), a_ref in zip(pieces, refs[:n]):
            d = _dot(a_ref[...].astype(BF), w_ref[:, off:off + a.shape[1]], "nt")
            acc = d if acc is None else acc + d
        o_ref[...] = acc

    vm = (sum(_nbytes((tm, a.shape[1]), a.dtype) for a, _ in pieces) + _nbytes(win.shape, win.dtype)
          + 2 * _nbytes((tm, D), F32))
    return _pallas(
        body, tuple(a for a, _ in pieces) + (win,), name="du_all", grid=(T // tm,),
        in_specs=[pl.BlockSpec((tm, a.shape[1]), lambda i: (i, 0)) for a, _ in pieces]
        + [pl.BlockSpec(win.shape, lambda i: (0, 0))],
        out_specs=[pl.BlockSpec((tm, D), lambda i: (i, 0))],
        out_shape=[jax.ShapeDtypeStruct((T, D), F32)],
        semantics=("arbitrary",), vmem=int(min(VMEM_CAP, 2 * vm + (4 << 20))), side=side)


def _local_step(x, tgt, w, T, blk=1024, dist=None):
    blk = min(blk, T)
    nb = T // blk
    win = w["win"]

    u, = _ew(_norm_fn, T, 1024, [(x, D, 0)], [w["g_mix"]], [(D, BF)], [], name="norm_mix")
    xg = _mm(u, win, "nn", T, 2 * D, D, name="proj_lru", out_dtype=BF)
    qkv = _mm(u, win, "nn", T, 3 * D, D, name="proj_qkv", out_dtype=BF, b_off=(0, 2),
              epi=lambda acc: acc * jnp.where(pl.program_id(1) == 0, _C2, 1.0))
    gates = _mm(u, win, "nn", T, 2 * D, D, name="proj_gates", b_off=(0, 5), out_dtype=BF)
    fl = _mm(u, win, "nn", T, DH, D, name="proj_f", tn=DH, b_off=(0, 7 * D // DH))
    fcum = _fgate_fwd(fl, w["fb"], T)
    qa, ka = _ew(_aug_fn, T, 512, [(qkv, D, 0), (qkv, D, 1), (fcum, DH, 0)], [],
                 [(NH * DA, BF), (NH * DA, BF)], [], name="attn_augment")

    h = _lru_fwd(xg, w["cw"], w["vec"], w["wabd"], w["wxbd"], T)
    ob, lse, *landed = _attn_fwd(qa, ka, qkv, T, blk, side=dist.weights_side() if dist else None)
    if dist:
        w = dict(w, **dist.weights_landed(landed))
    z, = _ew(_z_fn, T, 1024, [(xg, D, 1), (h, D, 0)], [], [(D, BF)], [], name="lru_gelu")
    ya = _mm(z, w["wa"], "nn", T, D, D, name="branch_a", out_dtype=BF)
    yb = _mm(ob, w["wb"], "nn", T, D, D, name="branch_b", out_dtype=BF)
    mix, = _ew(_mix_fn, T, 512, [(gates, 2 * D, 0), (ya, D, 0), (yb, D, 0)], [], [(D, BF)], [], name="mix")
    x1 = _mm(mix, w["wout"], "nn", T, D, D, name="out_proj", add=x)
    m, = _ew(_norm_fn, T, 1024, [(x1, D, 0)], [w["g_mlp"]], [(D, BF)], [], name="norm_mlp")
    hh = _mm(m, w["wup"], "nn", T, FF, D, name="mlp_up", out_dtype=BF,
             epi=lambda acc: jnp.square(jnp.maximum(acc, 0.0)))
    x2 = _mm(hh, w["wdown"], "nn", T, D, FF, name="mlp_down", add=x1, tk=2048)
    dx2, loss_vec, dg_fin = _ew(_final_fn, T, 512, [(x2, D, 0), (tgt, D, 0)], [w["g_fin"]], [(D, F32)],
                                [(1, D), (1, D)], name="final_norm_loss")

    dhpre = _mm(dx2, w["wdown"], "nt", T, FF, D, name="mlp_down_bwd", out_dtype=BF,
                epi=lambda acc, h2: acc * (2.0 * jnp.sqrt(h2.astype(F32))), epi_ins=[hh])
    dwdown = _mm(hh, dx2, "tn", FF, D, T, name="dw_down", out_dtype=BF)
    dwup = _mm(m, dhpre, "tn", D, FF, T, name="dw_up", out_dtype=BF)
    dm = _mm(dhpre, w["wup"], "nt", T, D, FF, name="mlp_up_bwd", tk=2048)
    dx1, dg_mlp = _ew(_norm_bwd_fn, T, 512, [(x1, D, 0), (dm, D, 0), (dx2, D, 0)], [w["g_mlp"]], [(D, F32)],
                      [(1, D)], name="norm_mlp_bwd")

    dmix = _mm(dx1, w["wout"], "nt", T, D, D, name="out_proj_bwd", out_dtype=BF)
    dwout = _mm(mix, dx1, "tn", D, D, T, name="dw_out", out_dtype=BF)
    dya, dyb, dgates = _ew(_mix_bwd_fn, T, 512, [(dmix, D, 0), (gates, 2 * D, 0), (ya, D, 0), (yb, D, 0)], [],
                           [(D, BF), (D, BF), (2 * D, BF)], [], name="mix_bwd")
    dob = _mm(dyb, w["wb"], "nt", T, D, D, name="branch_b_bwd", out_dtype=BF)
    dwb = _mm(ob, dyb, "tn", D, D, T, name="dw_b", out_dtype=BF)
    dz = _mm(dya, w["wa"], "nt", T, D, D, name="branch_a_bwd", out_dtype=BF)
    dwa = _mm(z, dya, "tn", D, D, T, name="dw_a", out_dtype=BF)
    dha, dglru = _ew(_z_bwd_fn, T, 512, [(dz, D, 0), (xg, D, 1), (h, D, 0)], [], [(D, F32), (D, BF)], [],
                     name="lru_gelu_bwd")

    delta, = _ew(_delta_fn, T, 1024, [(dob, D, 0), (ob, D, 0)], [], [(DH, F32)], [], name="attn_delta")
    drow = delta[:, :NH].T.reshape(NH, nb, 1, blk)
    big = dict(w_branch_a=dwa, w_branch_b=dwb, w_out=dwout, w_up=dwup, w_down=dwdown)
    side = dist.grads_side(big) if dist else None
    dq, dk, dv, dfs, dft, *landed = _attn_bwd(qa, ka, qkv, dob, lse, drow, T, blk, side=side)
    if dist:
        big = dist.grads_landed(side, landed)
    dfcum = jnp.pad((dft - dfs).reshape(NH, T).T, ((0, 0), (0, DH - NH)))
    dfl, dfb = _fgate_bwd(dfcum, fl, w["fb"], T)

    dxl, dwabd, dwxbd, lacc = _lru_bwd(xg, h, dha, w["cw"], w["vec"], w["wabd"], w["wxbd"], T)

    dproj = ((dxl, 0), (dglru, D), (dq, 2 * D), (dk, 3 * D), (dv, 4 * D), (dgates, 5 * D), (dfl, 7 * D))
    pieces = [_mm(u, p, "tn", D, p.shape[1], T, name="dw_in_%d" % n, out_dtype=BF)
              for n, (p, _) in enumerate(dproj)]
    pieces[-1] = pieces[-1][:, :NH]
    dwin = dict(w_in=jnp.concatenate(pieces, axis=1))
    side = dist.grads_side(dwin) if dist else None
    du, *landed = _du_all(dproj, win, T, side=side)
    big.update(dist.grads_landed(side, landed) if dist else dwin)
    dx, dg_mix = _ew(_norm_bwd_fn, T, 512, [(x, D, 0), (du, D, 0), (dx1, D, 0)], [w["g_mix"]], [(D, F32)],
                     [(1, D)], name="norm_mix_bwd")

    return dict(dx=dx, big=big, dwabd=dwabd, dwxbd=dwxbd, lacc=lacc, dfb=dfb, dg_mix=dg_mix, dg_mlp=dg_mlp,
                dg_fin=dg_fin, loss_vec=loss_vec)


def _block_diag(w):
    per = BD // LRU_BW
    w4 = w.reshape(NBD, per, LRU_BW, LRU_BW)
    out = jnp.zeros((NBD, per, LRU_BW, per, LRU_BW), w.dtype)
    for b in range(per):
        out = out.at[:, b, :, b, :].set(w4[:, b])
    return out.reshape(NBD, BD, BD)


def _block_diag_extract(wbd):
    per = BD // LRU_BW
    w5 = wbd.reshape(NBD, per, LRU_BW, per, LRU_BW)
    return jnp.stack([w5[:, b, :, b, :] for b in range(per)], axis=1).reshape(LRU_BLOCKS, LRU_BW, LRU_BW)


_ANY = pl.BlockSpec(memory_space=pl.ANY)


def _place():
    x, y, c = lax.axis_index("x"), lax.axis_index("y"), lax.axis_index("c")
    chips = [(1 - x, y), (x, 1 - y), (1 - x, 1 - y)]
    return x, y, c, chips


def _allgather_shards(shards):
    n = len(shards)

    def body(*refs):
        ins, outs = refs[:n], refs[n:2 * n]
        send_sems, recv_sems = refs[2 * n:]
        x, y, c, chips = _place()
        me = 2 * x + y
        sibling = (x, y, 1 - c)

        def remote(p, k, src, dst, to):
            return pltpu.make_async_remote_copy(src_ref=src, dst_ref=dst, send_sem=send_sems.at[p, k],
                                                recv_sem=recv_sems.at[p, k], device_id=to, device_id_type=MESH)

        sent = []
        for p in range(n):
            for k, chip in enumerate(chips):
                cp = remote(p, k, ins[p].at[c], outs[p].at[me, c], (chip[0], chip[1], c))
                cp.start()
                sent.append(cp)
        for p in range(n):
            for k, chip in enumerate(chips):
                half = outs[p].at[2 * chip[0] + chip[1], c]
                remote(p, k, half, half, sibling).wait_recv()
                fwd = remote(p, 3 + k, half, half, sibling)
                fwd.start()
                sent.append(fwd)
        for p in range(n):
            for k, chip in enumerate(chips):
                half = outs[p].at[2 * chip[0] + chip[1], 1 - c]
                remote(p, 3 + k, half, half, sibling).wait_recv()
        for cp in sent:
            cp.wait_send()

    gathered = pl.pallas_call(
        body, name="allgather_weights",
        in_specs=[_ANY] * n, out_specs=[_ANY] * n,
        out_shape=[jax.ShapeDtypeStruct((NCHIP,) + s.shape, s.dtype) for s in shards],
        scratch_shapes=[pltpu.SemaphoreType.DMA((n, 6)), pltpu.SemaphoreType.DMA((n, 6))],
    )(*shards)
    me = 2 * lax.axis_index("x") + lax.axis_index("y")
    return [lax.dynamic_update_index_in_dim(g, s, me, 0) for g, s in zip(gathered, shards)]


_LATE =["w_branch_a", "w_branch_b", "w_out", "w_up", "w_down"]
_COLUMN_CUT = ("w_in", "w_up")
N_PEERS = 7


def _shard_major(name, g):
    s = _columns_to_shards(g) if name in _COLUMN_CUT else g.reshape(NCHIP, g.shape[0] // NCHIP, g.shape[1])
    return s.reshape(NCHIP, 2, s.shape[1] // 2, s.shape[2])


class _Exchanges:
    def __init__(self, shards):
        self.shards = shards

    def weights_side(self):
        srcs = [self.shards[n] for n in _LATE]

        def copies(sin, sout, send, recv):
            x, y, c, chips = _place()
            return [pltpu.make_async_remote_copy(
                src_ref=sin[p], dst_ref=sout[p].at[2 * x + y], send_sem=send.at[3 * p + k], recv_sem=recv.at[3 * p + k],
                device_id=(chip[0], chip[1], c), device_id_type=MESH)
                for p in range(len(sin)) for k, chip in enumerate(chips)]

        return _Side(srcs, [jax.ShapeDtypeStruct((NCHIP,) + s.shape, s.dtype) for s in srcs], 3 * len(srcs), copies)

    def weights_landed(self, landed):
        me = 2 * lax.axis_index("x") + lax.axis_index("y")
        full = {n: lax.dynamic_update_index_in_dim(g, self.shards[n], me, 0) for n, g in zip(_LATE, landed)}
        return dict(wa=full["w_branch_a"].reshape(D, D), wb=full["w_branch_b"].reshape(D, D),
                    wout=full["w_out"].reshape(D, D), wup=_shards_to_columns(full["w_up"]),
                    wdown=full["w_down"].reshape(FF, D))

    def grads_side(self, grads):
        side_names = list(grads)
        srcs = [_shard_major(n, grads[n]) for n in side_names]

        def copies(sin, sout, send, recv):
            x, y, c, chips = _place()
            peers = [(x, y, 1 - c)] + [(cx, cy, c) for cx, cy in chips] + [(cx, cy, 1 - c) for cx, cy in chips]
            return [pltpu.make_async_remote_copy(
                src_ref=sin[p].at[2 * px + py, pc], dst_ref=sout[p].at[s], send_sem=send.at[N_PEERS * p + s],
                recv_sem=recv.at[N_PEERS * p + s], device_id=(px, py, pc), device_id_type=MESH)
                for p in range(len(sin)) for s, (px, py, pc) in enumerate(peers)]

        side = _Side(srcs, [jax.ShapeDtypeStruct((N_PEERS,) + s.shape[2:], s.dtype) for s in srcs],
                     N_PEERS * len(srcs), copies)
        side.names = side_names
        return side

    def grads_landed(self, side, landed):
        return {n: (own, got) for n, own, got in zip(side.names, side.srcs, landed)}


def _add8(g, got, me, c, name):
    _, _, half, cols = g.shape
    th = _row_tile(half, 2 * cols)

    def body(me_ref, c_ref, g_ref, r_ref, o_ref):
        acc = g_ref[...].astype(F32)
        for s in range(N_PEERS):
            acc = acc + r_ref[s].astype(F32)
        o_ref[...] = acc

    return pl.pallas_call(
        body, name=name,
        grid_spec=pltpu.PrefetchScalarGridSpec(
            num_scalar_prefetch=2, grid=(half // th,),
            in_specs=[pl.BlockSpec((None, None, th, cols), lambda i, me_ref, c_ref: (me_ref[0], c_ref[0], i, 0)),
                      pl.BlockSpec((N_PEERS, th, cols), lambda i, me_ref, c_ref: (0, i, 0))],
            out_specs=pl.BlockSpec((th, cols), lambda i, me_ref, c_ref: (i, 0))),
        out_shape=jax.ShapeDtypeStruct((half, cols), F32),
    )(me, c, g, got)


def _share_halves(halves):
    n = len(halves)

    def body(*refs):
        ins, outs = refs[:n], refs[n:2 * n]
        send_sems, recv_sems = refs[2 * n:]
        x, y, c, _ = _place()
        sibling = (x, y, 1 - c)
        copies = []
        for p in range(n):
            cp = pltpu.make_async_remote_copy(src_ref=ins[p], dst_ref=outs[p], send_sem=send_sems.at[p],
                                              recv_sem=recv_sems.at[p], device_id=sibling, device_id_type=MESH)
            cp.start()
            copies.append(cp)
        for cp in copies:
            cp.wait()

    return pl.pallas_call(
        body, name="reduce_share_halves",
        in_specs=[_ANY] * n, out_specs=[_ANY] * n,
        out_shape=[jax.ShapeDtypeStruct(h.shape, h.dtype) for h in halves],
        scratch_shapes=[pltpu.SemaphoreType.DMA((n,)), pltpu.SemaphoreType.DMA((n,))],
    )(*halves)


def _row_tile(half, cols):
    th = max(SLAB, min(half, (1 << 18) // cols // SLAB * SLAB))
    while half % th:
        th -= SLAB
    return th


N_DEV = 8
SMALL_ROWS = 208


def _allreduce_small(pack):
    def body(x_ref, out_ref, gbuf, send_sems, recv_sems, local_sem):
        x, y, c, chips = _place()
        me, sibling = (x, y, c), (x, y, 1 - c)

        def rows(px, py, pc):
            return gbuf.at[4 * px + 2 * py + pc]

        def copy(k, block, to, src=None):
            return pltpu.make_async_remote_copy(
                src_ref=rows(*block) if src is None else src, dst_ref=rows(*block),
                send_sem=send_sems.at[k], recv_sem=recv_sems.at[k], device_id=to, device_id_type=MESH)

        mine = pltpu.make_async_copy(x_ref, rows(*me), local_sem)
        mine.start()
        first = [copy(0, me, sibling, src=x_ref)]
        first += [copy(1 + j, me, (chip[0], chip[1], c), src=x_ref) for j, chip in enumerate(chips)]
        for cp in first:
            cp.start()
        passed = [copy(4 + j, (chip[0], chip[1], c), sibling) for j, chip in enumerate(chips)]
        for j, chip in enumerate(chips):
            copy(1 + j, (chip[0], chip[1], c), me).wait_recv()
            passed[j].start()
        copy(0, sibling, me).wait_recv()
        for j, chip in enumerate(chips):
            copy(4 + j, (chip[0], chip[1], 1 - c), me).wait_recv()
        for cp in first + passed:
            cp.wait_send()
        mine.wait()
        acc = gbuf[0]
        for d in range(1, N_DEV):
            acc = acc + gbuf[d]
        out_ref[...] = acc

    return pl.pallas_call(
        body, name="allreduce_small",
        in_specs=[pl.BlockSpec(memory_space=pltpu.VMEM)],
        out_specs=pl.BlockSpec(memory_space=pltpu.VMEM),
        out_shape=jax.ShapeDtypeStruct((SMALL_ROWS, D), F32),
        scratch_shapes=[pltpu.VMEM((N_DEV, SMALL_ROWS, D), F32), pltpu.SemaphoreType.DMA((7,)),
                        pltpu.SemaphoreType.DMA((7,)), pltpu.SemaphoreType.DMA],
    )(pack)


def _adamw(w, g, m, v, name):
    rows, cols = w.shape

    def body(w_ref, g_ref, m_ref, v_ref, d_ref, mo_ref, vo_ref):
        gv = g_ref[...]
        mn = ADAM_B1 * m_ref[...] + (1.0 - ADAM_B1) * gv
        vn = ADAM_B2 * v_ref[...] + (1.0 - ADAM_B2) * (gv * gv)
        m_hat = mn / (1.0 - ADAM_B1 ** ADAM_STEP)
        v_hat = vn / (1.0 - ADAM_B2 ** ADAM_STEP)
        d_ref[...] = -ADAM_LR * (m_hat / (jnp.sqrt(v_hat) + ADAM_EPS) + ADAM_WD * w_ref[...])
        mo_ref[...] = mn
        vo_ref[...] = vn

    if rows % SLAB:
        spec, steps = pl.BlockSpec((rows, DH), lambda i: (0, i)), cols // DH
    else:
        th = _row_tile(rows, cols)
        spec, steps = pl.BlockSpec((th, cols), lambda i: (i, 0)), rows // th
    return pl.pallas_call(
        body, name=name, grid=(steps,),
        in_specs=[spec] * 4, out_specs=[spec] * 3,
        out_shape=[jax.ShapeDtypeStruct((rows, cols), F32)] * 3,
        compiler_params=pltpu.CompilerParams(dimension_semantics=("parallel",)),
    )(w, g, m, v)


_SMALL = ["norm_mix_g", "norm_mlp_g", "norm_final_g", "conv_b", "lru_ba", "lru_bx", "lru_lambda"]
_ROW_FB, _ROW_CW, _ROW_WA, _ROW_WX, _ROW_LOSS = 56, 64, 72, 136, 200


def _pack_small(vals, col0):
    def slab(a):
        return jnp.pad(a, ((0, -a.shape[0] % SLAB), (0, D - a.shape[1])))

    rows = [slab(vals[n].reshape(1, D)) for n in _SMALL]
    rows.append(slab(vals["forget_b"].reshape(1, NH)))
    if vals["conv_w"].shape[1] == D:
        rows.append(slab(vals["conv_w"]))
    else:
        rows.append(slab(lax.dynamic_update_slice(jnp.zeros((CONV, D), F32), vals["conv_w"], (0, col0))))
    rows.append(vals["lru_wa"].reshape(LRU_BLOCKS * LRU_BW * LRU_BW // D, D))
    rows.append(vals["lru_wx"].reshape(LRU_BLOCKS * LRU_BW * LRU_BW // D, D))
    rows.append(slab(vals["loss"]) if "loss" in vals else jnp.zeros((SLAB, D), F32))
    return jnp.concatenate(rows, axis=0)


def _unpack_small(pack, col0):
    out = {n: pack[SLAB * i] for i, n in enumerate(_SMALL)}
    out["forget_b"] = pack[_ROW_FB, :NH]
    out["conv_w"] = lax.dynamic_slice(pack[_ROW_CW:_ROW_CW + CONV], (0, col0), (CONV, D // NCHIP))
    out["lru_wa"] = pack[_ROW_WA:_ROW_WX].reshape(LRU_BLOCKS, LRU_BW, LRU_BW)
    out["lru_wx"] = pack[_ROW_WX:_ROW_LOSS].reshape(LRU_BLOCKS, LRU_BW, LRU_BW)
    return out


_WEIGHTS = ["norm_mix_g", "w_in", "conv_w", "conv_b", "lru_wa", "lru_ba", "lru_wx", "lru_bx", "lru_lambda",
            "forget_b", "w_branch_a", "w_branch_b", "w_out", "norm_mlp_g", "w_up", "w_down", "norm_final_g"]
_BIG = ["w_in", "w_branch_a", "w_branch_b", "w_out", "w_up", "w_down"]


def _halves(a):
    return a.reshape(2, a.shape[0] // 2, a.shape[1])


def _columns_to_shards(a):
    rows, cols = a.shape[0], a.shape[1] // NCHIP
    return jnp.transpose(a.reshape(rows, NCHIP, cols), (1, 0, 2))


def _shards_to_columns(a):
    n, rows, cols = a.shape
    return jnp.transpose(a, (1, 0, 2)).reshape(rows, n * cols)


def kernel(x, norm_mix_g, w_in, conv_w, conv_b, lru_wa, lru_ba, lru_wx, lru_bx, lru_lambda, forget_b, w_branch_a, w_branch_b, w_out, norm_mlp_g, w_up, w_down, norm_final_g, loss_target, m_norm_mix_g, m_w_in, m_conv_w, m_conv_b, m_lru_wa, m_lru_ba, m_lru_wx, m_lru_bx, m_lru_lambda, m_forget_b, m_w_branch_a, m_w_branch_b, m_w_out, m_norm_mlp_g, m_w_up, m_w_down, m_norm_final_g, v_norm_mix_g, v_w_in, v_conv_w, v_conv_b, v_lru_wa, v_lru_ba, v_lru_wx, v_lru_bx, v_lru_lambda, v_forget_b, v_w_branch_a, v_w_branch_b, v_w_out, v_norm_mlp_g, v_w_up, v_w_down, v_norm_final_g):
    args = dict(locals())
    wts = {n: args[n] for n in _WEIGHTS}
    mom = {n: args["m_" + n] for n in _WEIGHTS}
    var = {n: args["v_" + n] for n in _WEIGHTS}
    T = x.shape[1]
    xi, yi, ci = lax.axis_index("x"), lax.axis_index("y"), lax.axis_index("c")
    me = 2 * xi + yi
    c1 = jnp.reshape(ci, (1,)).astype(jnp.int32)
    me1 = jnp.reshape(me, (1,)).astype(jnp.int32)
    col0 = me * (D // NCHIP)

    cw_pad = jnp.pad(conv_w, ((0, 4 * SLAB - CONV), (0, 0)))
    g_in, g_cw = _allgather_shards([_halves(w_in.astype(BF)), _halves(cw_pad)])
    cin = DIN // NCHIP
    win = _shards_to_columns(g_in.reshape(NCHIP, D, cin))
    w = dict(
        win=jnp.pad(win, ((0, 0), (0, DINP - DIN))),
        cw=_shards_to_columns(g_cw.reshape(NCHIP, 4 * SLAB, D // NCHIP)[:, :CONV]),
        vec=jnp.concatenate([conv_b[None], lru_ba[None], lru_bx[None], lru_lambda[None],
                             jnp.zeros((SLAB - 4, D), F32)], axis=0),
        fb=jnp.pad(forget_b[None], ((0, 0), (0, DH - NH))),
        wabd=_block_diag(lru_wa).astype(BF), wxbd=_block_diag(lru_wx).astype(BF),
        g_mix=norm_mix_g[None], g_mlp=norm_mlp_g[None], g_fin=norm_final_g[None])

    r = _local_step(x[0], loss_target[0], w, T, dist=_Exchanges({n: wts[n].astype(BF) for n in _LATE}))

    halves = [_add8(*r["big"][n], me1, c1, "add8_" + n) for n in _BIG]
    theirs = _share_halves(halves)
    low = ci == 0
    gsum = {n: jnp.concatenate([jnp.where(low, h, t), jnp.where(low, t, h)], axis=0)
            for n, h, t in zip(_BIG, halves, theirs)}
    lacc = r["lacc"]
    small = dict(norm_mix_g=r["dg_mix"], norm_mlp_g=r["dg_mlp"], norm_final_g=r["dg_fin"], conv_b=lacc[3],
                 lru_ba=lacc[0], lru_bx=lacc[1], lru_lambda=lacc[2], forget_b=r["dfb"][0, :NH],
                 conv_w=lacc[4:4 + CONV], lru_wa=_block_diag_extract(r["dwabd"]),
                 lru_wx=_block_diag_extract(r["dwxbd"]), loss=r["loss_vec"])
    gpack = _allreduce_small(_pack_small(small, col0))
    loss = jnp.sum(gpack[_ROW_LOSS])

    grads, delta, new_m, new_v = {}, {}, {}, {}
    for n in _BIG:
        if n == "w_in":
            gt = gsum[n].T
            grads[n] = gt.T
            delta[n], new_m[n], new_v[n] = [a.T for a in _adamw(wts[n].T, gt, mom[n].T, var[n].T, "adamw_" + n)]
        else:
            grads[n] = gsum[n]
            delta[n], new_m[n], new_v[n] = _adamw(wts[n], gsum[n], mom[n], var[n], "adamw_" + n)
    dp, mp, vp = _adamw(_pack_small(wts, col0), gpack, _pack_small(mom, col0), _pack_small(var, col0), "adamw_small")
    for dst, pack in ((grads, gpack), (delta, dp), (new_m, mp), (new_v, vp)):
        dst.update(_unpack_small(pack, col0))
    return (loss, r["dx"][None], *[grads[n] for n in _WEIGHTS], *[delta[n] for n in _WEIGHTS],
            *[new_m[n] for n in _WEIGHTS], *[new_v[n] for n in _WEIGHTS])
```

```python
import functools
import math

import jax
import jax.numpy as jnp
import numpy as np
from jax import lax
from jax.experimental import pallas as pl
from jax.experimental.pallas import tpu as pltpu

F32 = jnp.float32
BF = jnp.bfloat16

D = 1024
NH = 8
DH = 128
FF = 4096
CONV = 4
LRU_BLOCKS = 16
LRU_BW = 64
BD = 256
NBD = D // BD
LRU_C = 8.0
EPS = 1e-6
DIN = 7176
DINP = 7296
NCHIP = 4
SLAB = 8
VMEM_CAP = 60 * 1024 * 1024

ADAM_LR = 0.001
ADAM_B1 = 0.9
ADAM_B2 = 0.999
ADAM_EPS = 1e-08
ADAM_WD = 0.01
ADAM_STEP = 10

MESH = pl.DeviceIdType.MESH


def _vmem_limit(nbytes):
    return int(min(VMEM_CAP, max(32 * 1024 * 1024, 3 * nbytes)))


def _nbytes(shape, dtype):
    return int(np.prod(shape)) * jnp.dtype(dtype).itemsize


def _sig(x):
    return 0.5 * jnp.tanh(0.5 * x) + 0.5


def _log1p(u):
    w = 1.0 + u
    return jnp.where(w == 1.0, u, jnp.log(w) * (u / (w - 1.0)))


def _one_minus_sq(a, la):
    return jnp.tanh(-la) * (1.0 + a * a)


def _softplus(z):
    return jnp.maximum(z, 0.0) + _log1p(jnp.exp(-jnp.abs(z)))


_GELU_C = math.sqrt(2.0 / math.pi)


def _gelu(x):
    return 0.5 * x * (1.0 + jnp.tanh(_GELU_C * (x + 0.044715 * x * x * x)))


def _gelu_grad(x):
    t = jnp.tanh(_GELU_C * (x + 0.044715 * x * x * x))
    return 0.5 * (1.0 + t) + 0.5 * x * (1.0 - t * t) * _GELU_C * (1.0 + 3.0 * 0.044715 * x * x)


def _slab_scan_fwd(a, b):
    row = lax.broadcasted_iota(jnp.int32, a.shape, 0)
    for k in (1, 2, 4):
        a_s = pltpu.roll(a, k, 0)
        b_s = pltpu.roll(b, k, 0)
        m = row >= k
        b = jnp.where(m, a * b_s + b, b)
        a = jnp.where(m, a * a_s, a)
    return a, b


def _slab_scan_bwd(a, b):
    row = lax.broadcasted_iota(jnp.int32, a.shape, 0)
    for k in (1, 2, 4):
        a_s = pltpu.roll(a, SLAB - k, 0)
        b_s = pltpu.roll(b, SLAB - k, 0)
        m = row < SLAB - k
        b = jnp.where(m, a * b_s + b, b)
        a = jnp.where(m, a * a_s, a)
    return a, b


_DIMS = {"nn": (((1,), (0,)), ((), ())), "nt": (((1,), (1,)), ((), ())), "tn": (((0,), (0,)), ((), ()))}


def _dot(a, b, mode="nn"):
    return lax.dot_general(a, b, _DIMS[mode], preferred_element_type=F32)


def _mm(a, b, mode, M, N, K, *, name, out_dtype=F32, tm=None, tn=1024, tk=1024,
        a_off=(0, 0), b_off=(0, 0), add=None, epi=None, epi_ins=()):
    if tm is None:
        tm = 2048 if (K <= tk and out_dtype == BF and add is None) else 1024
    tm, tn, tk = min(tm, M), min(tn, N), min(tk, K)
    nk = K // tk
    grid = (M // tm, N // tn, nk)
    if mode == "nn":
        a_spec = pl.BlockSpec((tm, tk), lambda i, j, k: (i + a_off[0], k + a_off[1]))
        b_spec = pl.BlockSpec((tk, tn), lambda i, j, k: (k + b_off[0], j + b_off[1]))
    elif mode == "nt":
        a_spec = pl.BlockSpec((tm, tk), lambda i, j, k: (i + a_off[0], k + a_off[1]))
        b_spec = pl.BlockSpec((tn, tk), lambda i, j, k: (j + b_off[0], k + b_off[1]))
    else:
        a_spec = pl.BlockSpec((tk, tm), lambda i, j, k: (k + a_off[0], i + a_off[1]))
        b_spec = pl.BlockSpec((tk, tn), lambda i, j, k: (k + b_off[0], j + b_off[1]))
    o_spec = pl.BlockSpec((tm, tn), lambda i, j, k: (i, j))
    extra = ([add] if add is not None else []) + list(epi_ins)
    n_extra = len(extra)
    has_add = add is not None

    def body(*refs):
        a_ref, b_ref = refs[0], refs[1]
        ex = refs[2:2 + n_extra]
        o_ref = refs[2 + n_extra]

        def finish(acc):
            if has_add:
                acc = acc + ex[0][...].astype(F32)
            if epi is not None:
                acc = epi(acc, *[e[...] for e in ex[(1 if has_add else 0):]])
            o_ref[...] = acc.astype(o_ref.dtype)

        p = _dot(a_ref[...].astype(BF), b_ref[...].astype(BF), mode)
        if nk == 1:
            finish(p)
        else:
            acc_ref = refs[3 + n_extra]
            k = pl.program_id(2)

            @pl.when(k == 0)
            def _():
                acc_ref[...] = p

            @pl.when(k > 0)
            def _():
                acc_ref[...] += p

            @pl.when(k == nk - 1)
            def _():
                finish(acc_ref[...])

    blk = (_nbytes((tm, tk), a.dtype) + _nbytes((tk, tn), b.dtype) + _nbytes((tm, tn), out_dtype)
           + sum(_nbytes((tm, tn), e.dtype) for e in extra) + 2 * _nbytes((tm, tn), F32))
    return pl.pallas_call(
        body, name=name, grid=grid,
        in_specs=[a_spec, b_spec] + [o_spec] * n_extra,
        out_specs=o_spec,
        out_shape=jax.ShapeDtypeStruct((M, N), out_dtype),
        scratch_shapes=[pltpu.VMEM((tm, tn), F32)] if nk > 1 else [],
        compiler_params=pltpu.CompilerParams(
            dimension_semantics=("parallel", "parallel", "arbitrary"), vmem_limit_bytes=_vmem_limit(blk)),
    )(a, b, *extra)


def _ew(fn, T, tm, ins, consts, outs, accs, *, name, reverse=False):
    tm = min(tm, T)
    nt = T // tm
    n_in, n_c, n_o, n_a = len(ins), len(consts), len(outs), len(accs)

    def row(i):
        return nt - 1 - i if reverse else i

    in_specs = [pl.BlockSpec((tm, w), functools.partial(lambda i, cb: (row(i), cb), cb=cb)) for (_, w, cb) in ins]
    in_specs += [pl.BlockSpec(c.shape, functools.partial(lambda i, nd: (0,) * nd, nd=c.ndim)) for c in consts]
    out_specs = [pl.BlockSpec((tm, w), lambda i: (row(i), 0)) for (w, _) in outs]
    out_specs += [pl.BlockSpec((r, w), lambda i: (0, 0)) for (r, w) in accs]
    out_shape = [jax.ShapeDtypeStruct((T, w), dt) for (w, dt) in outs]
    out_shape += [jax.ShapeDtypeStruct((r, w), F32) for (r, w) in accs]

    def body(*refs):
        in_refs = refs[:n_in]
        c_refs = refs[n_in:n_in + n_c]
        o_refs = refs[n_in + n_c:n_in + n_c + n_o]
        a_refs = refs[n_in + n_c + n_o:]
        ov, av = fn([r[...] for r in in_refs], [r[...] for r in c_refs])
        for r, v in zip(o_refs, ov):
            r[...] = v.astype(r.dtype)
        if n_a:
            i = pl.program_id(0)

            @pl.when(i == 0)
            def _():
                for r, v in zip(a_refs, av):
                    r[...] = v

            @pl.when(i > 0)
            def _():
                for r, v in zip(a_refs, av):
                    r[...] += v

    blk = (sum(_nbytes((tm, w), a.dtype) for (a, w, _) in ins) + sum(_nbytes(c.shape, c.dtype) for c in consts)
           + sum(_nbytes((tm, w), dt) for (w, dt) in outs) + sum(_nbytes(s, F32) for s in accs))
    res = pl.pallas_call(
        body, name=name, grid=(nt,), in_specs=in_specs, out_specs=out_specs, out_shape=out_shape,
        compiler_params=pltpu.CompilerParams(
            dimension_semantics=("arbitrary",), vmem_limit_bytes=_vmem_limit(blk)),
    )(*[a for (a, _, _) in ins], *consts)
    return res


def _colsum(v):
    return jnp.sum(v, axis=0, keepdims=True)


FGATE_GROUP = 4


def _fgate_fwd(fl, fb, T, tm=512):
    tm = min(tm, T)

    def body(fl_ref, fb_ref, f_ref, carry_ref):
        i = pl.program_id(0)

        @pl.when(i == 0)
        def _():
            carry_ref[...] = jnp.zeros_like(carry_ref)

        rows = FGATE_GROUP * SLAB
        sub = lax.broadcasted_iota(jnp.int32, (rows, DH), 0) % SLAB

        def group(s, carry):
            r0 = pl.multiple_of(s * rows, rows)
            z = fl_ref[pl.ds(r0, rows), :] + fb_ref[...]
            c = jnp.minimum(z, 0.0) - _log1p(jnp.exp(-jnp.abs(z)))
            for k in (1, 2, 4):
                c = c + jnp.where(sub >= k, pltpu.roll(c, k, 0), 0.0)
            for u in range(FGATE_GROUP):
                cu = c[u * SLAB:(u + 1) * SLAB] + carry
                f_ref[pl.ds(r0 + u * SLAB, SLAB), :] = cu
                carry = cu[SLAB - 1:SLAB, :]
            return carry

        carry_ref[0:1, :] = lax.fori_loop(0, tm // rows, group, carry_ref[0:1, :])

    return pl.pallas_call(
        body, name="fgate_fwd", grid=(T // tm,),
        in_specs=[pl.BlockSpec((tm, DH), lambda i: (i, 0)), pl.BlockSpec((1, DH), lambda i: (0, 0))],
        out_specs=pl.BlockSpec((tm, DH), lambda i: (i, 0)),
        out_shape=jax.ShapeDtypeStruct((T, DH), F32),
        scratch_shapes=[pltpu.VMEM((SLAB, DH), F32)],
        compiler_params=pltpu.CompilerParams(dimension_semantics=("arbitrary",)),
    )(fl, fb)


def _fgate_bwd(dF, fl, fb, T, tm=512):
    tm = min(tm, T)
    nt = T // tm

    def body(df_ref, fl_ref, fb_ref, o_ref, acc_ref, carry_ref):
        i = pl.program_id(0)

        @pl.when(i == 0)
        def _():
            carry_ref[...] = jnp.zeros_like(carry_ref)
            acc_ref[...] = jnp.zeros_like(acc_ref)

        rows = FGATE_GROUP * SLAB
        sub = lax.broadcasted_iota(jnp.int32, (rows, DH), 0) % SLAB

        def group(n, carry):
            g_next, acc = carry
            r0 = pl.multiple_of((tm // rows - 1 - n) * rows, rows)
            c = df_ref[pl.ds(r0, rows), :]
            for k in (1, 2, 4):
                c = c + jnp.where(sub < SLAB - k, pltpu.roll(c, rows - k, 0), 0.0)
            sg = _sig(-(fl_ref[pl.ds(r0, rows), :] + fb_ref[...]))
            for u in reversed(range(FGATE_GROUP)):
                cu = c[u * SLAB:(u + 1) * SLAB] + g_next
                dfl = cu * sg[u * SLAB:(u + 1) * SLAB]
                o_ref[pl.ds(r0 + u * SLAB, SLAB), :] = dfl.astype(o_ref.dtype)
                acc = acc + _colsum(dfl)
                g_next = cu[0:1, :]
            return g_next, acc

        g, acc = lax.fori_loop(0, tm // rows, group, (carry_ref[0:1, :], jnp.zeros((1, DH), F32)))
        carry_ref[0:1, :] = g
        acc_ref[...] += acc

    return pl.pallas_call(
        body, name="fgate_bwd", grid=(nt,),
        in_specs=[pl.BlockSpec((tm, DH), lambda i: (nt - 1 - i, 0)), pl.BlockSpec((tm, DH), lambda i: (nt - 1 - i, 0)),
                  pl.BlockSpec((1, DH), lambda i: (0, 0))],
        out_specs=[pl.BlockSpec((tm, DH), lambda i: (nt - 1 - i, 0)), pl.BlockSpec((1, DH), lambda i: (0, 0))],
        out_shape=[jax.ShapeDtypeStruct((T, DH), BF), jax.ShapeDtypeStruct((1, DH), F32)],
        scratch_shapes=[pltpu.VMEM((SLAB, DH), F32)],
        compiler_params=pltpu.CompilerParams(dimension_semantics=("arbitrary",)),
    )(dF, fl, fb)


def _shift_down(x, d, prev8):
    n = x.shape[0]
    row8 = lax.broadcasted_iota(jnp.int32, (SLAB, x.shape[1]), 0)
    y = pltpu.roll(x, d, 0)
    top = jnp.where(row8 < d, pltpu.roll(prev8, d, 0), y[0:SLAB])
    return jnp.concatenate([top, y[SLAB:]], axis=0)


def _shift_up(x, d, next8):
    n = x.shape[0]
    row8 = lax.broadcasted_iota(jnp.int32, (SLAB, x.shape[1]), 0)
    y = pltpu.roll(x, n - d, 0)
    bottom = jnp.where(row8 >= SLAB - d, pltpu.roll(next8, SLAB - d, 0), y[n - SLAB:])
    return jnp.concatenate([y[:n - SLAB], bottom], axis=0)


def _conv(x, prev8, cw, cb):
    xs = [x] + [_shift_down(x, d, prev8) for d in (1, 2, 3)]
    xa = cb + cw[3:4, :] * xs[0] + cw[2:3, :] * xs[1] + cw[1:2, :] * xs[2] + cw[0:1, :] * xs[3]
    return xa, xs


def _lru_gates(xa_g, wa_g, wx_g, ba_g, bx_g, sp_g):
    xb = xa_g.astype(BF)
    r = _sig(_dot(xb, wa_g) + ba_g)
    ig = _sig(_dot(xb, wx_g) + bx_g)
    la = -LRU_C * r * sp_g
    a = jnp.exp(la)
    mult = jnp.sqrt(_one_minus_sq(a, la))
    return r, ig, a, mult


def _lru_fwd(xg, cw, vec, wabd, wxbd, T, tm=256):
    tm = min(tm, T)
    nsl = tm // SLAB

    def body(x_ref, xp_ref, cw_ref, vec_ref, wa_ref, wx_ref, h_ref, a_s, b_s, carry_ref):
        i = pl.program_id(0)

        @pl.when(i == 0)
        def _():
            carry_ref[...] = jnp.zeros_like(carry_ref)

        x = x_ref[...].astype(F32)
        prev8 = jnp.where(i > 0, xp_ref[SLAB:, :].astype(F32), 0.0)
        vec_v = vec_ref[...]
        xa, _ = _conv(x, prev8, cw_ref[...], vec_v[0:1, :])
        sp = _softplus(-vec_v[3:4, :])
        for g in range(NBD):
            sl = slice(g * BD, (g + 1) * BD)
            _, ig, a, mult = _lru_gates(xa[:, sl], wa_ref[g], wx_ref[g], vec_v[1:2, sl], vec_v[2:3, sl], sp[:, sl])
            a_s[:, sl] = a
            b_s[:, sl] = mult * ig * xa[:, sl]

        def slab(s, carry):
            r0 = pl.multiple_of(s * SLAB, SLAB)
            A, B = _slab_scan_fwd(a_s[pl.ds(r0, SLAB), :], b_s[pl.ds(r0, SLAB), :])
            h = A * carry + B
            h_ref[pl.ds(r0, SLAB), :] = h
            return h[SLAB - 1:SLAB, :]

        carry_ref[0:1, :] = lax.fori_loop(0, nsl, slab, carry_ref[0:1, :])

    blk = 5 * _nbytes((tm, D), F32) + 2 * _nbytes((NBD, BD, BD), BF)
    return pl.pallas_call(
        body, name="lru_fwd", grid=(T // tm,),
        in_specs=[pl.BlockSpec((tm, D), lambda i: (i, 0)),
                  pl.BlockSpec((2 * SLAB, D), lambda i: (jnp.maximum(i * (nsl // 2) - 1, 0), 0)),
                  pl.BlockSpec((CONV, D), lambda i: (0, 0)),
                  pl.BlockSpec((SLAB, D), lambda i: (0, 0)),
                  pl.BlockSpec((NBD, BD, BD), lambda i: (0, 0, 0)),
                  pl.BlockSpec((NBD, BD, BD), lambda i: (0, 0, 0))],
        out_specs=pl.BlockSpec((tm, D), lambda i: (i, 0)),
        out_shape=jax.ShapeDtypeStruct((T, D), F32),
        scratch_shapes=[pltpu.VMEM((tm, D), F32), pltpu.VMEM((tm, D), F32), pltpu.VMEM((SLAB, D), F32)],
        compiler_params=pltpu.CompilerParams(dimension_semantics=("arbitrary",), vmem_limit_bytes=_vmem_limit(blk)),
    )(xg, xg, cw, vec, wabd, wxbd)


def _lru_bwd(xg, h, dha, cw, vec, wabd, wxbd, T, tm=256):
    tm = min(tm, T)
    nsl = tm // SLAB
    nt = T // tm

    def body(x_ref, xp_ref, h_ref, hp_ref, dh_ref, cw_ref, vec_ref, wa_ref, wx_ref,
             dx_ref, dwa_ref, dwx_ref, acc_ref, a_s, b_s, g_s, dxa_s, carry_ref, dxan_ref):
        n = pl.program_id(0)
        it = nt - 1 - n

        @pl.when(n == 0)
        def _():
            carry_ref[...] = jnp.zeros_like(carry_ref)
            dxan_ref[...] = jnp.zeros_like(dxan_ref)
            dwa_ref[...] = jnp.zeros_like(dwa_ref)
            dwx_ref[...] = jnp.zeros_like(dwx_ref)
            acc_ref[...] = jnp.zeros_like(acc_ref)

        x = x_ref[...].astype(F32)
        prev8 = jnp.where(it > 0, xp_ref[SLAB:, :].astype(F32), 0.0)
        hprev8 = jnp.where(it > 0, hp_ref[...], 0.0)
        vec_v = vec_ref[...]
        cw_v = cw_ref[...]
        xa, xs = _conv(x, prev8, cw_v, vec_v[0:1, :])
        sp = _softplus(-vec_v[3:4, :])
        gates = []
        for g in range(NBD):
            sl = slice(g * BD, (g + 1) * BD)
            r, ig, a, mult = _lru_gates(xa[:, sl], wa_ref[g], wx_ref[g], vec_v[1:2, sl], vec_v[2:3, sl], sp[:, sl])
            gates.append((r, ig, a, mult))
            a_s[:, sl] = a
        a_s[...] = _shift_up(a_s[...], 1, carry_ref[...])
        b_s[...] = dh_ref[...]

        def slab(m, carry):
            r0 = pl.multiple_of((nsl - 1 - m) * SLAB, SLAB)
            A, B = _slab_scan_bwd(a_s[pl.ds(r0, SLAB), :], b_s[pl.ds(r0, SLAB), :])
            gg = A * carry + B
            g_s[pl.ds(r0, SLAB), :] = gg
            return gg[0:1, :]

        g_first = lax.fori_loop(0, nsl, slab, carry_ref[1:2, :])
        gt = g_s[...]
        h_prev = _shift_down(h_ref[...], 1, hprev8)
        dba = []
        dbx = []
        dsp = []
        for g in range(NBD):
            sl = slice(g * BD, (g + 1) * BD)
            r, ig, a, mult = gates[g]
            xa_g = xa[:, sl]
            g_g = gt[:, sl]
            da = g_g * h_prev[:, sl]
            dmult = g_g * ig * xa_g
            di = g_g * mult * xa_g
            dxa_g = g_g * mult * ig
            dla = da * a - dmult * (a * a / mult)
            dr = dla * (-LRU_C) * sp[:, sl]
            dsp.append(_colsum(dla * (-LRU_C) * r))
            dra = (dr * r * (1.0 - r))
            dix = (di * ig * (1.0 - ig))
            dba.append(_colsum(dra))
            dbx.append(_colsum(dix))
            dra_b = dra.astype(BF)
            dix_b = dix.astype(BF)
            xb = xa_g.astype(BF)
            dxa_g = dxa_g + _dot(dra_b, wa_ref[g], "nt") + _dot(dix_b, wx_ref[g], "nt")
            dwa_ref[g] += _dot(xb, dra_b, "tn")
            dwx_ref[g] += _dot(xb, dix_b, "tn")
            dxa_s[:, sl] = dxa_g
        dxa = dxa_s[...]
        nxt = dxan_ref[...]
        dx = (cw_v[3:4, :] * dxa + cw_v[2:3, :] * _shift_up(dxa, 1, nxt)
              + cw_v[1:2, :] * _shift_up(dxa, 2, nxt) + cw_v[0:1, :] * _shift_up(dxa, 3, nxt))
        dx_ref[...] = dx.astype(dx_ref.dtype)
        acc_ref[0:1, :] += jnp.concatenate(dba, axis=1)
        acc_ref[1:2, :] += jnp.concatenate(dbx, axis=1)
        acc_ref[2:3, :] += jnp.concatenate(dsp, axis=1)
        acc_ref[3:4, :] += _colsum(dxa)
        for k in range(CONV):
            acc_ref[4 + k:5 + k, :] += _colsum(dxa * xs[CONV - 1 - k])
        dxan_ref[...] = dxa[0:SLAB, :]
        a_first = jnp.concatenate([gates[g][2][0:1, :] for g in range(NBD)], axis=1)
        carry_ref[0:1, :] = a_first
        carry_ref[1:2, :] = g_first

        @pl.when(n == nt - 1)
        def _():
            acc_ref[2:3, :] = acc_ref[2:3, :] * (-_sig(-vec_v[3:4, :]))

    rowblk = lambda i: (nt - 1 - i, 0)
    prevblk = lambda i: (jnp.maximum((nt - 1 - i) * nsl - 1, 0), 0)
    c2 = lambda i: (0, 0)
    c3 = lambda i: (0, 0, 0)
    blk = 12 * _nbytes((tm, D), F32) + 6 * _nbytes((NBD, BD, BD), F32)
    return pl.pallas_call(
        body, name="lru_bwd", grid=(nt,),
        in_specs=[pl.BlockSpec((tm, D), rowblk),
                  pl.BlockSpec((2 * SLAB, D), lambda i: (jnp.maximum((nt - 1 - i) * (nsl // 2) - 1, 0), 0)),
                  pl.BlockSpec((tm, D), rowblk), pl.BlockSpec((SLAB, D), prevblk),
                  pl.BlockSpec((tm, D), rowblk),
                  pl.BlockSpec((CONV, D), c2), pl.BlockSpec((SLAB, D), c2),
                  pl.BlockSpec((NBD, BD, BD), c3), pl.BlockSpec((NBD, BD, BD), c3)],
        out_specs=[pl.BlockSpec((tm, D), rowblk), pl.BlockSpec((NBD, BD, BD), c3), pl.BlockSpec((NBD, BD, BD), c3),
                   pl.BlockSpec((16, D), c2)],
        out_shape=[jax.ShapeDtypeStruct((T, D), BF), jax.ShapeDtypeStruct((NBD, BD, BD), F32),
                   jax.ShapeDtypeStruct((NBD, BD, BD), F32), jax.ShapeDtypeStruct((16, D), F32)],
        scratch_shapes=[pltpu.VMEM((tm, D), F32), pltpu.VMEM((tm, D), F32), pltpu.VMEM((tm, D), F32),
                        pltpu.VMEM((tm, D), F32), pltpu.VMEM((SLAB, D), F32), pltpu.VMEM((SLAB, D), F32)],
        compiler_params=pltpu.CompilerParams(dimension_semantics=("arbitrary",), vmem_limit_bytes=_vmem_limit(blk)),
    )(xg, xg, h, h, dha, cw, vec, wabd, wxbd)


_SCALE = 1.0 / math.sqrt(DH)


_ANY = pl.BlockSpec(memory_space=pl.ANY)


class _Side:
    def __init__(self, srcs, outs, nsem, copies):
        self.srcs, self.outs, self.nsem, self.copies = list(srcs), list(outs), nsem, copies


def _pallas(body, operands, *, name, grid, in_specs, out_specs, out_shape, scratch_shapes=(), semantics,
            vmem=None, side=None):
    if side is None:
        return pl.pallas_call(
            body, name=name, grid=grid, in_specs=in_specs, out_specs=out_specs, out_shape=out_shape,
            scratch_shapes=list(scratch_shapes),
            compiler_params=pltpu.CompilerParams(dimension_semantics=semantics, vmem_limit_bytes=vmem),
        )(*operands)
    n_in, n_out, n_scr = len(in_specs), len(out_specs), len(scratch_shapes)
    ns, no = len(side.srcs), len(side.outs)

    def hosted(*refs):
        ins, refs = refs[:n_in], refs[n_in:]
        sin, refs = refs[:ns], refs[ns:]
        outs, refs = refs[:n_out], refs[n_out:]
        sout, refs = refs[:no], refs[no:]
        scr, (send, recv) = refs[:n_scr], refs[n_scr:]
        ids = [pl.program_id(a) for a in range(len(grid))]
        first = functools.reduce(jnp.logical_and, [i == 0 for i in ids])
        last = functools.reduce(jnp.logical_and, [i == g - 1 for i, g in zip(ids, grid)])

        @pl.when(first)
        def _():
            for cp in side.copies(sin, sout, send, recv):
                cp.start()

        body(*ins, *outs, *scr)

        @pl.when(last)
        def _():
            for cp in side.copies(sin, sout, send, recv):
                cp.wait()

    return pl.pallas_call(
        hosted, name=name, grid=grid, in_specs=list(in_specs) + [_ANY] * ns, out_specs=list(out_specs) + [_ANY] * no,
        out_shape=list(out_shape) + side.outs,
        scratch_shapes=list(scratch_shapes) + [pltpu.SemaphoreType.DMA((side.nsem,)), pltpu.SemaphoreType.DMA((side.nsem,))],
        compiler_params=pltpu.CompilerParams(dimension_semantics=("arbitrary",) * len(grid), vmem_limit_bytes=vmem),
    )(*operands, *side.srcs)


DA = 2 * DH
_LOG2E = math.log2(math.e)
_C2 = _SCALE * _LOG2E


def _aug_fn(ins, cs):
    q, k, fcum = ins
    g_all = fcum * _LOG2E
    lane = lax.broadcasted_iota(jnp.int32, (q.shape[0], DH), 1)
    qa, ka = [], []
    for hd in range(NH):
        g = g_all[:, hd:hd + 1]
        hi = g.astype(BF).astype(F32)
        mid = (g - hi).astype(BF).astype(F32)
        lo = ((g - hi) - mid).astype(BF).astype(F32)
        qx = jnp.where(lane == 0, hi, jnp.where(lane == 1, mid, jnp.where(lane == 2, lo,
                                                                          jnp.where(lane < 6, 1.0, 0.0))))
        kx = jnp.where(lane < 3, 1.0, jnp.where(lane == 3, -hi, jnp.where(lane == 4, -mid,
                                                                          jnp.where(lane == 5, -lo, 0.0))))
        qa += [q[:, hd * DH:(hd + 1) * DH], qx.astype(BF)]
        ka += [k[:, hd * DH:(hd + 1) * DH], kx.astype(BF)]
    return [jnp.concatenate(qa, axis=1), jnp.concatenate(ka, axis=1)], []


_KA_ONES = DH + 3
KT_ONES = 16
FWD_KEY_PARTS = 2
BWD_QUERY_PARTS = 2


def _attn_fwd(qa, ka, qkv, T, blk=512, side=None):
    blk = min(blk, T)
    nb = T // blk
    half = blk // 2

    def body(q_ref, k_ref, vn_ref, o_ref, lse_ref, v_ref):
        i = pl.program_id(1)

        @pl.when(i == 0)
        def _():
            for jj in range(nb):
                v_ref[jj] = jnp.concatenate([vn_ref[jj * blk:(jj + 1) * blk, :].astype(F32).T.astype(BF),
                                             jnp.ones((KT_ONES, blk), BF)], axis=0)

        q = q_ref[...]

        def scores(j):
            r0 = pl.multiple_of(j * blk, blk)
            return _dot(k_ref[pl.ds(r0, blk), :], q, "nt")

        def update(s, vj, carry):
            m, acc = carry
            m_new = jnp.maximum(m, jnp.max(s, axis=0, keepdims=True))
            alpha = jnp.exp2(m - m_new)
            acc = alpha * acc + _dot(vj, jnp.exp2(s - m_new).astype(BF))
            return m_new, acc

        def step(j, st):
            r0 = pl.multiple_of(j * blk, blk)
            part = blk // FWD_KEY_PARTS
            ss = [_dot(k_ref[pl.ds(r0 + n * part, part), :], q, "nt") for n in range(FWD_KEY_PARTS)]
            vj = v_ref[j]
            for n in range(FWD_KEY_PARTS):
                st = update(ss[n], vj[:, n * part:(n + 1) * part], st)
            return st

        init = (jnp.full((1, blk), -jnp.inf, F32), jnp.zeros((DH + KT_ONES, blk), F32))
        carry = lax.fori_loop(0, i, step, init)
        last = scores(i)
        vi = v_ref[i]
        rk = lax.broadcasted_iota(jnp.int32, (half, blk), 0)
        cq = lax.broadcasted_iota(jnp.int32, (half, blk), 1)
        m, acc = update(jnp.where(cq >= rk, last[:half], -jnp.inf), vi[:, :half], carry)
        s2 = jnp.where(cq[:, :half] >= rk[:, :half], last[half:, half:], -jnp.inf)
        m2, acc2 = update(s2, vi[:, half:], (m[:, half:], acc[:, half:]))
        m = jnp.concatenate([m[:, :half], m2], axis=1)
        acc = jnp.concatenate([acc[:, :half], acc2], axis=1)
        l = acc[DH:DH + 1]
        o_ref[...] = (acc[:DH] / l).T.astype(o_ref.dtype)
        lse_ref[...] = m + jnp.log(l) * _LOG2E

    vm = _nbytes((T, DA), BF) + 2 * _nbytes((T, DH), BF) + 6 * _nbytes((blk, blk), F32)
    return _pallas(
        body, (qa, ka, qkv), name="attn_fwd", grid=(NH, nb),
        in_specs=[pl.BlockSpec((blk, DA), lambda h, i: (i, h)),
                  pl.BlockSpec((T, DA), lambda h, i: (0, h)),
                  pl.BlockSpec((T, DH), lambda h, i: (0, 2 * NH + h))],
        out_specs=[pl.BlockSpec((blk, DH), lambda h, i: (i, h)),
                   pl.BlockSpec((None, None, 1, blk), lambda h, i: (h, i, 0, 0))],
        out_shape=[jax.ShapeDtypeStruct((T, D), BF), jax.ShapeDtypeStruct((NH, nb, 1, blk), F32)],
        scratch_shapes=[pltpu.VMEM((nb, DH + KT_ONES, blk), BF)],
        semantics=("parallel", "arbitrary"), vmem=_vmem_limit(vm), side=side)


def _attn_bwd(qa, ka, qkv, do, lrow, drow, T, blk=512, side=None):
    blk = min(blk, T)
    nb = T // blk

    def body(ka_ref, v_ref, qa_ref, do_ref, l_ref, d_ref, dq_ref, dk_ref, dv_ref, dfs_ref, dft_ref, dq_s):
        j = pl.program_id(1)

        @pl.when(j == 0)
        def _():
            dq_s[...] = jnp.zeros_like(dq_s)

        row = lax.broadcasted_iota(jnp.int32, (DH + KT_ONES, blk), 0)
        dq_scale = jnp.where(row < DH, _SCALE, 1.0)

        kaj = ka_ref[...]
        ktj = jnp.concatenate([kaj[:, :DH].astype(F32).T.astype(BF), jnp.ones((KT_ONES, blk), BF)], axis=0)
        vj = v_ref[...]

        def step(i, carry):
            dka, dv = carry
            r0 = pl.multiple_of(i * blk, blk)
            part = blk // BWD_QUERY_PARTS
            lr, dr = l_ref[i], d_ref[i]
            loaded = []
            for n in range(BWD_QUERY_PARTS):
                qn = qa_ref[pl.ds(r0 + n * part, part), :]
                don = do_ref[pl.ds(r0 + n * part, part), :]
                loaded.append((qn, don, _dot(kaj, qn, "nt"), _dot(vj, don, "nt")))
            dq_parts = []
            for n, (qn, don, raw, dpt) in enumerate(loaded):
                pt = jnp.exp2(raw - lr[:, n * part:(n + 1) * part])
                dsb = (pt * (dpt - dr[:, n * part:(n + 1) * part])).astype(BF)
                dv = dv + _dot(pt.astype(BF), don)
                dka = dka + _dot(dsb, qn)
                dq_parts.append(_dot(ktj, dsb))
            dq_s[i] += jnp.concatenate(dq_parts, axis=1) * dq_scale
            return dka, dv

        def diagonal():
            half = blk // 2
            r0 = pl.multiple_of(j * blk, blk)
            qi = qa_ref[pl.ds(r0, blk), :]
            doi = do_ref[pl.ds(r0, blk), :]
            lr, dr = l_ref[j], d_ref[j]
            rk = lax.broadcasted_iota(jnp.int32, (half, blk), 0)
            cq = lax.broadcasted_iota(jnp.int32, (half, blk), 1)

            def quarter(ka, v, q, do, l2, d2, keep):
                pt = jnp.exp2(jnp.where(keep, _dot(ka, q, "nt") - l2, -jnp.inf))
                dsb = (pt * (_dot(v, do, "nt") - d2)).astype(BF)
                return _dot(dsb, q), _dot(pt.astype(BF), do), dsb

            dka1, dv1, ds1 = quarter(kaj[:half], vj[:half], qi, doi, lr, dr, cq >= rk)
            dka2, dv2, ds2 = quarter(kaj[half:], vj[half:], qi[half:], doi[half:], lr[:, half:], dr[:, half:],
                                     cq[:, :half] >= rk[:, :half])
            dq2 = jnp.concatenate([jnp.zeros((DH + KT_ONES, half), F32), _dot(ktj[:, half:], ds2)], axis=1)
            dq_s[j] += (_dot(ktj[:, :half], ds1) + dq2) * dq_scale
            return jnp.concatenate([dka1, dka2], axis=0), jnp.concatenate([dv1, dv2], axis=0)

        dka, dv = lax.fori_loop(j + 1, nb, step, diagonal())
        dk_ref[...] = (dka[:, :DH] * (1.0 / _LOG2E)).astype(dk_ref.dtype)
        dv_ref[...] = dv.astype(dv_ref.dtype)
        dfs_ref[...] = dka[:, DH:].T[_KA_ONES - DH:_KA_ONES - DH + 1, :]

        @pl.when(j == nb - 1)
        def _():
            for ii in range(nb):
                t = dq_s[ii]
                dq_ref[ii * blk:(ii + 1) * blk, :] = t[:DH].T.astype(dq_ref.dtype)
                dft_ref[ii] = t[DH:DH + 1]

    rowv = pl.BlockSpec((None, nb, 1, blk), lambda h, j: (h, 0, 0, 0))
    vm = (_nbytes((T, DA), BF) + 2 * _nbytes((T, DH), BF) + _nbytes((T, DH + KT_ONES), F32)
          + 8 * _nbytes((blk, blk), F32))
    return _pallas(
        body, (ka, qkv, qa, do, lrow, drow), name="attn_bwd", grid=(NH, nb),
        in_specs=[pl.BlockSpec((blk, DA), lambda h, j: (j, h)),
                  pl.BlockSpec((blk, DH), lambda h, j: (j, 2 * NH + h)),
                  pl.BlockSpec((T, DA), lambda h, j: (0, h)),
                  pl.BlockSpec((T, DH), lambda h, j: (0, h)),
                  rowv, rowv],
        out_specs=[pl.BlockSpec((T, DH), lambda h, j: (0, h)),
                   pl.BlockSpec((blk, DH), lambda h, j: (j, h)),
                   pl.BlockSpec((blk, DH), lambda h, j: (j, h)),
                   pl.BlockSpec((None, None, 1, blk), lambda h, j: (h, j, 0, 0)), rowv],
        out_shape=[jax.ShapeDtypeStruct((T, D), BF), jax.ShapeDtypeStruct((T, D), BF),
                   jax.ShapeDtypeStruct((T, D), BF), jax.ShapeDtypeStruct((NH, nb, 1, blk), F32),
                   jax.ShapeDtypeStruct((NH, nb, 1, blk), F32)],
        scratch_shapes=[pltpu.VMEM((nb, DH + KT_ONES, blk), F32)],
        semantics=("parallel", "arbitrary"), vmem=_vmem_limit(vm), side=side)


def _norm_fn(ins, cs):
    x, = ins
    g, = cs
    r = lax.rsqrt(jnp.mean(x * x, axis=-1, keepdims=True) + EPS)
    return [x * r * g], []


def _norm_bwd_fn(ins, cs):
    x, dy, dres = ins
    g, = cs
    r = lax.rsqrt(jnp.mean(x * x, axis=-1, keepdims=True) + EPS)
    xh = x * r
    dxh = dy * g
    dx = dres + r * (dxh - xh * jnp.mean(dxh * xh, axis=-1, keepdims=True))
    return [dx, dx], [_colsum(dy * xh)]


def _final_fn(ins, cs):
    x2, tgt = ins
    g, = cs
    r = lax.rsqrt(jnp.mean(x2 * x2, axis=-1, keepdims=True) + EPS)
    xh = x2 * r
    e = xh * g - tgt
    dy = e * (1.0 / D)
    dxh = dy * g
    dx2 = r * (dxh - xh * jnp.mean(dxh * xh, axis=-1, keepdims=True))
    return [dx2, dx2], [_colsum(0.5 * e * e * (1.0 / D)), _colsum(dy * xh)]


def _z_fn(ins, cs):
    g, h = [v.astype(F32) for v in ins]
    return [_gelu(g) * h], []


def _mix_fn(ins, cs):
    gates, ya, yb = [v.astype(F32) for v in ins]
    return [_sig(gates[:, :D]) * ya + _sig(gates[:, D:]) * yb], []


def _mix_bwd_fn(ins, cs):
    dmix, gates, ya, yb = [v.astype(F32) for v in ins]
    ga = _sig(gates[:, :D])
    gb = _sig(gates[:, D:])
    dgates = jnp.concatenate([dmix * ya * ga * (1.0 - ga), dmix * yb * gb * (1.0 - gb)], axis=1)
    return [dmix * ga, dmix * gb, dgates], []


def _z_bwd_fn(ins, cs):
    dz, g, h = [v.astype(F32) for v in ins]
    return [dz * _gelu(g), dz * h * _gelu_grad(g)], []


def _delta_fn(ins, cs):
    do, o = ins
    p = do.astype(F32) * o.astype(F32)
    lane = lax.broadcasted_iota(jnp.int32, (p.shape[0], DH), 1)
    out = jnp.zeros((p.shape[0], DH), F32)
    for hd in range(NH):
        s = jnp.sum(p[:, hd * DH:(hd + 1) * DH], axis=1, keepdims=True)
        out = jnp.where(lane == hd, s, out)
    return [out], []


def _du_all(pieces, win, T, tm=256, side=None):
    tm = min(tm, T)
    n = len(pieces)

    def body(*refs):
        w_ref, o_ref = refs[n], refs[n + 1]
        acc = None
        for (a, off), a_ref in zip(pieces, refs[:n]):
            d = _dot(a_ref[...].astype(BF), w_ref[:, off:off + a.shape[1]], "nt")
            acc = d if acc is None else acc + d
        o_ref[...] = acc

    vm = (sum(_nbytes((tm, a.shape[1]), a.dtype) for a, _ in pieces) + _nbytes(win.shape, win.dtype)
          + 2 * _nbytes((tm, D), F32))
    return _pallas(
        body, tuple(a for a, _ in pieces) + (win,), name="du_all", grid=(T // tm,),
        in_specs=[pl.BlockSpec((tm, a.shape[1]), lambda i: (i, 0)) for a, _ in pieces]
        + [pl.BlockSpec(win.shape, lambda i: (0, 0))],
        out_specs=[pl.BlockSpec((tm, D), lambda i: (i, 0))],
        out_shape=[jax.ShapeDtypeStruct((T, D), F32)],
        semantics=("arbitrary",), vmem=int(min(VMEM_CAP, 2 * vm + (4 << 20))), side=side)


def _local_step(x, tgt, w, T, blk=1024, dist=None):
    blk = min(blk, T)
    nb = T // blk
    win = w["win"]

    u, = _ew(_norm_fn, T, 1024, [(x, D, 0)], [w["g_mix"]], [(D, BF)], [], name="norm_mix")
    xg = _mm(u, win, "nn", T, 2 * D, D, name="proj_lru", out_dtype=BF)
    qkv = _mm(u, win, "nn", T, 3 * D, D, name="proj_qkv", out_dtype=BF, b_off=(0, 2),
              epi=lambda acc: acc * jnp.where(pl.program_id(1) == 0, _C2, 1.0))
    gates = _mm(u, win, "nn", T, 2 * D, D, name="proj_gates", b_off=(0, 5), out_dtype=BF)
    fl = _mm(u, win, "nn", T, DH, D, name="proj_f", tn=DH, b_off=(0, 7 * D // DH))
    fcum = _fgate_fwd(fl, w["fb"], T)
    qa, ka = _ew(_aug_fn, T, 512, [(qkv, D, 0), (qkv, D, 1), (fcum, DH, 0)], [],
                 [(NH * DA, BF), (NH * DA, BF)], [], name="attn_augment")

    h = _lru_fwd(xg, w["cw"], w["vec"], w["wabd"], w["wxbd"], T)
    ob, lse, *landed = _attn_fwd(qa, ka, qkv, T, blk, side=dist.weights_side() if dist else None)
    if dist:
        w = dict(w, **dist.weights_landed(landed))
    z, = _ew(_z_fn, T, 1024, [(xg, D, 1), (h, D, 0)], [], [(D, BF)], [], name="lru_gelu")
    ya = _mm(z, w["wa"], "nn", T, D, D, name="branch_a", out_dtype=BF)
    yb = _mm(ob, w["wb"], "nn", T, D, D, name="branch_b", out_dtype=BF)
    mix, = _ew(_mix_fn, T, 512, [(gates, 2 * D, 0), (ya, D, 0), (yb, D, 0)], [], [(D, BF)], [], name="mix")
    x1 = _mm(mix, w["wout"], "nn", T, D, D, name="out_proj", add=x)
    m, = _ew(_norm_fn, T, 1024, [(x1, D, 0)], [w["g_mlp"]], [(D, BF)], [], name="norm_mlp")
    hh = _mm(m, w["wup"], "nn", T, FF, D, name="mlp_up", out_dtype=BF,
             epi=lambda acc: jnp.square(jnp.maximum(acc, 0.0)))
    x2 = _mm(hh, w["wdown"], "nn", T, D, FF, name="mlp_down", add=x1, tk=2048)
    dx2, dx2b, loss_vec, dg_fin = _ew(_final_fn, T, 512, [(x2, D, 0), (tgt, D, 0)], [w["g_fin"]],
                                      [(D, F32), (D, BF)], [(1, D), (1, D)], name="final_norm_loss")

    dhpre = _mm(dx2b, w["wdown"], "nt", T, FF, D, name="mlp_down_bwd", out_dtype=BF,
                epi=lambda acc, h2: acc * (2.0 * jnp.sqrt(h2.astype(F32))), epi_ins=[hh])
    dwdown = _mm(hh, dx2b, "tn", FF, D, T, name="dw_down", out_dtype=BF)
    dwup = _mm(m, dhpre, "tn", D, FF, T, name="dw_up", out_dtype=BF)
    dm = _mm(dhpre, w["wup"], "nt", T, D, FF, name="mlp_up_bwd", tk=2048)
    dx1, dx1b, dg_mlp = _ew(_norm_bwd_fn, T, 512, [(x1, D, 0), (dm, D, 0), (dx2, D, 0)], [w["g_mlp"]],
                            [(D, F32), (D, BF)], [(1, D)], name="norm_mlp_bwd")

    dmix = _mm(dx1b, w["wout"], "nt", T, D, D, name="out_proj_bwd", out_dtype=BF)
    dwout = _mm(mix, dx1b, "tn", D, D, T, name="dw_out", out_dtype=BF)
    dya, dyb, dgates = _ew(_mix_bwd_fn, T, 512, [(dmix, D, 0), (gates, 2 * D, 0), (ya, D, 0), (yb, D, 0)], [],
                           [(D, BF), (D, BF), (2 * D, BF)], [], name="mix_bwd")
    dob = _mm(dyb, w["wb"], "nt", T, D, D, name="branch_b_bwd", out_dtype=BF)
    dwb = _mm(ob, dyb, "tn", D, D, T, name="dw_b", out_dtype=BF)
    dz = _mm(dya, w["wa"], "nt", T, D, D, name="branch_a_bwd", out_dtype=BF)
    dwa = _mm(z, dya, "tn", D, D, T, name="dw_a", out_dtype=BF)
    dha, dglru = _ew(_z_bwd_fn, T, 512, [(dz, D, 0), (xg, D, 1), (h, D, 0)], [], [(D, F32), (D, BF)], [],
                     name="lru_gelu_bwd")

    delta, = _ew(_delta_fn, T, 1024, [(dob, D, 0), (ob, D, 0)], [], [(DH, F32)], [], name="attn_delta")
    drow = delta[:, :NH].T.reshape(NH, nb, 1, blk)
    big = dict(w_branch_a=dwa, w_branch_b=dwb, w_out=dwout, w_up=dwup, w_down=dwdown)
    side = dist.grads_side(big) if dist else None
    dq, dk, dv, dfs, dft, *landed = _attn_bwd(qa, ka, qkv, dob, lse, drow, T, blk, side=side)
    if dist:
        big = dist.grads_landed(side, landed)
    dfcum = jnp.pad((dft - dfs).reshape(NH, T).T, ((0, 0), (0, DH - NH)))
    dfl, dfb = _fgate_bwd(dfcum, fl, w["fb"], T)

    dxl, dwabd, dwxbd, lacc = _lru_bwd(xg, h, dha, w["cw"], w["vec"], w["wabd"], w["wxbd"], T)

    dproj = ((dxl, 0), (dglru, D), (dq, 2 * D), (dk, 3 * D), (dv, 4 * D), (dgates, 5 * D), (dfl, 7 * D))
    pieces = [_mm(u, p, "tn", D, p.shape[1], T, name="dw_in_%d" % n, out_dtype=BF)
              for n, (p, _) in enumerate(dproj)]
    pieces[-1] = pieces[-1][:, :NH]
    dwin = dict(w_in=jnp.concatenate(pieces, axis=1))
    side = dist.grads_side(dwin) if dist else None
    du, *landed = _du_all(dproj, win, T, side=side)
    big.update(dist.grads_landed(side, landed) if dist else dwin)
    dx, dg_mix = _ew(_norm_bwd_fn, T, 512, [(x, D, 0), (du, D, 0), (dx1, D, 0)], [w["g_mix"]], [(D, F32)],
                     [(1, D)], name="norm_mix_bwd")

    return dict(dx=dx, big=big, dwabd=dwabd, dwxbd=dwxbd, lacc=lacc, dfb=dfb, dg_mix=dg_mix, dg_mlp=dg_mlp,
                dg_fin=dg_fin, loss_vec=loss_vec)


def _block_diag(w):
    per = BD // LRU_BW
    w4 = w.reshape(NBD, per, LRU_BW, LRU_BW)
    out = jnp.zeros((NBD, per, LRU_BW, per, LRU_BW), w.dtype)
    for b in range(per):
        out = out.at[:, b, :, b, :].set(w4[:, b])
    return out.reshape(NBD, BD, BD)


def _block_diag_extract(wbd):
    per = BD // LRU_BW
    w5 = wbd.reshape(NBD, per, LRU_BW, per, LRU_BW)
    return jnp.stack([w5[:, b, :, b, :] for b in range(per)], axis=1).reshape(LRU_BLOCKS, LRU_BW, LRU_BW)


_ANY = pl.BlockSpec(memory_space=pl.ANY)


def _place():
    x, y, c = lax.axis_index("x"), lax.axis_index("y"), lax.axis_index("c")
    chips = [(1 - x, y), (x, 1 - y), (1 - x, 1 - y)]
    return x, y, c, chips


def _allgather_shards(shards):
    n = len(shards)

    def body(*refs):
        ins, outs = refs[:n], refs[n:2 * n]
        send_sems, recv_sems = refs[2 * n:]
        x, y, c, chips = _place()
        me = 2 * x + y
        sibling = (x, y, 1 - c)

        def remote(p, k, src, dst, to):
            return pltpu.make_async_remote_copy(src_ref=src, dst_ref=dst, send_sem=send_sems.at[p, k],
                                                recv_sem=recv_sems.at[p, k], device_id=to, device_id_type=MESH)

        sent = []
        for p in range(n):
            for k, chip in enumerate(chips):
                cp = remote(p, k, ins[p].at[c], outs[p].at[me, c], (chip[0], chip[1], c))
                cp.start()
                sent.append(cp)
        for p in range(n):
            for k, chip in enumerate(chips):
                half = outs[p].at[2 * chip[0] + chip[1], c]
                remote(p, k, half, half, sibling).wait_recv()
                fwd = remote(p, 3 + k, half, half, sibling)
                fwd.start()
                sent.append(fwd)
        for p in range(n):
            for k, chip in enumerate(chips):
                half = outs[p].at[2 * chip[0] + chip[1], 1 - c]
                remote(p, 3 + k, half, half, sibling).wait_recv()
        for cp in sent:
            cp.wait_send()

    gathered = pl.pallas_call(
        body, name="allgather_weights",
        in_specs=[_ANY] * n, out_specs=[_ANY] * n,
        out_shape=[jax.ShapeDtypeStruct((NCHIP,) + s.shape, s.dtype) for s in shards],
        scratch_shapes=[pltpu.SemaphoreType.DMA((n, 6)), pltpu.SemaphoreType.DMA((n, 6))],
    )(*shards)
    me = 2 * lax.axis_index("x") + lax.axis_index("y")
    return [lax.dynamic_update_index_in_dim(g, s, me, 0) for g, s in zip(gathered, shards)]


_LATE =["w_branch_a", "w_branch_b", "w_out", "w_up", "w_down"]
_COLUMN_CUT = ("w_in", "w_up")
N_PEERS = 7


def _shard_major(name, g):
    s = _columns_to_shards(g) if name in _COLUMN_CUT else g.reshape(NCHIP, g.shape[0] // NCHIP, g.shape[1])
    return s.reshape(NCHIP, 2, s.shape[1] // 2, s.shape[2])


class _Exchanges:
    def __init__(self, shards):
        self.shards = shards

    def weights_side(self):
        srcs = [self.shards[n] for n in _LATE]

        def copies(sin, sout, send, recv):
            x, y, c, chips = _place()
            return [pltpu.make_async_remote_copy(
                src_ref=sin[p], dst_ref=sout[p].at[2 * x + y], send_sem=send.at[3 * p + k], recv_sem=recv.at[3 * p + k],
                device_id=(chip[0], chip[1], c), device_id_type=MESH)
                for p in range(len(sin)) for k, chip in enumerate(chips)]

        return _Side(srcs, [jax.ShapeDtypeStruct((NCHIP,) + s.shape, s.dtype) for s in srcs], 3 * len(srcs), copies)

    def weights_landed(self, landed):
        me = 2 * lax.axis_index("x") + lax.axis_index("y")
        full = {n: lax.dynamic_update_index_in_dim(g, self.shards[n], me, 0) for n, g in zip(_LATE, landed)}
        return dict(wa=full["w_branch_a"].reshape(D, D), wb=full["w_branch_b"].reshape(D, D),
                    wout=full["w_out"].reshape(D, D), wup=_shards_to_columns(full["w_up"]),
                    wdown=full["w_down"].reshape(FF, D))

    def grads_side(self, grads):
        side_names = list(grads)
        srcs = [_shard_major(n, grads[n]) for n in side_names]

        def copies(sin, sout, send, recv):
            x, y, c, chips = _place()
            peers = [(x, y, 1 - c)] + [(cx, cy, c) for cx, cy in chips] + [(cx, cy, 1 - c) for cx, cy in chips]
            return [pltpu.make_async_remote_copy(
                src_ref=sin[p].at[2 * px + py, pc], dst_ref=sout[p].at[s], send_sem=send.at[N_PEERS * p + s],
                recv_sem=recv.at[N_PEERS * p + s], device_id=(px, py, pc), device_id_type=MESH)
                for p in range(len(sin)) for s, (px, py, pc) in enumerate(peers)]

        side = _Side(srcs, [jax.ShapeDtypeStruct((N_PEERS,) + s.shape[2:], s.dtype) for s in srcs],
                     N_PEERS * len(srcs), copies)
        side.names = side_names
        return side

    def grads_landed(self, side, landed):
        return {n: (own, got) for n, own, got in zip(side.names, side.srcs, landed)}


def _add8(g, got, me, c, name):
    _, _, half, cols = g.shape
    th = _row_tile(half, 2 * cols)

    def body(me_ref, c_ref, g_ref, r_ref, o_ref):
        acc = g_ref[...].astype(F32)
        for s in range(N_PEERS):
            acc = acc + r_ref[s].astype(F32)
        o_ref[...] = acc

    return pl.pallas_call(
        body, name=name,
        grid_spec=pltpu.PrefetchScalarGridSpec(
            num_scalar_prefetch=2, grid=(half // th,),
            in_specs=[pl.BlockSpec((None, None, th, cols), lambda i, me_ref, c_ref: (me_ref[0], c_ref[0], i, 0)),
                      pl.BlockSpec((N_PEERS, th, cols), lambda i, me_ref, c_ref: (0, i, 0))],
            out_specs=pl.BlockSpec((th, cols), lambda i, me_ref, c_ref: (i, 0))),
        out_shape=jax.ShapeDtypeStruct((half, cols), F32),
    )(me, c, g, got)


def _share_halves(halves):
    n = len(halves)

    def body(*refs):
        ins, outs = refs[:n], refs[n:2 * n]
        send_sems, recv_sems = refs[2 * n:]
        x, y, c, _ = _place()
        sibling = (x, y, 1 - c)
        copies = []
        for p in range(n):
            cp = pltpu.make_async_remote_copy(src_ref=ins[p], dst_ref=outs[p], send_sem=send_sems.at[p],
                                              recv_sem=recv_sems.at[p], device_id=sibling, device_id_type=MESH)
            cp.start()
            copies.append(cp)
        for cp in copies:
            cp.wait()

    return pl.pallas_call(
        body, name="reduce_share_halves",
        in_specs=[_ANY] * n, out_specs=[_ANY] * n,
        out_shape=[jax.ShapeDtypeStruct(h.shape, h.dtype) for h in halves],
        scratch_shapes=[pltpu.SemaphoreType.DMA((n,)), pltpu.SemaphoreType.DMA((n,))],
    )(*halves)


def _row_tile(half, cols):
    th = max(SLAB, min(half, (1 << 18) // cols // SLAB * SLAB))
    while half % th:
        th -= SLAB
    return th


N_DEV = 8
SMALL_ROWS = 208


def _allreduce_small(pack):
    def body(x_ref, out_ref, gbuf, send_sems, recv_sems, local_sem):
        x, y, c, chips = _place()
        me, sibling = (x, y, c), (x, y, 1 - c)

        def rows(px, py, pc):
            return gbuf.at[4 * px + 2 * py + pc]

        def copy(k, block, to, src=None):
            return pltpu.make_async_remote_copy(
                src_ref=rows(*block) if src is None else src, dst_ref=rows(*block),
                send_sem=send_sems.at[k], recv_sem=recv_sems.at[k], device_id=to, device_id_type=MESH)

        mine = pltpu.make_async_copy(x_ref, rows(*me), local_sem)
        mine.start()
        first = [copy(0, me, sibling, src=x_ref)]
        first += [copy(1 + j, me, (chip[0], chip[1], c), src=x_ref) for j, chip in enumerate(chips)]
        for cp in first:
            cp.start()
        passed = [copy(4 + j, (chip[0], chip[1], c), sibling) for j, chip in enumerate(chips)]
        for j, chip in enumerate(chips):
            copy(1 + j, (chip[0], chip[1], c), me).wait_recv()
            passed[j].start()
        copy(0, sibling, me).wait_recv()
        for j, chip in enumerate(chips):
            copy(4 + j, (chip[0], chip[1], 1 - c), me).wait_recv()
        for cp in first + passed:
            cp.wait_send()
        mine.wait()
        acc = gbuf[0]
        for d in range(1, N_DEV):
            acc = acc + gbuf[d]
        out_ref[...] = acc

    return pl.pallas_call(
        body, name="allreduce_small",
        in_specs=[pl.BlockSpec(memory_space=pltpu.VMEM)],
        out_specs=pl.BlockSpec(memory_space=pltpu.VMEM),
        out_shape=jax.ShapeDtypeStruct((SMALL_ROWS, D), F32),
        scratch_shapes=[pltpu.VMEM((N_DEV, SMALL_ROWS, D), F32), pltpu.SemaphoreType.DMA((7,)),
                        pltpu.SemaphoreType.DMA((7,)), pltpu.SemaphoreType.DMA],
    )(pack)


def _adamw(w, g, m, v, name):
    rows, cols = w.shape

    def body(w_ref, g_ref, m_ref, v_ref, d_ref, mo_ref, vo_ref):
        gv = g_ref[...]
        mn = ADAM_B1 * m_ref[...] + (1.0 - ADAM_B1) * gv
        vn = ADAM_B2 * v_ref[...] + (1.0 - ADAM_B2) * (gv * gv)
        m_hat = mn / (1.0 - ADAM_B1 ** ADAM_STEP)
        v_hat = vn / (1.0 - ADAM_B2 ** ADAM_STEP)
        d_ref[...] = -ADAM_LR * (m_hat / (jnp.sqrt(v_hat) + ADAM_EPS) + ADAM_WD * w_ref[...])
        mo_ref[...] = mn
        vo_ref[...] = vn

    if rows % SLAB:
        spec, steps = pl.BlockSpec((rows, DH), lambda i: (0, i)), cols // DH
    else:
        th = _row_tile(rows, cols)
        spec, steps = pl.BlockSpec((th, cols), lambda i: (i, 0)), rows // th
    return pl.pallas_call(
        body, name=name, grid=(steps,),
        in_specs=[spec] * 4, out_specs=[spec] * 3,
        out_shape=[jax.ShapeDtypeStruct((rows, cols), F32)] * 3,
        compiler_params=pltpu.CompilerParams(dimension_semantics=("parallel",)),
    )(w, g, m, v)


_SMALL = ["norm_mix_g", "norm_mlp_g", "norm_final_g", "conv_b", "lru_ba", "lru_bx", "lru_lambda"]
_ROW_FB, _ROW_CW, _ROW_WA, _ROW_WX, _ROW_LOSS = 56, 64, 72, 136, 200


def _pack_small(vals, col0):
    def slab(a):
        return jnp.pad(a, ((0, -a.shape[0] % SLAB), (0, D - a.shape[1])))

    rows = [slab(vals[n].reshape(1, D)) for n in _SMALL]
    rows.append(slab(vals["forget_b"].reshape(1, NH)))
    if vals["conv_w"].shape[1] == D:
        rows.append(slab(vals["conv_w"]))
    else:
        rows.append(slab(lax.dynamic_update_slice(jnp.zeros((CONV, D), F32), vals["conv_w"], (0, col0))))
    rows.append(vals["lru_wa"].reshape(LRU_BLOCKS * LRU_BW * LRU_BW // D, D))
    rows.append(vals["lru_wx"].reshape(LRU_BLOCKS * LRU_BW * LRU_BW // D, D))
    rows.append(slab(vals["loss"]) if "loss" in vals else jnp.zeros((SLAB, D), F32))
    return jnp.concatenate(rows, axis=0)


def _unpack_small(pack, col0):
    out = {n: pack[SLAB * i] for i, n in enumerate(_SMALL)}
    out["forget_b"] = pack[_ROW_FB, :NH]
    out["conv_w"] = lax.dynamic_slice(pack[_ROW_CW:_ROW_CW + CONV], (0, col0), (CONV, D // NCHIP))
    out["lru_wa"] = pack[_ROW_WA:_ROW_WX].reshape(LRU_BLOCKS, LRU_BW, LRU_BW)
    out["lru_wx"] = pack[_ROW_WX:_ROW_LOSS].reshape(LRU_BLOCKS, LRU_BW, LRU_BW)
    return out


_WEIGHTS = ["norm_mix_g", "w_in", "conv_w", "conv_b", "lru_wa", "lru_ba", "lru_wx", "lru_bx", "lru_lambda",
            "forget_b", "w_branch_a", "w_branch_b", "w_out", "norm_mlp_g", "w_up", "w_down", "norm_final_g"]
_BIG = ["w_in", "w_branch_a", "w_branch_b", "w_out", "w_up", "w_down"]


def _halves(a):
    return a.reshape(2, a.shape[0] // 2, a.shape[1])


def _columns_to_shards(a):
    rows, cols = a.shape[0], a.shape[1] // NCHIP
    return jnp.transpose(a.reshape(rows, NCHIP, cols), (1, 0, 2))


def _shards_to_columns(a):
    n, rows, cols = a.shape
    return jnp.transpose(a, (1, 0, 2)).reshape(rows, n * cols)


def kernel(x, norm_mix_g, w_in, conv_w, conv_b, lru_wa, lru_ba, lru_wx, lru_bx, lru_lambda, forget_b, w_branch_a, w_branch_b, w_out, norm_mlp_g, w_up, w_down, norm_final_g, loss_target, m_norm_mix_g, m_w_in, m_conv_w, m_conv_b, m_lru_wa, m_lru_ba, m_lru_wx, m_lru_bx, m_lru_lambda, m_forget_b, m_w_branch_a, m_w_branch_b, m_w_out, m_norm_mlp_g, m_w_up, m_w_down, m_norm_final_g, v_norm_mix_g, v_w_in, v_conv_w, v_conv_b, v_lru_wa, v_lru_ba, v_lru_wx, v_lru_bx, v_lru_lambda, v_forget_b, v_w_branch_a, v_w_branch_b, v_w_out, v_norm_mlp_g, v_w_up, v_w_down, v_norm_final_g):
    args = dict(locals())
    wts = {n: args[n] for n in _WEIGHTS}
    mom = {n: args["m_" + n] for n in _WEIGHTS}
    var = {n: args["v_" + n] for n in _WEIGHTS}
    T = x.shape[1]
    xi, yi, ci = lax.axis_index("x"), lax.axis_index("y"), lax.axis_index("c")
    me = 2 * xi + yi
    c1 = jnp.reshape(ci, (1,)).astype(jnp.int32)
    me1 = jnp.reshape(me, (1,)).astype(jnp.int32)
    col0 = me * (D // NCHIP)

    cw_pad = jnp.pad(conv_w, ((0, 4 * SLAB - CONV), (0, 0)))
    g_in, g_cw = _allgather_shards([_halves(w_in.astype(BF)), _halves(cw_pad)])
    cin = DIN // NCHIP
    win = _shards_to_columns(g_in.reshape(NCHIP, D, cin))
    w = dict(
        win=jnp.pad(win, ((0, 0), (0, DINP - DIN))),
        cw=_shards_to_columns(g_cw.reshape(NCHIP, 4 * SLAB, D // NCHIP)[:, :CONV]),
        vec=jnp.concatenate([conv_b[None], lru_ba[None], lru_bx[None], lru_lambda[None],
                             jnp.zeros((SLAB - 4, D), F32)], axis=0),
        fb=jnp.pad(forget_b[None], ((0, 0), (0, DH - NH))),
        wabd=_block_diag(lru_wa).astype(BF), wxbd=_block_diag(lru_wx).astype(BF),
        g_mix=norm_mix_g[None], g_mlp=norm_mlp_g[None], g_fin=norm_final_g[None])

    r = _local_step(x[0], loss_target[0], w, T, dist=_Exchanges({n: wts[n].astype(BF) for n in _LATE}))

    halves = [_add8(*r["big"][n], me1, c1, "add8_" + n) for n in _BIG]
    theirs = _share_halves(halves)
    low = ci == 0
    gsum = {n: jnp.concatenate([jnp.where(low, h, t), jnp.where(low, t, h)], axis=0)
            for n, h, t in zip(_BIG, halves, theirs)}
    lacc = r["lacc"]
    small = dict(norm_mix_g=r["dg_mix"], norm_mlp_g=r["dg_mlp"], norm_final_g=r["dg_fin"], conv_b=lacc[3],
                 lru_ba=lacc[0], lru_bx=lacc[1], lru_lambda=lacc[2], forget_b=r["dfb"][0, :NH],
                 conv_w=lacc[4:4 + CONV], lru_wa=_block_diag_extract(r["dwabd"]),
                 lru_wx=_block_diag_extract(r["dwxbd"]), loss=r["loss_vec"])
    gpack = _allreduce_small(_pack_small(small, col0))
    loss = jnp.sum(gpack[_ROW_LOSS])

    grads, delta, new_m, new_v = {}, {}, {}, {}
    for n in _BIG:
        if n == "w_in":
            gt = gsum[n].T
            grads[n] = gt.T
            delta[n], new_m[n], new_v[n] = [a.T for a in _adamw(wts[n].T, gt, mom[n].T, var[n].T, "adamw_" + n)]
        else:
            grads[n] = gsum[n]
            delta[n], new_m[n], new_v[n] = _adamw(wts[n], gsum[n], mom[n], var[n], "adamw_" + n)
    dp, mp, vp = _adamw(_pack_small(wts, col0), gpack, _pack_small(mom, col0), _pack_small(var, col0), "adamw_small")
    for dst, pack in ((grads, gpack), (delta, dp), (new_m, mp), (new_v, vp)):
        dst.update(_unpack_small(pack, col0))
    return (loss, r["dx"][None], *[grads[n] for n in _WEIGHTS], *[delta[n] for n in _WEIGHTS],
            *[new_m[n] for n in _WEIGHTS], *[new_v[n] for n in _WEIGHTS])
```

```python
import functools
import math

import jax
import jax.numpy as jnp
import numpy as np
from jax import lax
from jax.experimental import pallas as pl
from jax.experimental.pallas import tpu as pltpu

F32 = jnp.float32
BF = jnp.bfloat16

D = 1024
NH = 8
DH = 128
FF = 4096
CONV = 4
LRU_BLOCKS = 16
LRU_BW = 64
BD = 256
NBD = D // BD
LRU_C = 8.0
EPS = 1e-6
DIN = 7176
DINP = 7296
NCHIP = 4
SLAB = 8
VMEM_CAP = 60 * 1024 * 1024

ADAM_LR = 0.001
ADAM_B1 = 0.9
ADAM_B2 = 0.999
ADAM_EPS = 1e-08
ADAM_WD = 0.01
ADAM_STEP = 10

MESH = pl.DeviceIdType.MESH


def _vmem_limit(nbytes):
    return int(min(VMEM_CAP, max(32 * 1024 * 1024, 3 * nbytes)))


def _nbytes(shape, dtype):
    return int(np.prod(shape)) * jnp.dtype(dtype).itemsize


def _sig(x):
    return 0.5 * jnp.tanh(0.5 * x) + 0.5


def _log1p(u):
    w = 1.0 + u
    return jnp.where(w == 1.0, u, jnp.log(w) * (u / (w - 1.0)))


def _one_minus_sq(a, la):
    return jnp.tanh(-la) * (1.0 + a * a)


def _softplus(z):
    return jnp.maximum(z, 0.0) + _log1p(jnp.exp(-jnp.abs(z)))


_GELU_C = math.sqrt(2.0 / math.pi)


def _gelu(x):
    return 0.5 * x * (1.0 + jnp.tanh(_GELU_C * (x + 0.044715 * x * x * x)))


def _gelu_grad(x):
    t = jnp.tanh(_GELU_C * (x + 0.044715 * x * x * x))
    return 0.5 * (1.0 + t) + 0.5 * x * (1.0 - t * t) * _GELU_C * (1.0 + 3.0 * 0.044715 * x * x)


def _slab_scan_fwd(a, b):
    row = lax.broadcasted_iota(jnp.int32, a.shape, 0)
    for k in (1, 2, 4):
        a_s = pltpu.roll(a, k, 0)
        b_s = pltpu.roll(b, k, 0)
        m = row >= k
        b = jnp.where(m, a * b_s + b, b)
        a = jnp.where(m, a * a_s, a)
    return a, b


def _slab_scan_bwd(a, b):
    row = lax.broadcasted_iota(jnp.int32, a.shape, 0)
    for k in (1, 2, 4):
        a_s = pltpu.roll(a, SLAB - k, 0)
        b_s = pltpu.roll(b, SLAB - k, 0)
        m = row < SLAB - k
        b = jnp.where(m, a * b_s + b, b)
        a = jnp.where(m, a * a_s, a)
    return a, b


_DIMS = {"nn": (((1,), (0,)), ((), ())), "nt": (((1,), (1,)), ((), ())), "tn": (((0,), (0,)), ((), ()))}


def _dot(a, b, mode="nn"):
    return lax.dot_general(a, b, _DIMS[mode], preferred_element_type=F32)


def _mm(a, b, mode, M, N, K, *, name, out_dtype=F32, tm=None, tn=1024, tk=1024,
        a_off=(0, 0), b_off=(0, 0), add=None, epi=None, epi_ins=()):
    if tm is None:
        tm = 2048 if (K <= tk and out_dtype == BF and add is None) else 1024
    tm, tn, tk = min(tm, M), min(tn, N), min(tk, K)
    nk = K // tk
    grid = (M // tm, N // tn, nk)
    if mode == "nn":
        a_spec = pl.BlockSpec((tm, tk), lambda i, j, k: (i + a_off[0], k + a_off[1]))
        b_spec = pl.BlockSpec((tk, tn), lambda i, j, k: (k + b_off[0], j + b_off[1]))
    elif mode == "nt":
        a_spec = pl.BlockSpec((tm, tk), lambda i, j, k: (i + a_off[0], k + a_off[1]))
        b_spec = pl.BlockSpec((tn, tk), lambda i, j, k: (j + b_off[0], k + b_off[1]))
    else:
        a_spec = pl.BlockSpec((tk, tm), lambda i, j, k: (k + a_off[0], i + a_off[1]))
        b_spec = pl.BlockSpec((tk, tn), lambda i, j, k: (k + b_off[0], j + b_off[1]))
    o_spec = pl.BlockSpec((tm, tn), lambda i, j, k: (i, j))
    extra = ([add] if add is not None else []) + list(epi_ins)
    n_extra = len(extra)
    has_add = add is not None

    def body(*refs):
        a_ref, b_ref = refs[0], refs[1]
        ex = refs[2:2 + n_extra]
        o_ref = refs[2 + n_extra]

        def finish(acc):
            if has_add:
                acc = acc + ex[0][...].astype(F32)
            if epi is not None:
                acc = epi(acc, *[e[...] for e in ex[(1 if has_add else 0):]])
            o_ref[...] = acc.astype(o_ref.dtype)

        p = _dot(a_ref[...].astype(BF), b_ref[...].astype(BF), mode)
        if nk == 1:
            finish(p)
        else:
            acc_ref = refs[3 + n_extra]
            k = pl.program_id(2)

            @pl.when(k == 0)
            def _():
                acc_ref[...] = p

            @pl.when(k > 0)
            def _():
                acc_ref[...] += p

            @pl.when(k == nk - 1)
            def _():
                finish(acc_ref[...])

    blk = (_nbytes((tm, tk), a.dtype) + _nbytes((tk, tn), b.dtype) + _nbytes((tm, tn), out_dtype)
           + sum(_nbytes((tm, tn), e.dtype) for e in extra) + 2 * _nbytes((tm, tn), F32))
    return pl.pallas_call(
        body, name=name, grid=grid,
        in_specs=[a_spec, b_spec] + [o_spec] * n_extra,
        out_specs=o_spec,
        out_shape=jax.ShapeDtypeStruct((M, N), out_dtype),
        scratch_shapes=[pltpu.VMEM((tm, tn), F32)] if nk > 1 else [],
        compiler_params=pltpu.CompilerParams(
            dimension_semantics=("parallel", "parallel", "arbitrary"), vmem_limit_bytes=_vmem_limit(blk)),
    )(a, b, *extra)


def _ew(fn, T, tm, ins, consts, outs, accs, *, name, reverse=False):
    tm = min(tm, T)
    nt = T // tm
    n_in, n_c, n_o, n_a = len(ins), len(consts), len(outs), len(accs)

    def row(i):
        return nt - 1 - i if reverse else i

    in_specs = [pl.BlockSpec((tm, w), functools.partial(lambda i, cb: (row(i), cb), cb=cb)) for (_, w, cb) in ins]
    in_specs += [pl.BlockSpec(c.shape, functools.partial(lambda i, nd: (0,) * nd, nd=c.ndim)) for c in consts]
    out_specs = [pl.BlockSpec((tm, w), lambda i: (row(i), 0)) for (w, _) in outs]
    out_specs += [pl.BlockSpec((r, w), lambda i: (0, 0)) for (r, w) in accs]
    out_shape = [jax.ShapeDtypeStruct((T, w), dt) for (w, dt) in outs]
    out_shape += [jax.ShapeDtypeStruct((r, w), F32) for (r, w) in accs]

    def body(*refs):
        in_refs = refs[:n_in]
        c_refs = refs[n_in:n_in + n_c]
        o_refs = refs[n_in + n_c:n_in + n_c + n_o]
        a_refs = refs[n_in + n_c + n_o:]
        ov, av = fn([r[...] for r in in_refs], [r[...] for r in c_refs])
        for r, v in zip(o_refs, ov):
            r[...] = v.astype(r.dtype)
        if n_a:
            i = pl.program_id(0)

            @pl.when(i == 0)
            def _():
                for r, v in zip(a_refs, av):
                    r[...] = v

            @pl.when(i > 0)
            def _():
                for r, v in zip(a_refs, av):
                    r[...] += v

    blk = (sum(_nbytes((tm, w), a.dtype) for (a, w, _) in ins) + sum(_nbytes(c.shape, c.dtype) for c in consts)
           + sum(_nbytes((tm, w), dt) for (w, dt) in outs) + sum(_nbytes(s, F32) for s in accs))
    res = pl.pallas_call(
        body, name=name, grid=(nt,), in_specs=in_specs, out_specs=out_specs, out_shape=out_shape,
        compiler_params=pltpu.CompilerParams(
            dimension_semantics=("arbitrary",), vmem_limit_bytes=_vmem_limit(blk)),
    )(*[a for (a, _, _) in ins], *consts)
    return res


def _colsum(v):
    return jnp.sum(v, axis=0, keepdims=True)


FGATE_GROUP = 4


def _fgate_fwd(fl, fb, T, tm=512):
    tm = min(tm, T)

    def body(fl_ref, fb_ref, f_ref, carry_ref):
        i = pl.program_id(0)

        @pl.when(i == 0)
        def _():
            carry_ref[...] = jnp.zeros_like(carry_ref)

        rows = FGATE_GROUP * SLAB
        sub = lax.broadcasted_iota(jnp.int32, (rows, DH), 0) % SLAB

        def group(s, carry):
            r0 = pl.multiple_of(s * rows, rows)
            z = fl_ref[pl.ds(r0, rows), :] + fb_ref[...]
            c = jnp.minimum(z, 0.0) - _log1p(jnp.exp(-jnp.abs(z)))
            for k in (1, 2, 4):
                c = c + jnp.where(sub >= k, pltpu.roll(c, k, 0), 0.0)
            for u in range(FGATE_GROUP):
                cu = c[u * SLAB:(u + 1) * SLAB] + carry
                f_ref[pl.ds(r0 + u * SLAB, SLAB), :] = cu
                carry = cu[SLAB - 1:SLAB, :]
            return carry

        carry_ref[0:1, :] = lax.fori_loop(0, tm // rows, group, carry_ref[0:1, :])

    return pl.pallas_call(
        body, name="fgate_fwd", grid=(T // tm,),
        in_specs=[pl.BlockSpec((tm, DH), lambda i: (i, 0)), pl.BlockSpec((1, DH), lambda i: (0, 0))],
        out_specs=pl.BlockSpec((tm, DH), lambda i: (i, 0)),
        out_shape=jax.ShapeDtypeStruct((T, DH), F32),
        scratch_shapes=[pltpu.VMEM((SLAB, DH), F32)],
        compiler_params=pltpu.CompilerParams(dimension_semantics=("arbitrary",)),
    )(fl, fb)


def _fgate_bwd(dF, fl, fb, T, tm=512):
    tm = min(tm, T)
    nt = T // tm

    def body(df_ref, fl_ref, fb_ref, o_ref, acc_ref, carry_ref):
        i = pl.program_id(0)

        @pl.when(i == 0)
        def _():
            carry_ref[...] = jnp.zeros_like(carry_ref)
            acc_ref[...] = jnp.zeros_like(acc_ref)

        rows = FGATE_GROUP * SLAB
        sub = lax.broadcasted_iota(jnp.int32, (rows, DH), 0) % SLAB

        def group(n, carry):
            g_next, acc = carry
            r0 = pl.multiple_of((tm // rows - 1 - n) * rows, rows)
            c = df_ref[pl.ds(r0, rows), :]
            for k in (1, 2, 4):
                c = c + jnp.where(sub < SLAB - k, pltpu.roll(c, rows - k, 0), 0.0)
            sg = _sig(-(fl_ref[pl.ds(r0, rows), :] + fb_ref[...]))
            for u in reversed(range(FGATE_GROUP)):
                cu = c[u * SLAB:(u + 1) * SLAB] + g_next
                dfl = cu * sg[u * SLAB:(u + 1) * SLAB]
                o_ref[pl.ds(r0 + u * SLAB, SLAB), :] = dfl.astype(o_ref.dtype)
                acc = acc + _colsum(dfl)
                g_next = cu[0:1, :]
            return g_next, acc

        g, acc = lax.fori_loop(0, tm // rows, group, (carry_ref[0:1, :], jnp.zeros((1, DH), F32)))
        carry_ref[0:1, :] = g
        acc_ref[...] += acc

    return pl.pallas_call(
        body, name="fgate_bwd", grid=(nt,),
        in_specs=[pl.BlockSpec((tm, DH), lambda i: (nt - 1 - i, 0)), pl.BlockSpec((tm, DH), lambda i: (nt - 1 - i, 0)),
                  pl.BlockSpec((1, DH), lambda i: (0, 0))],
        out_specs=[pl.BlockSpec((tm, DH), lambda i: (nt - 1 - i, 0)), pl.BlockSpec((1, DH), lambda i: (0, 0))],
        out_shape=[jax.ShapeDtypeStruct((T, DH), BF), jax.ShapeDtypeStruct((1, DH), F32)],
        scratch_shapes=[pltpu.VMEM((SLAB, DH), F32)],
        compiler_params=pltpu.CompilerParams(dimension_semantics=("arbitrary",)),
    )(dF, fl, fb)


def _shift_down(x, d, prev8):
    n = x.shape[0]
    row8 = lax.broadcasted_iota(jnp.int32, (SLAB, x.shape[1]), 0)
    y = pltpu.roll(x, d, 0)
    top = jnp.where(row8 < d, pltpu.roll(prev8, d, 0), y[0:SLAB])
    return jnp.concatenate([top, y[SLAB:]], axis=0)


def _shift_up(x, d, next8):
    n = x.shape[0]
    row8 = lax.broadcasted_iota(jnp.int32, (SLAB, x.shape[1]), 0)
    y = pltpu.roll(x, n - d, 0)
    bottom = jnp.where(row8 >= SLAB - d, pltpu.roll(next8, SLAB - d, 0), y[n - SLAB:])
    return jnp.concatenate([y[:n - SLAB], bottom], axis=0)


def _conv(x, prev8, cw, cb):
    xs = [x] + [_shift_down(x, d, prev8) for d in (1, 2, 3)]
    xa = cb + cw[3:4, :] * xs[0] + cw[2:3, :] * xs[1] + cw[1:2, :] * xs[2] + cw[0:1, :] * xs[3]
    return xa, xs


def _lru_gates(xa_g, wa_g, wx_g, ba_g, bx_g, sp_g):
    xb = xa_g.astype(BF)
    r = _sig(_dot(xb, wa_g) + ba_g)
    ig = _sig(_dot(xb, wx_g) + bx_g)
    la = -LRU_C * r * sp_g
    a = jnp.exp(la)
    mult = jnp.sqrt(_one_minus_sq(a, la))
    return r, ig, a, mult


def _lru_fwd(xg, cw, vec, wabd, wxbd, T, tm=256):
    tm = min(tm, T)
    nsl = tm // SLAB

    def body(x_ref, xp_ref, cw_ref, vec_ref, wa_ref, wx_ref, h_ref, a_s, b_s, carry_ref):
        i = pl.program_id(0)

        @pl.when(i == 0)
        def _():
            carry_ref[...] = jnp.zeros_like(carry_ref)

        x = x_ref[...].astype(F32)
        prev8 = jnp.where(i > 0, xp_ref[SLAB:, :].astype(F32), 0.0)
        vec_v = vec_ref[...]
        xa, _ = _conv(x, prev8, cw_ref[...], vec_v[0:1, :])
        sp = _softplus(-vec_v[3:4, :])
        for g in range(NBD):
            sl = slice(g * BD, (g + 1) * BD)
            _, ig, a, mult = _lru_gates(xa[:, sl], wa_ref[g], wx_ref[g], vec_v[1:2, sl], vec_v[2:3, sl], sp[:, sl])
            a_s[:, sl] = a
            b_s[:, sl] = mult * ig * xa[:, sl]

        def slab(s, carry):
            r0 = pl.multiple_of(s * SLAB, SLAB)
            A, B = _slab_scan_fwd(a_s[pl.ds(r0, SLAB), :], b_s[pl.ds(r0, SLAB), :])
            h = A * carry + B
            h_ref[pl.ds(r0, SLAB), :] = h
            return h[SLAB - 1:SLAB, :]

        carry_ref[0:1, :] = lax.fori_loop(0, nsl, slab, carry_ref[0:1, :])

    blk = 5 * _nbytes((tm, D), F32) + 2 * _nbytes((NBD, BD, BD), BF)
    return pl.pallas_call(
        body, name="lru_fwd", grid=(T // tm,),
        in_specs=[pl.BlockSpec((tm, D), lambda i: (i, 0)),
                  pl.BlockSpec((2 * SLAB, D), lambda i: (jnp.maximum(i * (nsl // 2) - 1, 0), 0)),
                  pl.BlockSpec((CONV, D), lambda i: (0, 0)),
                  pl.BlockSpec((SLAB, D), lambda i: (0, 0)),
                  pl.BlockSpec((NBD, BD, BD), lambda i: (0, 0, 0)),
                  pl.BlockSpec((NBD, BD, BD), lambda i: (0, 0, 0))],
        out_specs=pl.BlockSpec((tm, D), lambda i: (i, 0)),
        out_shape=jax.ShapeDtypeStruct((T, D), F32),
        scratch_shapes=[pltpu.VMEM((tm, D), F32), pltpu.VMEM((tm, D), F32), pltpu.VMEM((SLAB, D), F32)],
        compiler_params=pltpu.CompilerParams(dimension_semantics=("arbitrary",), vmem_limit_bytes=_vmem_limit(blk)),
    )(xg, xg, cw, vec, wabd, wxbd)


def _lru_bwd(xg, h, dha, cw, vec, wabd, wxbd, T, tm=256):
    tm = min(tm, T)
    nsl = tm // SLAB
    nt = T // tm

    def body(x_ref, xp_ref, h_ref, hp_ref, dh_ref, cw_ref, vec_ref, wa_ref, wx_ref,
             dx_ref, dwa_ref, dwx_ref, acc_ref, a_s, b_s, g_s, dxa_s, carry_ref, dxan_ref):
        n = pl.program_id(0)
        it = nt - 1 - n

        @pl.when(n == 0)
        def _():
            carry_ref[...] = jnp.zeros_like(carry_ref)
            dxan_ref[...] = jnp.zeros_like(dxan_ref)
            dwa_ref[...] = jnp.zeros_like(dwa_ref)
            dwx_ref[...] = jnp.zeros_like(dwx_ref)
            acc_ref[...] = jnp.zeros_like(acc_ref)

        x = x_ref[...].astype(F32)
        prev8 = jnp.where(it > 0, xp_ref[SLAB:, :].astype(F32), 0.0)
        hprev8 = jnp.where(it > 0, hp_ref[...], 0.0)
        vec_v = vec_ref[...]
        cw_v = cw_ref[...]
        xa, xs = _conv(x, prev8, cw_v, vec_v[0:1, :])
        sp = _softplus(-vec_v[3:4, :])
        gates = []
        for g in range(NBD):
            sl = slice(g * BD, (g + 1) * BD)
            r, ig, a, mult = _lru_gates(xa[:, sl], wa_ref[g], wx_ref[g], vec_v[1:2, sl], vec_v[2:3, sl], sp[:, sl])
            gates.append((r, ig, a, mult))
            a_s[:, sl] = a
        a_s[...] = _shift_up(a_s[...], 1, carry_ref[...])
        b_s[...] = dh_ref[...]

        def slab(m, carry):
            r0 = pl.multiple_of((nsl - 1 - m) * SLAB, SLAB)
            A, B = _slab_scan_bwd(a_s[pl.ds(r0, SLAB), :], b_s[pl.ds(r0, SLAB), :])
            gg = A * carry + B
            g_s[pl.ds(r0, SLAB), :] = gg
            return gg[0:1, :]

        g_first = lax.fori_loop(0, nsl, slab, carry_ref[1:2, :])
        gt = g_s[...]
        h_prev = _shift_down(h_ref[...], 1, hprev8)
        dba = []
        dbx = []
        dsp = []
        for g in range(NBD):
            sl = slice(g * BD, (g + 1) * BD)
            r, ig, a, mult = gates[g]
            xa_g = xa[:, sl]
            g_g = gt[:, sl]
            da = g_g * h_prev[:, sl]
            dmult = g_g * ig * xa_g
            di = g_g * mult * xa_g
            dxa_g = g_g * mult * ig
            dla = da * a - dmult * (a * a / mult)
            dr = dla * (-LRU_C) * sp[:, sl]
            dsp.append(_colsum(dla * (-LRU_C) * r))
            dra = (dr * r * (1.0 - r))
            dix = (di * ig * (1.0 - ig))
            dba.append(_colsum(dra))
            dbx.append(_colsum(dix))
            dra_b = dra.astype(BF)
            dix_b = dix.astype(BF)
            xb = xa_g.astype(BF)
            dxa_g = dxa_g + _dot(dra_b, wa_ref[g], "nt") + _dot(dix_b, wx_ref[g], "nt")
            dwa_ref[g] += _dot(xb, dra_b, "tn")
            dwx_ref[g] += _dot(xb, dix_b, "tn")
            dxa_s[:, sl] = dxa_g
        dxa = dxa_s[...]
        nxt = dxan_ref[...]
        dx = (cw_v[3:4, :] * dxa + cw_v[2:3, :] * _shift_up(dxa, 1, nxt)
              + cw_v[1:2, :] * _shift_up(dxa, 2, nxt) + cw_v[0:1, :] * _shift_up(dxa, 3, nxt))
        dx_ref[...] = dx.astype(dx_ref.dtype)
        acc_ref[0:1, :] += jnp.concatenate(dba, axis=1)
        acc_ref[1:2, :] += jnp.concatenate(dbx, axis=1)
        acc_ref[2:3, :] += jnp.concatenate(dsp, axis=1)
        acc_ref[3:4, :] += _colsum(dxa)
        for k in range(CONV):
            acc_ref[4 + k:5 + k, :] += _colsum(dxa * xs[CONV - 1 - k])
        dxan_ref[...] = dxa[0:SLAB, :]
        a_first = jnp.concatenate([gates[g][2][0:1, :] for g in range(NBD)], axis=1)
        carry_ref[0:1, :] = a_first
        carry_ref[1:2, :] = g_first

        @pl.when(n == nt - 1)
        def _():
            acc_ref[2:3, :] = acc_ref[2:3, :] * (-_sig(-vec_v[3:4, :]))

    rowblk = lambda i: (nt - 1 - i, 0)
    prevblk = lambda i: (jnp.maximum((nt - 1 - i) * nsl - 1, 0), 0)
    c2 = lambda i: (0, 0)
    c3 = lambda i: (0, 0, 0)
    blk = 12 * _nbytes((tm, D), F32) + 6 * _nbytes((NBD, BD, BD), F32)
    return pl.pallas_call(
        body, name="lru_bwd", grid=(nt,),
        in_specs=[pl.BlockSpec((tm, D), rowblk),
                  pl.BlockSpec((2 * SLAB, D), lambda i: (jnp.maximum((nt - 1 - i) * (nsl // 2) - 1, 0), 0)),
                  pl.BlockSpec((tm, D), rowblk), pl.BlockSpec((SLAB, D), prevblk),
                  pl.BlockSpec((tm, D), rowblk),
                  pl.BlockSpec((CONV, D), c2), pl.BlockSpec((SLAB, D), c2),
                  pl.BlockSpec((NBD, BD, BD), c3), pl.BlockSpec((NBD, BD, BD), c3)],
        out_specs=[pl.BlockSpec((tm, D), rowblk), pl.BlockSpec((NBD, BD, BD), c3), pl.BlockSpec((NBD, BD, BD), c3),
                   pl.BlockSpec((16, D), c2)],
        out_shape=[jax.ShapeDtypeStruct((T, D), BF), jax.ShapeDtypeStruct((NBD, BD, BD), F32),
                   jax.ShapeDtypeStruct((NBD, BD, BD), F32), jax.ShapeDtypeStruct((16, D), F32)],
        scratch_shapes=[pltpu.VMEM((tm, D), F32), pltpu.VMEM((tm, D), F32), pltpu.VMEM((tm, D), F32),
                        pltpu.VMEM((tm, D), F32), pltpu.VMEM((SLAB, D), F32), pltpu.VMEM((SLAB, D), F32)],
        compiler_params=pltpu.CompilerParams(dimension_semantics=("arbitrary",), vmem_limit_bytes=_vmem_limit(blk)),
    )(xg, xg, h, h, dha, cw, vec, wabd, wxbd)


_SCALE = 1.0 / math.sqrt(DH)


_ANY = pl.BlockSpec(memory_space=pl.ANY)


class _Side:
    def __init__(self, srcs, outs, nsem, copies):
        self.srcs, self.outs, self.nsem, self.copies = list(srcs), list(outs), nsem, copies


def _pallas(body, operands, *, name, grid, in_specs, out_specs, out_shape, scratch_shapes=(), semantics,
            vmem=None, side=None):
    if side is None:
        return pl.pallas_call(
            body, name=name, grid=grid, in_specs=in_specs, out_specs=out_specs, out_shape=out_shape,
            scratch_shapes=list(scratch_shapes),
            compiler_params=pltpu.CompilerParams(dimension_semantics=semantics, vmem_limit_bytes=vmem),
        )(*operands)
    n_in, n_out, n_scr = len(in_specs), len(out_specs), len(scratch_shapes)
    ns, no = len(side.srcs), len(side.outs)

    def hosted(*refs):
        ins, refs = refs[:n_in], refs[n_in:]
        sin, refs = refs[:ns], refs[ns:]
        outs, refs = refs[:n_out], refs[n_out:]
        sout, refs = refs[:no], refs[no:]
        scr, (send, recv) = refs[:n_scr], refs[n_scr:]
        ids = [pl.program_id(a) for a in range(len(grid))]
        first = functools.reduce(jnp.logical_and, [i == 0 for i in ids])
        last = functools.reduce(jnp.logical_and, [i == g - 1 for i, g in zip(ids, grid)])

        @pl.when(first)
        def _():
            for cp in side.copies(sin, sout, send, recv):
                cp.start()

        body(*ins, *outs, *scr)

        @pl.when(last)
        def _():
            for cp in side.copies(sin, sout, send, recv):
                cp.wait()

    return pl.pallas_call(
        hosted, name=name, grid=grid, in_specs=list(in_specs) + [_ANY] * ns, out_specs=list(out_specs) + [_ANY] * no,
        out_shape=list(out_shape) + side.outs,
        scratch_shapes=list(scratch_shapes) + [pltpu.SemaphoreType.DMA((side.nsem,)), pltpu.SemaphoreType.DMA((side.nsem,))],
        compiler_params=pltpu.CompilerParams(dimension_semantics=("arbitrary",) * len(grid), vmem_limit_bytes=vmem),
    )(*operands, *side.srcs)


DA = 2 * DH
_LOG2E = math.log2(math.e)
_C2 = _SCALE * _LOG2E


def _aug_fn(ins, cs):
    q, k, fcum = ins
    g_all = fcum * _LOG2E
    lane = lax.broadcasted_iota(jnp.int32, (q.shape[0], DH), 1)
    qa, ka = [], []
    for hd in range(NH):
        g = g_all[:, hd:hd + 1]
        hi = g.astype(BF).astype(F32)
        mid = (g - hi).astype(BF).astype(F32)
        lo = ((g - hi) - mid).astype(BF).astype(F32)
        qx = jnp.where(lane == 0, hi, jnp.where(lane == 1, mid, jnp.where(lane == 2, lo,
                                                                          jnp.where(lane < 6, 1.0, 0.0))))
        kx = jnp.where(lane < 3, 1.0, jnp.where(lane == 3, -hi, jnp.where(lane == 4, -mid,
                                                                          jnp.where(lane == 5, -lo, 0.0))))
        qa += [q[:, hd * DH:(hd + 1) * DH], qx.astype(BF)]
        ka += [k[:, hd * DH:(hd + 1) * DH], kx.astype(BF)]
    return [jnp.concatenate(qa, axis=1), jnp.concatenate(ka, axis=1)], []


_KA_ONES = DH + 3
KT_ONES = 16
FWD_KEY_PARTS = 2
BWD_QUERY_PARTS = 2


def _attn_fwd(qa, ka, qkv, T, blk=512, side=None):
    blk = min(blk, T)
    nb = T // blk
    half = blk // 2

    def body(q_ref, k_ref, vn_ref, o_ref, lse_ref, v_ref):
        i = pl.program_id(1)

        @pl.when(i == 0)
        def _():
            for jj in range(nb):
                v_ref[jj] = jnp.concatenate([vn_ref[jj * blk:(jj + 1) * blk, :].astype(F32).T.astype(BF),
                                             jnp.ones((KT_ONES, blk), BF)], axis=0)

        q = q_ref[...]

        def scores(j):
            r0 = pl.multiple_of(j * blk, blk)
            return _dot(k_ref[pl.ds(r0, blk), :], q, "nt")

        def update(s, vj, carry):
            m, acc = carry
            m_new = jnp.maximum(m, jnp.max(s, axis=0, keepdims=True))
            alpha = jnp.exp2(m - m_new)
            acc = alpha * acc + _dot(vj, jnp.exp2(s - m_new).astype(BF))
            return m_new, acc

        def step(j, st):
            r0 = pl.multiple_of(j * blk, blk)
            part = blk // FWD_KEY_PARTS
            ss = [_dot(k_ref[pl.ds(r0 + n * part, part), :], q, "nt") for n in range(FWD_KEY_PARTS)]
            vj = v_ref[j]
            for n in range(FWD_KEY_PARTS):
                st = update(ss[n], vj[:, n * part:(n + 1) * part], st)
            return st

        init = (jnp.full((1, blk), -jnp.inf, F32), jnp.zeros((DH + KT_ONES, blk), F32))
        carry = lax.fori_loop(0, i, step, init)
        last = scores(i)
        vi = v_ref[i]
        rk = lax.broadcasted_iota(jnp.int32, (half, blk), 0)
        cq = lax.broadcasted_iota(jnp.int32, (half, blk), 1)
        m, acc = update(jnp.where(cq >= rk, last[:half], -jnp.inf), vi[:, :half], carry)
        s2 = jnp.where(cq[:, :half] >= rk[:, :half], last[half:, half:], -jnp.inf)
        m2, acc2 = update(s2, vi[:, half:], (m[:, half:], acc[:, half:]))
        m = jnp.concatenate([m[:, :half], m2], axis=1)
        acc = jnp.concatenate([acc[:, :half], acc2], axis=1)
        l = acc[DH:DH + 1]
        o_ref[...] = (acc[:DH] / l).T.astype(o_ref.dtype)
        lse_ref[...] = m + jnp.log(l) * _LOG2E

    vm = _nbytes((T, DA), BF) + 2 * _nbytes((T, DH), BF) + 6 * _nbytes((blk, blk), F32)
    return _pallas(
        body, (qa, ka, qkv), name="attn_fwd", grid=(NH, nb),
        in_specs=[pl.BlockSpec((blk, DA), lambda h, i: (i, h)),
                  pl.BlockSpec((T, DA), lambda h, i: (0, h)),
                  pl.BlockSpec((T, DH), lambda h, i: (0, 2 * NH + h))],
        out_specs=[pl.BlockSpec((blk, DH), lambda h, i: (i, h)),
                   pl.BlockSpec((None, None, 1, blk), lambda h, i: (h, i, 0, 0))],
        out_shape=[jax.ShapeDtypeStruct((T, D), BF), jax.ShapeDtypeStruct((NH, nb, 1, blk), F32)],
        scratch_shapes=[pltpu.VMEM((nb, DH + KT_ONES, blk), BF)],
        semantics=("parallel", "arbitrary"), vmem=_vmem_limit(vm), side=side)


def _attn_bwd(qa, ka, qkv, do, lrow, drow, T, blk=512, side=None):
    blk = min(blk, T)
    nb = T // blk

    def body(ka_ref, v_ref, qa_ref, do_ref, l_ref, d_ref, dq_ref, dk_ref, dv_ref, dfs_ref, dft_ref, dq_s):
        j = pl.program_id(1)

        @pl.when(j == 0)
        def _():
            dq_s[...] = jnp.zeros_like(dq_s)

        row = lax.broadcasted_iota(jnp.int32, (DH + KT_ONES, blk), 0)
        dq_scale = jnp.where(row < DH, _SCALE, 1.0)

        kaj = ka_ref[...]
        ktj = jnp.concatenate([kaj[:, :DH].astype(F32).T.astype(BF), jnp.ones((KT_ONES, blk), BF)], axis=0)
        vj = v_ref[...]

        def step(i, carry):
            dka, dv = carry
            r0 = pl.multiple_of(i * blk, blk)
            part = blk // BWD_QUERY_PARTS
            lr, dr = l_ref[i], d_ref[i]
            loaded = []
            for n in range(BWD_QUERY_PARTS):
                qn = qa_ref[pl.ds(r0 + n * part, part), :]
                don = do_ref[pl.ds(r0 + n * part, part), :]
                loaded.append((qn, don, _dot(kaj, qn, "nt"), _dot(vj, don, "nt")))
            dq_parts = []
            for n, (qn, don, raw, dpt) in enumerate(loaded):
                pt = jnp.exp2(raw - lr[:, n * part:(n + 1) * part])
                dsb = (pt * (dpt - dr[:, n * part:(n + 1) * part])).astype(BF)
                dv = dv + _dot(pt.astype(BF), don)
                dka = dka + _dot(dsb, qn)
                dq_parts.append(_dot(ktj, dsb))
            dq_s[i] += jnp.concatenate(dq_parts, axis=1) * dq_scale
            return dka, dv

        def diagonal():
            half = blk // 2
            r0 = pl.multiple_of(j * blk, blk)
            qi = qa_ref[pl.ds(r0, blk), :]
            doi = do_ref[pl.ds(r0, blk), :]
            lr, dr = l_ref[j], d_ref[j]
            rk = lax.broadcasted_iota(jnp.int32, (half, blk), 0)
            cq = lax.broadcasted_iota(jnp.int32, (half, blk), 1)

            def quarter(ka, v, q, do, l2, d2, keep):
                pt = jnp.exp2(jnp.where(keep, _dot(ka, q, "nt") - l2, -jnp.inf))
                dsb = (pt * (_dot(v, do, "nt") - d2)).astype(BF)
                return _dot(dsb, q), _dot(pt.astype(BF), do), dsb

            dka1, dv1, ds1 = quarter(kaj[:half], vj[:half], qi, doi, lr, dr, cq >= rk)
            dka2, dv2, ds2 = quarter(kaj[half:], vj[half:], qi[half:], doi[half:], lr[:, half:], dr[:, half:],
                                     cq[:, :half] >= rk[:, :half])
            dq2 = jnp.concatenate([jnp.zeros((DH + KT_ONES, half), F32), _dot(ktj[:, half:], ds2)], axis=1)
            dq_s[j] += (_dot(ktj[:, :half], ds1) + dq2) * dq_scale
            return jnp.concatenate([dka1, dka2], axis=0), jnp.concatenate([dv1, dv2], axis=0)

        dka, dv = lax.fori_loop(j + 1, nb, step, diagonal())
        dk_ref[...] = (dka[:, :DH] * (1.0 / _LOG2E)).astype(dk_ref.dtype)
        dv_ref[...] = dv.astype(dv_ref.dtype)
        dfs_ref[...] = dka[:, DH:].T[_KA_ONES - DH:_KA_ONES - DH + 1, :]

        @pl.when(j == nb - 1)
        def _():
            for ii in range(nb):
                t = dq_s[ii]
                dq_ref[ii * blk:(ii + 1) * blk, :] = t[:DH].T.astype(dq_ref.dtype)
                dft_ref[ii] = t[DH:DH + 1]

    rowv = pl.BlockSpec((None, nb, 1, blk), lambda h, j: (h, 0, 0, 0))
    vm = (_nbytes((T, DA), BF) + 2 * _nbytes((T, DH), BF) + _nbytes((T, DH + KT_ONES), F32)
          + 8 * _nbytes((blk, blk), F32))
    return _pallas(
        body, (ka, qkv, qa, do, lrow, drow), name="attn_bwd", grid=(NH, nb),
        in_specs=[pl.BlockSpec((blk, DA), lambda h, j: (j, h)),
                  pl.BlockSpec((blk, DH), lambda h, j: (j, 2 * NH + h)),
                  pl.BlockSpec((T, DA), lambda h, j: (0, h)),
                  pl.BlockSpec((T, DH), lambda h, j: (0, h)),
                  rowv, rowv],
        out_specs=[pl.BlockSpec((T, DH), lambda h, j: (0, h)),
                   pl.BlockSpec((blk, DH), lambda h, j: (j, h)),
                   pl.BlockSpec((blk, DH), lambda h, j: (j, h)),
                   pl.BlockSpec((None, None, 1, blk), lambda h, j: (h, j, 0, 0)), rowv],
        out_shape=[jax.ShapeDtypeStruct((T, D), BF), jax.ShapeDtypeStruct((T, D), BF),
                   jax.ShapeDtypeStruct((T, D), BF), jax.ShapeDtypeStruct((NH, nb, 1, blk), F32),
                   jax.ShapeDtypeStruct((NH, nb, 1, blk), F32)],
        scratch_shapes=[pltpu.VMEM((nb, DH + KT_ONES, blk), F32)],
        semantics=("parallel", "arbitrary"), vmem=_vmem_limit(vm), side=side)


def _norm_fn(ins, cs):
    x, = ins
    g, = cs
    r = lax.rsqrt(jnp.mean(x * x, axis=-1, keepdims=True) + EPS)
    return [x * r * g], []


def _norm_bwd_fn(ins, cs):
    x, dy, dres = ins
    g, = cs
    r = lax.rsqrt(jnp.mean(x * x, axis=-1, keepdims=True) + EPS)
    xh = x * r
    dxh = dy * g
    dx = dres + r * (dxh - xh * jnp.mean(dxh * xh, axis=-1, keepdims=True))
    return [dx, dx], [_colsum(dy * xh)]


def _final_fn(ins, cs):
    x2, tgt = ins
    g, = cs
    r = lax.rsqrt(jnp.mean(x2 * x2, axis=-1, keepdims=True) + EPS)
    xh = x2 * r
    e = xh * g - tgt
    dy = e * (1.0 / D)
    dxh = dy * g
    dx2 = r * (dxh - xh * jnp.mean(dxh * xh, axis=-1, keepdims=True))
    return [dx2, dx2], [_colsum(0.5 * e * e * (1.0 / D)), _colsum(dy * xh)]


def _z_fn(ins, cs):
    g, h = [v.astype(F32) for v in ins]
    return [_gelu(g) * h], []


def _mix_fn(ins, cs):
    gates, ya, yb = [v.astype(F32) for v in ins]
    return [_sig(gates[:, :D]) * ya + _sig(gates[:, D:]) * yb], []


def _mix_bwd_fn(ins, cs):
    dmix, gates, ya, yb = [v.astype(F32) for v in ins]
    ga = _sig(gates[:, :D])
    gb = _sig(gates[:, D:])
    dgates = jnp.concatenate([dmix * ya * ga * (1.0 - ga), dmix * yb * gb * (1.0 - gb)], axis=1)
    return [dmix * ga, dmix * gb, dgates], []


def _z_bwd_fn(ins, cs):
    dz, g, h = [v.astype(F32) for v in ins]
    return [dz * _gelu(g), dz * h * _gelu_grad(g)], []


def _delta_fn(ins, cs):
    do, o = ins
    p = do.astype(F32) * o.astype(F32)
    lane = lax.broadcasted_iota(jnp.int32, (p.shape[0], DH), 1)
    out = jnp.zeros((p.shape[0], DH), F32)
    for hd in range(NH):
        s = jnp.sum(p[:, hd * DH:(hd + 1) * DH], axis=1, keepdims=True)
        out = jnp.where(lane == hd, s, out)
    return [out], []


def _du_all(pieces, win, T, tm=256, side=None):
    tm = min(tm, T)
    n = len(pieces)

    def body(*refs):
        w_ref, o_ref = refs[n], refs[n + 1]
        acc = None
        for (a, off), a_ref in zip(pieces, refs[:n]):
            d = _dot(a_ref[...].astype(BF), w_ref[:, off:off + a.shape[1]], "nt")
            acc = d if acc is None else acc + d
        o_ref[...] = acc

    vm = (sum(_nbytes((tm, a.shape[1]), a.dtype) for a, _ in pieces) + _nbytes(win.shape, win.dtype)
          + 2 * _nbytes((tm, D), F32))
    return _pallas(
        body, tuple(a for a, _ in pieces) + (win,), name="du_all", grid=(T // tm,),
        in_specs=[pl.BlockSpec((tm, a.shape[1]), lambda i: (i, 0)) for a, _ in pieces]
        + [pl.BlockSpec(win.shape, lambda i: (0, 0))],
        out_specs=[pl.BlockSpec((tm, D), lambda i: (i, 0))],
        out_shape=[jax.ShapeDtypeStruct((T, D), F32)],
        semantics=("arbitrary",), vmem=int(min(VMEM_CAP, 2 * vm + (4 << 20))), side=side)


def _local_step(x, tgt, w, T, blk=1024, dist=None):
    blk = min(blk, T)
    nb = T // blk
    win = w["win"]

    u, = _ew(_norm_fn, T, 1024, [(x, D, 0)], [w["g_mix"]], [(D, BF)], [], name="norm_mix")
    xg = _mm(u, win, "nn", T, 2 * D, D, name="proj_lru", out_dtype=BF)
    qkv = _mm(u, win, "nn", T, 3 * D, D, name="proj_qkv", out_dtype=BF, b_off=(0, 2),
              epi=lambda acc: acc * jnp.where(pl.program_id(1) == 0, _C2, 1.0))
    gates = _mm(u, win, "nn", T, 2 * D, D, name="proj_gates", b_off=(0, 5), out_dtype=BF)
    fl = _mm(u, win, "nn", T, DH, D, name="proj_f", tn=DH, b_off=(0, 7 * D // DH))
    fcum = _fgate_fwd(fl, w["fb"], T)
    qa, ka = _ew(_aug_fn, T, 512, [(qkv, D, 0), (qkv, D, 1), (fcum, DH, 0)], [],
                 [(NH * DA, BF), (NH * DA, BF)], [], name="attn_augment")

    h = _lru_fwd(xg, w["cw"], w["vec"], w["wabd"], w["wxbd"], T)
    ob, lse, *landed = _attn_fwd(qa, ka, qkv, T, blk, side=dist.weights_side() if dist else None)
    if dist:
        w = dict(w, **dist.weights_landed(landed))
    z, = _ew(_z_fn, T, 1024, [(xg, D, 1), (h, D, 0)], [], [(D, BF)], [], name="lru_gelu")
    ya = _mm(z, w["wa"], "nn", T, D, D, name="branch_a", out_dtype=BF)
    yb = _mm(ob, w["wb"], "nn", T, D, D, name="branch_b", out_dtype=BF)
    mix, = _ew(_mix_fn, T, 512, [(gates, 2 * D, 0), (ya, D, 0), (yb, D, 0)], [], [(D, BF)], [], name="mix")
    x1 = _mm(mix, w["wout"], "nn", T, D, D, name="out_proj", add=x)
    m, = _ew(_norm_fn, T, 1024, [(x1, D, 0)], [w["g_mlp"]], [(D, BF)], [], name="norm_mlp")
    hh = _mm(m, w["wup"], "nn", T, FF, D, name="mlp_up", out_dtype=BF,
             epi=lambda acc: jnp.square(jnp.maximum(acc, 0.0)))
    x2 = _mm(hh, w["wdown"], "nn", T, D, FF, name="mlp_down", add=x1, tk=FF)
    dx2, dx2b, loss_vec, dg_fin = _ew(_final_fn, T, 512, [(x2, D, 0), (tgt, D, 0)], [w["g_fin"]],
                                      [(D, F32), (D, BF)], [(1, D), (1, D)], name="final_norm_loss")

    dhpre = _mm(dx2b, w["wdown"], "nt", T, FF, D, name="mlp_down_bwd", out_dtype=BF,
                epi=lambda acc, h2: acc * (2.0 * jnp.sqrt(h2.astype(F32))), epi_ins=[hh])
    dwdown = _mm(hh, dx2b, "tn", FF, D, T, name="dw_down", out_dtype=BF)
    dwup = _mm(m, dhpre, "tn", D, FF, T, name="dw_up", out_dtype=BF)
    dm = _mm(dhpre, w["wup"], "nt", T, D, FF, name="mlp_up_bwd", tk=FF)
    dx1, dx1b, dg_mlp = _ew(_norm_bwd_fn, T, 512, [(x1, D, 0), (dm, D, 0), (dx2, D, 0)], [w["g_mlp"]],
                            [(D, F32), (D, BF)], [(1, D)], name="norm_mlp_bwd")

    dmix = _mm(dx1b, w["wout"], "nt", T, D, D, name="out_proj_bwd", out_dtype=BF)
    dwout = _mm(mix, dx1b, "tn", D, D, T, name="dw_out", out_dtype=BF)
    dya, dyb, dgates = _ew(_mix_bwd_fn, T, 512, [(dmix, D, 0), (gates, 2 * D, 0), (ya, D, 0), (yb, D, 0)], [],
                           [(D, BF), (D, BF), (2 * D, BF)], [], name="mix_bwd")
    dob = _mm(dyb, w["wb"], "nt", T, D, D, name="branch_b_bwd", out_dtype=BF)
    dwb = _mm(ob, dyb, "tn", D, D, T, name="dw_b", out_dtype=BF)
    dz = _mm(dya, w["wa"], "nt", T, D, D, name="branch_a_bwd", out_dtype=BF)
    dwa = _mm(z, dya, "tn", D, D, T, name="dw_a", out_dtype=BF)
    dha, dglru = _ew(_z_bwd_fn, T, 512, [(dz, D, 0), (xg, D, 1), (h, D, 0)], [], [(D, F32), (D, BF)], [],
                     name="lru_gelu_bwd")

    delta, = _ew(_delta_fn, T, 1024, [(dob, D, 0), (ob, D, 0)], [], [(DH, F32)], [], name="attn_delta")
    drow = delta[:, :NH].T.reshape(NH, nb, 1, blk)
    big = dict(w_branch_a=dwa, w_branch_b=dwb, w_out=dwout, w_up=dwup, w_down=dwdown)
    side = dist.grads_side(big) if dist else None
    dq, dk, dv, dfs, dft, *landed = _attn_bwd(qa, ka, qkv, dob, lse, drow, T, blk, side=side)
    if dist:
        big = dist.grads_landed(side, landed)
    dfcum = jnp.pad((dft - dfs).reshape(NH, T).T, ((0, 0), (0, DH - NH)))
    dfl, dfb = _fgate_bwd(dfcum, fl, w["fb"], T)

    dxl, dwabd, dwxbd, lacc = _lru_bwd(xg, h, dha, w["cw"], w["vec"], w["wabd"], w["wxbd"], T)

    dproj = ((dxl, 0), (dglru, D), (dq, 2 * D), (dk, 3 * D), (dv, 4 * D), (dgates, 5 * D), (dfl, 7 * D))
    pieces = [_mm(u, p, "tn", D, p.shape[1], T, name="dw_in_%d" % n, out_dtype=BF)
              for n, (p, _) in enumerate(dproj)]
    pieces[-1] = pieces[-1][:, :NH]
    dwin = dict(w_in=jnp.concatenate(pieces, axis=1))
    side = dist.grads_side(dwin) if dist else None
    du, *landed = _du_all(dproj, win, T, side=side)
    big.update(dist.grads_landed(side, landed) if dist else dwin)
    dx, dg_mix = _ew(_norm_bwd_fn, T, 512, [(x, D, 0), (du, D, 0), (dx1, D, 0)], [w["g_mix"]], [(D, F32)],
                     [(1, D)], name="norm_mix_bwd")

    return dict(dx=dx, big=big, dwabd=dwabd, dwxbd=dwxbd, lacc=lacc, dfb=dfb, dg_mix=dg_mix, dg_mlp=dg_mlp,
                dg_fin=dg_fin, loss_vec=loss_vec)


def _block_diag(w):
    per = BD // LRU_BW
    w4 = w.reshape(NBD, per, LRU_BW, LRU_BW)
    on_diagonal = jnp.eye(per, dtype=bool)[None, :, None, :, None]
    return jnp.where(on_diagonal, w4[:, :, :, None, :], 0.0).reshape(NBD, BD, BD)


def _block_diag_extract(wbd):
    per = BD // LRU_BW
    w5 = wbd.reshape(NBD, per, LRU_BW, per, LRU_BW)
    return jnp.stack([w5[:, b, :, b, :] for b in range(per)], axis=1).reshape(LRU_BLOCKS, LRU_BW, LRU_BW)


def _place():
    x, y, c = lax.axis_index("x"), lax.axis_index("y"), lax.axis_index("c")
    chips = [(1 - x, y), (x, 1 - y), (1 - x, 1 - y)]
    return x, y, c, chips


def _allgather_shards(shards):
    n = len(shards)

    def body(*refs):
        ins, outs = refs[:n], refs[n:2 * n]
        send_sems, recv_sems = refs[2 * n:]
        x, y, c, chips = _place()
        me = 2 * x + y
        sibling = (x, y, 1 - c)

        def remote(p, k, src, dst, to):
            return pltpu.make_async_remote_copy(src_ref=src, dst_ref=dst, send_sem=send_sems.at[p, k],
                                                recv_sem=recv_sems.at[p, k], device_id=to, device_id_type=MESH)

        sent = []
        for p in range(n):
            for k, chip in enumerate(chips):
                cp = remote(p, k, ins[p].at[c], outs[p].at[me, c], (chip[0], chip[1], c))
                cp.start()
                sent.append(cp)
        for p in range(n):
            for k, chip in enumerate(chips):
                half = outs[p].at[2 * chip[0] + chip[1], c]
                remote(p, k, half, half, sibling).wait_recv()
                fwd = remote(p, 3 + k, half, half, sibling)
                fwd.start()
                sent.append(fwd)
        for p in range(n):
            for k, chip in enumerate(chips):
                half = outs[p].at[2 * chip[0] + chip[1], 1 - c]
                remote(p, 3 + k, half, half, sibling).wait_recv()
        for cp in sent:
            cp.wait_send()

    gathered = pl.pallas_call(
        body, name="allgather_weights",
        in_specs=[_ANY] * n, out_specs=[_ANY] * n,
        out_shape=[jax.ShapeDtypeStruct((NCHIP,) + s.shape, s.dtype) for s in shards],
        scratch_shapes=[pltpu.SemaphoreType.DMA((n, 6)), pltpu.SemaphoreType.DMA((n, 6))],
    )(*shards)
    me = 2 * lax.axis_index("x") + lax.axis_index("y")
    return [lax.dynamic_update_index_in_dim(g, s, me, 0) for g, s in zip(gathered, shards)]


_LATE =["w_branch_a", "w_branch_b", "w_out", "w_up", "w_down"]
_COLUMN_CUT = ("w_in", "w_up")
N_PEERS = 7


def _shard_major(name, g):
    s = _columns_to_shards(g) if name in _COLUMN_CUT else g.reshape(NCHIP, g.shape[0] // NCHIP, g.shape[1])
    return s.reshape(NCHIP, 2, s.shape[1] // 2, s.shape[2])


class _Exchanges:
    def __init__(self, shards):
        self.shards = shards

    def weights_side(self):
        srcs = [self.shards[n] for n in _LATE]

        def copies(sin, sout, send, recv):
            x, y, c, chips = _place()
            return [pltpu.make_async_remote_copy(
                src_ref=sin[p], dst_ref=sout[p].at[2 * x + y], send_sem=send.at[3 * p + k], recv_sem=recv.at[3 * p + k],
                device_id=(chip[0], chip[1], c), device_id_type=MESH)
                for p in range(len(sin)) for k, chip in enumerate(chips)]

        return _Side(srcs, [jax.ShapeDtypeStruct((NCHIP,) + s.shape, s.dtype) for s in srcs], 3 * len(srcs), copies)

    def weights_landed(self, landed):
        me = 2 * lax.axis_index("x") + lax.axis_index("y")
        full = {n: lax.dynamic_update_index_in_dim(g, self.shards[n], me, 0) for n, g in zip(_LATE, landed)}
        return dict(wa=full["w_branch_a"].reshape(D, D), wb=full["w_branch_b"].reshape(D, D),
                    wout=full["w_out"].reshape(D, D), wup=_shards_to_columns(full["w_up"]),
                    wdown=full["w_down"].reshape(FF, D))

    def grads_side(self, grads):
        side_names = list(grads)
        srcs = [_shard_major(n, grads[n]) for n in side_names]

        def copies(sin, sout, send, recv):
            x, y, c, chips = _place()
            peers = [(x, y, 1 - c)] + [(cx, cy, c) for cx, cy in chips] + [(cx, cy, 1 - c) for cx, cy in chips]
            return [pltpu.make_async_remote_copy(
                src_ref=sin[p].at[2 * px + py, pc], dst_ref=sout[p].at[s], send_sem=send.at[N_PEERS * p + s],
                recv_sem=recv.at[N_PEERS * p + s], device_id=(px, py, pc), device_id_type=MESH)
                for p in range(len(sin)) for s, (px, py, pc) in enumerate(peers)]

        side = _Side(srcs, [jax.ShapeDtypeStruct((N_PEERS,) + s.shape[2:], s.dtype) for s in srcs],
                     N_PEERS * len(srcs), copies)
        side.names = side_names
        return side

    def grads_landed(self, side, landed):
        return {n: (own, got) for n, own, got in zip(side.names, side.srcs, landed)}


def _add8(g, got, me, c, name):
    _, _, half, cols = g.shape
    th = _row_tile(half, 2 * cols)

    def body(me_ref, c_ref, g_ref, r_ref, o_ref):
        acc = g_ref[...].astype(F32)
        for s in range(N_PEERS):
            acc = acc + r_ref[s].astype(F32)
        o_ref[...] = acc

    return pl.pallas_call(
        body, name=name,
        grid_spec=pltpu.PrefetchScalarGridSpec(
            num_scalar_prefetch=2, grid=(half // th,),
            in_specs=[pl.BlockSpec((None, None, th, cols), lambda i, me_ref, c_ref: (me_ref[0], c_ref[0], i, 0)),
                      pl.BlockSpec((N_PEERS, th, cols), lambda i, me_ref, c_ref: (0, i, 0))],
            out_specs=pl.BlockSpec((th, cols), lambda i, me_ref, c_ref: (i, 0))),
        out_shape=jax.ShapeDtypeStruct((half, cols), F32),
    )(me, c, g, got)


def _share_halves(halves):
    n = len(halves)

    def body(*refs):
        ins, outs = refs[:n], refs[n:2 * n]
        send_sems, recv_sems = refs[2 * n:]
        x, y, c, _ = _place()
        sibling = (x, y, 1 - c)
        copies = []
        for p in range(n):
            cp = pltpu.make_async_remote_copy(src_ref=ins[p], dst_ref=outs[p], send_sem=send_sems.at[p],
                                              recv_sem=recv_sems.at[p], device_id=sibling, device_id_type=MESH)
            cp.start()
            copies.append(cp)
        for cp in copies:
            cp.wait()

    return pl.pallas_call(
        body, name="reduce_share_halves",
        in_specs=[_ANY] * n, out_specs=[_ANY] * n,
        out_shape=[jax.ShapeDtypeStruct(h.shape, h.dtype) for h in halves],
        scratch_shapes=[pltpu.SemaphoreType.DMA((n,)), pltpu.SemaphoreType.DMA((n,))],
    )(*halves)


def _row_tile(half, cols):
    th = max(SLAB, min(half, (1 << 18) // cols // SLAB * SLAB))
    while half % th:
        th -= SLAB
    return th


N_DEV = 8
SMALL_ROWS = 208


def _allreduce_small(pack):
    def body(x_ref, out_ref, gbuf, send_sems, recv_sems, local_sem):
        x, y, c, chips = _place()
        me, sibling = (x, y, c), (x, y, 1 - c)

        def rows(px, py, pc):
            return gbuf.at[4 * px + 2 * py + pc]

        def copy(k, block, to, src=None):
            return pltpu.make_async_remote_copy(
                src_ref=rows(*block) if src is None else src, dst_ref=rows(*block),
                send_sem=send_sems.at[k], recv_sem=recv_sems.at[k], device_id=to, device_id_type=MESH)

        mine = pltpu.make_async_copy(x_ref, rows(*me), local_sem)
        mine.start()
        first = [copy(0, me, sibling, src=x_ref)]
        first += [copy(1 + j, me, (chip[0], chip[1], c), src=x_ref) for j, chip in enumerate(chips)]
        for cp in first:
            cp.start()
        passed = [copy(4 + j, (chip[0], chip[1], c), sibling) for j, chip in enumerate(chips)]
        for j, chip in enumerate(chips):
            copy(1 + j, (chip[0], chip[1], c), me).wait_recv()
            passed[j].start()
        copy(0, sibling, me).wait_recv()
        for j, chip in enumerate(chips):
            copy(4 + j, (chip[0], chip[1], 1 - c), me).wait_recv()
        for cp in first + passed:
            cp.wait_send()
        mine.wait()
        acc = gbuf[0]
        for d in range(1, N_DEV):
            acc = acc + gbuf[d]
        out_ref[...] = acc

    return pl.pallas_call(
        body, name="allreduce_small",
        in_specs=[pl.BlockSpec(memory_space=pltpu.VMEM)],
        out_specs=pl.BlockSpec(memory_space=pltpu.VMEM),
        out_shape=jax.ShapeDtypeStruct((SMALL_ROWS, D), F32),
        scratch_shapes=[pltpu.VMEM((N_DEV, SMALL_ROWS, D), F32), pltpu.SemaphoreType.DMA((7,)),
                        pltpu.SemaphoreType.DMA((7,)), pltpu.SemaphoreType.DMA],
    )(pack)


def _adamw(w, g, m, v, name):
    rows, cols = w.shape

    def body(w_ref, g_ref, m_ref, v_ref, d_ref, mo_ref, vo_ref):
        gv = g_ref[...]
        mn = ADAM_B1 * m_ref[...] + (1.0 - ADAM_B1) * gv
        vn = ADAM_B2 * v_ref[...] + (1.0 - ADAM_B2) * (gv * gv)
        m_hat = mn / (1.0 - ADAM_B1 ** ADAM_STEP)
        v_hat = vn / (1.0 - ADAM_B2 ** ADAM_STEP)
        d_ref[...] = -ADAM_LR * (m_hat / (jnp.sqrt(v_hat) + ADAM_EPS) + ADAM_WD * w_ref[...])
        mo_ref[...] = mn
        vo_ref[...] = vn

    if rows % SLAB:
        spec, steps = pl.BlockSpec((rows, DH), lambda i: (0, i)), cols // DH
    else:
        th = _row_tile(rows, cols)
        spec, steps = pl.BlockSpec((th, cols), lambda i: (i, 0)), rows // th
    return pl.pallas_call(
        body, name=name, grid=(steps,),
        in_specs=[spec] * 4, out_specs=[spec] * 3,
        out_shape=[jax.ShapeDtypeStruct((rows, cols), F32)] * 3,
        compiler_params=pltpu.CompilerParams(dimension_semantics=("parallel",)),
    )(w, g, m, v)


_SMALL = ["norm_mix_g", "norm_mlp_g", "norm_final_g", "conv_b", "lru_ba", "lru_bx", "lru_lambda"]
_ROW_FB, _ROW_CW, _ROW_WA, _ROW_WX, _ROW_LOSS = 56, 64, 72, 136, 200


def _pack_small(vals, col0):
    def slab(a):
        return jnp.pad(a, ((0, -a.shape[0] % SLAB), (0, D - a.shape[1])))

    rows = [slab(vals[n].reshape(1, D)) for n in _SMALL]
    rows.append(slab(vals["forget_b"].reshape(1, NH)))
    if vals["conv_w"].shape[1] == D:
        rows.append(slab(vals["conv_w"]))
    else:
        rows.append(slab(lax.dynamic_update_slice(jnp.zeros((CONV, D), F32), vals["conv_w"], (0, col0))))
    rows.append(vals["lru_wa"].reshape(LRU_BLOCKS * LRU_BW * LRU_BW // D, D))
    rows.append(vals["lru_wx"].reshape(LRU_BLOCKS * LRU_BW * LRU_BW // D, D))
    rows.append(slab(vals["loss"]) if "loss" in vals else jnp.zeros((SLAB, D), F32))
    return jnp.concatenate(rows, axis=0)


def _unpack_small(pack, col0):
    out = {n: pack[SLAB * i] for i, n in enumerate(_SMALL)}
    out["forget_b"] = pack[_ROW_FB, :NH]
    out["conv_w"] = lax.dynamic_slice(pack[_ROW_CW:_ROW_CW + CONV], (0, col0), (CONV, D // NCHIP))
    out["lru_wa"] = pack[_ROW_WA:_ROW_WX].reshape(LRU_BLOCKS, LRU_BW, LRU_BW)
    out["lru_wx"] = pack[_ROW_WX:_ROW_LOSS].reshape(LRU_BLOCKS, LRU_BW, LRU_BW)
    return out


_WEIGHTS = ["norm_mix_g", "w_in", "conv_w", "conv_b", "lru_wa", "lru_ba", "lru_wx", "lru_bx", "lru_lambda",
            "forget_b", "w_branch_a", "w_branch_b", "w_out", "norm_mlp_g", "w_up", "w_down", "norm_final_g"]
_BIG = ["w_in", "w_branch_a", "w_branch_b", "w_out", "w_up", "w_down"]


def _halves(a):
    return a.reshape(2, a.shape[0] // 2, a.shape[1])


def _columns_to_shards(a):
    rows, cols = a.shape[0], a.shape[1] // NCHIP
    return jnp.transpose(a.reshape(rows, NCHIP, cols), (1, 0, 2))


def _shards_to_columns(a):
    n, rows, cols = a.shape
    return jnp.transpose(a, (1, 0, 2)).reshape(rows, n * cols)


def kernel(x, norm_mix_g, w_in, conv_w, conv_b, lru_wa, lru_ba, lru_wx, lru_bx, lru_lambda, forget_b, w_branch_a, w_branch_b, w_out, norm_mlp_g, w_up, w_down, norm_final_g, loss_target, m_norm_mix_g, m_w_in, m_conv_w, m_conv_b, m_lru_wa, m_lru_ba, m_lru_wx, m_lru_bx, m_lru_lambda, m_forget_b, m_w_branch_a, m_w_branch_b, m_w_out, m_norm_mlp_g, m_w_up, m_w_down, m_norm_final_g, v_norm_mix_g, v_w_in, v_conv_w, v_conv_b, v_lru_wa, v_lru_ba, v_lru_wx, v_lru_bx, v_lru_lambda, v_forget_b, v_w_branch_a, v_w_branch_b, v_w_out, v_norm_mlp_g, v_w_up, v_w_down, v_norm_final_g):
    args = dict(locals())
    wts = {n: args[n] for n in _WEIGHTS}
    mom = {n: args["m_" + n] for n in _WEIGHTS}
    var = {n: args["v_" + n] for n in _WEIGHTS}
    T = x.shape[1]
    xi, yi, ci = lax.axis_index("x"), lax.axis_index("y"), lax.axis_index("c")
    me = 2 * xi + yi
    c1 = jnp.reshape(ci, (1,)).astype(jnp.int32)
    me1 = jnp.reshape(me, (1,)).astype(jnp.int32)
    col0 = me * (D // NCHIP)

    cw_pad = jnp.pad(conv_w, ((0, 4 * SLAB - CONV), (0, 0)))
    g_in, g_cw = _allgather_shards([_halves(w_in.astype(BF)), _halves(cw_pad)])
    cin = DIN // NCHIP
    win = _shards_to_columns(g_in.reshape(NCHIP, D, cin))
    w = dict(
        win=jnp.pad(win, ((0, 0), (0, DINP - DIN))),
        cw=_shards_to_columns(g_cw.reshape(NCHIP, 4 * SLAB, D // NCHIP)[:, :CONV]),
        vec=jnp.concatenate([conv_b[None], lru_ba[None], lru_bx[None], lru_lambda[None],
                             jnp.zeros((SLAB - 4, D), F32)], axis=0),
        fb=jnp.pad(forget_b[None], ((0, 0), (0, DH - NH))),
        wabd=_block_diag(lru_wa).astype(BF), wxbd=_block_diag(lru_wx).astype(BF),
        g_mix=norm_mix_g[None], g_mlp=norm_mlp_g[None], g_fin=norm_final_g[None])

    r = _local_step(x[0], loss_target[0], w, T, dist=_Exchanges({n: wts[n].astype(BF) for n in _LATE}))

    halves = [_add8(*r["big"][n], me1, c1, "add8_" + n) for n in _BIG]
    theirs = _share_halves(halves)
    low = ci == 0
    gsum = {n: jnp.concatenate([jnp.where(low, h, t), jnp.where(low, t, h)], axis=0)
            for n, h, t in zip(_BIG, halves, theirs)}
    lacc = r["lacc"]
    small = dict(norm_mix_g=r["dg_mix"], norm_mlp_g=r["dg_mlp"], norm_final_g=r["dg_fin"], conv_b=lacc[3],
                 lru_ba=lacc[0], lru_bx=lacc[1], lru_lambda=lacc[2], forget_b=r["dfb"][0, :NH],
                 conv_w=lacc[4:4 + CONV], lru_wa=_block_diag_extract(r["dwabd"]),
                 lru_wx=_block_diag_extract(r["dwxbd"]), loss=r["loss_vec"])
    gpack = _allreduce_small(_pack_small(small, col0))
    loss = jnp.sum(gpack[_ROW_LOSS])

    grads, delta, new_m, new_v = {}, {}, {}, {}
    for n in _BIG:
        if n == "w_in":
            gt = gsum[n].T
            grads[n] = gt.T
            delta[n], new_m[n], new_v[n] = [a.T for a in _adamw(wts[n].T, gt, mom[n].T, var[n].T, "adamw_" + n)]
        else:
            grads[n] = gsum[n]
            delta[n], new_m[n], new_v[n] = _adamw(wts[n], gsum[n], mom[n], var[n], "adamw_" + n)
    dp, mp, vp = _adamw(_pack_small(wts, col0), gpack, _pack_small(mom, col0), _pack_small(var, col0), "adamw_small")
    for dst, pack in ((grads, gpack), (delta, dp), (new_m, mp), (new_v, vp)):
        dst.update(_unpack_small(pack, col0))
    return (loss, r["dx"][None], *[grads[n] for n in _WEIGHTS], *[delta[n] for n in _WEIGHTS],
            *[new_m[n] for n in _WEIGHTS], *[new_v[n] for n in _WEIGHTS])
```

```python
import functools
import math

import jax
import jax.numpy as jnp
import numpy as np
from jax import lax
from jax.experimental import pallas as pl
from jax.experimental.pallas import tpu as pltpu

F32 = jnp.float32
BF = jnp.bfloat16

D = 1024
NH = 8
DH = 128
FF = 4096
CONV = 4
LRU_BLOCKS = 16
LRU_BW = 64
BD = 256
NBD = D // BD
LRU_C = 8.0
EPS = 1e-6
DIN = 7176
DINP = 7296
NCHIP = 4
SLAB = 8
VMEM_CAP = 60 * 1024 * 1024

ADAM_LR = 0.001
ADAM_B1 = 0.9
ADAM_B2 = 0.999
ADAM_EPS = 1e-08
ADAM_WD = 0.01
ADAM_STEP = 10

MESH = pl.DeviceIdType.MESH


def _vmem_limit(nbytes):
    return int(min(VMEM_CAP, max(32 * 1024 * 1024, 3 * nbytes)))


def _nbytes(shape, dtype):
    return int(np.prod(shape)) * jnp.dtype(dtype).itemsize


def _sig(x):
    return 0.5 * jnp.tanh(0.5 * x) + 0.5


def _log1p(u):
    w = 1.0 + u
    return jnp.where(w == 1.0, u, jnp.log(w) * (u / (w - 1.0)))


def _one_minus_sq(a, la):
    return jnp.tanh(-la) * (1.0 + a * a)


def _softplus(z):
    return jnp.maximum(z, 0.0) + _log1p(jnp.exp(-jnp.abs(z)))


_GELU_C = math.sqrt(2.0 / math.pi)


def _gelu(x):
    return 0.5 * x * (1.0 + jnp.tanh(_GELU_C * (x + 0.044715 * x * x * x)))


def _gelu_grad(x):
    t = jnp.tanh(_GELU_C * (x + 0.044715 * x * x * x))
    return 0.5 * (1.0 + t) + 0.5 * x * (1.0 - t * t) * _GELU_C * (1.0 + 3.0 * 0.044715 * x * x)


def _slab_scan_fwd(a, b):
    row = lax.broadcasted_iota(jnp.int32, a.shape, 0)
    for k in (1, 2, 4):
        a_s = pltpu.roll(a, k, 0)
        b_s = pltpu.roll(b, k, 0)
        m = row >= k
        b = jnp.where(m, a * b_s + b, b)
        a = jnp.where(m, a * a_s, a)
    return a, b


def _slab_scan_bwd(a, b):
    row = lax.broadcasted_iota(jnp.int32, a.shape, 0)
    for k in (1, 2, 4):
        a_s = pltpu.roll(a, SLAB - k, 0)
        b_s = pltpu.roll(b, SLAB - k, 0)
        m = row < SLAB - k
        b = jnp.where(m, a * b_s + b, b)
        a = jnp.where(m, a * a_s, a)
    return a, b


_DIMS = {"nn": (((1,), (0,)), ((), ())), "nt": (((1,), (1,)), ((), ())), "tn": (((0,), (0,)), ((), ()))}


def _dot(a, b, mode="nn"):
    return lax.dot_general(a, b, _DIMS[mode], preferred_element_type=F32)


def _mm(a, b, mode, M, N, K, *, name, out_dtype=F32, tm=None, tn=1024, tk=1024,
        a_off=(0, 0), b_off=(0, 0), add=None, epi=None, epi_ins=()):
    if tm is None:
        tm = 2048 if (K <= tk and out_dtype == BF and add is None) else 1024
    tm, tn, tk = min(tm, M), min(tn, N), min(tk, K)
    nk = K // tk
    grid = (M // tm, N // tn, nk)
    if mode == "nn":
        a_spec = pl.BlockSpec((tm, tk), lambda i, j, k: (i + a_off[0], k + a_off[1]))
        b_spec = pl.BlockSpec((tk, tn), lambda i, j, k: (k + b_off[0], j + b_off[1]))
    elif mode == "nt":
        a_spec = pl.BlockSpec((tm, tk), lambda i, j, k: (i + a_off[0], k + a_off[1]))
        b_spec = pl.BlockSpec((tn, tk), lambda i, j, k: (j + b_off[0], k + b_off[1]))
    else:
        a_spec = pl.BlockSpec((tk, tm), lambda i, j, k: (k + a_off[0], i + a_off[1]))
        b_spec = pl.BlockSpec((tk, tn), lambda i, j, k: (k + b_off[0], j + b_off[1]))
    o_spec = pl.BlockSpec((tm, tn), lambda i, j, k: (i, j))
    extra = ([add] if add is not None else []) + list(epi_ins)
    n_extra = len(extra)
    has_add = add is not None

    def body(*refs):
        a_ref, b_ref = refs[0], refs[1]
        ex = refs[2:2 + n_extra]
        o_ref = refs[2 + n_extra]

        def finish(acc):
            if has_add:
                acc = acc + ex[0][...].astype(F32)
            if epi is not None:
                acc = epi(acc, *[e[...] for e in ex[(1 if has_add else 0):]])
            o_ref[...] = acc.astype(o_ref.dtype)

        p = _dot(a_ref[...].astype(BF), b_ref[...].astype(BF), mode)
        if nk == 1:
            finish(p)
        else:
            acc_ref = refs[3 + n_extra]
            k = pl.program_id(2)

            @pl.when(k == 0)
            def _():
                acc_ref[...] = p

            @pl.when(k > 0)
            def _():
                acc_ref[...] += p

            @pl.when(k == nk - 1)
            def _():
                finish(acc_ref[...])

    blk = (_nbytes((tm, tk), a.dtype) + _nbytes((tk, tn), b.dtype) + _nbytes((tm, tn), out_dtype)
           + sum(_nbytes((tm, tn), e.dtype) for e in extra) + 2 * _nbytes((tm, tn), F32))
    return pl.pallas_call(
        body, name=name, grid=grid,
        in_specs=[a_spec, b_spec] + [o_spec] * n_extra,
        out_specs=o_spec,
        out_shape=jax.ShapeDtypeStruct((M, N), out_dtype),
        scratch_shapes=[pltpu.VMEM((tm, tn), F32)] if nk > 1 else [],
        compiler_params=pltpu.CompilerParams(
            dimension_semantics=("parallel", "parallel", "arbitrary"), vmem_limit_bytes=_vmem_limit(blk)),
    )(a, b, *extra)


def _ew(fn, T, tm, ins, consts, outs, accs, *, name, reverse=False):
    tm = min(tm, T)
    nt = T // tm
    n_in, n_c, n_o, n_a = len(ins), len(consts), len(outs), len(accs)

    def row(i):
        return nt - 1 - i if reverse else i

    in_specs = [pl.BlockSpec((tm, w), functools.partial(lambda i, cb: (row(i), cb), cb=cb)) for (_, w, cb) in ins]
    in_specs += [pl.BlockSpec(c.shape, functools.partial(lambda i, nd: (0,) * nd, nd=c.ndim)) for c in consts]
    out_specs = [pl.BlockSpec((tm, w), lambda i: (row(i), 0)) for (w, _) in outs]
    out_specs += [pl.BlockSpec((r, w), lambda i: (0, 0)) for (r, w) in accs]
    out_shape = [jax.ShapeDtypeStruct((T, w), dt) for (w, dt) in outs]
    out_shape += [jax.ShapeDtypeStruct((r, w), F32) for (r, w) in accs]

    def body(*refs):
        in_refs = refs[:n_in]
        c_refs = refs[n_in:n_in + n_c]
        o_refs = refs[n_in + n_c:n_in + n_c + n_o]
        a_refs = refs[n_in + n_c + n_o:]
        ov, av = fn([r[...] for r in in_refs], [r[...] for r in c_refs])
        for r, v in zip(o_refs, ov):
            r[...] = v.astype(r.dtype)
        if n_a:
            i = pl.program_id(0)

            @pl.when(i == 0)
            def _():
                for r, v in zip(a_refs, av):
                    r[...] = v

            @pl.when(i > 0)
            def _():
                for r, v in zip(a_refs, av):
                    r[...] += v

    blk = (sum(_nbytes((tm, w), a.dtype) for (a, w, _) in ins) + sum(_nbytes(c.shape, c.dtype) for c in consts)
           + sum(_nbytes((tm, w), dt) for (w, dt) in outs) + sum(_nbytes(s, F32) for s in accs))
    res = pl.pallas_call(
        body, name=name, grid=(nt,), in_specs=in_specs, out_specs=out_specs, out_shape=out_shape,
        compiler_params=pltpu.CompilerParams(
            dimension_semantics=("arbitrary",), vmem_limit_bytes=_vmem_limit(blk)),
    )(*[a for (a, _, _) in ins], *consts)
    return res


def _colsum(v):
    return jnp.sum(v, axis=0, keepdims=True)


FGATE_GROUP = 4


def _fgate_fwd(fl, fb, T, tm=512):
    tm = min(tm, T)

    def body(fl_ref, fb_ref, f_ref, carry_ref):
        i = pl.program_id(0)

        @pl.when(i == 0)
        def _():
            carry_ref[...] = jnp.zeros_like(carry_ref)

        rows = FGATE_GROUP * SLAB
        sub = lax.broadcasted_iota(jnp.int32, (rows, DH), 0) % SLAB

        def group(s, carry):
            r0 = pl.multiple_of(s * rows, rows)
            z = fl_ref[pl.ds(r0, rows), :] + fb_ref[...]
            c = jnp.minimum(z, 0.0) - _log1p(jnp.exp(-jnp.abs(z)))
            for k in (1, 2, 4):
                c = c + jnp.where(sub >= k, pltpu.roll(c, k, 0), 0.0)
            for u in range(FGATE_GROUP):
                cu = c[u * SLAB:(u + 1) * SLAB] + carry
                f_ref[pl.ds(r0 + u * SLAB, SLAB), :] = cu
                carry = cu[SLAB - 1:SLAB, :]
            return carry

        carry_ref[0:1, :] = lax.fori_loop(0, tm // rows, group, carry_ref[0:1, :])

    return pl.pallas_call(
        body, name="fgate_fwd", grid=(T // tm,),
        in_specs=[pl.BlockSpec((tm, DH), lambda i: (i, 0)), pl.BlockSpec((1, DH), lambda i: (0, 0))],
        out_specs=pl.BlockSpec((tm, DH), lambda i: (i, 0)),
        out_shape=jax.ShapeDtypeStruct((T, DH), F32),
        scratch_shapes=[pltpu.VMEM((SLAB, DH), F32)],
        compiler_params=pltpu.CompilerParams(dimension_semantics=("arbitrary",)),
    )(fl, fb)


def _fgate_bwd(dF, fl, fb, T, tm=512):
    tm = min(tm, T)
    nt = T // tm

    def body(df_ref, fl_ref, fb_ref, o_ref, acc_ref, carry_ref):
        i = pl.program_id(0)

        @pl.when(i == 0)
        def _():
            carry_ref[...] = jnp.zeros_like(carry_ref)
            acc_ref[...] = jnp.zeros_like(acc_ref)

        rows = FGATE_GROUP * SLAB
        sub = lax.broadcasted_iota(jnp.int32, (rows, DH), 0) % SLAB

        def group(n, carry):
            g_next, acc = carry
            r0 = pl.multiple_of((tm // rows - 1 - n) * rows, rows)
            c = df_ref[pl.ds(r0, rows), :]
            for k in (1, 2, 4):
                c = c + jnp.where(sub < SLAB - k, pltpu.roll(c, rows - k, 0), 0.0)
            sg = _sig(-(fl_ref[pl.ds(r0, rows), :] + fb_ref[...]))
            for u in reversed(range(FGATE_GROUP)):
                cu = c[u * SLAB:(u + 1) * SLAB] + g_next
                dfl = cu * sg[u * SLAB:(u + 1) * SLAB]
                o_ref[pl.ds(r0 + u * SLAB, SLAB), :] = dfl.astype(o_ref.dtype)
                acc = acc + _colsum(dfl)
                g_next = cu[0:1, :]
            return g_next, acc

        g, acc = lax.fori_loop(0, tm // rows, group, (carry_ref[0:1, :], jnp.zeros((1, DH), F32)))
        carry_ref[0:1, :] = g
        acc_ref[...] += acc

    return pl.pallas_call(
        body, name="fgate_bwd", grid=(nt,),
        in_specs=[pl.BlockSpec((tm, DH), lambda i: (nt - 1 - i, 0)), pl.BlockSpec((tm, DH), lambda i: (nt - 1 - i, 0)),
                  pl.BlockSpec((1, DH), lambda i: (0, 0))],
        out_specs=[pl.BlockSpec((tm, DH), lambda i: (nt - 1 - i, 0)), pl.BlockSpec((1, DH), lambda i: (0, 0))],
        out_shape=[jax.ShapeDtypeStruct((T, DH), BF), jax.ShapeDtypeStruct((1, DH), F32)],
        scratch_shapes=[pltpu.VMEM((SLAB, DH), F32)],
        compiler_params=pltpu.CompilerParams(dimension_semantics=("arbitrary",)),
    )(dF, fl, fb)


def _shift_down(x, d, prev8):
    n = x.shape[0]
    row8 = lax.broadcasted_iota(jnp.int32, (SLAB, x.shape[1]), 0)
    y = pltpu.roll(x, d, 0)
    top = jnp.where(row8 < d, pltpu.roll(prev8, d, 0), y[0:SLAB])
    return jnp.concatenate([top, y[SLAB:]], axis=0)


def _shift_up(x, d, next8):
    n = x.shape[0]
    row8 = lax.broadcasted_iota(jnp.int32, (SLAB, x.shape[1]), 0)
    y = pltpu.roll(x, n - d, 0)
    bottom = jnp.where(row8 >= SLAB - d, pltpu.roll(next8, SLAB - d, 0), y[n - SLAB:])
    return jnp.concatenate([y[:n - SLAB], bottom], axis=0)


def _conv(x, prev8, cw, cb):
    xs = [x] + [_shift_down(x, d, prev8) for d in (1, 2, 3)]
    xa = cb + cw[3:4, :] * xs[0] + cw[2:3, :] * xs[1] + cw[1:2, :] * xs[2] + cw[0:1, :] * xs[3]
    return xa, xs


def _lru_gates(xa_g, wa_g, wx_g, ba_g, bx_g, sp_g):
    xb = xa_g.astype(BF)
    r = _sig(_dot(xb, wa_g) + ba_g)
    ig = _sig(_dot(xb, wx_g) + bx_g)
    la = -LRU_C * r * sp_g
    a = jnp.exp(la)
    mult = jnp.sqrt(_one_minus_sq(a, la))
    return r, ig, a, mult


def _lru_fwd(xg, cw, vec, wabd, wxbd, T, tm=256):
    tm = min(tm, T)
    nsl = tm // SLAB

    def body(x_ref, xp_ref, cw_ref, vec_ref, wa_ref, wx_ref, h_ref, a_s, b_s, carry_ref):
        i = pl.program_id(0)

        @pl.when(i == 0)
        def _():
            carry_ref[...] = jnp.zeros_like(carry_ref)

        x = x_ref[...].astype(F32)
        prev8 = jnp.where(i > 0, xp_ref[SLAB:, :].astype(F32), 0.0)
        vec_v = vec_ref[...]
        xa, _ = _conv(x, prev8, cw_ref[...], vec_v[0:1, :])
        sp = _softplus(-vec_v[3:4, :])
        for g in range(NBD):
            sl = slice(g * BD, (g + 1) * BD)
            _, ig, a, mult = _lru_gates(xa[:, sl], wa_ref[g], wx_ref[g], vec_v[1:2, sl], vec_v[2:3, sl], sp[:, sl])
            a_s[:, sl] = a
            b_s[:, sl] = mult * ig * xa[:, sl]

        def slab(s, carry):
            r0 = pl.multiple_of(s * SLAB, SLAB)
            A, B = _slab_scan_fwd(a_s[pl.ds(r0, SLAB), :], b_s[pl.ds(r0, SLAB), :])
            h = A * carry + B
            h_ref[pl.ds(r0, SLAB), :] = h
            return h[SLAB - 1:SLAB, :]

        carry_ref[0:1, :] = lax.fori_loop(0, nsl, slab, carry_ref[0:1, :])

    blk = 5 * _nbytes((tm, D), F32) + 2 * _nbytes((NBD, BD, BD), BF)
    return pl.pallas_call(
        body, name="lru_fwd", grid=(T // tm,),
        in_specs=[pl.BlockSpec((tm, D), lambda i: (i, 0)),
                  pl.BlockSpec((2 * SLAB, D), lambda i: (jnp.maximum(i * (nsl // 2) - 1, 0), 0)),
                  pl.BlockSpec((CONV, D), lambda i: (0, 0)),
                  pl.BlockSpec((SLAB, D), lambda i: (0, 0)),
                  pl.BlockSpec((NBD, BD, BD), lambda i: (0, 0, 0)),
                  pl.BlockSpec((NBD, BD, BD), lambda i: (0, 0, 0))],
        out_specs=pl.BlockSpec((tm, D), lambda i: (i, 0)),
        out_shape=jax.ShapeDtypeStruct((T, D), F32),
        scratch_shapes=[pltpu.VMEM((tm, D), F32), pltpu.VMEM((tm, D), F32), pltpu.VMEM((SLAB, D), F32)],
        compiler_params=pltpu.CompilerParams(dimension_semantics=("arbitrary",), vmem_limit_bytes=_vmem_limit(blk)),
    )(xg, xg, cw, vec, wabd, wxbd)


def _lru_bwd(xg, h, dha, cw, vec, wabd, wxbd, T, tm=256):
    tm = min(tm, T)
    nsl = tm // SLAB
    nt = T // tm

    def body(x_ref, xp_ref, h_ref, hp_ref, dh_ref, cw_ref, vec_ref, wa_ref, wx_ref,
             dx_ref, dwa_ref, dwx_ref, acc_ref, a_s, b_s, g_s, dxa_s, carry_ref, dxan_ref):
        n = pl.program_id(0)
        it = nt - 1 - n

        @pl.when(n == 0)
        def _():
            carry_ref[...] = jnp.zeros_like(carry_ref)
            dxan_ref[...] = jnp.zeros_like(dxan_ref)
            dwa_ref[...] = jnp.zeros_like(dwa_ref)
            dwx_ref[...] = jnp.zeros_like(dwx_ref)
            acc_ref[...] = jnp.zeros_like(acc_ref)

        x = x_ref[...].astype(F32)
        prev8 = jnp.where(it > 0, xp_ref[SLAB:, :].astype(F32), 0.0)
        hprev8 = jnp.where(it > 0, hp_ref[...], 0.0)
        vec_v = vec_ref[...]
        cw_v = cw_ref[...]
        xa, xs = _conv(x, prev8, cw_v, vec_v[0:1, :])
        sp = _softplus(-vec_v[3:4, :])
        gates = []
        for g in range(NBD):
            sl = slice(g * BD, (g + 1) * BD)
            r, ig, a, mult = _lru_gates(xa[:, sl], wa_ref[g], wx_ref[g], vec_v[1:2, sl], vec_v[2:3, sl], sp[:, sl])
            gates.append((r, ig, a, mult))
            a_s[:, sl] = a
        a_s[...] = _shift_up(a_s[...], 1, carry_ref[...])
        b_s[...] = dh_ref[...]

        def slab(m, carry):
            r0 = pl.multiple_of((nsl - 1 - m) * SLAB, SLAB)
            A, B = _slab_scan_bwd(a_s[pl.ds(r0, SLAB), :], b_s[pl.ds(r0, SLAB), :])
            gg = A * carry + B
            g_s[pl.ds(r0, SLAB), :] = gg
            return gg[0:1, :]

        g_first = lax.fori_loop(0, nsl, slab, carry_ref[1:2, :])
        gt = g_s[...]
        h_prev = _shift_down(h_ref[...], 1, hprev8)
        dba = []
        dbx = []
        dsp = []
        for g in range(NBD):
            sl = slice(g * BD, (g + 1) * BD)
            r, ig, a, mult = gates[g]
            xa_g = xa[:, sl]
            g_g = gt[:, sl]
            da = g_g * h_prev[:, sl]
            dmult = g_g * ig * xa_g
            di = g_g * mult * xa_g
            dxa_g = g_g * mult * ig
            dla = da * a - dmult * (a * a / mult)
            dr = dla * (-LRU_C) * sp[:, sl]
            dsp.append(_colsum(dla * (-LRU_C) * r))
            dra = (dr * r * (1.0 - r))
            dix = (di * ig * (1.0 - ig))
            dba.append(_colsum(dra))
            dbx.append(_colsum(dix))
            dra_b = dra.astype(BF)
            dix_b = dix.astype(BF)
            xb = xa_g.astype(BF)
            dxa_g = dxa_g + _dot(dra_b, wa_ref[g], "nt") + _dot(dix_b, wx_ref[g], "nt")
            dwa_ref[g] += _dot(xb, dra_b, "tn")
            dwx_ref[g] += _dot(xb, dix_b, "tn")
            dxa_s[:, sl] = dxa_g
        dxa = dxa_s[...]
        nxt = dxan_ref[...]
        dx = (cw_v[3:4, :] * dxa + cw_v[2:3, :] * _shift_up(dxa, 1, nxt)
              + cw_v[1:2, :] * _shift_up(dxa, 2, nxt) + cw_v[0:1, :] * _shift_up(dxa, 3, nxt))
        dx_ref[...] = dx.astype(dx_ref.dtype)
        acc_ref[0:1, :] += jnp.concatenate(dba, axis=1)
        acc_ref[1:2, :] += jnp.concatenate(dbx, axis=1)
        acc_ref[2:3, :] += jnp.concatenate(dsp, axis=1)
        acc_ref[3:4, :] += _colsum(dxa)
        for k in range(CONV):
            acc_ref[4 + k:5 + k, :] += _colsum(dxa * xs[CONV - 1 - k])
        dxan_ref[...] = dxa[0:SLAB, :]
        a_first = jnp.concatenate([gates[g][2][0:1, :] for g in range(NBD)], axis=1)
        carry_ref[0:1, :] = a_first
        carry_ref[1:2, :] = g_first

        @pl.when(n == nt - 1)
        def _():
            acc_ref[2:3, :] = acc_ref[2:3, :] * (-_sig(-vec_v[3:4, :]))

    rowblk = lambda i: (nt - 1 - i, 0)
    prevblk = lambda i: (jnp.maximum((nt - 1 - i) * nsl - 1, 0), 0)
    c2 = lambda i: (0, 0)
    c3 = lambda i: (0, 0, 0)
    blk = 12 * _nbytes((tm, D), F32) + 6 * _nbytes((NBD, BD, BD), F32)
    return pl.pallas_call(
        body, name="lru_bwd", grid=(nt,),
        in_specs=[pl.BlockSpec((tm, D), rowblk),
                  pl.BlockSpec((2 * SLAB, D), lambda i: (jnp.maximum((nt - 1 - i) * (nsl // 2) - 1, 0), 0)),
                  pl.BlockSpec((tm, D), rowblk), pl.BlockSpec((SLAB, D), prevblk),
                  pl.BlockSpec((tm, D), rowblk),
                  pl.BlockSpec((CONV, D), c2), pl.BlockSpec((SLAB, D), c2),
                  pl.BlockSpec((NBD, BD, BD), c3), pl.BlockSpec((NBD, BD, BD), c3)],
        out_specs=[pl.BlockSpec((tm, D), rowblk), pl.BlockSpec((NBD, BD, BD), c3), pl.BlockSpec((NBD, BD, BD), c3),
                   pl.BlockSpec((16, D), c2)],
        out_shape=[jax.ShapeDtypeStruct((T, D), BF), jax.ShapeDtypeStruct((NBD, BD, BD), F32),
                   jax.ShapeDtypeStruct((NBD, BD, BD), F32), jax.ShapeDtypeStruct((16, D), F32)],
        scratch_shapes=[pltpu.VMEM((tm, D), F32), pltpu.VMEM((tm, D), F32), pltpu.VMEM((tm, D), F32),
                        pltpu.VMEM((tm, D), F32), pltpu.VMEM((SLAB, D), F32), pltpu.VMEM((SLAB, D), F32)],
        compiler_params=pltpu.CompilerParams(dimension_semantics=("arbitrary",), vmem_limit_bytes=_vmem_limit(blk)),
    )(xg, xg, h, h, dha, cw, vec, wabd, wxbd)


_SCALE = 1.0 / math.sqrt(DH)


_ANY = pl.BlockSpec(memory_space=pl.ANY)


class _Side:
    def __init__(self, srcs, outs, nsem, copies):
        self.srcs, self.outs, self.nsem, self.copies = list(srcs), list(outs), nsem, copies


def _pallas(body, operands, *, name, grid, in_specs, out_specs, out_shape, scratch_shapes=(), semantics,
            vmem=None, side=None):
    if side is None:
        return pl.pallas_call(
            body, name=name, grid=grid, in_specs=in_specs, out_specs=out_specs, out_shape=out_shape,
            scratch_shapes=list(scratch_shapes),
            compiler_params=pltpu.CompilerParams(dimension_semantics=semantics, vmem_limit_bytes=vmem),
        )(*operands)
    n_in, n_out, n_scr = len(in_specs), len(out_specs), len(scratch_shapes)
    ns, no = len(side.srcs), len(side.outs)

    def hosted(*refs):
        ins, refs = refs[:n_in], refs[n_in:]
        sin, refs = refs[:ns], refs[ns:]
        outs, refs = refs[:n_out], refs[n_out:]
        sout, refs = refs[:no], refs[no:]
        scr, (send, recv) = refs[:n_scr], refs[n_scr:]
        ids = [pl.program_id(a) for a in range(len(grid))]
        first = functools.reduce(jnp.logical_and, [i == 0 for i in ids])
        last = functools.reduce(jnp.logical_and, [i == g - 1 for i, g in zip(ids, grid)])

        @pl.when(first)
        def _():
            for cp in side.copies(sin, sout, send, recv):
                cp.start()

        body(*ins, *outs, *scr)

        @pl.when(last)
        def _():
            for cp in side.copies(sin, sout, send, recv):
                cp.wait()

    return pl.pallas_call(
        hosted, name=name, grid=grid, in_specs=list(in_specs) + [_ANY] * ns, out_specs=list(out_specs) + [_ANY] * no,
        out_shape=list(out_shape) + side.outs,
        scratch_shapes=list(scratch_shapes) + [pltpu.SemaphoreType.DMA((side.nsem,)), pltpu.SemaphoreType.DMA((side.nsem,))],
        compiler_params=pltpu.CompilerParams(dimension_semantics=("arbitrary",) * len(grid), vmem_limit_bytes=vmem),
    )(*operands, *side.srcs)


DA = 2 * DH
_LOG2E = math.log2(math.e)
_C2 = _SCALE * _LOG2E


def _aug_fn(ins, cs):
    q, k, fcum = ins
    g_all = fcum * _LOG2E
    lane = lax.broadcasted_iota(jnp.int32, (q.shape[0], DH), 1)
    qa, ka = [], []
    for hd in range(NH):
        g = g_all[:, hd:hd + 1]
        hi = g.astype(BF).astype(F32)
        mid = (g - hi).astype(BF).astype(F32)
        lo = ((g - hi) - mid).astype(BF).astype(F32)
        qx = jnp.where(lane == 0, hi, jnp.where(lane == 1, mid, jnp.where(lane == 2, lo,
                                                                          jnp.where(lane < 6, 1.0, 0.0))))
        kx = jnp.where(lane < 3, 1.0, jnp.where(lane == 3, -hi, jnp.where(lane == 4, -mid,
                                                                          jnp.where(lane == 5, -lo, 0.0))))
        qa += [q[:, hd * DH:(hd + 1) * DH], qx.astype(BF)]
        ka += [k[:, hd * DH:(hd + 1) * DH], kx.astype(BF)]
    return [jnp.concatenate(qa, axis=1), jnp.concatenate(ka, axis=1)], []


_KA_ONES = DH + 3
KT_ONES = 16
FWD_KEY_PARTS = 2
BWD_QUERY_PARTS = 2


def _attn_fwd(qa, ka, qkv, T, blk=512, side=None):
    blk = min(blk, T)
    nb = T // blk
    half = blk // 2

    def body(q_ref, k_ref, vn_ref, o_ref, lse_ref, v_ref):
        i = pl.program_id(1)

        @pl.when(i == 0)
        def _():
            for jj in range(nb):
                v_ref[jj] = jnp.concatenate([vn_ref[jj * blk:(jj + 1) * blk, :].astype(F32).T.astype(BF),
                                             jnp.ones((KT_ONES, blk), BF)], axis=0)

        q = q_ref[...]

        def scores(j):
            r0 = pl.multiple_of(j * blk, blk)
            return _dot(k_ref[pl.ds(r0, blk), :], q, "nt")

        def update(s, vj, carry):
            m, acc = carry
            m_new = jnp.maximum(m, jnp.max(s, axis=0, keepdims=True))
            alpha = jnp.exp2(m - m_new)
            acc = alpha * acc + _dot(vj, jnp.exp2(s - m_new).astype(BF))
            return m_new, acc

        def step(j, st):
            r0 = pl.multiple_of(j * blk, blk)
            part = blk // FWD_KEY_PARTS
            ss = [_dot(k_ref[pl.ds(r0 + n * part, part), :], q, "nt") for n in range(FWD_KEY_PARTS)]
            vj = v_ref[j]
            for n in range(FWD_KEY_PARTS):
                st = update(ss[n], vj[:, n * part:(n + 1) * part], st)
            return st

        init = (jnp.full((1, blk), -jnp.inf, F32), jnp.zeros((DH + KT_ONES, blk), F32))
        carry = lax.fori_loop(0, i, step, init)
        last = scores(i)
        vi = v_ref[i]
        rk = lax.broadcasted_iota(jnp.int32, (half, blk), 0)
        cq = lax.broadcasted_iota(jnp.int32, (half, blk), 1)
        m, acc = update(jnp.where(cq >= rk, last[:half], -jnp.inf), vi[:, :half], carry)
        s2 = jnp.where(cq[:, :half] >= rk[:, :half], last[half:, half:], -jnp.inf)
        m2, acc2 = update(s2, vi[:, half:], (m[:, half:], acc[:, half:]))
        m = jnp.concatenate([m[:, :half], m2], axis=1)
        acc = jnp.concatenate([acc[:, :half], acc2], axis=1)
        l = acc[DH:DH + 1]
        o_ref[...] = (acc[:DH] / l).T.astype(o_ref.dtype)
        lse_ref[...] = m + jnp.log(l) * _LOG2E

    vm = _nbytes((T, DA), BF) + 2 * _nbytes((T, DH), BF) + 6 * _nbytes((blk, blk), F32)
    return _pallas(
        body, (qa, ka, qkv), name="attn_fwd", grid=(NH, nb),
        in_specs=[pl.BlockSpec((blk, DA), lambda h, i: (i, h)),
                  pl.BlockSpec((T, DA), lambda h, i: (0, h)),
                  pl.BlockSpec((T, DH), lambda h, i: (0, 2 * NH + h))],
        out_specs=[pl.BlockSpec((blk, DH), lambda h, i: (i, h)),
                   pl.BlockSpec((None, None, 1, blk), lambda h, i: (h, i, 0, 0))],
        out_shape=[jax.ShapeDtypeStruct((T, D), BF), jax.ShapeDtypeStruct((NH, nb, 1, blk), F32)],
        scratch_shapes=[pltpu.VMEM((nb, DH + KT_ONES, blk), BF)],
        semantics=("parallel", "arbitrary"), vmem=_vmem_limit(vm), side=side)


def _attn_bwd(qa, ka, qkv, do, lrow, drow, T, blk=512, side=None):
    blk = min(blk, T)
    nb = T // blk

    def body(ka_ref, v_ref, qa_ref, do_ref, l_ref, d_ref, dq_ref, dk_ref, dv_ref, dfs_ref, dft_ref, dq_s):
        j = pl.program_id(1)

        @pl.when(j == 0)
        def _():
            dq_s[...] = jnp.zeros_like(dq_s)

        row = lax.broadcasted_iota(jnp.int32, (DH + KT_ONES, blk), 0)
        dq_scale = jnp.where(row < DH, _SCALE, 1.0)

        kaj = ka_ref[...]
        ktj = jnp.concatenate([kaj[:, :DH].astype(F32).T.astype(BF), jnp.ones((KT_ONES, blk), BF)], axis=0)
        vj = v_ref[...]

        def step(i, carry):
            dka, dv = carry
            r0 = pl.multiple_of(i * blk, blk)
            part = blk // BWD_QUERY_PARTS
            lr, dr = l_ref[i], d_ref[i]
            loaded = []
            for n in range(BWD_QUERY_PARTS):
                qn = qa_ref[pl.ds(r0 + n * part, part), :]
                don = do_ref[pl.ds(r0 + n * part, part), :]
                loaded.append((qn, don, _dot(kaj, qn, "nt"), _dot(vj, don, "nt")))
            dq_parts = []
            for n, (qn, don, raw, dpt) in enumerate(loaded):
                pt = jnp.exp2(raw - lr[:, n * part:(n + 1) * part])
                dsb = (pt * (dpt - dr[:, n * part:(n + 1) * part])).astype(BF)
                dv = dv + _dot(pt.astype(BF), don)
                dka = dka + _dot(dsb, qn)
                dq_parts.append(_dot(ktj, dsb))
            dq_s[i] += jnp.concatenate(dq_parts, axis=1) * dq_scale
            return dka, dv

        def diagonal():
            half = blk // 2
            r0 = pl.multiple_of(j * blk, blk)
            qi = qa_ref[pl.ds(r0, blk), :]
            doi = do_ref[pl.ds(r0, blk), :]
            lr, dr = l_ref[j], d_ref[j]
            rk = lax.broadcasted_iota(jnp.int32, (half, blk), 0)
            cq = lax.broadcasted_iota(jnp.int32, (half, blk), 1)

            def quarter(ka, v, q, do, l2, d2, keep):
                pt = jnp.exp2(jnp.where(keep, _dot(ka, q, "nt") - l2, -jnp.inf))
                dsb = (pt * (_dot(v, do, "nt") - d2)).astype(BF)
                return _dot(dsb, q), _dot(pt.astype(BF), do), dsb

            dka1, dv1, ds1 = quarter(kaj[:half], vj[:half], qi, doi, lr, dr, cq >= rk)
            dka2, dv2, ds2 = quarter(kaj[half:], vj[half:], qi[half:], doi[half:], lr[:, half:], dr[:, half:],
                                     cq[:, :half] >= rk[:, :half])
            dq2 = jnp.concatenate([jnp.zeros((DH + KT_ONES, half), F32), _dot(ktj[:, half:], ds2)], axis=1)
            dq_s[j] += (_dot(ktj[:, :half], ds1) + dq2) * dq_scale
            return jnp.concatenate([dka1, dka2], axis=0), jnp.concatenate([dv1, dv2], axis=0)

        dka, dv = lax.fori_loop(j + 1, nb, step, diagonal())
        dk_ref[...] = (dka[:, :DH] * (1.0 / _LOG2E)).astype(dk_ref.dtype)
        dv_ref[...] = dv.astype(dv_ref.dtype)
        dfs_ref[...] = dka[:, DH:].T[_KA_ONES - DH:_KA_ONES - DH + 1, :]

        @pl.when(j == nb - 1)
        def _():
            for ii in range(nb):
                t = dq_s[ii]
                dq_ref[ii * blk:(ii + 1) * blk, :] = t[:DH].T.astype(dq_ref.dtype)
                dft_ref[ii] = t[DH:DH + 1]

    rowv = pl.BlockSpec((None, nb, 1, blk), lambda h, j: (h, 0, 0, 0))
    vm = (_nbytes((T, DA), BF) + 2 * _nbytes((T, DH), BF) + _nbytes((T, DH + KT_ONES), F32)
          + 8 * _nbytes((blk, blk), F32))
    return _pallas(
        body, (ka, qkv, qa, do, lrow, drow), name="attn_bwd", grid=(NH, nb),
        in_specs=[pl.BlockSpec((blk, DA), lambda h, j: (j, h)),
                  pl.BlockSpec((blk, DH), lambda h, j: (j, 2 * NH + h)),
                  pl.BlockSpec((T, DA), lambda h, j: (0, h)),
                  pl.BlockSpec((T, DH), lambda h, j: (0, h)),
                  rowv, rowv],
        out_specs=[pl.BlockSpec((T, DH), lambda h, j: (0, h)),
                   pl.BlockSpec((blk, DH), lambda h, j: (j, h)),
                   pl.BlockSpec((blk, DH), lambda h, j: (j, h)),
                   pl.BlockSpec((None, None, 1, blk), lambda h, j: (h, j, 0, 0)), rowv],
        out_shape=[jax.ShapeDtypeStruct((T, D), BF), jax.ShapeDtypeStruct((T, D), BF),
                   jax.ShapeDtypeStruct((T, D), BF), jax.ShapeDtypeStruct((NH, nb, 1, blk), F32),
                   jax.ShapeDtypeStruct((NH, nb, 1, blk), F32)],
        scratch_shapes=[pltpu.VMEM((nb, DH + KT_ONES, blk), F32)],
        semantics=("parallel", "arbitrary"), vmem=_vmem_limit(vm), side=side)


def _norm_fn(ins, cs):
    x, = ins
    g, = cs
    r = lax.rsqrt(jnp.mean(x * x, axis=-1, keepdims=True) + EPS)
    return [x * r * g], []


def _norm_bwd_fn(ins, cs):
    x, dy, dres = ins
    g, = cs
    r = lax.rsqrt(jnp.mean(x * x, axis=-1, keepdims=True) + EPS)
    xh = x * r
    dxh = dy * g
    dx = dres + r * (dxh - xh * jnp.mean(dxh * xh, axis=-1, keepdims=True))
    return [dx, dx], [_colsum(dy * xh)]


def _final_fn(ins, cs):
    x2, tgt = ins
    g, = cs
    r = lax.rsqrt(jnp.mean(x2 * x2, axis=-1, keepdims=True) + EPS)
    xh = x2 * r
    e = xh * g - tgt
    dy = e * (1.0 / D)
    dxh = dy * g
    dx2 = r * (dxh - xh * jnp.mean(dxh * xh, axis=-1, keepdims=True))
    return [dx2, dx2], [_colsum(0.5 * e * e * (1.0 / D)), _colsum(dy * xh)]


def _z_fn(ins, cs):
    g, h = [v.astype(F32) for v in ins]
    return [_gelu(g) * h], []


def _mix_fn(ins, cs):
    gates, ya, yb = [v.astype(F32) for v in ins]
    return [_sig(gates[:, :D]) * ya + _sig(gates[:, D:]) * yb], []


def _mix_bwd_fn(ins, cs):
    dmix, gates, ya, yb = [v.astype(F32) for v in ins]
    ga = _sig(gates[:, :D])
    gb = _sig(gates[:, D:])
    dgates = jnp.concatenate([dmix * ya * ga * (1.0 - ga), dmix * yb * gb * (1.0 - gb)], axis=1)
    return [dmix * ga, dmix * gb, dgates], []


def _z_bwd_fn(ins, cs):
    dz, g, h = [v.astype(F32) for v in ins]
    return [dz * _gelu(g), dz * h * _gelu_grad(g)], []


def _delta_fn(ins, cs):
    do, o = ins
    p = do.astype(F32) * o.astype(F32)
    lane = lax.broadcasted_iota(jnp.int32, (p.shape[0], DH), 1)
    out = jnp.zeros((p.shape[0], DH), F32)
    for hd in range(NH):
        s = jnp.sum(p[:, hd * DH:(hd + 1) * DH], axis=1, keepdims=True)
        out = jnp.where(lane == hd, s, out)
    return [out], []


def _du_all(pieces, win, T, tm=256, side=None):
    tm = min(tm, T)
    n = len(pieces)

    def body(*refs):
        w_ref, o_ref = refs[n], refs[n + 1]
        acc = None
        for (a, off), a_ref in zip(pieces, refs[:n]):
            d = _dot(a_ref[...].astype(BF), w_ref[:, off:off + a.shape[1]], "nt")
            acc = d if acc is None else acc + d
        o_ref[...] = acc

    vm = (sum(_nbytes((tm, a.shape[1]), a.dtype) for a, _ in pieces) + _nbytes(win.shape, win.dtype)
          + 2 * _nbytes((tm, D), F32))
    return _pallas(
        body, tuple(a for a, _ in pieces) + (win,), name="du_all", grid=(T // tm,),
        in_specs=[pl.BlockSpec((tm, a.shape[1]), lambda i: (i, 0)) for a, _ in pieces]
        + [pl.BlockSpec(win.shape, lambda i: (0, 0))],
        out_specs=[pl.BlockSpec((tm, D), lambda i: (i, 0))],
        out_shape=[jax.ShapeDtypeStruct((T, D), F32)],
        semantics=("arbitrary",), vmem=int(min(VMEM_CAP, 2 * vm + (4 << 20))), side=side)


def _local_step(x, tgt, w, T, blk=1024, dist=None):
    blk = min(blk, T)
    nb = T // blk
    win = w["win"]

    u, = _ew(_norm_fn, T, 1024, [(x, D, 0)], [w["g_mix"]], [(D, BF)], [], name="norm_mix")
    xg = _mm(u, win, "nn", T, 2 * D, D, name="proj_lru", out_dtype=BF)
    qkv = _mm(u, win, "nn", T, 3 * D, D, name="proj_qkv", out_dtype=BF, b_off=(0, 2),
              epi=lambda acc: acc * jnp.where(pl.program_id(1) == 0, _C2, 1.0))
    gates = _mm(u, win, "nn", T, 2 * D, D, name="proj_gates", b_off=(0, 5), out_dtype=BF)
    fl = _mm(u, win, "nn", T, DH, D, name="proj_f", tn=DH, b_off=(0, 7 * D // DH))
    fcum = _fgate_fwd(fl, w["fb"], T)
    qa, ka = _ew(_aug_fn, T, 512, [(qkv, D, 0), (qkv, D, 1), (fcum, DH, 0)], [],
                 [(NH * DA, BF), (NH * DA, BF)], [], name="attn_augment")

    h = _lru_fwd(xg, w["cw"], w["vec"], w["wabd"], w["wxbd"], T)
    ob, lse, *landed = _attn_fwd(qa, ka, qkv, T, min(2 * blk, T), side=dist.weights_side() if dist else None)
    lse = lse.reshape(NH, nb, 1, blk)
    if dist:
        w = dict(w, **dist.weights_landed(landed))
    z, = _ew(_z_fn, T, 1024, [(xg, D, 1), (h, D, 0)], [], [(D, BF)], [], name="lru_gelu")
    ya = _mm(z, w["wa"], "nn", T, D, D, name="branch_a", out_dtype=BF)
    yb = _mm(ob, w["wb"], "nn", T, D, D, name="branch_b", out_dtype=BF)
    mix, = _ew(_mix_fn, T, 512, [(gates, 2 * D, 0), (ya, D, 0), (yb, D, 0)], [], [(D, BF)], [], name="mix")
    x1 = _mm(mix, w["wout"], "nn", T, D, D, name="out_proj", add=x)
    m, = _ew(_norm_fn, T, 1024, [(x1, D, 0)], [w["g_mlp"]], [(D, BF)], [], name="norm_mlp")
    hh = _mm(m, w["wup"], "nn", T, FF, D, name="mlp_up", out_dtype=BF,
             epi=lambda acc: jnp.square(jnp.maximum(acc, 0.0)))
    x2 = _mm(hh, w["wdown"], "nn", T, D, FF, name="mlp_down", add=x1, tk=FF)
    dx2, dx2b, loss_vec, dg_fin = _ew(_final_fn, T, 512, [(x2, D, 0), (tgt, D, 0)], [w["g_fin"]],
                                      [(D, F32), (D, BF)], [(1, D), (1, D)], name="final_norm_loss")

    dhpre = _mm(dx2b, w["wdown"], "nt", T, FF, D, name="mlp_down_bwd", out_dtype=BF,
                epi=lambda acc, h2: acc * (2.0 * jnp.sqrt(h2.astype(F32))), epi_ins=[hh])
    dwdown = _mm(hh, dx2b, "tn", FF, D, T, name="dw_down", out_dtype=BF)
    dwup = _mm(m, dhpre, "tn", D, FF, T, name="dw_up", out_dtype=BF)
    dm = _mm(dhpre, w["wup"], "nt", T, D, FF, name="mlp_up_bwd", tk=FF)
    dx1, dx1b, dg_mlp = _ew(_norm_bwd_fn, T, 512, [(x1, D, 0), (dm, D, 0), (dx2, D, 0)], [w["g_mlp"]],
                            [(D, F32), (D, BF)], [(1, D)], name="norm_mlp_bwd")

    dmix = _mm(dx1b, w["wout"], "nt", T, D, D, name="out_proj_bwd", out_dtype=BF)
    dwout = _mm(mix, dx1b, "tn", D, D, T, name="dw_out", out_dtype=BF)
    dya, dyb, dgates = _ew(_mix_bwd_fn, T, 512, [(dmix, D, 0), (gates, 2 * D, 0), (ya, D, 0), (yb, D, 0)], [],
                           [(D, BF), (D, BF), (2 * D, BF)], [], name="mix_bwd")
    dob = _mm(dyb, w["wb"], "nt", T, D, D, name="branch_b_bwd", out_dtype=BF)
    dwb = _mm(ob, dyb, "tn", D, D, T, name="dw_b", out_dtype=BF)
    dz = _mm(dya, w["wa"], "nt", T, D, D, name="branch_a_bwd", out_dtype=BF)
    dwa = _mm(z, dya, "tn", D, D, T, name="dw_a", out_dtype=BF)
    dha, dglru = _ew(_z_bwd_fn, T, 512, [(dz, D, 0), (xg, D, 1), (h, D, 0)], [], [(D, F32), (D, BF)], [],
                     name="lru_gelu_bwd")

    delta, = _ew(_delta_fn, T, 1024, [(dob, D, 0), (ob, D, 0)], [], [(DH, F32)], [], name="attn_delta")
    drow = delta[:, :NH].T.reshape(NH, nb, 1, blk)
    big = dict(w_branch_a=dwa, w_branch_b=dwb, w_out=dwout, w_up=dwup, w_down=dwdown)
    side = dist.grads_side(big) if dist else None
    dq, dk, dv, dfs, dft, *landed = _attn_bwd(qa, ka, qkv, dob, lse, drow, T, blk, side=side)
    if dist:
        big = dist.grads_landed(side, landed)
    dfcum = jnp.pad((dft - dfs).reshape(NH, T).T, ((0, 0), (0, DH - NH)))
    dfl, dfb = _fgate_bwd(dfcum, fl, w["fb"], T)

    dxl, dwabd, dwxbd, lacc = _lru_bwd(xg, h, dha, w["cw"], w["vec"], w["wabd"], w["wxbd"], T)

    dproj = ((dxl, 0), (dglru, D), (dq, 2 * D), (dk, 3 * D), (dv, 4 * D), (dgates, 5 * D), (dfl, 7 * D))
    pieces = [_mm(u, p, "tn", D, p.shape[1], T, name="dw_in_%d" % n, out_dtype=BF)
              for n, (p, _) in enumerate(dproj)]
    pieces[-1] = pieces[-1][:, :NH]
    dwin = dict(w_in=jnp.concatenate(pieces, axis=1))
    side = dist.grads_side(dwin) if dist else None
    du, *landed = _du_all(dproj, win, T, side=side)
    big.update(dist.grads_landed(side, landed) if dist else dwin)
    dx, dg_mix = _ew(_norm_bwd_fn, T, 512, [(x, D, 0), (du, D, 0), (dx1, D, 0)], [w["g_mix"]], [(D, F32)],
                     [(1, D)], name="norm_mix_bwd")

    return dict(dx=dx, big=big, dwabd=dwabd, dwxbd=dwxbd, lacc=lacc, dfb=dfb, dg_mix=dg_mix, dg_mlp=dg_mlp,
                dg_fin=dg_fin, loss_vec=loss_vec)


def _block_diag(w):
    per = BD // LRU_BW
    w4 = w.reshape(NBD, per, LRU_BW, LRU_BW)
    on_diagonal = jnp.eye(per, dtype=bool)[None, :, None, :, None]
    return jnp.where(on_diagonal, w4[:, :, :, None, :], 0.0).reshape(NBD, BD, BD)


def _block_diag_extract(wbd):
    per = BD // LRU_BW
    w5 = wbd.reshape(NBD, per, LRU_BW, per, LRU_BW)
    return jnp.stack([w5[:, b, :, b, :] for b in range(per)], axis=1).reshape(LRU_BLOCKS, LRU_BW, LRU_BW)


def _place():
    x, y, c = lax.axis_index("x"), lax.axis_index("y"), lax.axis_index("c")
    chips = [(1 - x, y), (x, 1 - y), (1 - x, 1 - y)]
    return x, y, c, chips


def _allgather_shards(shards):
    n = len(shards)

    def body(*refs):
        ins, outs = refs[:n], refs[n:2 * n]
        send_sems, recv_sems = refs[2 * n:]
        x, y, c, chips = _place()
        me = 2 * x + y
        sibling = (x, y, 1 - c)

        def remote(p, k, src, dst, to):
            return pltpu.make_async_remote_copy(src_ref=src, dst_ref=dst, send_sem=send_sems.at[p, k],
                                                recv_sem=recv_sems.at[p, k], device_id=to, device_id_type=MESH)

        sent = []
        for p in range(n):
            for k, chip in enumerate(chips):
                cp = remote(p, k, ins[p].at[c], outs[p].at[me, c], (chip[0], chip[1], c))
                cp.start()
                sent.append(cp)
        for p in range(n):
            for k, chip in enumerate(chips):
                half = outs[p].at[2 * chip[0] + chip[1], c]
                remote(p, k, half, half, sibling).wait_recv()
                fwd = remote(p, 3 + k, half, half, sibling)
                fwd.start()
                sent.append(fwd)
        for p in range(n):
            for k, chip in enumerate(chips):
                half = outs[p].at[2 * chip[0] + chip[1], 1 - c]
                remote(p, 3 + k, half, half, sibling).wait_recv()
        for cp in sent:
            cp.wait_send()

    gathered = pl.pallas_call(
        body, name="allgather_weights",
        in_specs=[_ANY] * n, out_specs=[_ANY] * n,
        out_shape=[jax.ShapeDtypeStruct((NCHIP,) + s.shape, s.dtype) for s in shards],
        scratch_shapes=[pltpu.SemaphoreType.DMA((n, 6)), pltpu.SemaphoreType.DMA((n, 6))],
    )(*shards)
    me = 2 * lax.axis_index("x") + lax.axis_index("y")
    return [lax.dynamic_update_index_in_dim(g, s, me, 0) for g, s in zip(gathered, shards)]


_LATE = ["w_branch_a", "w_branch_b", "w_out", "w_up", "w_down"]
_COLUMN_CUT = ("w_in", "w_up")
N_PEERS = 7


def _shard_major(name, g):
    s = _columns_to_shards(g) if name in _COLUMN_CUT else g.reshape(NCHIP, g.shape[0] // NCHIP, g.shape[1])
    return s.reshape(NCHIP, 2, s.shape[1] // 2, s.shape[2])


class _Exchanges:
    def __init__(self, shards):
        self.shards = shards

    def weights_side(self):
        srcs = [self.shards[n] for n in _LATE]

        def copies(sin, sout, send, recv):
            x, y, c, chips = _place()
            return [pltpu.make_async_remote_copy(
                src_ref=sin[p], dst_ref=sout[p].at[2 * x + y], send_sem=send.at[3 * p + k], recv_sem=recv.at[3 * p + k],
                device_id=(chip[0], chip[1], c), device_id_type=MESH)
                for p in range(len(sin)) for k, chip in enumerate(chips)]

        return _Side(srcs, [jax.ShapeDtypeStruct((NCHIP,) + s.shape, s.dtype) for s in srcs], 3 * len(srcs), copies)

    def weights_landed(self, landed):
        me = 2 * lax.axis_index("x") + lax.axis_index("y")
        full = {n: lax.dynamic_update_index_in_dim(g, self.shards[n], me, 0) for n, g in zip(_LATE, landed)}
        return dict(wa=full["w_branch_a"].reshape(D, D), wb=full["w_branch_b"].reshape(D, D),
                    wout=full["w_out"].reshape(D, D), wup=_shards_to_columns(full["w_up"]),
                    wdown=full["w_down"].reshape(FF, D))

    def grads_side(self, grads):
        side_names = list(grads)
        srcs = [_shard_major(n, grads[n]) for n in side_names]

        def copies(sin, sout, send, recv):
            x, y, c, chips = _place()
            peers = [(x, y, 1 - c)] + [(cx, cy, c) for cx, cy in chips] + [(cx, cy, 1 - c) for cx, cy in chips]
            return [pltpu.make_async_remote_copy(
                src_ref=sin[p].at[2 * px + py, pc], dst_ref=sout[p].at[s], send_sem=send.at[N_PEERS * p + s],
                recv_sem=recv.at[N_PEERS * p + s], device_id=(px, py, pc), device_id_type=MESH)
                for p in range(len(sin)) for s, (px, py, pc) in enumerate(peers)]

        side = _Side(srcs, [jax.ShapeDtypeStruct((N_PEERS,) + s.shape[2:], s.dtype) for s in srcs],
                     N_PEERS * len(srcs), copies)
        side.names = side_names
        return side

    def grads_landed(self, side, landed):
        return {n: (own, got) for n, own, got in zip(side.names, side.srcs, landed)}


def _add8(g, got, me, c, name):
    _, _, half, cols = g.shape
    th = _row_tile(half, 2 * cols)

    def body(me_ref, c_ref, g_ref, r_ref, o_ref):
        acc = g_ref[...].astype(F32)
        for s in range(N_PEERS):
            acc = acc + r_ref[s].astype(F32)
        o_ref[...] = acc

    return pl.pallas_call(
        body, name=name,
        grid_spec=pltpu.PrefetchScalarGridSpec(
            num_scalar_prefetch=2, grid=(half // th,),
            in_specs=[pl.BlockSpec((None, None, th, cols), lambda i, me_ref, c_ref: (me_ref[0], c_ref[0], i, 0)),
                      pl.BlockSpec((N_PEERS, th, cols), lambda i, me_ref, c_ref: (0, i, 0))],
            out_specs=pl.BlockSpec((th, cols), lambda i, me_ref, c_ref: (i, 0))),
        out_shape=jax.ShapeDtypeStruct((half, cols), F32),
    )(me, c, g, got)


def _share_halves(halves):
    n = len(halves)

    def body(*refs):
        ins, outs = refs[:n], refs[n:2 * n]
        send_sems, recv_sems = refs[2 * n:]
        x, y, c, _ = _place()
        sibling = (x, y, 1 - c)
        copies = []
        for p in range(n):
            cp = pltpu.make_async_remote_copy(src_ref=ins[p], dst_ref=outs[p], send_sem=send_sems.at[p],
                                              recv_sem=recv_sems.at[p], device_id=sibling, device_id_type=MESH)
            cp.start()
            copies.append(cp)
        for cp in copies:
            cp.wait()

    return pl.pallas_call(
        body, name="reduce_share_halves",
        in_specs=[_ANY] * n, out_specs=[_ANY] * n,
        out_shape=[jax.ShapeDtypeStruct(h.shape, h.dtype) for h in halves],
        scratch_shapes=[pltpu.SemaphoreType.DMA((n,)), pltpu.SemaphoreType.DMA((n,))],
    )(*halves)


def _row_tile(half, cols):
    th = max(SLAB, min(half, (1 << 18) // cols // SLAB * SLAB))
    while half % th:
        th -= SLAB
    return th


N_DEV = 8
SMALL_ROWS = 208


def _allreduce_small(pack):
    def body(x_ref, out_ref, gbuf, send_sems, recv_sems, local_sem):
        x, y, c, chips = _place()
        me, sibling = (x, y, c), (x, y, 1 - c)

        def rows(px, py, pc):
            return gbuf.at[4 * px + 2 * py + pc]

        def copy(k, block, to, src=None):
            return pltpu.make_async_remote_copy(
                src_ref=rows(*block) if src is None else src, dst_ref=rows(*block),
                send_sem=send_sems.at[k], recv_sem=recv_sems.at[k], device_id=to, device_id_type=MESH)

        mine = pltpu.make_async_copy(x_ref, rows(*me), local_sem)
        mine.start()
        first = [copy(0, me, sibling, src=x_ref)]
        first += [copy(1 + j, me, (chip[0], chip[1], c), src=x_ref) for j, chip in enumerate(chips)]
        for cp in first:
            cp.start()
        passed = [copy(4 + j, (chip[0], chip[1], c), sibling) for j, chip in enumerate(chips)]
        for j, chip in enumerate(chips):
            copy(1 + j, (chip[0], chip[1], c), me).wait_recv()
            passed[j].start()
        copy(0, sibling, me).wait_recv()
        for j, chip in enumerate(chips):
            copy(4 + j, (chip[0], chip[1], 1 - c), me).wait_recv()
        for cp in first + passed:
            cp.wait_send()
        mine.wait()
        acc = gbuf[0]
        for d in range(1, N_DEV):
            acc = acc + gbuf[d]
        out_ref[...] = acc

    return pl.pallas_call(
        body, name="allreduce_small",
        in_specs=[pl.BlockSpec(memory_space=pltpu.VMEM)],
        out_specs=pl.BlockSpec(memory_space=pltpu.VMEM),
        out_shape=jax.ShapeDtypeStruct((SMALL_ROWS, D), F32),
        scratch_shapes=[pltpu.VMEM((N_DEV, SMALL_ROWS, D), F32), pltpu.SemaphoreType.DMA((7,)),
                        pltpu.SemaphoreType.DMA((7,)), pltpu.SemaphoreType.DMA],
    )(pack)


def _adamw(w, g, m, v, name):
    rows, cols = w.shape

    def body(w_ref, g_ref, m_ref, v_ref, d_ref, mo_ref, vo_ref):
        gv = g_ref[...]
        mn = ADAM_B1 * m_ref[...] + (1.0 - ADAM_B1) * gv
        vn = ADAM_B2 * v_ref[...] + (1.0 - ADAM_B2) * (gv * gv)
        m_hat = mn / (1.0 - ADAM_B1 ** ADAM_STEP)
        v_hat = vn / (1.0 - ADAM_B2 ** ADAM_STEP)
        d_ref[...] = -ADAM_LR * (m_hat / (jnp.sqrt(v_hat) + ADAM_EPS) + ADAM_WD * w_ref[...])
        mo_ref[...] = mn
        vo_ref[...] = vn

    if rows % SLAB:
        spec, steps = pl.BlockSpec((rows, DH), lambda i: (0, i)), cols // DH
    else:
        th = _row_tile(rows, cols)
        spec, steps = pl.BlockSpec((th, cols), lambda i: (i, 0)), rows // th
    return pl.pallas_call(
        body, name=name, grid=(steps,),
        in_specs=[spec] * 4, out_specs=[spec] * 3,
        out_shape=[jax.ShapeDtypeStruct((rows, cols), F32)] * 3,
        compiler_params=pltpu.CompilerParams(dimension_semantics=("parallel",)),
    )(w, g, m, v)


_SMALL = ["norm_mix_g", "norm_mlp_g", "norm_final_g", "conv_b", "lru_ba", "lru_bx", "lru_lambda"]
_ROW_FB, _ROW_CW, _ROW_WA, _ROW_WX, _ROW_LOSS = 56, 64, 72, 136, 200


def _pack_small(vals, col0):
    def slab(a):
        return jnp.pad(a, ((0, -a.shape[0] % SLAB), (0, D - a.shape[1])))

    rows = [slab(vals[n].reshape(1, D)) for n in _SMALL]
    rows.append(slab(vals["forget_b"].reshape(1, NH)))
    if vals["conv_w"].shape[1] == D:
        rows.append(slab(vals["conv_w"]))
    else:
        rows.append(slab(lax.dynamic_update_slice(jnp.zeros((CONV, D), F32), vals["conv_w"], (0, col0))))
    rows.append(vals["lru_wa"].reshape(LRU_BLOCKS * LRU_BW * LRU_BW // D, D))
    rows.append(vals["lru_wx"].reshape(LRU_BLOCKS * LRU_BW * LRU_BW // D, D))
    rows.append(slab(vals["loss"]) if "loss" in vals else jnp.zeros((SLAB, D), F32))
    return jnp.concatenate(rows, axis=0)


def _unpack_small(pack, col0):
    out = {n: pack[SLAB * i] for i, n in enumerate(_SMALL)}
    out["forget_b"] = pack[_ROW_FB, :NH]
    out["conv_w"] = lax.dynamic_slice(pack[_ROW_CW:_ROW_CW + CONV], (0, col0), (CONV, D // NCHIP))
    out["lru_wa"] = pack[_ROW_WA:_ROW_WX].reshape(LRU_BLOCKS, LRU_BW, LRU_BW)
    out["lru_wx"] = pack[_ROW_WX:_ROW_LOSS].reshape(LRU_BLOCKS, LRU_BW, LRU_BW)
    return out


_WEIGHTS = ["norm_mix_g", "w_in", "conv_w", "conv_b", "lru_wa", "lru_ba", "lru_wx", "lru_bx", "lru_lambda",
            "forget_b", "w_branch_a", "w_branch_b", "w_out", "norm_mlp_g", "w_up", "w_down", "norm_final_g"]
_BIG = ["w_in", "w_branch_a", "w_branch_b", "w_out", "w_up", "w_down"]


def _halves(a):
    return a.reshape(2, a.shape[0] // 2, a.shape[1])


def _columns_to_shards(a):
    rows, cols = a.shape[0], a.shape[1] // NCHIP
    return jnp.transpose(a.reshape(rows, NCHIP, cols), (1, 0, 2))


def _shards_to_columns(a):
    n, rows, cols = a.shape
    return jnp.transpose(a, (1, 0, 2)).reshape(rows, n * cols)


def kernel(x, norm_mix_g, w_in, conv_w, conv_b, lru_wa, lru_ba, lru_wx, lru_bx, lru_lambda, forget_b, w_branch_a, w_branch_b, w_out, norm_mlp_g, w_up, w_down, norm_final_g, loss_target, m_norm_mix_g, m_w_in, m_conv_w, m_conv_b, m_lru_wa, m_lru_ba, m_lru_wx, m_lru_bx, m_lru_lambda, m_forget_b, m_w_branch_a, m_w_branch_b, m_w_out, m_norm_mlp_g, m_w_up, m_w_down, m_norm_final_g, v_norm_mix_g, v_w_in, v_conv_w, v_conv_b, v_lru_wa, v_lru_ba, v_lru_wx, v_lru_bx, v_lru_lambda, v_forget_b, v_w_branch_a, v_w_branch_b, v_w_out, v_norm_mlp_g, v_w_up, v_w_down, v_norm_final_g):
    args = dict(locals())
    wts = {n: args[n] for n in _WEIGHTS}
    mom = {n: args["m_" + n] for n in _WEIGHTS}
    var = {n: args["v_" + n] for n in _WEIGHTS}
    T = x.shape[1]
    xi, yi, ci = lax.axis_index("x"), lax.axis_index("y"), lax.axis_index("c")
    me = 2 * xi + yi
    c1 = jnp.reshape(ci, (1,)).astype(jnp.int32)
    me1 = jnp.reshape(me, (1,)).astype(jnp.int32)
    col0 = me * (D // NCHIP)

    cw_pad = jnp.pad(conv_w, ((0, 4 * SLAB - CONV), (0, 0)))
    g_in, g_cw = _allgather_shards([_halves(w_in.astype(BF)), _halves(cw_pad)])
    cin = DIN // NCHIP
    win = _shards_to_columns(g_in.reshape(NCHIP, D, cin))
    w = dict(
        win=jnp.pad(win, ((0, 0), (0, DINP - DIN))),
        cw=_shards_to_columns(g_cw.reshape(NCHIP, 4 * SLAB, D // NCHIP)[:, :CONV]),
        vec=jnp.concatenate([conv_b[None], lru_ba[None], lru_bx[None], lru_lambda[None],
                             jnp.zeros((SLAB - 4, D), F32)], axis=0),
        fb=jnp.pad(forget_b[None], ((0, 0), (0, DH - NH))),
        wabd=_block_diag(lru_wa).astype(BF), wxbd=_block_diag(lru_wx).astype(BF),
        g_mix=norm_mix_g[None], g_mlp=norm_mlp_g[None], g_fin=norm_final_g[None])

    r = _local_step(x[0], loss_target[0], w, T, dist=_Exchanges({n: wts[n].astype(BF) for n in _LATE}))

    halves = [_add8(*r["big"][n], me1, c1, "add8_" + n) for n in _BIG]
    theirs = _share_halves(halves)
    low = ci == 0
    gsum = {n: jnp.concatenate([jnp.where(low, h, t), jnp.where(low, t, h)], axis=0)
            for n, h, t in zip(_BIG, halves, theirs)}
    lacc = r["lacc"]
    small = dict(norm_mix_g=r["dg_mix"], norm_mlp_g=r["dg_mlp"], norm_final_g=r["dg_fin"], conv_b=lacc[3],
                 lru_ba=lacc[0], lru_bx=lacc[1], lru_lambda=lacc[2], forget_b=r["dfb"][0, :NH],
                 conv_w=lacc[4:4 + CONV], lru_wa=_block_diag_extract(r["dwabd"]),
                 lru_wx=_block_diag_extract(r["dwxbd"]), loss=r["loss_vec"])
    gpack = _allreduce_small(_pack_small(small, col0))
    loss = jnp.sum(gpack[_ROW_LOSS])

    grads, delta, new_m, new_v = {}, {}, {}, {}
    for n in _BIG:
        if n == "w_in":
            gt = gsum[n].T
            grads[n] = gt.T
            delta[n], new_m[n], new_v[n] = [a.T for a in _adamw(wts[n].T, gt, mom[n].T, var[n].T, "adamw_" + n)]
        else:
            grads[n] = gsum[n]
            delta[n], new_m[n], new_v[n] = _adamw(wts[n], gsum[n], mom[n], var[n], "adamw_" + n)
    dp, mp, vp = _adamw(_pack_small(wts, col0), gpack, _pack_small(mom, col0), _pack_small(var, col0), "adamw_small")
    for dst, pack in ((grads, gpack), (delta, dp), (new_m, mp), (new_v, vp)):
        dst.update(_unpack_small(pack, col0))
    return (loss, r["dx"][None], *[grads[n] for n in _WEIGHTS], *[delta[n] for n in _WEIGHTS],
            *[new_m[n] for n in _WEIGHTS], *[new_v[n] for n in _WEIGHTS])
```

```python
import functools
import math

import jax
import jax.numpy as jnp
import numpy as np
from jax import lax
from jax.experimental import pallas as pl
from jax.experimental.pallas import tpu as pltpu

F32 = jnp.float32
BF = jnp.bfloat16

D = 1024
NH = 8
DH = 128
FF = 4096
CONV = 4
LRU_BLOCKS = 16
LRU_BW = 64
BD = 256
NBD = D // BD
LRU_C = 8.0
EPS = 1e-6
DIN = 7176
DINP = 7296
NCHIP = 4
SLAB = 8
VMEM_CAP = 60 * 1024 * 1024

ADAM_LR = 0.001
ADAM_B1 = 0.9
ADAM_B2 = 0.999
ADAM_EPS = 1e-08
ADAM_WD = 0.01
ADAM_STEP = 10

MESH = pl.DeviceIdType.MESH


def _vmem_limit(nbytes):
    return int(min(VMEM_CAP, max(32 * 1024 * 1024, 3 * nbytes)))


def _nbytes(shape, dtype):
    return int(np.prod(shape)) * jnp.dtype(dtype).itemsize


def _sig(x):
    return 0.5 * jnp.tanh(0.5 * x) + 0.5


def _log1p(u):
    w = 1.0 + u
    return jnp.where(w == 1.0, u, jnp.log(w) * (u / (w - 1.0)))


def _one_minus_sq(a, la):
    return jnp.tanh(-la) * (1.0 + a * a)


def _softplus(z):
    return jnp.maximum(z, 0.0) + _log1p(jnp.exp(-jnp.abs(z)))


_GELU_C = math.sqrt(2.0 / math.pi)


def _gelu(x):
    return 0.5 * x * (1.0 + jnp.tanh(_GELU_C * (x + 0.044715 * x * x * x)))


def _gelu_grad(x):
    t = jnp.tanh(_GELU_C * (x + 0.044715 * x * x * x))
    return 0.5 * (1.0 + t) + 0.5 * x * (1.0 - t * t) * _GELU_C * (1.0 + 3.0 * 0.044715 * x * x)


def _slab_scan_fwd(a, b):
    row = lax.broadcasted_iota(jnp.int32, a.shape, 0)
    for k in (1, 2, 4):
        a_s = pltpu.roll(a, k, 0)
        b_s = pltpu.roll(b, k, 0)
        m = row >= k
        b = jnp.where(m, a * b_s + b, b)
        a = jnp.where(m, a * a_s, a)
    return a, b


def _slab_scan_bwd(a, b):
    row = lax.broadcasted_iota(jnp.int32, a.shape, 0)
    for k in (1, 2, 4):
        a_s = pltpu.roll(a, SLAB - k, 0)
        b_s = pltpu.roll(b, SLAB - k, 0)
        m = row < SLAB - k
        b = jnp.where(m, a * b_s + b, b)
        a = jnp.where(m, a * a_s, a)
    return a, b


_DIMS = {"nn": (((1,), (0,)), ((), ())), "nt": (((1,), (1,)), ((), ())), "tn": (((0,), (0,)), ((), ()))}


def _dot(a, b, mode="nn"):
    return lax.dot_general(a, b, _DIMS[mode], preferred_element_type=F32)


def _mm(a, b, mode, M, N, K, *, name, out_dtype=F32, tm=None, tn=1024, tk=1024,
        a_off=(0, 0), b_off=(0, 0), add=None, epi=None, epi_ins=()):
    if tm is None:
        tm = 2048 if (K <= tk and out_dtype == BF and add is None) else 1024
    tm, tn, tk = min(tm, M), min(tn, N), min(tk, K)
    nk = K // tk
    grid = (M // tm, N // tn, nk)
    if mode == "nn":
        a_spec = pl.BlockSpec((tm, tk), lambda i, j, k: (i + a_off[0], k + a_off[1]))
        b_spec = pl.BlockSpec((tk, tn), lambda i, j, k: (k + b_off[0], j + b_off[1]))
    elif mode == "nt":
        a_spec = pl.BlockSpec((tm, tk), lambda i, j, k: (i + a_off[0], k + a_off[1]))
        b_spec = pl.BlockSpec((tn, tk), lambda i, j, k: (j + b_off[0], k + b_off[1]))
    else:
        a_spec = pl.BlockSpec((tk, tm), lambda i, j, k: (k + a_off[0], i + a_off[1]))
        b_spec = pl.BlockSpec((tk, tn), lambda i, j, k: (k + b_off[0], j + b_off[1]))
    o_spec = pl.BlockSpec((tm, tn), lambda i, j, k: (i, j))
    extra = ([add] if add is not None else []) + list(epi_ins)
    n_extra = len(extra)
    has_add = add is not None

    def body(*refs):
        a_ref, b_ref = refs[0], refs[1]
        ex = refs[2:2 + n_extra]
        o_ref = refs[2 + n_extra]

        def finish(acc):
            if has_add:
                acc = acc + ex[0][...].astype(F32)
            if epi is not None:
                acc = epi(acc, *[e[...] for e in ex[(1 if has_add else 0):]])
            o_ref[...] = acc.astype(o_ref.dtype)

        p = _dot(a_ref[...].astype(BF), b_ref[...].astype(BF), mode)
        if nk == 1:
            finish(p)
        else:
            acc_ref = refs[3 + n_extra]
            k = pl.program_id(2)

            @pl.when(k == 0)
            def _():
                acc_ref[...] = p

            @pl.when(k > 0)
            def _():
                acc_ref[...] += p

            @pl.when(k == nk - 1)
            def _():
                finish(acc_ref[...])

    blk = (_nbytes((tm, tk), a.dtype) + _nbytes((tk, tn), b.dtype) + _nbytes((tm, tn), out_dtype)
           + sum(_nbytes((tm, tn), e.dtype) for e in extra) + 2 * _nbytes((tm, tn), F32))
    return pl.pallas_call(
        body, name=name, grid=grid,
        in_specs=[a_spec, b_spec] + [o_spec] * n_extra,
        out_specs=o_spec,
        out_shape=jax.ShapeDtypeStruct((M, N), out_dtype),
        scratch_shapes=[pltpu.VMEM((tm, tn), F32)] if nk > 1 else [],
        compiler_params=pltpu.CompilerParams(
            dimension_semantics=("parallel", "parallel", "arbitrary"), vmem_limit_bytes=_vmem_limit(blk)),
    )(a, b, *extra)


def _ew(fn, T, tm, ins, consts, outs, accs, *, name, reverse=False):
    tm = min(tm, T)
    nt = T // tm
    n_in, n_c, n_o, n_a = len(ins), len(consts), len(outs), len(accs)

    def row(i):
        return nt - 1 - i if reverse else i

    in_specs = [pl.BlockSpec((tm, w), functools.partial(lambda i, cb: (row(i), cb), cb=cb)) for (_, w, cb) in ins]
    in_specs += [pl.BlockSpec(c.shape, functools.partial(lambda i, nd: (0,) * nd, nd=c.ndim)) for c in consts]
    out_specs = [pl.BlockSpec((tm, w), lambda i: (row(i), 0)) for (w, _) in outs]
    out_specs += [pl.BlockSpec((r, w), lambda i: (0, 0)) for (r, w) in accs]
    out_shape = [jax.ShapeDtypeStruct((T, w), dt) for (w, dt) in outs]
    out_shape += [jax.ShapeDtypeStruct((r, w), F32) for (r, w) in accs]

    def body(*refs):
        in_refs = refs[:n_in]
        c_refs = refs[n_in:n_in + n_c]
        o_refs = refs[n_in + n_c:n_in + n_c + n_o]
        a_refs = refs[n_in + n_c + n_o:]
        ov, av = fn([r[...] for r in in_refs], [r[...] for r in c_refs])
        for r, v in zip(o_refs, ov):
            r[...] = v.astype(r.dtype)
        if n_a:
            i = pl.program_id(0)

            @pl.when(i == 0)
            def _():
                for r, v in zip(a_refs, av):
                    r[...] = v

            @pl.when(i > 0)
            def _():
                for r, v in zip(a_refs, av):
                    r[...] += v

    blk = (sum(_nbytes((tm, w), a.dtype) for (a, w, _) in ins) + sum(_nbytes(c.shape, c.dtype) for c in consts)
           + sum(_nbytes((tm, w), dt) for (w, dt) in outs) + sum(_nbytes(s, F32) for s in accs))
    res = pl.pallas_call(
        body, name=name, grid=(nt,), in_specs=in_specs, out_specs=out_specs, out_shape=out_shape,
        compiler_params=pltpu.CompilerParams(
            dimension_semantics=("arbitrary",), vmem_limit_bytes=_vmem_limit(blk)),
    )(*[a for (a, _, _) in ins], *consts)
    return res


def _colsum(v):
    return jnp.sum(v, axis=0, keepdims=True)


FGATE_GROUP = 4


def _fgate_fwd(fl, fb, T, tm=512):
    tm = min(tm, T)

    def body(fl_ref, fb_ref, f_ref, carry_ref):
        i = pl.program_id(0)

        @pl.when(i == 0)
        def _():
            carry_ref[...] = jnp.zeros_like(carry_ref)

        rows = FGATE_GROUP * SLAB
        sub = lax.broadcasted_iota(jnp.int32, (rows, DH), 0) % SLAB

        def group(s, carry):
            r0 = pl.multiple_of(s * rows, rows)
            z = fl_ref[pl.ds(r0, rows), :] + fb_ref[...]
            c = jnp.minimum(z, 0.0) - _log1p(jnp.exp(-jnp.abs(z)))
            for k in (1, 2, 4):
                c = c + jnp.where(sub >= k, pltpu.roll(c, k, 0), 0.0)
            for u in range(FGATE_GROUP):
                cu = c[u * SLAB:(u + 1) * SLAB] + carry
                f_ref[pl.ds(r0 + u * SLAB, SLAB), :] = cu
                carry = cu[SLAB - 1:SLAB, :]
            return carry

        carry_ref[0:1, :] = lax.fori_loop(0, tm // rows, group, carry_ref[0:1, :])

    return pl.pallas_call(
        body, name="fgate_fwd", grid=(T // tm,),
        in_specs=[pl.BlockSpec((tm, DH), lambda i: (i, 0)), pl.BlockSpec((1, DH), lambda i: (0, 0))],
        out_specs=pl.BlockSpec((tm, DH), lambda i: (i, 0)),
        out_shape=jax.ShapeDtypeStruct((T, DH), F32),
        scratch_shapes=[pltpu.VMEM((SLAB, DH), F32)],
        compiler_params=pltpu.CompilerParams(dimension_semantics=("arbitrary",)),
    )(fl, fb)


def _fgate_bwd(dF, fl, fb, T, tm=512):
    tm = min(tm, T)
    nt = T // tm

    def body(df_ref, fl_ref, fb_ref, o_ref, acc_ref, carry_ref):
        i = pl.program_id(0)

        @pl.when(i == 0)
        def _():
            carry_ref[...] = jnp.zeros_like(carry_ref)
            acc_ref[...] = jnp.zeros_like(acc_ref)

        rows = FGATE_GROUP * SLAB
        sub = lax.broadcasted_iota(jnp.int32, (rows, DH), 0) % SLAB

        def group(n, carry):
            g_next, acc = carry
            r0 = pl.multiple_of((tm // rows - 1 - n) * rows, rows)
            c = df_ref[pl.ds(r0, rows), :]
            for k in (1, 2, 4):
                c = c + jnp.where(sub < SLAB - k, pltpu.roll(c, rows - k, 0), 0.0)
            sg = _sig(-(fl_ref[pl.ds(r0, rows), :] + fb_ref[...]))
            for u in reversed(range(FGATE_GROUP)):
                cu = c[u * SLAB:(u + 1) * SLAB] + g_next
                dfl = cu * sg[u * SLAB:(u + 1) * SLAB]
                o_ref[pl.ds(r0 + u * SLAB, SLAB), :] = dfl.astype(o_ref.dtype)
                acc = acc + _colsum(dfl)
                g_next = cu[0:1, :]
            return g_next, acc

        g, acc = lax.fori_loop(0, tm // rows, group, (carry_ref[0:1, :], jnp.zeros((1, DH), F32)))
        carry_ref[0:1, :] = g
        acc_ref[...] += acc

    return pl.pallas_call(
        body, name="fgate_bwd", grid=(nt,),
        in_specs=[pl.BlockSpec((tm, DH), lambda i: (nt - 1 - i, 0)), pl.BlockSpec((tm, DH), lambda i: (nt - 1 - i, 0)),
                  pl.BlockSpec((1, DH), lambda i: (0, 0))],
        out_specs=[pl.BlockSpec((tm, DH), lambda i: (nt - 1 - i, 0)), pl.BlockSpec((1, DH), lambda i: (0, 0))],
        out_shape=[jax.ShapeDtypeStruct((T, DH), BF), jax.ShapeDtypeStruct((1, DH), F32)],
        scratch_shapes=[pltpu.VMEM((SLAB, DH), F32)],
        compiler_params=pltpu.CompilerParams(dimension_semantics=("arbitrary",)),
    )(dF, fl, fb)


def _shift_down(x, d, prev8):
    n = x.shape[0]
    row8 = lax.broadcasted_iota(jnp.int32, (SLAB, x.shape[1]), 0)
    y = pltpu.roll(x, d, 0)
    top = jnp.where(row8 < d, pltpu.roll(prev8, d, 0), y[0:SLAB])
    return jnp.concatenate([top, y[SLAB:]], axis=0)


def _shift_up(x, d, next8):
    n = x.shape[0]
    row8 = lax.broadcasted_iota(jnp.int32, (SLAB, x.shape[1]), 0)
    y = pltpu.roll(x, n - d, 0)
    bottom = jnp.where(row8 >= SLAB - d, pltpu.roll(next8, SLAB - d, 0), y[n - SLAB:])
    return jnp.concatenate([y[:n - SLAB], bottom], axis=0)


def _conv(x, prev8, cw, cb):
    xs = [x] + [_shift_down(x, d, prev8) for d in (1, 2, 3)]
    xa = cb + cw[3:4, :] * xs[0] + cw[2:3, :] * xs[1] + cw[1:2, :] * xs[2] + cw[0:1, :] * xs[3]
    return xa, xs


def _lru_gates(xa_g, wa_g, wx_g, ba_g, bx_g, sp_g):
    xb = xa_g.astype(BF)
    r = _sig(_dot(xb, wa_g) + ba_g)
    ig = _sig(_dot(xb, wx_g) + bx_g)
    la = -LRU_C * r * sp_g
    a = jnp.exp(la)
    mult = jnp.sqrt(_one_minus_sq(a, la))
    return r, ig, a, mult


def _lru_fwd(xg, cw, vec, wabd, wxbd, T, tm=256):
    tm = min(tm, T)
    nsl = tm // SLAB

    def body(x_ref, xp_ref, cw_ref, vec_ref, wa_ref, wx_ref, h_ref, a_s, b_s, carry_ref):
        i = pl.program_id(0)

        @pl.when(i == 0)
        def _():
            carry_ref[...] = jnp.zeros_like(carry_ref)

        x = x_ref[...].astype(F32)
        prev8 = jnp.where(i > 0, xp_ref[SLAB:, :].astype(F32), 0.0)
        vec_v = vec_ref[...]
        xa, _ = _conv(x, prev8, cw_ref[...], vec_v[0:1, :])
        sp = _softplus(-vec_v[3:4, :])
        for g in range(NBD):
            sl = slice(g * BD, (g + 1) * BD)
            _, ig, a, mult = _lru_gates(xa[:, sl], wa_ref[g], wx_ref[g], vec_v[1:2, sl], vec_v[2:3, sl], sp[:, sl])
            a_s[:, sl] = a
            b_s[:, sl] = mult * ig * xa[:, sl]

        def slab(s, carry):
            r0 = pl.multiple_of(s * SLAB, SLAB)
            A, B = _slab_scan_fwd(a_s[pl.ds(r0, SLAB), :], b_s[pl.ds(r0, SLAB), :])
            h = A * carry + B
            h_ref[pl.ds(r0, SLAB), :] = h
            return h[SLAB - 1:SLAB, :]

        carry_ref[0:1, :] = lax.fori_loop(0, nsl, slab, carry_ref[0:1, :])

    blk = 5 * _nbytes((tm, D), F32) + 2 * _nbytes((NBD, BD, BD), BF)
    return pl.pallas_call(
        body, name="lru_fwd", grid=(T // tm,),
        in_specs=[pl.BlockSpec((tm, D), lambda i: (i, 0)),
                  pl.BlockSpec((2 * SLAB, D), lambda i: (jnp.maximum(i * (nsl // 2) - 1, 0), 0)),
                  pl.BlockSpec((CONV, D), lambda i: (0, 0)),
                  pl.BlockSpec((SLAB, D), lambda i: (0, 0)),
                  pl.BlockSpec((NBD, BD, BD), lambda i: (0, 0, 0)),
                  pl.BlockSpec((NBD, BD, BD), lambda i: (0, 0, 0))],
        out_specs=pl.BlockSpec((tm, D), lambda i: (i, 0)),
        out_shape=jax.ShapeDtypeStruct((T, D), F32),
        scratch_shapes=[pltpu.VMEM((tm, D), F32), pltpu.VMEM((tm, D), F32), pltpu.VMEM((SLAB, D), F32)],
        compiler_params=pltpu.CompilerParams(dimension_semantics=("arbitrary",), vmem_limit_bytes=_vmem_limit(blk)),
    )(xg, xg, cw, vec, wabd, wxbd)


def _lru_bwd(xg, h, dha, cw, vec, wabd, wxbd, T, tm=256):
    tm = min(tm, T)
    nsl = tm // SLAB
    nt = T // tm

    def body(x_ref, xp_ref, h_ref, hp_ref, dh_ref, cw_ref, vec_ref, wa_ref, wx_ref,
             dx_ref, dwa_ref, dwx_ref, acc_ref, a_s, b_s, g_s, dxa_s, carry_ref, dxan_ref):
        n = pl.program_id(0)
        it = nt - 1 - n

        @pl.when(n == 0)
        def _():
            carry_ref[...] = jnp.zeros_like(carry_ref)
            dxan_ref[...] = jnp.zeros_like(dxan_ref)
            dwa_ref[...] = jnp.zeros_like(dwa_ref)
            dwx_ref[...] = jnp.zeros_like(dwx_ref)
            acc_ref[...] = jnp.zeros_like(acc_ref)

        x = x_ref[...].astype(F32)
        prev8 = jnp.where(it > 0, xp_ref[SLAB:, :].astype(F32), 0.0)
        hprev8 = jnp.where(it > 0, hp_ref[...], 0.0)
        vec_v = vec_ref[...]
        cw_v = cw_ref[...]
        xa, xs = _conv(x, prev8, cw_v, vec_v[0:1, :])
        sp = _softplus(-vec_v[3:4, :])
        gates = []
        for g in range(NBD):
            sl = slice(g * BD, (g + 1) * BD)
            r, ig, a, mult = _lru_gates(xa[:, sl], wa_ref[g], wx_ref[g], vec_v[1:2, sl], vec_v[2:3, sl], sp[:, sl])
            gates.append((r, ig, a, mult))
            a_s[:, sl] = a
        a_s[...] = _shift_up(a_s[...], 1, carry_ref[...])
        b_s[...] = dh_ref[...]

        def slab(m, carry):
            r0 = pl.multiple_of((nsl - 1 - m) * SLAB, SLAB)
            A, B = _slab_scan_bwd(a_s[pl.ds(r0, SLAB), :], b_s[pl.ds(r0, SLAB), :])
            gg = A * carry + B
            g_s[pl.ds(r0, SLAB), :] = gg
            return gg[0:1, :]

        g_first = lax.fori_loop(0, nsl, slab, carry_ref[1:2, :])
        gt = g_s[...]
        h_prev = _shift_down(h_ref[...], 1, hprev8)
        dba = []
        dbx = []
        dsp = []
        for g in range(NBD):
            sl = slice(g * BD, (g + 1) * BD)
            r, ig, a, mult = gates[g]
            xa_g = xa[:, sl]
            g_g = gt[:, sl]
            da = g_g * h_prev[:, sl]
            dmult = g_g * ig * xa_g
            di = g_g * mult * xa_g
            dxa_g = g_g * mult * ig
            dla = da * a - dmult * (a * a / mult)
            dr = dla * (-LRU_C) * sp[:, sl]
            dsp.append(_colsum(dla * (-LRU_C) * r))
            dra = (dr * r * (1.0 - r))
            dix = (di * ig * (1.0 - ig))
            dba.append(_colsum(dra))
            dbx.append(_colsum(dix))
            dra_b = dra.astype(BF)
            dix_b = dix.astype(BF)
            xb = xa_g.astype(BF)
            dxa_g = dxa_g + _dot(dra_b, wa_ref[g], "nt") + _dot(dix_b, wx_ref[g], "nt")
            dwa_ref[g] += _dot(xb, dra_b, "tn")
            dwx_ref[g] += _dot(xb, dix_b, "tn")
            dxa_s[:, sl] = dxa_g
        dxa = dxa_s[...]
        nxt = dxan_ref[...]
        dx = (cw_v[3:4, :] * dxa + cw_v[2:3, :] * _shift_up(dxa, 1, nxt)
              + cw_v[1:2, :] * _shift_up(dxa, 2, nxt) + cw_v[0:1, :] * _shift_up(dxa, 3, nxt))
        dx_ref[...] = dx.astype(dx_ref.dtype)
        acc_ref[0:1, :] += jnp.concatenate(dba, axis=1)
        acc_ref[1:2, :] += jnp.concatenate(dbx, axis=1)
        acc_ref[2:3, :] += jnp.concatenate(dsp, axis=1)
        acc_ref[3:4, :] += _colsum(dxa)
        for k in range(CONV):
            acc_ref[4 + k:5 + k, :] += _colsum(dxa * xs[CONV - 1 - k])
        dxan_ref[...] = dxa[0:SLAB, :]
        a_first = jnp.concatenate([gates[g][2][0:1, :] for g in range(NBD)], axis=1)
        carry_ref[0:1, :] = a_first
        carry_ref[1:2, :] = g_first

        @pl.when(n == nt - 1)
        def _():
            acc_ref[2:3, :] = acc_ref[2:3, :] * (-_sig(-vec_v[3:4, :]))

    rowblk = lambda i: (nt - 1 - i, 0)
    prevblk = lambda i: (jnp.maximum((nt - 1 - i) * nsl - 1, 0), 0)
    c2 = lambda i: (0, 0)
    c3 = lambda i: (0, 0, 0)
    blk = 12 * _nbytes((tm, D), F32) + 6 * _nbytes((NBD, BD, BD), F32)
    return pl.pallas_call(
        body, name="lru_bwd", grid=(nt,),
        in_specs=[pl.BlockSpec((tm, D), rowblk),
                  pl.BlockSpec((2 * SLAB, D), lambda i: (jnp.maximum((nt - 1 - i) * (nsl // 2) - 1, 0), 0)),
                  pl.BlockSpec((tm, D), rowblk), pl.BlockSpec((SLAB, D), prevblk),
                  pl.BlockSpec((tm, D), rowblk),
                  pl.BlockSpec((CONV, D), c2), pl.BlockSpec((SLAB, D), c2),
                  pl.BlockSpec((NBD, BD, BD), c3), pl.BlockSpec((NBD, BD, BD), c3)],
        out_specs=[pl.BlockSpec((tm, D), rowblk), pl.BlockSpec((NBD, BD, BD), c3), pl.BlockSpec((NBD, BD, BD), c3),
                   pl.BlockSpec((16, D), c2)],
        out_shape=[jax.ShapeDtypeStruct((T, D), BF), jax.ShapeDtypeStruct((NBD, BD, BD), F32),
                   jax.ShapeDtypeStruct((NBD, BD, BD), F32), jax.ShapeDtypeStruct((16, D), F32)],
        scratch_shapes=[pltpu.VMEM((tm, D), F32), pltpu.VMEM((tm, D), F32), pltpu.VMEM((tm, D), F32),
                        pltpu.VMEM((tm, D), F32), pltpu.VMEM((SLAB, D), F32), pltpu.VMEM((SLAB, D), F32)],
        compiler_params=pltpu.CompilerParams(dimension_semantics=("arbitrary",), vmem_limit_bytes=_vmem_limit(blk)),
    )(xg, xg, h, h, dha, cw, vec, wabd, wxbd)


_SCALE = 1.0 / math.sqrt(DH)


_ANY = pl.BlockSpec(memory_space=pl.ANY)


class _Side:
    def __init__(self, srcs, outs, nsem, copies):
        self.srcs, self.outs, self.nsem, self.copies = list(srcs), list(outs), nsem, copies


def _pallas(body, operands, *, name, grid, in_specs, out_specs, out_shape, scratch_shapes=(), semantics,
            vmem=None, side=None):
    if side is None:
        return pl.pallas_call(
            body, name=name, grid=grid, in_specs=in_specs, out_specs=out_specs, out_shape=out_shape,
            scratch_shapes=list(scratch_shapes),
            compiler_params=pltpu.CompilerParams(dimension_semantics=semantics, vmem_limit_bytes=vmem),
        )(*operands)
    n_in, n_out, n_scr = len(in_specs), len(out_specs), len(scratch_shapes)
    ns, no = len(side.srcs), len(side.outs)

    def hosted(*refs):
        ins, refs = refs[:n_in], refs[n_in:]
        sin, refs = refs[:ns], refs[ns:]
        outs, refs = refs[:n_out], refs[n_out:]
        sout, refs = refs[:no], refs[no:]
        scr, (send, recv) = refs[:n_scr], refs[n_scr:]
        ids = [pl.program_id(a) for a in range(len(grid))]
        first = functools.reduce(jnp.logical_and, [i == 0 for i in ids])
        last = functools.reduce(jnp.logical_and, [i == g - 1 for i, g in zip(ids, grid)])

        @pl.when(first)
        def _():
            for cp in side.copies(sin, sout, send, recv):
                cp.start()

        body(*ins, *outs, *scr)

        @pl.when(last)
        def _():
            for cp in side.copies(sin, sout, send, recv):
                cp.wait()

    return pl.pallas_call(
        hosted, name=name, grid=grid, in_specs=list(in_specs) + [_ANY] * ns, out_specs=list(out_specs) + [_ANY] * no,
        out_shape=list(out_shape) + side.outs,
        scratch_shapes=list(scratch_shapes) + [pltpu.SemaphoreType.DMA((side.nsem,)), pltpu.SemaphoreType.DMA((side.nsem,))],
        compiler_params=pltpu.CompilerParams(dimension_semantics=("arbitrary",) * len(grid), vmem_limit_bytes=vmem),
    )(*operands, *side.srcs)


DA = 2 * DH
_LOG2E = math.log2(math.e)
_C2 = _SCALE * _LOG2E


def _aug_fn(ins, cs):
    q, k, fcum = ins
    g_all = fcum * _LOG2E
    lane = lax.broadcasted_iota(jnp.int32, (q.shape[0], DH), 1)
    qa, ka = [], []
    for hd in range(NH):
        g = g_all[:, hd:hd + 1]
        hi = g.astype(BF).astype(F32)
        mid = (g - hi).astype(BF).astype(F32)
        lo = ((g - hi) - mid).astype(BF).astype(F32)
        qx = jnp.where(lane == 0, hi, jnp.where(lane == 1, mid, jnp.where(lane == 2, lo,
                                                                          jnp.where(lane < 6, 1.0, 0.0))))
        kx = jnp.where(lane < 3, 1.0, jnp.where(lane == 3, -hi, jnp.where(lane == 4, -mid,
                                                                          jnp.where(lane == 5, -lo, 0.0))))
        qa += [q[:, hd * DH:(hd + 1) * DH], qx.astype(BF)]
        ka += [k[:, hd * DH:(hd + 1) * DH], kx.astype(BF)]
    return [jnp.concatenate(qa, axis=1), jnp.concatenate(ka, axis=1)], []


_KA_ONES = DH + 3
KT_ONES = 16
FWD_KEY_PARTS = 2
BWD_QUERY_PARTS = 2


def _attn_fwd(qa, ka, qkv, T, blk=512, side=None):
    blk = min(blk, T)
    nb = T // blk
    half = blk // 2

    def body(q_ref, k_ref, vn_ref, o_ref, lse_ref, v_ref):
        i = pl.program_id(1)

        @pl.when(i == 0)
        def _():
            for jj in range(nb):
                v_ref[jj] = jnp.concatenate([vn_ref[jj * blk:(jj + 1) * blk, :].astype(F32).T.astype(BF),
                                             jnp.ones((KT_ONES, blk), BF)], axis=0)

        q = q_ref[...]

        def scores(j):
            r0 = pl.multiple_of(j * blk, blk)
            return _dot(k_ref[pl.ds(r0, blk), :], q, "nt")

        def update(s, vj, carry):
            m, acc = carry
            m_new = jnp.maximum(m, jnp.max(s, axis=0, keepdims=True))
            alpha = jnp.exp2(m - m_new)
            acc = alpha * acc + _dot(vj, jnp.exp2(s - m_new).astype(BF))
            return m_new, acc

        def step(j, st):
            r0 = pl.multiple_of(j * blk, blk)
            part = blk // FWD_KEY_PARTS
            ss = [_dot(k_ref[pl.ds(r0 + n * part, part), :], q, "nt") for n in range(FWD_KEY_PARTS)]
            vj = v_ref[j]
            for n in range(FWD_KEY_PARTS):
                st = update(ss[n], vj[:, n * part:(n + 1) * part], st)
            return st

        init = (jnp.full((1, blk), -jnp.inf, F32), jnp.zeros((DH + KT_ONES, blk), F32))
        carry = lax.fori_loop(0, i, step, init)
        last = scores(i)
        vi = v_ref[i]
        rk = lax.broadcasted_iota(jnp.int32, (half, blk), 0)
        cq = lax.broadcasted_iota(jnp.int32, (half, blk), 1)
        m, acc = update(jnp.where(cq >= rk, last[:half], -jnp.inf), vi[:, :half], carry)
        s2 = jnp.where(cq[:, :half] >= rk[:, :half], last[half:, half:], -jnp.inf)
        m2, acc2 = update(s2, vi[:, half:], (m[:, half:], acc[:, half:]))
        m = jnp.concatenate([m[:, :half], m2], axis=1)
        acc = jnp.concatenate([acc[:, :half], acc2], axis=1)
        l = acc[DH:DH + 1]
        o_ref[...] = (acc[:DH] / l).T.astype(o_ref.dtype)
        lse_ref[...] = m + jnp.log(l) * _LOG2E

    vm = _nbytes((T, DA), BF) + 2 * _nbytes((T, DH), BF) + 6 * _nbytes((blk, blk), F32)
    return _pallas(
        body, (qa, ka, qkv), name="attn_fwd", grid=(NH, nb),
        in_specs=[pl.BlockSpec((blk, DA), lambda h, i: (i, h)),
                  pl.BlockSpec((T, DA), lambda h, i: (0, h)),
                  pl.BlockSpec((T, DH), lambda h, i: (0, 2 * NH + h))],
        out_specs=[pl.BlockSpec((blk, DH), lambda h, i: (i, h)),
                   pl.BlockSpec((None, None, 1, blk), lambda h, i: (h, i, 0, 0))],
        out_shape=[jax.ShapeDtypeStruct((T, D), BF), jax.ShapeDtypeStruct((NH, nb, 1, blk), F32)],
        scratch_shapes=[pltpu.VMEM((nb, DH + KT_ONES, blk), BF)],
        semantics=("parallel", "arbitrary"), vmem=_vmem_limit(vm), side=side)


def _attn_bwd(qa, ka, qkv, do, lrow, drow, T, blk=512, side=None):
    blk = min(blk, T)
    nb = T // blk

    def body(ka_ref, v_ref, qa_ref, do_ref, l_ref, d_ref, dq_ref, dk_ref, dv_ref, dfs_ref, dft_ref, dq_s):
        j = pl.program_id(1)

        @pl.when(j == 0)
        def _():
            dq_s[...] = jnp.zeros_like(dq_s)

        row = lax.broadcasted_iota(jnp.int32, (DH + KT_ONES, blk), 0)
        dq_scale = jnp.where(row < DH, _SCALE, 1.0)

        kaj = ka_ref[...]
        ktj = jnp.concatenate([kaj[:, :DH].astype(F32).T.astype(BF), jnp.ones((KT_ONES, blk), BF)], axis=0)
        vj = v_ref[...]

        def step(i, carry):
            dka, dv = carry
            r0 = pl.multiple_of(i * blk, blk)
            part = blk // BWD_QUERY_PARTS
            lr, dr = l_ref[i], d_ref[i]
            loaded = []
            for n in range(BWD_QUERY_PARTS):
                qn = qa_ref[pl.ds(r0 + n * part, part), :]
                don = do_ref[pl.ds(r0 + n * part, part), :]
                loaded.append((qn, don, _dot(kaj, qn, "nt"), _dot(vj, don, "nt")))
            dq_parts = []
            for n, (qn, don, raw, dpt) in enumerate(loaded):
                pt = jnp.exp2(raw - lr[:, n * part:(n + 1) * part])
                dsb = (pt * (dpt - dr[:, n * part:(n + 1) * part])).astype(BF)
                dv = dv + _dot(pt.astype(BF), don)
                dka = dka + _dot(dsb, qn)
                dq_parts.append(_dot(ktj, dsb))
            dq_s[i] += jnp.concatenate(dq_parts, axis=1) * dq_scale
            return dka, dv

        def diagonal():
            half = blk // 2
            r0 = pl.multiple_of(j * blk, blk)
            qi = qa_ref[pl.ds(r0, blk), :]
            doi = do_ref[pl.ds(r0, blk), :]
            lr, dr = l_ref[j], d_ref[j]
            rk = lax.broadcasted_iota(jnp.int32, (half, blk), 0)
            cq = lax.broadcasted_iota(jnp.int32, (half, blk), 1)

            def quarter(ka, v, q, do, l2, d2, keep):
                pt = jnp.exp2(jnp.where(keep, _dot(ka, q, "nt") - l2, -jnp.inf))
                dsb = (pt * (_dot(v, do, "nt") - d2)).astype(BF)
                return _dot(dsb, q), _dot(pt.astype(BF), do), dsb

            dka1, dv1, ds1 = quarter(kaj[:half], vj[:half], qi, doi, lr, dr, cq >= rk)
            dka2, dv2, ds2 = quarter(kaj[half:], vj[half:], qi[half:], doi[half:], lr[:, half:], dr[:, half:],
                                     cq[:, :half] >= rk[:, :half])
            dq2 = jnp.concatenate([jnp.zeros((DH + KT_ONES, half), F32), _dot(ktj[:, half:], ds2)], axis=1)
            dq_s[j] += (_dot(ktj[:, :half], ds1) + dq2) * dq_scale
            return jnp.concatenate([dka1, dka2], axis=0), jnp.concatenate([dv1, dv2], axis=0)

        dka, dv = lax.fori_loop(j + 1, nb, step, diagonal())
        dk_ref[...] = (dka[:, :DH] * (1.0 / _LOG2E)).astype(dk_ref.dtype)
        dv_ref[...] = dv.astype(dv_ref.dtype)
        dfs_ref[...] = dka[:, DH:].T[_KA_ONES - DH:_KA_ONES - DH + 1, :]

        @pl.when(j == nb - 1)
        def _():
            for ii in range(nb):
                t = dq_s[ii]
                dq_ref[ii * blk:(ii + 1) * blk, :] = t[:DH].T.astype(dq_ref.dtype)
                dft_ref[ii] = t[DH:DH + 1]

    rowv = pl.BlockSpec((None, nb, 1, blk), lambda h, j: (h, 0, 0, 0))
    vm = (_nbytes((T, DA), BF) + 2 * _nbytes((T, DH), BF) + _nbytes((T, DH + KT_ONES), F32)
          + 8 * _nbytes((blk, blk), F32))
    return _pallas(
        body, (ka, qkv, qa, do, lrow, drow), name="attn_bwd", grid=(NH, nb),
        in_specs=[pl.BlockSpec((blk, DA), lambda h, j: (j, h)),
                  pl.BlockSpec((blk, DH), lambda h, j: (j, 2 * NH + h)),
                  pl.BlockSpec((T, DA), lambda h, j: (0, h)),
                  pl.BlockSpec((T, DH), lambda h, j: (0, h)),
                  rowv, rowv],
        out_specs=[pl.BlockSpec((T, DH), lambda h, j: (0, h)),
                   pl.BlockSpec((blk, DH), lambda h, j: (j, h)),
                   pl.BlockSpec((blk, DH), lambda h, j: (j, h)),
                   pl.BlockSpec((None, None, 1, blk), lambda h, j: (h, j, 0, 0)), rowv],
        out_shape=[jax.ShapeDtypeStruct((T, D), BF), jax.ShapeDtypeStruct((T, D), BF),
                   jax.ShapeDtypeStruct((T, D), BF), jax.ShapeDtypeStruct((NH, nb, 1, blk), F32),
                   jax.ShapeDtypeStruct((NH, nb, 1, blk), F32)],
        scratch_shapes=[pltpu.VMEM((nb, DH + KT_ONES, blk), F32)],
        semantics=("parallel", "arbitrary"), vmem=_vmem_limit(vm), side=side)


def _norm_fn(ins, cs):
    x, = ins
    g, = cs
    r = lax.rsqrt(jnp.mean(x * x, axis=-1, keepdims=True) + EPS)
    return [x * r * g], []


def _norm_bwd_fn(ins, cs):
    x, dy, dres = ins
    g, = cs
    r = lax.rsqrt(jnp.mean(x * x, axis=-1, keepdims=True) + EPS)
    xh = x * r
    dxh = dy * g
    dx = dres + r * (dxh - xh * jnp.mean(dxh * xh, axis=-1, keepdims=True))
    return [dx, dx], [_colsum(dy * xh)]


def _final_fn(ins, cs):
    x2, tgt = ins
    g, = cs
    r = lax.rsqrt(jnp.mean(x2 * x2, axis=-1, keepdims=True) + EPS)
    xh = x2 * r
    e = xh * g - tgt
    dy = e * (1.0 / D)
    dxh = dy * g
    dx2 = r * (dxh - xh * jnp.mean(dxh * xh, axis=-1, keepdims=True))
    return [dx2, dx2], [_colsum(0.5 * e * e * (1.0 / D)), _colsum(dy * xh)]


def _z_fn(ins, cs):
    g, h = [v.astype(F32) for v in ins]
    return [_gelu(g) * h], []


def _mix_fn(ins, cs):
    gates, ya, yb = [v.astype(F32) for v in ins]
    return [_sig(gates[:, :D]) * ya + _sig(gates[:, D:]) * yb], []


def _mix_bwd_fn(ins, cs):
    dmix, gates, ya, yb = [v.astype(F32) for v in ins]
    ga = _sig(gates[:, :D])
    gb = _sig(gates[:, D:])
    dgates = jnp.concatenate([dmix * ya * ga * (1.0 - ga), dmix * yb * gb * (1.0 - gb)], axis=1)
    return [dmix * ga, dmix * gb, dgates], []


def _z_bwd_fn(ins, cs):
    dz, g, h = [v.astype(F32) for v in ins]
    return [dz * _gelu(g), dz * h * _gelu_grad(g)], []


def _delta_fn(ins, cs):
    do, o = ins
    p = do.astype(F32) * o.astype(F32)
    lane = lax.broadcasted_iota(jnp.int32, (p.shape[0], DH), 1)
    out = jnp.zeros((p.shape[0], DH), F32)
    for hd in range(NH):
        s = jnp.sum(p[:, hd * DH:(hd + 1) * DH], axis=1, keepdims=True)
        out = jnp.where(lane == hd, s, out)
    return [out], []


def _du_all(pieces, win, T, tm=256, side=None):
    tm = min(tm, T)
    n = len(pieces)

    def body(*refs):
        w_ref, o_ref = refs[n], refs[n + 1]
        acc = None
        for (a, off), a_ref in zip(pieces, refs[:n]):
            d = _dot(a_ref[...].astype(BF), w_ref[:, off:off + a.shape[1]], "nt")
            acc = d if acc is None else acc + d
        o_ref[...] = acc

    vm = (sum(_nbytes((tm, a.shape[1]), a.dtype) for a, _ in pieces) + _nbytes(win.shape, win.dtype)
          + 2 * _nbytes((tm, D), F32))
    return _pallas(
        body, tuple(a for a, _ in pieces) + (win,), name="du_all", grid=(T // tm,),
        in_specs=[pl.BlockSpec((tm, a.shape[1]), lambda i: (i, 0)) for a, _ in pieces]
        + [pl.BlockSpec(win.shape, lambda i: (0, 0))],
        out_specs=[pl.BlockSpec((tm, D), lambda i: (i, 0))],
        out_shape=[jax.ShapeDtypeStruct((T, D), F32)],
        semantics=("arbitrary",), vmem=int(min(VMEM_CAP, 2 * vm + (4 << 20))), side=side)


def _local_step(x, tgt, w, T, blk=1024, dist=None):
    blk = min(blk, T)
    nb = T // blk
    win = w["win"]

    u, = _ew(_norm_fn, T, 1024, [(x, D, 0)], [w["g_mix"]], [(D, BF)], [], name="norm_mix")
    xg = _mm(u, win, "nn", T, 2 * D, D, name="proj_lru", out_dtype=BF)
    qkv = _mm(u, win, "nn", T, 3 * D, D, name="proj_qkv", out_dtype=BF, b_off=(0, 2),
              epi=lambda acc: acc * jnp.where(pl.program_id(1) == 0, _C2, 1.0))
    gates = _mm(u, win, "nn", T, 2 * D, D, name="proj_gates", b_off=(0, 5), out_dtype=BF)
    fl = _mm(u, win, "nn", T, DH, D, name="proj_f", tn=DH, b_off=(0, 7 * D // DH))
    fcum = _fgate_fwd(fl, w["fb"], T)
    qa, ka = _ew(_aug_fn, T, 1024,[(qkv, D, 0), (qkv, D, 1), (fcum, DH, 0)], [],
                 [(NH * DA, BF), (NH * DA, BF)], [], name="attn_augment")

    h = _lru_fwd(xg, w["cw"], w["vec"], w["wabd"], w["wxbd"], T)
    ob, lse, *landed = _attn_fwd(qa, ka, qkv, T, min(2 * blk, T), side=dist.weights_side() if dist else None)
    lse = lse.reshape(NH, nb, 1, blk)
    if dist:
        w = dict(w, **dist.weights_landed(landed))
    z, = _ew(_z_fn, T, 1024, [(xg, D, 1), (h, D, 0)], [], [(D, BF)], [], name="lru_gelu")
    ya = _mm(z, w["wa"], "nn", T, D, D, name="branch_a", out_dtype=BF)
    yb = _mm(ob, w["wb"], "nn", T, D, D, name="branch_b", out_dtype=BF)
    mix, = _ew(_mix_fn, T, 1024,[(gates, 2 * D, 0), (ya, D, 0), (yb, D, 0)], [], [(D, BF)], [], name="mix")
    x1 = _mm(mix, w["wout"], "nn", T, D, D, name="out_proj", add=x)
    m, = _ew(_norm_fn, T, 1024, [(x1, D, 0)], [w["g_mlp"]], [(D, BF)], [], name="norm_mlp")
    hh = _mm(m, w["wup"], "nn", T, FF, D, name="mlp_up", out_dtype=BF,
             epi=lambda acc: jnp.square(jnp.maximum(acc, 0.0)))
    x2 = _mm(hh, w["wdown"], "nn", T, D, FF, name="mlp_down", add=x1, tk=FF)
    dx2, dx2b, loss_vec, dg_fin = _ew(_final_fn, T, 1024,[(x2, D, 0), (tgt, D, 0)], [w["g_fin"]],
                                      [(D, F32), (D, BF)], [(1, D), (1, D)], name="final_norm_loss")

    dhpre = _mm(dx2b, w["wdown"], "nt", T, FF, D, name="mlp_down_bwd", out_dtype=BF,
                epi=lambda acc, h2: acc * (2.0 * jnp.sqrt(h2.astype(F32))), epi_ins=[hh])
    dwdown = _mm(hh, dx2b, "tn", FF, D, T, name="dw_down", out_dtype=BF)
    dwup = _mm(m, dhpre, "tn", D, FF, T, name="dw_up", out_dtype=BF)
    dm = _mm(dhpre, w["wup"], "nt", T, D, FF, name="mlp_up_bwd", tk=FF)
    dx1, dx1b, dg_mlp = _ew(_norm_bwd_fn, T, 512, [(x1, D, 0), (dm, D, 0), (dx2, D, 0)], [w["g_mlp"]],
                            [(D, F32), (D, BF)], [(1, D)], name="norm_mlp_bwd")

    dmix = _mm(dx1b, w["wout"], "nt", T, D, D, name="out_proj_bwd", out_dtype=BF)
    dwout = _mm(mix, dx1b, "tn", D, D, T, name="dw_out", out_dtype=BF)
    dya, dyb, dgates = _ew(_mix_bwd_fn, T, 512, [(dmix, D, 0), (gates, 2 * D, 0), (ya, D, 0), (yb, D, 0)], [],
                           [(D, BF), (D, BF), (2 * D, BF)], [], name="mix_bwd")
    dob = _mm(dyb, w["wb"], "nt", T, D, D, name="branch_b_bwd", out_dtype=BF)
    dwb = _mm(ob, dyb, "tn", D, D, T, name="dw_b", out_dtype=BF)
    dz = _mm(dya, w["wa"], "nt", T, D, D, name="branch_a_bwd", out_dtype=BF)
    dwa = _mm(z, dya, "tn", D, D, T, name="dw_a", out_dtype=BF)
    dha, dglru = _ew(_z_bwd_fn, T, 1024,[(dz, D, 0), (xg, D, 1), (h, D, 0)], [], [(D, F32), (D, BF)], [],
                     name="lru_gelu_bwd")

    delta, = _ew(_delta_fn, T, 1024, [(dob, D, 0), (ob, D, 0)], [], [(DH, F32)], [], name="attn_delta")
    drow = delta[:, :NH].T.reshape(NH, nb, 1, blk)
    big = dict(w_branch_a=dwa, w_branch_b=dwb, w_out=dwout, w_up=dwup, w_down=dwdown)
    side = dist.grads_side(big) if dist else None
    dq, dk, dv, dfs, dft, *landed = _attn_bwd(qa, ka, qkv, dob, lse, drow, T, blk, side=side)
    if dist:
        big = dist.grads_landed(side, landed)
    dfcum = jnp.pad((dft - dfs).reshape(NH, T).T, ((0, 0), (0, DH - NH)))
    dfl, dfb = _fgate_bwd(dfcum, fl, w["fb"], T)

    dxl, dwabd, dwxbd, lacc = _lru_bwd(xg, h, dha, w["cw"], w["vec"], w["wabd"], w["wxbd"], T)

    dproj = ((dxl, 0), (dglru, D), (dq, 2 * D), (dk, 3 * D), (dv, 4 * D), (dgates, 5 * D), (dfl, 7 * D))
    pieces = [_mm(u, p, "tn", D, p.shape[1], T, name="dw_in_%d" % n, out_dtype=BF)
              for n, (p, _) in enumerate(dproj)]
    pieces[-1] = pieces[-1][:, :NH]
    dwin = dict(w_in=jnp.concatenate(pieces, axis=1))
    side = dist.grads_side(dwin) if dist else None
    du, *landed = _du_all(dproj, win, T, side=side)
    big.update(dist.grads_landed(side, landed) if dist else dwin)
    dx, dg_mix = _ew(_norm_bwd_fn, T, 512, [(x, D, 0), (du, D, 0), (dx1, D, 0)], [w["g_mix"]], [(D, F32)],
                     [(1, D)], name="norm_mix_bwd")

    return dict(dx=dx, big=big, dwabd=dwabd, dwxbd=dwxbd, lacc=lacc, dfb=dfb, dg_mix=dg_mix, dg_mlp=dg_mlp,
                dg_fin=dg_fin, loss_vec=loss_vec)


def _block_diag(w):
    per = BD // LRU_BW
    w4 = w.reshape(NBD, per, LRU_BW, LRU_BW)
    on_diagonal = jnp.eye(per, dtype=bool)[None, :, None, :, None]
    return jnp.where(on_diagonal, w4[:, :, :, None, :], 0.0).reshape(NBD, BD, BD)


def _block_diag_extract(wbd):
    per = BD // LRU_BW
    w5 = wbd.reshape(NBD, per, LRU_BW, per, LRU_BW)
    return jnp.stack([w5[:, b, :, b, :] for b in range(per)], axis=1).reshape(LRU_BLOCKS, LRU_BW, LRU_BW)


def _place():
    x, y, c = lax.axis_index("x"), lax.axis_index("y"), lax.axis_index("c")
    chips = [(1 - x, y), (x, 1 - y), (1 - x, 1 - y)]
    return x, y, c, chips


def _allgather_shards(shards):
    n = len(shards)

    def body(*refs):
        ins, outs = refs[:n], refs[n:2 * n]
        send_sems, recv_sems = refs[2 * n:]
        x, y, c, chips = _place()
        me = 2 * x + y
        sibling = (x, y, 1 - c)

        def remote(p, k, src, dst, to):
            return pltpu.make_async_remote_copy(src_ref=src, dst_ref=dst, send_sem=send_sems.at[p, k],
                                                recv_sem=recv_sems.at[p, k], device_id=to, device_id_type=MESH)

        sent = []
        for p in range(n):
            for k, chip in enumerate(chips):
                cp = remote(p, k, ins[p].at[c], outs[p].at[me, c], (chip[0], chip[1], c))
                cp.start()
                sent.append(cp)
        for p in range(n):
            for k, chip in enumerate(chips):
                half = outs[p].at[2 * chip[0] + chip[1], c]
                remote(p, k, half, half, sibling).wait_recv()
                fwd = remote(p, 3 + k, half, half, sibling)
                fwd.start()
                sent.append(fwd)
        for p in range(n):
            for k, chip in enumerate(chips):
                half = outs[p].at[2 * chip[0] + chip[1], 1 - c]
                remote(p, 3 + k, half, half, sibling).wait_recv()
        for cp in sent:
            cp.wait_send()

    gathered = pl.pallas_call(
        body, name="allgather_weights",
        in_specs=[_ANY] * n, out_specs=[_ANY] * n,
        out_shape=[jax.ShapeDtypeStruct((NCHIP,) + s.shape, s.dtype) for s in shards],
        scratch_shapes=[pltpu.SemaphoreType.DMA((n, 6)), pltpu.SemaphoreType.DMA((n, 6))],
    )(*shards)
    me = 2 * lax.axis_index("x") + lax.axis_index("y")
    return [lax.dynamic_update_index_in_dim(g, s, me, 0) for g, s in zip(gathered, shards)]


_LATE = ["w_branch_a", "w_branch_b", "w_out", "w_up", "w_down"]
_COLUMN_CUT = ("w_in", "w_up")
N_PEERS = 7


def _shard_major(name, g):
    s = _columns_to_shards(g) if name in _COLUMN_CUT else g.reshape(NCHIP, g.shape[0] // NCHIP, g.shape[1])
    return s.reshape(NCHIP, 2, s.shape[1] // 2, s.shape[2])


class _Exchanges:
    def __init__(self, shards):
        self.shards = shards

    def weights_side(self):
        srcs = [self.shards[n] for n in _LATE]

        def copies(sin, sout, send, recv):
            x, y, c, chips = _place()
            return [pltpu.make_async_remote_copy(
                src_ref=sin[p], dst_ref=sout[p].at[2 * x + y], send_sem=send.at[3 * p + k], recv_sem=recv.at[3 * p + k],
                device_id=(chip[0], chip[1], c), device_id_type=MESH)
                for p in range(len(sin)) for k, chip in enumerate(chips)]

        return _Side(srcs, [jax.ShapeDtypeStruct((NCHIP,) + s.shape, s.dtype) for s in srcs], 3 * len(srcs), copies)

    def weights_landed(self, landed):
        me = 2 * lax.axis_index("x") + lax.axis_index("y")
        full = {n: lax.dynamic_update_index_in_dim(g, self.shards[n], me, 0) for n, g in zip(_LATE, landed)}
        return dict(wa=full["w_branch_a"].reshape(D, D), wb=full["w_branch_b"].reshape(D, D),
                    wout=full["w_out"].reshape(D, D), wup=_shards_to_columns(full["w_up"]),
                    wdown=full["w_down"].reshape(FF, D))

    def grads_side(self, grads):
        side_names = list(grads)
        srcs = [_shard_major(n, grads[n]) for n in side_names]

        def copies(sin, sout, send, recv):
            x, y, c, chips = _place()
            peers = [(x, y, 1 - c)] + [(cx, cy, c) for cx, cy in chips] + [(cx, cy, 1 - c) for cx, cy in chips]
            return [pltpu.make_async_remote_copy(
                src_ref=sin[p].at[2 * px + py, pc], dst_ref=sout[p].at[s], send_sem=send.at[N_PEERS * p + s],
                recv_sem=recv.at[N_PEERS * p + s], device_id=(px, py, pc), device_id_type=MESH)
                for p in range(len(sin)) for s, (px, py, pc) in enumerate(peers)]

        side = _Side(srcs, [jax.ShapeDtypeStruct((N_PEERS,) + s.shape[2:], s.dtype) for s in srcs],
                     N_PEERS * len(srcs), copies)
        side.names = side_names
        return side

    def grads_landed(self, side, landed):
        return {n: (own, got) for n, own, got in zip(side.names, side.srcs, landed)}


def _add8(g, got, me, c, name):
    _, _, half, cols = g.shape
    th = _row_tile(half, 2 * cols)

    def body(me_ref, c_ref, g_ref, r_ref, o_ref):
        acc = g_ref[...].astype(F32)
        for s in range(N_PEERS):
            acc = acc + r_ref[s].astype(F32)
        o_ref[...] = acc

    return pl.pallas_call(
        body, name=name,
        grid_spec=pltpu.PrefetchScalarGridSpec(
            num_scalar_prefetch=2, grid=(half // th,),
            in_specs=[pl.BlockSpec((None, None, th, cols), lambda i, me_ref, c_ref: (me_ref[0], c_ref[0], i, 0)),
                      pl.BlockSpec((N_PEERS, th, cols), lambda i, me_ref, c_ref: (0, i, 0))],
            out_specs=pl.BlockSpec((th, cols), lambda i, me_ref, c_ref: (i, 0))),
        out_shape=jax.ShapeDtypeStruct((half, cols), F32),
    )(me, c, g, got)


def _share_halves(halves):
    n = len(halves)

    def body(*refs):
        ins, outs = refs[:n], refs[n:2 * n]
        send_sems, recv_sems = refs[2 * n:]
        x, y, c, _ = _place()
        sibling = (x, y, 1 - c)
        copies = []
        for p in range(n):
            cp = pltpu.make_async_remote_copy(src_ref=ins[p], dst_ref=outs[p], send_sem=send_sems.at[p],
                                              recv_sem=recv_sems.at[p], device_id=sibling, device_id_type=MESH)
            cp.start()
            copies.append(cp)
        for cp in copies:
            cp.wait()

    return pl.pallas_call(
        body, name="reduce_share_halves",
        in_specs=[_ANY] * n, out_specs=[_ANY] * n,
        out_shape=[jax.ShapeDtypeStruct(h.shape, h.dtype) for h in halves],
        scratch_shapes=[pltpu.SemaphoreType.DMA((n,)), pltpu.SemaphoreType.DMA((n,))],
    )(*halves)


def _row_tile(half, cols):
    th = max(SLAB, min(half, (1 << 18) // cols // SLAB * SLAB))
    while half % th:
        th -= SLAB
    return th


N_DEV = 8
SMALL_ROWS = 208


def _allreduce_small(pack):
    def body(x_ref, out_ref, gbuf, send_sems, recv_sems, local_sem):
        x, y, c, chips = _place()
        me, sibling = (x, y, c), (x, y, 1 - c)

        def rows(px, py, pc):
            return gbuf.at[4 * px + 2 * py + pc]

        def copy(k, block, to, src=None):
            return pltpu.make_async_remote_copy(
                src_ref=rows(*block) if src is None else src, dst_ref=rows(*block),
                send_sem=send_sems.at[k], recv_sem=recv_sems.at[k], device_id=to, device_id_type=MESH)

        mine = pltpu.make_async_copy(x_ref, rows(*me), local_sem)
        mine.start()
        first = [copy(0, me, sibling, src=x_ref)]
        first += [copy(1 + j, me, (chip[0], chip[1], c), src=x_ref) for j, chip in enumerate(chips)]
        for cp in first:
            cp.start()
        passed = [copy(4 + j, (chip[0], chip[1], c), sibling) for j, chip in enumerate(chips)]
        for j, chip in enumerate(chips):
            copy(1 + j, (chip[0], chip[1], c), me).wait_recv()
            passed[j].start()
        copy(0, sibling, me).wait_recv()
        for j, chip in enumerate(chips):
            copy(4 + j, (chip[0], chip[1], 1 - c), me).wait_recv()
        for cp in first + passed:
            cp.wait_send()
        mine.wait()
        acc = gbuf[0]
        for d in range(1, N_DEV):
            acc = acc + gbuf[d]
        out_ref[...] = acc

    return pl.pallas_call(
        body, name="allreduce_small",
        in_specs=[pl.BlockSpec(memory_space=pltpu.VMEM)],
        out_specs=pl.BlockSpec(memory_space=pltpu.VMEM),
        out_shape=jax.ShapeDtypeStruct((SMALL_ROWS, D), F32),
        scratch_shapes=[pltpu.VMEM((N_DEV, SMALL_ROWS, D), F32), pltpu.SemaphoreType.DMA((7,)),
                        pltpu.SemaphoreType.DMA((7,)), pltpu.SemaphoreType.DMA],
    )(pack)


def _adamw(w, g, m, v, name):
    rows, cols = w.shape

    def body(w_ref, g_ref, m_ref, v_ref, d_ref, mo_ref, vo_ref):
        gv = g_ref[...]
        mn = ADAM_B1 * m_ref[...] + (1.0 - ADAM_B1) * gv
        vn = ADAM_B2 * v_ref[...] + (1.0 - ADAM_B2) * (gv * gv)
        m_hat = mn / (1.0 - ADAM_B1 ** ADAM_STEP)
        v_hat = vn / (1.0 - ADAM_B2 ** ADAM_STEP)
        d_ref[...] = -ADAM_LR * (m_hat / (jnp.sqrt(v_hat) + ADAM_EPS) + ADAM_WD * w_ref[...])
        mo_ref[...] = mn
        vo_ref[...] = vn

    if rows % SLAB:
        spec, steps = pl.BlockSpec((rows, DH), lambda i: (0, i)), cols // DH
    else:
        th = _row_tile(rows, cols)
        spec, steps = pl.BlockSpec((th, cols), lambda i: (i, 0)), rows // th
    return pl.pallas_call(
        body, name=name, grid=(steps,),
        in_specs=[spec] * 4, out_specs=[spec] * 3,
        out_shape=[jax.ShapeDtypeStruct((rows, cols), F32)] * 3,
        compiler_params=pltpu.CompilerParams(dimension_semantics=("parallel",)),
    )(w, g, m, v)


_SMALL = ["norm_mix_g", "norm_mlp_g", "norm_final_g", "conv_b", "lru_ba", "lru_bx", "lru_lambda"]
_ROW_FB, _ROW_CW, _ROW_WA, _ROW_WX, _ROW_LOSS = 56, 64, 72, 136, 200


def _pack_small(vals, col0):
    def slab(a):
        return jnp.pad(a, ((0, -a.shape[0] % SLAB), (0, D - a.shape[1])))

    rows = [slab(vals[n].reshape(1, D)) for n in _SMALL]
    rows.append(slab(vals["forget_b"].reshape(1, NH)))
    if vals["conv_w"].shape[1] == D:
        rows.append(slab(vals["conv_w"]))
    else:
        rows.append(slab(lax.dynamic_update_slice(jnp.zeros((CONV, D), F32), vals["conv_w"], (0, col0))))
    rows.append(vals["lru_wa"].reshape(LRU_BLOCKS * LRU_BW * LRU_BW // D, D))
    rows.append(vals["lru_wx"].reshape(LRU_BLOCKS * LRU_BW * LRU_BW // D, D))
    rows.append(slab(vals["loss"]) if "loss" in vals else jnp.zeros((SLAB, D), F32))
    return jnp.concatenate(rows, axis=0)


def _unpack_small(pack, col0):
    out = {n: pack[SLAB * i] for i, n in enumerate(_SMALL)}
    out["forget_b"] = pack[_ROW_FB, :NH]
    out["conv_w"] = lax.dynamic_slice(pack[_ROW_CW:_ROW_CW + CONV], (0, col0), (CONV, D // NCHIP))
    out["lru_wa"] = pack[_ROW_WA:_ROW_WX].reshape(LRU_BLOCKS, LRU_BW, LRU_BW)
    out["lru_wx"] = pack[_ROW_WX:_ROW_LOSS].reshape(LRU_BLOCKS, LRU_BW, LRU_BW)
    return out


_WEIGHTS = ["norm_mix_g", "w_in", "conv_w", "conv_b", "lru_wa", "lru_ba", "lru_wx", "lru_bx", "lru_lambda",
            "forget_b", "w_branch_a", "w_branch_b", "w_out", "norm_mlp_g", "w_up", "w_down", "norm_final_g"]
_BIG = ["w_in", "w_branch_a", "w_branch_b", "w_out", "w_up", "w_down"]


def _halves(a):
    return a.reshape(2, a.shape[0] // 2, a.shape[1])


def _columns_to_shards(a):
    rows, cols = a.shape[0], a.shape[1] // NCHIP
    return jnp.transpose(a.reshape(rows, NCHIP, cols), (1, 0, 2))


def _shards_to_columns(a):
    n, rows, cols = a.shape
    return jnp.transpose(a, (1, 0, 2)).reshape(rows, n * cols)


def kernel(x, norm_mix_g, w_in, conv_w, conv_b, lru_wa, lru_ba, lru_wx, lru_bx, lru_lambda, forget_b, w_branch_a, w_branch_b, w_out, norm_mlp_g, w_up, w_down, norm_final_g, loss_target, m_norm_mix_g, m_w_in, m_conv_w, m_conv_b, m_lru_wa, m_lru_ba, m_lru_wx, m_lru_bx, m_lru_lambda, m_forget_b, m_w_branch_a, m_w_branch_b, m_w_out, m_norm_mlp_g, m_w_up, m_w_down, m_norm_final_g, v_norm_mix_g, v_w_in, v_conv_w, v_conv_b, v_lru_wa, v_lru_ba, v_lru_wx, v_lru_bx, v_lru_lambda, v_forget_b, v_w_branch_a, v_w_branch_b, v_w_out, v_norm_mlp_g, v_w_up, v_w_down, v_norm_final_g):
    args = dict(locals())
    wts = {n: args[n] for n in _WEIGHTS}
    mom = {n: args["m_" + n] for n in _WEIGHTS}
    var = {n: args["v_" + n] for n in _WEIGHTS}
    T = x.shape[1]
    xi, yi, ci = lax.axis_index("x"), lax.axis_index("y"), lax.axis_index("c")
    me = 2 * xi + yi
    c1 = jnp.reshape(ci, (1,)).astype(jnp.int32)
    me1 = jnp.reshape(me, (1,)).astype(jnp.int32)
    col0 = me * (D // NCHIP)

    cw_pad = jnp.pad(conv_w, ((0, 4 * SLAB - CONV), (0, 0)))
    g_in, g_cw = _allgather_shards([_halves(w_in.astype(BF)), _halves(cw_pad)])
    cin = DIN // NCHIP
    win = _shards_to_columns(g_in.reshape(NCHIP, D, cin))
    w = dict(
        win=jnp.pad(win, ((0, 0), (0, DINP - DIN))),
        cw=_shards_to_columns(g_cw.reshape(NCHIP, 4 * SLAB, D // NCHIP)[:, :CONV]),
        vec=jnp.concatenate([conv_b[None], lru_ba[None], lru_bx[None], lru_lambda[None],
                             jnp.zeros((SLAB - 4, D), F32)], axis=0),
        fb=jnp.pad(forget_b[None], ((0, 0), (0, DH - NH))),
        wabd=_block_diag(lru_wa).astype(BF), wxbd=_block_diag(lru_wx).astype(BF),
        g_mix=norm_mix_g[None], g_mlp=norm_mlp_g[None], g_fin=norm_final_g[None])

    r = _local_step(x[0], loss_target[0], w, T, dist=_Exchanges({n: wts[n].astype(BF) for n in _LATE}))

    halves = [_add8(*r["big"][n], me1, c1, "add8_" + n) for n in _BIG]
    theirs = _share_halves(halves)
    low = ci == 0
    gsum = {n: jnp.concatenate([jnp.where(low, h, t), jnp.where(low, t, h)], axis=0)
            for n, h, t in zip(_BIG, halves, theirs)}
    lacc = r["lacc"]
    small = dict(norm_mix_g=r["dg_mix"], norm_mlp_g=r["dg_mlp"], norm_final_g=r["dg_fin"], conv_b=lacc[3],
                 lru_ba=lacc[0], lru_bx=lacc[1], lru_lambda=lacc[2], forget_b=r["dfb"][0, :NH],
                 conv_w=lacc[4:4 + CONV], lru_wa=_block_diag_extract(r["dwabd"]),
                 lru_wx=_block_diag_extract(r["dwxbd"]), loss=r["loss_vec"])
    gpack = _allreduce_small(_pack_small(small, col0))
    loss = jnp.sum(gpack[_ROW_LOSS])

    grads, delta, new_m, new_v = {}, {}, {}, {}
    for n in _BIG:
        if n == "w_in":
            gt = gsum[n].T
            grads[n] = gt.T
            delta[n], new_m[n], new_v[n] = [a.T for a in _adamw(wts[n].T, gt, mom[n].T, var[n].T, "adamw_" + n)]
        else:
            grads[n] = gsum[n]
            delta[n], new_m[n], new_v[n] = _adamw(wts[n], gsum[n], mom[n], var[n], "adamw_" + n)
    dp, mp, vp = _adamw(_pack_small(wts, col0), gpack, _pack_small(mom, col0), _pack_small(var, col0), "adamw_small")
    for dst, pack in ((grads, gpack), (delta, dp), (new_m, mp), (new_v, vp)):
        dst.update(_unpack_small(pack, col0))
    return (loss, r["dx"][None], *[grads[n] for n in _WEIGHTS], *[delta[n] for n in _WEIGHTS],
            *[new_m[n] for n in _WEIGHTS], *[new_v[n] for n in _WEIGHTS])
```

```python
import functools
import math

import jax
import jax.numpy as jnp
import numpy as np
from jax import lax
from jax.experimental import pallas as pl
from jax.experimental.pallas import tpu as pltpu

F32 = jnp.float32
BF = jnp.bfloat16

D = 1024
NH = 8
DH = 128
FF = 4096
CONV = 4
LRU_BLOCKS = 16
LRU_BW = 64
BD = 256
NBD = D // BD
LRU_C = 8.0
EPS = 1e-6
DIN = 7176
DINP = 7296
NCHIP = 4
SLAB = 8
VMEM_CAP = 60 * 1024 * 1024

ADAM_LR = 0.001
ADAM_B1 = 0.9
ADAM_B2 = 0.999
ADAM_EPS = 1e-08
ADAM_WD = 0.01
ADAM_STEP = 10

MESH = pl.DeviceIdType.MESH


def _vmem_limit(nbytes):
    return int(min(VMEM_CAP, max(32 * 1024 * 1024, 3 * nbytes)))


def _nbytes(shape, dtype):
    return int(np.prod(shape)) * jnp.dtype(dtype).itemsize


def _sig(x):
    return 0.5 * jnp.tanh(0.5 * x) + 0.5


def _log1p(u):
    w = 1.0 + u
    return jnp.where(w == 1.0, u, jnp.log(w) * (u / (w - 1.0)))


def _one_minus_sq(a, la):
    return jnp.tanh(-la) * (1.0 + a * a)


def _softplus(z):
    return jnp.maximum(z, 0.0) + _log1p(jnp.exp(-jnp.abs(z)))


_GELU_C = math.sqrt(2.0 / math.pi)


def _gelu(x):
    return 0.5 * x * (1.0 + jnp.tanh(_GELU_C * (x + 0.044715 * x * x * x)))


def _gelu_grad(x):
    t = jnp.tanh(_GELU_C * (x + 0.044715 * x * x * x))
    return 0.5 * (1.0 + t) + 0.5 * x * (1.0 - t * t) * _GELU_C * (1.0 + 3.0 * 0.044715 * x * x)


def _slab_scan_fwd(a, b):
    row = lax.broadcasted_iota(jnp.int32, a.shape, 0)
    for k in (1, 2, 4):
        a_s = pltpu.roll(a, k, 0)
        b_s = pltpu.roll(b, k, 0)
        m = row >= k
        b = jnp.where(m, a * b_s + b, b)
        a = jnp.where(m, a * a_s, a)
    return a, b


def _slab_scan_bwd(a, b):
    row = lax.broadcasted_iota(jnp.int32, a.shape, 0)
    for k in (1, 2, 4):
        a_s = pltpu.roll(a, SLAB - k, 0)
        b_s = pltpu.roll(b, SLAB - k, 0)
        m = row < SLAB - k
        b = jnp.where(m, a * b_s + b, b)
        a = jnp.where(m, a * a_s, a)
    return a, b


_DIMS = {"nn": (((1,), (0,)), ((), ())), "nt": (((1,), (1,)), ((), ())), "tn": (((0,), (0,)), ((), ()))}


def _dot(a, b, mode="nn"):
    return lax.dot_general(a, b, _DIMS[mode], preferred_element_type=F32)


def _mm(a, b, mode, M, N, K, *, name, out_dtype=F32, tm=None, tn=1024, tk=1024,
        a_off=(0, 0), b_off=(0, 0), add=None, epi=None, epi_ins=()):
    if tm is None:
        tm = 2048 if (K <= tk and out_dtype == BF and add is None) else 1024
    tm, tn, tk = min(tm, M), min(tn, N), min(tk, K)
    nk = K // tk
    grid = (M // tm, N // tn, nk)
    if mode == "nn":
        a_spec = pl.BlockSpec((tm, tk), lambda i, j, k: (i + a_off[0], k + a_off[1]))
        b_spec = pl.BlockSpec((tk, tn), lambda i, j, k: (k + b_off[0], j + b_off[1]))
    elif mode == "nt":
        a_spec = pl.BlockSpec((tm, tk), lambda i, j, k: (i + a_off[0], k + a_off[1]))
        b_spec = pl.BlockSpec((tn, tk), lambda i, j, k: (j + b_off[0], k + b_off[1]))
    else:
        a_spec = pl.BlockSpec((tk, tm), lambda i, j, k: (k + a_off[0], i + a_off[1]))
        b_spec = pl.BlockSpec((tk, tn), lambda i, j, k: (k + b_off[0], j + b_off[1]))
    o_spec = pl.BlockSpec((tm, tn), lambda i, j, k: (i, j))
    extra = ([add] if add is not None else []) + list(epi_ins)
    n_extra = len(extra)
    has_add = add is not None

    def body(*refs):
        a_ref, b_ref = refs[0], refs[1]
        ex = refs[2:2 + n_extra]
        o_ref = refs[2 + n_extra]

        def finish(acc):
            if has_add:
                acc = acc + ex[0][...].astype(F32)
            if epi is not None:
                acc = epi(acc, *[e[...] for e in ex[(1 if has_add else 0):]])
            o_ref[...] = acc.astype(o_ref.dtype)

        p = _dot(a_ref[...].astype(BF), b_ref[...].astype(BF), mode)
        if nk == 1:
            finish(p)
        else:
            acc_ref = refs[3 + n_extra]
            k = pl.program_id(2)

            @pl.when(k == 0)
            def _():
                acc_ref[...] = p

            @pl.when(k > 0)
            def _():
                acc_ref[...] += p

            @pl.when(k == nk - 1)
            def _():
                finish(acc_ref[...])

    blk = (_nbytes((tm, tk), a.dtype) + _nbytes((tk, tn), b.dtype) + _nbytes((tm, tn), out_dtype)
           + sum(_nbytes((tm, tn), e.dtype) for e in extra) + 2 * _nbytes((tm, tn), F32))
    return pl.pallas_call(
        body, name=name, grid=grid,
        in_specs=[a_spec, b_spec] + [o_spec] * n_extra,
        out_specs=o_spec,
        out_shape=jax.ShapeDtypeStruct((M, N), out_dtype),
        scratch_shapes=[pltpu.VMEM((tm, tn), F32)] if nk > 1 else [],
        compiler_params=pltpu.CompilerParams(
            dimension_semantics=("parallel", "parallel", "arbitrary"), vmem_limit_bytes=_vmem_limit(blk)),
    )(a, b, *extra)


def _ew(fn, T, tm, ins, consts, outs, accs, *, name, reverse=False):
    tm = min(tm, T)
    nt = T // tm
    n_in, n_c, n_o, n_a = len(ins), len(consts), len(outs), len(accs)

    def row(i):
        return nt - 1 - i if reverse else i

    in_specs = [pl.BlockSpec((tm, w), functools.partial(lambda i, cb: (row(i), cb), cb=cb)) for (_, w, cb) in ins]
    in_specs += [pl.BlockSpec(c.shape, functools.partial(lambda i, nd: (0,) * nd, nd=c.ndim)) for c in consts]
    out_specs = [pl.BlockSpec((tm, w), lambda i: (row(i), 0)) for (w, _) in outs]
    out_specs += [pl.BlockSpec((r, w), lambda i: (0, 0)) for (r, w) in accs]
    out_shape = [jax.ShapeDtypeStruct((T, w), dt) for (w, dt) in outs]
    out_shape += [jax.ShapeDtypeStruct((r, w), F32) for (r, w) in accs]

    def body(*refs):
        in_refs = refs[:n_in]
        c_refs = refs[n_in:n_in + n_c]
        o_refs = refs[n_in + n_c:n_in + n_c + n_o]
        a_refs = refs[n_in + n_c + n_o:]
        ov, av = fn([r[...] for r in in_refs], [r[...] for r in c_refs])
        for r, v in zip(o_refs, ov):
            r[...] = v.astype(r.dtype)
        if n_a:
            i = pl.program_id(0)

            @pl.when(i == 0)
            def _():
                for r, v in zip(a_refs, av):
                    r[...] = v

            @pl.when(i > 0)
            def _():
                for r, v in zip(a_refs, av):
                    r[...] += v

    blk = (sum(_nbytes((tm, w), a.dtype) for (a, w, _) in ins) + sum(_nbytes(c.shape, c.dtype) for c in consts)
           + sum(_nbytes((tm, w), dt) for (w, dt) in outs) + sum(_nbytes(s, F32) for s in accs))
    res = pl.pallas_call(
        body, name=name, grid=(nt,), in_specs=in_specs, out_specs=out_specs, out_shape=out_shape,
        compiler_params=pltpu.CompilerParams(
            dimension_semantics=("arbitrary",), vmem_limit_bytes=_vmem_limit(blk)),
    )(*[a for (a, _, _) in ins], *consts)
    return res


def _colsum(v):
    return jnp.sum(v, axis=0, keepdims=True)


FGATE_GROUP = 4


def _fgate_fwd(fl, fb, T, tm=512):
    tm = min(tm, T)

    def body(fl_ref, fb_ref, f_ref, carry_ref):
        i = pl.program_id(0)

        @pl.when(i == 0)
        def _():
            carry_ref[...] = jnp.zeros_like(carry_ref)

        rows = FGATE_GROUP * SLAB
        sub = lax.broadcasted_iota(jnp.int32, (rows, DH), 0) % SLAB

        def group(s, carry):
            r0 = pl.multiple_of(s * rows, rows)
            z = fl_ref[pl.ds(r0, rows), :] + fb_ref[...]
            c = jnp.minimum(z, 0.0) - _log1p(jnp.exp(-jnp.abs(z)))
            for k in (1, 2, 4):
                c = c + jnp.where(sub >= k, pltpu.roll(c, k, 0), 0.0)
            for u in range(FGATE_GROUP):
                cu = c[u * SLAB:(u + 1) * SLAB] + carry
                f_ref[pl.ds(r0 + u * SLAB, SLAB), :] = cu
                carry = cu[SLAB - 1:SLAB, :]
            return carry

        carry_ref[0:1, :] = lax.fori_loop(0, tm // rows, group, carry_ref[0:1, :])

    return pl.pallas_call(
        body, name="fgate_fwd", grid=(T // tm,),
        in_specs=[pl.BlockSpec((tm, DH), lambda i: (i, 0)), pl.BlockSpec((1, DH), lambda i: (0, 0))],
        out_specs=pl.BlockSpec((tm, DH), lambda i: (i, 0)),
        out_shape=jax.ShapeDtypeStruct((T, DH), F32),
        scratch_shapes=[pltpu.VMEM((SLAB, DH), F32)],
        compiler_params=pltpu.CompilerParams(dimension_semantics=("arbitrary",)),
    )(fl, fb)


def _fgate_bwd(dF, fl, fb, T, tm=512):
    tm = min(tm, T)
    nt = T // tm

    def body(df_ref, fl_ref, fb_ref, o_ref, acc_ref, carry_ref):
        i = pl.program_id(0)

        @pl.when(i == 0)
        def _():
            carry_ref[...] = jnp.zeros_like(carry_ref)
            acc_ref[...] = jnp.zeros_like(acc_ref)

        rows = FGATE_GROUP * SLAB
        sub = lax.broadcasted_iota(jnp.int32, (rows, DH), 0) % SLAB

        def group(n, carry):
            g_next, acc = carry
            r0 = pl.multiple_of((tm // rows - 1 - n) * rows, rows)
            c = df_ref[pl.ds(r0, rows), :]
            for k in (1, 2, 4):
                c = c + jnp.where(sub < SLAB - k, pltpu.roll(c, rows - k, 0), 0.0)
            sg = _sig(-(fl_ref[pl.ds(r0, rows), :] + fb_ref[...]))
            for u in reversed(range(FGATE_GROUP)):
                cu = c[u * SLAB:(u + 1) * SLAB] + g_next
                dfl = cu * sg[u * SLAB:(u + 1) * SLAB]
                o_ref[pl.ds(r0 + u * SLAB, SLAB), :] = dfl.astype(o_ref.dtype)
                acc = acc + _colsum(dfl)
                g_next = cu[0:1, :]
            return g_next, acc

        g, acc = lax.fori_loop(0, tm // rows, group, (carry_ref[0:1, :], jnp.zeros((1, DH), F32)))
        carry_ref[0:1, :] = g
        acc_ref[...] += acc

    return pl.pallas_call(
        body, name="fgate_bwd", grid=(nt,),
        in_specs=[pl.BlockSpec((tm, DH), lambda i: (nt - 1 - i, 0)), pl.BlockSpec((tm, DH), lambda i: (nt - 1 - i, 0)),
                  pl.BlockSpec((1, DH), lambda i: (0, 0))],
        out_specs=[pl.BlockSpec((tm, DH), lambda i: (nt - 1 - i, 0)), pl.BlockSpec((1, DH), lambda i: (0, 0))],
        out_shape=[jax.ShapeDtypeStruct((T, DH), BF), jax.ShapeDtypeStruct((1, DH), F32)],
        scratch_shapes=[pltpu.VMEM((SLAB, DH), F32)],
        compiler_params=pltpu.CompilerParams(dimension_semantics=("arbitrary",)),
    )(dF, fl, fb)


def _shift_down(x, d, prev8):
    n = x.shape[0]
    row8 = lax.broadcasted_iota(jnp.int32, (SLAB, x.shape[1]), 0)
    y = pltpu.roll(x, d, 0)
    top = jnp.where(row8 < d, pltpu.roll(prev8, d, 0), y[0:SLAB])
    return jnp.concatenate([top, y[SLAB:]], axis=0)


def _shift_up(x, d, next8):
    n = x.shape[0]
    row8 = lax.broadcasted_iota(jnp.int32, (SLAB, x.shape[1]), 0)
    y = pltpu.roll(x, n - d, 0)
    bottom = jnp.where(row8 >= SLAB - d, pltpu.roll(next8, SLAB - d, 0), y[n - SLAB:])
    return jnp.concatenate([y[:n - SLAB], bottom], axis=0)


def _conv(x, prev8, cw, cb):
    xs = [x] + [_shift_down(x, d, prev8) for d in (1, 2, 3)]
    xa = cb + cw[3:4, :] * xs[0] + cw[2:3, :] * xs[1] + cw[1:2, :] * xs[2] + cw[0:1, :] * xs[3]
    return xa, xs


def _lru_gates(xa_g, wa_g, wx_g, ba_g, bx_g, sp_g):
    xb = xa_g.astype(BF)
    r = _sig(_dot(xb, wa_g) + ba_g)
    ig = _sig(_dot(xb, wx_g) + bx_g)
    la = -LRU_C * r * sp_g
    a = jnp.exp(la)
    mult = jnp.sqrt(_one_minus_sq(a, la))
    return r, ig, a, mult


def _lru_fwd(xg, cw, vec, wabd, wxbd, T, tm=256):
    tm = min(tm, T)
    nsl = tm // SLAB

    def body(x_ref, xp_ref, cw_ref, vec_ref, wa_ref, wx_ref, h_ref, a_s, b_s, carry_ref):
        i = pl.program_id(0)

        @pl.when(i == 0)
        def _():
            carry_ref[...] = jnp.zeros_like(carry_ref)

        x = x_ref[...].astype(F32)
        prev8 = jnp.where(i > 0, xp_ref[SLAB:, :].astype(F32), 0.0)
        vec_v = vec_ref[...]
        xa, _ = _conv(x, prev8, cw_ref[...], vec_v[0:1, :])
        sp = _softplus(-vec_v[3:4, :])
        for g in range(NBD):
            sl = slice(g * BD, (g + 1) * BD)
            _, ig, a, mult = _lru_gates(xa[:, sl], wa_ref[g], wx_ref[g], vec_v[1:2, sl], vec_v[2:3, sl], sp[:, sl])
            a_s[:, sl] = a
            b_s[:, sl] = mult * ig * xa[:, sl]

        def slab(s, carry):
            r0 = pl.multiple_of(s * SLAB, SLAB)
            A, B = _slab_scan_fwd(a_s[pl.ds(r0, SLAB), :], b_s[pl.ds(r0, SLAB), :])
            h = A * carry + B
            h_ref[pl.ds(r0, SLAB), :] = h
            return h[SLAB - 1:SLAB, :]

        carry_ref[0:1, :] = lax.fori_loop(0, nsl, slab, carry_ref[0:1, :])

    blk = 5 * _nbytes((tm, D), F32) + 2 * _nbytes((NBD, BD, BD), BF)
    return pl.pallas_call(
        body, name="lru_fwd", grid=(T // tm,),
        in_specs=[pl.BlockSpec((tm, D), lambda i: (i, 0)),
                  pl.BlockSpec((2 * SLAB, D), lambda i: (jnp.maximum(i * (nsl // 2) - 1, 0), 0)),
                  pl.BlockSpec((CONV, D), lambda i: (0, 0)),
                  pl.BlockSpec((SLAB, D), lambda i: (0, 0)),
                  pl.BlockSpec((NBD, BD, BD), lambda i: (0, 0, 0)),
                  pl.BlockSpec((NBD, BD, BD), lambda i: (0, 0, 0))],
        out_specs=pl.BlockSpec((tm, D), lambda i: (i, 0)),
        out_shape=jax.ShapeDtypeStruct((T, D), F32),
        scratch_shapes=[pltpu.VMEM((tm, D), F32), pltpu.VMEM((tm, D), F32), pltpu.VMEM((SLAB, D), F32)],
        compiler_params=pltpu.CompilerParams(dimension_semantics=("arbitrary",), vmem_limit_bytes=_vmem_limit(blk)),
    )(xg, xg, cw, vec, wabd, wxbd)


def _lru_bwd(xg, h, dha, cw, vec, wabd, wxbd, T, tm=256):
    tm = min(tm, T)
    nsl = tm // SLAB
    nt = T // tm

    def body(x_ref, xp_ref, h_ref, hp_ref, dh_ref, cw_ref, vec_ref, wa_ref, wx_ref,
             dx_ref, dwa_ref, dwx_ref, acc_ref, a_s, b_s, g_s, dxa_s, carry_ref, dxan_ref):
        n = pl.program_id(0)
        it = nt - 1 - n

        @pl.when(n == 0)
        def _():
            carry_ref[...] = jnp.zeros_like(carry_ref)
            dxan_ref[...] = jnp.zeros_like(dxan_ref)
            dwa_ref[...] = jnp.zeros_like(dwa_ref)
            dwx_ref[...] = jnp.zeros_like(dwx_ref)
            acc_ref[...] = jnp.zeros_like(acc_ref)

        x = x_ref[...].astype(F32)
        prev8 = jnp.where(it > 0, xp_ref[SLAB:, :].astype(F32), 0.0)
        hprev8 = jnp.where(it > 0, hp_ref[...], 0.0)
        vec_v = vec_ref[...]
        cw_v = cw_ref[...]
        xa, xs = _conv(x, prev8, cw_v, vec_v[0:1, :])
        sp = _softplus(-vec_v[3:4, :])
        gates = []
        for g in range(NBD):
            sl = slice(g * BD, (g + 1) * BD)
            r, ig, a, mult = _lru_gates(xa[:, sl], wa_ref[g], wx_ref[g], vec_v[1:2, sl], vec_v[2:3, sl], sp[:, sl])
            gates.append((r, ig, a, mult))
            a_s[:, sl] = a
        a_s[...] = _shift_up(a_s[...], 1, carry_ref[...])
        b_s[...] = dh_ref[...]

        def slab(m, carry):
            r0 = pl.multiple_of((nsl - 1 - m) * SLAB, SLAB)
            A, B = _slab_scan_bwd(a_s[pl.ds(r0, SLAB), :], b_s[pl.ds(r0, SLAB), :])
            gg = A * carry + B
            g_s[pl.ds(r0, SLAB), :] = gg
            return gg[0:1, :]

        g_first = lax.fori_loop(0, nsl, slab, carry_ref[1:2, :])
        gt = g_s[...]
        h_prev = _shift_down(h_ref[...], 1, hprev8)
        dba = []
        dbx = []
        dsp = []
        for g in range(NBD):
            sl = slice(g * BD, (g + 1) * BD)
            r, ig, a, mult = gates[g]
            xa_g = xa[:, sl]
            g_g = gt[:, sl]
            da = g_g * h_prev[:, sl]
            dmult = g_g * ig * xa_g
            di = g_g * mult * xa_g
            dxa_g = g_g * mult * ig
            dla = da * a - dmult * (a * a / mult)
            dr = dla * (-LRU_C) * sp[:, sl]
            dsp.append(_colsum(dla * (-LRU_C) * r))
            dra = (dr * r * (1.0 - r))
            dix = (di * ig * (1.0 - ig))
            dba.append(_colsum(dra))
            dbx.append(_colsum(dix))
            dra_b = dra.astype(BF)
            dix_b = dix.astype(BF)
            xb = xa_g.astype(BF)
            dxa_g = dxa_g + _dot(dra_b, wa_ref[g], "nt") + _dot(dix_b, wx_ref[g], "nt")
            dwa_ref[g] += _dot(xb, dra_b, "tn")
            dwx_ref[g] += _dot(xb, dix_b, "tn")
            dxa_s[:, sl] = dxa_g
        dxa = dxa_s[...]
        nxt = dxan_ref[...]
        dx = (cw_v[3:4, :] * dxa + cw_v[2:3, :] * _shift_up(dxa, 1, nxt)
              + cw_v[1:2, :] * _shift_up(dxa, 2, nxt) + cw_v[0:1, :] * _shift_up(dxa, 3, nxt))
        dx_ref[...] = dx.astype(dx_ref.dtype)
        acc_ref[0:1, :] += jnp.concatenate(dba, axis=1)
        acc_ref[1:2, :] += jnp.concatenate(dbx, axis=1)
        acc_ref[2:3, :] += jnp.concatenate(dsp, axis=1)
        acc_ref[3:4, :] += _colsum(dxa)
        for k in range(CONV):
            acc_ref[4 + k:5 + k, :] += _colsum(dxa * xs[CONV - 1 - k])
        dxan_ref[...] = dxa[0:SLAB, :]
        a_first = jnp.concatenate([gates[g][2][0:1, :] for g in range(NBD)], axis=1)
        carry_ref[0:1, :] = a_first
        carry_ref[1:2, :] = g_first

        @pl.when(n == nt - 1)
        def _():
            acc_ref[2:3, :] = acc_ref[2:3, :] * (-_sig(-vec_v[3:4, :]))

    rowblk = lambda i: (nt - 1 - i, 0)
    prevblk = lambda i: (jnp.maximum((nt - 1 - i) * nsl - 1, 0), 0)
    c2 = lambda i: (0, 0)
    c3 = lambda i: (0, 0, 0)
    blk = 12 * _nbytes((tm, D), F32) + 6 * _nbytes((NBD, BD, BD), F32)
    return pl.pallas_call(
        body, name="lru_bwd", grid=(nt,),
        in_specs=[pl.BlockSpec((tm, D), rowblk),
                  pl.BlockSpec((2 * SLAB, D), lambda i: (jnp.maximum((nt - 1 - i) * (nsl // 2) - 1, 0), 0)),
                  pl.BlockSpec((tm, D), rowblk), pl.BlockSpec((SLAB, D), prevblk),
                  pl.BlockSpec((tm, D), rowblk),
                  pl.BlockSpec((CONV, D), c2), pl.BlockSpec((SLAB, D), c2),
                  pl.BlockSpec((NBD, BD, BD), c3), pl.BlockSpec((NBD, BD, BD), c3)],
        out_specs=[pl.BlockSpec((tm, D), rowblk), pl.BlockSpec((NBD, BD, BD), c3), pl.BlockSpec((NBD, BD, BD), c3),
                   pl.BlockSpec((16, D), c2)],
        out_shape=[jax.ShapeDtypeStruct((T, D), BF), jax.ShapeDtypeStruct((NBD, BD, BD), F32),
                   jax.ShapeDtypeStruct((NBD, BD, BD), F32), jax.ShapeDtypeStruct((16, D), F32)],
        scratch_shapes=[pltpu.VMEM((tm, D), F32), pltpu.VMEM((tm, D), F32), pltpu.VMEM((tm, D), F32),
                        pltpu.VMEM((tm, D), F32), pltpu.VMEM((SLAB, D), F32), pltpu.VMEM((SLAB, D), F32)],
        compiler_params=pltpu.CompilerParams(dimension_semantics=("arbitrary",), vmem_limit_bytes=_vmem_limit(blk)),
    )(xg, xg, h, h, dha, cw, vec, wabd, wxbd)


_SCALE = 1.0 / math.sqrt(DH)


_ANY = pl.BlockSpec(memory_space=pl.ANY)


class _Side:
    def __init__(self, srcs, outs, nsem, copies):
        self.srcs, self.outs, self.nsem, self.copies = list(srcs), list(outs), nsem, copies


def _pallas(body, operands, *, name, grid, in_specs, out_specs, out_shape, scratch_shapes=(), semantics,
            vmem=None, side=None):
    if side is None:
        return pl.pallas_call(
            body, name=name, grid=grid, in_specs=in_specs, out_specs=out_specs, out_shape=out_shape,
            scratch_shapes=list(scratch_shapes),
            compiler_params=pltpu.CompilerParams(dimension_semantics=semantics, vmem_limit_bytes=vmem),
        )(*operands)
    n_in, n_out, n_scr = len(in_specs), len(out_specs), len(scratch_shapes)
    ns, no = len(side.srcs), len(side.outs)

    def hosted(*refs):
        ins, refs = refs[:n_in], refs[n_in:]
        sin, refs = refs[:ns], refs[ns:]
        outs, refs = refs[:n_out], refs[n_out:]
        sout, refs = refs[:no], refs[no:]
        scr, (send, recv) = refs[:n_scr], refs[n_scr:]
        ids = [pl.program_id(a) for a in range(len(grid))]
        first = functools.reduce(jnp.logical_and, [i == 0 for i in ids])
        last = functools.reduce(jnp.logical_and, [i == g - 1 for i, g in zip(ids, grid)])

        @pl.when(first)
        def _():
            for cp in side.copies(sin, sout, send, recv):
                cp.start()

        body(*ins, *outs, *scr)

        @pl.when(last)
        def _():
            for cp in side.copies(sin, sout, send, recv):
                cp.wait()

    return pl.pallas_call(
        hosted, name=name, grid=grid, in_specs=list(in_specs) + [_ANY] * ns, out_specs=list(out_specs) + [_ANY] * no,
        out_shape=list(out_shape) + side.outs,
        scratch_shapes=list(scratch_shapes) + [pltpu.SemaphoreType.DMA((side.nsem,)), pltpu.SemaphoreType.DMA((side.nsem,))],
        compiler_params=pltpu.CompilerParams(dimension_semantics=("arbitrary",) * len(grid), vmem_limit_bytes=vmem),
    )(*operands, *side.srcs)


DA = 2 * DH
_LOG2E = math.log2(math.e)
_C2 = _SCALE * _LOG2E


def _aug_fn(ins, cs):
    q, k, fcum = ins
    g_all = fcum * _LOG2E
    lane = lax.broadcasted_iota(jnp.int32, (q.shape[0], DH), 1)
    qa, ka = [], []
    for hd in range(NH):
        g = g_all[:, hd:hd + 1]
        hi = g.astype(BF).astype(F32)
        mid = (g - hi).astype(BF).astype(F32)
        lo = ((g - hi) - mid).astype(BF).astype(F32)
        qx = jnp.where(lane == 0, hi, jnp.where(lane == 1, mid, jnp.where(lane == 2, lo,
                                                                          jnp.where(lane < 6, 1.0, 0.0))))
        kx = jnp.where(lane < 3, 1.0, jnp.where(lane == 3, -hi, jnp.where(lane == 4, -mid,
                                                                          jnp.where(lane == 5, -lo, 0.0))))
        qa += [q[:, hd * DH:(hd + 1) * DH], qx.astype(BF)]
        ka += [k[:, hd * DH:(hd + 1) * DH], kx.astype(BF)]
    return [jnp.concatenate(qa, axis=1), jnp.concatenate(ka, axis=1)], []


_KA_ONES = DH + 3
KT_ONES = 16
FWD_KEY_PARTS = 2
BWD_QUERY_PARTS = 2


def _attn_fwd(qa, ka, qkv, T, blk=512, side=None):
    blk = min(blk, T)
    nb = T // blk
    half = blk // 2

    def body(q_ref, k_ref, vn_ref, o_ref, lse_ref, v_ref):
        i = pl.program_id(1)

        @pl.when(i == 0)
        def _():
            for jj in range(nb):
                v_ref[jj] = jnp.concatenate([vn_ref[jj * blk:(jj + 1) * blk, :].astype(F32).T.astype(BF),
                                             jnp.ones((KT_ONES, blk), BF)], axis=0)

        q = q_ref[...]

        def scores(j):
            r0 = pl.multiple_of(j * blk, blk)
            return _dot(k_ref[pl.ds(r0, blk), :], q, "nt")

        def update(s, vj, carry):
            m, acc = carry
            m_new = jnp.maximum(m, jnp.max(s, axis=0, keepdims=True))
            alpha = jnp.exp2(m - m_new)
            acc = alpha * acc + _dot(vj, jnp.exp2(s - m_new).astype(BF))
            return m_new, acc

        def step(j, st):
            r0 = pl.multiple_of(j * blk, blk)
            part = blk // FWD_KEY_PARTS
            ss = [_dot(k_ref[pl.ds(r0 + n * part, part), :], q, "nt") for n in range(FWD_KEY_PARTS)]
            vj = v_ref[j]
            for n in range(FWD_KEY_PARTS):
                st = update(ss[n], vj[:, n * part:(n + 1) * part], st)
            return st

        init = (jnp.full((1, blk), -jnp.inf, F32), jnp.zeros((DH + KT_ONES, blk), F32))
        carry = lax.fori_loop(0, i, step, init)
        last = scores(i)
        vi = v_ref[i]
        rk = lax.broadcasted_iota(jnp.int32, (half, blk), 0)
        cq = lax.broadcasted_iota(jnp.int32, (half, blk), 1)
        m, acc = update(jnp.where(cq >= rk, last[:half], -jnp.inf), vi[:, :half], carry)
        s2 = jnp.where(cq[:, :half] >= rk[:, :half], last[half:, half:], -jnp.inf)
        m2, acc2 = update(s2, vi[:, half:], (m[:, half:], acc[:, half:]))
        m = jnp.concatenate([m[:, :half], m2], axis=1)
        acc = jnp.concatenate([acc[:, :half], acc2], axis=1)
        l = acc[DH:DH + 1]
        o_ref[...] = (acc[:DH] / l).T.astype(o_ref.dtype)
        lse_ref[...] = m + jnp.log(l) * _LOG2E

    vm = _nbytes((T, DA), BF) + 2 * _nbytes((T, DH), BF) + 6 * _nbytes((blk, blk), F32)
    return _pallas(
        body, (qa, ka, qkv), name="attn_fwd", grid=(NH, nb),
        in_specs=[pl.BlockSpec((blk, DA), lambda h, i: (i, h)),
                  pl.BlockSpec((T, DA), lambda h, i: (0, h)),
                  pl.BlockSpec((T, DH), lambda h, i: (0, 2 * NH + h))],
        out_specs=[pl.BlockSpec((blk, DH), lambda h, i: (i, h)),
                   pl.BlockSpec((None, None, 1, blk), lambda h, i: (h, i, 0, 0))],
        out_shape=[jax.ShapeDtypeStruct((T, D), BF), jax.ShapeDtypeStruct((NH, nb, 1, blk), F32)],
        scratch_shapes=[pltpu.VMEM((nb, DH + KT_ONES, blk), BF)],
        semantics=("parallel", "arbitrary"), vmem=_vmem_limit(vm), side=side)


def _attn_bwd(qa, ka, qkv, do, lrow, drow, T, blk=512, side=None):
    blk = min(blk, T)
    nb = T // blk

    def body(ka_ref, v_ref, qa_ref, do_ref, l_ref, d_ref, dq_ref, dk_ref, dv_ref, dfs_ref, dft_ref, dq_s,
             dka_s, dv_s):
        j = pl.program_id(1)

        @pl.when(j == 0)
        def _():
            dq_s[...] = jnp.zeros_like(dq_s)

        row = lax.broadcasted_iota(jnp.int32, (DH + KT_ONES, blk), 0)
        dq_scale = jnp.where(row < DH, _SCALE, 1.0)

        kaj = ka_ref[...]
        ktj = jnp.concatenate([kaj[:, :DH].astype(F32).T.astype(BF), jnp.ones((KT_ONES, blk), BF)], axis=0)
        vj = v_ref[...]

        def step(i, carry):
            r0 = pl.multiple_of(i * blk, blk)
            part = blk // BWD_QUERY_PARTS
            lr, dr = l_ref[i], d_ref[i]
            loaded = []
            for n in range(BWD_QUERY_PARTS):
                qn = qa_ref[pl.ds(r0 + n * part, part), :]
                don = do_ref[pl.ds(r0 + n * part, part), :]
                loaded.append((qn, don, _dot(kaj, qn, "nt"), _dot(vj, don, "nt")))
            dq_parts = []
            for n, (qn, don, raw, dpt) in enumerate(loaded):
                pt = jnp.exp2(raw - lr[:, n * part:(n + 1) * part])
                dsb = (pt * (dpt - dr[:, n * part:(n + 1) * part])).astype(BF)
                dv_s[...] += _dot(pt.astype(BF), don)
                dka_s[...] += _dot(dsb, qn)
                dq_parts.append(_dot(ktj, dsb))
            dq_s[i] += jnp.concatenate(dq_parts, axis=1) * dq_scale
            return carry

        def diagonal():
            half = blk // 2
            r0 = pl.multiple_of(j * blk, blk)
            qi = qa_ref[pl.ds(r0, blk), :]
            doi = do_ref[pl.ds(r0, blk), :]
            lr, dr = l_ref[j], d_ref[j]
            rk = lax.broadcasted_iota(jnp.int32, (half, blk), 0)
            cq = lax.broadcasted_iota(jnp.int32, (half, blk), 1)

            def quarter(ka, v, q, do, l2, d2, keep):
                pt = jnp.exp2(jnp.where(keep, _dot(ka, q, "nt") - l2, -jnp.inf))
                dsb = (pt * (_dot(v, do, "nt") - d2)).astype(BF)
                return _dot(dsb, q), _dot(pt.astype(BF), do), dsb

            dka1, dv1, ds1 = quarter(kaj[:half], vj[:half], qi, doi, lr, dr, cq >= rk)
            dka2, dv2, ds2 = quarter(kaj[half:], vj[half:], qi[half:], doi[half:], lr[:, half:], dr[:, half:],
                                     cq[:, :half] >= rk[:, :half])
            dq2 = jnp.concatenate([jnp.zeros((DH + KT_ONES, half), F32), _dot(ktj[:, half:], ds2)], axis=1)
            dq_s[j] += (_dot(ktj[:, :half], ds1) + dq2) * dq_scale
            dka_s[...] = jnp.concatenate([dka1, dka2], axis=0)
            dv_s[...] = jnp.concatenate([dv1, dv2], axis=0)

        diagonal()
        lax.fori_loop(j + 1, nb, step, 0)
        dka, dv = dka_s[...], dv_s[...]
        dk_ref[...] = (dka[:, :DH] * (1.0 / _LOG2E)).astype(dk_ref.dtype)
        dv_ref[...] = dv.astype(dv_ref.dtype)
        dfs_ref[...] = dka[:, DH:].T[_KA_ONES - DH:_KA_ONES - DH + 1, :]

        @pl.when(j == nb - 1)
        def _():
            for ii in range(nb):
                t = dq_s[ii]
                dq_ref[ii * blk:(ii + 1) * blk, :] = t[:DH].T.astype(dq_ref.dtype)
                dft_ref[ii] = t[DH:DH + 1]

    rowv = pl.BlockSpec((None, nb, 1, blk), lambda h, j: (h, 0, 0, 0))
    vm = (_nbytes((T, DA), BF) + 2 * _nbytes((T, DH), BF) + _nbytes((T, DH + KT_ONES), F32)
          + 8 * _nbytes((blk, blk), F32))
    return _pallas(
        body, (ka, qkv, qa, do, lrow, drow), name="attn_bwd", grid=(NH, nb),
        in_specs=[pl.BlockSpec((blk, DA), lambda h, j: (j, h)),
                  pl.BlockSpec((blk, DH), lambda h, j: (j, 2 * NH + h)),
                  pl.BlockSpec((T, DA), lambda h, j: (0, h)),
                  pl.BlockSpec((T, DH), lambda h, j: (0, h)),
                  rowv, rowv],
        out_specs=[pl.BlockSpec((T, DH), lambda h, j: (0, h)),
                   pl.BlockSpec((blk, DH), lambda h, j: (j, h)),
                   pl.BlockSpec((blk, DH), lambda h, j: (j, h)),
                   pl.BlockSpec((None, None, 1, blk), lambda h, j: (h, j, 0, 0)), rowv],
        out_shape=[jax.ShapeDtypeStruct((T, D), BF), jax.ShapeDtypeStruct((T, D), BF),
                   jax.ShapeDtypeStruct((T, D), BF), jax.ShapeDtypeStruct((NH, nb, 1, blk), F32),
                   jax.ShapeDtypeStruct((NH, nb, 1, blk), F32)],
        scratch_shapes=[pltpu.VMEM((nb, DH + KT_ONES, blk), F32), pltpu.VMEM((blk, DA), F32),
                        pltpu.VMEM((blk, DH), F32)],
        semantics=("parallel", "arbitrary"), vmem=_vmem_limit(vm), side=side)


def _norm_fn(ins, cs):
    x, = ins
    g, = cs
    r = lax.rsqrt(jnp.mean(x * x, axis=-1, keepdims=True) + EPS)
    return [x * r * g], []


def _norm_bwd_fn(ins, cs):
    x, dy, dres = ins
    g, = cs
    r = lax.rsqrt(jnp.mean(x * x, axis=-1, keepdims=True) + EPS)
    xh = x * r
    dxh = dy * g
    dx = dres + r * (dxh - xh * jnp.mean(dxh * xh, axis=-1, keepdims=True))
    return [dx, dx], [_colsum(dy * xh)]


def _final_fn(ins, cs):
    x2, tgt = ins
    g, = cs
    r = lax.rsqrt(jnp.mean(x2 * x2, axis=-1, keepdims=True) + EPS)
    xh = x2 * r
    e = xh * g - tgt
    dy = e * (1.0 / D)
    dxh = dy * g
    dx2 = r * (dxh - xh * jnp.mean(dxh * xh, axis=-1, keepdims=True))
    return [dx2, dx2], [_colsum(0.5 * e * e * (1.0 / D)), _colsum(dy * xh)]


def _z_fn(ins, cs):
    g, h = [v.astype(F32) for v in ins]
    return [_gelu(g) * h], []


def _mix_fn(ins, cs):
    gates, ya, yb = [v.astype(F32) for v in ins]
    return [_sig(gates[:, :D]) * ya + _sig(gates[:, D:]) * yb], []


def _mix_bwd_fn(ins, cs):
    dmix, gates, ya, yb = [v.astype(F32) for v in ins]
    ga = _sig(gates[:, :D])
    gb = _sig(gates[:, D:])
    dgates = jnp.concatenate([dmix * ya * ga * (1.0 - ga), dmix * yb * gb * (1.0 - gb)], axis=1)
    return [dmix * ga, dmix * gb, dgates], []


def _z_bwd_fn(ins, cs):
    dz, g, h = [v.astype(F32) for v in ins]
    return [dz * _gelu(g), dz * h * _gelu_grad(g)], []


def _delta_fn(ins, cs):
    do, o = ins
    p = do.astype(F32) * o.astype(F32)
    lane = lax.broadcasted_iota(jnp.int32, (p.shape[0], DH), 1)
    out = jnp.zeros((p.shape[0], DH), F32)
    for hd in range(NH):
        s = jnp.sum(p[:, hd * DH:(hd + 1) * DH], axis=1, keepdims=True)
        out = jnp.where(lane == hd, s, out)
    return [out], []


def _du_all(pieces, win, T, tm=256, side=None):
    tm = min(tm, T)
    n = len(pieces)

    def body(*refs):
        w_ref, o_ref = refs[n], refs[n + 1]
        acc = None
        for (a, off), a_ref in zip(pieces, refs[:n]):
            d = _dot(a_ref[...].astype(BF), w_ref[:, off:off + a.shape[1]], "nt")
            acc = d if acc is None else acc + d
        o_ref[...] = acc

    vm = (sum(_nbytes((tm, a.shape[1]), a.dtype) for a, _ in pieces) + _nbytes(win.shape, win.dtype)
          + 2 * _nbytes((tm, D), F32))
    return _pallas(
        body, tuple(a for a, _ in pieces) + (win,), name="du_all", grid=(T // tm,),
        in_specs=[pl.BlockSpec((tm, a.shape[1]), lambda i: (i, 0)) for a, _ in pieces]
        + [pl.BlockSpec(win.shape, lambda i: (0, 0))],
        out_specs=[pl.BlockSpec((tm, D), lambda i: (i, 0))],
        out_shape=[jax.ShapeDtypeStruct((T, D), F32)],
        semantics=("arbitrary",), vmem=int(min(VMEM_CAP, 2 * vm + (4 << 20))), side=side)


def _local_step(x, tgt, w, T, blk=1024, dist=None):
    blk = min(blk, T)
    nb = T // blk
    win = w["win"]

    u, = _ew(_norm_fn, T, 1024, [(x, D, 0)], [w["g_mix"]], [(D, BF)], [], name="norm_mix")
    xg = _mm(u, win, "nn", T, 2 * D, D, name="proj_lru", out_dtype=BF)
    qkv = _mm(u, win, "nn", T, 3 * D, D, name="proj_qkv", out_dtype=BF, b_off=(0, 2),
              epi=lambda acc: acc * jnp.where(pl.program_id(1) == 0, _C2, 1.0))
    gates = _mm(u, win, "nn", T, 2 * D, D, name="proj_gates", b_off=(0, 5), out_dtype=BF)
    fl = _mm(u, win, "nn", T, DH, D, name="proj_f", tn=DH, b_off=(0, 7 * D // DH))
    fcum = _fgate_fwd(fl, w["fb"], T)
    qa, ka = _ew(_aug_fn, T, 1024,[(qkv, D, 0), (qkv, D, 1), (fcum, DH, 0)], [],
                 [(NH * DA, BF), (NH * DA, BF)], [], name="attn_augment")

    h = _lru_fwd(xg, w["cw"], w["vec"], w["wabd"], w["wxbd"], T)
    ob, lse, *landed = _attn_fwd(qa, ka, qkv, T, min(2 * blk, T), side=dist.weights_side() if dist else None)
    lse = lse.reshape(NH, nb, 1, blk)
    if dist:
        w = dict(w, **dist.weights_landed(landed))
    z, = _ew(_z_fn, T, 1024, [(xg, D, 1), (h, D, 0)], [], [(D, BF)], [], name="lru_gelu")
    ya = _mm(z, w["wa"], "nn", T, D, D, name="branch_a", out_dtype=BF)
    yb = _mm(ob, w["wb"], "nn", T, D, D, name="branch_b", out_dtype=BF)
    mix, = _ew(_mix_fn, T, 1024,[(gates, 2 * D, 0), (ya, D, 0), (yb, D, 0)], [], [(D, BF)], [], name="mix")
    x1 = _mm(mix, w["wout"], "nn", T, D, D, name="out_proj", add=x)
    m, = _ew(_norm_fn, T, 1024, [(x1, D, 0)], [w["g_mlp"]], [(D, BF)], [], name="norm_mlp")
    hh = _mm(m, w["wup"], "nn", T, FF, D, name="mlp_up", out_dtype=BF,
             epi=lambda acc: jnp.square(jnp.maximum(acc, 0.0)))
    x2 = _mm(hh, w["wdown"], "nn", T, D, FF, name="mlp_down", add=x1, tk=FF)
    dx2, dx2b, loss_vec, dg_fin = _ew(_final_fn, T, 1024,[(x2, D, 0), (tgt, D, 0)], [w["g_fin"]],
                                      [(D, F32), (D, BF)], [(1, D), (1, D)], name="final_norm_loss")

    dhpre = _mm(dx2b, w["wdown"], "nt", T, FF, D, name="mlp_down_bwd", out_dtype=BF,
                epi=lambda acc, h2: acc * (2.0 * jnp.sqrt(h2.astype(F32))), epi_ins=[hh])
    dwdown = _mm(hh, dx2b, "tn", FF, D, T, name="dw_down", out_dtype=BF)
    dwup = _mm(m, dhpre, "tn", D, FF, T, name="dw_up", out_dtype=BF)
    dm = _mm(dhpre, w["wup"], "nt", T, D, FF, name="mlp_up_bwd", tk=FF)
    dx1, dx1b, dg_mlp = _ew(_norm_bwd_fn, T, 512, [(x1, D, 0), (dm, D, 0), (dx2, D, 0)], [w["g_mlp"]],
                            [(D, F32), (D, BF)], [(1, D)], name="norm_mlp_bwd")

    dmix = _mm(dx1b, w["wout"], "nt", T, D, D, name="out_proj_bwd", out_dtype=BF)
    dwout = _mm(mix, dx1b, "tn", D, D, T, name="dw_out", out_dtype=BF)
    dya, dyb, dgates = _ew(_mix_bwd_fn, T, 512, [(dmix, D, 0), (gates, 2 * D, 0), (ya, D, 0), (yb, D, 0)], [],
                           [(D, BF), (D, BF), (2 * D, BF)], [], name="mix_bwd")
    dob = _mm(dyb, w["wb"], "nt", T, D, D, name="branch_b_bwd", out_dtype=BF)
    dwb = _mm(ob, dyb, "tn", D, D, T, name="dw_b", out_dtype=BF)
    dz = _mm(dya, w["wa"], "nt", T, D, D, name="branch_a_bwd", out_dtype=BF)
    dwa = _mm(z, dya, "tn", D, D, T, name="dw_a", out_dtype=BF)
    dha, dglru = _ew(_z_bwd_fn, T, 1024,[(dz, D, 0), (xg, D, 1), (h, D, 0)], [], [(D, F32), (D, BF)], [],
                     name="lru_gelu_bwd")

    delta, = _ew(_delta_fn, T, 1024, [(dob, D, 0), (ob, D, 0)], [], [(DH, F32)], [], name="attn_delta")
    drow = delta[:, :NH].T.reshape(NH, nb, 1, blk)
    big = dict(w_branch_a=dwa, w_branch_b=dwb, w_out=dwout, w_up=dwup, w_down=dwdown)
    side = dist.grads_side(big) if dist else None
    dq, dk, dv, dfs, dft, *landed = _attn_bwd(qa, ka, qkv, dob, lse, drow, T, blk, side=side)
    if dist:
        big = dist.grads_landed(side, landed)
    dfcum = jnp.pad((dft - dfs).reshape(NH, T).T, ((0, 0), (0, DH - NH)))
    dfl, dfb = _fgate_bwd(dfcum, fl, w["fb"], T)

    dxl, dwabd, dwxbd, lacc = _lru_bwd(xg, h, dha, w["cw"], w["vec"], w["wabd"], w["wxbd"], T)

    dproj = ((dxl, 0), (dglru, D), (dq, 2 * D), (dk, 3 * D), (dv, 4 * D), (dgates, 5 * D), (dfl, 7 * D))
    pieces = [_mm(u, p, "tn", D, p.shape[1], T, name="dw_in_%d" % n, out_dtype=BF)
              for n, (p, _) in enumerate(dproj)]
    pieces[-1] = pieces[-1][:, :NH]
    dwin = dict(w_in=jnp.concatenate(pieces, axis=1))
    side = dist.grads_side(dwin) if dist else None
    du, *landed = _du_all(dproj, win, T, side=side)
    big.update(dist.grads_landed(side, landed) if dist else dwin)
    dx, dg_mix = _ew(_norm_bwd_fn, T, 512, [(x, D, 0), (du, D, 0), (dx1, D, 0)], [w["g_mix"]], [(D, F32)],
                     [(1, D)], name="norm_mix_bwd")

    return dict(dx=dx, big=big, dwabd=dwabd, dwxbd=dwxbd, lacc=lacc, dfb=dfb, dg_mix=dg_mix, dg_mlp=dg_mlp,
                dg_fin=dg_fin, loss_vec=loss_vec)


def _block_diag(w):
    per = BD // LRU_BW
    w4 = w.reshape(NBD, per, LRU_BW, LRU_BW)
    on_diagonal = jnp.eye(per, dtype=bool)[None, :, None, :, None]
    return jnp.where(on_diagonal, w4[:, :, :, None, :], 0.0).reshape(NBD, BD, BD)


def _block_diag_extract(wbd):
    per = BD // LRU_BW
    w5 = wbd.reshape(NBD, per, LRU_BW, per, LRU_BW)
    return jnp.stack([w5[:, b, :, b, :] for b in range(per)], axis=1).reshape(LRU_BLOCKS, LRU_BW, LRU_BW)


def _place():
    x, y, c = lax.axis_index("x"), lax.axis_index("y"), lax.axis_index("c")
    chips = [(1 - x, y), (x, 1 - y), (1 - x, 1 - y)]
    return x, y, c, chips


def _allgather_shards(shards):
    n = len(shards)

    def body(*refs):
        ins, outs = refs[:n], refs[n:2 * n]
        send_sems, recv_sems = refs[2 * n:]
        x, y, c, chips = _place()
        me = 2 * x + y
        sibling = (x, y, 1 - c)

        def remote(p, k, src, dst, to):
            return pltpu.make_async_remote_copy(src_ref=src, dst_ref=dst, send_sem=send_sems.at[p, k],
                                                recv_sem=recv_sems.at[p, k], device_id=to, device_id_type=MESH)

        sent = []
        for p in range(n):
            for k, chip in enumerate(chips):
                cp = remote(p, k, ins[p].at[c], outs[p].at[me, c], (chip[0], chip[1], c))
                cp.start()
                sent.append(cp)
        for p in range(n):
            for k, chip in enumerate(chips):
                half = outs[p].at[2 * chip[0] + chip[1], c]
                remote(p, k, half, half, sibling).wait_recv()
                fwd = remote(p, 3 + k, half, half, sibling)
                fwd.start()
                sent.append(fwd)
        for p in range(n):
            for k, chip in enumerate(chips):
                half = outs[p].at[2 * chip[0] + chip[1], 1 - c]
                remote(p, 3 + k, half, half, sibling).wait_recv()
        for cp in sent:
            cp.wait_send()

    gathered = pl.pallas_call(
        body, name="allgather_weights",
        in_specs=[_ANY] * n, out_specs=[_ANY] * n,
        out_shape=[jax.ShapeDtypeStruct((NCHIP,) + s.shape, s.dtype) for s in shards],
        scratch_shapes=[pltpu.SemaphoreType.DMA((n, 6)), pltpu.SemaphoreType.DMA((n, 6))],
    )(*shards)
    me = 2 * lax.axis_index("x") + lax.axis_index("y")
    return [lax.dynamic_update_index_in_dim(g, s, me, 0) for g, s in zip(gathered, shards)]


_LATE = ["w_branch_a", "w_branch_b", "w_out", "w_up", "w_down"]
_COLUMN_CUT = ("w_in", "w_up")
N_PEERS = 7


def _shard_major(name, g):
    s = _columns_to_shards(g) if name in _COLUMN_CUT else g.reshape(NCHIP, g.shape[0] // NCHIP, g.shape[1])
    return s.reshape(NCHIP, 2, s.shape[1] // 2, s.shape[2])


class _Exchanges:
    def __init__(self, shards):
        self.shards = shards

    def weights_side(self):
        srcs = [self.shards[n] for n in _LATE]

        def copies(sin, sout, send, recv):
            x, y, c, chips = _place()
            return [pltpu.make_async_remote_copy(
                src_ref=sin[p], dst_ref=sout[p].at[2 * x + y], send_sem=send.at[3 * p + k], recv_sem=recv.at[3 * p + k],
                device_id=(chip[0], chip[1], c), device_id_type=MESH)
                for p in range(len(sin)) for k, chip in enumerate(chips)]

        return _Side(srcs, [jax.ShapeDtypeStruct((NCHIP,) + s.shape, s.dtype) for s in srcs], 3 * len(srcs), copies)

    def weights_landed(self, landed):
        me = 2 * lax.axis_index("x") + lax.axis_index("y")
        full = {n: lax.dynamic_update_index_in_dim(g, self.shards[n], me, 0) for n, g in zip(_LATE, landed)}
        return dict(wa=full["w_branch_a"].reshape(D, D), wb=full["w_branch_b"].reshape(D, D),
                    wout=full["w_out"].reshape(D, D), wup=_shards_to_columns(full["w_up"]),
                    wdown=full["w_down"].reshape(FF, D))

    def grads_side(self, grads):
        side_names = list(grads)
        srcs = [_shard_major(n, grads[n]) for n in side_names]

        def copies(sin, sout, send, recv):
            x, y, c, chips = _place()
            peers = [(x, y, 1 - c)] + [(cx, cy, c) for cx, cy in chips] + [(cx, cy, 1 - c) for cx, cy in chips]
            return [pltpu.make_async_remote_copy(
                src_ref=sin[p].at[2 * px + py, pc], dst_ref=sout[p].at[s], send_sem=send.at[N_PEERS * p + s],
                recv_sem=recv.at[N_PEERS * p + s], device_id=(px, py, pc), device_id_type=MESH)
                for p in range(len(sin)) for s, (px, py, pc) in enumerate(peers)]

        side = _Side(srcs, [jax.ShapeDtypeStruct((N_PEERS,) + s.shape[2:], s.dtype) for s in srcs],
                     N_PEERS * len(srcs), copies)
        side.names = side_names
        return side

    def grads_landed(self, side, landed):
        return {n: (own, got) for n, own, got in zip(side.names, side.srcs, landed)}


def _add8(g, got, me, c, name):
    _, _, half, cols = g.shape
    th = _row_tile(half, 2 * cols)

    def body(me_ref, c_ref, g_ref, r_ref, o_ref):
        acc = g_ref[...].astype(F32)
        for s in range(N_PEERS):
            acc = acc + r_ref[s].astype(F32)
        o_ref[...] = acc

    return pl.pallas_call(
        body, name=name,
        grid_spec=pltpu.PrefetchScalarGridSpec(
            num_scalar_prefetch=2, grid=(half // th,),
            in_specs=[pl.BlockSpec((None, None, th, cols), lambda i, me_ref, c_ref: (me_ref[0], c_ref[0], i, 0)),
                      pl.BlockSpec((N_PEERS, th, cols), lambda i, me_ref, c_ref: (0, i, 0))],
            out_specs=pl.BlockSpec((th, cols), lambda i, me_ref, c_ref: (i, 0))),
        out_shape=jax.ShapeDtypeStruct((half, cols), F32),
    )(me, c, g, got)


def _share_halves(halves):
    n = len(halves)

    def body(*refs):
        ins, outs = refs[:n], refs[n:2 * n]
        send_sems, recv_sems = refs[2 * n:]
        x, y, c, _ = _place()
        sibling = (x, y, 1 - c)
        copies = []
        for p in range(n):
            cp = pltpu.make_async_remote_copy(src_ref=ins[p], dst_ref=outs[p], send_sem=send_sems.at[p],
                                              recv_sem=recv_sems.at[p], device_id=sibling, device_id_type=MESH)
            cp.start()
            copies.append(cp)
        for cp in copies:
            cp.wait()

    return pl.pallas_call(
        body, name="reduce_share_halves",
        in_specs=[_ANY] * n, out_specs=[_ANY] * n,
        out_shape=[jax.ShapeDtypeStruct(h.shape, h.dtype) for h in halves],
        scratch_shapes=[pltpu.SemaphoreType.DMA((n,)), pltpu.SemaphoreType.DMA((n,))],
    )(*halves)


def _row_tile(half, cols):
    th = max(SLAB, min(half, (1 << 18) // cols // SLAB * SLAB))
    while half % th:
        th -= SLAB
    return th


N_DEV = 8
SMALL_ROWS = 208


def _allreduce_small(pack):
    def body(x_ref, out_ref, gbuf, send_sems, recv_sems, local_sem):
        x, y, c, chips = _place()
        me, sibling = (x, y, c), (x, y, 1 - c)

        def rows(px, py, pc):
            return gbuf.at[4 * px + 2 * py + pc]

        def copy(k, block, to, src=None):
            return pltpu.make_async_remote_copy(
                src_ref=rows(*block) if src is None else src, dst_ref=rows(*block),
                send_sem=send_sems.at[k], recv_sem=recv_sems.at[k], device_id=to, device_id_type=MESH)

        mine = pltpu.make_async_copy(x_ref, rows(*me), local_sem)
        mine.start()
        first = [copy(0, me, sibling, src=x_ref)]
        first += [copy(1 + j, me, (chip[0], chip[1], c), src=x_ref) for j, chip in enumerate(chips)]
        for cp in first:
            cp.start()
        passed = [copy(4 + j, (chip[0], chip[1], c), sibling) for j, chip in enumerate(chips)]
        for j, chip in enumerate(chips):
            copy(1 + j, (chip[0], chip[1], c), me).wait_recv()
            passed[j].start()
        copy(0, sibling, me).wait_recv()
        for j, chip in enumerate(chips):
            copy(4 + j, (chip[0], chip[1], 1 - c), me).wait_recv()
        for cp in first + passed:
            cp.wait_send()
        mine.wait()
        acc = gbuf[0]
        for d in range(1, N_DEV):
            acc = acc + gbuf[d]
        out_ref[...] = acc

    return pl.pallas_call(
        body, name="allreduce_small",
        in_specs=[pl.BlockSpec(memory_space=pltpu.VMEM)],
        out_specs=pl.BlockSpec(memory_space=pltpu.VMEM),
        out_shape=jax.ShapeDtypeStruct((SMALL_ROWS, D), F32),
        scratch_shapes=[pltpu.VMEM((N_DEV, SMALL_ROWS, D), F32), pltpu.SemaphoreType.DMA((7,)),
                        pltpu.SemaphoreType.DMA((7,)), pltpu.SemaphoreType.DMA],
    )(pack)


def _adamw(w, g, m, v, name):
    rows, cols = w.shape

    def body(w_ref, g_ref, m_ref, v_ref, d_ref, mo_ref, vo_ref):
        gv = g_ref[...]
        mn = ADAM_B1 * m_ref[...] + (1.0 - ADAM_B1) * gv
        vn = ADAM_B2 * v_ref[...] + (1.0 - ADAM_B2) * (gv * gv)
        m_hat = mn / (1.0 - ADAM_B1 ** ADAM_STEP)
        v_hat = vn / (1.0 - ADAM_B2 ** ADAM_STEP)
        d_ref[...] = -ADAM_LR * (m_hat / (jnp.sqrt(v_hat) + ADAM_EPS) + ADAM_WD * w_ref[...])
        mo_ref[...] = mn
        vo_ref[...] = vn

    if rows % SLAB:
        spec, steps = pl.BlockSpec((rows, DH), lambda i: (0, i)), cols // DH
    else:
        th = _row_tile(rows, cols)
        spec, steps = pl.BlockSpec((th, cols), lambda i: (i, 0)), rows // th
    return pl.pallas_call(
        body, name=name, grid=(steps,),
        in_specs=[spec] * 4, out_specs=[spec] * 3,
        out_shape=[jax.ShapeDtypeStruct((rows, cols), F32)] * 3,
        compiler_params=pltpu.CompilerParams(dimension_semantics=("parallel",)),
    )(w, g, m, v)


_SMALL = ["norm_mix_g", "norm_mlp_g", "norm_final_g", "conv_b", "lru_ba", "lru_bx", "lru_lambda"]
_ROW_FB, _ROW_CW, _ROW_WA, _ROW_WX, _ROW_LOSS = 56, 64, 72, 136, 200


def _pack_small(vals, col0):
    def slab(a):
        return jnp.pad(a, ((0, -a.shape[0] % SLAB), (0, D - a.shape[1])))

    rows = [slab(vals[n].reshape(1, D)) for n in _SMALL]
    rows.append(slab(vals["forget_b"].reshape(1, NH)))
    if vals["conv_w"].shape[1] == D:
        rows.append(slab(vals["conv_w"]))
    else:
        rows.append(slab(lax.dynamic_update_slice(jnp.zeros((CONV, D), F32), vals["conv_w"], (0, col0))))
    rows.append(vals["lru_wa"].reshape(LRU_BLOCKS * LRU_BW * LRU_BW // D, D))
    rows.append(vals["lru_wx"].reshape(LRU_BLOCKS * LRU_BW * LRU_BW // D, D))
    rows.append(slab(vals["loss"]) if "loss" in vals else jnp.zeros((SLAB, D), F32))
    return jnp.concatenate(rows, axis=0)


def _unpack_small(pack, col0):
    out = {n: pack[SLAB * i] for i, n in enumerate(_SMALL)}
    out["forget_b"] = pack[_ROW_FB, :NH]
    out["conv_w"] = lax.dynamic_slice(pack[_ROW_CW:_ROW_CW + CONV], (0, col0), (CONV, D // NCHIP))
    out["lru_wa"] = pack[_ROW_WA:_ROW_WX].reshape(LRU_BLOCKS, LRU_BW, LRU_BW)
    out["lru_wx"] = pack[_ROW_WX:_ROW_LOSS].reshape(LRU_BLOCKS, LRU_BW, LRU_BW)
    return out


_WEIGHTS = ["norm_mix_g", "w_in", "conv_w", "conv_b", "lru_wa", "lru_ba", "lru_wx", "lru_bx", "lru_lambda",
            "forget_b", "w_branch_a", "w_branch_b", "w_out", "norm_mlp_g", "w_up", "w_down", "norm_final_g"]
_BIG = ["w_in", "w_branch_a", "w_branch_b", "w_out", "w_up", "w_down"]


def _halves(a):
    return a.reshape(2, a.shape[0] // 2, a.shape[1])


def _columns_to_shards(a):
    rows, cols = a.shape[0], a.shape[1] // NCHIP
    return jnp.transpose(a.reshape(rows, NCHIP, cols), (1, 0, 2))


def _shards_to_columns(a):
    n, rows, cols = a.shape
    return jnp.transpose(a, (1, 0, 2)).reshape(rows, n * cols)


def kernel(x, norm_mix_g, w_in, conv_w, conv_b, lru_wa, lru_ba, lru_wx, lru_bx, lru_lambda, forget_b, w_branch_a, w_branch_b, w_out, norm_mlp_g, w_up, w_down, norm_final_g, loss_target, m_norm_mix_g, m_w_in, m_conv_w, m_conv_b, m_lru_wa, m_lru_ba, m_lru_wx, m_lru_bx, m_lru_lambda, m_forget_b, m_w_branch_a, m_w_branch_b, m_w_out, m_norm_mlp_g, m_w_up, m_w_down, m_norm_final_g, v_norm_mix_g, v_w_in, v_conv_w, v_conv_b, v_lru_wa, v_lru_ba, v_lru_wx, v_lru_bx, v_lru_lambda, v_forget_b, v_w_branch_a, v_w_branch_b, v_w_out, v_norm_mlp_g, v_w_up, v_w_down, v_norm_final_g):
    args = dict(locals())
    wts = {n: args[n] for n in _WEIGHTS}
    mom = {n: args["m_" + n] for n in _WEIGHTS}
    var = {n: args["v_" + n] for n in _WEIGHTS}
    T = x.shape[1]
    xi, yi, ci = lax.axis_index("x"), lax.axis_index("y"), lax.axis_index("c")
    me = 2 * xi + yi
    c1 = jnp.reshape(ci, (1,)).astype(jnp.int32)
    me1 = jnp.reshape(me, (1,)).astype(jnp.int32)
    col0 = me * (D // NCHIP)

    cw_pad = jnp.pad(conv_w, ((0, 4 * SLAB - CONV), (0, 0)))
    g_in, g_cw = _allgather_shards([_halves(w_in.astype(BF)), _halves(cw_pad)])
    cin = DIN // NCHIP
    win = _shards_to_columns(g_in.reshape(NCHIP, D, cin))
    w = dict(
        win=jnp.pad(win, ((0, 0), (0, DINP - DIN))),
        cw=_shards_to_columns(g_cw.reshape(NCHIP, 4 * SLAB, D // NCHIP)[:, :CONV]),
        vec=jnp.concatenate([conv_b[None], lru_ba[None], lru_bx[None], lru_lambda[None],
                             jnp.zeros((SLAB - 4, D), F32)], axis=0),
        fb=jnp.pad(forget_b[None], ((0, 0), (0, DH - NH))),
        wabd=_block_diag(lru_wa).astype(BF), wxbd=_block_diag(lru_wx).astype(BF),
        g_mix=norm_mix_g[None], g_mlp=norm_mlp_g[None], g_fin=norm_final_g[None])

    r = _local_step(x[0], loss_target[0], w, T, dist=_Exchanges({n: wts[n].astype(BF) for n in _LATE}))

    halves = [_add8(*r["big"][n], me1, c1, "add8_" + n) for n in _BIG]
    theirs = _share_halves(halves)
    low = ci == 0
    gsum = {n: jnp.concatenate([jnp.where(low, h, t), jnp.where(low, t, h)], axis=0)
            for n, h, t in zip(_BIG, halves, theirs)}
    lacc = r["lacc"]
    small = dict(norm_mix_g=r["dg_mix"], norm_mlp_g=r["dg_mlp"], norm_final_g=r["dg_fin"], conv_b=lacc[3],
                 lru_ba=lacc[0], lru_bx=lacc[1], lru_lambda=lacc[2], forget_b=r["dfb"][0, :NH],
                 conv_w=lacc[4:4 + CONV], lru_wa=_block_diag_extract(r["dwabd"]),
                 lru_wx=_block_diag_extract(r["dwxbd"]), loss=r["loss_vec"])
    gpack = _allreduce_small(_pack_small(small, col0))
    loss = jnp.sum(gpack[_ROW_LOSS])

    grads, delta, new_m, new_v = {}, {}, {}, {}
    for n in _BIG:
        if n == "w_in":
            gt = gsum[n].T
            grads[n] = gt.T
            delta[n], new_m[n], new_v[n] = [a.T for a in _adamw(wts[n].T, gt, mom[n].T, var[n].T, "adamw_" + n)]
        else:
            grads[n] = gsum[n]
            delta[n], new_m[n], new_v[n] = _adamw(wts[n], gsum[n], mom[n], var[n], "adamw_" + n)
    dp, mp, vp = _adamw(_pack_small(wts, col0), gpack, _pack_small(mom, col0), _pack_small(var, col0), "adamw_small")
    for dst, pack in ((grads, gpack), (delta, dp), (new_m, mp), (new_v, vp)):
        dst.update(_unpack_small(pack, col0))
    return (loss, r["dx"][None], *[grads[n] for n in _WEIGHTS], *[delta[n] for n in _WEIGHTS],
            *[new_m[n] for n in _WEIGHTS], *[new_v[n] for n in _WEIGHTS])
```

```python
import functools
import math

import jax
import jax.numpy as jnp
import numpy as np
from jax import lax
from jax.experimental import pallas as pl
from jax.experimental.pallas import tpu as pltpu

F32 = jnp.float32
BF = jnp.bfloat16

D = 1024
NH = 8
DH = 128
FF = 4096
CONV = 4
LRU_BLOCKS = 16
LRU_BW = 64
BD = 256
NBD = D // BD
LRU_C = 8.0
EPS = 1e-6
DIN = 7176
DINP = 7296
NCHIP = 4
SLAB = 8
VMEM_CAP = 60 * 1024 * 1024

ADAM_LR = 0.001
ADAM_B1 = 0.9
ADAM_B2 = 0.999
ADAM_EPS = 1e-08
ADAM_WD = 0.01
ADAM_STEP = 10

MESH = pl.DeviceIdType.MESH


def _vmem_limit(nbytes):
    return int(min(VMEM_CAP, max(32 * 1024 * 1024, 3 * nbytes)))


def _nbytes(shape, dtype):
    return int(np.prod(shape)) * jnp.dtype(dtype).itemsize


def _sig(x):
    return 0.5 * jnp.tanh(0.5 * x) + 0.5


def _log1p(u):
    w = 1.0 + u
    return jnp.where(w == 1.0, u, jnp.log(w) * (u / (w - 1.0)))


def _one_minus_sq(a, la):
    return jnp.tanh(-la) * (1.0 + a * a)


def _softplus(z):
    return jnp.maximum(z, 0.0) + _log1p(jnp.exp(-jnp.abs(z)))


_GELU_C = math.sqrt(2.0 / math.pi)


def _gelu(x):
    return 0.5 * x * (1.0 + jnp.tanh(_GELU_C * (x + 0.044715 * x * x * x)))


def _gelu_grad(x):
    t = jnp.tanh(_GELU_C * (x + 0.044715 * x * x * x))
    return 0.5 * (1.0 + t) + 0.5 * x * (1.0 - t * t) * _GELU_C * (1.0 + 3.0 * 0.044715 * x * x)


def _slab_scan_fwd(a, b):
    row = lax.broadcasted_iota(jnp.int32, a.shape, 0)
    for k in (1, 2, 4):
        a_s = pltpu.roll(a, k, 0)
        b_s = pltpu.roll(b, k, 0)
        m = row >= k
        b = jnp.where(m, a * b_s + b, b)
        a = jnp.where(m, a * a_s, a)
    return a, b


def _slab_scan_bwd(a, b):
    row = lax.broadcasted_iota(jnp.int32, a.shape, 0)
    for k in (1, 2, 4):
        a_s = pltpu.roll(a, SLAB - k, 0)
        b_s = pltpu.roll(b, SLAB - k, 0)
        m = row < SLAB - k
        b = jnp.where(m, a * b_s + b, b)
        a = jnp.where(m, a * a_s, a)
    return a, b


_DIMS = {"nn": (((1,), (0,)), ((), ())), "nt": (((1,), (1,)), ((), ())), "tn": (((0,), (0,)), ((), ()))}


def _dot(a, b, mode="nn"):
    return lax.dot_general(a, b, _DIMS[mode], preferred_element_type=F32)


def _mm(a, b, mode, M, N, K, *, name, out_dtype=F32, tm=None, tn=1024, tk=1024,
        a_off=(0, 0), b_off=(0, 0), add=None, epi=None, epi_ins=()):
    if tm is None:
        tm = 2048 if (K <= tk and out_dtype == BF and add is None) else 1024
    tm, tn, tk = min(tm, M), min(tn, N), min(tk, K)
    nk = K // tk
    grid = (M // tm, N // tn, nk)
    if mode == "nn":
        a_spec = pl.BlockSpec((tm, tk), lambda i, j, k: (i + a_off[0], k + a_off[1]))
        b_spec = pl.BlockSpec((tk, tn), lambda i, j, k: (k + b_off[0], j + b_off[1]))
    elif mode == "nt":
        a_spec = pl.BlockSpec((tm, tk), lambda i, j, k: (i + a_off[0], k + a_off[1]))
        b_spec = pl.BlockSpec((tn, tk), lambda i, j, k: (j + b_off[0], k + b_off[1]))
    else:
        a_spec = pl.BlockSpec((tk, tm), lambda i, j, k: (k + a_off[0], i + a_off[1]))
        b_spec = pl.BlockSpec((tk, tn), lambda i, j, k: (k + b_off[0], j + b_off[1]))
    o_spec = pl.BlockSpec((tm, tn), lambda i, j, k: (i, j))
    extra = ([add] if add is not None else []) + list(epi_ins)
    n_extra = len(extra)
    has_add = add is not None

    def body(*refs):
        a_ref, b_ref = refs[0], refs[1]
        ex = refs[2:2 + n_extra]
        o_ref = refs[2 + n_extra]

        def finish(acc):
            if has_add:
                acc = acc + ex[0][...].astype(F32)
            if epi is not None:
                acc = epi(acc, *[e[...] for e in ex[(1 if has_add else 0):]])
            o_ref[...] = acc.astype(o_ref.dtype)

        p = _dot(a_ref[...].astype(BF), b_ref[...].astype(BF), mode)
        if nk == 1:
            finish(p)
        else:
            acc_ref = refs[3 + n_extra]
            k = pl.program_id(2)

            @pl.when(k == 0)
            def _():
                acc_ref[...] = p

            @pl.when(k > 0)
            def _():
                acc_ref[...] += p

            @pl.when(k == nk - 1)
            def _():
                finish(acc_ref[...])

    blk = (_nbytes((tm, tk), a.dtype) + _nbytes((tk, tn), b.dtype) + _nbytes((tm, tn), out_dtype)
           + sum(_nbytes((tm, tn), e.dtype) for e in extra) + 2 * _nbytes((tm, tn), F32))
    return pl.pallas_call(
        body, name=name, grid=grid,
        in_specs=[a_spec, b_spec] + [o_spec] * n_extra,
        out_specs=o_spec,
        out_shape=jax.ShapeDtypeStruct((M, N), out_dtype),
        scratch_shapes=[pltpu.VMEM((tm, tn), F32)] if nk > 1 else [],
        compiler_params=pltpu.CompilerParams(
            dimension_semantics=("parallel", "parallel", "arbitrary"), vmem_limit_bytes=_vmem_limit(blk)),
    )(a, b, *extra)


def _ew(fn, T, tm, ins, consts, outs, accs, *, name, reverse=False):
    tm = min(tm, T)
    nt = T // tm
    n_in, n_c, n_o, n_a = len(ins), len(consts), len(outs), len(accs)

    def row(i):
        return nt - 1 - i if reverse else i

    in_specs = [pl.BlockSpec((tm, w), functools.partial(lambda i, cb: (row(i), cb), cb=cb)) for (_, w, cb) in ins]
    in_specs += [pl.BlockSpec(c.shape, functools.partial(lambda i, nd: (0,) * nd, nd=c.ndim)) for c in consts]
    out_specs = [pl.BlockSpec((tm, w), lambda i: (row(i), 0)) for (w, _) in outs]
    out_specs += [pl.BlockSpec((r, w), lambda i: (0, 0)) for (r, w) in accs]
    out_shape = [jax.ShapeDtypeStruct((T, w), dt) for (w, dt) in outs]
    out_shape += [jax.ShapeDtypeStruct((r, w), F32) for (r, w) in accs]

    def body(*refs):
        in_refs = refs[:n_in]
        c_refs = refs[n_in:n_in + n_c]
        o_refs = refs[n_in + n_c:n_in + n_c + n_o]
        a_refs = refs[n_in + n_c + n_o:]
        ov, av = fn([r[...] for r in in_refs], [r[...] for r in c_refs])
        for r, v in zip(o_refs, ov):
            r[...] = v.astype(r.dtype)
        if n_a:
            i = pl.program_id(0)

            @pl.when(i == 0)
            def _():
                for r, v in zip(a_refs, av):
                    r[...] = v

            @pl.when(i > 0)
            def _():
                for r, v in zip(a_refs, av):
                    r[...] += v

    blk = (sum(_nbytes((tm, w), a.dtype) for (a, w, _) in ins) + sum(_nbytes(c.shape, c.dtype) for c in consts)
           + sum(_nbytes((tm, w), dt) for (w, dt) in outs) + sum(_nbytes(s, F32) for s in accs))
    res = pl.pallas_call(
        body, name=name, grid=(nt,), in_specs=in_specs, out_specs=out_specs, out_shape=out_shape,
        compiler_params=pltpu.CompilerParams(
            dimension_semantics=("arbitrary",), vmem_limit_bytes=_vmem_limit(blk)),
    )(*[a for (a, _, _) in ins], *consts)
    return res


def _colsum(v):
    return jnp.sum(v, axis=0, keepdims=True)


FGATE_GROUP = 4


def _fgate_fwd(fl, fb, T, tm=512):
    tm = min(tm, T)

    def body(fl_ref, fb_ref, f_ref, carry_ref):
        i = pl.program_id(0)

        @pl.when(i == 0)
        def _():
            carry_ref[...] = jnp.zeros_like(carry_ref)

        rows = FGATE_GROUP * SLAB
        sub = lax.broadcasted_iota(jnp.int32, (rows, DH), 0) % SLAB

        def group(s, carry):
            r0 = pl.multiple_of(s * rows, rows)
            z = fl_ref[pl.ds(r0, rows), :] + fb_ref[...]
            c = jnp.minimum(z, 0.0) - _log1p(jnp.exp(-jnp.abs(z)))
            for k in (1, 2, 4):
                c = c + jnp.where(sub >= k, pltpu.roll(c, k, 0), 0.0)
            for u in range(FGATE_GROUP):
                cu = c[u * SLAB:(u + 1) * SLAB] + carry
                f_ref[pl.ds(r0 + u * SLAB, SLAB), :] = cu
                carry = cu[SLAB - 1:SLAB, :]
            return carry

        carry_ref[0:1, :] = lax.fori_loop(0, tm // rows, group, carry_ref[0:1, :])

    return pl.pallas_call(
        body, name="fgate_fwd", grid=(T // tm,),
        in_specs=[pl.BlockSpec((tm, DH), lambda i: (i, 0)), pl.BlockSpec((1, DH), lambda i: (0, 0))],
        out_specs=pl.BlockSpec((tm, DH), lambda i: (i, 0)),
        out_shape=jax.ShapeDtypeStruct((T, DH), F32),
        scratch_shapes=[pltpu.VMEM((SLAB, DH), F32)],
        compiler_params=pltpu.CompilerParams(dimension_semantics=("arbitrary",)),
    )(fl, fb)


def _fgate_bwd(dF, fl, fb, T, tm=512):
    tm = min(tm, T)
    nt = T // tm

    def body(df_ref, fl_ref, fb_ref, o_ref, acc_ref, carry_ref):
        i = pl.program_id(0)

        @pl.when(i == 0)
        def _():
            carry_ref[...] = jnp.zeros_like(carry_ref)
            acc_ref[...] = jnp.zeros_like(acc_ref)

        rows = FGATE_GROUP * SLAB
        sub = lax.broadcasted_iota(jnp.int32, (rows, DH), 0) % SLAB

        def group(n, carry):
            g_next, acc = carry
            r0 = pl.multiple_of((tm // rows - 1 - n) * rows, rows)
            c = df_ref[pl.ds(r0, rows), :]
            for k in (1, 2, 4):
                c = c + jnp.where(sub < SLAB - k, pltpu.roll(c, rows - k, 0), 0.0)
            sg = _sig(-(fl_ref[pl.ds(r0, rows), :] + fb_ref[...]))
            for u in reversed(range(FGATE_GROUP)):
                cu = c[u * SLAB:(u + 1) * SLAB] + g_next
                dfl = cu * sg[u * SLAB:(u + 1) * SLAB]
                o_ref[pl.ds(r0 + u * SLAB, SLAB), :] = dfl.astype(o_ref.dtype)
                acc = acc + _colsum(dfl)
                g_next = cu[0:1, :]
            return g_next, acc

        g, acc = lax.fori_loop(0, tm // rows, group, (carry_ref[0:1, :], jnp.zeros((1, DH), F32)))
        carry_ref[0:1, :] = g
        acc_ref[...] += acc

    return pl.pallas_call(
        body, name="fgate_bwd", grid=(nt,),
        in_specs=[pl.BlockSpec((tm, DH), lambda i: (nt - 1 - i, 0)), pl.BlockSpec((tm, DH), lambda i: (nt - 1 - i, 0)),
                  pl.BlockSpec((1, DH), lambda i: (0, 0))],
        out_specs=[pl.BlockSpec((tm, DH), lambda i: (nt - 1 - i, 0)), pl.BlockSpec((1, DH), lambda i: (0, 0))],
        out_shape=[jax.ShapeDtypeStruct((T, DH), BF), jax.ShapeDtypeStruct((1, DH), F32)],
        scratch_shapes=[pltpu.VMEM((SLAB, DH), F32)],
        compiler_params=pltpu.CompilerParams(dimension_semantics=("arbitrary",)),
    )(dF, fl, fb)


def _shift_down(x, d, prev8):
    n = x.shape[0]
    row8 = lax.broadcasted_iota(jnp.int32, (SLAB, x.shape[1]), 0)
    y = pltpu.roll(x, d, 0)
    top = jnp.where(row8 < d, pltpu.roll(prev8, d, 0), y[0:SLAB])
    return jnp.concatenate([top, y[SLAB:]], axis=0)


def _shift_up(x, d, next8):
    n = x.shape[0]
    row8 = lax.broadcasted_iota(jnp.int32, (SLAB, x.shape[1]), 0)
    y = pltpu.roll(x, n - d, 0)
    bottom = jnp.where(row8 >= SLAB - d, pltpu.roll(next8, SLAB - d, 0), y[n - SLAB:])
    return jnp.concatenate([y[:n - SLAB], bottom], axis=0)


def _conv(x, prev8, cw, cb):
    xs = [x] + [_shift_down(x, d, prev8) for d in (1, 2, 3)]
    xa = cb + cw[3:4, :] * xs[0] + cw[2:3, :] * xs[1] + cw[1:2, :] * xs[2] + cw[0:1, :] * xs[3]
    return xa, xs


def _lru_gates(xa_g, wa_g, wx_g, ba_g, bx_g, sp_g):
    xb = xa_g.astype(BF)
    r = _sig(_dot(xb, wa_g) + ba_g)
    ig = _sig(_dot(xb, wx_g) + bx_g)
    la = -LRU_C * r * sp_g
    a = jnp.exp(la)
    mult = jnp.sqrt(_one_minus_sq(a, la))
    return r, ig, a, mult


def _lru_fwd(xg, cw, vec, wabd, wxbd, T, tm=256):
    tm = min(tm, T)
    nsl = tm // SLAB

    def body(x_ref, xp_ref, cw_ref, vec_ref, wa_ref, wx_ref, h_ref, a_s, b_s, carry_ref):
        i = pl.program_id(0)

        @pl.when(i == 0)
        def _():
            carry_ref[...] = jnp.zeros_like(carry_ref)

        x = x_ref[...].astype(F32)
        prev8 = jnp.where(i > 0, xp_ref[SLAB:, :].astype(F32), 0.0)
        vec_v = vec_ref[...]
        xa, _ = _conv(x, prev8, cw_ref[...], vec_v[0:1, :])
        sp = _softplus(-vec_v[3:4, :])
        for g in range(NBD):
            sl = slice(g * BD, (g + 1) * BD)
            _, ig, a, mult = _lru_gates(xa[:, sl], wa_ref[g], wx_ref[g], vec_v[1:2, sl], vec_v[2:3, sl], sp[:, sl])
            a_s[:, sl] = a
            b_s[:, sl] = mult * ig * xa[:, sl]

        def slab(s, carry):
            r0 = pl.multiple_of(s * SLAB, SLAB)
            A, B = _slab_scan_fwd(a_s[pl.ds(r0, SLAB), :], b_s[pl.ds(r0, SLAB), :])
            h = A * carry + B
            h_ref[pl.ds(r0, SLAB), :] = h
            return h[SLAB - 1:SLAB, :]

        carry_ref[0:1, :] = lax.fori_loop(0, nsl, slab, carry_ref[0:1, :])

    blk = 5 * _nbytes((tm, D), F32) + 2 * _nbytes((NBD, BD, BD), BF)
    return pl.pallas_call(
        body, name="lru_fwd", grid=(T // tm,),
        in_specs=[pl.BlockSpec((tm, D), lambda i: (i, 0)),
                  pl.BlockSpec((2 * SLAB, D), lambda i: (jnp.maximum(i * (nsl // 2) - 1, 0), 0)),
                  pl.BlockSpec((CONV, D), lambda i: (0, 0)),
                  pl.BlockSpec((SLAB, D), lambda i: (0, 0)),
                  pl.BlockSpec((NBD, BD, BD), lambda i: (0, 0, 0)),
                  pl.BlockSpec((NBD, BD, BD), lambda i: (0, 0, 0))],
        out_specs=pl.BlockSpec((tm, D), lambda i: (i, 0)),
        out_shape=jax.ShapeDtypeStruct((T, D), F32),
        scratch_shapes=[pltpu.VMEM((tm, D), F32), pltpu.VMEM((tm, D), F32), pltpu.VMEM((SLAB, D), F32)],
        compiler_params=pltpu.CompilerParams(dimension_semantics=("arbitrary",), vmem_limit_bytes=_vmem_limit(blk)),
    )(xg, xg, cw, vec, wabd, wxbd)


def _lru_bwd(xg, h, dha, cw, vec, wabd, wxbd, T, tm=256):
    tm = min(tm, T)
    nsl = tm // SLAB
    nt = T // tm

    def body(x_ref, xp_ref, h_ref, hp_ref, dh_ref, cw_ref, vec_ref, wa_ref, wx_ref,
             dx_ref, dwa_ref, dwx_ref, acc_ref, a_s, b_s, g_s, dxa_s, carry_ref, dxan_ref, r_s, i_s, m_s):
        n = pl.program_id(0)
        it = nt - 1 - n

        @pl.when(n == 0)
        def _():
            carry_ref[...] = jnp.zeros_like(carry_ref)
            dxan_ref[...] = jnp.zeros_like(dxan_ref)
            dwa_ref[...] = jnp.zeros_like(dwa_ref)
            dwx_ref[...] = jnp.zeros_like(dwx_ref)
            acc_ref[...] = jnp.zeros_like(acc_ref)

        x = x_ref[...].astype(F32)
        prev8 = jnp.where(it > 0, xp_ref[SLAB:, :].astype(F32), 0.0)
        hprev8 = jnp.where(it > 0, hp_ref[...], 0.0)
        vec_v = vec_ref[...]
        cw_v = cw_ref[...]
        xa, xs = _conv(x, prev8, cw_v, vec_v[0:1, :])
        sp = _softplus(-vec_v[3:4, :])
        for g in range(NBD):
            sl = slice(g * BD, (g + 1) * BD)
            r, ig, a, mult = _lru_gates(xa[:, sl], wa_ref[g], wx_ref[g], vec_v[1:2, sl], vec_v[2:3, sl], sp[:, sl])
            r_s[:, sl] = r
            i_s[:, sl] = ig
            m_s[:, sl] = mult
            a_s[:, sl] = a
        b_s[...] = _shift_up(a_s[...], 1, carry_ref[...])

        def slab(m, carry):
            r0 = pl.multiple_of((nsl - 1 - m) * SLAB, SLAB)
            A, B = _slab_scan_bwd(b_s[pl.ds(r0, SLAB), :], dh_ref[pl.ds(r0, SLAB), :])
            gg = A * carry + B
            g_s[pl.ds(r0, SLAB), :] = gg
            return gg[0:1, :]

        g_first = lax.fori_loop(0, nsl, slab, carry_ref[1:2, :])
        gt = g_s[...]
        h_prev = _shift_down(h_ref[...], 1, hprev8)
        dba = []
        dbx = []
        dsp = []
        for g in range(NBD):
            sl = slice(g * BD, (g + 1) * BD)
            r, ig, a, mult = r_s[:, sl], i_s[:, sl], a_s[:, sl], m_s[:, sl]
            xa_g = xa[:, sl]
            g_g = gt[:, sl]
            da = g_g * h_prev[:, sl]
            dmult = g_g * ig * xa_g
            di = g_g * mult * xa_g
            dxa_g = g_g * mult * ig
            dla = da * a - dmult * (a * a / mult)
            dr = dla * (-LRU_C) * sp[:, sl]
            dsp.append(_colsum(dla * (-LRU_C) * r))
            dra = (dr * r * (1.0 - r))
            dix = (di * ig * (1.0 - ig))
            dba.append(_colsum(dra))
            dbx.append(_colsum(dix))
            dra_b = dra.astype(BF)
            dix_b = dix.astype(BF)
            xb = xa_g.astype(BF)
            dxa_g = dxa_g + _dot(dra_b, wa_ref[g], "nt") + _dot(dix_b, wx_ref[g], "nt")
            dwa_ref[g] += _dot(xb, dra_b, "tn")
            dwx_ref[g] += _dot(xb, dix_b, "tn")
            dxa_s[:, sl] = dxa_g
        dxa = dxa_s[...]
        nxt = dxan_ref[...]
        dx = (cw_v[3:4, :] * dxa + cw_v[2:3, :] * _shift_up(dxa, 1, nxt)
              + cw_v[1:2, :] * _shift_up(dxa, 2, nxt) + cw_v[0:1, :] * _shift_up(dxa, 3, nxt))
        dx_ref[...] = dx.astype(dx_ref.dtype)
        acc_ref[0:1, :] += jnp.concatenate(dba, axis=1)
        acc_ref[1:2, :] += jnp.concatenate(dbx, axis=1)
        acc_ref[2:3, :] += jnp.concatenate(dsp, axis=1)
        acc_ref[3:4, :] += _colsum(dxa)
        for k in range(CONV):
            acc_ref[4 + k:5 + k, :] += _colsum(dxa * xs[CONV - 1 - k])
        dxan_ref[...] = dxa[0:SLAB, :]
        carry_ref[0:1, :] = a_s[0:1, :]
        carry_ref[1:2, :] = g_first

        @pl.when(n == nt - 1)
        def _():
            acc_ref[2:3, :] = acc_ref[2:3, :] * (-_sig(-vec_v[3:4, :]))

    rowblk = lambda i: (nt - 1 - i, 0)
    prevblk = lambda i: (jnp.maximum((nt - 1 - i) * nsl - 1, 0), 0)
    c2 = lambda i: (0, 0)
    c3 = lambda i: (0, 0, 0)
    blk = 12 * _nbytes((tm, D), F32) + 6 * _nbytes((NBD, BD, BD), F32)
    return pl.pallas_call(
        body, name="lru_bwd", grid=(nt,),
        in_specs=[pl.BlockSpec((tm, D), rowblk),
                  pl.BlockSpec((2 * SLAB, D), lambda i: (jnp.maximum((nt - 1 - i) * (nsl // 2) - 1, 0), 0)),
                  pl.BlockSpec((tm, D), rowblk), pl.BlockSpec((SLAB, D), prevblk),
                  pl.BlockSpec((tm, D), rowblk),
                  pl.BlockSpec((CONV, D), c2), pl.BlockSpec((SLAB, D), c2),
                  pl.BlockSpec((NBD, BD, BD), c3), pl.BlockSpec((NBD, BD, BD), c3)],
        out_specs=[pl.BlockSpec((tm, D), rowblk), pl.BlockSpec((NBD, BD, BD), c3), pl.BlockSpec((NBD, BD, BD), c3),
                   pl.BlockSpec((16, D), c2)],
        out_shape=[jax.ShapeDtypeStruct((T, D), BF), jax.ShapeDtypeStruct((NBD, BD, BD), F32),
                   jax.ShapeDtypeStruct((NBD, BD, BD), F32), jax.ShapeDtypeStruct((16, D), F32)],
        scratch_shapes=[pltpu.VMEM((tm, D), F32), pltpu.VMEM((tm, D), F32), pltpu.VMEM((tm, D), F32),
                        pltpu.VMEM((tm, D), F32), pltpu.VMEM((SLAB, D), F32), pltpu.VMEM((SLAB, D), F32),
                        pltpu.VMEM((tm, D), F32), pltpu.VMEM((tm, D), F32), pltpu.VMEM((tm, D), F32)],
        compiler_params=pltpu.CompilerParams(dimension_semantics=("arbitrary",), vmem_limit_bytes=_vmem_limit(blk)),
    )(xg, xg, h, h, dha, cw, vec, wabd, wxbd)


_SCALE = 1.0 / math.sqrt(DH)


_ANY = pl.BlockSpec(memory_space=pl.ANY)


class _Side:
    def __init__(self, srcs, outs, nsem, copies):
        self.srcs, self.outs, self.nsem, self.copies = list(srcs), list(outs), nsem, copies


def _pallas(body, operands, *, name, grid, in_specs, out_specs, out_shape, scratch_shapes=(), semantics,
            vmem=None, side=None):
    if side is None:
        return pl.pallas_call(
            body, name=name, grid=grid, in_specs=in_specs, out_specs=out_specs, out_shape=out_shape,
            scratch_shapes=list(scratch_shapes),
            compiler_params=pltpu.CompilerParams(dimension_semantics=semantics, vmem_limit_bytes=vmem),
        )(*operands)
    n_in, n_out, n_scr = len(in_specs), len(out_specs), len(scratch_shapes)
    ns, no = len(side.srcs), len(side.outs)

    def hosted(*refs):
        ins, refs = refs[:n_in], refs[n_in:]
        sin, refs = refs[:ns], refs[ns:]
        outs, refs = refs[:n_out], refs[n_out:]
        sout, refs = refs[:no], refs[no:]
        scr, (send, recv) = refs[:n_scr], refs[n_scr:]
        ids = [pl.program_id(a) for a in range(len(grid))]
        first = functools.reduce(jnp.logical_and, [i == 0 for i in ids])
        last = functools.reduce(jnp.logical_and, [i == g - 1 for i, g in zip(ids, grid)])

        @pl.when(first)
        def _():
            for cp in side.copies(sin, sout, send, recv):
                cp.start()

        body(*ins, *outs, *scr)

        @pl.when(last)
        def _():
            for cp in side.copies(sin, sout, send, recv):
                cp.wait()

    return pl.pallas_call(
        hosted, name=name, grid=grid, in_specs=list(in_specs) + [_ANY] * ns, out_specs=list(out_specs) + [_ANY] * no,
        out_shape=list(out_shape) + side.outs,
        scratch_shapes=list(scratch_shapes) + [pltpu.SemaphoreType.DMA((side.nsem,)), pltpu.SemaphoreType.DMA((side.nsem,))],
        compiler_params=pltpu.CompilerParams(dimension_semantics=("arbitrary",) * len(grid), vmem_limit_bytes=vmem),
    )(*operands, *side.srcs)


DA = 2 * DH
_LOG2E = math.log2(math.e)
_C2 = _SCALE * _LOG2E


def _aug_fn(ins, cs):
    q, k, fcum = ins
    g_all = fcum * _LOG2E
    lane = lax.broadcasted_iota(jnp.int32, (q.shape[0], DH), 1)
    qa, ka = [], []
    for hd in range(NH):
        g = g_all[:, hd:hd + 1]
        hi = g.astype(BF).astype(F32)
        mid = (g - hi).astype(BF).astype(F32)
        lo = ((g - hi) - mid).astype(BF).astype(F32)
        qx = jnp.where(lane == 0, hi, jnp.where(lane == 1, mid, jnp.where(lane == 2, lo,
                                                                          jnp.where(lane < 6, 1.0, 0.0))))
        kx = jnp.where(lane < 3, 1.0, jnp.where(lane == 3, -hi, jnp.where(lane == 4, -mid,
                                                                          jnp.where(lane == 5, -lo, 0.0))))
        qa += [q[:, hd * DH:(hd + 1) * DH], qx.astype(BF)]
        ka += [k[:, hd * DH:(hd + 1) * DH], kx.astype(BF)]
    return [jnp.concatenate(qa, axis=1), jnp.concatenate(ka, axis=1)], []


_KA_ONES = DH + 3
KT_ONES = 16
FWD_KEY_PARTS = 2
BWD_QUERY_PARTS = 2


def _attn_fwd(qa, ka, qkv, T, blk=512, side=None):
    blk = min(blk, T)
    nb = T // blk
    half = blk // 2

    def body(q_ref, k_ref, vn_ref, o_ref, lse_ref, v_ref):
        i = pl.program_id(1)

        @pl.when(i == 0)
        def _():
            for jj in range(nb):
                v_ref[jj] = jnp.concatenate([vn_ref[jj * blk:(jj + 1) * blk, :].astype(F32).T.astype(BF),
                                             jnp.ones((KT_ONES, blk), BF)], axis=0)

        q = q_ref[...]

        def scores(j):
            r0 = pl.multiple_of(j * blk, blk)
            return _dot(k_ref[pl.ds(r0, blk), :], q, "nt")

        def update(s, vj, carry):
            m, acc = carry
            m_new = jnp.maximum(m, jnp.max(s, axis=0, keepdims=True))
            alpha = jnp.exp2(m - m_new)
            acc = alpha * acc + _dot(vj, jnp.exp2(s - m_new).astype(BF))
            return m_new, acc

        def step(j, st):
            r0 = pl.multiple_of(j * blk, blk)
            part = blk // FWD_KEY_PARTS
            ss = [_dot(k_ref[pl.ds(r0 + n * part, part), :], q, "nt") for n in range(FWD_KEY_PARTS)]
            vj = v_ref[j]
            for n in range(FWD_KEY_PARTS):
                st = update(ss[n], vj[:, n * part:(n + 1) * part], st)
            return st

        init = (jnp.full((1, blk), -jnp.inf, F32), jnp.zeros((DH + KT_ONES, blk), F32))
        carry = lax.fori_loop(0, i, step, init)
        last = scores(i)
        vi = v_ref[i]
        rk = lax.broadcasted_iota(jnp.int32, (half, blk), 0)
        cq = lax.broadcasted_iota(jnp.int32, (half, blk), 1)
        m, acc = update(jnp.where(cq >= rk, last[:half], -jnp.inf), vi[:, :half], carry)
        s2 = jnp.where(cq[:, :half] >= rk[:, :half], last[half:, half:], -jnp.inf)
        m2, acc2 = update(s2, vi[:, half:], (m[:, half:], acc[:, half:]))
        m = jnp.concatenate([m[:, :half], m2], axis=1)
        acc = jnp.concatenate([acc[:, :half], acc2], axis=1)
        l = acc[DH:DH + 1]
        o_ref[...] = (acc[:DH] / l).T.astype(o_ref.dtype)
        lse_ref[...] = m + jnp.log(l) * _LOG2E

    vm = _nbytes((T, DA), BF) + 2 * _nbytes((T, DH), BF) + 6 * _nbytes((blk, blk), F32)
    return _pallas(
        body, (qa, ka, qkv), name="attn_fwd", grid=(NH, nb),
        in_specs=[pl.BlockSpec((blk, DA), lambda h, i: (i, h)),
                  pl.BlockSpec((T, DA), lambda h, i: (0, h)),
                  pl.BlockSpec((T, DH), lambda h, i: (0, 2 * NH + h))],
        out_specs=[pl.BlockSpec((blk, DH), lambda h, i: (i, h)),
                   pl.BlockSpec((None, None, 1, blk), lambda h, i: (h, i, 0, 0))],
        out_shape=[jax.ShapeDtypeStruct((T, D), BF), jax.ShapeDtypeStruct((NH, nb, 1, blk), F32)],
        scratch_shapes=[pltpu.VMEM((nb, DH + KT_ONES, blk), BF)],
        semantics=("parallel", "arbitrary"), vmem=_vmem_limit(vm), side=side)


def _attn_bwd(qa, ka, qkv, do, lrow, drow, T, blk=512, side=None):
    blk = min(blk, T)
    nb = T // blk

    def body(ka_ref, v_ref, qa_ref, do_ref, l_ref, d_ref, dq_ref, dk_ref, dv_ref, dfs_ref, dft_ref, dq_s,
             dka_s, dv_s):
        j = pl.program_id(1)

        @pl.when(j == 0)
        def _():
            dq_s[...] = jnp.zeros_like(dq_s)

        row = lax.broadcasted_iota(jnp.int32, (DH + KT_ONES, blk), 0)
        dq_scale = jnp.where(row < DH, _SCALE, 1.0)

        kaj = ka_ref[...]
        ktj = jnp.concatenate([kaj[:, :DH].astype(F32).T.astype(BF), jnp.ones((KT_ONES, blk), BF)], axis=0)
        vj = v_ref[...]

        def step(i, carry):
            r0 = pl.multiple_of(i * blk, blk)
            part = blk // BWD_QUERY_PARTS
            lr, dr = l_ref[i], d_ref[i]
            loaded = []
            for n in range(BWD_QUERY_PARTS):
                qn = qa_ref[pl.ds(r0 + n * part, part), :]
                don = do_ref[pl.ds(r0 + n * part, part), :]
                loaded.append((qn, don, _dot(kaj, qn, "nt"), _dot(vj, don, "nt")))
            dq_parts = []
            for n, (qn, don, raw, dpt) in enumerate(loaded):
                pt = jnp.exp2(raw - lr[:, n * part:(n + 1) * part])
                dsb = (pt * (dpt - dr[:, n * part:(n + 1) * part])).astype(BF)
                dv_s[...] += _dot(pt.astype(BF), don)
                dka_s[...] += _dot(dsb, qn)
                dq_parts.append(_dot(ktj, dsb))
            dq_s[i] += jnp.concatenate(dq_parts, axis=1) * dq_scale
            return carry

        def diagonal():
            half = blk // 2
            r0 = pl.multiple_of(j * blk, blk)
            qi = qa_ref[pl.ds(r0, blk), :]
            doi = do_ref[pl.ds(r0, blk), :]
            lr, dr = l_ref[j], d_ref[j]
            rk = lax.broadcasted_iota(jnp.int32, (half, blk), 0)
            cq = lax.broadcasted_iota(jnp.int32, (half, blk), 1)

            def quarter(ka, v, q, do, l2, d2, keep):
                pt = jnp.exp2(jnp.where(keep, _dot(ka, q, "nt") - l2, -jnp.inf))
                dsb = (pt * (_dot(v, do, "nt") - d2)).astype(BF)
                return _dot(dsb, q), _dot(pt.astype(BF), do), dsb

            dka1, dv1, ds1 = quarter(kaj[:half], vj[:half], qi, doi, lr, dr, cq >= rk)
            dka2, dv2, ds2 = quarter(kaj[half:], vj[half:], qi[half:], doi[half:], lr[:, half:], dr[:, half:],
                                     cq[:, :half] >= rk[:, :half])
            dq2 = jnp.concatenate([jnp.zeros((DH + KT_ONES, half), F32), _dot(ktj[:, half:], ds2)], axis=1)
            dq_s[j] += (_dot(ktj[:, :half], ds1) + dq2) * dq_scale
            dka_s[...] = jnp.concatenate([dka1, dka2], axis=0)
            dv_s[...] = jnp.concatenate([dv1, dv2], axis=0)

        diagonal()
        lax.fori_loop(j + 1, nb, step, 0)
        dka, dv = dka_s[...], dv_s[...]
        dk_ref[...] = (dka[:, :DH] * (1.0 / _LOG2E)).astype(dk_ref.dtype)
        dv_ref[...] = dv.astype(dv_ref.dtype)
        dfs_ref[...] = dka[:, DH:].T[_KA_ONES - DH:_KA_ONES - DH + 1, :]

        @pl.when(j == nb - 1)
        def _():
            for ii in range(nb):
                t = dq_s[ii]
                dq_ref[ii * blk:(ii + 1) * blk, :] = t[:DH].T.astype(dq_ref.dtype)
                dft_ref[ii] = t[DH:DH + 1]

    rowv = pl.BlockSpec((None, nb, 1, blk), lambda h, j: (h, 0, 0, 0))
    vm = (_nbytes((T, DA), BF) + 2 * _nbytes((T, DH), BF) + _nbytes((T, DH + KT_ONES), F32)
          + 8 * _nbytes((blk, blk), F32))
    return _pallas(
        body, (ka, qkv, qa, do, lrow, drow), name="attn_bwd", grid=(NH, nb),
        in_specs=[pl.BlockSpec((blk, DA), lambda h, j: (j, h)),
                  pl.BlockSpec((blk, DH), lambda h, j: (j, 2 * NH + h)),
                  pl.BlockSpec((T, DA), lambda h, j: (0, h)),
                  pl.BlockSpec((T, DH), lambda h, j: (0, h)),
                  rowv, rowv],
        out_specs=[pl.BlockSpec((T, DH), lambda h, j: (0, h)),
                   pl.BlockSpec((blk, DH), lambda h, j: (j, h)),
                   pl.BlockSpec((blk, DH), lambda h, j: (j, h)),
                   pl.BlockSpec((None, None, 1, blk), lambda h, j: (h, j, 0, 0)), rowv],
        out_shape=[jax.ShapeDtypeStruct((T, D), BF), jax.ShapeDtypeStruct((T, D), BF),
                   jax.ShapeDtypeStruct((T, D), BF), jax.ShapeDtypeStruct((NH, nb, 1, blk), F32),
                   jax.ShapeDtypeStruct((NH, nb, 1, blk), F32)],
        scratch_shapes=[pltpu.VMEM((nb, DH + KT_ONES, blk), F32), pltpu.VMEM((blk, DA), F32),
                        pltpu.VMEM((blk, DH), F32)],
        semantics=("parallel", "arbitrary"), vmem=_vmem_limit(vm), side=side)


def _norm_fn(ins, cs):
    x, = ins
    g, = cs
    r = lax.rsqrt(jnp.mean(x * x, axis=-1, keepdims=True) + EPS)
    return [x * r * g], []


def _norm_bwd_fn(ins, cs):
    x, dy, dres = ins
    g, = cs
    r = lax.rsqrt(jnp.mean(x * x, axis=-1, keepdims=True) + EPS)
    xh = x * r
    dxh = dy * g
    dx = dres + r * (dxh - xh * jnp.mean(dxh * xh, axis=-1, keepdims=True))
    return [dx, dx], [_colsum(dy * xh)]


def _final_fn(ins, cs):
    x2, tgt = ins
    g, = cs
    r = lax.rsqrt(jnp.mean(x2 * x2, axis=-1, keepdims=True) + EPS)
    xh = x2 * r
    e = xh * g - tgt
    dy = e * (1.0 / D)
    dxh = dy * g
    dx2 = r * (dxh - xh * jnp.mean(dxh * xh, axis=-1, keepdims=True))
    return [dx2, dx2], [_colsum(0.5 * e * e * (1.0 / D)), _colsum(dy * xh)]


def _z_fn(ins, cs):
    g, h = [v.astype(F32) for v in ins]
    return [_gelu(g) * h], []


def _mix_fn(ins, cs):
    gates, ya, yb = [v.astype(F32) for v in ins]
    return [_sig(gates[:, :D]) * ya + _sig(gates[:, D:]) * yb], []


def _mix_bwd_fn(ins, cs):
    dmix, gates, ya, yb = [v.astype(F32) for v in ins]
    ga = _sig(gates[:, :D])
    gb = _sig(gates[:, D:])
    dgates = jnp.concatenate([dmix * ya * ga * (1.0 - ga), dmix * yb * gb * (1.0 - gb)], axis=1)
    return [dmix * ga, dmix * gb, dgates], []


def _z_bwd_fn(ins, cs):
    dz, g, h = [v.astype(F32) for v in ins]
    return [dz * _gelu(g), dz * h * _gelu_grad(g)], []


def _delta_fn(ins, cs):
    do, o = ins
    p = do.astype(F32) * o.astype(F32)
    lane = lax.broadcasted_iota(jnp.int32, (p.shape[0], DH), 1)
    out = jnp.zeros((p.shape[0], DH), F32)
    for hd in range(NH):
        s = jnp.sum(p[:, hd * DH:(hd + 1) * DH], axis=1, keepdims=True)
        out = jnp.where(lane == hd, s, out)
    return [out], []


def _du_all(pieces, win, T, tm=256, side=None):
    tm = min(tm, T)
    n = len(pieces)

    def body(*refs):
        w_ref, o_ref = refs[n], refs[n + 1]
        acc = None
        for (a, off), a_ref in zip(pieces, refs[:n]):
            d = _dot(a_ref[...].astype(BF), w_ref[:, off:off + a.shape[1]], "nt")
            acc = d if acc is None else acc + d
        o_ref[...] = acc

    vm = (sum(_nbytes((tm, a.shape[1]), a.dtype) for a, _ in pieces) + _nbytes(win.shape, win.dtype)
          + 2 * _nbytes((tm, D), F32))
    return _pallas(
        body, tuple(a for a, _ in pieces) + (win,), name="du_all", grid=(T // tm,),
        in_specs=[pl.BlockSpec((tm, a.shape[1]), lambda i: (i, 0)) for a, _ in pieces]
        + [pl.BlockSpec(win.shape, lambda i: (0, 0))],
        out_specs=[pl.BlockSpec((tm, D), lambda i: (i, 0))],
        out_shape=[jax.ShapeDtypeStruct((T, D), F32)],
        semantics=("arbitrary",), vmem=int(min(VMEM_CAP, 2 * vm + (4 << 20))), side=side)


def _local_step(x, tgt, w, T, blk=1024, dist=None):
    blk = min(blk, T)
    nb = T // blk
    win = w["win"]

    u, = _ew(_norm_fn, T, 1024, [(x, D, 0)], [w["g_mix"]], [(D, BF)], [], name="norm_mix")
    xg = _mm(u, win, "nn", T, 2 * D, D, name="proj_lru", out_dtype=BF)
    qkv = _mm(u, win, "nn", T, 3 * D, D, name="proj_qkv", out_dtype=BF, b_off=(0, 2),
              epi=lambda acc: acc * jnp.where(pl.program_id(1) == 0, _C2, 1.0))
    gates = _mm(u, win, "nn", T, 2 * D, D, name="proj_gates", b_off=(0, 5), out_dtype=BF)
    fl = _mm(u, win, "nn", T, DH, D, name="proj_f", tn=DH, b_off=(0, 7 * D // DH))
    fcum = _fgate_fwd(fl, w["fb"], T)
    qa, ka = _ew(_aug_fn, T, 1024,[(qkv, D, 0), (qkv, D, 1), (fcum, DH, 0)], [],
                 [(NH * DA, BF), (NH * DA, BF)], [], name="attn_augment")

    h = _lru_fwd(xg, w["cw"], w["vec"], w["wabd"], w["wxbd"], T)
    ob, lse, *landed = _attn_fwd(qa, ka, qkv, T, min(2 * blk, T), side=dist.weights_side() if dist else None)
    lse = lse.reshape(NH, nb, 1, blk)
    if dist:
        w = dict(w, **dist.weights_landed(landed))
    z, = _ew(_z_fn, T, 1024, [(xg, D, 1), (h, D, 0)], [], [(D, BF)], [], name="lru_gelu")
    ya = _mm(z, w["wa"], "nn", T, D, D, name="branch_a", out_dtype=BF)
    yb = _mm(ob, w["wb"], "nn", T, D, D, name="branch_b", out_dtype=BF)
    mix, = _ew(_mix_fn, T, 1024,[(gates, 2 * D, 0), (ya, D, 0), (yb, D, 0)], [], [(D, BF)], [], name="mix")
    x1 = _mm(mix, w["wout"], "nn", T, D, D, name="out_proj", add=x)
    m, = _ew(_norm_fn, T, 1024, [(x1, D, 0)], [w["g_mlp"]], [(D, BF)], [], name="norm_mlp")
    hh = _mm(m, w["wup"], "nn", T, FF, D, name="mlp_up", out_dtype=BF,
             epi=lambda acc: jnp.square(jnp.maximum(acc, 0.0)))
    x2 = _mm(hh, w["wdown"], "nn", T, D, FF, name="mlp_down", add=x1, tk=FF)
    dx2, dx2b, loss_vec, dg_fin = _ew(_final_fn, T, 1024,[(x2, D, 0), (tgt, D, 0)], [w["g_fin"]],
                                      [(D, F32), (D, BF)], [(1, D), (1, D)], name="final_norm_loss")

    dhpre = _mm(dx2b, w["wdown"], "nt", T, FF, D, name="mlp_down_bwd", out_dtype=BF,
                epi=lambda acc, h2: acc * (2.0 * jnp.sqrt(h2.astype(F32))), epi_ins=[hh])
    dwdown = _mm(hh, dx2b, "tn", FF, D, T, name="dw_down", out_dtype=BF)
    dwup = _mm(m, dhpre, "tn", D, FF, T, name="dw_up", out_dtype=BF)
    dm = _mm(dhpre, w["wup"], "nt", T, D, FF, name="mlp_up_bwd", tk=FF)
    dx1, dx1b, dg_mlp = _ew(_norm_bwd_fn, T, 512, [(x1, D, 0), (dm, D, 0), (dx2, D, 0)], [w["g_mlp"]],
                            [(D, F32), (D, BF)], [(1, D)], name="norm_mlp_bwd")

    dmix = _mm(dx1b, w["wout"], "nt", T, D, D, name="out_proj_bwd", out_dtype=BF)
    dwout = _mm(mix, dx1b, "tn", D, D, T, name="dw_out", out_dtype=BF)
    dya, dyb, dgates = _ew(_mix_bwd_fn, T, 512, [(dmix, D, 0), (gates, 2 * D, 0), (ya, D, 0), (yb, D, 0)], [],
                           [(D, BF), (D, BF), (2 * D, BF)], [], name="mix_bwd")
    dob = _mm(dyb, w["wb"], "nt", T, D, D, name="branch_b_bwd", out_dtype=BF)
    dwb = _mm(ob, dyb, "tn", D, D, T, name="dw_b", out_dtype=BF)
    dz = _mm(dya, w["wa"], "nt", T, D, D, name="branch_a_bwd", out_dtype=BF)
    dwa = _mm(z, dya, "tn", D, D, T, name="dw_a", out_dtype=BF)
    dha, dglru = _ew(_z_bwd_fn, T, 1024,[(dz, D, 0), (xg, D, 1), (h, D, 0)], [], [(D, F32), (D, BF)], [],
                     name="lru_gelu_bwd")

    delta, = _ew(_delta_fn, T, 1024, [(dob, D, 0), (ob, D, 0)], [], [(DH, F32)], [], name="attn_delta")
    drow = delta[:, :NH].T.reshape(NH, nb, 1, blk)
    big = dict(w_branch_a=dwa, w_branch_b=dwb, w_out=dwout, w_up=dwup, w_down=dwdown)
    side = dist.grads_side(big) if dist else None
    dq, dk, dv, dfs, dft, *landed = _attn_bwd(qa, ka, qkv, dob, lse, drow, T, blk, side=side)
    if dist:
        big = dist.grads_landed(side, landed)
    dfcum = jnp.pad((dft - dfs).reshape(NH, T).T, ((0, 0), (0, DH - NH)))
    dfl, dfb = _fgate_bwd(dfcum, fl, w["fb"], T)

    dxl, dwabd, dwxbd, lacc = _lru_bwd(xg, h, dha, w["cw"], w["vec"], w["wabd"], w["wxbd"], T)

    dproj = ((dxl, 0), (dglru, D), (dq, 2 * D), (dk, 3 * D), (dv, 4 * D), (dgates, 5 * D), (dfl, 7 * D))
    pieces = [_mm(u, p, "tn", D, p.shape[1], T, name="dw_in_%d" % n, out_dtype=BF)
              for n, (p, _) in enumerate(dproj)]
    pieces[-1] = pieces[-1][:, :NH]
    dwin = dict(w_in=jnp.concatenate(pieces, axis=1))
    side = dist.grads_side(dwin) if dist else None
    du, *landed = _du_all(dproj, win, T, side=side)
    big.update(dist.grads_landed(side, landed) if dist else dwin)
    dx, dg_mix = _ew(_norm_bwd_fn, T, 512, [(x, D, 0), (du, D, 0), (dx1, D, 0)], [w["g_mix"]], [(D, F32)],
                     [(1, D)], name="norm_mix_bwd")

    return dict(dx=dx, big=big, dwabd=dwabd, dwxbd=dwxbd, lacc=lacc, dfb=dfb, dg_mix=dg_mix, dg_mlp=dg_mlp,
                dg_fin=dg_fin, loss_vec=loss_vec)


def _block_diag(w):
    per = BD // LRU_BW
    w4 = w.reshape(NBD, per, LRU_BW, LRU_BW)
    on_diagonal = jnp.eye(per, dtype=bool)[None, :, None, :, None]
    return jnp.where(on_diagonal, w4[:, :, :, None, :], 0.0).reshape(NBD, BD, BD)


def _block_diag_extract(wbd):
    per = BD // LRU_BW
    w5 = wbd.reshape(NBD, per, LRU_BW, per, LRU_BW)
    return jnp.stack([w5[:, b, :, b, :] for b in range(per)], axis=1).reshape(LRU_BLOCKS, LRU_BW, LRU_BW)


def _place():
    x, y, c = lax.axis_index("x"), lax.axis_index("y"), lax.axis_index("c")
    chips = [(1 - x, y), (x, 1 - y), (1 - x, 1 - y)]
    return x, y, c, chips


def _allgather_shards(shards):
    n = len(shards)

    def body(*refs):
        ins, outs = refs[:n], refs[n:2 * n]
        send_sems, recv_sems = refs[2 * n:]
        x, y, c, chips = _place()
        me = 2 * x + y
        sibling = (x, y, 1 - c)

        def remote(p, k, src, dst, to):
            return pltpu.make_async_remote_copy(src_ref=src, dst_ref=dst, send_sem=send_sems.at[p, k],
                                                recv_sem=recv_sems.at[p, k], device_id=to, device_id_type=MESH)

        sent = []
        for p in range(n):
            for k, chip in enumerate(chips):
                cp = remote(p, k, ins[p].at[c], outs[p].at[me, c], (chip[0], chip[1], c))
                cp.start()
                sent.append(cp)
        for p in range(n):
            for k, chip in enumerate(chips):
                half = outs[p].at[2 * chip[0] + chip[1], c]
                remote(p, k, half, half, sibling).wait_recv()
                fwd = remote(p, 3 + k, half, half, sibling)
                fwd.start()
                sent.append(fwd)
        for p in range(n):
            for k, chip in enumerate(chips):
                half = outs[p].at[2 * chip[0] + chip[1], 1 - c]
                remote(p, 3 + k, half, half, sibling).wait_recv()
        for cp in sent:
            cp.wait_send()

    gathered = pl.pallas_call(
        body, name="allgather_weights",
        in_specs=[_ANY] * n, out_specs=[_ANY] * n,
        out_shape=[jax.ShapeDtypeStruct((NCHIP,) + s.shape, s.dtype) for s in shards],
        scratch_shapes=[pltpu.SemaphoreType.DMA((n, 6)), pltpu.SemaphoreType.DMA((n, 6))],
    )(*shards)
    me = 2 * lax.axis_index("x") + lax.axis_index("y")
    return [lax.dynamic_update_index_in_dim(g, s, me, 0) for g, s in zip(gathered, shards)]


_LATE = ["w_branch_a", "w_branch_b", "w_out", "w_up", "w_down"]
_COLUMN_CUT = ("w_in", "w_up")
N_PEERS = 7


def _shard_major(name, g):
    s = _columns_to_shards(g) if name in _COLUMN_CUT else g.reshape(NCHIP, g.shape[0] // NCHIP, g.shape[1])
    return s.reshape(NCHIP, 2, s.shape[1] // 2, s.shape[2])


class _Exchanges:
    def __init__(self, shards):
        self.shards = shards

    def weights_side(self):
        srcs = [self.shards[n] for n in _LATE]

        def copies(sin, sout, send, recv):
            x, y, c, chips = _place()
            return [pltpu.make_async_remote_copy(
                src_ref=sin[p], dst_ref=sout[p].at[2 * x + y], send_sem=send.at[3 * p + k], recv_sem=recv.at[3 * p + k],
                device_id=(chip[0], chip[1], c), device_id_type=MESH)
                for p in range(len(sin)) for k, chip in enumerate(chips)]

        return _Side(srcs, [jax.ShapeDtypeStruct((NCHIP,) + s.shape, s.dtype) for s in srcs], 3 * len(srcs), copies)

    def weights_landed(self, landed):
        me = 2 * lax.axis_index("x") + lax.axis_index("y")
        full = {n: lax.dynamic_update_index_in_dim(g, self.shards[n], me, 0) for n, g in zip(_LATE, landed)}
        return dict(wa=full["w_branch_a"].reshape(D, D), wb=full["w_branch_b"].reshape(D, D),
                    wout=full["w_out"].reshape(D, D), wup=_shards_to_columns(full["w_up"]),
                    wdown=full["w_down"].reshape(FF, D))

    def grads_side(self, grads):
        side_names = list(grads)
        srcs = [_shard_major(n, grads[n]) for n in side_names]

        def copies(sin, sout, send, recv):
            x, y, c, chips = _place()
            peers = [(x, y, 1 - c)] + [(cx, cy, c) for cx, cy in chips] + [(cx, cy, 1 - c) for cx, cy in chips]
            return [pltpu.make_async_remote_copy(
                src_ref=sin[p].at[2 * px + py, pc], dst_ref=sout[p].at[s], send_sem=send.at[N_PEERS * p + s],
                recv_sem=recv.at[N_PEERS * p + s], device_id=(px, py, pc), device_id_type=MESH)
                for p in range(len(sin)) for s, (px, py, pc) in enumerate(peers)]

        side = _Side(srcs, [jax.ShapeDtypeStruct((N_PEERS,) + s.shape[2:], s.dtype) for s in srcs],
                     N_PEERS * len(srcs), copies)
        side.names = side_names
        return side

    def grads_landed(self, side, landed):
        return {n: (own, got) for n, own, got in zip(side.names, side.srcs, landed)}


def _add8(g, got, me, c, name):
    _, _, half, cols = g.shape
    th = _row_tile(half, 2 * cols)

    def body(me_ref, c_ref, g_ref, r_ref, o_ref):
        acc = g_ref[...].astype(F32)
        for s in range(N_PEERS):
            acc = acc + r_ref[s].astype(F32)
        o_ref[...] = acc

    return pl.pallas_call(
        body, name=name,
        grid_spec=pltpu.PrefetchScalarGridSpec(
            num_scalar_prefetch=2, grid=(half // th,),
            in_specs=[pl.BlockSpec((None, None, th, cols), lambda i, me_ref, c_ref: (me_ref[0], c_ref[0], i, 0)),
                      pl.BlockSpec((N_PEERS, th, cols), lambda i, me_ref, c_ref: (0, i, 0))],
            out_specs=pl.BlockSpec((th, cols), lambda i, me_ref, c_ref: (i, 0))),
        out_shape=jax.ShapeDtypeStruct((half, cols), F32),
    )(me, c, g, got)


def _share_halves(halves):
    n = len(halves)

    def body(*refs):
        ins, outs = refs[:n], refs[n:2 * n]
        send_sems, recv_sems = refs[2 * n:]
        x, y, c, _ = _place()
        sibling = (x, y, 1 - c)
        copies = []
        for p in range(n):
            cp = pltpu.make_async_remote_copy(src_ref=ins[p], dst_ref=outs[p], send_sem=send_sems.at[p],
                                              recv_sem=recv_sems.at[p], device_id=sibling, device_id_type=MESH)
            cp.start()
            copies.append(cp)
        for cp in copies:
            cp.wait()

    return pl.pallas_call(
        body, name="reduce_share_halves",
        in_specs=[_ANY] * n, out_specs=[_ANY] * n,
        out_shape=[jax.ShapeDtypeStruct(h.shape, h.dtype) for h in halves],
        scratch_shapes=[pltpu.SemaphoreType.DMA((n,)), pltpu.SemaphoreType.DMA((n,))],
    )(*halves)


def _row_tile(half, cols):
    th = max(SLAB, min(half, (1 << 18) // cols // SLAB * SLAB))
    while half % th:
        th -= SLAB
    return th


N_DEV = 8
SMALL_ROWS = 208


def _allreduce_small(pack):
    def body(x_ref, out_ref, gbuf, send_sems, recv_sems, local_sem):
        x, y, c, chips = _place()
        me, sibling = (x, y, c), (x, y, 1 - c)

        def rows(px, py, pc):
            return gbuf.at[4 * px + 2 * py + pc]

        def copy(k, block, to, src=None):
            return pltpu.make_async_remote_copy(
                src_ref=rows(*block) if src is None else src, dst_ref=rows(*block),
                send_sem=send_sems.at[k], recv_sem=recv_sems.at[k], device_id=to, device_id_type=MESH)

        mine = pltpu.make_async_copy(x_ref, rows(*me), local_sem)
        mine.start()
        first = [copy(0, me, sibling, src=x_ref)]
        first += [copy(1 + j, me, (chip[0], chip[1], c), src=x_ref) for j, chip in enumerate(chips)]
        for cp in first:
            cp.start()
        passed = [copy(4 + j, (chip[0], chip[1], c), sibling) for j, chip in enumerate(chips)]
        for j, chip in enumerate(chips):
            copy(1 + j, (chip[0], chip[1], c), me).wait_recv()
            passed[j].start()
        copy(0, sibling, me).wait_recv()
        for j, chip in enumerate(chips):
            copy(4 + j, (chip[0], chip[1], 1 - c), me).wait_recv()
        for cp in first + passed:
            cp.wait_send()
        mine.wait()
        acc = gbuf[0]
        for d in range(1, N_DEV):
            acc = acc + gbuf[d]
        out_ref[...] = acc

    return pl.pallas_call(
        body, name="allreduce_small",
        in_specs=[pl.BlockSpec(memory_space=pltpu.VMEM)],
        out_specs=pl.BlockSpec(memory_space=pltpu.VMEM),
        out_shape=jax.ShapeDtypeStruct((SMALL_ROWS, D), F32),
        scratch_shapes=[pltpu.VMEM((N_DEV, SMALL_ROWS, D), F32), pltpu.SemaphoreType.DMA((7,)),
                        pltpu.SemaphoreType.DMA((7,)), pltpu.SemaphoreType.DMA],
    )(pack)


def _adamw(w, g, m, v, name):
    rows, cols = w.shape

    def body(w_ref, g_ref, m_ref, v_ref, d_ref, mo_ref, vo_ref):
        gv = g_ref[...]
        mn = ADAM_B1 * m_ref[...] + (1.0 - ADAM_B1) * gv
        vn = ADAM_B2 * v_ref[...] + (1.0 - ADAM_B2) * (gv * gv)
        m_hat = mn / (1.0 - ADAM_B1 ** ADAM_STEP)
        v_hat = vn / (1.0 - ADAM_B2 ** ADAM_STEP)
        d_ref[...] = -ADAM_LR * (m_hat / (jnp.sqrt(v_hat) + ADAM_EPS) + ADAM_WD * w_ref[...])
        mo_ref[...] = mn
        vo_ref[...] = vn

    if rows % SLAB:
        spec, steps = pl.BlockSpec((rows, DH), lambda i: (0, i)), cols // DH
    else:
        th = _row_tile(rows, cols)
        spec, steps = pl.BlockSpec((th, cols), lambda i: (i, 0)), rows // th
    return pl.pallas_call(
        body, name=name, grid=(steps,),
        in_specs=[spec] * 4, out_specs=[spec] * 3,
        out_shape=[jax.ShapeDtypeStruct((rows, cols), F32)] * 3,
        compiler_params=pltpu.CompilerParams(dimension_semantics=("parallel",)),
    )(w, g, m, v)


_SMALL = ["norm_mix_g", "norm_mlp_g", "norm_final_g", "conv_b", "lru_ba", "lru_bx", "lru_lambda"]
_ROW_FB, _ROW_CW, _ROW_WA, _ROW_WX, _ROW_LOSS = 56, 64, 72, 136, 200


def _pack_small(vals, col0):
    def slab(a):
        return jnp.pad(a, ((0, -a.shape[0] % SLAB), (0, D - a.shape[1])))

    rows = [slab(vals[n].reshape(1, D)) for n in _SMALL]
    rows.append(slab(vals["forget_b"].reshape(1, NH)))
    if vals["conv_w"].shape[1] == D:
        rows.append(slab(vals["conv_w"]))
    else:
        rows.append(slab(lax.dynamic_update_slice(jnp.zeros((CONV, D), F32), vals["conv_w"], (0, col0))))
    rows.append(vals["lru_wa"].reshape(LRU_BLOCKS * LRU_BW * LRU_BW // D, D))
    rows.append(vals["lru_wx"].reshape(LRU_BLOCKS * LRU_BW * LRU_BW // D, D))
    rows.append(slab(vals["loss"]) if "loss" in vals else jnp.zeros((SLAB, D), F32))
    return jnp.concatenate(rows, axis=0)


def _unpack_small(pack, col0):
    out = {n: pack[SLAB * i] for i, n in enumerate(_SMALL)}
    out["forget_b"] = pack[_ROW_FB, :NH]
    out["conv_w"] = lax.dynamic_slice(pack[_ROW_CW:_ROW_CW + CONV], (0, col0), (CONV, D // NCHIP))
    out["lru_wa"] = pack[_ROW_WA:_ROW_WX].reshape(LRU_BLOCKS, LRU_BW, LRU_BW)
    out["lru_wx"] = pack[_ROW_WX:_ROW_LOSS].reshape(LRU_BLOCKS, LRU_BW, LRU_BW)
    return out


_WEIGHTS = ["norm_mix_g", "w_in", "conv_w", "conv_b", "lru_wa", "lru_ba", "lru_wx", "lru_bx", "lru_lambda",
            "forget_b", "w_branch_a", "w_branch_b", "w_out", "norm_mlp_g", "w_up", "w_down", "norm_final_g"]
_BIG = ["w_in", "w_branch_a", "w_branch_b", "w_out", "w_up", "w_down"]


def _halves(a):
    return a.reshape(2, a.shape[0] // 2, a.shape[1])


def _columns_to_shards(a):
    rows, cols = a.shape[0], a.shape[1] // NCHIP
    return jnp.transpose(a.reshape(rows, NCHIP, cols), (1, 0, 2))


def _shards_to_columns(a):
    n, rows, cols = a.shape
    return jnp.transpose(a, (1, 0, 2)).reshape(rows, n * cols)


def kernel(x, norm_mix_g, w_in, conv_w, conv_b, lru_wa, lru_ba, lru_wx, lru_bx, lru_lambda, forget_b, w_branch_a, w_branch_b, w_out, norm_mlp_g, w_up, w_down, norm_final_g, loss_target, m_norm_mix_g, m_w_in, m_conv_w, m_conv_b, m_lru_wa, m_lru_ba, m_lru_wx, m_lru_bx, m_lru_lambda, m_forget_b, m_w_branch_a, m_w_branch_b, m_w_out, m_norm_mlp_g, m_w_up, m_w_down, m_norm_final_g, v_norm_mix_g, v_w_in, v_conv_w, v_conv_b, v_lru_wa, v_lru_ba, v_lru_wx, v_lru_bx, v_lru_lambda, v_forget_b, v_w_branch_a, v_w_branch_b, v_w_out, v_norm_mlp_g, v_w_up, v_w_down, v_norm_final_g):
    args = dict(locals())
    wts = {n: args[n] for n in _WEIGHTS}
    mom = {n: args["m_" + n] for n in _WEIGHTS}
    var = {n: args["v_" + n] for n in _WEIGHTS}
    T = x.shape[1]
    xi, yi, ci = lax.axis_index("x"), lax.axis_index("y"), lax.axis_index("c")
    me = 2 * xi + yi
    c1 = jnp.reshape(ci, (1,)).astype(jnp.int32)
    me1 = jnp.reshape(me, (1,)).astype(jnp.int32)
    col0 = me * (D // NCHIP)

    cw_pad = jnp.pad(conv_w, ((0, 4 * SLAB - CONV), (0, 0)))
    g_in, g_cw = _allgather_shards([_halves(w_in.astype(BF)), _halves(cw_pad)])
    cin = DIN // NCHIP
    win = _shards_to_columns(g_in.reshape(NCHIP, D, cin))
    w = dict(
        win=jnp.pad(win, ((0, 0), (0, DINP - DIN))),
        cw=_shards_to_columns(g_cw.reshape(NCHIP, 4 * SLAB, D // NCHIP)[:, :CONV]),
        vec=jnp.concatenate([conv_b[None], lru_ba[None], lru_bx[None], lru_lambda[None],
                             jnp.zeros((SLAB - 4, D), F32)], axis=0),
        fb=jnp.pad(forget_b[None], ((0, 0), (0, DH - NH))),
        wabd=_block_diag(lru_wa).astype(BF), wxbd=_block_diag(lru_wx).astype(BF),
        g_mix=norm_mix_g[None], g_mlp=norm_mlp_g[None], g_fin=norm_final_g[None])

    r = _local_step(x[0], loss_target[0], w, T, dist=_Exchanges({n: wts[n].astype(BF) for n in _LATE}))

    halves = [_add8(*r["big"][n], me1, c1, "add8_" + n) for n in _BIG]
    theirs = _share_halves(halves)
    low = ci == 0
    gsum = {n: jnp.concatenate([jnp.where(low, h, t), jnp.where(low, t, h)], axis=0)
            for n, h, t in zip(_BIG, halves, theirs)}
    lacc = r["lacc"]
    small = dict(norm_mix_g=r["dg_mix"], norm_mlp_g=r["dg_mlp"], norm_final_g=r["dg_fin"], conv_b=lacc[3],
                 lru_ba=lacc[0], lru_bx=lacc[1], lru_lambda=lacc[2], forget_b=r["dfb"][0, :NH],
                 conv_w=lacc[4:4 + CONV], lru_wa=_block_diag_extract(r["dwabd"]),
                 lru_wx=_block_diag_extract(r["dwxbd"]), loss=r["loss_vec"])
    gpack = _allreduce_small(_pack_small(small, col0))
    loss = jnp.sum(gpack[_ROW_LOSS])

    grads, delta, new_m, new_v = {}, {}, {}, {}
    for n in _BIG:
        if n == "w_in":
            gt = gsum[n].T
            grads[n] = gt.T
            delta[n], new_m[n], new_v[n] = [a.T for a in _adamw(wts[n].T, gt, mom[n].T, var[n].T, "adamw_" + n)]
        else:
            grads[n] = gsum[n]
            delta[n], new_m[n], new_v[n] = _adamw(wts[n], gsum[n], mom[n], var[n], "adamw_" + n)
    dp, mp, vp = _adamw(_pack_small(wts, col0), gpack, _pack_small(mom, col0), _pack_small(var, col0), "adamw_small")
    for dst, pack in ((grads, gpack), (delta, dp), (new_m, mp), (new_v, vp)):
        dst.update(_unpack_small(pack, col0))
    return (loss, r["dx"][None], *[grads[n] for n in _WEIGHTS], *[delta[n] for n in _WEIGHTS],
            *[new_m[n] for n in _WEIGHTS], *[new_v[n] for n in _WEIGHTS])
```
